```python
import math
import jax, jax.numpy as jnp
from jax import lax
import numpy as np

D_MODEL = 1024
BATCH = 8
SEQ = 8192
DEPTH = 1

D_A = D_MODEL
D_B = D_MODEL
CONV_A_WIDTH = 3
CONV_B_WIDTH = 31
D_FF = 4 * D_MODEL
N_GROUPS = 16
LN_EPS = 1e-5
ALPHA = (2.0 * DEPTH) ** 0.25
BETA = (8.0 * DEPTH) ** -0.25
W_IN_COLS = 3 * D_A + 2 * D_B + 2 * D_MODEL

kernel_name = "hybrid_shortconv_conformer_gated_deepnorm"


def layernorm(x, gamma, beta):
    xf = x.astype(jnp.float32)
    mu = jnp.mean(xf, axis=-1, keepdims=True)
    var = jnp.mean(jnp.square(xf - mu), axis=-1, keepdims=True)
    y = (xf - mu) * lax.rsqrt(var + LN_EPS)
    y = y * gamma.astype(jnp.float32) + beta.astype(jnp.float32)
    return y.astype(x.dtype)


def causal_depthwise_conv(x, w):
    k = w.shape[0]
    c = x.shape[-1]
    return lax.conv_general_dilated(
        x, w[:, None, :].astype(x.dtype),
        window_strides=(1,),
        padding=[(k - 1, 0)],
        dimension_numbers=("NWC", "WIO", "NWC"),
        feature_group_count=c,
    )


def _fwd_setup_inputs(seed: int = 0) -> dict:
    key = jax.random.key(seed)
    ks = jax.random.split(key, 20)
    f32 = jnp.float32
    nrm = lambda k, shape, scale: jax.random.normal(k, shape, f32) * scale
    return {
        "x": jax.random.normal(ks[0], (BATCH, SEQ, D_MODEL), f32),
        "w_in": nrm(ks[1], (D_MODEL, W_IN_COLS), D_MODEL ** -0.5),
        "conv_a_w": nrm(ks[2], (CONV_A_WIDTH, D_A), CONV_A_WIDTH ** -0.5),
        "w_out_a": nrm(ks[3], (D_A, D_MODEL), BETA * D_A ** -0.5),
        "conv_b_w": nrm(ks[4], (CONV_B_WIDTH, D_B), CONV_B_WIDTH ** -0.5),
        "conv_b_bias": nrm(ks[5], (D_B,), 0.02),
        "ln_b_gamma": 1.0 + nrm(ks[6], (D_B,), 0.02),
        "ln_b_beta": nrm(ks[7], (D_B,), 0.02),
        "w_out_b": nrm(ks[8], (D_B, D_MODEL), BETA * D_B ** -0.5),
        "w_o": nrm(ks[9], (D_MODEL, D_MODEL), BETA * D_MODEL ** -0.5),
        "ln1_gamma": 1.0 + nrm(ks[10], (D_MODEL,), 0.02),
        "ln1_beta": nrm(ks[11], (D_MODEL,), 0.02),
        "w_up": nrm(ks[12], (D_MODEL, D_FF), D_MODEL ** -0.5),
        "w_down": nrm(ks[13], (D_FF, D_MODEL), BETA * D_FF ** -0.5),
        "ln2_gamma": 1.0 + nrm(ks[14], (D_MODEL,), 0.02),
        "ln2_beta": nrm(ks[15], (D_MODEL,), 0.02),
    }


def token_mixer(x, w_in, conv_a_w, w_out_a, conv_b_w, conv_b_bias,
                ln_b_gamma, ln_b_beta, w_out_b, w_o):
    p = jnp.einsum("bsd,dc->bsc", x, w_in)
    splits = np.cumsum([D_A, D_A, D_A, D_B, D_B, D_MODEL])
    b_a, c_a, v_a, val_b, gate_b, g_a, g_b = jnp.split(p, splits, axis=-1)

    y_a = b_a * causal_depthwise_conv(c_a * v_a, conv_a_w)
    y_a = jnp.einsum("bsc,cd->bsd", y_a, w_out_a)

    u = val_b * jax.nn.sigmoid(gate_b)
    u = causal_depthwise_conv(u, conv_b_w) + conv_b_bias.astype(u.dtype)
    u = jax.nn.silu(layernorm(u, ln_b_gamma, ln_b_beta))
    y_b = jnp.einsum("bsc,cd->bsd", u, w_out_b)

    merged = jax.nn.sigmoid(g_a) * y_a + jax.nn.sigmoid(g_b) * y_b
    return jnp.einsum("bsd,de->bse", merged, w_o)


def channel_mixer(x, w_up, w_down):
    h = jnp.square(jax.nn.relu(jnp.einsum("bsd,df->bsf", x, w_up)))
    return jnp.einsum("bsf,fd->bsd", h, w_down)


def _fwd_reference(x, w_in, conv_a_w, w_out_a, conv_b_w, conv_b_bias, ln_b_gamma,
              ln_b_beta, w_out_b, w_o, ln1_gamma, ln1_beta, w_up, w_down,
              ln2_gamma, ln2_beta):
    alpha = jnp.asarray(ALPHA, dtype=x.dtype)
    for _ in range(DEPTH):
        mix = token_mixer(x, w_in, conv_a_w, w_out_a, conv_b_w, conv_b_bias,
                          ln_b_gamma, ln_b_beta, w_out_b, w_o)
        x = layernorm(alpha * x + mix, ln1_gamma, ln1_beta)
        ff = channel_mixer(x, w_up, w_down)
        x = layernorm(alpha * x + ff, ln2_gamma, ln2_beta)
    return x


import jax as _jax
import jax.numpy as _jnp

TWIN_FORMAT = 'train_step'
FWD_PARAMS = ['x', 'w_in', 'conv_a_w', 'w_out_a', 'conv_b_w', 'conv_b_bias', 'ln_b_gamma', 'ln_b_beta', 'w_out_b', 'w_o', 'ln1_gamma', 'ln1_beta', 'w_up', 'w_down', 'ln2_gamma', 'ln2_beta']
TWIN_WEIGHTS = ['w_in', 'conv_a_w', 'w_out_a', 'conv_b_w', 'conv_b_bias', 'ln_b_gamma', 'ln_b_beta', 'w_out_b', 'w_o', 'ln1_gamma', 'ln1_beta', 'w_up', 'w_down', 'ln2_gamma', 'ln2_beta']
TWIN_DIFF_INPUT = 'x'
TWIN_INPUTS = ['x', 'w_in', 'conv_a_w', 'w_out_a', 'conv_b_w', 'conv_b_bias', 'ln_b_gamma', 'ln_b_beta', 'w_out_b', 'w_o', 'ln1_gamma', 'ln1_beta', 'w_up', 'w_down', 'ln2_gamma', 'ln2_beta', 'loss_target', 'm_w_in', 'm_conv_a_w', 'm_w_out_a', 'm_conv_b_w', 'm_conv_b_bias', 'm_ln_b_gamma', 'm_ln_b_beta', 'm_w_out_b', 'm_w_o', 'm_ln1_gamma', 'm_ln1_beta', 'm_w_up', 'm_w_down', 'm_ln2_gamma', 'm_ln2_beta', 'v_w_in', 'v_conv_a_w', 'v_w_out_a', 'v_conv_b_w', 'v_conv_b_bias', 'v_ln_b_gamma', 'v_ln_b_beta', 'v_w_out_b', 'v_w_o', 'v_ln1_gamma', 'v_ln1_beta', 'v_w_up', 'v_w_down', 'v_ln2_gamma', 'v_ln2_beta']
TWIN_OUTPUTS = ['loss', 'grad_x', 'grad_w_in', 'grad_conv_a_w', 'grad_w_out_a', 'grad_conv_b_w', 'grad_conv_b_bias', 'grad_ln_b_gamma', 'grad_ln_b_beta', 'grad_w_out_b', 'grad_w_o', 'grad_ln1_gamma', 'grad_ln1_beta', 'grad_w_up', 'grad_w_down', 'grad_ln2_gamma', 'grad_ln2_beta', 'delta_w_in', 'delta_conv_a_w', 'delta_w_out_a', 'delta_conv_b_w', 'delta_conv_b_bias', 'delta_ln_b_gamma', 'delta_ln_b_beta', 'delta_w_out_b', 'delta_w_o', 'delta_ln1_gamma', 'delta_ln1_beta', 'delta_w_up', 'delta_w_down', 'delta_ln2_gamma', 'delta_ln2_beta', 'new_m_w_in', 'new_m_conv_a_w', 'new_m_w_out_a', 'new_m_conv_b_w', 'new_m_conv_b_bias', 'new_m_ln_b_gamma', 'new_m_ln_b_beta', 'new_m_w_out_b', 'new_m_w_o', 'new_m_ln1_gamma', 'new_m_ln1_beta', 'new_m_w_up', 'new_m_w_down', 'new_m_ln2_gamma', 'new_m_ln2_beta', 'new_v_w_in', 'new_v_conv_a_w', 'new_v_w_out_a', 'new_v_conv_b_w', 'new_v_conv_b_bias', 'new_v_ln_b_gamma', 'new_v_ln_b_beta', 'new_v_w_out_b', 'new_v_w_o', 'new_v_ln1_gamma', 'new_v_ln1_beta', 'new_v_w_up', 'new_v_w_down', 'new_v_ln2_gamma', 'new_v_ln2_beta']
TWIN_LEAF_KINDS = {'loss': 'loss', 'grad_x': 'grad_x', 'grad_w_in': 'grad_w', 'grad_conv_a_w': 'grad_w', 'grad_w_out_a': 'grad_w', 'grad_conv_b_w': 'grad_w', 'grad_conv_b_bias': 'grad_w', 'grad_ln_b_gamma': 'grad_w', 'grad_ln_b_beta': 'grad_w', 'grad_w_out_b': 'grad_w', 'grad_w_o': 'grad_w', 'grad_ln1_gamma': 'grad_w', 'grad_ln1_beta': 'grad_w', 'grad_w_up': 'grad_w', 'grad_w_down': 'grad_w', 'grad_ln2_gamma': 'grad_w', 'grad_ln2_beta': 'grad_w', 'delta_w_in': 'delta_w', 'delta_conv_a_w': 'delta_w', 'delta_w_out_a': 'delta_w', 'delta_conv_b_w': 'delta_w', 'delta_conv_b_bias': 'delta_w', 'delta_ln_b_gamma': 'delta_w', 'delta_ln_b_beta': 'delta_w', 'delta_w_out_b': 'delta_w', 'delta_w_o': 'delta_w', 'delta_ln1_gamma': 'delta_w', 'delta_ln1_beta': 'delta_w', 'delta_w_up': 'delta_w', 'delta_w_down': 'delta_w', 'delta_ln2_gamma': 'delta_w', 'delta_ln2_beta': 'delta_w', 'new_m_w_in': 'new_m', 'new_m_conv_a_w': 'new_m', 'new_m_w_out_a': 'new_m', 'new_m_conv_b_w': 'new_m', 'new_m_conv_b_bias': 'new_m', 'new_m_ln_b_gamma': 'new_m', 'new_m_ln_b_beta': 'new_m', 'new_m_w_out_b': 'new_m', 'new_m_w_o': 'new_m', 'new_m_ln1_gamma': 'new_m', 'new_m_ln1_beta': 'new_m', 'new_m_w_up': 'new_m', 'new_m_w_down': 'new_m', 'new_m_ln2_gamma': 'new_m', 'new_m_ln2_beta': 'new_m', 'new_v_w_in': 'new_v', 'new_v_conv_a_w': 'new_v', 'new_v_w_out_a': 'new_v', 'new_v_conv_b_w': 'new_v', 'new_v_conv_b_bias': 'new_v', 'new_v_ln_b_gamma': 'new_v', 'new_v_ln_b_beta': 'new_v', 'new_v_w_out_b': 'new_v', 'new_v_w_o': 'new_v', 'new_v_ln1_gamma': 'new_v', 'new_v_ln1_beta': 'new_v', 'new_v_w_up': 'new_v', 'new_v_w_down': 'new_v', 'new_v_ln2_gamma': 'new_v', 'new_v_ln2_beta': 'new_v'}


def _forward(args):
    return _fwd_reference(*[args[k] for k in FWD_PARAMS])


def _output_shape():
    def fwd():
        inp = _fwd_setup_inputs(0)
        return _fwd_reference(*[inp[k] for k in FWD_PARAMS])
    out = _jax.eval_shape(fwd)
    return out.shape, out.dtype

N_MICROBATCH = 1
ADAM_LR = 0.001
ADAM_B1 = 0.9
ADAM_B2 = 0.999
ADAM_EPS = 1e-08
ADAM_WD = 0.01
ADAM_STEP = 10
PER_EXAMPLE_BATCH_AXIS = {'x': 0, 'loss_target': 0}
SHARED_INPUTS = []
_WEIGHT_DTYPES = {'w_in': _jnp.float32, 'conv_a_w': _jnp.float32, 'w_out_a': _jnp.float32, 'conv_b_w': _jnp.float32, 'conv_b_bias': _jnp.float32, 'ln_b_gamma': _jnp.float32, 'ln_b_beta': _jnp.float32, 'w_out_b': _jnp.float32, 'w_o': _jnp.float32, 'ln1_gamma': _jnp.float32, 'ln1_beta': _jnp.float32, 'w_up': _jnp.float32, 'w_down': _jnp.float32, 'ln2_gamma': _jnp.float32, 'ln2_beta': _jnp.float32}
MOMENT_SCALE = {'w_in': 3.013068e-02, 'conv_a_w': 4.138054e-02, 'w_out_a': 7.025541e-02, 'conv_b_w': 2.627945e-02, 'conv_b_bias': 1.154437e-01, 'ln_b_gamma': 4.788731e-02, 'ln_b_beta': 6.914236e-02, 'w_out_b': 5.601887e-02, 'w_o': 8.808570e-02, 'ln1_gamma': 1.557381e+00, 'ln1_beta': 1.011396e+00, 'w_up': 7.632905e-02, 'w_down': 3.789077e-01, 'ln2_gamma': 6.416245e+01, 'ln2_beta': 1.334841e+01}


def _to_microbatches(a, axis):
    t = _jnp.moveaxis(a, axis, 0)
    t = t.reshape((N_MICROBATCH, t.shape[0] // N_MICROBATCH) + t.shape[1:])
    return _jnp.moveaxis(t, 1, axis + 1)


def setup_inputs(seed: int = 0) -> dict:
    inp = _fwd_setup_inputs(seed)
    key = _jax.random.fold_in(_jax.random.key(seed), 7919)
    shape, _ = _output_shape()
    out = dict(inp)
    out["loss_target"] = _jax.random.normal(_jax.random.fold_in(key, 0), shape, _jnp.float32)
    for i, name in enumerate(TWIN_WEIGHTS):
        w = inp[name].astype(_jnp.float32)
        if MOMENT_SCALE is None:
            s = _jnp.sqrt(_jnp.mean(_jnp.square(w)) + 1e-30)
        else:
            s = MOMENT_SCALE[name]
        km, kv = _jax.random.split(_jax.random.fold_in(key, i + 1))
        out[name] = w
        out["m_" + name] = s * _jax.random.normal(km, w.shape, _jnp.float32)
        out["v_" + name] = (s * s) * _jax.random.uniform(kv, w.shape, _jnp.float32, 0.5, 1.5)
    if N_MICROBATCH > 1:
        for name, axis in PER_EXAMPLE_BATCH_AXIS.items():
            out[name] = _to_microbatches(out[name], axis)
    return {'x': out['x'], 'w_in': out['w_in'], 'conv_a_w': out['conv_a_w'], 'w_out_a': out['w_out_a'], 'conv_b_w': out['conv_b_w'], 'conv_b_bias': out['conv_b_bias'], 'ln_b_gamma': out['ln_b_gamma'], 'ln_b_beta': out['ln_b_beta'], 'w_out_b': out['w_out_b'], 'w_o': out['w_o'], 'ln1_gamma': out['ln1_gamma'], 'ln1_beta': out['ln1_beta'], 'w_up': out['w_up'], 'w_down': out['w_down'], 'ln2_gamma': out['ln2_gamma'], 'ln2_beta': out['ln2_beta'], 'loss_target': out['loss_target'], 'm_w_in': out['m_w_in'], 'm_conv_a_w': out['m_conv_a_w'], 'm_w_out_a': out['m_w_out_a'], 'm_conv_b_w': out['m_conv_b_w'], 'm_conv_b_bias': out['m_conv_b_bias'], 'm_ln_b_gamma': out['m_ln_b_gamma'], 'm_ln_b_beta': out['m_ln_b_beta'], 'm_w_out_b': out['m_w_out_b'], 'm_w_o': out['m_w_o'], 'm_ln1_gamma': out['m_ln1_gamma'], 'm_ln1_beta': out['m_ln1_beta'], 'm_w_up': out['m_w_up'], 'm_w_down': out['m_w_down'], 'm_ln2_gamma': out['m_ln2_gamma'], 'm_ln2_beta': out['m_ln2_beta'], 'v_w_in': out['v_w_in'], 'v_conv_a_w': out['v_conv_a_w'], 'v_w_out_a': out['v_w_out_a'], 'v_conv_b_w': out['v_conv_b_w'], 'v_conv_b_bias': out['v_conv_b_bias'], 'v_ln_b_gamma': out['v_ln_b_gamma'], 'v_ln_b_beta': out['v_ln_b_beta'], 'v_w_out_b': out['v_w_out_b'], 'v_w_o': out['v_w_o'], 'v_ln1_gamma': out['v_ln1_gamma'], 'v_ln1_beta': out['v_ln1_beta'], 'v_w_up': out['v_w_up'], 'v_w_down': out['v_w_down'], 'v_ln2_gamma': out['v_ln2_gamma'], 'v_ln2_beta': out['v_ln2_beta']}


def _loss(weights, diff, rest, loss_target):
    with _jax.named_scope("forward"):
        args = {**rest, TWIN_DIFF_INPUT: diff, **{k: w.astype(_WEIGHT_DTYPES[k]) for k, w in weights.items()}}
        y = _forward(args)
    with _jax.named_scope("loss_head"):
        err = _jnp.square(y.astype(_jnp.float32) - loss_target)
        return 0.5 * _jnp.sum(_jnp.mean(err, axis=-1)) if err.ndim else 0.5 * err


def _adamw(w, g, m, v):
    m = ADAM_B1 * m + (1.0 - ADAM_B1) * g
    v = ADAM_B2 * v + (1.0 - ADAM_B2) * _jnp.square(g)
    m_hat = m / (1.0 - ADAM_B1 ** ADAM_STEP)
    v_hat = v / (1.0 - ADAM_B2 ** ADAM_STEP)
    delta = -ADAM_LR * (m_hat / (_jnp.sqrt(v_hat) + ADAM_EPS) + ADAM_WD * w)
    return delta, m, v


def reference(x, w_in, conv_a_w, w_out_a, conv_b_w, conv_b_bias, ln_b_gamma, ln_b_beta, w_out_b, w_o, ln1_gamma, ln1_beta, w_up, w_down, ln2_gamma, ln2_beta, loss_target, m_w_in, m_conv_a_w, m_w_out_a, m_conv_b_w, m_conv_b_bias, m_ln_b_gamma, m_ln_b_beta, m_w_out_b, m_w_o, m_ln1_gamma, m_ln1_beta, m_w_up, m_w_down, m_ln2_gamma, m_ln2_beta, v_w_in, v_conv_a_w, v_w_out_a, v_conv_b_w, v_conv_b_bias, v_ln_b_gamma, v_ln_b_beta, v_w_out_b, v_w_o, v_ln1_gamma, v_ln1_beta, v_w_up, v_w_down, v_ln2_gamma, v_ln2_beta):
    given = dict(x=x, w_in=w_in, conv_a_w=conv_a_w, w_out_a=w_out_a, conv_b_w=conv_b_w, conv_b_bias=conv_b_bias, ln_b_gamma=ln_b_gamma, ln_b_beta=ln_b_beta, w_out_b=w_out_b, w_o=w_o, ln1_gamma=ln1_gamma, ln1_beta=ln1_beta, w_up=w_up, w_down=w_down, ln2_gamma=ln2_gamma, ln2_beta=ln2_beta, loss_target=loss_target, m_w_in=m_w_in, m_conv_a_w=m_conv_a_w, m_w_out_a=m_w_out_a, m_conv_b_w=m_conv_b_w, m_conv_b_bias=m_conv_b_bias, m_ln_b_gamma=m_ln_b_gamma, m_ln_b_beta=m_ln_b_beta, m_w_out_b=m_w_out_b, m_w_o=m_w_o, m_ln1_gamma=m_ln1_gamma, m_ln1_beta=m_ln1_beta, m_w_up=m_w_up, m_w_down=m_w_down, m_ln2_gamma=m_ln2_gamma, m_ln2_beta=m_ln2_beta, v_w_in=v_w_in, v_conv_a_w=v_conv_a_w, v_w_out_a=v_w_out_a, v_conv_b_w=v_conv_b_w, v_conv_b_bias=v_conv_b_bias, v_ln_b_gamma=v_ln_b_gamma, v_ln_b_beta=v_ln_b_beta, v_w_out_b=v_w_out_b, v_w_o=v_w_o, v_ln1_gamma=v_ln1_gamma, v_ln1_beta=v_ln1_beta, v_w_up=v_w_up, v_w_down=v_w_down, v_ln2_gamma=v_ln2_gamma, v_ln2_beta=v_ln2_beta)
    weights = {n: given[n] for n in TWIN_WEIGHTS}
    shared = {n: given[n] for n in SHARED_INPUTS}
    per_example = {n: given[n] for n in ['x']}
    grad_fn = _jax.value_and_grad(_loss, argnums=(0, 1))

    def one_microbatch(ex, loss_target):
        ex = dict(ex)
        diff = ex.pop(TWIN_DIFF_INPUT)
        return grad_fn(weights, diff, {**shared, **ex}, loss_target)

    if N_MICROBATCH == 1:
        loss, (grad_w, grad_x) = one_microbatch(per_example, given["loss_target"])
    else:
        def body(carry, xs):
            loss_sum, grad_sum = carry
            l_k, (gw_k, gx_k) = one_microbatch(xs[0], xs[1])
            with _jax.named_scope("update"):
                return (loss_sum + l_k, _jax.tree.map(_jnp.add, grad_sum, gw_k)), gx_k

        init = (_jnp.zeros((), _jnp.float32), _jax.tree.map(_jnp.zeros_like, weights))
        (loss, grad_w), grad_x = _jax.lax.scan(body, init, (per_example, given["loss_target"]))
    with _jax.named_scope("update"):
        delta_w, new_m, new_v = {}, {}, {}
        for n in TWIN_WEIGHTS:
            delta_w[n], new_m[n], new_v[n] = _adamw(weights[n], grad_w[n], given["m_" + n], given["v_" + n])
    return (loss, grad_x, *[grad_w[n] for n in TWIN_WEIGHTS], *[delta_w[n] for n in TWIN_WEIGHTS],
            *[new_m[n] for n in TWIN_WEIGHTS], *[new_v[n] for n in TWIN_WEIGHTS])
```

```python
import jax
import jax.numpy as jnp
from jax import lax
from jax.experimental import pallas as pl
from jax.experimental.pallas import tpu as pltpu

F32 = jnp.float32
BF = jnp.bfloat16
SDS = jax.ShapeDtypeStruct
MESH = pl.DeviceIdType.MESH

ALPHA = 2.0 ** 0.25
LN_EPS = 1e-5
K_A = 3
K_B = 31
HALO = 32
CONV_ROWS = 32
CONV_LANES = 256
ADAM_LR = 0.001
ADAM_B1 = 0.9
ADAM_B2 = 0.999
ADAM_EPS = 1e-08
ADAM_WD = 0.01
ADAM_STEP = 10
P_DT = F32
CHIP_RELS = ((1, 0), (0, 1), (1, 1))
DEV_RELS = tuple((fx, fy, fc) for fx in (0, 1) for fy in (0, 1) for fc in (0, 1))[1:]


def _cp(sem=None, vmem_mb=56):
    return pltpu.CompilerParams(dimension_semantics=sem, vmem_limit_bytes=vmem_mb << 20)


def _const(shape):
    return pl.BlockSpec(shape, lambda *_: (0,) * len(shape), pipeline_mode=pl.Buffered(1))


def _sig(v):
    return jax.nn.sigmoid(v)


def _ln_fwd(r):
    mu = jnp.mean(r, axis=-1, keepdims=True)
    xc = r - mu
    var = jnp.mean(xc * xc, axis=-1, keepdims=True)
    rstd = lax.rsqrt(var + LN_EPS)
    return xc * rstd, rstd


def _ln_bwd(dn, n, rstd):
    m1 = jnp.mean(dn, axis=-1, keepdims=True)
    m2 = jnp.mean(dn * n, axis=-1, keepdims=True)
    return rstd * (dn - m1 - n * m2)


def _dot(a, b):
    return jnp.dot(a, b, preferred_element_type=F32)


def _dot_nt(a, b):
    return lax.dot_general(a, b, (((1,), (1,)), ((), ())), preferred_element_type=F32)


def _dot_tn(a, b):
    return lax.dot_general(a, b, (((0,), (0,)), ((), ())), preferred_element_type=F32)


def _tile(n, pref):
    if n <= pref:
        return n
    return max(t for t in range(128, pref + 1, 128) if n % t == 0)


def _rowsum8(v):
    acc = v[0:8]
    for g in range(1, v.shape[0] // 8):
        acc = acc + v[8 * g:8 * g + 8]
    return acc


def _in_proj(x, wi):
    s, d = x.shape
    n = wi.shape[1]
    tm, tn = min(s, 1024), d

    def body(x_ref, w_ref, p_ref, xb_ref):
        @pl.when(pl.program_id(1) == 0)
        def _():
            xb_ref[...] = x_ref[...].astype(BF)

        p_ref[...] = _dot(xb_ref[...], w_ref[...]).astype(p_ref.dtype)

    return pl.pallas_call(
        body, name="in_proj", grid=(s // tm, n // tn),
        in_specs=[pl.BlockSpec((tm, d), lambda i, j: (i, 0)), pl.BlockSpec((d, tn), lambda i, j: (0, j))],
        out_specs=[pl.BlockSpec((tm, tn), lambda i, j: (i, j)), pl.BlockSpec((tm, d), lambda i, j: (i, 0))],
        out_shape=[SDS((s, n), P_DT), SDS((s, d), BF)],
        compiler_params=_cp(("parallel", "arbitrary")),
    )(x, wi)


def _col_spec(tm, d, k):
    return pl.BlockSpec((tm, d), lambda i, k=k: (i, k))


def _prev_halo_spec(tm, d, k):
    r = tm // HALO
    return pl.BlockSpec((HALO, d), lambda i, k=k: (jnp.maximum(i * r - 1, 0), k))


def _next_halo_spec(tm, d, k, s):
    r = tm // HALO
    last = s // HALO - 1
    return pl.BlockSpec((HALO, d), lambda i, k=k: (jnp.minimum((i + 1) * r, last), k))


def _conv_fwd(p, cwa, cwb, bias_b, lbg, lbb, d):
    s = p.shape[0]
    tm = min(s, 256)
    nt = s // tm

    def body(ba_ref, ca_ref, va_ref, vb_ref, gb_ref, hca_ref, hva_ref, hvb_ref, hgb_ref,
             cwa_ref, cwb_ref, bias_ref, lbg_ref, lbb_ref,
             conva_ref, yapre_ref, nb_ref, rstdb_ref, u3_ref,
             zbuf, ubuf, wbuf, u1buf):
        i = pl.program_id(0)
        keep = (i > 0).astype(F32)
        zbuf[pl.ds(0, HALO), :] = hca_ref[...].astype(F32) * hva_ref[...].astype(F32) * keep
        ubuf[pl.ds(0, HALO), :] = hvb_ref[...].astype(F32) * _sig(hgb_ref[...].astype(F32)) * keep
        zbuf[pl.ds(HALO, tm), :] = ca_ref[...].astype(F32) * va_ref[...].astype(F32)
        ubuf[pl.ds(HALO, tm), :] = vb_ref[...].astype(F32) * _sig(gb_ref[...].astype(F32))

        conva = jnp.zeros((tm, d), F32)
        for k in range(K_A):
            conva = conva + cwa_ref[pl.ds(k, 1), :] * zbuf[pl.ds(HALO - (K_A - 1) + k, tm), :]
        conva_ref[...] = conva
        yapre_ref[...] = (ba_ref[...].astype(F32) * conva).astype(BF)

        def chunk(j, carry):
            r0 = pl.multiple_of(j * CONV_ROWS, CONV_ROWS)
            wbuf[...] = ubuf[pl.ds(r0, CONV_ROWS + HALO), :]
            for lc in range(d // CONV_LANES):
                ls = pl.ds(lc * CONV_LANES, CONV_LANES)
                acc = jnp.zeros((CONV_ROWS, CONV_LANES), F32)
                for k in range(K_B):
                    acc = acc + cwb_ref[pl.ds(k, 1), ls] * wbuf[pl.ds(HALO - (K_B - 1) + k, CONV_ROWS), ls]
                u1buf[pl.ds(r0, CONV_ROWS), ls] = acc + bias_ref[:, ls]
            return carry

        lax.fori_loop(0, tm // CONV_ROWS, chunk, 0)
        nb, rstd = _ln_fwd(u1buf[...])
        nb_ref[...] = nb
        rstdb_ref[...] = rstd
        u2 = nb * lbg_ref[...] + lbb_ref[...]
        u3_ref[...] = (u2 * _sig(u2)).astype(BF)

    vec = _const((1, d))
    return pl.pallas_call(
        body, name="conv_fwd", grid=(nt,),
        in_specs=[_col_spec(tm, d, k) for k in range(5)] + [_prev_halo_spec(tm, d, k) for k in (1, 2, 3, 4)]
        + [_const((K_A, d)), _const((K_B, d)), vec, vec, vec],
        out_specs=[pl.BlockSpec((tm, d), lambda i: (i, 0)), pl.BlockSpec((tm, d), lambda i: (i, 0)),
                   pl.BlockSpec((tm, d), lambda i: (i, 0)), pl.BlockSpec((tm, 1), lambda i: (i, 0)),
                   pl.BlockSpec((tm, d), lambda i: (i, 0))],
        out_shape=[SDS((s, d), F32), SDS((s, d), BF), SDS((s, d), F32), SDS((s, 1), F32), SDS((s, d), BF)],
        scratch_shapes=[pltpu.VMEM((HALO + tm, d), F32), pltpu.VMEM((HALO + tm, d), F32),
                        pltpu.VMEM((CONV_ROWS + HALO, d), F32), pltpu.VMEM((tm, d), F32)],
        compiler_params=_cp(("arbitrary",)),
    )(p, p, p, p, p, p, p, p, p, cwa, cwb, bias_b, lbg, lbb)


def _mixer_out(yapre, u3, p, x, woa, wob, wo, d):
    s = x.shape[0]
    tm = min(s, 512)

    def body(yapre_ref, u3_ref, ga_ref, gb_ref, x_ref, woa_ref, wob_ref, wo_ref,
             ya_ref, yb_ref, merged_ref, n1_ref, rstd1_ref):
        ya = _dot(yapre_ref[...], woa_ref[...])
        yb = _dot(u3_ref[...], wob_ref[...])
        ya_ref[...] = ya
        yb_ref[...] = yb
        merged = (_sig(ga_ref[...].astype(F32)) * ya + _sig(gb_ref[...].astype(F32)) * yb).astype(BF)
        merged_ref[...] = merged
        r1 = F32(ALPHA) * x_ref[...] + _dot(merged, wo_ref[...])
        n1, rstd1 = _ln_fwd(r1)
        n1_ref[...] = n1
        rstd1_ref[...] = rstd1

    row = pl.BlockSpec((tm, d), lambda i: (i, 0))
    return pl.pallas_call(
        body, name="mixer_out", grid=(s // tm,),
        in_specs=[row, row, _col_spec(tm, d, 5), _col_spec(tm, d, 6), row,
                  _const((d, d)), _const((d, d)), _const((d, d))],
        out_specs=[row, row, row, row, pl.BlockSpec((tm, 1), lambda i: (i, 0))],
        out_shape=[SDS((s, d), F32), SDS((s, d), F32), SDS((s, d), BF), SDS((s, d), F32), SDS((s, 1), F32)],
        compiler_params=_cp(("parallel",)),
    )(yapre, u3, p, p, x, woa, wob, wo)


def _mlp(n1, rstd1, tgt, wup, wdown, l1g, l1b, l2g, l2b):
    s, d = n1.shape
    dff = wup.shape[1]
    tm = min(s, 256)
    fc = min(dff, 1024)
    nq = dff // fc

    def body(n1_ref, rstd1_ref, tgt_ref, wup_ref, wdown_ref, l1g_ref, l1b_ref, l2g_ref, l2b_ref,
             x1b_ref, hb_ref, dhpre_ref, dr2b_ref, dr1_ref, dr1b_ref, acc_ref, rbuf):
        i = pl.program_id(0)
        n1v = n1_ref[...]
        x1 = n1v * l1g_ref[...] + l1b_ref[...]
        x1b = x1.astype(BF)
        x1b_ref[...] = x1b
        ff = jnp.zeros((tm, d), F32)
        for q in range(nq):
            cs = pl.ds(q * fc, fc)
            r = jnp.maximum(_dot(x1b, wup_ref[:, cs]), 0.0)
            rbuf[:, cs] = r
            hq = (r * r).astype(BF)
            hb_ref[:, cs] = hq
            ff = ff + _dot(hq, wdown_ref[cs, :])
        n2, rstd2 = _ln_fwd(F32(ALPHA) * x1 + ff)
        x2 = n2 * l2g_ref[...] + l2b_ref[...]
        err = x2 - tgt_ref[...]
        dx2 = err * F32(1.0 / d)
        dr2 = _ln_bwd(dx2 * l2g_ref[...], n2, rstd2)
        dr2b = dr2.astype(BF)
        dr2b_ref[...] = dr2b
        dx1 = F32(ALPHA) * dr2
        for q in range(nq):
            cs = pl.ds(q * fc, fc)
            dh = _dot_nt(dr2b, wdown_ref[cs, :])
            dhp = (dh * (2.0 * rbuf[:, cs])).astype(BF)
            dhpre_ref[:, cs] = dhp
            dx1 = dx1 + _dot_nt(dhp, wup_ref[:, cs])
        dr1 = _ln_bwd(dx1 * l1g_ref[...], n1v, rstd1_ref[...])
        dr1_ref[...] = dr1
        dr1b_ref[...] = dr1.astype(BF)

        @pl.when(i == 0)
        def _():
            acc_ref[...] = jnp.zeros_like(acc_ref)

        for q, val in enumerate((err * err, dx2 * n2, dx2, dx1 * n1v, dx1)):
            acc_ref[pl.ds(q, 1), :] += jnp.sum(val, axis=0, keepdims=True)

    row = pl.BlockSpec((tm, d), lambda i: (i, 0))
    wide = pl.BlockSpec((tm, dff), lambda i: (i, 0))
    vec = _const((1, d))
    return pl.pallas_call(
        body, name="mlp_fwd_bwd", grid=(s // tm,),
        in_specs=[row, pl.BlockSpec((tm, 1), lambda i: (i, 0)), row, _const((d, dff)), _const((dff, d)),
                  vec, vec, vec, vec],
        out_specs=[row, wide, wide, row, row, row, pl.BlockSpec((8, d), lambda i: (0, 0))],
        out_shape=[SDS((s, d), BF), SDS((s, dff), BF), SDS((s, dff), BF), SDS((s, d), BF), SDS((s, d), F32),
                   SDS((s, d), BF), SDS((8, d), F32)],
        scratch_shapes=[pltpu.VMEM((tm, dff), F32)],
        compiler_params=_cp(("arbitrary",)),
    )(n1, rstd1, tgt, wup, wdown, l1g, l1b, l2g, l2b)


def _mixer_bwd_local(dr1b, p, ya, yb, conva, nb, rstdb, wo, woa, wob, lbg, lbb):
    s, d = ya.shape
    tm = min(s, 256)

    def body(dr1b_ref, ba_ref, ga_ref, gb_ref, ya_ref, yb_ref, conva_ref, nb_ref, rstdb_ref,
             wo_ref, woa_ref, wob_ref, lbg_ref, lbb_ref,
             dya_ref, dyb_ref, dba_ref, dga_ref, dgb_ref, dca_ref, du1_ref, acc_ref):
        i = pl.program_id(0)
        dmerged = _dot_nt(dr1b_ref[...], wo_ref[...])
        sa = _sig(ga_ref[...].astype(F32))
        sb = _sig(gb_ref[...].astype(F32))
        dya = (dmerged * sa).astype(BF)
        dyb = (dmerged * sb).astype(BF)
        dya_ref[...] = dya
        dyb_ref[...] = dyb
        dga_ref[...] = (dmerged * ya_ref[...] * (sa * (1.0 - sa))).astype(BF)
        dgb_ref[...] = (dmerged * yb_ref[...] * (sb * (1.0 - sb))).astype(BF)
        dyapre = _dot_nt(dya, woa_ref[...])
        dba_ref[...] = (dyapre * conva_ref[...]).astype(BF)
        dca_ref[...] = dyapre * ba_ref[...].astype(F32)
        du3 = _dot_nt(dyb, wob_ref[...])
        nbv = nb_ref[...]
        u2 = nbv * lbg_ref[...] + lbb_ref[...]
        sg = _sig(u2)
        du2 = du3 * (sg * (1.0 + u2 * (1.0 - sg)))
        du1 = _ln_bwd(du2 * lbg_ref[...], nbv, rstdb_ref[...])
        du1_ref[...] = du1

        @pl.when(i == 0)
        def _():
            acc_ref[...] = jnp.zeros_like(acc_ref)

        for q, val in enumerate((du2 * nbv, du2, du1)):
            acc_ref[pl.ds(q, 1), :] += jnp.sum(val, axis=0, keepdims=True)

    row = pl.BlockSpec((tm, d), lambda i: (i, 0))
    vec = _const((1, d))
    return pl.pallas_call(
        body, name="mixer_bwd_local", grid=(s // tm,),
        in_specs=[row, _col_spec(tm, d, 0), _col_spec(tm, d, 5), _col_spec(tm, d, 6), row, row, row, row,
                  pl.BlockSpec((tm, 1), lambda i: (i, 0)), _const((d, d)), _const((d, d)), _const((d, d)), vec, vec],
        out_specs=[row, row, row, row, row, row, row, pl.BlockSpec((8, d), lambda i: (0, 0))],
        out_shape=[SDS((s, d), BF)] * 5 + [SDS((s, d), F32), SDS((s, d), F32), SDS((8, d), F32)],
        compiler_params=_cp(("arbitrary",)),
    )(dr1b, p, p, p, ya, yb, conva, nb, rstdb, wo, woa, wob, lbg, lbb)


def _conv_bwd(dca, du1, p, dba, dga, dgb, cwa, cwb, d):
    s = dca.shape[0]
    tm = min(s, 256)
    nt = s // tm

    def body(dca_ref, du1_ref, ndca_ref, ndu1_ref, ca_ref, va_ref, vb_ref, gb_ref,
             hca_ref, hva_ref, hvb_ref, hgb_ref, dba_ref, dga_ref, dgb_ref, cwa_ref, cwb_ref,
             dp_ref, dcwa_ref, dcwb_ref,
             dcabuf, du1buf, zbuf, ubuf, wd, wu, du0buf, acca, accb):
        i = pl.program_id(0)
        keep_prev = (i > 0).astype(F32)
        keep_next = (i < nt - 1).astype(F32)

        @pl.when(i == 0)
        def _():
            acca[...] = jnp.zeros_like(acca)
            accb[...] = jnp.zeros_like(accb)

        ca = ca_ref[...].astype(F32)
        va = va_ref[...].astype(F32)
        vb = vb_ref[...].astype(F32)
        sg = _sig(gb_ref[...].astype(F32))
        dcabuf[pl.ds(0, tm), :] = dca_ref[...]
        dcabuf[pl.ds(tm, HALO), :] = ndca_ref[...] * keep_next
        du1buf[pl.ds(0, tm), :] = du1_ref[...]
        du1buf[pl.ds(tm, HALO), :] = ndu1_ref[...] * keep_next
        zbuf[pl.ds(0, HALO), :] = hca_ref[...].astype(F32) * hva_ref[...].astype(F32) * keep_prev
        ubuf[pl.ds(0, HALO), :] = hvb_ref[...].astype(F32) * _sig(hgb_ref[...].astype(F32)) * keep_prev
        zbuf[pl.ds(HALO, tm), :] = ca * va
        ubuf[pl.ds(HALO, tm), :] = vb * sg

        dz = jnp.zeros((tm, d), F32)
        dcav = dca_ref[...]
        for k in range(K_A):
            dz = dz + cwa_ref[pl.ds(k, 1), :] * dcabuf[pl.ds(K_A - 1 - k, tm), :]
            acca[pl.ds(8 * k, 8), :] += _rowsum8(dcav * zbuf[pl.ds(HALO - (K_A - 1) + k, tm), :])

        def chunk(j, carry):
            r0 = pl.multiple_of(j * CONV_ROWS, CONV_ROWS)
            wd[...] = du1buf[pl.ds(r0, CONV_ROWS + HALO), :]
            wu[...] = ubuf[pl.ds(r0, CONV_ROWS + HALO), :]
            for lc in range(d // CONV_LANES):
                ls = pl.ds(lc * CONV_LANES, CONV_LANES)
                dchunk = wd[pl.ds(0, CONV_ROWS), ls]
                acc = jnp.zeros((CONV_ROWS, CONV_LANES), F32)
                for k in range(K_B):
                    acc = acc + cwb_ref[pl.ds(k, 1), ls] * wd[pl.ds(K_B - 1 - k, CONV_ROWS), ls]
                    accb[pl.ds(8 * k, 8), ls] += _rowsum8(dchunk * wu[pl.ds(HALO - (K_B - 1) + k, CONV_ROWS), ls])
                du0buf[pl.ds(r0, CONV_ROWS), ls] = acc
            return carry

        lax.fori_loop(0, tm // CONV_ROWS, chunk, 0)
        du0 = du0buf[...]
        dp_ref[:, pl.ds(0, d)] = dba_ref[...]
        dp_ref[:, pl.ds(d, d)] = (dz * va).astype(BF)
        dp_ref[:, pl.ds(2 * d, d)] = (dz * ca).astype(BF)
        dp_ref[:, pl.ds(3 * d, d)] = (du0 * sg).astype(BF)
        dp_ref[:, pl.ds(4 * d, d)] = (du0 * vb * (sg * (1.0 - sg))).astype(BF)
        dp_ref[:, pl.ds(5 * d, d)] = dga_ref[...]
        dp_ref[:, pl.ds(6 * d, d)] = dgb_ref[...]

        @pl.when(i == nt - 1)
        def _():
            dcwa_ref[...] = jnp.zeros_like(dcwa_ref)
            dcwb_ref[...] = jnp.zeros_like(dcwb_ref)
            for k in range(K_A):
                dcwa_ref[pl.ds(k, 1), :] = jnp.sum(acca[pl.ds(8 * k, 8), :], axis=0, keepdims=True)
            for k in range(K_B):
                dcwb_ref[pl.ds(k, 1), :] = jnp.sum(accb[pl.ds(8 * k, 8), :], axis=0, keepdims=True)

    row = pl.BlockSpec((tm, d), lambda i: (i, 0))
    nxt = _next_halo_spec(tm, d, 0, s)
    return pl.pallas_call(
        body, name="conv_bwd", grid=(nt,),
        in_specs=[row, row, nxt, nxt] + [_col_spec(tm, d, k) for k in (1, 2, 3, 4)]
        + [_prev_halo_spec(tm, d, k) for k in (1, 2, 3, 4)] + [row, row, row, _const((K_A, d)), _const((K_B, d))],
        out_specs=[pl.BlockSpec((tm, 7 * d), lambda i: (i, 0)), pl.BlockSpec((8, d), lambda i: (0, 0)),
                   pl.BlockSpec((32, d), lambda i: (0, 0))],
        out_shape=[SDS((s, 7 * d), BF), SDS((8, d), F32), SDS((32, d), F32)],
        scratch_shapes=[pltpu.VMEM((tm + HALO, d), F32), pltpu.VMEM((tm + HALO, d), F32),
                        pltpu.VMEM((HALO + tm, d), F32), pltpu.VMEM((HALO + tm, d), F32),
                        pltpu.VMEM((CONV_ROWS + HALO, d), F32), pltpu.VMEM((CONV_ROWS + HALO, d), F32),
                        pltpu.VMEM((tm, d), F32), pltpu.VMEM((8 * K_A, d), F32), pltpu.VMEM((8 * K_B, d), F32)],
        compiler_params=_cp(("arbitrary",)),
    )(dca, du1, dca, du1, p, p, p, p, p, p, p, p, dba, dga, dgb, cwa, cwb)


def _grad_w(a, b, name):
    s, m = a.shape
    n = b.shape[1]
    tm, tn, tk = _tile(m, 1024), _tile(n, 1024), _tile(s, 512)

    def body(a_ref, b_ref, o_ref):
        @pl.when(pl.program_id(2) == 0)
        def _():
            o_ref[...] = jnp.zeros_like(o_ref)

        o_ref[...] += _dot_tn(a_ref[...], b_ref[...])

    return pl.pallas_call(
        body, name=name, grid=(m // tm, n // tn, s // tk),
        in_specs=[pl.BlockSpec((tk, tm), lambda i, j, k: (k, i)), pl.BlockSpec((tk, tn), lambda i, j, k: (k, j))],
        out_specs=pl.BlockSpec((tm, tn), lambda i, j, k: (i, j)),
        out_shape=SDS((m, n), F32),
        compiler_params=_cp(("parallel", "parallel", "arbitrary")),
    )(a, b)


def _grad_x(dr1, dp, wi):
    s, d = dr1.shape
    n = wi.shape[1]
    tm, tk = min(s, 1024), d

    def body(dr1_ref, dp_ref, w_ref, o_ref):
        @pl.when(pl.program_id(1) == 0)
        def _():
            o_ref[...] = F32(ALPHA) * dr1_ref[...]

        o_ref[...] += _dot_nt(dp_ref[...], w_ref[...])

    return pl.pallas_call(
        body, name="grad_x", grid=(s // tm, n // tk),
        in_specs=[pl.BlockSpec((tm, d), lambda i, k: (i, 0)), pl.BlockSpec((tm, tk), lambda i, k: (i, k)),
                  pl.BlockSpec((d, tk), lambda i, k: (0, k))],
        out_specs=pl.BlockSpec((tm, d), lambda i, k: (i, 0)),
        out_shape=SDS((s, d), F32),
        compiler_params=_cp(("parallel", "arbitrary")),
    )(dr1, dp, wi)


def _adamw_math(w, g, m, v):
    m2 = ADAM_B1 * m + (1.0 - ADAM_B1) * g
    v2 = ADAM_B2 * v + (1.0 - ADAM_B2) * (g * g)
    m_hat = m2 / (1.0 - ADAM_B1 ** ADAM_STEP)
    v_hat = v2 / (1.0 - ADAM_B2 ** ADAM_STEP)
    delta = -ADAM_LR * (m_hat / (jnp.sqrt(v_hat) + ADAM_EPS) + ADAM_WD * w)
    return delta, m2, v2


def _adamw(w, g, m, v, name):
    r, c = w.shape
    tr = r if r <= 256 else 256

    def body(w_ref, g_ref, m_ref, v_ref, d_ref, m2_ref, v2_ref):
        delta, m2, v2 = _adamw_math(w_ref[...], g_ref[...], m_ref[...], v_ref[...])
        d_ref[...] = delta
        m2_ref[...] = m2
        v2_ref[...] = v2

    blk = pl.BlockSpec((tr, c), lambda i: (i, 0))
    return pl.pallas_call(
        body, name=name, grid=(r // tr,), in_specs=[blk] * 4, out_specs=[blk] * 3,
        out_shape=[SDS((r, c), F32)] * 3, compiler_params=_cp(("parallel",)),
    )(w, g, m, v)


def _sum_parts(parts, name):
    k, r, c = parts.shape
    tr = r if r <= 256 else 256

    def body(p_ref, o_ref):
        acc = p_ref[0]
        for q in range(1, k):
            acc = acc + p_ref[q]
        o_ref[...] = acc

    return pl.pallas_call(
        body, name=name, grid=(r // tr,),
        in_specs=[pl.BlockSpec((k, tr, c), lambda i: (0, i, 0))],
        out_specs=pl.BlockSpec((tr, c), lambda i: (i, 0)),
        out_shape=SDS((r, c), F32), compiler_params=_cp(("parallel",)),
    )(parts)


def _add2(a, b, name):
    k, r, c = a.shape
    tr = r if r <= 256 else 256

    def body(a_ref, b_ref, o_ref):
        o_ref[...] = a_ref[...] + b_ref[...]

    blk = pl.BlockSpec((1, tr, c), lambda q, i: (q, i, 0))
    return pl.pallas_call(
        body, name=name, grid=(k, r // tr), in_specs=[blk, blk], out_specs=blk,
        out_shape=SDS((k, r, c), F32), compiler_params=_cp(("parallel", "parallel")),
    )(a, b)


def _sum_own_and_parts(own, parts, name):
    k, r, c = parts.shape
    tr = r if r <= 256 else 256

    def body(o_in, p_ref, o_ref):
        acc = o_in[...]
        for q in range(k):
            acc = acc + p_ref[q]
        o_ref[...] = acc

    return pl.pallas_call(
        body, name=name, grid=(r // tr,),
        in_specs=[pl.BlockSpec((tr, c), lambda i: (i, 0)), pl.BlockSpec((k, tr, c), lambda i: (0, i, 0))],
        out_specs=pl.BlockSpec((tr, c), lambda i: (i, 0)),
        out_shape=SDS((r, c), F32), compiler_params=_cp(("parallel",)),
    )(own, parts)


ANY = pl.BlockSpec(memory_space=pl.ANY)


def _on_each_device(fn):
    x, y, c = lax.axis_index("x"), lax.axis_index("y"), lax.axis_index("c")
    for sx in (0, 1):
        for sy in (0, 1):
            for sc in (0, 1):
                @pl.when(jnp.logical_and(jnp.logical_and(x == sx, y == sy), c == sc))
                def _(sx=sx, sy=sy, sc=sc):
                    fn(sx, sy, sc)


def _remote(src, dst, send_sem, recv_sem, to):
    return pltpu.make_async_remote_copy(src_ref=src, dst_ref=dst, send_sem=send_sem, recv_sem=recv_sem,
                                        device_id=to, device_id_type=MESH)


def _shard_piece(ref, axis, full_rows, shard_cols, j, h):
    if axis == 1:
        hr = full_rows // 2
        return ref.at[pl.ds(h * hr, hr), pl.ds(j * shard_cols, shard_cols)]
    sr = full_rows // 4
    hr = sr // 2
    return ref.at[pl.ds(j * sr + h * hr, hr), :]


def _gather_weights(shards, axes):
    n = len(shards)
    fulls = []
    for w, a in zip(shards, axes):
        r, c = w.shape
        fulls.append(SDS((r, 4 * c) if a == 1 else (4 * r, c), w.dtype))

    def body(*refs):
        srcs, outs = refs[:n], refs[n:2 * n]
        send_sems, recv_sems, local_sems = refs[2 * n:]

        def dev(sx, sy, sc):
            j = 2 * sx + sy
            sib = (sx, sy, 1 - sc)

            def region(w, jj, h):
                return _shard_piece(outs[w], axes[w], fulls[w].shape[0], shards[w].shape[1], jj, h)

            def own_half(w, h):
                hr = shards[w].shape[0] // 2
                return srcs[w].at[pl.ds(h * hr, hr), :]

            pending = []
            for w in range(n):
                for h in (0, 1):
                    cp = pltpu.make_async_copy(own_half(w, h), region(w, j, h), local_sems.at[2 * w + h])
                    cp.start()
                    pending.append(cp)
            sends = []
            for w in range(n):
                for r, (fx, fy) in enumerate(CHIP_RELS):
                    k = 6 * w + r
                    cp = _remote(own_half(w, sc), region(w, j, sc), send_sems.at[k], recv_sems.at[k],
                                 (sx ^ fx, sy ^ fy, sc))
                    cp.start()
                    sends.append(cp)
            for w in range(n):
                for r, (fx, fy) in enumerate(CHIP_RELS):
                    pj = 2 * (sx ^ fx) + (sy ^ fy)
                    k = 6 * w + r
                    got = region(w, pj, sc)
                    _remote(got, got, send_sems.at[k], recv_sems.at[k], (sx ^ fx, sy ^ fy, sc)).wait_recv()
                    cp = _remote(got, got, send_sems.at[k + 3], recv_sems.at[k + 3], sib)
                    cp.start()
                    sends.append(cp)
            for w in range(n):
                for r, (fx, fy) in enumerate(CHIP_RELS):
                    pj = 2 * (sx ^ fx) + (sy ^ fy)
                    k = 6 * w + 3 + r
                    got = region(w, pj, 1 - sc)
                    _remote(got, got, send_sems.at[k], recv_sems.at[k], sib).wait_recv()
            for cp in sends:
                cp.wait_send()
            for cp in pending:
                cp.wait()

        _on_each_device(dev)

    return pl.pallas_call(
        body, name="gather_weights", in_specs=[ANY] * n, out_specs=[ANY] * n, out_shape=fulls,
        scratch_shapes=[pltpu.SemaphoreType.DMA((6 * n,)), pltpu.SemaphoreType.DMA((6 * n,)),
                        pltpu.SemaphoreType.DMA((2 * n,))],
        compiler_params=pltpu.CompilerParams(has_side_effects=True),
    )(*shards)


def _piece_shape(g, axis):
    r, c = g.shape
    return (r // 2, c // 4) if axis == 1 else (r // 8, c)


def _pair_exchange(grads, axes):
    n = len(grads)
    shapes = [SDS((4,) + _piece_shape(g, a), F32) for g, a in zip(grads, axes)]

    def body(*refs):
        gs, own, land = refs[:n], refs[n:2 * n], refs[2 * n:3 * n]
        send_sems, recv_sems, local_sems = refs[3 * n:]

        def dev(sx, sy, sc):
            sib = (sx, sy, 1 - sc)
            cps = []
            for w in range(n):
                r, c = grads[w].shape
                for jj in range(4):
                    k = 4 * w + jj
                    lc = pltpu.make_async_copy(_shard_piece(gs[w], axes[w], r, c // 4, jj, sc), own[w].at[jj],
                                               local_sems.at[k])
                    lc.start()
                    rc = _remote(_shard_piece(gs[w], axes[w], r, c // 4, jj, 1 - sc), land[w].at[jj],
                                 send_sems.at[k], recv_sems.at[k], sib)
                    rc.start()
                    cps.append((lc, rc))
            for lc, rc in cps:
                lc.wait()
                rc.wait()

        _on_each_device(dev)

    return pl.pallas_call(
        body, name="pair_exchange", in_specs=[ANY] * n, out_specs=[ANY] * (2 * n), out_shape=shapes + shapes,
        scratch_shapes=[pltpu.SemaphoreType.DMA((4 * n,)), pltpu.SemaphoreType.DMA((4 * n,)),
                        pltpu.SemaphoreType.DMA((4 * n,))],
        compiler_params=pltpu.CompilerParams(has_side_effects=True),
    )(*grads)


def _chip_scatter(pieces):
    n = len(pieces)
    own_shapes = [SDS(p.shape[1:], F32) for p in pieces]
    land_shapes = [SDS((3,) + p.shape[1:], F32) for p in pieces]

    def body(*refs):
        ps, own, land = refs[:n], refs[n:2 * n], refs[2 * n:3 * n]
        send_sems, recv_sems, local_sems = refs[3 * n:]

        def dev(sx, sy, sc):
            j = 2 * sx + sy
            cps = []
            for w in range(n):
                lc = pltpu.make_async_copy(ps[w].at[j], own[w], local_sems.at[w])
                lc.start()
                cps.append(lc)
                for r, (fx, fy) in enumerate(CHIP_RELS):
                    k = 3 * w + r
                    rc = _remote(ps[w].at[2 * (sx ^ fx) + (sy ^ fy)], land[w].at[r], send_sems.at[k], recv_sems.at[k],
                                 (sx ^ fx, sy ^ fy, sc))
                    rc.start()
                    cps.append(rc)
            for cp in cps:
                cp.wait()

        _on_each_device(dev)

    return pl.pallas_call(
        body, name="chip_scatter", in_specs=[ANY] * n, out_specs=[ANY] * (2 * n), out_shape=own_shapes + land_shapes,
        scratch_shapes=[pltpu.SemaphoreType.DMA((3 * n,)), pltpu.SemaphoreType.DMA((3 * n,)),
                        pltpu.SemaphoreType.DMA((n,))],
        compiler_params=pltpu.CompilerParams(has_side_effects=True),
    )(*pieces)


def _pair_share(halves):
    n = len(halves)
    shapes = [SDS((2 * h.shape[0], h.shape[1]), F32) for h in halves]

    def body(*refs):
        hs, outs = refs[:n], refs[n:2 * n]
        send_sems, recv_sems, local_sems = refs[2 * n:]

        def dev(sx, sy, sc):
            cps = []
            for w in range(n):
                hr = halves[w].shape[0]
                dst = outs[w].at[pl.ds(sc * hr, hr), :]
                lc = pltpu.make_async_copy(hs[w], dst, local_sems.at[w])
                lc.start()
                rc = _remote(hs[w], dst, send_sems.at[w], recv_sems.at[w], (sx, sy, 1 - sc))
                rc.start()
                cps += [lc, rc]
            for cp in cps:
                cp.wait()

        _on_each_device(dev)

    return pl.pallas_call(
        body, name="pair_share", in_specs=[ANY] * n, out_specs=[ANY] * n, out_shape=shapes,
        scratch_shapes=[pltpu.SemaphoreType.DMA((n,)), pltpu.SemaphoreType.DMA((n,)), pltpu.SemaphoreType.DMA((n,))],
        compiler_params=pltpu.CompilerParams(has_side_effects=True),
    )(*halves)


def _gather_small(pack):
    r, c = pack.shape

    def body(src, out, send_sems, recv_sems, local_sem):
        def dev(sx, sy, sc):
            me = 4 * sx + 2 * sy + sc
            lc = pltpu.make_async_copy(src, out.at[me], local_sem)
            lc.start()
            cps = [lc]
            for k, (fx, fy, fc) in enumerate(DEV_RELS):
                rc = _remote(src, out.at[me], send_sems.at[k], recv_sems.at[k], (sx ^ fx, sy ^ fy, sc ^ fc))
                rc.start()
                cps.append(rc)
            for cp in cps:
                cp.wait()

        _on_each_device(dev)

    return pl.pallas_call(
        body, name="gather_small", in_specs=[ANY], out_specs=ANY, out_shape=SDS((8, r, c), F32),
        scratch_shapes=[pltpu.SemaphoreType.DMA((7,)), pltpu.SemaphoreType.DMA((7,)), pltpu.SemaphoreType.DMA(())],
        compiler_params=pltpu.CompilerParams(has_side_effects=True),
    )(pack)


def kernel(x, w_in, conv_a_w, w_out_a, conv_b_w, conv_b_bias, ln_b_gamma, ln_b_beta, w_out_b, w_o, ln1_gamma, ln1_beta, w_up, w_down, ln2_gamma, ln2_beta, loss_target, m_w_in, m_conv_a_w, m_w_out_a, m_conv_b_w, m_conv_b_bias, m_ln_b_gamma, m_ln_b_beta, m_w_out_b, m_w_o, m_ln1_gamma, m_ln1_beta, m_w_up, m_w_down, m_ln2_gamma, m_ln2_beta, v_w_in, v_conv_a_w, v_w_out_a, v_conv_b_w, v_conv_b_bias, v_ln_b_gamma, v_ln_b_beta, v_w_out_b, v_w_o, v_ln1_gamma, v_ln1_beta, v_w_up, v_w_down, v_ln2_gamma, v_ln2_beta):
    s, d = x.shape[1], x.shape[2]
    xs = x.reshape(s, d)
    tgt = loss_target.reshape(s, d)
    dq = d // 4
    chip = 2 * lax.axis_index("x") + lax.axis_index("y")

    conv_pack = jnp.concatenate([jnp.pad(conv_a_w, ((0, 8 - K_A), (0, 0))), jnp.pad(conv_b_w, ((0, 32 - K_B), (0, 0))),
                                 jnp.zeros((8, dq), F32)], axis=0)
    axes = (1, 0, 0, 0, 1, 0)
    shards = [w.astype(BF) for w in (w_in, w_out_a, w_out_b, w_o, w_up, w_down)]
    wi, woa, wob, wo, wup, wdown, convs = _gather_weights(shards + [conv_pack], axes + (1,))
    cwa, cwb = convs[0:K_A], convs[8:8 + K_B]
    vec = lambda a: a.reshape(1, d)
    bias_b, lbg, lbb = vec(conv_b_bias), vec(ln_b_gamma), vec(ln_b_beta)
    l1g, l1b, l2g, l2b = vec(ln1_gamma), vec(ln1_beta), vec(ln2_gamma), vec(ln2_beta)

    p, xb = _in_proj(xs, wi)
    conva, yapre, nb, rstdb, u3 = _conv_fwd(p, cwa, cwb, bias_b, lbg, lbb, d)
    ya, yb, merged, n1, rstd1 = _mixer_out(yapre, u3, p, xs, woa, wob, wo, d)
    x1b, hb, dhpre, dr2b, dr1, dr1b, acc_mlp = _mlp(n1, rstd1, tgt, wup, wdown, l1g, l1b, l2g, l2b)
    dya, dyb, dba, dga, dgb, dca, du1, acc_mix = _mixer_bwd_local(dr1b, p, ya, yb, conva, nb, rstdb, wo, woa, wob,
                                                                   lbg, lbb)
    dp, dcwa, dcwb = _conv_bwd(dca, du1, p, dba, dga, dgb, cwa, cwb, d)
    grad_x = _grad_x(dr1, dp, wi)
    grads = [
        _grad_w(xb, dp, "grad_w_in"),
        _grad_w(yapre, dya, "grad_w_out_a"),
        _grad_w(u3, dyb, "grad_w_out_b"),
        _grad_w(merged, dr1b, "grad_w_o"),
        _grad_w(x1b, dhpre, "grad_w_up"),
        _grad_w(hb, dr2b, "grad_w_down"),
    ]

    own1, land1 = _split2(_pair_exchange(grads, axes))
    pieces = [_add2(a, b, "pair_add_%d" % i) for i, (a, b) in enumerate(zip(own1, land1))]
    own2, land2 = _split2(_chip_scatter(pieces))
    halves = [_sum_own_and_parts(a, b, "chip_sum_%d" % i) for i, (a, b) in enumerate(zip(own2, land2))]
    g_in, g_oa, g_ob, g_o, g_up, g_down = _pair_share(halves)

    pack = jnp.concatenate([dcwa, dcwb, acc_mix, acc_mlp], axis=0)
    small = _sum_parts(_gather_small(pack), "small_sum")
    g_ca = lax.dynamic_slice(small, (0, chip * dq), (K_A, dq))
    g_cb = lax.dynamic_slice(small, (8, chip * dq), (K_B, dq))
    g_vec = jnp.stack([small[r] for r in (42, 40, 41, 51, 52, 49, 50)])

    loss = lax.psum((0.5 / d) * jnp.sum(acc_mlp[0]), ("x", "y", "c"))

    big = {}
    for name, w, g, m, v in (("w_in", w_in, g_in, m_w_in, v_w_in), ("w_out_a", w_out_a, g_oa, m_w_out_a, v_w_out_a),
                             ("w_out_b", w_out_b, g_ob, m_w_out_b, v_w_out_b), ("w_o", w_o, g_o, m_w_o, v_w_o),
                             ("w_up", w_up, g_up, m_w_up, v_w_up), ("w_down", w_down, g_down, m_w_down, v_w_down),
                             ("conv_a_w", conv_a_w, g_ca, m_conv_a_w, v_conv_a_w),
                             ("conv_b_w", conv_b_w, g_cb, m_conv_b_w, v_conv_b_w)):
        big[name] = (g,) + tuple(_adamw(w, g, m, v, "adamw_" + name))
    vec_names = ("conv_b_bias", "ln_b_gamma", "ln_b_beta", "ln1_gamma", "ln1_beta", "ln2_gamma", "ln2_beta")
    w7 = jnp.stack([conv_b_bias, ln_b_gamma, ln_b_beta, ln1_gamma, ln1_beta, ln2_gamma, ln2_beta])
    m7 = jnp.stack([m_conv_b_bias, m_ln_b_gamma, m_ln_b_beta, m_ln1_gamma, m_ln1_beta, m_ln2_gamma, m_ln2_beta])
    v7 = jnp.stack([v_conv_b_bias, v_ln_b_gamma, v_ln_b_beta, v_ln1_gamma, v_ln1_beta, v_ln2_gamma, v_ln2_beta])
    d7, nm7, nv7 = _adamw(w7, g_vec, m7, v7, "adamw_vectors")
    for q, name in enumerate(vec_names):
        big[name] = (g_vec[q], d7[q], nm7[q], nv7[q])

    order = ("w_in", "conv_a_w", "w_out_a", "conv_b_w", "conv_b_bias", "ln_b_gamma", "ln_b_beta", "w_out_b", "w_o",
             "ln1_gamma", "ln1_beta", "w_up", "w_down", "ln2_gamma", "ln2_beta")
    outs = [loss, grad_x.reshape(x.shape)]
    for part in range(4):
        outs += [big[name][part] for name in order]
    return tuple(outs)


def _split2(seq):
    seq = list(seq)
    h = len(seq) // 2
    return seq[:h], seq[h:]
```

```python
import jax
import jax.numpy as jnp
from jax import lax
from jax.experimental import pallas as pl
from jax.experimental.pallas import tpu as pltpu

F32 = jnp.float32
BF = jnp.bfloat16
SDS = jax.ShapeDtypeStruct
MESH = pl.DeviceIdType.MESH

ALPHA = 2.0 ** 0.25
LN_EPS = 1e-5
K_A = 3
K_B = 31
HALO = 32
CONV_ROWS = 64
CONV_LANES = 128
ADAM_LR = 0.001
ADAM_B1 = 0.9
ADAM_B2 = 0.999
ADAM_EPS = 1e-08
ADAM_WD = 0.01
ADAM_STEP = 10
P_DT = F32
CHIP_RELS = ((1, 0), (0, 1), (1, 1))
DEV_RELS = tuple((fx, fy, fc) for fx in (0, 1) for fy in (0, 1) for fc in (0, 1))[1:]


def _cp(sem=None, vmem_mb=56):
    return pltpu.CompilerParams(dimension_semantics=sem, vmem_limit_bytes=vmem_mb << 20)


def _const(shape):
    return pl.BlockSpec(shape, lambda *_: (0,) * len(shape), pipeline_mode=pl.Buffered(1))


def _sig(v):
    return jax.nn.sigmoid(v)


def _ln_fwd(r):
    mu = jnp.mean(r, axis=-1, keepdims=True)
    xc = r - mu
    var = jnp.mean(xc * xc, axis=-1, keepdims=True)
    rstd = lax.rsqrt(var + LN_EPS)
    return xc * rstd, rstd


def _ln_bwd(dn, n, rstd):
    m1 = jnp.mean(dn, axis=-1, keepdims=True)
    m2 = jnp.mean(dn * n, axis=-1, keepdims=True)
    return rstd * (dn - m1 - n * m2)


def _dot(a, b):
    return jnp.dot(a, b, preferred_element_type=F32)


def _dot_nt(a, b):
    return lax.dot_general(a, b, (((1,), (1,)), ((), ())), preferred_element_type=F32)


def _dot_tn(a, b):
    return lax.dot_general(a, b, (((0,), (0,)), ((), ())), preferred_element_type=F32)


def _tile(n, pref):
    if n <= pref:
        return n
    return max(t for t in range(128, pref + 1, 128) if n % t == 0)


def _rowsum8(v):
    acc = v[0:8]
    for g in range(1, v.shape[0] // 8):
        acc = acc + v[8 * g:8 * g + 8]
    return acc


def _taps(win, offsets, rows):
    r_all = win.shape[0]
    by_res = {}
    for k, o in enumerate(offsets):
        by_res.setdefault(o % 8, []).append((k, o // 8))
    for s, taps in sorted(by_res.items()):
        r = win if s == 0 else pltpu.roll(win, r_all - s, 0)
        for k, q in taps:
            yield k, r[8 * q:8 * q + rows]


CAUSAL_A = [HALO - (K_A - 1) + k for k in range(K_A)]
CAUSAL_B = [HALO - (K_B - 1) + k for k in range(K_B)]
ANTI_A = [K_A - 1 - k for k in range(K_A)]
ANTI_B = [K_B - 1 - k for k in range(K_B)]


def _in_proj(x, wi):
    s, d = x.shape
    n = wi.shape[1]
    tm, tn = min(s, 1024), d

    def body(x_ref, w_ref, p_ref, xb_ref):
        @pl.when(pl.program_id(1) == 0)
        def _():
            xb_ref[...] = x_ref[...].astype(BF)

        p_ref[...] = _dot(xb_ref[...], w_ref[...]).astype(p_ref.dtype)

    return pl.pallas_call(
        body, name="in_proj", grid=(s // tm, n // tn),
        in_specs=[pl.BlockSpec((tm, d), lambda i, j: (i, 0)), pl.BlockSpec((d, tn), lambda i, j: (0, j))],
        out_specs=[pl.BlockSpec((tm, tn), lambda i, j: (i, j)), pl.BlockSpec((tm, d), lambda i, j: (i, 0))],
        out_shape=[SDS((s, n), P_DT), SDS((s, d), BF)],
        compiler_params=_cp(("parallel", "arbitrary")),
    )(x, wi)


def _col_spec(tm, d, k):
    return pl.BlockSpec((tm, d), lambda i, k=k: (i, k))


def _prev_halo_spec(tm, d, k):
    r = tm // HALO
    return pl.BlockSpec((HALO, d), lambda i, k=k: (jnp.maximum(i * r - 1, 0), k))


def _next_halo_spec(tm, d, k, s):
    r = tm // HALO
    last = s // HALO - 1
    return pl.BlockSpec((HALO, d), lambda i, k=k: (jnp.minimum((i + 1) * r, last), k))


def _conv_fwd(p, cwa, cwb, bias_b, lbg, lbb, d):
    s = p.shape[0]
    tm = min(s, 256)
    nt = s // tm

    def body(ba_ref, ca_ref, va_ref, vb_ref, gb_ref, hca_ref, hva_ref, hvb_ref, hgb_ref,
             cwa_ref, cwb_ref, bias_ref, lbg_ref, lbb_ref,
             conva_ref, yapre_ref, nb_ref, rstdb_ref, u3_ref,
             zbuf, ubuf, u1buf):
        i = pl.program_id(0)
        keep = (i > 0).astype(F32)
        zbuf[pl.ds(0, HALO), :] = hca_ref[...].astype(F32) * hva_ref[...].astype(F32) * keep
        ubuf[pl.ds(0, HALO), :] = hvb_ref[...].astype(F32) * _sig(hgb_ref[...].astype(F32)) * keep
        zbuf[pl.ds(HALO, tm), :] = ca_ref[...].astype(F32) * va_ref[...].astype(F32)
        ubuf[pl.ds(HALO, tm), :] = vb_ref[...].astype(F32) * _sig(gb_ref[...].astype(F32))

        def chunk(j, carry):
            r0 = pl.multiple_of(j * CONV_ROWS, CONV_ROWS)
            rows = pl.ds(r0, CONV_ROWS)
            for lc in range(d // CONV_LANES):
                ls = pl.ds(lc * CONV_LANES, CONV_LANES)
                acc = jnp.zeros((CONV_ROWS, CONV_LANES), F32)
                for k, sl in _taps(zbuf[pl.ds(r0, CONV_ROWS + HALO), ls], CAUSAL_A, CONV_ROWS):
                    acc = acc + cwa_ref[pl.ds(k, 1), ls] * sl
                conva_ref[rows, ls] = acc
                yapre_ref[rows, ls] = (ba_ref[rows, ls].astype(F32) * acc).astype(BF)
                acc = jnp.zeros((CONV_ROWS, CONV_LANES), F32)
                for k, sl in _taps(ubuf[pl.ds(r0, CONV_ROWS + HALO), ls], CAUSAL_B, CONV_ROWS):
                    acc = acc + cwb_ref[pl.ds(k, 1), ls] * sl
                u1buf[rows, ls] = acc + bias_ref[:, ls]
            return carry

        lax.fori_loop(0, tm // CONV_ROWS, chunk, 0)
        nb, rstd = _ln_fwd(u1buf[...])
        nb_ref[...] = nb
        rstdb_ref[...] = rstd
        u2 = nb * lbg_ref[...] + lbb_ref[...]
        u3_ref[...] = (u2 * _sig(u2)).astype(BF)

    vec = _const((1, d))
    return pl.pallas_call(
        body, name="conv_fwd", grid=(nt,),
        in_specs=[_col_spec(tm, d, k) for k in range(5)] + [_prev_halo_spec(tm, d, k) for k in (1, 2, 3, 4)]
        + [_const((K_A, d)), _const((K_B, d)), vec, vec, vec],
        out_specs=[pl.BlockSpec((tm, d), lambda i: (i, 0)), pl.BlockSpec((tm, d), lambda i: (i, 0)),
                   pl.BlockSpec((tm, d), lambda i: (i, 0)), pl.BlockSpec((tm, 1), lambda i: (i, 0)),
                   pl.BlockSpec((tm, d), lambda i: (i, 0))],
        out_shape=[SDS((s, d), F32), SDS((s, d), BF), SDS((s, d), F32), SDS((s, 1), F32), SDS((s, d), BF)],
        scratch_shapes=[pltpu.VMEM((HALO + tm, d), F32), pltpu.VMEM((HALO + tm, d), F32), pltpu.VMEM((tm, d), F32)],
        compiler_params=_cp(("arbitrary",)),
    )(p, p, p, p, p, p, p, p, p, cwa, cwb, bias_b, lbg, lbb)


def _mixer_out(yapre, u3, p, x, woa, wob, wo, d):
    s = x.shape[0]
    tm = min(s, 512)

    def body(yapre_ref, u3_ref, ga_ref, gb_ref, x_ref, woa_ref, wob_ref, wo_ref,
             ya_ref, yb_ref, merged_ref, n1_ref, rstd1_ref):
        ya = _dot(yapre_ref[...], woa_ref[...])
        yb = _dot(u3_ref[...], wob_ref[...])
        ya_ref[...] = ya
        yb_ref[...] = yb
        merged = (_sig(ga_ref[...].astype(F32)) * ya + _sig(gb_ref[...].astype(F32)) * yb).astype(BF)
        merged_ref[...] = merged
        r1 = F32(ALPHA) * x_ref[...] + _dot(merged, wo_ref[...])
        n1, rstd1 = _ln_fwd(r1)
        n1_ref[...] = n1
        rstd1_ref[...] = rstd1

    row = pl.BlockSpec((tm, d), lambda i: (i, 0))
    return pl.pallas_call(
        body, name="mixer_out", grid=(s // tm,),
        in_specs=[row, row, _col_spec(tm, d, 5), _col_spec(tm, d, 6), row,
                  _const((d, d)), _const((d, d)), _const((d, d))],
        out_specs=[row, row, row, row, pl.BlockSpec((tm, 1), lambda i: (i, 0))],
        out_shape=[SDS((s, d), F32), SDS((s, d), F32), SDS((s, d), BF), SDS((s, d), F32), SDS((s, 1), F32)],
        compiler_params=_cp(("parallel",)),
    )(yapre, u3, p, p, x, woa, wob, wo)


def _mlp(n1, rstd1, tgt, wup, wdown, l1g, l1b, l2g, l2b):
    s, d = n1.shape
    dff = wup.shape[1]
    tm = min(s, 256)
    fc = min(dff, 1024)
    nq = dff // fc

    def body(n1_ref, rstd1_ref, tgt_ref, wup_ref, wdown_ref, l1g_ref, l1b_ref, l2g_ref, l2b_ref,
             x1b_ref, hb_ref, dhpre_ref, dr2b_ref, dr1_ref, dr1b_ref, acc_ref, rbuf):
        i = pl.program_id(0)
        n1v = n1_ref[...]
        x1 = n1v * l1g_ref[...] + l1b_ref[...]
        x1b = x1.astype(BF)
        x1b_ref[...] = x1b
        ff = jnp.zeros((tm, d), F32)
        for q in range(nq):
            cs = pl.ds(q * fc, fc)
            r = jnp.maximum(_dot(x1b, wup_ref[:, cs]), 0.0)
            rbuf[:, cs] = r
            hq = (r * r).astype(BF)
            hb_ref[:, cs] = hq
            ff = ff + _dot(hq, wdown_ref[cs, :])
        n2, rstd2 = _ln_fwd(F32(ALPHA) * x1 + ff)
        x2 = n2 * l2g_ref[...] + l2b_ref[...]
        err = x2 - tgt_ref[...]
        dx2 = err * F32(1.0 / d)
        dr2 = _ln_bwd(dx2 * l2g_ref[...], n2, rstd2)
        dr2b = dr2.astype(BF)
        dr2b_ref[...] = dr2b
        dx1 = F32(ALPHA) * dr2
        for q in range(nq):
            cs = pl.ds(q * fc, fc)
            dh = _dot_nt(dr2b, wdown_ref[cs, :])
            dhp = (dh * (2.0 * rbuf[:, cs])).astype(BF)
            dhpre_ref[:, cs] = dhp
            dx1 = dx1 + _dot_nt(dhp, wup_ref[:, cs])
        dr1 = _ln_bwd(dx1 * l1g_ref[...], n1v, rstd1_ref[...])
        dr1_ref[...] = dr1
        dr1b_ref[...] = dr1.astype(BF)

        @pl.when(i == 0)
        def _():
            acc_ref[...] = jnp.zeros_like(acc_ref)

        for q, val in enumerate((err * err, dx2 * n2, dx2, dx1 * n1v, dx1)):
            acc_ref[pl.ds(q, 1), :] += jnp.sum(val, axis=0, keepdims=True)

    row = pl.BlockSpec((tm, d), lambda i: (i, 0))
    wide = pl.BlockSpec((tm, dff), lambda i: (i, 0))
    vec = _const((1, d))
    return pl.pallas_call(
        body, name="mlp_fwd_bwd", grid=(s // tm,),
        in_specs=[row, pl.BlockSpec((tm, 1), lambda i: (i, 0)), row, _const((d, dff)), _const((dff, d)),
                  vec, vec, vec, vec],
        out_specs=[row, wide, wide, row, row, row, pl.BlockSpec((8, d), lambda i: (0, 0))],
        out_shape=[SDS((s, d), BF), SDS((s, dff), BF), SDS((s, dff), BF), SDS((s, d), BF), SDS((s, d), F32),
                   SDS((s, d), BF), SDS((8, d), F32)],
        scratch_shapes=[pltpu.VMEM((tm, dff), F32)],
        compiler_params=_cp(("arbitrary",)),
    )(n1, rstd1, tgt, wup, wdown, l1g, l1b, l2g, l2b)


def _mixer_bwd_local(dr1b, p, ya, yb, conva, nb, rstdb, wo, woa, wob, lbg, lbb):
    s, d = ya.shape
    tm = min(s, 256)

    def body(dr1b_ref, ba_ref, ga_ref, gb_ref, ya_ref, yb_ref, conva_ref, nb_ref, rstdb_ref,
             wo_ref, woa_ref, wob_ref, lbg_ref, lbb_ref,
             dya_ref, dyb_ref, dba_ref, dga_ref, dgb_ref, dca_ref, du1_ref, acc_ref):
        i = pl.program_id(0)
        dmerged = _dot_nt(dr1b_ref[...], wo_ref[...])
        sa = _sig(ga_ref[...].astype(F32))
        sb = _sig(gb_ref[...].astype(F32))
        dya = (dmerged * sa).astype(BF)
        dyb = (dmerged * sb).astype(BF)
        dya_ref[...] = dya
        dyb_ref[...] = dyb
        dga_ref[...] = (dmerged * ya_ref[...] * (sa * (1.0 - sa))).astype(BF)
        dgb_ref[...] = (dmerged * yb_ref[...] * (sb * (1.0 - sb))).astype(BF)
        dyapre = _dot_nt(dya, woa_ref[...])
        dba_ref[...] = (dyapre * conva_ref[...]).astype(BF)
        dca_ref[...] = dyapre * ba_ref[...].astype(F32)
        du3 = _dot_nt(dyb, wob_ref[...])
        nbv = nb_ref[...]
        u2 = nbv * lbg_ref[...] + lbb_ref[...]
        sg = _sig(u2)
        du2 = du3 * (sg * (1.0 + u2 * (1.0 - sg)))
        du1 = _ln_bwd(du2 * lbg_ref[...], nbv, rstdb_ref[...])
        du1_ref[...] = du1

        @pl.when(i == 0)
        def _():
            acc_ref[...] = jnp.zeros_like(acc_ref)

        for q, val in enumerate((du2 * nbv, du2, du1)):
            acc_ref[pl.ds(q, 1), :] += jnp.sum(val, axis=0, keepdims=True)

    row = pl.BlockSpec((tm, d), lambda i: (i, 0))
    vec = _const((1, d))
    return pl.pallas_call(
        body, name="mixer_bwd_local", grid=(s // tm,),
        in_specs=[row, _col_spec(tm, d, 0), _col_spec(tm, d, 5), _col_spec(tm, d, 6), row, row, row, row,
                  pl.BlockSpec((tm, 1), lambda i: (i, 0)), _const((d, d)), _const((d, d)), _const((d, d)), vec, vec],
        out_specs=[row, row, row, row, row, row, row, pl.BlockSpec((8, d), lambda i: (0, 0))],
        out_shape=[SDS((s, d), BF)] * 5 + [SDS((s, d), F32), SDS((s, d), F32), SDS((8, d), F32)],
        compiler_params=_cp(("arbitrary",)),
    )(dr1b, p, p, p, ya, yb, conva, nb, rstdb, wo, woa, wob, lbg, lbb)


def _conv_bwd(dca, du1, p, dba, dga, dgb, cwa, cwb, d):
    s = dca.shape[0]
    tm = min(s, 256)
    nt = s // tm

    def body(dca_ref, du1_ref, ndca_ref, ndu1_ref, ca_ref, va_ref, vb_ref, gb_ref,
             hca_ref, hva_ref, hvb_ref, hgb_ref, dba_ref, dga_ref, dgb_ref, cwa_ref, cwb_ref,
             dp_ref, dcwa_ref, dcwb_ref,
             dcabuf, du1buf, zbuf, ubuf, sgbuf, acca, accb):
        i = pl.program_id(0)
        keep_prev = (i > 0).astype(F32)
        keep_next = (i < nt - 1).astype(F32)

        @pl.when(i == 0)
        def _():
            acca[...] = jnp.zeros_like(acca)
            accb[...] = jnp.zeros_like(accb)

        sg = _sig(gb_ref[...].astype(F32))
        sgbuf[...] = sg
        dcabuf[pl.ds(0, tm), :] = dca_ref[...]
        dcabuf[pl.ds(tm, HALO), :] = ndca_ref[...] * keep_next
        du1buf[pl.ds(0, tm), :] = du1_ref[...]
        du1buf[pl.ds(tm, HALO), :] = ndu1_ref[...] * keep_next
        zbuf[pl.ds(0, HALO), :] = hca_ref[...].astype(F32) * hva_ref[...].astype(F32) * keep_prev
        ubuf[pl.ds(0, HALO), :] = hvb_ref[...].astype(F32) * _sig(hgb_ref[...].astype(F32)) * keep_prev
        zbuf[pl.ds(HALO, tm), :] = ca_ref[...].astype(F32) * va_ref[...].astype(F32)
        ubuf[pl.ds(HALO, tm), :] = vb_ref[...].astype(F32) * sg
        dp_ref[:, pl.ds(0, d)] = dba_ref[...]
        dp_ref[:, pl.ds(5 * d, d)] = dga_ref[...]
        dp_ref[:, pl.ds(6 * d, d)] = dgb_ref[...]

        def chunk(j, carry):
            r0 = pl.multiple_of(j * CONV_ROWS, CONV_ROWS)
            rows = pl.ds(r0, CONV_ROWS)
            for lc in range(d // CONV_LANES):
                lo = lc * CONV_LANES
                ls = pl.ds(lo, CONV_LANES)
                dchunk = dcabuf[rows, ls]
                acc = jnp.zeros((CONV_ROWS, CONV_LANES), F32)
                for k, sl in _taps(dcabuf[pl.ds(r0, CONV_ROWS + HALO), ls], ANTI_A, CONV_ROWS):
                    acc = acc + cwa_ref[pl.ds(k, 1), ls] * sl
                for k, sl in _taps(zbuf[pl.ds(r0, CONV_ROWS + HALO), ls], CAUSAL_A, CONV_ROWS):
                    acca[pl.ds(8 * k, 8), ls] += _rowsum8(dchunk * sl)
                dp_ref[rows, pl.ds(d + lo, CONV_LANES)] = (acc * va_ref[rows, ls].astype(F32)).astype(BF)
                dp_ref[rows, pl.ds(2 * d + lo, CONV_LANES)] = (acc * ca_ref[rows, ls].astype(F32)).astype(BF)
                dchunk = du1buf[rows, ls]
                acc = jnp.zeros((CONV_ROWS, CONV_LANES), F32)
                for k, sl in _taps(du1buf[pl.ds(r0, CONV_ROWS + HALO), ls], ANTI_B, CONV_ROWS):
                    acc = acc + cwb_ref[pl.ds(k, 1), ls] * sl
                for k, sl in _taps(ubuf[pl.ds(r0, CONV_ROWS + HALO), ls], CAUSAL_B, CONV_ROWS):
                    accb[pl.ds(8 * k, 8), ls] += _rowsum8(dchunk * sl)
                sgc = sgbuf[rows, ls]
                dp_ref[rows, pl.ds(3 * d + lo, CONV_LANES)] = (acc * sgc).astype(BF)
                dp_ref[rows, pl.ds(4 * d + lo, CONV_LANES)] = (
                    acc * vb_ref[rows, ls].astype(F32) * (sgc * (1.0 - sgc))).astype(BF)
            return carry

        lax.fori_loop(0, tm // CONV_ROWS, chunk, 0)

        @pl.when(i == nt - 1)
        def _():
            dcwa_ref[...] = jnp.zeros_like(dcwa_ref)
            dcwb_ref[...] = jnp.zeros_like(dcwb_ref)
            for k in range(K_A):
                dcwa_ref[pl.ds(k, 1), :] = jnp.sum(acca[pl.ds(8 * k, 8), :], axis=0, keepdims=True)
            for k in range(K_B):
                dcwb_ref[pl.ds(k, 1), :] = jnp.sum(accb[pl.ds(8 * k, 8), :], axis=0, keepdims=True)

    row = pl.BlockSpec((tm, d), lambda i: (i, 0))
    nxt = _next_halo_spec(tm, d, 0, s)
    return pl.pallas_call(
        body, name="conv_bwd", grid=(nt,),
        in_specs=[row, row, nxt, nxt] + [_col_spec(tm, d, k) for k in (1, 2, 3, 4)]
        + [_prev_halo_spec(tm, d, k) for k in (1, 2, 3, 4)] + [row, row, row, _const((K_A, d)), _const((K_B, d))],
        out_specs=[pl.BlockSpec((tm, 7 * d), lambda i: (i, 0)), pl.BlockSpec((8, d), lambda i: (0, 0)),
                   pl.BlockSpec((32, d), lambda i: (0, 0))],
        out_shape=[SDS((s, 7 * d), BF), SDS((8, d), F32), SDS((32, d), F32)],
        scratch_shapes=[pltpu.VMEM((tm + HALO, d), F32), pltpu.VMEM((tm + HALO, d), F32),
                        pltpu.VMEM((HALO + tm, d), F32), pltpu.VMEM((HALO + tm, d), F32),
                        pltpu.VMEM((tm, d), F32), pltpu.VMEM((8 * K_A, d), F32), pltpu.VMEM((8 * K_B, d), F32)],
        compiler_params=_cp(("arbitrary",)),
    )(dca, du1, dca, du1, p, p, p, p, p, p, p, p, dba, dga, dgb, cwa, cwb)


def _grad_w(a, b, name):
    s, m = a.shape
    n = b.shape[1]
    tm, tn, tk = _tile(m, 1024), _tile(n, 1024), _tile(s, 512)

    def body(a_ref, b_ref, o_ref):
        @pl.when(pl.program_id(2) == 0)
        def _():
            o_ref[...] = jnp.zeros_like(o_ref)

        o_ref[...] += _dot_tn(a_ref[...], b_ref[...])

    return pl.pallas_call(
        body, name=name, grid=(m // tm, n // tn, s // tk),
        in_specs=[pl.BlockSpec((tk, tm), lambda i, j, k: (k, i)), pl.BlockSpec((tk, tn), lambda i, j, k: (k, j))],
        out_specs=pl.BlockSpec((tm, tn), lambda i, j, k: (i, j)),
        out_shape=SDS((m, n), F32),
        compiler_params=_cp(("parallel", "parallel", "arbitrary")),
    )(a, b)


def _grad_x(dr1, dp, wi):
    s, d = dr1.shape
    n = wi.shape[1]
    tm, tk = min(s, 1024), d

    def body(dr1_ref, dp_ref, w_ref, o_ref):
        @pl.when(pl.program_id(1) == 0)
        def _():
            o_ref[...] = F32(ALPHA) * dr1_ref[...]

        o_ref[...] += _dot_nt(dp_ref[...], w_ref[...])

    return pl.pallas_call(
        body, name="grad_x", grid=(s // tm, n // tk),
        in_specs=[pl.BlockSpec((tm, d), lambda i, k: (i, 0)), pl.BlockSpec((tm, tk), lambda i, k: (i, k)),
                  pl.BlockSpec((d, tk), lambda i, k: (0, k))],
        out_specs=pl.BlockSpec((tm, d), lambda i, k: (i, 0)),
        out_shape=SDS((s, d), F32),
        compiler_params=_cp(("parallel", "arbitrary")),
    )(dr1, dp, wi)


def _adamw_math(w, g, m, v):
    m2 = ADAM_B1 * m + (1.0 - ADAM_B1) * g
    v2 = ADAM_B2 * v + (1.0 - ADAM_B2) * (g * g)
    m_hat = m2 / (1.0 - ADAM_B1 ** ADAM_STEP)
    v_hat = v2 / (1.0 - ADAM_B2 ** ADAM_STEP)
    delta = -ADAM_LR * (m_hat / (jnp.sqrt(v_hat) + ADAM_EPS) + ADAM_WD * w)
    return delta, m2, v2


def _adamw(w, g, m, v, name):
    r, c = w.shape
    tr = r if r <= 256 else 256

    def body(w_ref, g_ref, m_ref, v_ref, d_ref, m2_ref, v2_ref):
        delta, m2, v2 = _adamw_math(w_ref[...], g_ref[...], m_ref[...], v_ref[...])
        d_ref[...] = delta
        m2_ref[...] = m2
        v2_ref[...] = v2

    blk = pl.BlockSpec((tr, c), lambda i: (i, 0))
    return pl.pallas_call(
        body, name=name, grid=(r // tr,), in_specs=[blk] * 4, out_specs=[blk] * 3,
        out_shape=[SDS((r, c), F32)] * 3, compiler_params=_cp(("parallel",)),
    )(w, g, m, v)


def _sum_parts(parts, name):
    k, r, c = parts.shape

    def body(p_ref, o_ref):
        acc = p_ref[0]
        for q in range(1, k):
            acc = acc + p_ref[q]
        o_ref[...] = acc

    return pl.pallas_call(
        body, name=name, grid=(1,),
        in_specs=[pl.BlockSpec((k, r, c), lambda i: (0, 0, 0))],
        out_specs=pl.BlockSpec((r, c), lambda i: (0, 0)),
        out_shape=SDS((r, c), F32), compiler_params=_cp(("arbitrary",)),
    )(parts)


def _piece_shape(full_shape, axis):
    r, c = full_shape
    return (r // 2, c // 4) if axis == 1 else (r // 8, c)


def _piece_spec(full_shape, axis, tr, chip_of, half_of):
    hr, wc = _piece_shape(full_shape, axis)
    nb = hr // tr
    if axis == 1:
        return pl.BlockSpec((tr, wc), lambda *a: (half_of(*a) * nb + a[-2], chip_of(*a)))
    return pl.BlockSpec((tr, wc), lambda *a: ((2 * chip_of(*a) + half_of(*a)) * nb + a[-2], 0))


def _place_cast(w, axis, pos, name):
    r, c = w.shape
    tr = min(r, 256)
    nb = r // tr
    full = (r, 4 * c) if axis == 1 else (4 * r, c)
    out_map = (lambda i, pos: (i, pos[0])) if axis == 1 else (lambda i, pos: (pos[0] * nb + i, 0))

    def body(pos_ref, w_ref, o_ref):
        o_ref[...] = w_ref[...].astype(o_ref.dtype)

    gs = pltpu.PrefetchScalarGridSpec(
        num_scalar_prefetch=1, grid=(nb,),
        in_specs=[pl.BlockSpec((tr, c), lambda i, pos: (i, 0))], out_specs=pl.BlockSpec((tr, c), out_map))
    return pl.pallas_call(body, name=name, grid_spec=gs, out_shape=SDS(full, BF),
                          compiler_params=_cp(("arbitrary",)))(pos, w)


def _pair_add(g, land, axis, pos, name):
    hr, wc = _piece_shape(g.shape, axis)
    tr = min(hr, 256)

    def body(pos_ref, g_ref, l_ref, o_ref):
        o_ref[0] = (g_ref[...] + l_ref[0]).astype(BF)

    blk = pl.BlockSpec((1, tr, wc), lambda j, i, pos: (j, i, 0))
    gs = pltpu.PrefetchScalarGridSpec(
        num_scalar_prefetch=1, grid=(4, hr // tr),
        in_specs=[_piece_spec(g.shape, axis, tr, lambda j, i, pos: j, lambda j, i, pos: pos[1]), blk], out_specs=blk)
    return pl.pallas_call(body, name=name, grid_spec=gs, out_shape=SDS((4, hr, wc), BF),
                          compiler_params=_cp(("arbitrary", "arbitrary")))(pos, g, land)


def _chip_sum(g, land1, land2, axis, pos, name):
    hr, wc = _piece_shape(g.shape, axis)
    tr = min(hr, 256)
    nb = hr // tr

    def body(pos_ref, g_ref, l1_ref, l2_ref, o_ref):
        acc = g_ref[...] + l1_ref[0]
        for q in range(3):
            acc = acc + l2_ref[q].astype(F32)
        o_ref[...] = acc

    gs = pltpu.PrefetchScalarGridSpec(
        num_scalar_prefetch=1, grid=(nb,),
        in_specs=[_piece_spec(g.shape, axis, tr, lambda i, pos: pos[0], lambda i, pos: pos[1]),
                  pl.BlockSpec((1, tr, wc), lambda i, pos: (pos[0], i, 0)),
                  pl.BlockSpec((3, tr, wc), lambda i, pos: (0, i, 0))],
        out_specs=pl.BlockSpec((tr, wc), lambda i, pos: (pos[1] * nb + i, 0)))
    return pl.pallas_call(body, name=name, grid_spec=gs, out_shape=SDS((2 * hr, wc), F32),
                          compiler_params=_cp(("arbitrary",)))(pos, g, land1, land2)


ANY = pl.BlockSpec(memory_space=pl.ANY)
COMM = pltpu.CompilerParams(has_side_effects=True)


def _on_each_device(fn):
    x, y, c = lax.axis_index("x"), lax.axis_index("y"), lax.axis_index("c")
    for sx in (0, 1):
        for sy in (0, 1):
            for sc in (0, 1):
                @pl.when(jnp.logical_and(jnp.logical_and(x == sx, y == sy), c == sc))
                def _(sx=sx, sy=sy, sc=sc):
                    fn(sx, sy, sc)


def _remote(src, dst, send_sem, recv_sem, to):
    return pltpu.make_async_remote_copy(src_ref=src, dst_ref=dst, send_sem=send_sem, recv_sem=recv_sem,
                                        device_id=to, device_id_type=MESH)


def _piece_ref(ref, axis, j, h):
    r, c = ref.shape
    hr, wc = _piece_shape((r, c), axis)
    if axis == 1:
        return ref.at[pl.ds(h * hr, hr), pl.ds(j * wc, wc)]
    return ref.at[pl.ds((2 * j + h) * hr, hr), :]


def _gather_weights(fulls, axes):
    n = len(fulls)

    def body(*refs):
        outs = refs[n:2 * n]
        send_sems, recv_sems = refs[2 * n:]

        def dev(sx, sy, sc):
            j = 2 * sx + sy
            sib = (sx, sy, 1 - sc)
            sends = []
            for w in range(n):
                mine = _piece_ref(outs[w], axes[w], j, sc)
                for r, (fx, fy) in enumerate(CHIP_RELS):
                    k = 6 * w + r
                    cp = _remote(mine, mine, send_sems.at[k], recv_sems.at[k], (sx ^ fx, sy ^ fy, sc))
                    cp.start()
                    sends.append(cp)
            for w in range(n):
                for r, (fx, fy) in enumerate(CHIP_RELS):
                    k = 6 * w + r
                    got = _piece_ref(outs[w], axes[w], 2 * (sx ^ fx) + (sy ^ fy), sc)
                    _remote(got, got, send_sems.at[k], recv_sems.at[k], (sx ^ fx, sy ^ fy, sc)).wait_recv()
                    cp = _remote(got, got, send_sems.at[k + 3], recv_sems.at[k + 3], sib)
                    cp.start()
                    sends.append(cp)
            for w in range(n):
                for r, (fx, fy) in enumerate(CHIP_RELS):
                    k = 6 * w + 3 + r
                    got = _piece_ref(outs[w], axes[w], 2 * (sx ^ fx) + (sy ^ fy), 1 - sc)
                    _remote(got, got, send_sems.at[k], recv_sems.at[k], sib).wait_recv()
            for cp in sends:
                cp.wait_send()

        _on_each_device(dev)

    return pl.pallas_call(
        body, name="gather_weights", in_specs=[ANY] * n, out_specs=[ANY] * n,
        out_shape=[SDS(f.shape, f.dtype) for f in fulls], input_output_aliases={i: i for i in range(n)},
        scratch_shapes=[pltpu.SemaphoreType.DMA((6 * n,)), pltpu.SemaphoreType.DMA((6 * n,))],
        compiler_params=COMM,
    )(*fulls)


def _pair_exchange(grads, axes):
    n = len(grads)
    shapes = [SDS((4,) + _piece_shape(g.shape, a), F32) for g, a in zip(grads, axes)]

    def body(*refs):
        gs, land = refs[:n], refs[n:2 * n]
        send_sems, recv_sems = refs[2 * n:]

        def dev(sx, sy, sc):
            cps = []
            for w in range(n):
                for jj in range(4):
                    k = 4 * w + jj
                    cp = _remote(_piece_ref(gs[w], axes[w], jj, 1 - sc), land[w].at[jj],
                                 send_sems.at[k], recv_sems.at[k], (sx, sy, 1 - sc))
                    cp.start()
                    cps.append(cp)
            for cp in cps:
                cp.wait()

        _on_each_device(dev)

    return pl.pallas_call(
        body, name="pair_exchange", in_specs=[ANY] * n, out_specs=[ANY] * n, out_shape=shapes,
        scratch_shapes=[pltpu.SemaphoreType.DMA((4 * n,)), pltpu.SemaphoreType.DMA((4 * n,))],
        compiler_params=COMM,
    )(*grads)


def _chip_scatter(pieces):
    n = len(pieces)
    shapes = [SDS((3,) + p.shape[1:], p.dtype) for p in pieces]

    def body(*refs):
        ps, land = refs[:n], refs[n:2 * n]
        send_sems, recv_sems = refs[2 * n:]

        def dev(sx, sy, sc):
            cps = []
            for w in range(n):
                for r, (fx, fy) in enumerate(CHIP_RELS):
                    k = 3 * w + r
                    cp = _remote(ps[w].at[2 * (sx ^ fx) + (sy ^ fy)], land[w].at[r], send_sems.at[k], recv_sems.at[k],
                                 (sx ^ fx, sy ^ fy, sc))
                    cp.start()
                    cps.append(cp)
            for cp in cps:
                cp.wait()

        _on_each_device(dev)

    return pl.pallas_call(
        body, name="chip_scatter", in_specs=[ANY] * n, out_specs=[ANY] * n, out_shape=shapes,
        scratch_shapes=[pltpu.SemaphoreType.DMA((3 * n,)), pltpu.SemaphoreType.DMA((3 * n,))],
        compiler_params=COMM,
    )(*pieces)


def _pair_share(shards):
    n = len(shards)

    def body(*refs):
        outs = refs[n:2 * n]
        send_sems, recv_sems = refs[2 * n:]

        def dev(sx, sy, sc):
            cps = []
            for w in range(n):
                hr = shards[w].shape[0] // 2
                mine = outs[w].at[pl.ds(sc * hr, hr), :]
                cp = _remote(mine, mine, send_sems.at[w], recv_sems.at[w], (sx, sy, 1 - sc))
                cp.start()
                cps.append(cp)
            for w, cp in enumerate(cps):
                hr = shards[w].shape[0] // 2
                cp.wait_send()
                theirs = outs[w].at[pl.ds((1 - sc) * hr, hr), :]
                _remote(theirs, theirs, send_sems.at[w], recv_sems.at[w], (sx, sy, 1 - sc)).wait_recv()

        _on_each_device(dev)

    return pl.pallas_call(
        body, name="pair_share", in_specs=[ANY] * n, out_specs=[ANY] * n,
        out_shape=[SDS(g.shape, g.dtype) for g in shards], input_output_aliases={i: i for i in range(n)},
        scratch_shapes=[pltpu.SemaphoreType.DMA((n,)), pltpu.SemaphoreType.DMA((n,))],
        compiler_params=COMM,
    )(*shards)


def _gather_small(stack):
    def body(src, out, send_sems, recv_sems):
        def dev(sx, sy, sc):
            me = 4 * sx + 2 * sy + sc
            cps = []
            for k, (fx, fy, fc) in enumerate(DEV_RELS):
                cp = _remote(out.at[me], out.at[me], send_sems.at[k], recv_sems.at[k], (sx ^ fx, sy ^ fy, sc ^ fc))
                cp.start()
                cps.append(cp)
            for cp in cps:
                cp.wait()

        _on_each_device(dev)

    return pl.pallas_call(
        body, name="gather_small", in_specs=[ANY], out_specs=ANY, out_shape=SDS(stack.shape, stack.dtype),
        input_output_aliases={0: 0},
        scratch_shapes=[pltpu.SemaphoreType.DMA((7,)), pltpu.SemaphoreType.DMA((7,))],
        compiler_params=COMM,
    )(stack)


def kernel(x, w_in, conv_a_w, w_out_a, conv_b_w, conv_b_bias, ln_b_gamma, ln_b_beta, w_out_b, w_o, ln1_gamma, ln1_beta, w_up, w_down, ln2_gamma, ln2_beta, loss_target, m_w_in, m_conv_a_w, m_w_out_a, m_conv_b_w, m_conv_b_bias, m_ln_b_gamma, m_ln_b_beta, m_w_out_b, m_w_o, m_ln1_gamma, m_ln1_beta, m_w_up, m_w_down, m_ln2_gamma, m_ln2_beta, v_w_in, v_conv_a_w, v_w_out_a, v_conv_b_w, v_conv_b_bias, v_ln_b_gamma, v_ln_b_beta, v_w_out_b, v_w_o, v_ln1_gamma, v_ln1_beta, v_w_up, v_w_down, v_ln2_gamma, v_ln2_beta):
    s, d = x.shape[1], x.shape[2]
    xs = x.reshape(s, d)
    tgt = loss_target.reshape(s, d)
    dq = d // 4
    chip = 2 * lax.axis_index("x") + lax.axis_index("y")
    core = lax.axis_index("c")
    pos = jnp.stack([chip, core]).astype(jnp.int32)
    names = ("w_in", "w_out_a", "w_out_b", "w_o", "w_up", "w_down")
    axes = (1, 0, 0, 0, 1, 0)

    conv_pack = jnp.concatenate([jnp.pad(conv_a_w, ((0, 8 - K_A), (0, 0))), jnp.pad(conv_b_w, ((0, 32 - K_B), (0, 0))),
                                 jnp.zeros((8, dq), F32)], axis=0)
    conv_full = lax.dynamic_update_slice(jnp.zeros((conv_pack.shape[0], d), F32), conv_pack, (0, chip * dq))
    fulls = [_place_cast(w, a, pos, "place_" + nm)
             for w, a, nm in zip((w_in, w_out_a, w_out_b, w_o, w_up, w_down), axes, names)]
    wi, woa, wob, wo, wup, wdown, convs = _gather_weights(fulls + [conv_full], axes + (1,))
    cwa, cwb = convs[0:K_A], convs[8:8 + K_B]
    vec = lambda a: a.reshape(1, d)
    bias_b, lbg, lbb = vec(conv_b_bias), vec(ln_b_gamma), vec(ln_b_beta)
    l1g, l1b, l2g, l2b = vec(ln1_gamma), vec(ln1_beta), vec(ln2_gamma), vec(ln2_beta)

    p, xb = _in_proj(xs, wi)
    conva, yapre, nb, rstdb, u3 = _conv_fwd(p, cwa, cwb, bias_b, lbg, lbb, d)
    ya, yb, merged, n1, rstd1 = _mixer_out(yapre, u3, p, xs, woa, wob, wo, d)
    x1b, hb, dhpre, dr2b, dr1, dr1b, acc_mlp = _mlp(n1, rstd1, tgt, wup, wdown, l1g, l1b, l2g, l2b)
    dya, dyb, dba, dga, dgb, dca, du1, acc_mix = _mixer_bwd_local(dr1b, p, ya, yb, conva, nb, rstdb, wo, woa, wob,
                                                                   lbg, lbb)
    dp, dcwa, dcwb = _conv_bwd(dca, du1, p, dba, dga, dgb, cwa, cwb, d)
    grad_x = _grad_x(dr1, dp, wi)
    grads = [
        _grad_w(xb, dp, "grad_w_in"),
        _grad_w(yapre, dya, "grad_w_out_a"),
        _grad_w(u3, dyb, "grad_w_out_b"),
        _grad_w(merged, dr1b, "grad_w_o"),
        _grad_w(x1b, dhpre, "grad_w_up"),
        _grad_w(hb, dr2b, "grad_w_down"),
    ]

    land1 = _pair_exchange(grads, axes)
    pieces = [_pair_add(g, l, a, pos, "pair_add_" + nm) for g, l, a, nm in zip(grads, land1, axes, names)]
    land2 = _chip_scatter(pieces)
    halves = [_chip_sum(g, l1, l2, a, pos, "chip_sum_" + nm)
              for g, l1, l2, a, nm in zip(grads, land1, land2, axes, names)]
    g_in, g_oa, g_ob, g_o, g_up, g_down = _pair_share(halves)

    pack = jnp.concatenate([dcwa, dcwb, acc_mix, acc_mlp], axis=0)
    stack = lax.dynamic_update_slice(jnp.zeros((8,) + pack.shape, F32), pack[None], (2 * chip + core, 0, 0))
    small = _sum_parts(_gather_small(stack), "small_sum")
    g_ca = lax.dynamic_slice(small, (0, chip * dq), (K_A, dq))
    g_cb = lax.dynamic_slice(small, (8, chip * dq), (K_B, dq))
    g_vec = jnp.stack([small[r] for r in (42, 40, 41, 51, 52, 49, 50)])

    loss = lax.psum((0.5 / d) * jnp.sum(acc_mlp[0]), ("x", "y", "c"))

    big = {}
    for name, w, g, m, v in (("w_in", w_in, g_in, m_w_in, v_w_in), ("w_out_a", w_out_a, g_oa, m_w_out_a, v_w_out_a),
                             ("w_out_b", w_out_b, g_ob, m_w_out_b, v_w_out_b), ("w_o", w_o, g_o, m_w_o, v_w_o),
                             ("w_up", w_up, g_up, m_w_up, v_w_up), ("w_down", w_down, g_down, m_w_down, v_w_down),
                             ("conv_a_w", conv_a_w, g_ca, m_conv_a_w, v_conv_a_w),
                             ("conv_b_w", conv_b_w, g_cb, m_conv_b_w, v_conv_b_w)):
        big[name] = (g,) + tuple(_adamw(w, g, m, v, "adamw_" + name))
    vec_names = ("conv_b_bias", "ln_b_gamma", "ln_b_beta", "ln1_gamma", "ln1_beta", "ln2_gamma", "ln2_beta")
    w7 = jnp.stack([conv_b_bias, ln_b_gamma, ln_b_beta, ln1_gamma, ln1_beta, ln2_gamma, ln2_beta])
    m7 = jnp.stack([m_conv_b_bias, m_ln_b_gamma, m_ln_b_beta, m_ln1_gamma, m_ln1_beta, m_ln2_gamma, m_ln2_beta])
    v7 = jnp.stack([v_conv_b_bias, v_ln_b_gamma, v_ln_b_beta, v_ln1_gamma, v_ln1_beta, v_ln2_gamma, v_ln2_beta])
    d7, nm7, nv7 = _adamw(w7, g_vec, m7, v7, "adamw_vectors")
    for q, name in enumerate(vec_names):
        big[name] = (g_vec[q], d7[q], nm7[q], nv7[q])

    order = ("w_in", "conv_a_w", "w_out_a", "conv_b_w", "conv_b_bias", "ln_b_gamma", "ln_b_beta", "w_out_b", "w_o",
             "ln1_gamma", "ln1_beta", "w_up", "w_down", "ln2_gamma", "ln2_beta")
    outs = [loss, grad_x.reshape(x.shape)]
    for part in range(4):
        outs += [big[name][part] for name in order]
    return tuple(outs)
```

```python
import jax
import jax.numpy as jnp
from jax import lax
from jax.experimental import pallas as pl
from jax.experimental.pallas import tpu as pltpu

F32 = jnp.float32
BF = jnp.bfloat16
SDS = jax.ShapeDtypeStruct
MESH = pl.DeviceIdType.MESH

ALPHA = 2.0 ** 0.25
LN_EPS = 1e-5
K_A = 3
K_B = 31
HALO = 32
CONV_ROWS = 64
CONV_LANES = 128
ADAM_LR = 0.001
ADAM_B1 = 0.9
ADAM_B2 = 0.999
ADAM_EPS = 1e-08
ADAM_WD = 0.01
ADAM_STEP = 10
P_DT = F32
CHIP_RELS = ((1, 0), (0, 1), (1, 1))
DEV_RELS = tuple((fx, fy, fc) for fx in (0, 1) for fy in (0, 1) for fc in (0, 1))[1:]


def _cp(sem=None, vmem_mb=56, side_effects=False):
    return pltpu.CompilerParams(dimension_semantics=sem, vmem_limit_bytes=vmem_mb << 20,
                                has_side_effects=side_effects)


def _const(shape):
    return pl.BlockSpec(shape, lambda *_: (0,) * len(shape), pipeline_mode=pl.Buffered(1))


def _sig(v):
    return jax.nn.sigmoid(v)


def _ln_fwd(r):
    mu = jnp.mean(r, axis=-1, keepdims=True)
    xc = r - mu
    var = jnp.mean(xc * xc, axis=-1, keepdims=True)
    rstd = lax.rsqrt(var + LN_EPS)
    return xc * rstd, rstd


def _ln_bwd(dn, n, rstd):
    m1 = jnp.mean(dn, axis=-1, keepdims=True)
    m2 = jnp.mean(dn * n, axis=-1, keepdims=True)
    return rstd * (dn - m1 - n * m2)


def _dot(a, b):
    return jnp.dot(a, b, preferred_element_type=F32)


def _dot_nt(a, b):
    return lax.dot_general(a, b, (((1,), (1,)), ((), ())), preferred_element_type=F32)


def _dot_tn(a, b):
    return lax.dot_general(a, b, (((0,), (0,)), ((), ())), preferred_element_type=F32)


def _tile(n, pref):
    if n <= pref:
        return n
    return max(t for t in range(128, pref + 1, 128) if n % t == 0)


def _rowsum8(v):
    acc = v[0:8]
    for g in range(1, v.shape[0] // 8):
        acc = acc + v[8 * g:8 * g + 8]
    return acc


def _taps(win, offsets, rows):
    r_all = win.shape[0]
    by_res = {}
    for k, o in enumerate(offsets):
        by_res.setdefault(o % 8, []).append((k, o // 8))
    for s, taps in sorted(by_res.items()):
        r = win if s == 0 else pltpu.roll(win, r_all - s, 0)
        for k, q in taps:
            yield k, r[8 * q:8 * q + rows]


CAUSAL_A = [HALO - (K_A - 1) + k for k in range(K_A)]
CAUSAL_B = [HALO - (K_B - 1) + k for k in range(K_B)]
ANTI_A = [K_A - 1 - k for k in range(K_A)]
ANTI_B = [K_B - 1 - k for k in range(K_B)]


def _host_call(body, *, name, grid, in_specs, out_specs, out_shape, scratch_shapes, args, hosted):
    n_in, n_out, n_scr = len(in_specs), len(out_specs), len(scratch_shapes)
    sem = ("arbitrary",) * len(grid)
    if hosted is None:
        res = pl.pallas_call(body, name=name, grid=grid, in_specs=in_specs, out_specs=out_specs, out_shape=out_shape,
                             scratch_shapes=scratch_shapes, compiler_params=_cp(sem))(*args)
        return list(res), []
    h_in, h_out = len(hosted.ins), len(hosted.outs)

    def full_body(*refs):
        ins, refs = refs[:n_in], refs[n_in:]
        hins, refs = refs[:h_in], refs[h_in:]
        outs, refs = refs[:n_out], refs[n_out:]
        houts, refs = refs[:h_out], refs[h_out:]
        scr, (send_sems, recv_sems) = refs[:n_scr], refs[n_scr:]
        plan = hosted.make_plan(hins, houts, send_sems, recv_sems)
        first = last = None
        for a, size in enumerate(grid):
            at0, at1 = pl.program_id(a) == 0, pl.program_id(a) == size - 1
            first = at0 if first is None else jnp.logical_and(first, at0)
            last = at1 if last is None else jnp.logical_and(last, at1)

        @pl.when(first)
        def _():
            _start(plan)

        body(*ins, *outs, *scr)

        @pl.when(last)
        def _():
            _finish(plan)

    res = pl.pallas_call(
        full_body, name=name, grid=grid, in_specs=list(in_specs) + [ANY] * h_in,
        out_specs=list(out_specs) + [ANY] * h_out, out_shape=list(out_shape) + list(hosted.outs),
        scratch_shapes=list(scratch_shapes) + [pltpu.SemaphoreType.DMA((hosted.n_sems,)),
                                               pltpu.SemaphoreType.DMA((hosted.n_sems,))],
        input_output_aliases={n_in + a: n_out + b for a, b in hosted.aliases.items()},
        compiler_params=_cp(sem, side_effects=True),
    )(*args, *hosted.ins)
    return list(res[:n_out]), list(res[n_out:])


def _in_proj(x, wi, hosted=None):
    s, d = x.shape
    n = wi.shape[1]
    tm, tn = min(s, 1024), d

    def body(x_ref, w_ref, p_ref, xb_ref):
        @pl.when(pl.program_id(1) == 0)
        def _():
            xb_ref[...] = x_ref[...].astype(BF)

        p_ref[...] = _dot(xb_ref[...], w_ref[...]).astype(p_ref.dtype)

    return _host_call(
        body, name="in_proj", grid=(s // tm, n // tn),
        in_specs=[pl.BlockSpec((tm, d), lambda i, j: (i, 0)), pl.BlockSpec((d, tn), lambda i, j: (0, j))],
        out_specs=[pl.BlockSpec((tm, tn), lambda i, j: (i, j)), pl.BlockSpec((tm, d), lambda i, j: (i, 0))],
        out_shape=[SDS((s, n), P_DT), SDS((s, d), BF)], scratch_shapes=[], args=(x, wi), hosted=hosted)


def _col_spec(tm, d, k):
    return pl.BlockSpec((tm, d), lambda i, k=k: (i, k))


def _prev_halo_spec(tm, d, k):
    r = tm // HALO
    return pl.BlockSpec((HALO, d), lambda i, k=k: (jnp.maximum(i * r - 1, 0), k))


def _next_halo_spec(tm, d, k, s):
    r = tm // HALO
    last = s // HALO - 1
    return pl.BlockSpec((HALO, d), lambda i, k=k: (jnp.minimum((i + 1) * r, last), k))


def _conv_fwd(p, cwa, cwb, bias_b, lbg, lbb, d, hosted=None):
    s = p.shape[0]
    tm = min(s, 256)
    nt = s // tm

    def body(ba_ref, ca_ref, va_ref, vb_ref, gb_ref, hca_ref, hva_ref, hvb_ref, hgb_ref,
             cwa_ref, cwb_ref, bias_ref, lbg_ref, lbb_ref,
             conva_ref, yapre_ref, nb_ref, rstdb_ref, u3_ref,
             zbuf, ubuf, u1buf):
        i = pl.program_id(0)
        keep = (i > 0).astype(F32)
        zbuf[pl.ds(0, HALO), :] = hca_ref[...].astype(F32) * hva_ref[...].astype(F32) * keep
        ubuf[pl.ds(0, HALO), :] = hvb_ref[...].astype(F32) * _sig(hgb_ref[...].astype(F32)) * keep
        zbuf[pl.ds(HALO, tm), :] = ca_ref[...].astype(F32) * va_ref[...].astype(F32)
        ubuf[pl.ds(HALO, tm), :] = vb_ref[...].astype(F32) * _sig(gb_ref[...].astype(F32))

        def chunk(j, carry):
            r0 = pl.multiple_of(j * CONV_ROWS, CONV_ROWS)
            rows = pl.ds(r0, CONV_ROWS)
            for lc in range(d // CONV_LANES):
                ls = pl.ds(lc * CONV_LANES, CONV_LANES)
                acc = jnp.zeros((CONV_ROWS, CONV_LANES), F32)
                for k, sl in _taps(zbuf[pl.ds(r0, CONV_ROWS + HALO), ls], CAUSAL_A, CONV_ROWS):
                    acc = acc + cwa_ref[pl.ds(k, 1), ls] * sl
                conva_ref[rows, ls] = acc
                yapre_ref[rows, ls] = (ba_ref[rows, ls].astype(F32) * acc).astype(BF)
                acc = jnp.zeros((CONV_ROWS, CONV_LANES), F32)
                for k, sl in _taps(ubuf[pl.ds(r0, CONV_ROWS + HALO), ls], CAUSAL_B, CONV_ROWS):
                    acc = acc + cwb_ref[pl.ds(k, 1), ls] * sl
                u1buf[rows, ls] = acc + bias_ref[:, ls]
            return carry

        lax.fori_loop(0, tm // CONV_ROWS, chunk, 0)
        nb, rstd = _ln_fwd(u1buf[...])
        nb_ref[...] = nb
        rstdb_ref[...] = rstd
        u2 = nb * lbg_ref[...] + lbb_ref[...]
        u3_ref[...] = (u2 * _sig(u2)).astype(BF)

    vec = _const((1, d))
    return _host_call(
        body, name="conv_fwd", grid=(nt,),
        in_specs=[_col_spec(tm, d, k) for k in range(5)] + [_prev_halo_spec(tm, d, k) for k in (1, 2, 3, 4)]
        + [_const((K_A, d)), _const((K_B, d)), vec, vec, vec],
        out_specs=[pl.BlockSpec((tm, d), lambda i: (i, 0)), pl.BlockSpec((tm, d), lambda i: (i, 0)),
                   pl.BlockSpec((tm, d), lambda i: (i, 0)), pl.BlockSpec((tm, 1), lambda i: (i, 0)),
                   pl.BlockSpec((tm, d), lambda i: (i, 0))],
        out_shape=[SDS((s, d), F32), SDS((s, d), BF), SDS((s, d), F32), SDS((s, 1), F32), SDS((s, d), BF)],
        scratch_shapes=[pltpu.VMEM((HALO + tm, d), F32), pltpu.VMEM((HALO + tm, d), F32), pltpu.VMEM((tm, d), F32)],
        args=(p, p, p, p, p, p, p, p, p, cwa, cwb, bias_b, lbg, lbb), hosted=hosted)


def _mixer_out(yapre, u3, p, x, woa, wob, wo, d):
    s = x.shape[0]
    tm = min(s, 512)

    def body(yapre_ref, u3_ref, ga_ref, gb_ref, x_ref, woa_ref, wob_ref, wo_ref,
             ya_ref, yb_ref, merged_ref, n1_ref, rstd1_ref):
        ya = _dot(yapre_ref[...], woa_ref[...])
        yb = _dot(u3_ref[...], wob_ref[...])
        ya_ref[...] = ya
        yb_ref[...] = yb
        merged = (_sig(ga_ref[...].astype(F32)) * ya + _sig(gb_ref[...].astype(F32)) * yb).astype(BF)
        merged_ref[...] = merged
        r1 = F32(ALPHA) * x_ref[...] + _dot(merged, wo_ref[...])
        n1, rstd1 = _ln_fwd(r1)
        n1_ref[...] = n1
        rstd1_ref[...] = rstd1

    row = pl.BlockSpec((tm, d), lambda i: (i, 0))
    return pl.pallas_call(
        body, name="mixer_out", grid=(s // tm,),
        in_specs=[row, row, _col_spec(tm, d, 5), _col_spec(tm, d, 6), row,
                  _const((d, d)), _const((d, d)), _const((d, d))],
        out_specs=[row, row, row, row, pl.BlockSpec((tm, 1), lambda i: (i, 0))],
        out_shape=[SDS((s, d), F32), SDS((s, d), F32), SDS((s, d), BF), SDS((s, d), F32), SDS((s, 1), F32)],
        compiler_params=_cp(("parallel",)),
    )(yapre, u3, p, p, x, woa, wob, wo)


def _mlp(n1, rstd1, tgt, wup, wdown, l1g, l1b, l2g, l2b):
    s, d = n1.shape
    dff = wup.shape[1]
    tm = min(s, 256)
    fc = min(dff, 1024)
    nq = dff // fc

    def body(n1_ref, rstd1_ref, tgt_ref, wup_ref, wdown_ref, l1g_ref, l1b_ref, l2g_ref, l2b_ref,
             x1b_ref, hb_ref, dhpre_ref, dr2b_ref, dr1_ref, dr1b_ref, acc_ref, rbuf):
        i = pl.program_id(0)
        n1v = n1_ref[...]
        x1 = n1v * l1g_ref[...] + l1b_ref[...]
        x1b = x1.astype(BF)
        x1b_ref[...] = x1b
        ff = jnp.zeros((tm, d), F32)
        for q in range(nq):
            cs = pl.ds(q * fc, fc)
            r = jnp.maximum(_dot(x1b, wup_ref[:, cs]), 0.0)
            rbuf[:, cs] = r
            hq = (r * r).astype(BF)
            hb_ref[:, cs] = hq
            ff = ff + _dot(hq, wdown_ref[cs, :])
        n2, rstd2 = _ln_fwd(F32(ALPHA) * x1 + ff)
        x2 = n2 * l2g_ref[...] + l2b_ref[...]
        err = x2 - tgt_ref[...]
        dx2 = err * F32(1.0 / d)
        dr2 = _ln_bwd(dx2 * l2g_ref[...], n2, rstd2)
        dr2b = dr2.astype(BF)
        dr2b_ref[...] = dr2b
        dx1 = F32(ALPHA) * dr2
        for q in range(nq):
            cs = pl.ds(q * fc, fc)
            dh = _dot_nt(dr2b, wdown_ref[cs, :])
            dhp = (dh * (2.0 * rbuf[:, cs])).astype(BF)
            dhpre_ref[:, cs] = dhp
            dx1 = dx1 + _dot_nt(dhp, wup_ref[:, cs])
        dr1 = _ln_bwd(dx1 * l1g_ref[...], n1v, rstd1_ref[...])
        dr1_ref[...] = dr1
        dr1b_ref[...] = dr1.astype(BF)

        @pl.when(i == 0)
        def _():
            acc_ref[...] = jnp.zeros_like(acc_ref)

        for q, val in enumerate((err * err, dx2 * n2, dx2, dx1 * n1v, dx1)):
            acc_ref[pl.ds(q, 1), :] += jnp.sum(val, axis=0, keepdims=True)

    row = pl.BlockSpec((tm, d), lambda i: (i, 0))
    wide = pl.BlockSpec((tm, dff), lambda i: (i, 0))
    vec = _const((1, d))
    return pl.pallas_call(
        body, name="mlp_fwd_bwd", grid=(s // tm,),
        in_specs=[row, pl.BlockSpec((tm, 1), lambda i: (i, 0)), row, _const((d, dff)), _const((dff, d)),
                  vec, vec, vec, vec],
        out_specs=[row, wide, wide, row, row, row, pl.BlockSpec((8, d), lambda i: (0, 0))],
        out_shape=[SDS((s, d), BF), SDS((s, dff), BF), SDS((s, dff), BF), SDS((s, d), BF), SDS((s, d), F32),
                   SDS((s, d), BF), SDS((8, d), F32)],
        scratch_shapes=[pltpu.VMEM((tm, dff), F32)],
        compiler_params=_cp(("arbitrary",)),
    )(n1, rstd1, tgt, wup, wdown, l1g, l1b, l2g, l2b)


def _mixer_bwd_local(dr1b, p, ya, yb, conva, nb, rstdb, wo, woa, wob, lbg, lbb):
    s, d = ya.shape
    tm = min(s, 256)

    def body(dr1b_ref, ba_ref, ga_ref, gb_ref, ya_ref, yb_ref, conva_ref, nb_ref, rstdb_ref,
             wo_ref, woa_ref, wob_ref, lbg_ref, lbb_ref,
             dya_ref, dyb_ref, dba_ref, dga_ref, dgb_ref, dca_ref, du1_ref, acc_ref):
        i = pl.program_id(0)
        dmerged = _dot_nt(dr1b_ref[...], wo_ref[...])
        sa = _sig(ga_ref[...].astype(F32))
        sb = _sig(gb_ref[...].astype(F32))
        dya = (dmerged * sa).astype(BF)
        dyb = (dmerged * sb).astype(BF)
        dya_ref[...] = dya
        dyb_ref[...] = dyb
        dga_ref[...] = (dmerged * ya_ref[...] * (sa * (1.0 - sa))).astype(BF)
        dgb_ref[...] = (dmerged * yb_ref[...] * (sb * (1.0 - sb))).astype(BF)
        dyapre = _dot_nt(dya, woa_ref[...])
        dba_ref[...] = (dyapre * conva_ref[...]).astype(BF)
        dca_ref[...] = dyapre * ba_ref[...].astype(F32)
        du3 = _dot_nt(dyb, wob_ref[...])
        nbv = nb_ref[...]
        u2 = nbv * lbg_ref[...] + lbb_ref[...]
        sg = _sig(u2)
        du2 = du3 * (sg * (1.0 + u2 * (1.0 - sg)))
        du1 = _ln_bwd(du2 * lbg_ref[...], nbv, rstdb_ref[...])
        du1_ref[...] = du1

        @pl.when(i == 0)
        def _():
            acc_ref[...] = jnp.zeros_like(acc_ref)

        for q, val in enumerate((du2 * nbv, du2, du1)):
            acc_ref[pl.ds(q, 1), :] += jnp.sum(val, axis=0, keepdims=True)

    row = pl.BlockSpec((tm, d), lambda i: (i, 0))
    vec = _const((1, d))
    return pl.pallas_call(
        body, name="mixer_bwd_local", grid=(s // tm,),
        in_specs=[row, _col_spec(tm, d, 0), _col_spec(tm, d, 5), _col_spec(tm, d, 6), row, row, row, row,
                  pl.BlockSpec((tm, 1), lambda i: (i, 0)), _const((d, d)), _const((d, d)), _const((d, d)), vec, vec],
        out_specs=[row, row, row, row, row, row, row, pl.BlockSpec((8, d), lambda i: (0, 0))],
        out_shape=[SDS((s, d), BF)] * 5 + [SDS((s, d), F32), SDS((s, d), F32), SDS((8, d), F32)],
        compiler_params=_cp(("arbitrary",)),
    )(dr1b, p, p, p, ya, yb, conva, nb, rstdb, wo, woa, wob, lbg, lbb)


def _conv_bwd(dca, du1, p, dba, dga, dgb, cwa, cwb, d, hosted=None):
    s = dca.shape[0]
    tm = min(s, 256)
    nt = s // tm

    def body(dca_ref, du1_ref, ndca_ref, ndu1_ref, ca_ref, va_ref, vb_ref, gb_ref,
             hca_ref, hva_ref, hvb_ref, hgb_ref, dba_ref, dga_ref, dgb_ref, cwa_ref, cwb_ref,
             dp_ref, dcwa_ref, dcwb_ref,
             dcabuf, du1buf, zbuf, ubuf, sgbuf, acca, accb):
        i = pl.program_id(0)
        keep_prev = (i > 0).astype(F32)
        keep_next = (i < nt - 1).astype(F32)

        @pl.when(i == 0)
        def _():
            acca[...] = jnp.zeros_like(acca)
            accb[...] = jnp.zeros_like(accb)

        sg = _sig(gb_ref[...].astype(F32))
        sgbuf[...] = sg
        dcabuf[pl.ds(0, tm), :] = dca_ref[...]
        dcabuf[pl.ds(tm, HALO), :] = ndca_ref[...] * keep_next
        du1buf[pl.ds(0, tm), :] = du1_ref[...]
        du1buf[pl.ds(tm, HALO), :] = ndu1_ref[...] * keep_next
        zbuf[pl.ds(0, HALO), :] = hca_ref[...].astype(F32) * hva_ref[...].astype(F32) * keep_prev
        ubuf[pl.ds(0, HALO), :] = hvb_ref[...].astype(F32) * _sig(hgb_ref[...].astype(F32)) * keep_prev
        zbuf[pl.ds(HALO, tm), :] = ca_ref[...].astype(F32) * va_ref[...].astype(F32)
        ubuf[pl.ds(HALO, tm), :] = vb_ref[...].astype(F32) * sg
        dp_ref[:, pl.ds(0, d)] = dba_ref[...]
        dp_ref[:, pl.ds(5 * d, d)] = dga_ref[...]
        dp_ref[:, pl.ds(6 * d, d)] = dgb_ref[...]

        def chunk(j, carry):
            r0 = pl.multiple_of(j * CONV_ROWS, CONV_ROWS)
            rows = pl.ds(r0, CONV_ROWS)
            for lc in range(d // CONV_LANES):
                lo = lc * CONV_LANES
                ls = pl.ds(lo, CONV_LANES)
                dchunk = dcabuf[rows, ls]
                acc = jnp.zeros((CONV_ROWS, CONV_LANES), F32)
                for k, sl in _taps(dcabuf[pl.ds(r0, CONV_ROWS + HALO), ls], ANTI_A, CONV_ROWS):
                    acc = acc + cwa_ref[pl.ds(k, 1), ls] * sl
                for k, sl in _taps(zbuf[pl.ds(r0, CONV_ROWS + HALO), ls], CAUSAL_A, CONV_ROWS):
                    acca[pl.ds(8 * k, 8), ls] += _rowsum8(dchunk * sl)
                dp_ref[rows, pl.ds(d + lo, CONV_LANES)] = (acc * va_ref[rows, ls].astype(F32)).astype(BF)
                dp_ref[rows, pl.ds(2 * d + lo, CONV_LANES)] = (acc * ca_ref[rows, ls].astype(F32)).astype(BF)
                dchunk = du1buf[rows, ls]
                acc = jnp.zeros((CONV_ROWS, CONV_LANES), F32)
                for k, sl in _taps(du1buf[pl.ds(r0, CONV_ROWS + HALO), ls], ANTI_B, CONV_ROWS):
                    acc = acc + cwb_ref[pl.ds(k, 1), ls] * sl
                for k, sl in _taps(ubuf[pl.ds(r0, CONV_ROWS + HALO), ls], CAUSAL_B, CONV_ROWS):
                    accb[pl.ds(8 * k, 8), ls] += _rowsum8(dchunk * sl)
                sgc = sgbuf[rows, ls]
                dp_ref[rows, pl.ds(3 * d + lo, CONV_LANES)] = (acc * sgc).astype(BF)
                dp_ref[rows, pl.ds(4 * d + lo, CONV_LANES)] = (
                    acc * vb_ref[rows, ls].astype(F32) * (sgc * (1.0 - sgc))).astype(BF)
            return carry

        lax.fori_loop(0, tm // CONV_ROWS, chunk, 0)

        @pl.when(i == nt - 1)
        def _():
            dcwa_ref[...] = jnp.zeros_like(dcwa_ref)
            dcwb_ref[...] = jnp.zeros_like(dcwb_ref)
            for k in range(K_A):
                dcwa_ref[pl.ds(k, 1), :] = jnp.sum(acca[pl.ds(8 * k, 8), :], axis=0, keepdims=True)
            for k in range(K_B):
                dcwb_ref[pl.ds(k, 1), :] = jnp.sum(accb[pl.ds(8 * k, 8), :], axis=0, keepdims=True)

    row = pl.BlockSpec((tm, d), lambda i: (i, 0))
    nxt = _next_halo_spec(tm, d, 0, s)
    return _host_call(
        body, name="conv_bwd", grid=(nt,),
        in_specs=[row, row, nxt, nxt] + [_col_spec(tm, d, k) for k in (1, 2, 3, 4)]
        + [_prev_halo_spec(tm, d, k) for k in (1, 2, 3, 4)] + [row, row, row, _const((K_A, d)), _const((K_B, d))],
        out_specs=[pl.BlockSpec((tm, 7 * d), lambda i: (i, 0)), pl.BlockSpec((8, d), lambda i: (0, 0)),
                   pl.BlockSpec((32, d), lambda i: (0, 0))],
        out_shape=[SDS((s, 7 * d), BF), SDS((8, d), F32), SDS((32, d), F32)],
        scratch_shapes=[pltpu.VMEM((tm + HALO, d), F32), pltpu.VMEM((tm + HALO, d), F32),
                        pltpu.VMEM((HALO + tm, d), F32), pltpu.VMEM((HALO + tm, d), F32),
                        pltpu.VMEM((tm, d), F32), pltpu.VMEM((8 * K_A, d), F32), pltpu.VMEM((8 * K_B, d), F32)],
        args=(dca, du1, dca, du1, p, p, p, p, p, p, p, p, dba, dga, dgb, cwa, cwb), hosted=hosted)


def _grad_w(a, b, name, hosted=None):
    s, m = a.shape
    n = b.shape[1]
    tm, tn, tk = _tile(m, 1024), _tile(n, 1024), _tile(s, 512)

    def body(a_ref, b_ref, o_ref):
        @pl.when(pl.program_id(2) == 0)
        def _():
            o_ref[...] = jnp.zeros_like(o_ref)

        o_ref[...] += _dot_tn(a_ref[...], b_ref[...])

    (g,), extra = _host_call(
        body, name=name, grid=(m // tm, n // tn, s // tk),
        in_specs=[pl.BlockSpec((tk, tm), lambda i, j, k: (k, i)), pl.BlockSpec((tk, tn), lambda i, j, k: (k, j))],
        out_specs=[pl.BlockSpec((tm, tn), lambda i, j, k: (i, j))],
        out_shape=[SDS((m, n), F32)], scratch_shapes=[], args=(a, b), hosted=hosted)
    return g, extra


def _grad_x(dr1, dp, wi):
    s, d = dr1.shape
    n = wi.shape[1]
    tm, tk = min(s, 1024), d

    def body(dr1_ref, dp_ref, w_ref, o_ref):
        @pl.when(pl.program_id(1) == 0)
        def _():
            o_ref[...] = F32(ALPHA) * dr1_ref[...]

        o_ref[...] += _dot_nt(dp_ref[...], w_ref[...])

    return pl.pallas_call(
        body, name="grad_x", grid=(s // tm, n // tk),
        in_specs=[pl.BlockSpec((tm, d), lambda i, k: (i, 0)), pl.BlockSpec((tm, tk), lambda i, k: (i, k)),
                  pl.BlockSpec((d, tk), lambda i, k: (0, k))],
        out_specs=pl.BlockSpec((tm, d), lambda i, k: (i, 0)),
        out_shape=SDS((s, d), F32),
        compiler_params=_cp(("parallel", "arbitrary")),
    )(dr1, dp, wi)


def _adamw_math(w, g, m, v):
    m2 = ADAM_B1 * m + (1.0 - ADAM_B1) * g
    v2 = ADAM_B2 * v + (1.0 - ADAM_B2) * (g * g)
    m_hat = m2 / (1.0 - ADAM_B1 ** ADAM_STEP)
    v_hat = v2 / (1.0 - ADAM_B2 ** ADAM_STEP)
    delta = -ADAM_LR * (m_hat / (jnp.sqrt(v_hat) + ADAM_EPS) + ADAM_WD * w)
    return delta, m2, v2


def _adamw(w, g, m, v, name):
    r, c = w.shape
    tr = r if r <= 256 else 256

    def body(w_ref, g_ref, m_ref, v_ref, d_ref, m2_ref, v2_ref):
        delta, m2, v2 = _adamw_math(w_ref[...], g_ref[...], m_ref[...], v_ref[...])
        d_ref[...] = delta
        m2_ref[...] = m2
        v2_ref[...] = v2

    blk = pl.BlockSpec((tr, c), lambda i: (i, 0))
    return pl.pallas_call(
        body, name=name, grid=(r // tr,), in_specs=[blk] * 4, out_specs=[blk] * 3,
        out_shape=[SDS((r, c), F32)] * 3, compiler_params=_cp(("parallel",)),
    )(w, g, m, v)


def _sum_parts(parts, name):
    k, r, c = parts.shape

    def body(p_ref, o_ref):
        acc = p_ref[0]
        for q in range(1, k):
            acc = acc + p_ref[q]
        o_ref[...] = acc

    return pl.pallas_call(
        body, name=name, grid=(1,),
        in_specs=[pl.BlockSpec((k, r, c), lambda i: (0, 0, 0))],
        out_specs=pl.BlockSpec((r, c), lambda i: (0, 0)),
        out_shape=SDS((r, c), F32), compiler_params=_cp(("arbitrary",)),
    )(parts)


def _piece_shape(full_shape, axis):
    r, c = full_shape
    return (r // 2, c // 4) if axis == 1 else (r // 8, c)


def _piece_spec(full_shape, axis, tr, chip_of, half_of):
    hr, wc = _piece_shape(full_shape, axis)
    nb = hr // tr
    if axis == 1:
        return pl.BlockSpec((tr, wc), lambda *a: (half_of(*a) * nb + a[-2], chip_of(*a)))
    return pl.BlockSpec((tr, wc), lambda *a: ((2 * chip_of(*a) + half_of(*a)) * nb + a[-2], 0))


def _place_cast(w, axis, pos, name):
    r, c = w.shape
    tr = min(r, 256)
    nb = r // tr
    full = (r, 4 * c) if axis == 1 else (4 * r, c)
    out_map = (lambda i, pos: (i, pos[0])) if axis == 1 else (lambda i, pos: (pos[0] * nb + i, 0))

    def body(pos_ref, w_ref, o_ref):
        o_ref[...] = w_ref[...].astype(o_ref.dtype)

    gs = pltpu.PrefetchScalarGridSpec(
        num_scalar_prefetch=1, grid=(nb,),
        in_specs=[pl.BlockSpec((tr, c), lambda i, pos: (i, 0))], out_specs=pl.BlockSpec((tr, c), out_map))
    return pl.pallas_call(body, name=name, grid_spec=gs, out_shape=SDS(full, BF),
                          compiler_params=_cp(("arbitrary",)))(pos, w)


def _pair_add(g, land, axis, pos, name):
    hr, wc = _piece_shape(g.shape, axis)
    tr = min(hr, 256)

    def body(pos_ref, g_ref, l_ref, o_ref):
        o_ref[0] = (g_ref[...] + l_ref[0]).astype(BF)

    blk = pl.BlockSpec((1, tr, wc), lambda j, i, pos: (j, i, 0))
    gs = pltpu.PrefetchScalarGridSpec(
        num_scalar_prefetch=1, grid=(4, hr // tr),
        in_specs=[_piece_spec(g.shape, axis, tr, lambda j, i, pos: j, lambda j, i, pos: pos[1]), blk], out_specs=blk)
    return pl.pallas_call(body, name=name, grid_spec=gs, out_shape=SDS((4, hr, wc), BF),
                          compiler_params=_cp(("arbitrary", "arbitrary")))(pos, g, land)


def _chip_sum(g, land1, land2, axis, pos, name):
    hr, wc = _piece_shape(g.shape, axis)
    tr = min(hr, 256)
    nb = hr // tr

    def body(pos_ref, g_ref, l1_ref, l2_ref, o_ref):
        acc = g_ref[...] + l1_ref[0]
        for q in range(3):
            acc = acc + l2_ref[q].astype(F32)
        o_ref[...] = acc

    gs = pltpu.PrefetchScalarGridSpec(
        num_scalar_prefetch=1, grid=(nb,),
        in_specs=[_piece_spec(g.shape, axis, tr, lambda i, pos: pos[0], lambda i, pos: pos[1]),
                  pl.BlockSpec((1, tr, wc), lambda i, pos: (pos[0], i, 0)),
                  pl.BlockSpec((3, tr, wc), lambda i, pos: (0, i, 0))],
        out_specs=pl.BlockSpec((tr, wc), lambda i, pos: (pos[1] * nb + i, 0)))
    return pl.pallas_call(body, name=name, grid_spec=gs, out_shape=SDS((2 * hr, wc), F32),
                          compiler_params=_cp(("arbitrary",)))(pos, g, land1, land2)


ANY = pl.BlockSpec(memory_space=pl.ANY)
COMM = pltpu.CompilerParams(has_side_effects=True)


def _on_each_device(fn):
    x, y, c = lax.axis_index("x"), lax.axis_index("y"), lax.axis_index("c")
    for sx in (0, 1):
        for sy in (0, 1):
            for sc in (0, 1):
                @pl.when(jnp.logical_and(jnp.logical_and(x == sx, y == sy), c == sc))
                def _(sx=sx, sy=sy, sc=sc):
                    fn(sx, sy, sc)


def _remote(src, dst, send_sem, recv_sem, to):
    return pltpu.make_async_remote_copy(src_ref=src, dst_ref=dst, send_sem=send_sem, recv_sem=recv_sem,
                                        device_id=to, device_id_type=MESH)


def _piece_ref(ref, axis, j, h):
    r, c = ref.shape
    hr, wc = _piece_shape((r, c), axis)
    if axis == 1:
        return ref.at[pl.ds(h * hr, hr), pl.ds(j * wc, wc)]
    return ref.at[pl.ds((2 * j + h) * hr, hr), :]


def _gather_weights(fulls, axes):
    n = len(fulls)

    def body(*refs):
        outs = refs[n:2 * n]
        send_sems, recv_sems = refs[2 * n:]

        def dev(sx, sy, sc):
            j = 2 * sx + sy
            sib = (sx, sy, 1 - sc)
            sends = []
            for w in range(n):
                mine = _piece_ref(outs[w], axes[w], j, sc)
                for r, (fx, fy) in enumerate(CHIP_RELS):
                    k = 6 * w + r
                    cp = _remote(mine, mine, send_sems.at[k], recv_sems.at[k], (sx ^ fx, sy ^ fy, sc))
                    cp.start()
                    sends.append(cp)
            for w in range(n):
                for r, (fx, fy) in enumerate(CHIP_RELS):
                    k = 6 * w + r
                    got = _piece_ref(outs[w], axes[w], 2 * (sx ^ fx) + (sy ^ fy), sc)
                    _remote(got, got, send_sems.at[k], recv_sems.at[k], (sx ^ fx, sy ^ fy, sc)).wait_recv()
                    cp = _remote(got, got, send_sems.at[k + 3], recv_sems.at[k + 3], sib)
                    cp.start()
                    sends.append(cp)
            for w in range(n):
                for r, (fx, fy) in enumerate(CHIP_RELS):
                    k = 6 * w + 3 + r
                    got = _piece_ref(outs[w], axes[w], 2 * (sx ^ fx) + (sy ^ fy), 1 - sc)
                    _remote(got, got, send_sems.at[k], recv_sems.at[k], sib).wait_recv()
            for cp in sends:
                cp.wait_send()

        _on_each_device(dev)

    return pl.pallas_call(
        body, name="gather_weights", in_specs=[ANY] * n, out_specs=[ANY] * n,
        out_shape=[SDS(f.shape, f.dtype) for f in fulls], input_output_aliases={i: i for i in range(n)},
        scratch_shapes=[pltpu.SemaphoreType.DMA((6 * n,)), pltpu.SemaphoreType.DMA((6 * n,))],
        compiler_params=COMM,
    )(*fulls)


class _Stage:
    def __init__(self, ins, outs, n_sems, make_plan, aliases=None):
        self.ins, self.outs, self.n_sems, self.make_plan = list(ins), list(outs), n_sems, make_plan
        self.aliases = dict(aliases or {})


def _start(plan):
    def dev(sx, sy, sc):
        for cp, _, _ in plan(sx, sy, sc):
            cp.start()

    _on_each_device(dev)


def _finish(plan):
    def dev(sx, sy, sc):
        for _, sent, got in plan(sx, sy, sc):
            sent.wait_send()
            got.wait_recv()

    _on_each_device(dev)


def _comm_call(stage, name):
    n_in, n_out = len(stage.ins), len(stage.outs)

    def body(*refs):
        plan = stage.make_plan(refs[:n_in], refs[n_in:n_in + n_out], *refs[n_in + n_out:])
        _start(plan)
        _finish(plan)

    return pl.pallas_call(
        body, name=name, in_specs=[ANY] * n_in, out_specs=[ANY] * n_out, out_shape=stage.outs,
        input_output_aliases=stage.aliases,
        scratch_shapes=[pltpu.SemaphoreType.DMA((stage.n_sems,)), pltpu.SemaphoreType.DMA((stage.n_sems,))],
        compiler_params=COMM,
    )(*stage.ins)


def _same(cp):
    return (cp, cp, cp)


def _stage_gather_send(fulls, axes):
    n = len(fulls)

    def make_plan(ins, outs, send_sems, recv_sems):
        def plan(sx, sy, sc):
            cps = []
            for w in range(n):
                mine = _piece_ref(outs[w], axes[w], 2 * sx + sy, sc)
                for r, (fx, fy) in enumerate(CHIP_RELS):
                    k = 3 * w + r
                    to = (sx ^ fx, sy ^ fy, sc)
                    got = _piece_ref(outs[w], axes[w], 2 * (sx ^ fx) + (sy ^ fy), sc)
                    send = _remote(mine, mine, send_sems.at[k], recv_sems.at[k], to)
                    cps.append((send, send, _remote(got, got, send_sems.at[k], recv_sems.at[k], to)))
            return cps

        return plan

    return _Stage(fulls, [SDS(f.shape, f.dtype) for f in fulls], 3 * n, make_plan, {i: i for i in range(n)})


def _stage_gather_forward(fulls, axes):
    n = len(fulls)

    def make_plan(ins, outs, send_sems, recv_sems):
        def plan(sx, sy, sc):
            cps = []
            sib = (sx, sy, 1 - sc)
            for w in range(n):
                for r, (fx, fy) in enumerate(CHIP_RELS):
                    k = 3 * w + r
                    pj = 2 * (sx ^ fx) + (sy ^ fy)
                    have = _piece_ref(outs[w], axes[w], pj, sc)
                    want = _piece_ref(outs[w], axes[w], pj, 1 - sc)
                    send = _remote(have, have, send_sems.at[k], recv_sems.at[k], sib)
                    cps.append((send, send, _remote(want, want, send_sems.at[k], recv_sems.at[k], sib)))
            return cps

        return plan

    return _Stage(fulls, [SDS(f.shape, f.dtype) for f in fulls], 3 * n, make_plan, {i: i for i in range(n)})


def _stage_pair_exchange(grads, axes):
    n = len(grads)

    def make_plan(gs, land, send_sems, recv_sems):
        def plan(sx, sy, sc):
            return [_same(_remote(_piece_ref(gs[w], axes[w], jj, 1 - sc), land[w].at[jj], send_sems.at[4 * w + jj],
                                  recv_sems.at[4 * w + jj], (sx, sy, 1 - sc)))
                    for w in range(n) for jj in range(4)]

        return plan

    return _Stage(grads, [SDS((4,) + _piece_shape(g.shape, a), F32) for g, a in zip(grads, axes)], 4 * n, make_plan)


def _stage_chip_scatter(pieces):
    n = len(pieces)

    def make_plan(ps, land, send_sems, recv_sems):
        def plan(sx, sy, sc):
            return [_same(_remote(ps[w].at[2 * (sx ^ fx) + (sy ^ fy)], land[w].at[r], send_sems.at[3 * w + r],
                                  recv_sems.at[3 * w + r], (sx ^ fx, sy ^ fy, sc)))
                    for w in range(n) for r, (fx, fy) in enumerate(CHIP_RELS)]

        return plan

    return _Stage(pieces, [SDS((3,) + p.shape[1:], p.dtype) for p in pieces], 3 * n, make_plan)


def _stage_pair_share(shards):
    n = len(shards)

    def make_plan(ins, outs, send_sems, recv_sems):
        def plan(sx, sy, sc):
            cps = []
            sib = (sx, sy, 1 - sc)
            for w in range(n):
                hr = shards[w].shape[0] // 2
                mine = outs[w].at[pl.ds(sc * hr, hr), :]
                theirs = outs[w].at[pl.ds((1 - sc) * hr, hr), :]
                send = _remote(mine, mine, send_sems.at[w], recv_sems.at[w], sib)
                cps.append((send, send, _remote(theirs, theirs, send_sems.at[w], recv_sems.at[w], sib)))
            return cps

        return plan

    return _Stage(shards, [SDS(g.shape, g.dtype) for g in shards], n, make_plan, {i: i for i in range(n)})


def _gather_small(stack):
    def body(src, out, send_sems, recv_sems):
        def dev(sx, sy, sc):
            me = 4 * sx + 2 * sy + sc
            cps = []
            for k, (fx, fy, fc) in enumerate(DEV_RELS):
                cp = _remote(out.at[me], out.at[me], send_sems.at[k], recv_sems.at[k], (sx ^ fx, sy ^ fy, sc ^ fc))
                cp.start()
                cps.append(cp)
            for cp in cps:
                cp.wait()

        _on_each_device(dev)

    return pl.pallas_call(
        body, name="gather_small", in_specs=[ANY], out_specs=ANY, out_shape=SDS(stack.shape, stack.dtype),
        input_output_aliases={0: 0},
        scratch_shapes=[pltpu.SemaphoreType.DMA((7,)), pltpu.SemaphoreType.DMA((7,))],
        compiler_params=COMM,
    )(stack)


def kernel(x, w_in, conv_a_w, w_out_a, conv_b_w, conv_b_bias, ln_b_gamma, ln_b_beta, w_out_b, w_o, ln1_gamma, ln1_beta, w_up, w_down, ln2_gamma, ln2_beta, loss_target, m_w_in, m_conv_a_w, m_w_out_a, m_conv_b_w, m_conv_b_bias, m_ln_b_gamma, m_ln_b_beta, m_w_out_b, m_w_o, m_ln1_gamma, m_ln1_beta, m_w_up, m_w_down, m_ln2_gamma, m_ln2_beta, v_w_in, v_conv_a_w, v_w_out_a, v_conv_b_w, v_conv_b_bias, v_ln_b_gamma, v_ln_b_beta, v_w_out_b, v_w_o, v_ln1_gamma, v_ln1_beta, v_w_up, v_w_down, v_ln2_gamma, v_ln2_beta):
    s, d = x.shape[1], x.shape[2]
    xs = x.reshape(s, d)
    tgt = loss_target.reshape(s, d)
    dq = d // 4
    chip = 2 * lax.axis_index("x") + lax.axis_index("y")
    core = lax.axis_index("c")
    pos = jnp.stack([chip, core]).astype(jnp.int32)
    names = ("w_in", "w_out_a", "w_out_b", "w_o", "w_up", "w_down")
    axes = (1, 0, 0, 0, 1, 0)

    conv_pack = jnp.concatenate([jnp.pad(conv_a_w, ((0, 8 - K_A), (0, 0))), jnp.pad(conv_b_w, ((0, 32 - K_B), (0, 0))),
                                 jnp.zeros((8, dq), F32)], axis=0)
    conv_full = lax.dynamic_update_slice(jnp.zeros((conv_pack.shape[0], d), F32), conv_pack, (0, chip * dq))
    fulls = [_place_cast(w, a, pos, "place_" + nm)
             for w, a, nm in zip((w_in, w_out_a, w_out_b, w_o, w_up, w_down), axes, names)]
    wi, convs = _gather_weights([fulls[0], conv_full], (1, 1))
    cwa, cwb = convs[0:K_A], convs[8:8 + K_B]
    vec = lambda a: a.reshape(1, d)
    bias_b, lbg, lbb = vec(conv_b_bias), vec(ln_b_gamma), vec(ln_b_beta)
    l1g, l1b, l2g, l2b = vec(ln1_gamma), vec(ln1_beta), vec(ln2_gamma), vec(ln2_beta)

    (p, xb), rest = _in_proj(xs, wi, _stage_gather_send(fulls[1:], axes[1:]))
    (conva, yapre, nb, rstdb, u3), rest = _conv_fwd(p, cwa, cwb, bias_b, lbg, lbb, d,
                                                    _stage_gather_forward(rest, axes[1:]))
    woa, wob, wo, wup, wdown = rest
    ya, yb, merged, n1, rstd1 = _mixer_out(yapre, u3, p, xs, woa, wob, wo, d)
    x1b, hb, dhpre, dr2b, dr1, dr1b, acc_mlp = _mlp(n1, rstd1, tgt, wup, wdown, l1g, l1b, l2g, l2b)
    g_up, _ = _grad_w(x1b, dhpre, "grad_w_up")
    g_down, _ = _grad_w(hb, dr2b, "grad_w_down")
    dya, dyb, dba, dga, dgb, dca, du1, acc_mix = _mixer_bwd_local(dr1b, p, ya, yb, conva, nb, rstdb, wo, woa, wob,
                                                                   lbg, lbb)
    g_oa, _ = _grad_w(yapre, dya, "grad_w_out_a")
    g_ob, _ = _grad_w(u3, dyb, "grad_w_out_b")
    g_o, _ = _grad_w(merged, dr1b, "grad_w_o")

    early, e_axes, e_names = [g_oa, g_ob, g_o, g_up, g_down], axes[1:], names[1:]
    (dp, dcwa, dcwb), land1 = _conv_bwd(dca, du1, p, dba, dga, dgb, cwa, cwb, d, _stage_pair_exchange(early, e_axes))
    pieces = [_pair_add(g, l, a, pos, "pair_add_" + nm) for g, l, a, nm in zip(early, land1, e_axes, e_names)]
    g_wi, land2 = _grad_w(xb, dp, "grad_w_in", _stage_chip_scatter(pieces))
    halves = [_chip_sum(g, l1, l2, a, pos, "chip_sum_" + nm)
              for g, l1, l2, a, nm in zip(early, land1, land2, e_axes, e_names)]
    grad_x = _grad_x(dr1, dp, wi)
    (land1_in,) = _comm_call(_stage_pair_exchange([g_wi], (1,)), "pair_exchange_w_in")
    piece_in = _pair_add(g_wi, land1_in, 1, pos, "pair_add_w_in")
    (land2_in,) = _comm_call(_stage_chip_scatter([piece_in]), "chip_scatter_w_in")
    half_in = _chip_sum(g_wi, land1_in, land2_in, 1, pos, "chip_sum_w_in")
    g_in, g_oa, g_ob, g_o, g_up, g_down = _comm_call(_stage_pair_share([half_in] + halves), "pair_share")

    pack = jnp.concatenate([dcwa, dcwb, acc_mix, acc_mlp], axis=0)
    stack = lax.dynamic_update_slice(jnp.zeros((8,) + pack.shape, F32), pack[None], (2 * chip + core, 0, 0))
    small = _sum_parts(_gather_small(stack), "small_sum")
    g_ca = lax.dynamic_slice(small, (0, chip * dq), (K_A, dq))
    g_cb = lax.dynamic_slice(small, (8, chip * dq), (K_B, dq))
    g_vec = jnp.stack([small[r] for r in (42, 40, 41, 51, 52, 49, 50)])

    loss = lax.psum((0.5 / d) * jnp.sum(acc_mlp[0]), ("x", "y", "c"))

    big = {}
    for name, w, g, m, v in (("w_in", w_in, g_in, m_w_in, v_w_in), ("w_out_a", w_out_a, g_oa, m_w_out_a, v_w_out_a),
                             ("w_out_b", w_out_b, g_ob, m_w_out_b, v_w_out_b), ("w_o", w_o, g_o, m_w_o, v_w_o),
                             ("w_up", w_up, g_up, m_w_up, v_w_up), ("w_down", w_down, g_down, m_w_down, v_w_down),
                             ("conv_a_w", conv_a_w, g_ca, m_conv_a_w, v_conv_a_w),
                             ("conv_b_w", conv_b_w, g_cb, m_conv_b_w, v_conv_b_w)):
        big[name] = (g,) + tuple(_adamw(w, g, m, v, "adamw_" + name))
    vec_names = ("conv_b_bias", "ln_b_gamma", "ln_b_beta", "ln1_gamma", "ln1_beta", "ln2_gamma", "ln2_beta")
    w7 = jnp.stack([conv_b_bias, ln_b_gamma, ln_b_beta, ln1_gamma, ln1_beta, ln2_gamma, ln2_beta])
    m7 = jnp.stack([m_conv_b_bias, m_ln_b_gamma, m_ln_b_beta, m_ln1_gamma, m_ln1_beta, m_ln2_gamma, m_ln2_beta])
    v7 = jnp.stack([v_conv_b_bias, v_ln_b_gamma, v_ln_b_beta, v_ln1_gamma, v_ln1_beta, v_ln2_gamma, v_ln2_beta])
    d7, nm7, nv7 = _adamw(w7, g_vec, m7, v7, "adamw_vectors")
    for q, name in enumerate(vec_names):
        big[name] = (g_vec[q], d7[q], nm7[q], nv7[q])

    order = ("w_in", "conv_a_w", "w_out_a", "conv_b_w", "conv_b_bias", "ln_b_gamma", "ln_b_beta", "w_out_b", "w_o",
             "ln1_gamma", "ln1_beta", "w_up", "w_down", "ln2_gamma", "ln2_beta")
    outs = [loss, grad_x.reshape(x.shape)]
    for part in range(4):
        outs += [big[name][part] for name in order]
    return tuple(outs)
```

```python
import jax
import jax.numpy as jnp
from jax import lax
from jax.experimental import pallas as pl
from jax.experimental.pallas import tpu as pltpu

F32 = jnp.float32
BF = jnp.bfloat16
SDS = jax.ShapeDtypeStruct
MESH = pl.DeviceIdType.MESH

ALPHA = 2.0 ** 0.25
LN_EPS = 1e-5
K_A = 3
K_B = 31
HALO = 32
CONV_ROWS = 64
CONV_LANES = 128
ADAM_LR = 0.001
ADAM_B1 = 0.9
ADAM_B2 = 0.999
ADAM_EPS = 1e-08
ADAM_WD = 0.01
ADAM_STEP = 10
P_DT = BF
CHIP_RELS = ((1, 0), (0, 1), (1, 1))
DEV_RELS = tuple((fx, fy, fc) for fx in (0, 1) for fy in (0, 1) for fc in (0, 1))[1:]


def _cp(sem=None, vmem_mb=56, side_effects=False):
    return pltpu.CompilerParams(dimension_semantics=sem, vmem_limit_bytes=vmem_mb << 20,
                                has_side_effects=side_effects)


def _const(shape):
    return pl.BlockSpec(shape, lambda *_: (0,) * len(shape), pipeline_mode=pl.Buffered(1))


def _sig(v):
    return jax.nn.sigmoid(v)


def _ln_fwd(r):
    mu = jnp.mean(r, axis=-1, keepdims=True)
    xc = r - mu
    var = jnp.mean(xc * xc, axis=-1, keepdims=True)
    rstd = lax.rsqrt(var + LN_EPS)
    return xc * rstd, rstd


def _ln_bwd(dn, n, rstd):
    m1 = jnp.mean(dn, axis=-1, keepdims=True)
    m2 = jnp.mean(dn * n, axis=-1, keepdims=True)
    return rstd * (dn - m1 - n * m2)


def _dot(a, b):
    return jnp.dot(a, b, preferred_element_type=F32)


def _dot_nt(a, b):
    return lax.dot_general(a, b, (((1,), (1,)), ((), ())), preferred_element_type=F32)


def _dot_tn(a, b):
    return lax.dot_general(a, b, (((0,), (0,)), ((), ())), preferred_element_type=F32)


def _tile(n, pref):
    if n <= pref:
        return n
    return max(t for t in range(128, pref + 1, 128) if n % t == 0)


def _rowsum8(v):
    acc = v[0:8]
    for g in range(1, v.shape[0] // 8):
        acc = acc + v[8 * g:8 * g + 8]
    return acc


def _taps(win, offsets, rows):
    r_all = win.shape[0]
    by_res = {}
    for k, o in enumerate(offsets):
        by_res.setdefault(o % 8, []).append((k, o // 8))
    for s, taps in sorted(by_res.items()):
        r = win if s == 0 else pltpu.roll(win, r_all - s, 0)
        for k, q in taps:
            yield k, r[8 * q:8 * q + rows]


CAUSAL_A = [HALO - (K_A - 1) + k for k in range(K_A)]
CAUSAL_B = [HALO - (K_B - 1) + k for k in range(K_B)]
ANTI_A = [K_A - 1 - k for k in range(K_A)]
ANTI_B = [K_B - 1 - k for k in range(K_B)]


def _host_call(body, *, name, grid, in_specs, out_specs, out_shape, scratch_shapes, args, hosted):
    n_in, n_out, n_scr = len(in_specs), len(out_specs), len(scratch_shapes)
    sem = ("arbitrary",) * len(grid)
    if hosted is None:
        res = pl.pallas_call(body, name=name, grid=grid, in_specs=in_specs, out_specs=out_specs, out_shape=out_shape,
                             scratch_shapes=scratch_shapes, compiler_params=_cp(sem))(*args)
        return list(res), []
    h_in, h_out = len(hosted.ins), len(hosted.outs)

    def full_body(*refs):
        ins, refs = refs[:n_in], refs[n_in:]
        hins, refs = refs[:h_in], refs[h_in:]
        outs, refs = refs[:n_out], refs[n_out:]
        houts, refs = refs[:h_out], refs[h_out:]
        scr, (send_sems, recv_sems) = refs[:n_scr], refs[n_scr:]
        plan = hosted.make_plan(hins, houts, send_sems, recv_sems)
        first = last = None
        for a, size in enumerate(grid):
            at0, at1 = pl.program_id(a) == 0, pl.program_id(a) == size - 1
            first = at0 if first is None else jnp.logical_and(first, at0)
            last = at1 if last is None else jnp.logical_and(last, at1)

        @pl.when(first)
        def _():
            _start(plan)

        body(*ins, *outs, *scr)

        @pl.when(last)
        def _():
            _finish(plan)

    res = pl.pallas_call(
        full_body, name=name, grid=grid, in_specs=list(in_specs) + [ANY] * h_in,
        out_specs=list(out_specs) + [ANY] * h_out, out_shape=list(out_shape) + list(hosted.outs),
        scratch_shapes=list(scratch_shapes) + [pltpu.SemaphoreType.DMA((hosted.n_sems,)),
                                               pltpu.SemaphoreType.DMA((hosted.n_sems,))],
        input_output_aliases={n_in + a: n_out + b for a, b in hosted.aliases.items()},
        compiler_params=_cp(sem, side_effects=True),
    )(*args, *hosted.ins)
    return list(res[:n_out]), list(res[n_out:])


def _in_proj(x, wi, hosted=None):
    s, d = x.shape
    n = wi.shape[1]
    tm, tn = min(s, 1024), d

    def body(x_ref, w_ref, p_ref, xb_ref):
        @pl.when(pl.program_id(1) == 0)
        def _():
            xb_ref[...] = x_ref[...].astype(BF)

        p_ref[...] = _dot(xb_ref[...], w_ref[...]).astype(p_ref.dtype)

    return _host_call(
        body, name="in_proj", grid=(s // tm, n // tn),
        in_specs=[pl.BlockSpec((tm, d), lambda i, j: (i, 0)), pl.BlockSpec((d, tn), lambda i, j: (0, j))],
        out_specs=[pl.BlockSpec((tm, tn), lambda i, j: (i, j)), pl.BlockSpec((tm, d), lambda i, j: (i, 0))],
        out_shape=[SDS((s, n), P_DT), SDS((s, d), BF)], scratch_shapes=[], args=(x, wi), hosted=hosted)


def _col_spec(tm, d, k):
    return pl.BlockSpec((tm, d), lambda i, k=k: (i, k))


def _prev_halo_spec(tm, d, k):
    r = tm // HALO
    return pl.BlockSpec((HALO, d), lambda i, k=k: (jnp.maximum(i * r - 1, 0), k))


def _next_halo_spec(tm, d, k, s):
    r = tm // HALO
    last = s // HALO - 1
    return pl.BlockSpec((HALO, d), lambda i, k=k: (jnp.minimum((i + 1) * r, last), k))


def _conv_fwd(p, cwa, cwb, bias_b, lbg, lbb, d, hosted=None):
    s = p.shape[0]
    tm = min(s, 256)
    nt = s // tm

    def body(ba_ref, ca_ref, va_ref, vb_ref, gb_ref, hca_ref, hva_ref, hvb_ref, hgb_ref,
             cwa_ref, cwb_ref, bias_ref, lbg_ref, lbb_ref,
             conva_ref, yapre_ref, nb_ref, rstdb_ref, u3_ref,
             zbuf, ubuf, u1buf):
        i = pl.program_id(0)
        keep = (i > 0).astype(F32)
        zbuf[pl.ds(0, HALO), :] = hca_ref[...].astype(F32) * hva_ref[...].astype(F32) * keep
        ubuf[pl.ds(0, HALO), :] = hvb_ref[...].astype(F32) * _sig(hgb_ref[...].astype(F32)) * keep
        zbuf[pl.ds(HALO, tm), :] = ca_ref[...].astype(F32) * va_ref[...].astype(F32)
        ubuf[pl.ds(HALO, tm), :] = vb_ref[...].astype(F32) * _sig(gb_ref[...].astype(F32))

        def chunk(j, carry):
            r0 = pl.multiple_of(j * CONV_ROWS, CONV_ROWS)
            rows = pl.ds(r0, CONV_ROWS)
            for lc in range(d // CONV_LANES):
                ls = pl.ds(lc * CONV_LANES, CONV_LANES)
                acc = jnp.zeros((CONV_ROWS, CONV_LANES), F32)
                for k, sl in _taps(zbuf[pl.ds(r0, CONV_ROWS + HALO), ls], CAUSAL_A, CONV_ROWS):
                    acc = acc + cwa_ref[pl.ds(k, 1), ls] * sl
                conva_ref[rows, ls] = acc
                yapre_ref[rows, ls] = (ba_ref[rows, ls].astype(F32) * acc).astype(BF)
                acc = jnp.zeros((CONV_ROWS, CONV_LANES), F32)
                for k, sl in _taps(ubuf[pl.ds(r0, CONV_ROWS + HALO), ls], CAUSAL_B, CONV_ROWS):
                    acc = acc + cwb_ref[pl.ds(k, 1), ls] * sl
                u1buf[rows, ls] = acc + bias_ref[:, ls]
            return carry

        lax.fori_loop(0, tm // CONV_ROWS, chunk, 0)
        nb, rstd = _ln_fwd(u1buf[...])
        nb_ref[...] = nb
        rstdb_ref[...] = rstd
        u2 = nb * lbg_ref[...] + lbb_ref[...]
        u3_ref[...] = (u2 * _sig(u2)).astype(BF)

    vec = _const((1, d))
    return _host_call(
        body, name="conv_fwd", grid=(nt,),
        in_specs=[_col_spec(tm, d, k) for k in range(5)] + [_prev_halo_spec(tm, d, k) for k in (1, 2, 3, 4)]
        + [_const((K_A, d)), _const((K_B, d)), vec, vec, vec],
        out_specs=[pl.BlockSpec((tm, d), lambda i: (i, 0)), pl.BlockSpec((tm, d), lambda i: (i, 0)),
                   pl.BlockSpec((tm, d), lambda i: (i, 0)), pl.BlockSpec((tm, 1), lambda i: (i, 0)),
                   pl.BlockSpec((tm, d), lambda i: (i, 0))],
        out_shape=[SDS((s, d), F32), SDS((s, d), BF), SDS((s, d), F32), SDS((s, 1), F32), SDS((s, d), BF)],
        scratch_shapes=[pltpu.VMEM((HALO + tm, d), F32), pltpu.VMEM((HALO + tm, d), F32), pltpu.VMEM((tm, d), F32)],
        args=(p, p, p, p, p, p, p, p, p, cwa, cwb, bias_b, lbg, lbb), hosted=hosted)


def _mixer_out(yapre, u3, p, x, woa, wob, wo, d):
    s = x.shape[0]
    tm = min(s, 512)

    def body(yapre_ref, u3_ref, ga_ref, gb_ref, x_ref, woa_ref, wob_ref, wo_ref,
             ya_ref, yb_ref, merged_ref, n1_ref, rstd1_ref):
        ya = _dot(yapre_ref[...], woa_ref[...])
        yb = _dot(u3_ref[...], wob_ref[...])
        ya_ref[...] = ya
        yb_ref[...] = yb
        merged = (_sig(ga_ref[...].astype(F32)) * ya + _sig(gb_ref[...].astype(F32)) * yb).astype(BF)
        merged_ref[...] = merged
        r1 = F32(ALPHA) * x_ref[...] + _dot(merged, wo_ref[...])
        n1, rstd1 = _ln_fwd(r1)
        n1_ref[...] = n1
        rstd1_ref[...] = rstd1

    row = pl.BlockSpec((tm, d), lambda i: (i, 0))
    return pl.pallas_call(
        body, name="mixer_out", grid=(s // tm,),
        in_specs=[row, row, _col_spec(tm, d, 5), _col_spec(tm, d, 6), row,
                  _const((d, d)), _const((d, d)), _const((d, d))],
        out_specs=[row, row, row, row, pl.BlockSpec((tm, 1), lambda i: (i, 0))],
        out_shape=[SDS((s, d), F32), SDS((s, d), F32), SDS((s, d), BF), SDS((s, d), F32), SDS((s, 1), F32)],
        compiler_params=_cp(("parallel",)),
    )(yapre, u3, p, p, x, woa, wob, wo)


def _mlp(n1, rstd1, tgt, wup, wdown, l1g, l1b, l2g, l2b):
    s, d = n1.shape
    dff = wup.shape[1]
    tm = min(s, 256)
    fc = min(dff, 1024)
    nq = dff // fc

    def body(n1_ref, rstd1_ref, tgt_ref, wup_ref, wdown_ref, l1g_ref, l1b_ref, l2g_ref, l2b_ref,
             x1b_ref, hb_ref, dhpre_ref, dr2b_ref, dr1_ref, dr1b_ref, acc_ref, rbuf):
        i = pl.program_id(0)
        n1v = n1_ref[...]
        x1 = n1v * l1g_ref[...] + l1b_ref[...]
        x1b = x1.astype(BF)
        x1b_ref[...] = x1b
        ff = jnp.zeros((tm, d), F32)
        for q in range(nq):
            cs = pl.ds(q * fc, fc)
            r = jnp.maximum(_dot(x1b, wup_ref[:, cs]), 0.0)
            rbuf[:, cs] = r
            hq = (r * r).astype(BF)
            hb_ref[:, cs] = hq
            ff = ff + _dot(hq, wdown_ref[cs, :])
        n2, rstd2 = _ln_fwd(F32(ALPHA) * x1 + ff)
        x2 = n2 * l2g_ref[...] + l2b_ref[...]
        err = x2 - tgt_ref[...]
        dx2 = err * F32(1.0 / d)
        dr2 = _ln_bwd(dx2 * l2g_ref[...], n2, rstd2)
        dr2b = dr2.astype(BF)
        dr2b_ref[...] = dr2b
        dx1 = F32(ALPHA) * dr2
        for q in range(nq):
            cs = pl.ds(q * fc, fc)
            dh = _dot_nt(dr2b, wdown_ref[cs, :])
            dhp = (dh * (2.0 * rbuf[:, cs])).astype(BF)
            dhpre_ref[:, cs] = dhp
            dx1 = dx1 + _dot_nt(dhp, wup_ref[:, cs])
        dr1 = _ln_bwd(dx1 * l1g_ref[...], n1v, rstd1_ref[...])
        dr1_ref[...] = dr1
        dr1b_ref[...] = dr1.astype(BF)

        @pl.when(i == 0)
        def _():
            acc_ref[...] = jnp.zeros_like(acc_ref)

        for q, val in enumerate((err * err, dx2 * n2, dx2, dx1 * n1v, dx1)):
            acc_ref[pl.ds(q, 1), :] += jnp.sum(val, axis=0, keepdims=True)

    row = pl.BlockSpec((tm, d), lambda i: (i, 0))
    wide = pl.BlockSpec((tm, dff), lambda i: (i, 0))
    vec = _const((1, d))
    return pl.pallas_call(
        body, name="mlp_fwd_bwd", grid=(s // tm,),
        in_specs=[row, pl.BlockSpec((tm, 1), lambda i: (i, 0)), row, _const((d, dff)), _const((dff, d)),
                  vec, vec, vec, vec],
        out_specs=[row, wide, wide, row, row, row, pl.BlockSpec((8, d), lambda i: (0, 0))],
        out_shape=[SDS((s, d), BF), SDS((s, dff), BF), SDS((s, dff), BF), SDS((s, d), BF), SDS((s, d), F32),
                   SDS((s, d), BF), SDS((8, d), F32)],
        scratch_shapes=[pltpu.VMEM((tm, dff), F32)],
        compiler_params=_cp(("arbitrary",)),
    )(n1, rstd1, tgt, wup, wdown, l1g, l1b, l2g, l2b)


def _mixer_bwd_local(dr1b, p, ya, yb, conva, nb, rstdb, wo, woa, wob, lbg, lbb):
    s, d = ya.shape
    tm = min(s, 256)

    def body(dr1b_ref, ba_ref, ga_ref, gb_ref, ya_ref, yb_ref, conva_ref, nb_ref, rstdb_ref,
             wo_ref, woa_ref, wob_ref, lbg_ref, lbb_ref,
             dya_ref, dyb_ref, dba_ref, dga_ref, dgb_ref, dca_ref, du1_ref, acc_ref):
        i = pl.program_id(0)
        dmerged = _dot_nt(dr1b_ref[...], wo_ref[...])
        sa = _sig(ga_ref[...].astype(F32))
        sb = _sig(gb_ref[...].astype(F32))
        dya = (dmerged * sa).astype(BF)
        dyb = (dmerged * sb).astype(BF)
        dya_ref[...] = dya
        dyb_ref[...] = dyb
        dga_ref[...] = (dmerged * ya_ref[...] * (sa * (1.0 - sa))).astype(BF)
        dgb_ref[...] = (dmerged * yb_ref[...] * (sb * (1.0 - sb))).astype(BF)
        dyapre = _dot_nt(dya, woa_ref[...])
        dba_ref[...] = (dyapre * conva_ref[...]).astype(BF)
        dca_ref[...] = dyapre * ba_ref[...].astype(F32)
        du3 = _dot_nt(dyb, wob_ref[...])
        nbv = nb_ref[...]
        u2 = nbv * lbg_ref[...] + lbb_ref[...]
        sg = _sig(u2)
        du2 = du3 * (sg * (1.0 + u2 * (1.0 - sg)))
        du1 = _ln_bwd(du2 * lbg_ref[...], nbv, rstdb_ref[...])
        du1_ref[...] = du1

        @pl.when(i == 0)
        def _():
            acc_ref[...] = jnp.zeros_like(acc_ref)

        for q, val in enumerate((du2 * nbv, du2, du1)):
            acc_ref[pl.ds(q, 1), :] += jnp.sum(val, axis=0, keepdims=True)

    row = pl.BlockSpec((tm, d), lambda i: (i, 0))
    vec = _const((1, d))
    return pl.pallas_call(
        body, name="mixer_bwd_local", grid=(s // tm,),
        in_specs=[row, _col_spec(tm, d, 0), _col_spec(tm, d, 5), _col_spec(tm, d, 6), row, row, row, row,
                  pl.BlockSpec((tm, 1), lambda i: (i, 0)), _const((d, d)), _const((d, d)), _const((d, d)), vec, vec],
        out_specs=[row, row, row, row, row, row, row, pl.BlockSpec((8, d), lambda i: (0, 0))],
        out_shape=[SDS((s, d), BF)] * 5 + [SDS((s, d), F32), SDS((s, d), F32), SDS((8, d), F32)],
        compiler_params=_cp(("arbitrary",)),
    )(dr1b, p, p, p, ya, yb, conva, nb, rstdb, wo, woa, wob, lbg, lbb)


def _conv_bwd(dca, du1, p, dba, dga, dgb, cwa, cwb, d, hosted=None):
    s = dca.shape[0]
    tm = min(s, 256)
    nt = s // tm

    def body(dca_ref, du1_ref, ndca_ref, ndu1_ref, ca_ref, va_ref, vb_ref, gb_ref,
             hca_ref, hva_ref, hvb_ref, hgb_ref, dba_ref, dga_ref, dgb_ref, cwa_ref, cwb_ref,
             dp_ref, dcwa_ref, dcwb_ref,
             dcabuf, du1buf, zbuf, ubuf, sgbuf, acca, accb):
        i = pl.program_id(0)
        keep_prev = (i > 0).astype(F32)
        keep_next = (i < nt - 1).astype(F32)

        @pl.when(i == 0)
        def _():
            acca[...] = jnp.zeros_like(acca)
            accb[...] = jnp.zeros_like(accb)

        sg = _sig(gb_ref[...].astype(F32))
        sgbuf[...] = sg
        dcabuf[pl.ds(0, tm), :] = dca_ref[...]
        dcabuf[pl.ds(tm, HALO), :] = ndca_ref[...] * keep_next
        du1buf[pl.ds(0, tm), :] = du1_ref[...]
        du1buf[pl.ds(tm, HALO), :] = ndu1_ref[...] * keep_next
        zbuf[pl.ds(0, HALO), :] = hca_ref[...].astype(F32) * hva_ref[...].astype(F32) * keep_prev
        ubuf[pl.ds(0, HALO), :] = hvb_ref[...].astype(F32) * _sig(hgb_ref[...].astype(F32)) * keep_prev
        zbuf[pl.ds(HALO, tm), :] = ca_ref[...].astype(F32) * va_ref[...].astype(F32)
        ubuf[pl.ds(HALO, tm), :] = vb_ref[...].astype(F32) * sg
        dp_ref[:, pl.ds(0, d)] = dba_ref[...]
        dp_ref[:, pl.ds(5 * d, d)] = dga_ref[...]
        dp_ref[:, pl.ds(6 * d, d)] = dgb_ref[...]

        def chunk(j, carry):
            r0 = pl.multiple_of(j * CONV_ROWS, CONV_ROWS)
            rows = pl.ds(r0, CONV_ROWS)
            for lc in range(d // CONV_LANES):
                lo = lc * CONV_LANES
                ls = pl.ds(lo, CONV_LANES)
                dchunk = dcabuf[rows, ls]
                acc = jnp.zeros((CONV_ROWS, CONV_LANES), F32)
                for k, sl in _taps(dcabuf[pl.ds(r0, CONV_ROWS + HALO), ls], ANTI_A, CONV_ROWS):
                    acc = acc + cwa_ref[pl.ds(k, 1), ls] * sl
                for k, sl in _taps(zbuf[pl.ds(r0, CONV_ROWS + HALO), ls], CAUSAL_A, CONV_ROWS):
                    acca[pl.ds(8 * k, 8), ls] += _rowsum8(dchunk * sl)
                dp_ref[rows, pl.ds(d + lo, CONV_LANES)] = (acc * va_ref[rows, ls].astype(F32)).astype(BF)
                dp_ref[rows, pl.ds(2 * d + lo, CONV_LANES)] = (acc * ca_ref[rows, ls].astype(F32)).astype(BF)
                dchunk = du1buf[rows, ls]
                acc = jnp.zeros((CONV_ROWS, CONV_LANES), F32)
                for k, sl in _taps(du1buf[pl.ds(r0, CONV_ROWS + HALO), ls], ANTI_B, CONV_ROWS):
                    acc = acc + cwb_ref[pl.ds(k, 1), ls] * sl
                for k, sl in _taps(ubuf[pl.ds(r0, CONV_ROWS + HALO), ls], CAUSAL_B, CONV_ROWS):
                    accb[pl.ds(8 * k, 8), ls] += _rowsum8(dchunk * sl)
                sgc = sgbuf[rows, ls]
                dp_ref[rows, pl.ds(3 * d + lo, CONV_LANES)] = (acc * sgc).astype(BF)
                dp_ref[rows, pl.ds(4 * d + lo, CONV_LANES)] = (
                    acc * vb_ref[rows, ls].astype(F32) * (sgc * (1.0 - sgc))).astype(BF)
            return carry

        lax.fori_loop(0, tm // CONV_ROWS, chunk, 0)

        @pl.when(i == nt - 1)
        def _():
            dcwa_ref[...] = jnp.zeros_like(dcwa_ref)
            dcwb_ref[...] = jnp.zeros_like(dcwb_ref)
            for k in range(K_A):
                dcwa_ref[pl.ds(k, 1), :] = jnp.sum(acca[pl.ds(8 * k, 8), :], axis=0, keepdims=True)
            for k in range(K_B):
                dcwb_ref[pl.ds(k, 1), :] = jnp.sum(accb[pl.ds(8 * k, 8), :], axis=0, keepdims=True)

    row = pl.BlockSpec((tm, d), lambda i: (i, 0))
    nxt = _next_halo_spec(tm, d, 0, s)
    return _host_call(
        body, name="conv_bwd", grid=(nt,),
        in_specs=[row, row, nxt, nxt] + [_col_spec(tm, d, k) for k in (1, 2, 3, 4)]
        + [_prev_halo_spec(tm, d, k) for k in (1, 2, 3, 4)] + [row, row, row, _const((K_A, d)), _const((K_B, d))],
        out_specs=[pl.BlockSpec((tm, 7 * d), lambda i: (i, 0)), pl.BlockSpec((8, d), lambda i: (0, 0)),
                   pl.BlockSpec((32, d), lambda i: (0, 0))],
        out_shape=[SDS((s, 7 * d), BF), SDS((8, d), F32), SDS((32, d), F32)],
        scratch_shapes=[pltpu.VMEM((tm + HALO, d), F32), pltpu.VMEM((tm + HALO, d), F32),
                        pltpu.VMEM((HALO + tm, d), F32), pltpu.VMEM((HALO + tm, d), F32),
                        pltpu.VMEM((tm, d), F32), pltpu.VMEM((8 * K_A, d), F32), pltpu.VMEM((8 * K_B, d), F32)],
        args=(dca, du1, dca, du1, p, p, p, p, p, p, p, p, dba, dga, dgb, cwa, cwb), hosted=hosted)


def _grad_w(a, b, name, hosted=None):
    s, m = a.shape
    n = b.shape[1]
    tm, tn, tk = _tile(m, 1024), _tile(n, 1024), _tile(s, 2048)

    def body(a_ref, b_ref, o_ref):
        @pl.when(pl.program_id(2) == 0)
        def _():
            o_ref[...] = jnp.zeros_like(o_ref)

        o_ref[...] += _dot_tn(a_ref[...], b_ref[...])

    (g,), extra = _host_call(
        body, name=name, grid=(m // tm, n // tn, s // tk),
        in_specs=[pl.BlockSpec((tk, tm), lambda i, j, k: (k, i)), pl.BlockSpec((tk, tn), lambda i, j, k: (k, j))],
        out_specs=[pl.BlockSpec((tm, tn), lambda i, j, k: (i, j))],
        out_shape=[SDS((m, n), F32)], scratch_shapes=[], args=(a, b), hosted=hosted)
    return g, extra


def _grad_x(dr1, dp, wi, hosted=None):
    s, d = dr1.shape
    n = wi.shape[1]
    tm, tk = min(s, 512), _tile(n, 3584)

    def body(dr1_ref, dp_ref, w_ref, o_ref):
        @pl.when(pl.program_id(1) == 0)
        def _():
            o_ref[...] = F32(ALPHA) * dr1_ref[...]

        o_ref[...] += _dot_nt(dp_ref[...], w_ref[...])

    (gx,), extra = _host_call(
        body, name="grad_x", grid=(s // tm, n // tk),
        in_specs=[pl.BlockSpec((tm, d), lambda i, k: (i, 0)), pl.BlockSpec((tm, tk), lambda i, k: (i, k)),
                  pl.BlockSpec((d, tk), lambda i, k: (0, k))],
        out_specs=[pl.BlockSpec((tm, d), lambda i, k: (i, 0))],
        out_shape=[SDS((s, d), F32)], scratch_shapes=[], args=(dr1, dp, wi), hosted=hosted)
    return gx, extra


def _adamw_math(w, g, m, v):
    m2 = ADAM_B1 * m + (1.0 - ADAM_B1) * g
    v2 = ADAM_B2 * v + (1.0 - ADAM_B2) * (g * g)
    m_hat = m2 / (1.0 - ADAM_B1 ** ADAM_STEP)
    v_hat = v2 / (1.0 - ADAM_B2 ** ADAM_STEP)
    delta = -ADAM_LR * (m_hat / (jnp.sqrt(v_hat) + ADAM_EPS) + ADAM_WD * w)
    return delta, m2, v2


def _adamw(w, g, m, v, name):
    r, c = w.shape
    tr = r if r <= 256 else 256

    def body(w_ref, g_ref, m_ref, v_ref, d_ref, m2_ref, v2_ref):
        delta, m2, v2 = _adamw_math(w_ref[...], g_ref[...], m_ref[...], v_ref[...])
        d_ref[...] = delta
        m2_ref[...] = m2
        v2_ref[...] = v2

    blk = pl.BlockSpec((tr, c), lambda i: (i, 0))
    return pl.pallas_call(
        body, name=name, grid=(r // tr,), in_specs=[blk] * 4, out_specs=[blk] * 3,
        out_shape=[SDS((r, c), F32)] * 3, compiler_params=_cp(("parallel",)),
    )(w, g, m, v)


def _sum_parts(parts, name):
    k, r, c = parts.shape

    def body(p_ref, o_ref):
        acc = p_ref[0]
        for q in range(1, k):
            acc = acc + p_ref[q]
        o_ref[...] = acc

    return pl.pallas_call(
        body, name=name, grid=(1,),
        in_specs=[pl.BlockSpec((k, r, c), lambda i: (0, 0, 0))],
        out_specs=pl.BlockSpec((r, c), lambda i: (0, 0)),
        out_shape=SDS((r, c), F32), compiler_params=_cp(("arbitrary",)),
    )(parts)


def _piece_shape(full_shape, axis):
    r, c = full_shape
    return (r // 2, c // 4) if axis == 1 else (r // 8, c)


def _piece_spec(full_shape, axis, tr, chip_of, half_of):
    hr, wc = _piece_shape(full_shape, axis)
    nb = hr // tr
    if axis == 1:
        return pl.BlockSpec((tr, wc), lambda *a: (half_of(*a) * nb + a[-2], chip_of(*a)))
    return pl.BlockSpec((tr, wc), lambda *a: ((2 * chip_of(*a) + half_of(*a)) * nb + a[-2], 0))


def _place_cast(w, axis, pos, name):
    r, c = w.shape
    tr = min(r, 256)
    nb = r // tr
    full = (r, 4 * c) if axis == 1 else (4 * r, c)
    out_map = (lambda i, pos: (i, pos[0])) if axis == 1 else (lambda i, pos: (pos[0] * nb + i, 0))

    def body(pos_ref, w_ref, o_ref):
        o_ref[...] = w_ref[...].astype(o_ref.dtype)

    gs = pltpu.PrefetchScalarGridSpec(
        num_scalar_prefetch=1, grid=(nb,),
        in_specs=[pl.BlockSpec((tr, c), lambda i, pos: (i, 0))], out_specs=pl.BlockSpec((tr, c), out_map))
    return pl.pallas_call(body, name=name, grid_spec=gs, out_shape=SDS(full, BF),
                          compiler_params=_cp(("arbitrary",)))(pos, w)


def _pair_add(g, land, axis, pos, name):
    hr, wc = _piece_shape(g.shape, axis)
    tr = min(hr, 256)

    def body(pos_ref, g_ref, l_ref, o_ref):
        o_ref[0] = (g_ref[...] + l_ref[0]).astype(BF)

    blk = pl.BlockSpec((1, tr, wc), lambda j, i, pos: (j, i, 0))
    gs = pltpu.PrefetchScalarGridSpec(
        num_scalar_prefetch=1, grid=(4, hr // tr),
        in_specs=[_piece_spec(g.shape, axis, tr, lambda j, i, pos: j, lambda j, i, pos: pos[1]), blk], out_specs=blk)
    return pl.pallas_call(body, name=name, grid_spec=gs, out_shape=SDS((4, hr, wc), BF),
                          compiler_params=_cp(("arbitrary", "arbitrary")))(pos, g, land)


def _chip_sum(g, land1, land2, axis, pos, name):
    hr, wc = _piece_shape(g.shape, axis)
    tr = min(hr, 256)
    nb = hr // tr

    def body(pos_ref, g_ref, l1_ref, l2_ref, o_ref):
        acc = g_ref[...] + l1_ref[0]
        for q in range(3):
            acc = acc + l2_ref[q].astype(F32)
        o_ref[...] = acc

    gs = pltpu.PrefetchScalarGridSpec(
        num_scalar_prefetch=1, grid=(nb,),
        in_specs=[_piece_spec(g.shape, axis, tr, lambda i, pos: pos[0], lambda i, pos: pos[1]),
                  pl.BlockSpec((1, tr, wc), lambda i, pos: (pos[0], i, 0)),
                  pl.BlockSpec((3, tr, wc), lambda i, pos: (0, i, 0))],
        out_specs=pl.BlockSpec((tr, wc), lambda i, pos: (pos[1] * nb + i, 0)))
    return pl.pallas_call(body, name=name, grid_spec=gs, out_shape=SDS((2 * hr, wc), F32),
                          compiler_params=_cp(("arbitrary",)))(pos, g, land1, land2)


ANY = pl.BlockSpec(memory_space=pl.ANY)
COMM = pltpu.CompilerParams(has_side_effects=True)


def _on_each_device(fn):
    x, y, c = lax.axis_index("x"), lax.axis_index("y"), lax.axis_index("c")
    for sx in (0, 1):
        for sy in (0, 1):
            for sc in (0, 1):
                @pl.when(jnp.logical_and(jnp.logical_and(x == sx, y == sy), c == sc))
                def _(sx=sx, sy=sy, sc=sc):
                    fn(sx, sy, sc)


def _remote(src, dst, send_sem, recv_sem, to):
    return pltpu.make_async_remote_copy(src_ref=src, dst_ref=dst, send_sem=send_sem, recv_sem=recv_sem,
                                        device_id=to, device_id_type=MESH)


def _piece_ref(ref, axis, j, h):
    r, c = ref.shape
    hr, wc = _piece_shape((r, c), axis)
    if axis == 1:
        return ref.at[pl.ds(h * hr, hr), pl.ds(j * wc, wc)]
    return ref.at[pl.ds((2 * j + h) * hr, hr), :]


def _gather_weights(fulls, axes):
    n = len(fulls)

    def body(*refs):
        outs = refs[n:2 * n]
        send_sems, recv_sems = refs[2 * n:]

        def dev(sx, sy, sc):
            j = 2 * sx + sy
            sib = (sx, sy, 1 - sc)
            sends = []
            for w in range(n):
                mine = _piece_ref(outs[w], axes[w], j, sc)
                for r, (fx, fy) in enumerate(CHIP_RELS):
                    k = 6 * w + r
                    cp = _remote(mine, mine, send_sems.at[k], recv_sems.at[k], (sx ^ fx, sy ^ fy, sc))
                    cp.start()
                    sends.append(cp)
            for w in range(n):
                for r, (fx, fy) in enumerate(CHIP_RELS):
                    k = 6 * w + r
                    got = _piece_ref(outs[w], axes[w], 2 * (sx ^ fx) + (sy ^ fy), sc)
                    _remote(got, got, send_sems.at[k], recv_sems.at[k], (sx ^ fx, sy ^ fy, sc)).wait_recv()
                    cp = _remote(got, got, send_sems.at[k + 3], recv_sems.at[k + 3], sib)
                    cp.start()
                    sends.append(cp)
            for w in range(n):
                for r, (fx, fy) in enumerate(CHIP_RELS):
                    k = 6 * w + 3 + r
                    got = _piece_ref(outs[w], axes[w], 2 * (sx ^ fx) + (sy ^ fy), 1 - sc)
                    _remote(got, got, send_sems.at[k], recv_sems.at[k], sib).wait_recv()
            for cp in sends:
                cp.wait_send()

        _on_each_device(dev)

    return pl.pallas_call(
        body, name="gather_weights", in_specs=[ANY] * n, out_specs=[ANY] * n,
        out_shape=[SDS(f.shape, f.dtype) for f in fulls], input_output_aliases={i: i for i in range(n)},
        scratch_shapes=[pltpu.SemaphoreType.DMA((6 * n,)), pltpu.SemaphoreType.DMA((6 * n,))],
        compiler_params=COMM,
    )(*fulls)


class _Stage:
    def __init__(self, ins, outs, n_sems, make_plan, aliases=None):
        self.ins, self.outs, self.n_sems, self.make_plan = list(ins), list(outs), n_sems, make_plan
        self.aliases = dict(aliases or {})


def _start(plan):
    def dev(sx, sy, sc):
        for cp, _, _ in plan(sx, sy, sc):
            cp.start()

    _on_each_device(dev)


def _finish(plan):
    def dev(sx, sy, sc):
        for _, sent, got in plan(sx, sy, sc):
            sent.wait_send()
            got.wait_recv()

    _on_each_device(dev)


def _comm_call(stage, name):
    n_in, n_out = len(stage.ins), len(stage.outs)

    def body(*refs):
        plan = stage.make_plan(refs[:n_in], refs[n_in:n_in + n_out], *refs[n_in + n_out:])
        _start(plan)
        _finish(plan)

    return pl.pallas_call(
        body, name=name, in_specs=[ANY] * n_in, out_specs=[ANY] * n_out, out_shape=stage.outs,
        input_output_aliases=stage.aliases,
        scratch_shapes=[pltpu.SemaphoreType.DMA((stage.n_sems,)), pltpu.SemaphoreType.DMA((stage.n_sems,))],
        compiler_params=COMM,
    )(*stage.ins)


class _SemsFrom:
    def __init__(self, sems, base):
        self.sems, self.base = sems, base

    @property
    def at(self):
        return self

    def __getitem__(self, k):
        return self.sems.at[self.base + k]


def _both(a, b):
    na, nb = len(a.ins), len(b.ins)
    ma = len(a.outs)

    def make_plan(ins, outs, send_sems, recv_sems):
        pa = a.make_plan(ins[:na], outs[:ma], send_sems, recv_sems)
        pb = b.make_plan(ins[na:], outs[ma:], _SemsFrom(send_sems, a.n_sems), _SemsFrom(recv_sems, a.n_sems))
        return lambda sx, sy, sc: pa(sx, sy, sc) + pb(sx, sy, sc)

    aliases = dict(a.aliases)
    aliases.update({na + i: ma + o for i, o in b.aliases.items()})
    return _Stage(a.ins + b.ins, a.outs + b.outs, a.n_sems + b.n_sems, make_plan, aliases)


def _same(cp):
    return (cp, cp, cp)


def _stage_gather_send(fulls, axes):
    n = len(fulls)

    def make_plan(ins, outs, send_sems, recv_sems):
        def plan(sx, sy, sc):
            cps = []
            for w in range(n):
                mine = _piece_ref(outs[w], axes[w], 2 * sx + sy, sc)
                for r, (fx, fy) in enumerate(CHIP_RELS):
                    k = 3 * w + r
                    to = (sx ^ fx, sy ^ fy, sc)
                    got = _piece_ref(outs[w], axes[w], 2 * (sx ^ fx) + (sy ^ fy), sc)
                    send = _remote(mine, mine, send_sems.at[k], recv_sems.at[k], to)
                    cps.append((send, send, _remote(got, got, send_sems.at[k], recv_sems.at[k], to)))
            return cps

        return plan

    return _Stage(fulls, [SDS(f.shape, f.dtype) for f in fulls], 3 * n, make_plan, {i: i for i in range(n)})


def _stage_gather_forward(fulls, axes):
    n = len(fulls)

    def make_plan(ins, outs, send_sems, recv_sems):
        def plan(sx, sy, sc):
            cps = []
            sib = (sx, sy, 1 - sc)
            for w in range(n):
                for r, (fx, fy) in enumerate(CHIP_RELS):
                    k = 3 * w + r
                    pj = 2 * (sx ^ fx) + (sy ^ fy)
                    have = _piece_ref(outs[w], axes[w], pj, sc)
                    want = _piece_ref(outs[w], axes[w], pj, 1 - sc)
                    send = _remote(have, have, send_sems.at[k], recv_sems.at[k], sib)
                    cps.append((send, send, _remote(want, want, send_sems.at[k], recv_sems.at[k], sib)))
            return cps

        return plan

    return _Stage(fulls, [SDS(f.shape, f.dtype) for f in fulls], 3 * n, make_plan, {i: i for i in range(n)})


def _stage_pair_exchange(grads, axes):
    n = len(grads)

    def make_plan(gs, land, send_sems, recv_sems):
        def plan(sx, sy, sc):
            return [_same(_remote(_piece_ref(gs[w], axes[w], jj, 1 - sc), land[w].at[jj], send_sems.at[4 * w + jj],
                                  recv_sems.at[4 * w + jj], (sx, sy, 1 - sc)))
                    for w in range(n) for jj in range(4)]

        return plan

    return _Stage(grads, [SDS((4,) + _piece_shape(g.shape, a), F32) for g, a in zip(grads, axes)], 4 * n, make_plan)


def _stage_chip_scatter(pieces):
    n = len(pieces)

    def make_plan(ps, land, send_sems, recv_sems):
        def plan(sx, sy, sc):
            return [_same(_remote(ps[w].at[2 * (sx ^ fx) + (sy ^ fy)], land[w].at[r], send_sems.at[3 * w + r],
                                  recv_sems.at[3 * w + r], (sx ^ fx, sy ^ fy, sc)))
                    for w in range(n) for r, (fx, fy) in enumerate(CHIP_RELS)]

        return plan

    return _Stage(pieces, [SDS((3,) + p.shape[1:], p.dtype) for p in pieces], 3 * n, make_plan)


def _stage_pair_share(shards):
    n = len(shards)

    def make_plan(ins, outs, send_sems, recv_sems):
        def plan(sx, sy, sc):
            cps = []
            sib = (sx, sy, 1 - sc)
            for w in range(n):
                hr = shards[w].shape[0] // 2
                mine = outs[w].at[pl.ds(sc * hr, hr), :]
                theirs = outs[w].at[pl.ds((1 - sc) * hr, hr), :]
                send = _remote(mine, mine, send_sems.at[w], recv_sems.at[w], sib)
                cps.append((send, send, _remote(theirs, theirs, send_sems.at[w], recv_sems.at[w], sib)))
            return cps

        return plan

    return _Stage(shards, [SDS(g.shape, g.dtype) for g in shards], n, make_plan, {i: i for i in range(n)})


def _stage_gather_small(stack):
    def make_plan(ins, outs, send_sems, recv_sems):
        def plan(sx, sy, sc):
            mine = outs[0].at[4 * sx + 2 * sy + sc]
            return [_same(_remote(mine, mine, send_sems.at[k], recv_sems.at[k], (sx ^ fx, sy ^ fy, sc ^ fc)))
                    for k, (fx, fy, fc) in enumerate(DEV_RELS)]

        return plan

    return _Stage([stack], [SDS(stack.shape, stack.dtype)], 7, make_plan, {0: 0})


def kernel(x, w_in, conv_a_w, w_out_a, conv_b_w, conv_b_bias, ln_b_gamma, ln_b_beta, w_out_b, w_o, ln1_gamma, ln1_beta, w_up, w_down, ln2_gamma, ln2_beta, loss_target, m_w_in, m_conv_a_w, m_w_out_a, m_conv_b_w, m_conv_b_bias, m_ln_b_gamma, m_ln_b_beta, m_w_out_b, m_w_o, m_ln1_gamma, m_ln1_beta, m_w_up, m_w_down, m_ln2_gamma, m_ln2_beta, v_w_in, v_conv_a_w, v_w_out_a, v_conv_b_w, v_conv_b_bias, v_ln_b_gamma, v_ln_b_beta, v_w_out_b, v_w_o, v_ln1_gamma, v_ln1_beta, v_w_up, v_w_down, v_ln2_gamma, v_ln2_beta):
    s, d = x.shape[1], x.shape[2]
    xs = x.reshape(s, d)
    tgt = loss_target.reshape(s, d)
    dq = d // 4
    chip = 2 * lax.axis_index("x") + lax.axis_index("y")
    core = lax.axis_index("c")
    pos = jnp.stack([chip, core]).astype(jnp.int32)
    names = ("w_in", "w_out_a", "w_out_b", "w_o", "w_up", "w_down")
    axes = (1, 0, 0, 0, 1, 0)

    conv_pack = jnp.concatenate([jnp.pad(conv_a_w, ((0, 8 - K_A), (0, 0))), jnp.pad(conv_b_w, ((0, 32 - K_B), (0, 0))),
                                 jnp.zeros((8, dq), F32)], axis=0)
    conv_full = lax.dynamic_update_slice(jnp.zeros((conv_pack.shape[0], d), F32), conv_pack, (0, chip * dq))
    fulls = [_place_cast(w, a, pos, "place_" + nm)
             for w, a, nm in zip((w_in, w_out_a, w_out_b, w_o, w_up, w_down), axes, names)]
    wi, convs = _gather_weights([fulls[0], conv_full], (1, 1))
    cwa, cwb = convs[0:K_A], convs[8:8 + K_B]
    vec = lambda a: a.reshape(1, d)
    bias_b, lbg, lbb = vec(conv_b_bias), vec(ln_b_gamma), vec(ln_b_beta)
    l1g, l1b, l2g, l2b = vec(ln1_gamma), vec(ln1_beta), vec(ln2_gamma), vec(ln2_beta)

    (p, xb), rest = _in_proj(xs, wi, _stage_gather_send(fulls[1:], axes[1:]))
    (conva, yapre, nb, rstdb, u3), rest = _conv_fwd(p, cwa, cwb, bias_b, lbg, lbb, d,
                                                    _stage_gather_forward(rest, axes[1:]))
    woa, wob, wo, wup, wdown = rest
    ya, yb, merged, n1, rstd1 = _mixer_out(yapre, u3, p, xs, woa, wob, wo, d)
    x1b, hb, dhpre, dr2b, dr1, dr1b, acc_mlp = _mlp(n1, rstd1, tgt, wup, wdown, l1g, l1b, l2g, l2b)
    g_up, _ = _grad_w(x1b, dhpre, "grad_w_up")
    g_down, _ = _grad_w(hb, dr2b, "grad_w_down")
    dya, dyb, dba, dga, dgb, dca, du1, acc_mix = _mixer_bwd_local(dr1b, p, ya, yb, conva, nb, rstdb, wo, woa, wob,
                                                                   lbg, lbb)
    g_oa, _ = _grad_w(yapre, dya, "grad_w_out_a")
    g_ob, _ = _grad_w(u3, dyb, "grad_w_out_b")
    g_o, _ = _grad_w(merged, dr1b, "grad_w_o")

    early, e_axes, e_names = [g_oa, g_ob, g_o, g_up, g_down], axes[1:], names[1:]
    (dp, dcwa, dcwb), land1 = _conv_bwd(dca, du1, p, dba, dga, dgb, cwa, cwb, d, _stage_pair_exchange(early, e_axes))
    pieces = [_pair_add(g, l, a, pos, "pair_add_" + nm) for g, l, a, nm in zip(early, land1, e_axes, e_names)]
    pack = jnp.concatenate([dcwa, dcwb, acc_mix, acc_mlp], axis=0)
    stack = lax.dynamic_update_slice(jnp.zeros((8,) + pack.shape, F32), pack[None], (2 * chip + core, 0, 0))
    g_wi, landed = _grad_w(xb, dp, "grad_w_in", _both(_stage_chip_scatter(pieces), _stage_gather_small(stack)))
    land2, stack = landed[:-1], landed[-1]
    halves = [_chip_sum(g, l1, l2, a, pos, "chip_sum_" + nm)
              for g, l1, l2, a, nm in zip(early, land1, land2, e_axes, e_names)]
    (land1_in,) = _comm_call(_stage_pair_exchange([g_wi], (1,)), "pair_exchange_w_in")
    piece_in = _pair_add(g_wi, land1_in, 1, pos, "pair_add_w_in")
    grad_x, (land2_in,) = _grad_x(dr1, dp, wi, _stage_chip_scatter([piece_in]))
    half_in = _chip_sum(g_wi, land1_in, land2_in, 1, pos, "chip_sum_w_in")
    g_in, g_oa, g_ob, g_o, g_up, g_down = _comm_call(_stage_pair_share([half_in] + halves), "pair_share")
    small = _sum_parts(stack, "small_sum")
    g_ca = lax.dynamic_slice(small, (0, chip * dq), (K_A, dq))
    g_cb = lax.dynamic_slice(small, (8, chip * dq), (K_B, dq))
    g_vec = jnp.stack([small[r] for r in (42, 40, 41, 51, 52, 49, 50)])

    loss = lax.psum((0.5 / d) * jnp.sum(acc_mlp[0]), ("x", "y", "c"))

    big = {}
    for name, w, g, m, v in (("w_in", w_in, g_in, m_w_in, v_w_in), ("w_out_a", w_out_a, g_oa, m_w_out_a, v_w_out_a),
                             ("w_out_b", w_out_b, g_ob, m_w_out_b, v_w_out_b), ("w_o", w_o, g_o, m_w_o, v_w_o),
                             ("w_up", w_up, g_up, m_w_up, v_w_up), ("w_down", w_down, g_down, m_w_down, v_w_down),
                             ("conv_a_w", conv_a_w, g_ca, m_conv_a_w, v_conv_a_w),
                             ("conv_b_w", conv_b_w, g_cb, m_conv_b_w, v_conv_b_w)):
        big[name] = (g,) + tuple(_adamw(w, g, m, v, "adamw_" + name))
    vec_names = ("conv_b_bias", "ln_b_gamma", "ln_b_beta", "ln1_gamma", "ln1_beta", "ln2_gamma", "ln2_beta")
    w7 = jnp.stack([conv_b_bias, ln_b_gamma, ln_b_beta, ln1_gamma, ln1_beta, ln2_gamma, ln2_beta])
    m7 = jnp.stack([m_conv_b_bias, m_ln_b_gamma, m_ln_b_beta, m_ln1_gamma, m_ln1_beta, m_ln2_gamma, m_ln2_beta])
    v7 = jnp.stack([v_conv_b_bias, v_ln_b_gamma, v_ln_b_beta, v_ln1_gamma, v_ln1_beta, v_ln2_gamma, v_ln2_beta])
    d7, nm7, nv7 = _adamw(w7, g_vec, m7, v7, "adamw_vectors")
    for q, name in enumerate(vec_names):
        big[name] = (g_vec[q], d7[q], nm7[q], nv7[q])

    order = ("w_in", "conv_a_w", "w_out_a", "conv_b_w", "conv_b_bias", "ln_b_gamma", "ln_b_beta", "w_out_b", "w_o",
             "ln1_gamma", "ln1_beta", "w_up", "w_down", "ln2_gamma", "ln2_beta")
    outs = [loss, grad_x.reshape(x.shape)]
    for part in range(4):
        outs += [big[name][part] for name in order]
    return tuple(outs)
```

```python
import jax
import jax.numpy as jnp
from jax import lax
from jax.experimental import pallas as pl
from jax.experimental.pallas import tpu as pltpu

F32 = jnp.float32
BF = jnp.bfloat16
SDS = jax.ShapeDtypeStruct
MESH = pl.DeviceIdType.MESH

ALPHA = 2.0 ** 0.25
LN_EPS = 1e-5
K_A = 3
K_B = 31
HALO = 32
CONV_ROWS = 64
CONV_LANES = 128
ADAM_LR = 0.001
ADAM_B1 = 0.9
ADAM_B2 = 0.999
ADAM_EPS = 1e-08
ADAM_WD = 0.01
ADAM_STEP = 10
P_DT = BF
CHIP_RELS = ((1, 0), (0, 1), (1, 1))
DEV_RELS = tuple((fx, fy, fc) for fx in (0, 1) for fy in (0, 1) for fc in (0, 1))[1:]


def _cp(sem=None, vmem_mb=56, side_effects=False):
    return pltpu.CompilerParams(dimension_semantics=sem, vmem_limit_bytes=vmem_mb << 20,
                                has_side_effects=side_effects)


def _const(shape):
    return pl.BlockSpec(shape, lambda *_: (0,) * len(shape), pipeline_mode=pl.Buffered(1))


def _sig(v):
    return jax.nn.sigmoid(v)


def _ln_fwd(r):
    mu = jnp.mean(r, axis=-1, keepdims=True)
    xc = r - mu
    var = jnp.mean(xc * xc, axis=-1, keepdims=True)
    rstd = lax.rsqrt(var + LN_EPS)
    return xc * rstd, rstd


def _ln_bwd(dn, n, rstd):
    m1 = jnp.mean(dn, axis=-1, keepdims=True)
    m2 = jnp.mean(dn * n, axis=-1, keepdims=True)
    return rstd * (dn - m1 - n * m2)


def _dot(a, b):
    return jnp.dot(a, b, preferred_element_type=F32)


def _dot_nt(a, b):
    return lax.dot_general(a, b, (((1,), (1,)), ((), ())), preferred_element_type=F32)


def _dot_tn(a, b):
    return lax.dot_general(a, b, (((0,), (0,)), ((), ())), preferred_element_type=F32)


def _tile(n, pref):
    if n <= pref:
        return n
    return max(t for t in range(128, pref + 1, 128) if n % t == 0)


def _rowsum8(v):
    acc = v[0:8]
    for g in range(1, v.shape[0] // 8):
        acc = acc + v[8 * g:8 * g + 8]
    return acc


def _taps(win, offsets, rows):
    r_all = win.shape[0]
    by_res = {}
    for k, o in enumerate(offsets):
        by_res.setdefault(o % 8, []).append((k, o // 8))
    for s, taps in sorted(by_res.items()):
        r = win if s == 0 else pltpu.roll(win, r_all - s, 0)
        for k, q in taps:
            yield k, r[8 * q:8 * q + rows]


CAUSAL_A = [HALO - (K_A - 1) + k for k in range(K_A)]
CAUSAL_B = [HALO - (K_B - 1) + k for k in range(K_B)]
ANTI_A = [K_A - 1 - k for k in range(K_A)]
ANTI_B = [K_B - 1 - k for k in range(K_B)]


def _host_call(body, *, name, grid, in_specs, out_specs, out_shape, scratch_shapes, args, hosted, prefetch=None,
               aliases=None):
    n_in, n_out, n_scr = len(in_specs), len(out_specs), len(scratch_shapes)
    n_pre = 0 if prefetch is None else 1
    pre = () if prefetch is None else (prefetch,)
    sem = ("arbitrary",) * len(grid)
    own_aliases = {n_pre + a: b for a, b in (aliases or {}).items()}

    def call(kernel_body, ins, outs, shapes, scratch, all_aliases, side_effects, operands):
        gs = pltpu.PrefetchScalarGridSpec(num_scalar_prefetch=n_pre, grid=grid, in_specs=ins, out_specs=outs,
                                          scratch_shapes=scratch)
        return pl.pallas_call(kernel_body, name=name, grid_spec=gs, out_shape=shapes,
                              input_output_aliases=all_aliases,
                              compiler_params=_cp(sem, side_effects=side_effects))(*pre, *operands)

    if hosted is None:
        res = call(body, list(in_specs), list(out_specs), list(out_shape), list(scratch_shapes), own_aliases, False,
                   args)
        return list(res), []
    h_in, h_out = len(hosted.ins), len(hosted.outs)

    def full_body(*refs):
        pre_refs, refs = refs[:n_pre], refs[n_pre:]
        ins, refs = refs[:n_in], refs[n_in:]
        hins, refs = refs[:h_in], refs[h_in:]
        outs, refs = refs[:n_out], refs[n_out:]
        houts, refs = refs[:h_out], refs[h_out:]
        scr, (send_sems, recv_sems) = refs[:n_scr], refs[n_scr:]
        plan = hosted.make_plan(hins, houts, send_sems, recv_sems)
        first = last = None
        for a, size in enumerate(grid):
            at0, at1 = pl.program_id(a) == 0, pl.program_id(a) == size - 1
            first = at0 if first is None else jnp.logical_and(first, at0)
            last = at1 if last is None else jnp.logical_and(last, at1)

        @pl.when(first)
        def _():
            _start(plan)

        body(*pre_refs, *ins, *outs, *scr)

        @pl.when(last)
        def _():
            _finish(plan)

    all_aliases = dict(own_aliases)
    all_aliases.update({n_pre + n_in + a: n_out + b for a, b in hosted.aliases.items()})
    res = call(full_body, list(in_specs) + [ANY] * h_in, list(out_specs) + [ANY] * h_out,
               list(out_shape) + list(hosted.outs),
               list(scratch_shapes) + [pltpu.SemaphoreType.DMA((hosted.n_sems,)),
                                       pltpu.SemaphoreType.DMA((hosted.n_sems,))],
               all_aliases, True, (*args, *hosted.ins))
    return list(res[:n_out]), list(res[n_out:])


def _in_proj_own(x, w_shard, pos, hosted):
    s, d = x.shape
    tn = w_shard.shape[1]
    tm = min(s, 1024)

    def body(pos_ref, x_ref, w_ref, p_ref, xb_ref, wb):
        @pl.when(pl.program_id(0) == 0)
        def _():
            wb[...] = w_ref[...].astype(BF)

        xb = x_ref[...].astype(BF)
        xb_ref[...] = xb
        p_ref[...] = _dot(xb, wb[...]).astype(p_ref.dtype)

    return _host_call(
        body, name="in_proj_own", grid=(s // tm,), prefetch=pos,
        in_specs=[pl.BlockSpec((tm, d), lambda i, pos: (i, 0)),
                  pl.BlockSpec((d, tn), lambda i, pos: (0, 0), pipeline_mode=pl.Buffered(1))],
        out_specs=[pl.BlockSpec((tm, tn), lambda i, pos: (i, pos[0])), pl.BlockSpec((tm, d), lambda i, pos: (i, 0))],
        out_shape=[SDS((s, 4 * tn), P_DT), SDS((s, d), BF)], scratch_shapes=[pltpu.VMEM((d, tn), BF)],
        args=(x, w_shard), hosted=hosted)


def _in_proj_rest(xb, wi, p, pos, hosted):
    s, d = xb.shape
    n = wi.shape[1]
    tm, tn = min(s, 1024), n // 4

    def body(pos_ref, xb_ref, w_ref, p_in, p_ref):
        p_ref[...] = _dot(xb_ref[...], w_ref[...]).astype(p_ref.dtype)

    col = lambda i, j, pos: (pos[0] + 1 + j) % 4
    (p,), extra = _host_call(
        body, name="in_proj_rest", grid=(s // tm, 3), prefetch=pos, aliases={2: 0},
        in_specs=[pl.BlockSpec((tm, d), lambda i, j, pos: (i, 0)),
                  pl.BlockSpec((d, tn), lambda i, j, pos: (0, col(i, j, pos))), ANY],
        out_specs=[pl.BlockSpec((tm, tn), lambda i, j, pos: (i, col(i, j, pos)))],
        out_shape=[SDS((s, n), P_DT)], scratch_shapes=[], args=(xb, wi, p), hosted=hosted)
    return p, extra


def _col_spec(tm, d, k):
    return pl.BlockSpec((tm, d), lambda i, k=k: (i, k))


def _prev_halo_spec(tm, d, k):
    r = tm // HALO
    return pl.BlockSpec((HALO, d), lambda i, k=k: (jnp.maximum(i * r - 1, 0), k))


def _next_halo_spec(tm, d, k, s):
    r = tm // HALO
    last = s // HALO - 1
    return pl.BlockSpec((HALO, d), lambda i, k=k: (jnp.minimum((i + 1) * r, last), k))


def _conv_fwd(p, cwa, cwb, bias_b, lbg, lbb, d, hosted=None):
    s = p.shape[0]
    tm = min(s, 256)
    nt = s // tm

    def body(ba_ref, ca_ref, va_ref, vb_ref, gb_ref, hca_ref, hva_ref, hvb_ref, hgb_ref,
             cwa_ref, cwb_ref, bias_ref, lbg_ref, lbb_ref,
             conva_ref, yapre_ref, nb_ref, rstdb_ref, u3_ref,
             zbuf, ubuf, u1buf):
        i = pl.program_id(0)
        keep = (i > 0).astype(F32)
        zbuf[pl.ds(0, HALO), :] = hca_ref[...].astype(F32) * hva_ref[...].astype(F32) * keep
        ubuf[pl.ds(0, HALO), :] = hvb_ref[...].astype(F32) * _sig(hgb_ref[...].astype(F32)) * keep
        zbuf[pl.ds(HALO, tm), :] = ca_ref[...].astype(F32) * va_ref[...].astype(F32)
        ubuf[pl.ds(HALO, tm), :] = vb_ref[...].astype(F32) * _sig(gb_ref[...].astype(F32))

        def chunk(j, carry):
            r0 = pl.multiple_of(j * CONV_ROWS, CONV_ROWS)
            rows = pl.ds(r0, CONV_ROWS)
            for lc in range(d // CONV_LANES):
                ls = pl.ds(lc * CONV_LANES, CONV_LANES)
                acc = jnp.zeros((CONV_ROWS, CONV_LANES), F32)
                for k, sl in _taps(zbuf[pl.ds(r0, CONV_ROWS + HALO), ls], CAUSAL_A, CONV_ROWS):
                    acc = acc + cwa_ref[pl.ds(k, 1), ls] * sl
                conva_ref[rows, ls] = acc
                yapre_ref[rows, ls] = (ba_ref[rows, ls].astype(F32) * acc).astype(BF)
                acc = jnp.zeros((CONV_ROWS, CONV_LANES), F32)
                for k, sl in _taps(ubuf[pl.ds(r0, CONV_ROWS + HALO), ls], CAUSAL_B, CONV_ROWS):
                    acc = acc + cwb_ref[pl.ds(k, 1), ls] * sl
                u1buf[rows, ls] = acc + bias_ref[:, ls]
            return carry

        lax.fori_loop(0, tm // CONV_ROWS, chunk, 0)
        nb, rstd = _ln_fwd(u1buf[...])
        nb_ref[...] = nb
        rstdb_ref[...] = rstd
        u2 = nb * lbg_ref[...] + lbb_ref[...]
        u3_ref[...] = (u2 * _sig(u2)).astype(BF)

    vec = _const((1, d))
    return _host_call(
        body, name="conv_fwd", grid=(nt,),
        in_specs=[_col_spec(tm, d, k) for k in range(5)] + [_prev_halo_spec(tm, d, k) for k in (1, 2, 3, 4)]
        + [_const((K_A, d)), _const((K_B, d)), vec, vec, vec],
        out_specs=[pl.BlockSpec((tm, d), lambda i: (i, 0)), pl.BlockSpec((tm, d), lambda i: (i, 0)),
                   pl.BlockSpec((tm, d), lambda i: (i, 0)), pl.BlockSpec((tm, 1), lambda i: (i, 0)),
                   pl.BlockSpec((tm, d), lambda i: (i, 0))],
        out_shape=[SDS((s, d), F32), SDS((s, d), BF), SDS((s, d), F32), SDS((s, 1), F32), SDS((s, d), BF)],
        scratch_shapes=[pltpu.VMEM((HALO + tm, d), F32), pltpu.VMEM((HALO + tm, d), F32), pltpu.VMEM((tm, d), F32)],
        args=(p, p, p, p, p, p, p, p, p, cwa, cwb, bias_b, lbg, lbb), hosted=hosted)


def _mixer_out(yapre, u3, p, x, woa, wob, wo, d):
    s = x.shape[0]
    tm = min(s, 512)

    def body(yapre_ref, u3_ref, ga_ref, gb_ref, x_ref, woa_ref, wob_ref, wo_ref,
             ya_ref, yb_ref, merged_ref, n1_ref, rstd1_ref):
        ya = _dot(yapre_ref[...], woa_ref[...])
        yb = _dot(u3_ref[...], wob_ref[...])
        ya_ref[...] = ya
        yb_ref[...] = yb
        merged = (_sig(ga_ref[...].astype(F32)) * ya + _sig(gb_ref[...].astype(F32)) * yb).astype(BF)
        merged_ref[...] = merged
        r1 = F32(ALPHA) * x_ref[...] + _dot(merged, wo_ref[...])
        n1, rstd1 = _ln_fwd(r1)
        n1_ref[...] = n1
        rstd1_ref[...] = rstd1

    row = pl.BlockSpec((tm, d), lambda i: (i, 0))
    return pl.pallas_call(
        body, name="mixer_out", grid=(s // tm,),
        in_specs=[row, row, _col_spec(tm, d, 5), _col_spec(tm, d, 6), row,
                  _const((d, d)), _const((d, d)), _const((d, d))],
        out_specs=[row, row, row, row, pl.BlockSpec((tm, 1), lambda i: (i, 0))],
        out_shape=[SDS((s, d), F32), SDS((s, d), F32), SDS((s, d), BF), SDS((s, d), F32), SDS((s, 1), F32)],
        compiler_params=_cp(("parallel",)),
    )(yapre, u3, p, p, x, woa, wob, wo)


def _mlp(n1, rstd1, tgt, wup, wdown, l1g, l1b, l2g, l2b):
    s, d = n1.shape
    dff = wup.shape[1]
    tm = min(s, 256)
    fc = min(dff, 1024)
    nq = dff // fc

    def body(n1_ref, rstd1_ref, tgt_ref, wup_ref, wdown_ref, l1g_ref, l1b_ref, l2g_ref, l2b_ref,
             x1b_ref, hb_ref, dhpre_ref, dr2b_ref, dr1_ref, dr1b_ref, acc_ref, rbuf):
        i = pl.program_id(0)
        n1v = n1_ref[...]
        x1 = n1v * l1g_ref[...] + l1b_ref[...]
        x1b = x1.astype(BF)
        x1b_ref[...] = x1b
        ff = jnp.zeros((tm, d), F32)
        for q in range(nq):
            cs = pl.ds(q * fc, fc)
            r = jnp.maximum(_dot(x1b, wup_ref[:, cs]), 0.0)
            rbuf[:, cs] = r
            hq = (r * r).astype(BF)
            hb_ref[:, cs] = hq
            ff = ff + _dot(hq, wdown_ref[cs, :])
        n2, rstd2 = _ln_fwd(F32(ALPHA) * x1 + ff)
        x2 = n2 * l2g_ref[...] + l2b_ref[...]
        err = x2 - tgt_ref[...]
        dx2 = err * F32(1.0 / d)
        dr2 = _ln_bwd(dx2 * l2g_ref[...], n2, rstd2)
        dr2b = dr2.astype(BF)
        dr2b_ref[...] = dr2b
        dx1 = F32(ALPHA) * dr2
        for q in range(nq):
            cs = pl.ds(q * fc, fc)
            dh = _dot_nt(dr2b, wdown_ref[cs, :])
            dhp = (dh * (2.0 * rbuf[:, cs])).astype(BF)
            dhpre_ref[:, cs] = dhp
            dx1 = dx1 + _dot_nt(dhp, wup_ref[:, cs])
        dr1 = _ln_bwd(dx1 * l1g_ref[...], n1v, rstd1_ref[...])
        dr1_ref[...] = dr1
        dr1b_ref[...] = dr1.astype(BF)

        @pl.when(i == 0)
        def _():
            acc_ref[...] = jnp.zeros_like(acc_ref)

        for q, val in enumerate((err * err, dx2 * n2, dx2, dx1 * n1v, dx1)):
            acc_ref[pl.ds(q, 1), :] += jnp.sum(val, axis=0, keepdims=True)

    row = pl.BlockSpec((tm, d), lambda i: (i, 0))
    wide = pl.BlockSpec((tm, dff), lambda i: (i, 0))
    vec = _const((1, d))
    return pl.pallas_call(
        body, name="mlp_fwd_bwd", grid=(s // tm,),
        in_specs=[row, pl.BlockSpec((tm, 1), lambda i: (i, 0)), row, _const((d, dff)), _const((dff, d)),
                  vec, vec, vec, vec],
        out_specs=[row, wide, wide, row, row, row, pl.BlockSpec((8, d), lambda i: (0, 0))],
        out_shape=[SDS((s, d), BF), SDS((s, dff), BF), SDS((s, dff), BF), SDS((s, d), BF), SDS((s, d), F32),
                   SDS((s, d), BF), SDS((8, d), F32)],
        scratch_shapes=[pltpu.VMEM((tm, dff), F32)],
        compiler_params=_cp(("arbitrary",)),
    )(n1, rstd1, tgt, wup, wdown, l1g, l1b, l2g, l2b)


def _mixer_bwd_local(dr1b, p, ya, yb, conva, nb, rstdb, wo, woa, wob, lbg, lbb):
    s, d = ya.shape
    tm = min(s, 256)

    def body(dr1b_ref, ba_ref, ga_ref, gb_ref, ya_ref, yb_ref, conva_ref, nb_ref, rstdb_ref,
             wo_ref, woa_ref, wob_ref, lbg_ref, lbb_ref,
             dya_ref, dyb_ref, dba_ref, dga_ref, dgb_ref, dca_ref, du1_ref, acc_ref):
        i = pl.program_id(0)
        dmerged = _dot_nt(dr1b_ref[...], wo_ref[...])
        sa = _sig(ga_ref[...].astype(F32))
        sb = _sig(gb_ref[...].astype(F32))
        dya = (dmerged * sa).astype(BF)
        dyb = (dmerged * sb).astype(BF)
        dya_ref[...] = dya
        dyb_ref[...] = dyb
        dga_ref[...] = (dmerged * ya_ref[...] * (sa * (1.0 - sa))).astype(BF)
        dgb_ref[...] = (dmerged * yb_ref[...] * (sb * (1.0 - sb))).astype(BF)
        dyapre = _dot_nt(dya, woa_ref[...])
        dba_ref[...] = (dyapre * conva_ref[...]).astype(BF)
        dca_ref[...] = dyapre * ba_ref[...].astype(F32)
        du3 = _dot_nt(dyb, wob_ref[...])
        nbv = nb_ref[...]
        u2 = nbv * lbg_ref[...] + lbb_ref[...]
        sg = _sig(u2)
        du2 = du3 * (sg * (1.0 + u2 * (1.0 - sg)))
        du1 = _ln_bwd(du2 * lbg_ref[...], nbv, rstdb_ref[...])
        du1_ref[...] = du1

        @pl.when(i == 0)
        def _():
            acc_ref[...] = jnp.zeros_like(acc_ref)

        for q, val in enumerate((du2 * nbv, du2, du1)):
            acc_ref[pl.ds(q, 1), :] += jnp.sum(val, axis=0, keepdims=True)

    row = pl.BlockSpec((tm, d), lambda i: (i, 0))
    vec = _const((1, d))
    return pl.pallas_call(
        body, name="mixer_bwd_local", grid=(s // tm,),
        in_specs=[row, _col_spec(tm, d, 0), _col_spec(tm, d, 5), _col_spec(tm, d, 6), row, row, row, row,
                  pl.BlockSpec((tm, 1), lambda i: (i, 0)), _const((d, d)), _const((d, d)), _const((d, d)), vec, vec],
        out_specs=[row, row, row, row, row, row, row, pl.BlockSpec((8, d), lambda i: (0, 0))],
        out_shape=[SDS((s, d), BF)] * 5 + [SDS((s, d), F32), SDS((s, d), F32), SDS((8, d), F32)],
        compiler_params=_cp(("arbitrary",)),
    )(dr1b, p, p, p, ya, yb, conva, nb, rstdb, wo, woa, wob, lbg, lbb)


def _conv_bwd(dca, du1, p, dba, dga, dgb, cwa, cwb, d, hosted=None):
    s = dca.shape[0]
    tm = min(s, 256)
    nt = s // tm

    def body(dca_ref, du1_ref, ndca_ref, ndu1_ref, ca_ref, va_ref, vb_ref, gb_ref,
             dba_ref, dga_ref, dgb_ref, cwa_ref, cwb_ref,
             dp_ref, dcwa_ref, dcwb_ref,
             dcabuf, du1buf, sgbuf, acca, accb):
        i = pl.program_id(0)
        keep_next = (i < nt - 1).astype(F32)

        @pl.when(i == 0)
        def _():
            acca[...] = jnp.zeros_like(acca)
            accb[...] = jnp.zeros_like(accb)

        sgbuf[...] = _sig(gb_ref[...].astype(F32))
        dcabuf[pl.ds(0, tm), :] = dca_ref[...]
        dcabuf[pl.ds(tm, HALO), :] = ndca_ref[...] * keep_next
        du1buf[pl.ds(0, tm), :] = du1_ref[...]
        du1buf[pl.ds(tm, HALO), :] = ndu1_ref[...] * keep_next
        dp_ref[:, pl.ds(0, d)] = dba_ref[...]
        dp_ref[:, pl.ds(5 * d, d)] = dga_ref[...]
        dp_ref[:, pl.ds(6 * d, d)] = dgb_ref[...]

        def chunk(j, carry):
            r0 = pl.multiple_of(j * CONV_ROWS, CONV_ROWS)
            rows = pl.ds(r0, CONV_ROWS)
            for lc in range(d // CONV_LANES):
                lo = lc * CONV_LANES
                ls = pl.ds(lo, CONV_LANES)
                cac = ca_ref[rows, ls].astype(F32)
                vac = va_ref[rows, ls].astype(F32)
                zc = cac * vac
                acc = jnp.zeros((CONV_ROWS, CONV_LANES), F32)
                for k, sl in _taps(dcabuf[pl.ds(r0, CONV_ROWS + HALO), ls], ANTI_A, CONV_ROWS):
                    acc = acc + cwa_ref[pl.ds(k, 1), ls] * sl
                    acca[pl.ds(8 * k, 8), ls] += _rowsum8(sl * zc)
                dp_ref[rows, pl.ds(d + lo, CONV_LANES)] = (acc * vac).astype(BF)
                dp_ref[rows, pl.ds(2 * d + lo, CONV_LANES)] = (acc * cac).astype(BF)
                sgc = sgbuf[rows, ls]
                vbc = vb_ref[rows, ls].astype(F32)
                uc = vbc * sgc
                acc = jnp.zeros((CONV_ROWS, CONV_LANES), F32)
                for k, sl in _taps(du1buf[pl.ds(r0, CONV_ROWS + HALO), ls], ANTI_B, CONV_ROWS):
                    acc = acc + cwb_ref[pl.ds(k, 1), ls] * sl
                    accb[pl.ds(8 * k, 8), ls] += _rowsum8(sl * uc)
                dp_ref[rows, pl.ds(3 * d + lo, CONV_LANES)] = (acc * sgc).astype(BF)
                dp_ref[rows, pl.ds(4 * d + lo, CONV_LANES)] = (acc * vbc * (sgc * (1.0 - sgc))).astype(BF)
            return carry

        lax.fori_loop(0, tm // CONV_ROWS, chunk, 0)

        @pl.when(i == nt - 1)
        def _():
            dcwa_ref[...] = jnp.zeros_like(dcwa_ref)
            dcwb_ref[...] = jnp.zeros_like(dcwb_ref)
            for k in range(K_A):
                dcwa_ref[pl.ds(k, 1), :] = jnp.sum(acca[pl.ds(8 * k, 8), :], axis=0, keepdims=True)
            for k in range(K_B):
                dcwb_ref[pl.ds(k, 1), :] = jnp.sum(accb[pl.ds(8 * k, 8), :], axis=0, keepdims=True)

    row = pl.BlockSpec((tm, d), lambda i: (i, 0))
    nxt = _next_halo_spec(tm, d, 0, s)
    return _host_call(
        body, name="conv_bwd", grid=(nt,),
        in_specs=[row, row, nxt, nxt] + [_col_spec(tm, d, k) for k in (1, 2, 3, 4)]
        + [row, row, row, _const((K_A, d)), _const((K_B, d))],
        out_specs=[pl.BlockSpec((tm, 7 * d), lambda i: (i, 0)), pl.BlockSpec((8, d), lambda i: (0, 0)),
                   pl.BlockSpec((32, d), lambda i: (0, 0))],
        out_shape=[SDS((s, 7 * d), BF), SDS((8, d), F32), SDS((32, d), F32)],
        scratch_shapes=[pltpu.VMEM((tm + HALO, d), F32), pltpu.VMEM((tm + HALO, d), F32),
                        pltpu.VMEM((tm, d), F32), pltpu.VMEM((8 * K_A, d), F32), pltpu.VMEM((8 * K_B, d), F32)],
        args=(dca, du1, dca, du1, p, p, p, p, dba, dga, dgb, cwa, cwb), hosted=hosted)


def _grad_w(a, b, name, hosted=None):
    s, m = a.shape
    n = b.shape[1]
    tm, tn, tk = _tile(m, 1024), _tile(n, 1024), _tile(s, 2048)

    def body(a_ref, b_ref, o_ref):
        @pl.when(pl.program_id(2) == 0)
        def _():
            o_ref[...] = jnp.zeros_like(o_ref)

        o_ref[...] += _dot_tn(a_ref[...], b_ref[...])

    (g,), extra = _host_call(
        body, name=name, grid=(m // tm, n // tn, s // tk),
        in_specs=[pl.BlockSpec((tk, tm), lambda i, j, k: (k, i)), pl.BlockSpec((tk, tn), lambda i, j, k: (k, j))],
        out_specs=[pl.BlockSpec((tm, tn), lambda i, j, k: (i, j))],
        out_shape=[SDS((m, n), F32)], scratch_shapes=[], args=(a, b), hosted=hosted)
    return g, extra


def _grad_x(dr1, dp, wi, hosted=None):
    s, d = dr1.shape
    n = wi.shape[1]
    tm, tk = min(s, 512), _tile(n, 3584)

    def body(dr1_ref, dp_ref, w_ref, o_ref):
        @pl.when(pl.program_id(1) == 0)
        def _():
            o_ref[...] = F32(ALPHA) * dr1_ref[...]

        o_ref[...] += _dot_nt(dp_ref[...], w_ref[...])

    (gx,), extra = _host_call(
        body, name="grad_x", grid=(s // tm, n // tk),
        in_specs=[pl.BlockSpec((tm, d), lambda i, k: (i, 0)), pl.BlockSpec((tm, tk), lambda i, k: (i, k)),
                  pl.BlockSpec((d, tk), lambda i, k: (0, k))],
        out_specs=[pl.BlockSpec((tm, d), lambda i, k: (i, 0))],
        out_shape=[SDS((s, d), F32)], scratch_shapes=[], args=(dr1, dp, wi), hosted=hosted)
    return gx, extra


def _adamw_math(w, g, m, v):
    m2 = ADAM_B1 * m + (1.0 - ADAM_B1) * g
    v2 = ADAM_B2 * v + (1.0 - ADAM_B2) * (g * g)
    m_hat = m2 / (1.0 - ADAM_B1 ** ADAM_STEP)
    v_hat = v2 / (1.0 - ADAM_B2 ** ADAM_STEP)
    delta = -ADAM_LR * (m_hat / (jnp.sqrt(v_hat) + ADAM_EPS) + ADAM_WD * w)
    return delta, m2, v2


def _adamw(w, g, m, v, name):
    r, c = w.shape
    tr = r if r <= 256 else 256

    def body(w_ref, g_ref, m_ref, v_ref, d_ref, m2_ref, v2_ref):
        delta, m2, v2 = _adamw_math(w_ref[...], g_ref[...], m_ref[...], v_ref[...])
        d_ref[...] = delta
        m2_ref[...] = m2
        v2_ref[...] = v2

    blk = pl.BlockSpec((tr, c), lambda i: (i, 0))
    return pl.pallas_call(
        body, name=name, grid=(r // tr,), in_specs=[blk] * 4, out_specs=[blk] * 3,
        out_shape=[SDS((r, c), F32)] * 3, compiler_params=_cp(("parallel",)),
    )(w, g, m, v)


def _sum_parts(parts, name):
    k, r, c = parts.shape

    def body(p_ref, o_ref):
        acc = p_ref[0]
        for q in range(1, k):
            acc = acc + p_ref[q]
        o_ref[...] = acc

    return pl.pallas_call(
        body, name=name, grid=(1,),
        in_specs=[pl.BlockSpec((k, r, c), lambda i: (0, 0, 0))],
        out_specs=pl.BlockSpec((r, c), lambda i: (0, 0)),
        out_shape=SDS((r, c), F32), compiler_params=_cp(("arbitrary",)),
    )(parts)


def _piece_shape(full_shape, axis):
    r, c = full_shape
    return (r // 2, c // 4) if axis == 1 else (r // 8, c)


def _piece_spec(full_shape, axis, tr, chip_of, half_of):
    hr, wc = _piece_shape(full_shape, axis)
    nb = hr // tr
    if axis == 1:
        return pl.BlockSpec((tr, wc), lambda *a: (half_of(*a) * nb + a[-2], chip_of(*a)))
    return pl.BlockSpec((tr, wc), lambda *a: ((2 * chip_of(*a) + half_of(*a)) * nb + a[-2], 0))


def _place_cast(w, axis, pos, name):
    r, c = w.shape
    tr = min(r, 256)
    nb = r // tr
    full = (r, 4 * c) if axis == 1 else (4 * r, c)
    out_map = (lambda i, pos: (i, pos[0])) if axis == 1 else (lambda i, pos: (pos[0] * nb + i, 0))

    def body(pos_ref, w_ref, o_ref):
        o_ref[...] = w_ref[...].astype(o_ref.dtype)

    gs = pltpu.PrefetchScalarGridSpec(
        num_scalar_prefetch=1, grid=(nb,),
        in_specs=[pl.BlockSpec((tr, c), lambda i, pos: (i, 0))], out_specs=pl.BlockSpec((tr, c), out_map))
    return pl.pallas_call(body, name=name, grid_spec=gs, out_shape=SDS(full, BF),
                          compiler_params=_cp(("arbitrary",)))(pos, w)


def _pair_add(g, land, axis, pos, name):
    hr, wc = _piece_shape(g.shape, axis)
    tr = min(hr, 256)

    def body(pos_ref, g_ref, l_ref, o_ref):
        o_ref[0] = (g_ref[...] + l_ref[0]).astype(BF)

    blk = pl.BlockSpec((1, tr, wc), lambda j, i, pos: (j, i, 0))
    gs = pltpu.PrefetchScalarGridSpec(
        num_scalar_prefetch=1, grid=(4, hr // tr),
        in_specs=[_piece_spec(g.shape, axis, tr, lambda j, i, pos: j, lambda j, i, pos: pos[1]), blk], out_specs=blk)
    return pl.pallas_call(body, name=name, grid_spec=gs, out_shape=SDS((4, hr, wc), BF),
                          compiler_params=_cp(("arbitrary", "arbitrary")))(pos, g, land)


def _chip_sum(g, land1, land2, axis, pos, name):
    hr, wc = _piece_shape(g.shape, axis)
    tr = min(hr, 256)
    nb = hr // tr

    def body(pos_ref, g_ref, l1_ref, l2_ref, o_ref):
        acc = g_ref[...] + l1_ref[0]
        for q in range(3):
            acc = acc + l2_ref[q].astype(F32)
        o_ref[...] = acc

    gs = pltpu.PrefetchScalarGridSpec(
        num_scalar_prefetch=1, grid=(nb,),
        in_specs=[_piece_spec(g.shape, axis, tr, lambda i, pos: pos[0], lambda i, pos: pos[1]),
                  pl.BlockSpec((1, tr, wc), lambda i, pos: (pos[0], i, 0)),
                  pl.BlockSpec((3, tr, wc), lambda i, pos: (0, i, 0))],
        out_specs=pl.BlockSpec((tr, wc), lambda i, pos: (pos[1] * nb + i, 0)))
    return pl.pallas_call(body, name=name, grid_spec=gs, out_shape=SDS((2 * hr, wc), F32),
                          compiler_params=_cp(("arbitrary",)))(pos, g, land1, land2)


ANY = pl.BlockSpec(memory_space=pl.ANY)
COMM = pltpu.CompilerParams(has_side_effects=True)


def _on_each_device(fn):
    x, y, c = lax.axis_index("x"), lax.axis_index("y"), lax.axis_index("c")
    for sx in (0, 1):
        for sy in (0, 1):
            for sc in (0, 1):
                @pl.when(jnp.logical_and(jnp.logical_and(x == sx, y == sy), c == sc))
                def _(sx=sx, sy=sy, sc=sc):
                    fn(sx, sy, sc)


def _remote(src, dst, send_sem, recv_sem, to):
    return pltpu.make_async_remote_copy(src_ref=src, dst_ref=dst, send_sem=send_sem, recv_sem=recv_sem,
                                        device_id=to, device_id_type=MESH)


def _piece_ref(ref, axis, j, h):
    r, c = ref.shape
    hr, wc = _piece_shape((r, c), axis)
    if axis == 1:
        return ref.at[pl.ds(h * hr, hr), pl.ds(j * wc, wc)]
    return ref.at[pl.ds((2 * j + h) * hr, hr), :]


class _Stage:
    def __init__(self, ins, outs, n_sems, make_plan, aliases=None):
        self.ins, self.outs, self.n_sems, self.make_plan = list(ins), list(outs), n_sems, make_plan
        self.aliases = dict(aliases or {})


def _start(plan):
    def dev(sx, sy, sc):
        for cp, _, _ in plan(sx, sy, sc):
            cp.start()

    _on_each_device(dev)


def _finish(plan):
    def dev(sx, sy, sc):
        for _, sent, got in plan(sx, sy, sc):
            sent.wait_send()
            got.wait_recv()

    _on_each_device(dev)


def _comm_call(stage, name):
    n_in, n_out = len(stage.ins), len(stage.outs)

    def body(*refs):
        plan = stage.make_plan(refs[:n_in], refs[n_in:n_in + n_out], *refs[n_in + n_out:])
        _start(plan)
        _finish(plan)

    return pl.pallas_call(
        body, name=name, in_specs=[ANY] * n_in, out_specs=[ANY] * n_out, out_shape=stage.outs,
        input_output_aliases=stage.aliases,
        scratch_shapes=[pltpu.SemaphoreType.DMA((stage.n_sems,)), pltpu.SemaphoreType.DMA((stage.n_sems,))],
        compiler_params=COMM,
    )(*stage.ins)


class _SemsFrom:
    def __init__(self, sems, base):
        self.sems, self.base = sems, base

    @property
    def at(self):
        return self

    def __getitem__(self, k):
        return self.sems.at[self.base + k]


def _both(a, b):
    na, nb = len(a.ins), len(b.ins)
    ma = len(a.outs)

    def make_plan(ins, outs, send_sems, recv_sems):
        pa = a.make_plan(ins[:na], outs[:ma], send_sems, recv_sems)
        pb = b.make_plan(ins[na:], outs[ma:], _SemsFrom(send_sems, a.n_sems), _SemsFrom(recv_sems, a.n_sems))
        return lambda sx, sy, sc: pa(sx, sy, sc) + pb(sx, sy, sc)

    aliases = dict(a.aliases)
    aliases.update({na + i: ma + o for i, o in b.aliases.items()})
    return _Stage(a.ins + b.ins, a.outs + b.outs, a.n_sems + b.n_sems, make_plan, aliases)


def _same(cp):
    return (cp, cp, cp)


def _stage_gather_send(fulls, axes):
    n = len(fulls)

    def make_plan(ins, outs, send_sems, recv_sems):
        def plan(sx, sy, sc):
            cps = []
            for w in range(n):
                mine = _piece_ref(outs[w], axes[w], 2 * sx + sy, sc)
                for r, (fx, fy) in enumerate(CHIP_RELS):
                    k = 3 * w + r
                    to = (sx ^ fx, sy ^ fy, sc)
                    got = _piece_ref(outs[w], axes[w], 2 * (sx ^ fx) + (sy ^ fy), sc)
                    send = _remote(mine, mine, send_sems.at[k], recv_sems.at[k], to)
                    cps.append((send, send, _remote(got, got, send_sems.at[k], recv_sems.at[k], to)))
            return cps

        return plan

    return _Stage(fulls, [SDS(f.shape, f.dtype) for f in fulls], 3 * n, make_plan, {i: i for i in range(n)})


def _stage_gather_forward(fulls, axes):
    n = len(fulls)

    def make_plan(ins, outs, send_sems, recv_sems):
        def plan(sx, sy, sc):
            cps = []
            sib = (sx, sy, 1 - sc)
            for w in range(n):
                for r, (fx, fy) in enumerate(CHIP_RELS):
                    k = 3 * w + r
                    pj = 2 * (sx ^ fx) + (sy ^ fy)
                    have = _piece_ref(outs[w], axes[w], pj, sc)
                    want = _piece_ref(outs[w], axes[w], pj, 1 - sc)
                    send = _remote(have, have, send_sems.at[k], recv_sems.at[k], sib)
                    cps.append((send, send, _remote(want, want, send_sems.at[k], recv_sems.at[k], sib)))
            return cps

        return plan

    return _Stage(fulls, [SDS(f.shape, f.dtype) for f in fulls], 3 * n, make_plan, {i: i for i in range(n)})


def _stage_pair_exchange(grads, axes):
    n = len(grads)

    def make_plan(gs, land, send_sems, recv_sems):
        def plan(sx, sy, sc):
            return [_same(_remote(_piece_ref(gs[w], axes[w], jj, 1 - sc), land[w].at[jj], send_sems.at[4 * w + jj],
                                  recv_sems.at[4 * w + jj], (sx, sy, 1 - sc)))
                    for w in range(n) for jj in range(4)]

        return plan

    return _Stage(grads, [SDS((4,) + _piece_shape(g.shape, a), F32) for g, a in zip(grads, axes)], 4 * n, make_plan)


def _stage_chip_scatter(pieces):
    n = len(pieces)

    def make_plan(ps, land, send_sems, recv_sems):
        def plan(sx, sy, sc):
            return [_same(_remote(ps[w].at[2 * (sx ^ fx) + (sy ^ fy)], land[w].at[r], send_sems.at[3 * w + r],
                                  recv_sems.at[3 * w + r], (sx ^ fx, sy ^ fy, sc)))
                    for w in range(n) for r, (fx, fy) in enumerate(CHIP_RELS)]

        return plan

    return _Stage(pieces, [SDS((3,) + p.shape[1:], p.dtype) for p in pieces], 3 * n, make_plan)


def _stage_pair_share(shards):
    n = len(shards)

    def make_plan(ins, outs, send_sems, recv_sems):
        def plan(sx, sy, sc):
            cps = []
            sib = (sx, sy, 1 - sc)
            for w in range(n):
                hr = shards[w].shape[0] // 2
                mine = outs[w].at[pl.ds(sc * hr, hr), :]
                theirs = outs[w].at[pl.ds((1 - sc) * hr, hr), :]
                send = _remote(mine, mine, send_sems.at[w], recv_sems.at[w], sib)
                cps.append((send, send, _remote(theirs, theirs, send_sems.at[w], recv_sems.at[w], sib)))
            return cps

        return plan

    return _Stage(shards, [SDS(g.shape, g.dtype) for g in shards], n, make_plan, {i: i for i in range(n)})


def _stage_gather_small(stack):
    def make_plan(ins, outs, send_sems, recv_sems):
        def plan(sx, sy, sc):
            mine = outs[0].at[4 * sx + 2 * sy + sc]
            return [_same(_remote(mine, mine, send_sems.at[k], recv_sems.at[k], (sx ^ fx, sy ^ fy, sc ^ fc)))
                    for k, (fx, fy, fc) in enumerate(DEV_RELS)]

        return plan

    return _Stage([stack], [SDS(stack.shape, stack.dtype)], 7, make_plan, {0: 0})


def kernel(x, w_in, conv_a_w, w_out_a, conv_b_w, conv_b_bias, ln_b_gamma, ln_b_beta, w_out_b, w_o, ln1_gamma, ln1_beta, w_up, w_down, ln2_gamma, ln2_beta, loss_target, m_w_in, m_conv_a_w, m_w_out_a, m_conv_b_w, m_conv_b_bias, m_ln_b_gamma, m_ln_b_beta, m_w_out_b, m_w_o, m_ln1_gamma, m_ln1_beta, m_w_up, m_w_down, m_ln2_gamma, m_ln2_beta, v_w_in, v_conv_a_w, v_w_out_a, v_conv_b_w, v_conv_b_bias, v_ln_b_gamma, v_ln_b_beta, v_w_out_b, v_w_o, v_ln1_gamma, v_ln1_beta, v_w_up, v_w_down, v_ln2_gamma, v_ln2_beta):
    s, d = x.shape[1], x.shape[2]
    xs = x.reshape(s, d)
    tgt = loss_target.reshape(s, d)
    dq = d // 4
    chip = 2 * lax.axis_index("x") + lax.axis_index("y")
    core = lax.axis_index("c")
    pos = jnp.stack([chip, core]).astype(jnp.int32)
    names = ("w_in", "w_out_a", "w_out_b", "w_o", "w_up", "w_down")
    axes = (1, 0, 0, 0, 1, 0)

    conv_pack = jnp.concatenate([jnp.pad(conv_a_w, ((0, 8 - K_A), (0, 0))), jnp.pad(conv_b_w, ((0, 32 - K_B), (0, 0))),
                                 jnp.zeros((8, dq), F32)], axis=0)
    conv_full = lax.dynamic_update_slice(jnp.zeros((conv_pack.shape[0], d), F32), conv_pack, (0, chip * dq))
    fulls = [_place_cast(w, a, pos, "place_" + nm)
             for w, a, nm in zip((w_in, w_out_a, w_out_b, w_o, w_up, w_down), axes, names)]
    vec = lambda a: a.reshape(1, d)
    bias_b, lbg, lbb = vec(conv_b_bias), vec(ln_b_gamma), vec(ln_b_beta)
    l1g, l1b, l2g, l2b = vec(ln1_gamma), vec(ln1_beta), vec(ln2_gamma), vec(ln2_beta)

    (p, xb), first = _in_proj_own(xs, w_in, pos, _stage_gather_send([fulls[0], conv_full], (1, 1)))
    wi, convs = _comm_call(_stage_gather_forward(first, (1, 1)), "gather_forward_w_in")
    cwa, cwb = convs[0:K_A], convs[8:8 + K_B]
    p, rest = _in_proj_rest(xb, wi, p, pos, _stage_gather_send(fulls[1:], axes[1:]))
    (conva, yapre, nb, rstdb, u3), rest = _conv_fwd(p, cwa, cwb, bias_b, lbg, lbb, d,
                                                    _stage_gather_forward(rest, axes[1:]))
    woa, wob, wo, wup, wdown = rest
    ya, yb, merged, n1, rstd1 = _mixer_out(yapre, u3, p, xs, woa, wob, wo, d)
    x1b, hb, dhpre, dr2b, dr1, dr1b, acc_mlp = _mlp(n1, rstd1, tgt, wup, wdown, l1g, l1b, l2g, l2b)
    g_up, _ = _grad_w(x1b, dhpre, "grad_w_up")
    g_down, _ = _grad_w(hb, dr2b, "grad_w_down")
    dya, dyb, dba, dga, dgb, dca, du1, acc_mix = _mixer_bwd_local(dr1b, p, ya, yb, conva, nb, rstdb, wo, woa, wob,
                                                                   lbg, lbb)
    g_oa, _ = _grad_w(yapre, dya, "grad_w_out_a")
    g_ob, _ = _grad_w(u3, dyb, "grad_w_out_b")
    g_o, _ = _grad_w(merged, dr1b, "grad_w_o")

    early, e_axes, e_names = [g_oa, g_ob, g_o, g_up, g_down], axes[1:], names[1:]
    (dp, dcwa, dcwb), land1 = _conv_bwd(dca, du1, p, dba, dga, dgb, cwa, cwb, d, _stage_pair_exchange(early, e_axes))
    pieces = [_pair_add(g, l, a, pos, "pair_add_" + nm) for g, l, a, nm in zip(early, land1, e_axes, e_names)]
    pack = jnp.concatenate([dcwa, dcwb, acc_mix, acc_mlp], axis=0)
    stack = lax.dynamic_update_slice(jnp.zeros((8,) + pack.shape, F32), pack[None], (2 * chip + core, 0, 0))
    g_wi, landed = _grad_w(xb, dp, "grad_w_in", _both(_stage_chip_scatter(pieces), _stage_gather_small(stack)))
    land2, stack = landed[:-1], landed[-1]
    halves = [_chip_sum(g, l1, l2, a, pos, "chip_sum_" + nm)
              for g, l1, l2, a, nm in zip(early, land1, land2, e_axes, e_names)]
    (land1_in,) = _comm_call(_stage_pair_exchange([g_wi], (1,)), "pair_exchange_w_in")
    piece_in = _pair_add(g_wi, land1_in, 1, pos, "pair_add_w_in")
    grad_x, (land2_in,) = _grad_x(dr1, dp, wi, _stage_chip_scatter([piece_in]))
    half_in = _chip_sum(g_wi, land1_in, land2_in, 1, pos, "chip_sum_w_in")
    g_in, g_oa, g_ob, g_o, g_up, g_down = _comm_call(_stage_pair_share([half_in] + halves), "pair_share")
    small = _sum_parts(stack, "small_sum")
    g_ca = lax.dynamic_slice(small, (0, chip * dq), (K_A, dq))
    g_cb = lax.dynamic_slice(small, (8, chip * dq), (K_B, dq))
    g_vec = jnp.stack([small[r] for r in (42, 40, 41, 51, 52, 49, 50)])

    loss = lax.psum((0.5 / d) * jnp.sum(acc_mlp[0]), ("x", "y", "c"))

    big = {}
    for name, w, g, m, v in (("w_in", w_in, g_in, m_w_in, v_w_in), ("w_out_a", w_out_a, g_oa, m_w_out_a, v_w_out_a),
                             ("w_out_b", w_out_b, g_ob, m_w_out_b, v_w_out_b), ("w_o", w_o, g_o, m_w_o, v_w_o),
                             ("w_up", w_up, g_up, m_w_up, v_w_up), ("w_down", w_down, g_down, m_w_down, v_w_down),
                             ("conv_a_w", conv_a_w, g_ca, m_conv_a_w, v_conv_a_w),
                             ("conv_b_w", conv_b_w, g_cb, m_conv_b_w, v_conv_b_w)):
        big[name] = (g,) + tuple(_adamw(w, g, m, v, "adamw_" + name))
    vec_names = ("conv_b_bias", "ln_b_gamma", "ln_b_beta", "ln1_gamma", "ln1_beta", "ln2_gamma", "ln2_beta")
    w7 = jnp.stack([conv_b_bias, ln_b_gamma, ln_b_beta, ln1_gamma, ln1_beta, ln2_gamma, ln2_beta])
    m7 = jnp.stack([m_conv_b_bias, m_ln_b_gamma, m_ln_b_beta, m_ln1_gamma, m_ln1_beta, m_ln2_gamma, m_ln2_beta])
    v7 = jnp.stack([v_conv_b_bias, v_ln_b_gamma, v_ln_b_beta, v_ln1_gamma, v_ln1_beta, v_ln2_gamma, v_ln2_beta])
    d7, nm7, nv7 = _adamw(w7, g_vec, m7, v7, "adamw_vectors")
    for q, name in enumerate(vec_names):
        big[name] = (g_vec[q], d7[q], nm7[q], nv7[q])

    order = ("w_in", "conv_a_w", "w_out_a", "conv_b_w", "conv_b_bias", "ln_b_gamma", "ln_b_beta", "w_out_b", "w_o",
             "ln1_gamma", "ln1_beta", "w_up", "w_down", "ln2_gamma", "ln2_beta")
    outs = [loss, grad_x.reshape(x.shape)]
    for part in range(4):
        outs += [big[name][part] for name in order]
    return tuple(outs)
```

```python
import jax
import jax.numpy as jnp
from jax import lax
from jax.experimental import pallas as pl
from jax.experimental.pallas import tpu as pltpu

F32 = jnp.float32
BF = jnp.bfloat16
SDS = jax.ShapeDtypeStruct
MESH = pl.DeviceIdType.MESH

ALPHA = 2.0 ** 0.25
LN_EPS = 1e-5
K_A = 3
K_B = 31
HALO = 32
CONV_ROWS = 64
CONV_LANES = 128
ADAM_LR = 0.001
ADAM_B1 = 0.9
ADAM_B2 = 0.999
ADAM_EPS = 1e-08
ADAM_WD = 0.01
ADAM_STEP = 10
P_DT = BF
CHIP_RELS = ((1, 0), (0, 1), (1, 1))
DEV_RELS = tuple((fx, fy, fc) for fx in (0, 1) for fy in (0, 1) for fc in (0, 1))[1:]


def _cp(sem=None, vmem_mb=56, side_effects=False):
    return pltpu.CompilerParams(dimension_semantics=sem, vmem_limit_bytes=vmem_mb << 20,
                                has_side_effects=side_effects)


def _const(shape):
    return pl.BlockSpec(shape, lambda *_: (0,) * len(shape), pipeline_mode=pl.Buffered(1))


def _sig(v):
    return jax.nn.sigmoid(v)


def _ln_fwd(r):
    mu = jnp.mean(r, axis=-1, keepdims=True)
    xc = r - mu
    var = jnp.mean(xc * xc, axis=-1, keepdims=True)
    rstd = lax.rsqrt(var + LN_EPS)
    return xc * rstd, rstd


def _ln_bwd(dn, n, rstd):
    m1 = jnp.mean(dn, axis=-1, keepdims=True)
    m2 = jnp.mean(dn * n, axis=-1, keepdims=True)
    return rstd * (dn - m1 - n * m2)


def _dot(a, b):
    return jnp.dot(a, b, preferred_element_type=F32)


def _dot_nt(a, b):
    return lax.dot_general(a, b, (((1,), (1,)), ((), ())), preferred_element_type=F32)


def _dot_tn(a, b):
    return lax.dot_general(a, b, (((0,), (0,)), ((), ())), preferred_element_type=F32)


def _tile(n, pref):
    if n <= pref:
        return n
    return max(t for t in range(128, pref + 1, 128) if n % t == 0)


def _rowsum8(v):
    acc = v[0:8]
    for g in range(1, v.shape[0] // 8):
        acc = acc + v[8 * g:8 * g + 8]
    return acc


def _taps(win, offsets, rows):
    r_all = win.shape[0]
    by_res = {}
    for k, o in enumerate(offsets):
        by_res.setdefault(o % 8, []).append((k, o // 8))
    for s, taps in sorted(by_res.items()):
        r = win if s == 0 else pltpu.roll(win, r_all - s, 0)
        for k, q in taps:
            yield k, r[8 * q:8 * q + rows]


CAUSAL_A = [HALO - (K_A - 1) + k for k in range(K_A)]
CAUSAL_B = [HALO - (K_B - 1) + k for k in range(K_B)]
ANTI_A = [K_A - 1 - k for k in range(K_A)]
ANTI_B = [K_B - 1 - k for k in range(K_B)]


def _host_call(body, *, name, grid, in_specs, out_specs, out_shape, scratch_shapes, args, hosted, prefetch=None,
               aliases=None):
    n_in, n_out, n_scr = len(in_specs), len(out_specs), len(scratch_shapes)
    n_pre = 0 if prefetch is None else 1
    pre = () if prefetch is None else (prefetch,)
    sem = ("arbitrary",) * len(grid)
    own_aliases = {n_pre + a: b for a, b in (aliases or {}).items()}

    def call(kernel_body, ins, outs, shapes, scratch, all_aliases, side_effects, operands):
        gs = pltpu.PrefetchScalarGridSpec(num_scalar_prefetch=n_pre, grid=grid, in_specs=ins, out_specs=outs,
                                          scratch_shapes=scratch)
        return pl.pallas_call(kernel_body, name=name, grid_spec=gs, out_shape=shapes,
                              input_output_aliases=all_aliases,
                              compiler_params=_cp(sem, side_effects=side_effects))(*pre, *operands)

    if hosted is None:
        res = call(body, list(in_specs), list(out_specs), list(out_shape), list(scratch_shapes), own_aliases, False,
                   args)
        return list(res), []
    h_in, h_out = len(hosted.ins), len(hosted.outs)

    def full_body(*refs):
        pre_refs, refs = refs[:n_pre], refs[n_pre:]
        ins, refs = refs[:n_in], refs[n_in:]
        hins, refs = refs[:h_in], refs[h_in:]
        outs, refs = refs[:n_out], refs[n_out:]
        houts, refs = refs[:h_out], refs[h_out:]
        scr, (send_sems, recv_sems) = refs[:n_scr], refs[n_scr:]
        plan = hosted.make_plan(hins, houts, send_sems, recv_sems)
        first = last = None
        for a, size in enumerate(grid):
            at0, at1 = pl.program_id(a) == 0, pl.program_id(a) == size - 1
            first = at0 if first is None else jnp.logical_and(first, at0)
            last = at1 if last is None else jnp.logical_and(last, at1)

        @pl.when(first)
        def _():
            _start(plan)

        body(*pre_refs, *ins, *outs, *scr)

        @pl.when(last)
        def _():
            _finish(plan)

    all_aliases = dict(own_aliases)
    all_aliases.update({n_pre + n_in + a: n_out + b for a, b in hosted.aliases.items()})
    res = call(full_body, list(in_specs) + [ANY] * h_in, list(out_specs) + [ANY] * h_out,
               list(out_shape) + list(hosted.outs),
               list(scratch_shapes) + [pltpu.SemaphoreType.DMA((hosted.n_sems,)),
                                       pltpu.SemaphoreType.DMA((hosted.n_sems,))],
               all_aliases, True, (*args, *hosted.ins))
    return list(res[:n_out]), list(res[n_out:])


def _in_proj_own(x, w_shard, pos, hosted):
    s, d = x.shape
    tn = w_shard.shape[1]
    tm = min(s, 1024)

    def body(pos_ref, x_ref, w_ref, p_ref, xb_ref, wb):
        @pl.when(pl.program_id(0) == 0)
        def _():
            wb[...] = w_ref[...].astype(BF)

        xb = x_ref[...].astype(BF)
        xb_ref[...] = xb
        p_ref[...] = _dot(xb, wb[...]).astype(p_ref.dtype)

    return _host_call(
        body, name="in_proj_own", grid=(s // tm,), prefetch=pos,
        in_specs=[pl.BlockSpec((tm, d), lambda i, pos: (i, 0)),
                  pl.BlockSpec((d, tn), lambda i, pos: (0, 0), pipeline_mode=pl.Buffered(1))],
        out_specs=[pl.BlockSpec((tm, tn), lambda i, pos: (i, pos[0])), pl.BlockSpec((tm, d), lambda i, pos: (i, 0))],
        out_shape=[SDS((s, 4 * tn), P_DT), SDS((s, d), BF)], scratch_shapes=[pltpu.VMEM((d, tn), BF)],
        args=(x, w_shard), hosted=hosted)


def _in_proj_rest(xb, wi, p, pos, hosted):
    s, d = xb.shape
    n = wi.shape[1]
    tm, tn = min(s, 1024), n // 4

    def body(pos_ref, xb_ref, w_ref, p_in, p_ref):
        p_ref[...] = _dot(xb_ref[...], w_ref[...]).astype(p_ref.dtype)

    col = lambda i, j, pos: (pos[0] + 1 + j) % 4
    (p,), extra = _host_call(
        body, name="in_proj_rest", grid=(s // tm, 3), prefetch=pos, aliases={2: 0},
        in_specs=[pl.BlockSpec((tm, d), lambda i, j, pos: (i, 0)),
                  pl.BlockSpec((d, tn), lambda i, j, pos: (0, col(i, j, pos))), ANY],
        out_specs=[pl.BlockSpec((tm, tn), lambda i, j, pos: (i, col(i, j, pos)))],
        out_shape=[SDS((s, n), P_DT)], scratch_shapes=[], args=(xb, wi, p), hosted=hosted)
    return p, extra


def _col_spec(tm, d, k):
    return pl.BlockSpec((tm, d), lambda i, k=k: (i, k))


def _prev_halo_spec(tm, d, k):
    r = tm // HALO
    return pl.BlockSpec((HALO, d), lambda i, k=k: (jnp.maximum(i * r - 1, 0), k))


def _next_halo_spec(tm, d, k, s):
    r = tm // HALO
    last = s // HALO - 1
    return pl.BlockSpec((HALO, d), lambda i, k=k: (jnp.minimum((i + 1) * r, last), k))


def _conv_fwd(p, cwa, cwb, bias_b, lbg, lbb, d, hosted=None):
    s = p.shape[0]
    tm = min(s, 256)
    nt = s // tm

    def body(ba_ref, ca_ref, va_ref, vb_ref, gb_ref, hca_ref, hva_ref, hvb_ref, hgb_ref,
             cwa_ref, cwb_ref, bias_ref, lbg_ref, lbb_ref,
             conva_ref, yapre_ref, nb_ref, rstdb_ref, u3_ref,
             zbuf, ubuf, u1buf):
        i = pl.program_id(0)
        keep = (i > 0).astype(F32)
        zbuf[pl.ds(0, HALO), :] = hca_ref[...].astype(F32) * hva_ref[...].astype(F32) * keep
        ubuf[pl.ds(0, HALO), :] = hvb_ref[...].astype(F32) * _sig(hgb_ref[...].astype(F32)) * keep
        zbuf[pl.ds(HALO, tm), :] = ca_ref[...].astype(F32) * va_ref[...].astype(F32)
        ubuf[pl.ds(HALO, tm), :] = vb_ref[...].astype(F32) * _sig(gb_ref[...].astype(F32))

        def chunk(j, carry):
            r0 = pl.multiple_of(j * CONV_ROWS, CONV_ROWS)
            rows = pl.ds(r0, CONV_ROWS)
            for lc in range(d // CONV_LANES):
                ls = pl.ds(lc * CONV_LANES, CONV_LANES)
                acc = jnp.zeros((CONV_ROWS, CONV_LANES), F32)
                for k, sl in _taps(zbuf[pl.ds(r0, CONV_ROWS + HALO), ls], CAUSAL_A, CONV_ROWS):
                    acc = acc + cwa_ref[pl.ds(k, 1), ls] * sl
                conva_ref[rows, ls] = acc
                yapre_ref[rows, ls] = (ba_ref[rows, ls].astype(F32) * acc).astype(BF)
                acc = jnp.zeros((CONV_ROWS, CONV_LANES), F32)
                for k, sl in _taps(ubuf[pl.ds(r0, CONV_ROWS + HALO), ls], CAUSAL_B, CONV_ROWS):
                    acc = acc + cwb_ref[pl.ds(k, 1), ls] * sl
                u1buf[rows, ls] = acc + bias_ref[:, ls]
            return carry

        lax.fori_loop(0, tm // CONV_ROWS, chunk, 0)
        nb, rstd = _ln_fwd(u1buf[...])
        nb_ref[...] = nb
        rstdb_ref[...] = rstd
        u2 = nb * lbg_ref[...] + lbb_ref[...]
        u3_ref[...] = (u2 * _sig(u2)).astype(BF)

    vec = _const((1, d))
    return _host_call(
        body, name="conv_fwd", grid=(nt,),
        in_specs=[_col_spec(tm, d, k) for k in range(5)] + [_prev_halo_spec(tm, d, k) for k in (1, 2, 3, 4)]
        + [_const((K_A, d)), _const((K_B, d)), vec, vec, vec],
        out_specs=[pl.BlockSpec((tm, d), lambda i: (i, 0)), pl.BlockSpec((tm, d), lambda i: (i, 0)),
                   pl.BlockSpec((tm, d), lambda i: (i, 0)), pl.BlockSpec((tm, 1), lambda i: (i, 0)),
                   pl.BlockSpec((tm, d), lambda i: (i, 0))],
        out_shape=[SDS((s, d), F32), SDS((s, d), BF), SDS((s, d), F32), SDS((s, 1), F32), SDS((s, d), BF)],
        scratch_shapes=[pltpu.VMEM((HALO + tm, d), F32), pltpu.VMEM((HALO + tm, d), F32), pltpu.VMEM((tm, d), F32)],
        args=(p, p, p, p, p, p, p, p, p, cwa, cwb, bias_b, lbg, lbb), hosted=hosted)


def _mixer_out(yapre, u3, p, x, woa, wob, wo, d, hosted=None):
    s = x.shape[0]
    tm = min(s, 512)

    def body(yapre_ref, u3_ref, ga_ref, gb_ref, x_ref, woa_ref, wob_ref, wo_ref,
             ya_ref, yb_ref, merged_ref, n1_ref, rstd1_ref):
        ya = _dot(yapre_ref[...], woa_ref[...])
        yb = _dot(u3_ref[...], wob_ref[...])
        ya_ref[...] = ya
        yb_ref[...] = yb
        merged = (_sig(ga_ref[...].astype(F32)) * ya + _sig(gb_ref[...].astype(F32)) * yb).astype(BF)
        merged_ref[...] = merged
        r1 = F32(ALPHA) * x_ref[...] + _dot(merged, wo_ref[...])
        n1, rstd1 = _ln_fwd(r1)
        n1_ref[...] = n1
        rstd1_ref[...] = rstd1

    row = pl.BlockSpec((tm, d), lambda i: (i, 0))
    return _host_call(
        body, name="mixer_out", grid=(s // tm,),
        in_specs=[row, row, _col_spec(tm, d, 5), _col_spec(tm, d, 6), row,
                  _const((d, d)), _const((d, d)), _const((d, d))],
        out_specs=[row, row, row, row, pl.BlockSpec((tm, 1), lambda i: (i, 0))],
        out_shape=[SDS((s, d), F32), SDS((s, d), F32), SDS((s, d), BF), SDS((s, d), F32), SDS((s, 1), F32)],
        scratch_shapes=[], args=(yapre, u3, p, p, x, woa, wob, wo), hosted=hosted)


def _mlp(n1, rstd1, tgt, wup, wdown, l1g, l1b, l2g, l2b):
    s, d = n1.shape
    dff = wup.shape[1]
    tm = min(s, 256)
    fc = min(dff, 1024)
    nq = dff // fc

    def body(n1_ref, rstd1_ref, tgt_ref, wup_ref, wdown_ref, l1g_ref, l1b_ref, l2g_ref, l2b_ref,
             x1b_ref, hb_ref, dhpre_ref, dr2b_ref, dr1_ref, dr1b_ref, acc_ref, rbuf):
        i = pl.program_id(0)
        n1v = n1_ref[...]
        x1 = n1v * l1g_ref[...] + l1b_ref[...]
        x1b = x1.astype(BF)
        x1b_ref[...] = x1b
        ff = jnp.zeros((tm, d), F32)
        for q in range(nq):
            cs = pl.ds(q * fc, fc)
            r = jnp.maximum(_dot(x1b, wup_ref[:, cs]), 0.0)
            rbuf[:, cs] = r
            hq = (r * r).astype(BF)
            hb_ref[:, cs] = hq
            ff = ff + _dot(hq, wdown_ref[cs, :])
        n2, rstd2 = _ln_fwd(F32(ALPHA) * x1 + ff)
        x2 = n2 * l2g_ref[...] + l2b_ref[...]
        err = x2 - tgt_ref[...]
        dx2 = err * F32(1.0 / d)
        dr2 = _ln_bwd(dx2 * l2g_ref[...], n2, rstd2)
        dr2b = dr2.astype(BF)
        dr2b_ref[...] = dr2b
        dx1 = F32(ALPHA) * dr2
        for q in range(nq):
            cs = pl.ds(q * fc, fc)
            dh = _dot_nt(dr2b, wdown_ref[cs, :])
            dhp = (dh * (2.0 * rbuf[:, cs])).astype(BF)
            dhpre_ref[:, cs] = dhp
            dx1 = dx1 + _dot_nt(dhp, wup_ref[:, cs])
        dr1 = _ln_bwd(dx1 * l1g_ref[...], n1v, rstd1_ref[...])
        dr1_ref[...] = dr1
        dr1b_ref[...] = dr1.astype(BF)

        @pl.when(i == 0)
        def _():
            acc_ref[...] = jnp.zeros_like(acc_ref)

        for q, val in enumerate((err * err, dx2 * n2, dx2, dx1 * n1v, dx1)):
            acc_ref[pl.ds(q, 1), :] += jnp.sum(val, axis=0, keepdims=True)

    row = pl.BlockSpec((tm, d), lambda i: (i, 0))
    wide = pl.BlockSpec((tm, dff), lambda i: (i, 0))
    vec = _const((1, d))
    return pl.pallas_call(
        body, name="mlp_fwd_bwd", grid=(s // tm,),
        in_specs=[row, pl.BlockSpec((tm, 1), lambda i: (i, 0)), row, _const((d, dff)), _const((dff, d)),
                  vec, vec, vec, vec],
        out_specs=[row, wide, wide, row, row, row, pl.BlockSpec((8, d), lambda i: (0, 0))],
        out_shape=[SDS((s, d), BF), SDS((s, dff), BF), SDS((s, dff), BF), SDS((s, d), BF), SDS((s, d), F32),
                   SDS((s, d), BF), SDS((8, d), F32)],
        scratch_shapes=[pltpu.VMEM((tm, dff), F32)],
        compiler_params=_cp(("arbitrary",)),
    )(n1, rstd1, tgt, wup, wdown, l1g, l1b, l2g, l2b)


def _mixer_bwd_local(dr1b, p, ya, yb, conva, nb, rstdb, wo, woa, wob, lbg, lbb):
    s, d = ya.shape
    tm = min(s, 256)

    def body(dr1b_ref, ba_ref, ga_ref, gb_ref, ya_ref, yb_ref, conva_ref, nb_ref, rstdb_ref,
             wo_ref, woa_ref, wob_ref, lbg_ref, lbb_ref,
             dya_ref, dyb_ref, dba_ref, dga_ref, dgb_ref, dca_ref, du1_ref, acc_ref):
        i = pl.program_id(0)
        dmerged = _dot_nt(dr1b_ref[...], wo_ref[...])
        sa = _sig(ga_ref[...].astype(F32))
        sb = _sig(gb_ref[...].astype(F32))
        dya = (dmerged * sa).astype(BF)
        dyb = (dmerged * sb).astype(BF)
        dya_ref[...] = dya
        dyb_ref[...] = dyb
        dga_ref[...] = (dmerged * ya_ref[...] * (sa * (1.0 - sa))).astype(BF)
        dgb_ref[...] = (dmerged * yb_ref[...] * (sb * (1.0 - sb))).astype(BF)
        dyapre = _dot_nt(dya, woa_ref[...])
        dba_ref[...] = (dyapre * conva_ref[...]).astype(BF)
        dca_ref[...] = dyapre * ba_ref[...].astype(F32)
        du3 = _dot_nt(dyb, wob_ref[...])
        nbv = nb_ref[...]
        u2 = nbv * lbg_ref[...] + lbb_ref[...]
        sg = _sig(u2)
        du2 = du3 * (sg * (1.0 + u2 * (1.0 - sg)))
        du1 = _ln_bwd(du2 * lbg_ref[...], nbv, rstdb_ref[...])
        du1_ref[...] = du1

        @pl.when(i == 0)
        def _():
            acc_ref[...] = jnp.zeros_like(acc_ref)

        for q, val in enumerate((du2 * nbv, du2, du1)):
            acc_ref[pl.ds(q, 1), :] += jnp.sum(val, axis=0, keepdims=True)

    row = pl.BlockSpec((tm, d), lambda i: (i, 0))
    vec = _const((1, d))
    return pl.pallas_call(
        body, name="mixer_bwd_local", grid=(s // tm,),
        in_specs=[row, _col_spec(tm, d, 0), _col_spec(tm, d, 5), _col_spec(tm, d, 6), row, row, row, row,
                  pl.BlockSpec((tm, 1), lambda i: (i, 0)), _const((d, d)), _const((d, d)), _const((d, d)), vec, vec],
        out_specs=[row, row, row, row, row, row, row, pl.BlockSpec((8, d), lambda i: (0, 0))],
        out_shape=[SDS((s, d), BF)] * 5 + [SDS((s, d), F32), SDS((s, d), F32), SDS((8, d), F32)],
        compiler_params=_cp(("arbitrary",)),
    )(dr1b, p, p, p, ya, yb, conva, nb, rstdb, wo, woa, wob, lbg, lbb)


def _conv_bwd(dca, du1, p, dba, dga, dgb, cwa, cwb, d, hosted=None):
    s = dca.shape[0]
    tm = min(s, 256)
    nt = s // tm

    def body(dca_ref, du1_ref, ndca_ref, ndu1_ref, ca_ref, va_ref, vb_ref, gb_ref,
             dba_ref, dga_ref, dgb_ref, cwa_ref, cwb_ref,
             dp_ref, dcwa_ref, dcwb_ref,
             dcabuf, du1buf, sgbuf, acca, accb):
        i = pl.program_id(0)
        keep_next = (i < nt - 1).astype(F32)

        @pl.when(i == 0)
        def _():
            acca[...] = jnp.zeros_like(acca)
            accb[...] = jnp.zeros_like(accb)

        sgbuf[...] = _sig(gb_ref[...].astype(F32))
        dcabuf[pl.ds(0, tm), :] = dca_ref[...]
        dcabuf[pl.ds(tm, HALO), :] = ndca_ref[...] * keep_next
        du1buf[pl.ds(0, tm), :] = du1_ref[...]
        du1buf[pl.ds(tm, HALO), :] = ndu1_ref[...] * keep_next
        dp_ref[:, pl.ds(0, d)] = dba_ref[...]
        dp_ref[:, pl.ds(5 * d, d)] = dga_ref[...]
        dp_ref[:, pl.ds(6 * d, d)] = dgb_ref[...]

        def chunk(j, carry):
            r0 = pl.multiple_of(j * CONV_ROWS, CONV_ROWS)
            rows = pl.ds(r0, CONV_ROWS)
            for lc in range(d // CONV_LANES):
                lo = lc * CONV_LANES
                ls = pl.ds(lo, CONV_LANES)
                cac = ca_ref[rows, ls].astype(F32)
                vac = va_ref[rows, ls].astype(F32)
                zc = cac * vac
                acc = jnp.zeros((CONV_ROWS, CONV_LANES), F32)
                for k, sl in _taps(dcabuf[pl.ds(r0, CONV_ROWS + HALO), ls], ANTI_A, CONV_ROWS):
                    acc = acc + cwa_ref[pl.ds(k, 1), ls] * sl
                    acca[pl.ds(8 * k, 8), ls] += _rowsum8(sl * zc)
                dp_ref[rows, pl.ds(d + lo, CONV_LANES)] = (acc * vac).astype(BF)
                dp_ref[rows, pl.ds(2 * d + lo, CONV_LANES)] = (acc * cac).astype(BF)
                sgc = sgbuf[rows, ls]
                vbc = vb_ref[rows, ls].astype(F32)
                uc = vbc * sgc
                acc = jnp.zeros((CONV_ROWS, CONV_LANES), F32)
                for k, sl in _taps(du1buf[pl.ds(r0, CONV_ROWS + HALO), ls], ANTI_B, CONV_ROWS):
                    acc = acc + cwb_ref[pl.ds(k, 1), ls] * sl
                    accb[pl.ds(8 * k, 8), ls] += _rowsum8(sl * uc)
                dp_ref[rows, pl.ds(3 * d + lo, CONV_LANES)] = (acc * sgc).astype(BF)
                dp_ref[rows, pl.ds(4 * d + lo, CONV_LANES)] = (acc * vbc * (sgc * (1.0 - sgc))).astype(BF)
            return carry

        lax.fori_loop(0, tm // CONV_ROWS, chunk, 0)

        @pl.when(i == nt - 1)
        def _():
            dcwa_ref[...] = jnp.zeros_like(dcwa_ref)
            dcwb_ref[...] = jnp.zeros_like(dcwb_ref)
            for k in range(K_A):
                dcwa_ref[pl.ds(k, 1), :] = jnp.sum(acca[pl.ds(8 * k, 8), :], axis=0, keepdims=True)
            for k in range(K_B):
                dcwb_ref[pl.ds(k, 1), :] = jnp.sum(accb[pl.ds(8 * k, 8), :], axis=0, keepdims=True)

    row = pl.BlockSpec((tm, d), lambda i: (i, 0))
    nxt = _next_halo_spec(tm, d, 0, s)
    return _host_call(
        body, name="conv_bwd", grid=(nt,),
        in_specs=[row, row, nxt, nxt] + [_col_spec(tm, d, k) for k in (1, 2, 3, 4)]
        + [row, row, row, _const((K_A, d)), _const((K_B, d))],
        out_specs=[pl.BlockSpec((tm, 7 * d), lambda i: (i, 0)), pl.BlockSpec((8, d), lambda i: (0, 0)),
                   pl.BlockSpec((32, d), lambda i: (0, 0))],
        out_shape=[SDS((s, 7 * d), BF), SDS((8, d), F32), SDS((32, d), F32)],
        scratch_shapes=[pltpu.VMEM((tm + HALO, d), F32), pltpu.VMEM((tm + HALO, d), F32),
                        pltpu.VMEM((tm, d), F32), pltpu.VMEM((8 * K_A, d), F32), pltpu.VMEM((8 * K_B, d), F32)],
        args=(dca, du1, dca, du1, p, p, p, p, dba, dga, dgb, cwa, cwb), hosted=hosted)


def _grad_w(a, b, name, hosted=None):
    s, m = a.shape
    n = b.shape[1]
    tm, tn, tk = _tile(m, 1024), _tile(n, 1024), _tile(s, 2048)
    nk = s // tk

    def body(a_ref, b_ref, o_ref, ob_ref):
        k = pl.program_id(2)

        @pl.when(k == 0)
        def _():
            o_ref[...] = jnp.zeros_like(o_ref)

        o_ref[...] += _dot_tn(a_ref[...], b_ref[...])

        @pl.when(k == nk - 1)
        def _():
            ob_ref[...] = o_ref[...].astype(BF)

    blk = pl.BlockSpec((tm, tn), lambda i, j, k: (i, j))
    (g, gb), extra = _host_call(
        body, name=name, grid=(m // tm, n // tn, nk),
        in_specs=[pl.BlockSpec((tk, tm), lambda i, j, k: (k, i)), pl.BlockSpec((tk, tn), lambda i, j, k: (k, j))],
        out_specs=[blk, blk], out_shape=[SDS((m, n), F32), SDS((m, n), BF)], scratch_shapes=[], args=(a, b),
        hosted=hosted)
    return g, gb, extra


def _grad_x(dr1, dp, wi, hosted=None):
    s, d = dr1.shape
    n = wi.shape[1]
    tm, tk = min(s, 512), _tile(n, 3584)

    def body(dr1_ref, dp_ref, w_ref, o_ref):
        @pl.when(pl.program_id(1) == 0)
        def _():
            o_ref[...] = F32(ALPHA) * dr1_ref[...]

        o_ref[...] += _dot_nt(dp_ref[...], w_ref[...])

    (gx,), extra = _host_call(
        body, name="grad_x", grid=(s // tm, n // tk),
        in_specs=[pl.BlockSpec((tm, d), lambda i, k: (i, 0)), pl.BlockSpec((tm, tk), lambda i, k: (i, k)),
                  pl.BlockSpec((d, tk), lambda i, k: (0, k))],
        out_specs=[pl.BlockSpec((tm, d), lambda i, k: (i, 0))],
        out_shape=[SDS((s, d), F32)], scratch_shapes=[], args=(dr1, dp, wi), hosted=hosted)
    return gx, extra


def _adamw_math(w, g, m, v):
    m2 = ADAM_B1 * m + (1.0 - ADAM_B1) * g
    v2 = ADAM_B2 * v + (1.0 - ADAM_B2) * (g * g)
    m_hat = m2 / (1.0 - ADAM_B1 ** ADAM_STEP)
    v_hat = v2 / (1.0 - ADAM_B2 ** ADAM_STEP)
    delta = -ADAM_LR * (m_hat / (jnp.sqrt(v_hat) + ADAM_EPS) + ADAM_WD * w)
    return delta, m2, v2


def _adamw(w, g, m, v, name):
    r, c = w.shape
    tr = r if r <= 256 else 256

    def body(w_ref, g_ref, m_ref, v_ref, d_ref, m2_ref, v2_ref):
        delta, m2, v2 = _adamw_math(w_ref[...], g_ref[...], m_ref[...], v_ref[...])
        d_ref[...] = delta
        m2_ref[...] = m2
        v2_ref[...] = v2

    blk = pl.BlockSpec((tr, c), lambda i: (i, 0))
    return pl.pallas_call(
        body, name=name, grid=(r // tr,), in_specs=[blk] * 4, out_specs=[blk] * 3,
        out_shape=[SDS((r, c), F32)] * 3, compiler_params=_cp(("parallel",)),
    )(w, g, m, v)


def _sum_parts(parts, name):
    k, r, c = parts.shape

    def body(p_ref, o_ref):
        acc = p_ref[0]
        for q in range(1, k):
            acc = acc + p_ref[q]
        o_ref[...] = acc

    return pl.pallas_call(
        body, name=name, grid=(1,),
        in_specs=[pl.BlockSpec((k, r, c), lambda i: (0, 0, 0))],
        out_specs=pl.BlockSpec((r, c), lambda i: (0, 0)),
        out_shape=SDS((r, c), F32), compiler_params=_cp(("arbitrary",)),
    )(parts)


def _piece_shape(full_shape, axis):
    r, c = full_shape
    return (r // 2, c // 4) if axis == 1 else (r // 8, c)


def _piece_spec(full_shape, axis, tr, chip_of, half_of):
    hr, wc = _piece_shape(full_shape, axis)
    nb = hr // tr
    if axis == 1:
        return pl.BlockSpec((tr, wc), lambda *a: (half_of(*a) * nb + a[-2], chip_of(*a)))
    return pl.BlockSpec((tr, wc), lambda *a: ((2 * chip_of(*a) + half_of(*a)) * nb + a[-2], 0))


def _place_cast(w, axis, pos, name):
    r, c = w.shape
    tr = min(r, 256)
    nb = r // tr
    full = (r, 4 * c) if axis == 1 else (4 * r, c)
    out_map = (lambda i, pos: (i, pos[0])) if axis == 1 else (lambda i, pos: (pos[0] * nb + i, 0))

    def body(pos_ref, w_ref, o_ref):
        o_ref[...] = w_ref[...].astype(o_ref.dtype)

    gs = pltpu.PrefetchScalarGridSpec(
        num_scalar_prefetch=1, grid=(nb,),
        in_specs=[pl.BlockSpec((tr, c), lambda i, pos: (i, 0))], out_specs=pl.BlockSpec((tr, c), out_map))
    return pl.pallas_call(body, name=name, grid_spec=gs, out_shape=SDS(full, BF),
                          compiler_params=_cp(("arbitrary",)))(pos, w)


def _pair_add(g, land, axis, pos, name):
    hr, wc = _piece_shape(g.shape, axis)
    tr = min(hr, 256)

    def body(pos_ref, g_ref, l_ref, o_ref):
        o_ref[0] = (g_ref[...] + l_ref[0].astype(F32)).astype(BF)

    blk = pl.BlockSpec((1, tr, wc), lambda j, i, pos: (j, i, 0))
    gs = pltpu.PrefetchScalarGridSpec(
        num_scalar_prefetch=1, grid=(4, hr // tr),
        in_specs=[_piece_spec(g.shape, axis, tr, lambda j, i, pos: j, lambda j, i, pos: pos[1]), blk], out_specs=blk)
    return pl.pallas_call(body, name=name, grid_spec=gs, out_shape=SDS((4, hr, wc), BF),
                          compiler_params=_cp(("arbitrary", "arbitrary")))(pos, g, land)


def _chip_sum(g, land1, land2, axis, pos, name):
    hr, wc = _piece_shape(g.shape, axis)
    tr = min(hr, 256)
    nb = hr // tr

    def body(pos_ref, g_ref, l1_ref, l2_ref, o_ref):
        acc = g_ref[...] + l1_ref[0].astype(F32)
        for q in range(3):
            acc = acc + l2_ref[q].astype(F32)
        o_ref[...] = acc

    gs = pltpu.PrefetchScalarGridSpec(
        num_scalar_prefetch=1, grid=(nb,),
        in_specs=[_piece_spec(g.shape, axis, tr, lambda i, pos: pos[0], lambda i, pos: pos[1]),
                  pl.BlockSpec((1, tr, wc), lambda i, pos: (pos[0], i, 0)),
                  pl.BlockSpec((3, tr, wc), lambda i, pos: (0, i, 0))],
        out_specs=pl.BlockSpec((tr, wc), lambda i, pos: (pos[1] * nb + i, 0)))
    return pl.pallas_call(body, name=name, grid_spec=gs, out_shape=SDS((2 * hr, wc), F32),
                          compiler_params=_cp(("arbitrary",)))(pos, g, land1, land2)


ANY = pl.BlockSpec(memory_space=pl.ANY)
COMM = pltpu.CompilerParams(has_side_effects=True)


def _on_each_device(fn):
    x, y, c = lax.axis_index("x"), lax.axis_index("y"), lax.axis_index("c")
    for sx in (0, 1):
        for sy in (0, 1):
            for sc in (0, 1):
                @pl.when(jnp.logical_and(jnp.logical_and(x == sx, y == sy), c == sc))
                def _(sx=sx, sy=sy, sc=sc):
                    fn(sx, sy, sc)


def _remote(src, dst, send_sem, recv_sem, to):
    return pltpu.make_async_remote_copy(src_ref=src, dst_ref=dst, send_sem=send_sem, recv_sem=recv_sem,
                                        device_id=to, device_id_type=MESH)


def _piece_ref(ref, axis, j, h):
    r, c = ref.shape
    hr, wc = _piece_shape((r, c), axis)
    if axis == 1:
        return ref.at[pl.ds(h * hr, hr), pl.ds(j * wc, wc)]
    return ref.at[pl.ds((2 * j + h) * hr, hr), :]


class _Stage:
    def __init__(self, ins, outs, n_sems, make_plan, aliases=None):
        self.ins, self.outs, self.n_sems, self.make_plan = list(ins), list(outs), n_sems, make_plan
        self.aliases = dict(aliases or {})


def _start(plan):
    def dev(sx, sy, sc):
        for cp, _, _ in plan(sx, sy, sc):
            cp.start()

    _on_each_device(dev)


def _finish(plan):
    def dev(sx, sy, sc):
        for _, sent, got in plan(sx, sy, sc):
            sent.wait_send()
            got.wait_recv()

    _on_each_device(dev)


def _comm_call(stage, name):
    n_in, n_out = len(stage.ins), len(stage.outs)

    def body(*refs):
        plan = stage.make_plan(refs[:n_in], refs[n_in:n_in + n_out], *refs[n_in + n_out:])
        _start(plan)
        _finish(plan)

    return pl.pallas_call(
        body, name=name, in_specs=[ANY] * n_in, out_specs=[ANY] * n_out, out_shape=stage.outs,
        input_output_aliases=stage.aliases,
        scratch_shapes=[pltpu.SemaphoreType.DMA((stage.n_sems,)), pltpu.SemaphoreType.DMA((stage.n_sems,))],
        compiler_params=COMM,
    )(*stage.ins)


class _SemsFrom:
    def __init__(self, sems, base):
        self.sems, self.base = sems, base

    @property
    def at(self):
        return self

    def __getitem__(self, k):
        return self.sems.at[self.base + k]


def _both(a, b):
    na, nb = len(a.ins), len(b.ins)
    ma = len(a.outs)

    def make_plan(ins, outs, send_sems, recv_sems):
        pa = a.make_plan(ins[:na], outs[:ma], send_sems, recv_sems)
        pb = b.make_plan(ins[na:], outs[ma:], _SemsFrom(send_sems, a.n_sems), _SemsFrom(recv_sems, a.n_sems))
        return lambda sx, sy, sc: pa(sx, sy, sc) + pb(sx, sy, sc)

    aliases = dict(a.aliases)
    aliases.update({na + i: ma + o for i, o in b.aliases.items()})
    return _Stage(a.ins + b.ins, a.outs + b.outs, a.n_sems + b.n_sems, make_plan, aliases)


def _same(cp):
    return (cp, cp, cp)


def _stage_gather_send(fulls, axes):
    n = len(fulls)

    def make_plan(ins, outs, send_sems, recv_sems):
        def plan(sx, sy, sc):
            cps = []
            for w in range(n):
                mine = _piece_ref(outs[w], axes[w], 2 * sx + sy, sc)
                for r, (fx, fy) in enumerate(CHIP_RELS):
                    k = 3 * w + r
                    to = (sx ^ fx, sy ^ fy, sc)
                    got = _piece_ref(outs[w], axes[w], 2 * (sx ^ fx) + (sy ^ fy), sc)
                    send = _remote(mine, mine, send_sems.at[k], recv_sems.at[k], to)
                    cps.append((send, send, _remote(got, got, send_sems.at[k], recv_sems.at[k], to)))
            return cps

        return plan

    return _Stage(fulls, [SDS(f.shape, f.dtype) for f in fulls], 3 * n, make_plan, {i: i for i in range(n)})


def _stage_gather_forward(fulls, axes):
    n = len(fulls)

    def make_plan(ins, outs, send_sems, recv_sems):
        def plan(sx, sy, sc):
            cps = []
            sib = (sx, sy, 1 - sc)
            for w in range(n):
                for r, (fx, fy) in enumerate(CHIP_RELS):
                    k = 3 * w + r
                    pj = 2 * (sx ^ fx) + (sy ^ fy)
                    have = _piece_ref(outs[w], axes[w], pj, sc)
                    want = _piece_ref(outs[w], axes[w], pj, 1 - sc)
                    send = _remote(have, have, send_sems.at[k], recv_sems.at[k], sib)
                    cps.append((send, send, _remote(want, want, send_sems.at[k], recv_sems.at[k], sib)))
            return cps

        return plan

    return _Stage(fulls, [SDS(f.shape, f.dtype) for f in fulls], 3 * n, make_plan, {i: i for i in range(n)})


def _in_place(fulls, n_sems, make_plan):
    return _Stage(fulls, [SDS(f.shape, f.dtype) for f in fulls], n_sems, make_plan, {i: i for i in range(len(fulls))})


def _stage_gather_neighbours(fulls, axes):
    n = len(fulls)

    def make_plan(ins, outs, send_sems, recv_sems):
        def plan(sx, sy, sc):
            cps = []
            for w in range(n):
                mine = _piece_ref(outs[w], axes[w], 2 * sx + sy, sc)
                for r, (px, py) in enumerate(((sx ^ 1, sy), (sx, sy ^ 1))):
                    got = _piece_ref(outs[w], axes[w], 2 * px + py, sc)
                    send = _remote(mine, mine, send_sems.at[2 * w + r], recv_sems.at[2 * w + r], (px, py, sc))
                    cps.append((send, send, _remote(got, got, send_sems.at[2 * w + r], recv_sems.at[2 * w + r],
                                                    (px, py, sc))))
            return cps

        return plan

    return _in_place(fulls, 2 * n, make_plan)


def _stage_gather_relay(fulls, axes):
    n = len(fulls)

    def make_plan(ins, outs, send_sems, recv_sems):
        def plan(sx, sy, sc):
            cps = []
            sib = (sx, sy, 1 - sc)
            jx, jy, jd = 2 * (sx ^ 1) + sy, 2 * sx + (sy ^ 1), 2 * (sx ^ 1) + (sy ^ 1)
            for w in range(n):
                piece = lambda j, h, w=w: _piece_ref(outs[w], axes[w], j, h)
                relayed, to = (jx, (sx, sy ^ 1, sc)) if sc == 0 else (jy, (sx ^ 1, sy, sc))
                k = 3 * w
                send = _remote(piece(relayed, sc), piece(relayed, sc), send_sems.at[k], recv_sems.at[k], to)
                cps.append((send, send, _remote(piece(jd, sc), piece(jd, sc), send_sems.at[k], recv_sems.at[k], to)))
                for r, j in enumerate((jx, jy)):
                    k = 3 * w + 1 + r
                    send = _remote(piece(j, sc), piece(j, sc), send_sems.at[k], recv_sems.at[k], sib)
                    cps.append((send, send, _remote(piece(j, 1 - sc), piece(j, 1 - sc), send_sems.at[k],
                                                    recv_sems.at[k], sib)))
            return cps

        return plan

    return _in_place(fulls, 3 * n, make_plan)


def _stage_gather_last(fulls, axes):
    n = len(fulls)

    def make_plan(ins, outs, send_sems, recv_sems):
        def plan(sx, sy, sc):
            cps = []
            sib = (sx, sy, 1 - sc)
            jd = 2 * (sx ^ 1) + (sy ^ 1)
            for w in range(n):
                have, want = _piece_ref(outs[w], axes[w], jd, sc), _piece_ref(outs[w], axes[w], jd, 1 - sc)
                send = _remote(have, have, send_sems.at[w], recv_sems.at[w], sib)
                cps.append((send, send, _remote(want, want, send_sems.at[w], recv_sems.at[w], sib)))
            return cps

        return plan

    return _in_place(fulls, n, make_plan)


def _stage_pair_exchange(grads, axes):
    n = len(grads)

    def make_plan(gs, land, send_sems, recv_sems):
        def plan(sx, sy, sc):
            return [_same(_remote(_piece_ref(gs[w], axes[w], jj, 1 - sc), land[w].at[jj], send_sems.at[4 * w + jj],
                                  recv_sems.at[4 * w + jj], (sx, sy, 1 - sc)))
                    for w in range(n) for jj in range(4)]

        return plan

    return _Stage(grads, [SDS((4,) + _piece_shape(g.shape, a), g.dtype) for g, a in zip(grads, axes)], 4 * n,
                  make_plan)


def _stage_chip_scatter(pieces):
    n = len(pieces)

    def make_plan(ps, land, send_sems, recv_sems):
        def plan(sx, sy, sc):
            return [_same(_remote(ps[w].at[2 * (sx ^ fx) + (sy ^ fy)], land[w].at[r], send_sems.at[3 * w + r],
                                  recv_sems.at[3 * w + r], (sx ^ fx, sy ^ fy, sc)))
                    for w in range(n) for r, (fx, fy) in enumerate(CHIP_RELS)]

        return plan

    return _Stage(pieces, [SDS((3,) + p.shape[1:], p.dtype) for p in pieces], 3 * n, make_plan)


def _stage_pair_share(shards):
    n = len(shards)

    def make_plan(ins, outs, send_sems, recv_sems):
        def plan(sx, sy, sc):
            cps = []
            sib = (sx, sy, 1 - sc)
            for w in range(n):
                hr = shards[w].shape[0] // 2
                mine = outs[w].at[pl.ds(sc * hr, hr), :]
                theirs = outs[w].at[pl.ds((1 - sc) * hr, hr), :]
                send = _remote(mine, mine, send_sems.at[w], recv_sems.at[w], sib)
                cps.append((send, send, _remote(theirs, theirs, send_sems.at[w], recv_sems.at[w], sib)))
            return cps

        return plan

    return _Stage(shards, [SDS(g.shape, g.dtype) for g in shards], n, make_plan, {i: i for i in range(n)})


def _stage_gather_small(stack):
    def make_plan(ins, outs, send_sems, recv_sems):
        def plan(sx, sy, sc):
            mine = outs[0].at[4 * sx + 2 * sy + sc]
            return [_same(_remote(mine, mine, send_sems.at[k], recv_sems.at[k], (sx ^ fx, sy ^ fy, sc ^ fc)))
                    for k, (fx, fy, fc) in enumerate(DEV_RELS)]

        return plan

    return _Stage([stack], [SDS(stack.shape, stack.dtype)], 7, make_plan, {0: 0})


def kernel(x, w_in, conv_a_w, w_out_a, conv_b_w, conv_b_bias, ln_b_gamma, ln_b_beta, w_out_b, w_o, ln1_gamma, ln1_beta, w_up, w_down, ln2_gamma, ln2_beta, loss_target, m_w_in, m_conv_a_w, m_w_out_a, m_conv_b_w, m_conv_b_bias, m_ln_b_gamma, m_ln_b_beta, m_w_out_b, m_w_o, m_ln1_gamma, m_ln1_beta, m_w_up, m_w_down, m_ln2_gamma, m_ln2_beta, v_w_in, v_conv_a_w, v_w_out_a, v_conv_b_w, v_conv_b_bias, v_ln_b_gamma, v_ln_b_beta, v_w_out_b, v_w_o, v_ln1_gamma, v_ln1_beta, v_w_up, v_w_down, v_ln2_gamma, v_ln2_beta):
    s, d = x.shape[1], x.shape[2]
    xs = x.reshape(s, d)
    tgt = loss_target.reshape(s, d)
    dq = d // 4
    chip = 2 * lax.axis_index("x") + lax.axis_index("y")
    core = lax.axis_index("c")
    pos = jnp.stack([chip, core]).astype(jnp.int32)
    names = ("w_in", "w_out_a", "w_out_b", "w_o", "w_up", "w_down")
    axes = (1, 0, 0, 0, 1, 0)

    conv_pack = jnp.concatenate([jnp.pad(conv_a_w, ((0, 8 - K_A), (0, 0))), jnp.pad(conv_b_w, ((0, 32 - K_B), (0, 0))),
                                 jnp.zeros((8, dq), F32)], axis=0)
    conv_full = lax.dynamic_update_slice(jnp.zeros((conv_pack.shape[0], d), F32), conv_pack, (0, chip * dq))
    fulls = [_place_cast(w, a, pos, "place_" + nm)
             for w, a, nm in zip((w_in, w_out_a, w_out_b, w_o, w_up, w_down), axes, names)]
    vec = lambda a: a.reshape(1, d)
    bias_b, lbg, lbb = vec(conv_b_bias), vec(ln_b_gamma), vec(ln_b_beta)
    l1g, l1b, l2g, l2b = vec(ln1_gamma), vec(ln1_beta), vec(ln2_gamma), vec(ln2_beta)

    (p, xb), landed = _in_proj_own(xs, w_in, pos, _stage_gather_neighbours([fulls[0], conv_full], (1, 1)))
    landed = _comm_call(_stage_gather_relay(landed, (1, 1)), "gather_relay_w_in")
    wi, convs = _comm_call(_stage_gather_last(landed, (1, 1)), "gather_last_w_in")
    cwa, cwb = convs[0:K_A], convs[8:8 + K_B]
    p, small3 = _in_proj_rest(xb, wi, p, pos, _stage_gather_send(fulls[1:4], axes[1:4]))
    (conva, yapre, nb, rstdb, u3), landed = _conv_fwd(
        p, cwa, cwb, bias_b, lbg, lbb, d,
        _both(_stage_gather_forward(small3, axes[1:4]), _stage_gather_send(fulls[4:6], axes[4:6])))
    woa, wob, wo = landed[:3]
    (ya, yb, merged, n1, rstd1), (wup, wdown) = _mixer_out(yapre, u3, p, xs, woa, wob, wo, d,
                                                           _stage_gather_forward(landed[3:], axes[4:6]))
    x1b, hb, dhpre, dr2b, dr1, dr1b, acc_mlp = _mlp(n1, rstd1, tgt, wup, wdown, l1g, l1b, l2g, l2b)
    g_up, gb_up, _ = _grad_w(x1b, dhpre, "grad_w_up")
    g_down, gb_down, _ = _grad_w(hb, dr2b, "grad_w_down")
    dya, dyb, dba, dga, dgb, dca, du1, acc_mix = _mixer_bwd_local(dr1b, p, ya, yb, conva, nb, rstdb, wo, woa, wob,
                                                                   lbg, lbb)
    g_oa, gb_oa, _ = _grad_w(yapre, dya, "grad_w_out_a")
    g_ob, gb_ob, _ = _grad_w(u3, dyb, "grad_w_out_b")
    g_o, gb_o, _ = _grad_w(merged, dr1b, "grad_w_o")

    early, e_axes, e_names = [g_oa, g_ob, g_o, g_up, g_down], axes[1:], names[1:]
    (dp, dcwa, dcwb), land1 = _conv_bwd(dca, du1, p, dba, dga, dgb, cwa, cwb, d,
                                        _stage_pair_exchange([gb_oa, gb_ob, gb_o, gb_up, gb_down], e_axes))
    pieces = [_pair_add(g, l, a, pos, "pair_add_" + nm) for g, l, a, nm in zip(early, land1, e_axes, e_names)]
    pack = jnp.concatenate([dcwa, dcwb, acc_mix, acc_mlp], axis=0)
    stack = lax.dynamic_update_slice(jnp.zeros((8,) + pack.shape, F32), pack[None], (2 * chip + core, 0, 0))
    g_wi, gb_wi, landed = _grad_w(xb, dp, "grad_w_in",
                                  _both(_stage_chip_scatter(pieces), _stage_gather_small(stack)))
    land2, stack = landed[:-1], landed[-1]
    halves = [_chip_sum(g, l1, l2, a, pos, "chip_sum_" + nm)
              for g, l1, l2, a, nm in zip(early, land1, land2, e_axes, e_names)]
    (land1_in,) = _comm_call(_stage_pair_exchange([gb_wi], (1,)), "pair_exchange_w_in")
    piece_in = _pair_add(g_wi, land1_in, 1, pos, "pair_add_w_in")
    grad_x, (land2_in,) = _grad_x(dr1, dp, wi, _stage_chip_scatter([piece_in]))
    half_in = _chip_sum(g_wi, land1_in, land2_in, 1, pos, "chip_sum_w_in")
    g_in, g_oa, g_ob, g_o, g_up, g_down = _comm_call(_stage_pair_share([half_in] + halves), "pair_share")
    small = _sum_parts(stack, "small_sum")
    g_ca = lax.dynamic_slice(small, (0, chip * dq), (K_A, dq))
    g_cb = lax.dynamic_slice(small, (8, chip * dq), (K_B, dq))
    g_vec = jnp.stack([small[r] for r in (42, 40, 41, 51, 52, 49, 50)])

    loss = lax.psum((0.5 / d) * jnp.sum(acc_mlp[0]), ("x", "y", "c"))

    big = {}
    for name, w, g, m, v in (("w_in", w_in, g_in, m_w_in, v_w_in), ("w_out_a", w_out_a, g_oa, m_w_out_a, v_w_out_a),
                             ("w_out_b", w_out_b, g_ob, m_w_out_b, v_w_out_b), ("w_o", w_o, g_o, m_w_o, v_w_o),
                             ("w_up", w_up, g_up, m_w_up, v_w_up), ("w_down", w_down, g_down, m_w_down, v_w_down),
                             ("conv_a_w", conv_a_w, g_ca, m_conv_a_w, v_conv_a_w),
                             ("conv_b_w", conv_b_w, g_cb, m_conv_b_w, v_conv_b_w)):
        big[name] = (g,) + tuple(_adamw(w, g, m, v, "adamw_" + name))
    vec_names = ("conv_b_bias", "ln_b_gamma", "ln_b_beta", "ln1_gamma", "ln1_beta", "ln2_gamma", "ln2_beta")
    w7 = jnp.stack([conv_b_bias, ln_b_gamma, ln_b_beta, ln1_gamma, ln1_beta, ln2_gamma, ln2_beta])
    m7 = jnp.stack([m_conv_b_bias, m_ln_b_gamma, m_ln_b_beta, m_ln1_gamma, m_ln1_beta, m_ln2_gamma, m_ln2_beta])
    v7 = jnp.stack([v_conv_b_bias, v_ln_b_gamma, v_ln_b_beta, v_ln1_gamma, v_ln1_beta, v_ln2_gamma, v_ln2_beta])
    d7, nm7, nv7 = _adamw(w7, g_vec, m7, v7, "adamw_vectors")
    for q, name in enumerate(vec_names):
        big[name] = (g_vec[q], d7[q], nm7[q], nv7[q])

    order = ("w_in", "conv_a_w", "w_out_a", "conv_b_w", "conv_b_bias", "ln_b_gamma", "ln_b_beta", "w_out_b", "w_o",
             "ln1_gamma", "ln1_beta", "w_up", "w_down", "ln2_gamma", "ln2_beta")
    outs = [loss, grad_x.reshape(x.shape)]
    for part in range(4):
        outs += [big[name][part] for name in order]
    return tuple(outs)
```

```python
import jax
import jax.numpy as jnp
from jax import lax
from jax.experimental import pallas as pl
from jax.experimental.pallas import tpu as pltpu

F32 = jnp.float32
BF = jnp.bfloat16
SDS = jax.ShapeDtypeStruct
MESH = pl.DeviceIdType.MESH

ALPHA = 2.0 ** 0.25
LN_EPS = 1e-5
K_A = 3
K_B = 31
HALO = 32
CONV_ROWS = 64
CONV_LANES = 128
ADAM_LR = 0.001
ADAM_B1 = 0.9
ADAM_B2 = 0.999
ADAM_EPS = 1e-08
ADAM_WD = 0.01
ADAM_STEP = 10
P_DT = BF
CHIP_RELS = ((1, 0), (0, 1), (1, 1))
DEV_RELS = tuple((fx, fy, fc) for fx in (0, 1) for fy in (0, 1) for fc in (0, 1))[1:]


def _cp(sem=None, vmem_mb=56, side_effects=False):
    return pltpu.CompilerParams(dimension_semantics=sem, vmem_limit_bytes=vmem_mb << 20,
                                has_side_effects=side_effects)


def _const(shape):
    return pl.BlockSpec(shape, lambda *_: (0,) * len(shape), pipeline_mode=pl.Buffered(1))


def _sig(v):
    return jax.nn.sigmoid(v)


def _ln_fwd(r):
    mu = jnp.mean(r, axis=-1, keepdims=True)
    xc = r - mu
    var = jnp.mean(xc * xc, axis=-1, keepdims=True)
    rstd = lax.rsqrt(var + LN_EPS)
    return xc * rstd, rstd


def _ln_bwd(dn, n, rstd):
    m1 = jnp.mean(dn, axis=-1, keepdims=True)
    m2 = jnp.mean(dn * n, axis=-1, keepdims=True)
    return rstd * (dn - m1 - n * m2)


def _dot(a, b):
    return jnp.dot(a, b, preferred_element_type=F32)


def _dot_nt(a, b):
    return lax.dot_general(a, b, (((1,), (1,)), ((), ())), preferred_element_type=F32)


def _dot_tn(a, b):
    return lax.dot_general(a, b, (((0,), (0,)), ((), ())), preferred_element_type=F32)


def _tile(n, pref):
    if n <= pref:
        return n
    return max(t for t in range(128, pref + 1, 128) if n % t == 0)


def _rowsum8(v):
    acc = v[0:8]
    for g in range(1, v.shape[0] // 8):
        acc = acc + v[8 * g:8 * g + 8]
    return acc


def _taps(win, offsets, rows):
    r_all = win.shape[0]
    by_res = {}
    for k, o in enumerate(offsets):
        by_res.setdefault(o % 8, []).append((k, o // 8))
    for s, taps in sorted(by_res.items()):
        r = win if s == 0 else pltpu.roll(win, r_all - s, 0)
        for k, q in taps:
            yield k, r[8 * q:8 * q + rows]


CAUSAL_A = [HALO - (K_A - 1) + k for k in range(K_A)]
CAUSAL_B = [HALO - (K_B - 1) + k for k in range(K_B)]
ANTI_A = [K_A - 1 - k for k in range(K_A)]
ANTI_B = [K_B - 1 - k for k in range(K_B)]


def _host_call(body, *, name, grid, in_specs, out_specs, out_shape, scratch_shapes, args, hosted, prefetch=None,
               aliases=None):
    n_in, n_out, n_scr = len(in_specs), len(out_specs), len(scratch_shapes)
    n_pre = 0 if prefetch is None else 1
    pre = () if prefetch is None else (prefetch,)
    sem = ("arbitrary",) * len(grid)
    own_aliases = {n_pre + a: b for a, b in (aliases or {}).items()}

    def call(kernel_body, ins, outs, shapes, scratch, all_aliases, side_effects, operands):
        gs = pltpu.PrefetchScalarGridSpec(num_scalar_prefetch=n_pre, grid=grid, in_specs=ins, out_specs=outs,
                                          scratch_shapes=scratch)
        return pl.pallas_call(kernel_body, name=name, grid_spec=gs, out_shape=shapes,
                              input_output_aliases=all_aliases,
                              compiler_params=_cp(sem, side_effects=side_effects))(*pre, *operands)

    if hosted is None:
        res = call(body, list(in_specs), list(out_specs), list(out_shape), list(scratch_shapes), own_aliases, False,
                   args)
        return list(res), []
    h_in, h_out = len(hosted.ins), len(hosted.outs)

    def full_body(*refs):
        pre_refs, refs = refs[:n_pre], refs[n_pre:]
        ins, refs = refs[:n_in], refs[n_in:]
        hins, refs = refs[:h_in], refs[h_in:]
        outs, refs = refs[:n_out], refs[n_out:]
        houts, refs = refs[:h_out], refs[h_out:]
        scr, (send_sems, recv_sems) = refs[:n_scr], refs[n_scr:]
        plan = hosted.make_plan(hins, houts, send_sems, recv_sems)
        first = last = None
        for a, size in enumerate(grid):
            at0, at1 = pl.program_id(a) == 0, pl.program_id(a) == size - 1
            first = at0 if first is None else jnp.logical_and(first, at0)
            last = at1 if last is None else jnp.logical_and(last, at1)

        @pl.when(first)
        def _():
            _start(plan)

        body(*pre_refs, *ins, *outs, *scr)

        @pl.when(last)
        def _():
            _finish(plan)

    all_aliases = dict(own_aliases)
    all_aliases.update({n_pre + n_in + a: n_out + b for a, b in hosted.aliases.items()})
    res = call(full_body, list(in_specs) + [ANY] * h_in, list(out_specs) + [ANY] * h_out,
               list(out_shape) + list(hosted.outs),
               list(scratch_shapes) + [pltpu.SemaphoreType.DMA((hosted.n_sems,)),
                                       pltpu.SemaphoreType.DMA((hosted.n_sems,))],
               all_aliases, True, (*args, *hosted.ins))
    return list(res[:n_out]), list(res[n_out:])


def _in_proj_own(x, w_shard, pos, hosted):
    s, d = x.shape
    tn = w_shard.shape[1]
    tm = min(s, 1024)

    def body(pos_ref, x_ref, w_ref, p_ref, xb_ref, wb):
        @pl.when(pl.program_id(0) == 0)
        def _():
            wb[...] = w_ref[...].astype(BF)

        xb = x_ref[...].astype(BF)
        xb_ref[...] = xb
        p_ref[...] = _dot(xb, wb[...]).astype(p_ref.dtype)

    return _host_call(
        body, name="in_proj_own", grid=(s // tm,), prefetch=pos,
        in_specs=[pl.BlockSpec((tm, d), lambda i, pos: (i, 0)),
                  pl.BlockSpec((d, tn), lambda i, pos: (0, 0), pipeline_mode=pl.Buffered(1))],
        out_specs=[pl.BlockSpec((tm, tn), lambda i, pos: (i, pos[0])), pl.BlockSpec((tm, d), lambda i, pos: (i, 0))],
        out_shape=[SDS((s, 4 * tn), P_DT), SDS((s, d), BF)], scratch_shapes=[pltpu.VMEM((d, tn), BF)],
        args=(x, w_shard), hosted=hosted)


def _in_proj_rest(xb, wi, p, pos, hosted):
    s, d = xb.shape
    n = wi.shape[1]
    tm, tn = min(s, 1024), n // 4

    def body(pos_ref, xb_ref, w_ref, p_in, p_ref):
        p_ref[...] = _dot(xb_ref[...], w_ref[...]).astype(p_ref.dtype)

    col = lambda i, j, pos: (pos[0] + 1 + j) % 4
    (p,), extra = _host_call(
        body, name="in_proj_rest", grid=(s // tm, 3), prefetch=pos, aliases={2: 0},
        in_specs=[pl.BlockSpec((tm, d), lambda i, j, pos: (i, 0)),
                  pl.BlockSpec((d, tn), lambda i, j, pos: (0, col(i, j, pos))), ANY],
        out_specs=[pl.BlockSpec((tm, tn), lambda i, j, pos: (i, col(i, j, pos)))],
        out_shape=[SDS((s, n), P_DT)], scratch_shapes=[], args=(xb, wi, p), hosted=hosted)
    return p, extra


def _col_spec(tm, d, k):
    return pl.BlockSpec((tm, d), lambda i, k=k: (i, k))


def _prev_halo_spec(tm, d, k):
    r = tm // HALO
    return pl.BlockSpec((HALO, d), lambda i, k=k: (jnp.maximum(i * r - 1, 0), k))


def _next_halo_spec(tm, d, k, s):
    r = tm // HALO
    last = s // HALO - 1
    return pl.BlockSpec((HALO, d), lambda i, k=k: (jnp.minimum((i + 1) * r, last), k))


def _conv_fwd(p, cwa, cwb, bias_b, lbg, lbb, d, hosted=None):
    s = p.shape[0]
    tm = min(s, 256)
    nt = s // tm

    def body(ba_ref, ca_ref, va_ref, vb_ref, gb_ref, hca_ref, hva_ref, hvb_ref, hgb_ref,
             cwa_ref, cwb_ref, bias_ref, lbg_ref, lbb_ref,
             conva_ref, yapre_ref, nb_ref, rstdb_ref, u3_ref,
             zbuf, ubuf, u1buf):
        i = pl.program_id(0)
        keep = (i > 0).astype(F32)
        zbuf[pl.ds(0, HALO), :] = hca_ref[...].astype(F32) * hva_ref[...].astype(F32) * keep
        ubuf[pl.ds(0, HALO), :] = hvb_ref[...].astype(F32) * _sig(hgb_ref[...].astype(F32)) * keep
        zbuf[pl.ds(HALO, tm), :] = ca_ref[...].astype(F32) * va_ref[...].astype(F32)
        ubuf[pl.ds(HALO, tm), :] = vb_ref[...].astype(F32) * _sig(gb_ref[...].astype(F32))

        def chunk(j, carry):
            r0 = pl.multiple_of(j * CONV_ROWS, CONV_ROWS)
            rows = pl.ds(r0, CONV_ROWS)
            for lc in range(d // CONV_LANES):
                ls = pl.ds(lc * CONV_LANES, CONV_LANES)
                acc = jnp.zeros((CONV_ROWS, CONV_LANES), F32)
                for k, sl in _taps(zbuf[pl.ds(r0, CONV_ROWS + HALO), ls], CAUSAL_A, CONV_ROWS):
                    acc = acc + cwa_ref[pl.ds(k, 1), ls] * sl
                conva_ref[rows, ls] = acc
                yapre_ref[rows, ls] = (ba_ref[rows, ls].astype(F32) * acc).astype(BF)
                acc = jnp.zeros((CONV_ROWS, CONV_LANES), F32)
                for k, sl in _taps(ubuf[pl.ds(r0, CONV_ROWS + HALO), ls], CAUSAL_B, CONV_ROWS):
                    acc = acc + cwb_ref[pl.ds(k, 1), ls] * sl
                u1buf[rows, ls] = acc + bias_ref[:, ls]
            return carry

        lax.fori_loop(0, tm // CONV_ROWS, chunk, 0)
        nb, rstd = _ln_fwd(u1buf[...])
        nb_ref[...] = nb
        rstdb_ref[...] = rstd
        u2 = nb * lbg_ref[...] + lbb_ref[...]
        u3_ref[...] = (u2 * _sig(u2)).astype(BF)

    vec = _const((1, d))
    return _host_call(
        body, name="conv_fwd", grid=(nt,),
        in_specs=[_col_spec(tm, d, k) for k in range(5)] + [_prev_halo_spec(tm, d, k) for k in (1, 2, 3, 4)]
        + [_const((K_A, d)), _const((K_B, d)), vec, vec, vec],
        out_specs=[pl.BlockSpec((tm, d), lambda i: (i, 0)), pl.BlockSpec((tm, d), lambda i: (i, 0)),
                   pl.BlockSpec((tm, d), lambda i: (i, 0)), pl.BlockSpec((tm, 1), lambda i: (i, 0)),
                   pl.BlockSpec((tm, d), lambda i: (i, 0))],
        out_shape=[SDS((s, d), F32), SDS((s, d), BF), SDS((s, d), F32), SDS((s, 1), F32), SDS((s, d), BF)],
        scratch_shapes=[pltpu.VMEM((HALO + tm, d), F32), pltpu.VMEM((HALO + tm, d), F32), pltpu.VMEM((tm, d), F32)],
        args=(p, p, p, p, p, p, p, p, p, cwa, cwb, bias_b, lbg, lbb), hosted=hosted)


def _mixer_out(yapre, u3, p, x, woa, wob, wo, d, hosted=None):
    s = x.shape[0]
    tm = min(s, 512)

    def body(yapre_ref, u3_ref, ga_ref, gb_ref, x_ref, woa_ref, wob_ref, wo_ref,
             ya_ref, yb_ref, merged_ref, n1_ref, rstd1_ref):
        ya = _dot(yapre_ref[...], woa_ref[...])
        yb = _dot(u3_ref[...], wob_ref[...])
        ya_ref[...] = ya
        yb_ref[...] = yb
        merged = (_sig(ga_ref[...].astype(F32)) * ya + _sig(gb_ref[...].astype(F32)) * yb).astype(BF)
        merged_ref[...] = merged
        r1 = F32(ALPHA) * x_ref[...] + _dot(merged, wo_ref[...])
        n1, rstd1 = _ln_fwd(r1)
        n1_ref[...] = n1
        rstd1_ref[...] = rstd1

    row = pl.BlockSpec((tm, d), lambda i: (i, 0))
    return _host_call(
        body, name="mixer_out", grid=(s // tm,),
        in_specs=[row, row, _col_spec(tm, d, 5), _col_spec(tm, d, 6), row,
                  _const((d, d)), _const((d, d)), _const((d, d))],
        out_specs=[row, row, row, row, pl.BlockSpec((tm, 1), lambda i: (i, 0))],
        out_shape=[SDS((s, d), F32), SDS((s, d), F32), SDS((s, d), BF), SDS((s, d), F32), SDS((s, 1), F32)],
        scratch_shapes=[], args=(yapre, u3, p, p, x, woa, wob, wo), hosted=hosted)


def _mlp(n1, rstd1, tgt, wup, wdown, l1g, l1b, l2g, l2b):
    s, d = n1.shape
    dff = wup.shape[1]
    tm = min(s, 256)
    fc = min(dff, 1024)
    nq = dff // fc

    def body(n1_ref, rstd1_ref, tgt_ref, wup_ref, wdown_ref, l1g_ref, l1b_ref, l2g_ref, l2b_ref,
             x1b_ref, hb_ref, dhpre_ref, dr2b_ref, dr1_ref, dr1b_ref, acc_ref, rbuf):
        i = pl.program_id(0)
        n1v = n1_ref[...]
        x1 = n1v * l1g_ref[...] + l1b_ref[...]
        x1b = x1.astype(BF)
        x1b_ref[...] = x1b
        ff = jnp.zeros((tm, d), F32)
        for q in range(nq):
            cs = pl.ds(q * fc, fc)
            r = jnp.maximum(_dot(x1b, wup_ref[:, cs]), 0.0)
            rbuf[:, cs] = r
            hq = (r * r).astype(BF)
            hb_ref[:, cs] = hq
            ff = ff + _dot(hq, wdown_ref[cs, :])
        n2, rstd2 = _ln_fwd(F32(ALPHA) * x1 + ff)
        x2 = n2 * l2g_ref[...] + l2b_ref[...]
        err = x2 - tgt_ref[...]
        dx2 = err * F32(1.0 / d)
        dr2 = _ln_bwd(dx2 * l2g_ref[...], n2, rstd2)
        dr2b = dr2.astype(BF)
        dr2b_ref[...] = dr2b
        dx1 = F32(ALPHA) * dr2
        for q in range(nq):
            cs = pl.ds(q * fc, fc)
            dh = _dot_nt(dr2b, wdown_ref[cs, :])
            dhp = (dh * (2.0 * rbuf[:, cs])).astype(BF)
            dhpre_ref[:, cs] = dhp
            dx1 = dx1 + _dot_nt(dhp, wup_ref[:, cs])
        dr1 = _ln_bwd(dx1 * l1g_ref[...], n1v, rstd1_ref[...])
        dr1_ref[...] = dr1
        dr1b_ref[...] = dr1.astype(BF)

        @pl.when(i == 0)
        def _():
            acc_ref[...] = jnp.zeros_like(acc_ref)

        for q, val in enumerate((err * err, dx2 * n2, dx2, dx1 * n1v, dx1)):
            acc_ref[pl.ds(q, 1), :] += jnp.sum(val, axis=0, keepdims=True)

    row = pl.BlockSpec((tm, d), lambda i: (i, 0))
    wide = pl.BlockSpec((tm, dff), lambda i: (i, 0))
    vec = _const((1, d))
    return pl.pallas_call(
        body, name="mlp_fwd_bwd", grid=(s // tm,),
        in_specs=[row, pl.BlockSpec((tm, 1), lambda i: (i, 0)), row, _const((d, dff)), _const((dff, d)),
                  vec, vec, vec, vec],
        out_specs=[row, wide, wide, row, row, row, pl.BlockSpec((8, d), lambda i: (0, 0))],
        out_shape=[SDS((s, d), BF), SDS((s, dff), BF), SDS((s, dff), BF), SDS((s, d), BF), SDS((s, d), F32),
                   SDS((s, d), BF), SDS((8, d), F32)],
        scratch_shapes=[pltpu.VMEM((tm, dff), F32)],
        compiler_params=_cp(("arbitrary",)),
    )(n1, rstd1, tgt, wup, wdown, l1g, l1b, l2g, l2b)


def _mixer_bwd_local(dr1b, p, ya, yb, conva, nb, rstdb, wo, woa, wob, lbg, lbb):
    s, d = ya.shape
    tm = min(s, 256)

    def body(dr1b_ref, ba_ref, ga_ref, gb_ref, ya_ref, yb_ref, conva_ref, nb_ref, rstdb_ref,
             wo_ref, woa_ref, wob_ref, lbg_ref, lbb_ref,
             dya_ref, dyb_ref, dba_ref, dga_ref, dgb_ref, dca_ref, du1_ref, acc_ref):
        i = pl.program_id(0)
        dmerged = _dot_nt(dr1b_ref[...], wo_ref[...])
        sa = _sig(ga_ref[...].astype(F32))
        sb = _sig(gb_ref[...].astype(F32))
        dya = (dmerged * sa).astype(BF)
        dyb = (dmerged * sb).astype(BF)
        dya_ref[...] = dya
        dyb_ref[...] = dyb
        dga_ref[...] = (dmerged * ya_ref[...] * (sa * (1.0 - sa))).astype(BF)
        dgb_ref[...] = (dmerged * yb_ref[...] * (sb * (1.0 - sb))).astype(BF)
        dyapre = _dot_nt(dya, woa_ref[...])
        dba_ref[...] = (dyapre * conva_ref[...]).astype(BF)
        dca_ref[...] = dyapre * ba_ref[...].astype(F32)
        du3 = _dot_nt(dyb, wob_ref[...])
        nbv = nb_ref[...]
        u2 = nbv * lbg_ref[...] + lbb_ref[...]
        sg = _sig(u2)
        du2 = du3 * (sg * (1.0 + u2 * (1.0 - sg)))
        du1 = _ln_bwd(du2 * lbg_ref[...], nbv, rstdb_ref[...])
        du1_ref[...] = du1

        @pl.when(i == 0)
        def _():
            acc_ref[...] = jnp.zeros_like(acc_ref)

        for q, val in enumerate((du2 * nbv, du2, du1)):
            acc_ref[pl.ds(q, 1), :] += jnp.sum(val, axis=0, keepdims=True)

    row = pl.BlockSpec((tm, d), lambda i: (i, 0))
    vec = _const((1, d))
    return pl.pallas_call(
        body, name="mixer_bwd_local", grid=(s // tm,),
        in_specs=[row, _col_spec(tm, d, 0), _col_spec(tm, d, 5), _col_spec(tm, d, 6), row, row, row, row,
                  pl.BlockSpec((tm, 1), lambda i: (i, 0)), _const((d, d)), _const((d, d)), _const((d, d)), vec, vec],
        out_specs=[row, row, row, row, row, row, row, pl.BlockSpec((8, d), lambda i: (0, 0))],
        out_shape=[SDS((s, d), BF)] * 5 + [SDS((s, d), F32), SDS((s, d), F32), SDS((8, d), F32)],
        compiler_params=_cp(("arbitrary",)),
    )(dr1b, p, p, p, ya, yb, conva, nb, rstdb, wo, woa, wob, lbg, lbb)


def _conv_bwd(dca, du1, p, dba, dga, dgb, cwa, cwb, d, hosted=None):
    s = dca.shape[0]
    tm = min(s, 256)
    nt = s // tm

    def body(dca_ref, du1_ref, ndca_ref, ndu1_ref, ca_ref, va_ref, vb_ref, gb_ref,
             dba_ref, dga_ref, dgb_ref, cwa_ref, cwb_ref,
             dp_ref, dcwa_ref, dcwb_ref,
             dcabuf, du1buf, sgbuf, acca, accb):
        i = pl.program_id(0)
        keep_next = (i < nt - 1).astype(F32)

        @pl.when(i == 0)
        def _():
            acca[...] = jnp.zeros_like(acca)
            accb[...] = jnp.zeros_like(accb)

        sgbuf[...] = _sig(gb_ref[...].astype(F32))
        dcabuf[pl.ds(0, tm), :] = dca_ref[...]
        dcabuf[pl.ds(tm, HALO), :] = ndca_ref[...] * keep_next
        du1buf[pl.ds(0, tm), :] = du1_ref[...]
        du1buf[pl.ds(tm, HALO), :] = ndu1_ref[...] * keep_next
        dp_ref[:, pl.ds(0, d)] = dba_ref[...]
        dp_ref[:, pl.ds(5 * d, d)] = dga_ref[...]
        dp_ref[:, pl.ds(6 * d, d)] = dgb_ref[...]

        def chunk(j, carry):
            r0 = pl.multiple_of(j * CONV_ROWS, CONV_ROWS)
            rows = pl.ds(r0, CONV_ROWS)
            for lc in range(d // CONV_LANES):
                lo = lc * CONV_LANES
                ls = pl.ds(lo, CONV_LANES)
                cac = ca_ref[rows, ls].astype(F32)
                vac = va_ref[rows, ls].astype(F32)
                zc = cac * vac
                acc = jnp.zeros((CONV_ROWS, CONV_LANES), F32)
                for k, sl in _taps(dcabuf[pl.ds(r0, CONV_ROWS + HALO), ls], ANTI_A, CONV_ROWS):
                    acc = acc + cwa_ref[pl.ds(k, 1), ls] * sl
                    acca[pl.ds(8 * k, 8), ls] += _rowsum8(sl * zc)
                dp_ref[rows, pl.ds(d + lo, CONV_LANES)] = (acc * vac).astype(BF)
                dp_ref[rows, pl.ds(2 * d + lo, CONV_LANES)] = (acc * cac).astype(BF)
                sgc = sgbuf[rows, ls]
                vbc = vb_ref[rows, ls].astype(F32)
                uc = vbc * sgc
                acc = jnp.zeros((CONV_ROWS, CONV_LANES), F32)
                for k, sl in _taps(du1buf[pl.ds(r0, CONV_ROWS + HALO), ls], ANTI_B, CONV_ROWS):
                    acc = acc + cwb_ref[pl.ds(k, 1), ls] * sl
                    accb[pl.ds(8 * k, 8), ls] += _rowsum8(sl * uc)
                dp_ref[rows, pl.ds(3 * d + lo, CONV_LANES)] = (acc * sgc).astype(BF)
                dp_ref[rows, pl.ds(4 * d + lo, CONV_LANES)] = (acc * vbc * (sgc * (1.0 - sgc))).astype(BF)
            return carry

        lax.fori_loop(0, tm // CONV_ROWS, chunk, 0)

        @pl.when(i == nt - 1)
        def _():
            dcwa_ref[...] = jnp.zeros_like(dcwa_ref)
            dcwb_ref[...] = jnp.zeros_like(dcwb_ref)
            for k in range(K_A):
                dcwa_ref[pl.ds(k, 1), :] = jnp.sum(acca[pl.ds(8 * k, 8), :], axis=0, keepdims=True)
            for k in range(K_B):
                dcwb_ref[pl.ds(k, 1), :] = jnp.sum(accb[pl.ds(8 * k, 8), :], axis=0, keepdims=True)

    row = pl.BlockSpec((tm, d), lambda i: (i, 0))
    nxt = _next_halo_spec(tm, d, 0, s)
    return _host_call(
        body, name="conv_bwd", grid=(nt,),
        in_specs=[row, row, nxt, nxt] + [_col_spec(tm, d, k) for k in (1, 2, 3, 4)]
        + [row, row, row, _const((K_A, d)), _const((K_B, d))],
        out_specs=[pl.BlockSpec((tm, 7 * d), lambda i: (i, 0)), pl.BlockSpec((8, d), lambda i: (0, 0)),
                   pl.BlockSpec((32, d), lambda i: (0, 0))],
        out_shape=[SDS((s, 7 * d), BF), SDS((8, d), F32), SDS((32, d), F32)],
        scratch_shapes=[pltpu.VMEM((tm + HALO, d), F32), pltpu.VMEM((tm + HALO, d), F32),
                        pltpu.VMEM((tm, d), F32), pltpu.VMEM((8 * K_A, d), F32), pltpu.VMEM((8 * K_B, d), F32)],
        args=(dca, du1, dca, du1, p, p, p, p, dba, dga, dgb, cwa, cwb), hosted=hosted)


def _grad_w(a, b, name, hosted=None):
    s, m = a.shape
    n = b.shape[1]
    tm, tn, tk = _tile(m, 1024), _tile(n, 1024), _tile(s, 2048)
    nk = s // tk

    def body(a_ref, b_ref, o_ref, ob_ref):
        k = pl.program_id(2)

        @pl.when(k == 0)
        def _():
            o_ref[...] = jnp.zeros_like(o_ref)

        o_ref[...] += _dot_tn(a_ref[...], b_ref[...])

        @pl.when(k == nk - 1)
        def _():
            ob_ref[...] = o_ref[...].astype(BF)

    blk = pl.BlockSpec((tm, tn), lambda i, j, k: (i, j))
    (g, gb), extra = _host_call(
        body, name=name, grid=(m // tm, n // tn, nk),
        in_specs=[pl.BlockSpec((tk, tm), lambda i, j, k: (k, i)), pl.BlockSpec((tk, tn), lambda i, j, k: (k, j))],
        out_specs=[blk, blk], out_shape=[SDS((m, n), F32), SDS((m, n), BF)], scratch_shapes=[], args=(a, b),
        hosted=hosted)
    return g, gb, extra


GRAD_X_TILE = 512


def _grad_x(dr1, dp, wi, first, count, name, into=None, hosted=None):
    s, d = dr1.shape
    n = wi.shape[1]
    tm, tk = min(s, GRAD_X_TILE), _tile(n, 3584)

    def body(dr1_ref, dp_ref, w_ref, *rest):
        o_ref = rest[-1]

        @pl.when(pl.program_id(1) == 0)
        def _():
            o_ref[...] = F32(ALPHA) * dr1_ref[...]

        o_ref[...] += _dot_nt(dp_ref[...], w_ref[...])

    (gx,), extra = _host_call(
        body, name=name, grid=(count, n // tk),
        in_specs=[pl.BlockSpec((tm, d), lambda i, k: (i + first, 0)), pl.BlockSpec((tm, tk), lambda i, k: (i + first, k)),
                  pl.BlockSpec((d, tk), lambda i, k: (0, k))] + ([] if into is None else [ANY]),
        out_specs=[pl.BlockSpec((tm, d), lambda i, k: (i + first, 0))],
        out_shape=[SDS((s, d), F32)], scratch_shapes=[],
        args=(dr1, dp, wi) + (() if into is None else (into,)), aliases=None if into is None else {3: 0},
        hosted=hosted)
    return gx, extra


def _adamw_math(w, g, m, v):
    m2 = ADAM_B1 * m + (1.0 - ADAM_B1) * g
    v2 = ADAM_B2 * v + (1.0 - ADAM_B2) * (g * g)
    m_hat = m2 / (1.0 - ADAM_B1 ** ADAM_STEP)
    v_hat = v2 / (1.0 - ADAM_B2 ** ADAM_STEP)
    delta = -ADAM_LR * (m_hat / (jnp.sqrt(v_hat) + ADAM_EPS) + ADAM_WD * w)
    return delta, m2, v2


def _adamw(w, g, m, v, name):
    r, c = w.shape
    tr = r if r <= 256 else 256

    def body(w_ref, g_ref, m_ref, v_ref, d_ref, m2_ref, v2_ref):
        delta, m2, v2 = _adamw_math(w_ref[...], g_ref[...], m_ref[...], v_ref[...])
        d_ref[...] = delta
        m2_ref[...] = m2
        v2_ref[...] = v2

    blk = pl.BlockSpec((tr, c), lambda i: (i, 0))
    return pl.pallas_call(
        body, name=name, grid=(r // tr,), in_specs=[blk] * 4, out_specs=[blk] * 3,
        out_shape=[SDS((r, c), F32)] * 3, compiler_params=_cp(("parallel",)),
    )(w, g, m, v)


def _sum_parts(parts, name):
    k, r, c = parts.shape

    def body(p_ref, o_ref):
        acc = p_ref[0]
        for q in range(1, k):
            acc = acc + p_ref[q]
        o_ref[...] = acc

    return pl.pallas_call(
        body, name=name, grid=(1,),
        in_specs=[pl.BlockSpec((k, r, c), lambda i: (0, 0, 0))],
        out_specs=pl.BlockSpec((r, c), lambda i: (0, 0)),
        out_shape=SDS((r, c), F32), compiler_params=_cp(("arbitrary",)),
    )(parts)


def _piece_shape(full_shape, axis):
    r, c = full_shape
    return (r // 2, c // 4) if axis == 1 else (r // 8, c)


def _piece_spec(full_shape, axis, tr, chip_of, half_of):
    hr, wc = _piece_shape(full_shape, axis)
    nb = hr // tr
    if axis == 1:
        return pl.BlockSpec((tr, wc), lambda *a: (half_of(*a) * nb + a[-2], chip_of(*a)))
    return pl.BlockSpec((tr, wc), lambda *a: ((2 * chip_of(*a) + half_of(*a)) * nb + a[-2], 0))


def _place_cast(w, axis, pos, name):
    r, c = w.shape
    tr = min(r, 256)
    nb = r // tr
    full = (r, 4 * c) if axis == 1 else (4 * r, c)
    out_map = (lambda i, pos: (i, pos[0])) if axis == 1 else (lambda i, pos: (pos[0] * nb + i, 0))

    def body(pos_ref, w_ref, o_ref):
        o_ref[...] = w_ref[...].astype(o_ref.dtype)

    gs = pltpu.PrefetchScalarGridSpec(
        num_scalar_prefetch=1, grid=(nb,),
        in_specs=[pl.BlockSpec((tr, c), lambda i, pos: (i, 0))], out_specs=pl.BlockSpec((tr, c), out_map))
    return pl.pallas_call(body, name=name, grid_spec=gs, out_shape=SDS(full, BF),
                          compiler_params=_cp(("arbitrary",)))(pos, w)


def _pair_add(g, land, axis, pos, name):
    hr, wc = _piece_shape(g.shape, axis)
    tr = min(hr, 256)

    def body(pos_ref, g_ref, l_ref, o_ref):
        o_ref[0] = (g_ref[...] + l_ref[0].astype(F32)).astype(BF)

    blk = pl.BlockSpec((1, tr, wc), lambda j, i, pos: (j, i, 0))
    gs = pltpu.PrefetchScalarGridSpec(
        num_scalar_prefetch=1, grid=(4, hr // tr),
        in_specs=[_piece_spec(g.shape, axis, tr, lambda j, i, pos: j, lambda j, i, pos: pos[1]), blk], out_specs=blk)
    return pl.pallas_call(body, name=name, grid_spec=gs, out_shape=SDS((4, hr, wc), BF),
                          compiler_params=_cp(("arbitrary", "arbitrary")))(pos, g, land)


def _chip_sum(g, land1, land2, axis, pos, name):
    hr, wc = _piece_shape(g.shape, axis)
    tr = min(hr, 256)
    nb = hr // tr

    def body(pos_ref, g_ref, l1_ref, l2_ref, o_ref):
        acc = g_ref[...] + l1_ref[0].astype(F32)
        for q in range(3):
            acc = acc + l2_ref[q].astype(F32)
        o_ref[...] = acc

    gs = pltpu.PrefetchScalarGridSpec(
        num_scalar_prefetch=1, grid=(nb,),
        in_specs=[_piece_spec(g.shape, axis, tr, lambda i, pos: pos[0], lambda i, pos: pos[1]),
                  pl.BlockSpec((1, tr, wc), lambda i, pos: (pos[0], i, 0)),
                  pl.BlockSpec((3, tr, wc), lambda i, pos: (0, i, 0))],
        out_specs=pl.BlockSpec((tr, wc), lambda i, pos: (pos[1] * nb + i, 0)))
    return pl.pallas_call(body, name=name, grid_spec=gs, out_shape=SDS((2 * hr, wc), F32),
                          compiler_params=_cp(("arbitrary",)))(pos, g, land1, land2)


ANY = pl.BlockSpec(memory_space=pl.ANY)
COMM = pltpu.CompilerParams(has_side_effects=True)


def _on_each_device(fn):
    x, y, c = lax.axis_index("x"), lax.axis_index("y"), lax.axis_index("c")
    for sx in (0, 1):
        for sy in (0, 1):
            for sc in (0, 1):
                @pl.when(jnp.logical_and(jnp.logical_and(x == sx, y == sy), c == sc))
                def _(sx=sx, sy=sy, sc=sc):
                    fn(sx, sy, sc)


def _remote(src, dst, send_sem, recv_sem, to):
    return pltpu.make_async_remote_copy(src_ref=src, dst_ref=dst, send_sem=send_sem, recv_sem=recv_sem,
                                        device_id=to, device_id_type=MESH)


def _piece_ref(ref, axis, j, h):
    r, c = ref.shape
    hr, wc = _piece_shape((r, c), axis)
    if axis == 1:
        return ref.at[pl.ds(h * hr, hr), pl.ds(j * wc, wc)]
    return ref.at[pl.ds((2 * j + h) * hr, hr), :]


class _Stage:
    def __init__(self, ins, outs, n_sems, make_plan, aliases=None):
        self.ins, self.outs, self.n_sems, self.make_plan = list(ins), list(outs), n_sems, make_plan
        self.aliases = dict(aliases or {})


def _start(plan):
    def dev(sx, sy, sc):
        for cp, _, _ in plan(sx, sy, sc):
            cp.start()

    _on_each_device(dev)


def _finish(plan):
    def dev(sx, sy, sc):
        for _, sent, got in plan(sx, sy, sc):
            sent.wait_send()
            got.wait_recv()

    _on_each_device(dev)


def _comm_call(stage, name):
    n_in, n_out = len(stage.ins), len(stage.outs)

    def body(*refs):
        plan = stage.make_plan(refs[:n_in], refs[n_in:n_in + n_out], *refs[n_in + n_out:])
        _start(plan)
        _finish(plan)

    return pl.pallas_call(
        body, name=name, in_specs=[ANY] * n_in, out_specs=[ANY] * n_out, out_shape=stage.outs,
        input_output_aliases=stage.aliases,
        scratch_shapes=[pltpu.SemaphoreType.DMA((stage.n_sems,)), pltpu.SemaphoreType.DMA((stage.n_sems,))],
        compiler_params=COMM,
    )(*stage.ins)


class _SemsFrom:
    def __init__(self, sems, base):
        self.sems, self.base = sems, base

    @property
    def at(self):
        return self

    def __getitem__(self, k):
        return self.sems.at[self.base + k]


def _both(a, b):
    na, nb = len(a.ins), len(b.ins)
    ma = len(a.outs)

    def make_plan(ins, outs, send_sems, recv_sems):
        pa = a.make_plan(ins[:na], outs[:ma], send_sems, recv_sems)
        pb = b.make_plan(ins[na:], outs[ma:], _SemsFrom(send_sems, a.n_sems), _SemsFrom(recv_sems, a.n_sems))
        return lambda sx, sy, sc: pa(sx, sy, sc) + pb(sx, sy, sc)

    aliases = dict(a.aliases)
    aliases.update({na + i: ma + o for i, o in b.aliases.items()})
    return _Stage(a.ins + b.ins, a.outs + b.outs, a.n_sems + b.n_sems, make_plan, aliases)


def _same(cp):
    return (cp, cp, cp)


def _stage_gather_send(fulls, axes):
    n = len(fulls)

    def make_plan(ins, outs, send_sems, recv_sems):
        def plan(sx, sy, sc):
            cps = []
            for w in range(n):
                mine = _piece_ref(outs[w], axes[w], 2 * sx + sy, sc)
                for r, (fx, fy) in enumerate(CHIP_RELS):
                    k = 3 * w + r
                    to = (sx ^ fx, sy ^ fy, sc)
                    got = _piece_ref(outs[w], axes[w], 2 * (sx ^ fx) + (sy ^ fy), sc)
                    send = _remote(mine, mine, send_sems.at[k], recv_sems.at[k], to)
                    cps.append((send, send, _remote(got, got, send_sems.at[k], recv_sems.at[k], to)))
            return cps

        return plan

    return _Stage(fulls, [SDS(f.shape, f.dtype) for f in fulls], 3 * n, make_plan, {i: i for i in range(n)})


def _stage_gather_forward(fulls, axes):
    n = len(fulls)

    def make_plan(ins, outs, send_sems, recv_sems):
        def plan(sx, sy, sc):
            cps = []
            sib = (sx, sy, 1 - sc)
            for w in range(n):
                for r, (fx, fy) in enumerate(CHIP_RELS):
                    k = 3 * w + r
                    pj = 2 * (sx ^ fx) + (sy ^ fy)
                    have = _piece_ref(outs[w], axes[w], pj, sc)
                    want = _piece_ref(outs[w], axes[w], pj, 1 - sc)
                    send = _remote(have, have, send_sems.at[k], recv_sems.at[k], sib)
                    cps.append((send, send, _remote(want, want, send_sems.at[k], recv_sems.at[k], sib)))
            return cps

        return plan

    return _Stage(fulls, [SDS(f.shape, f.dtype) for f in fulls], 3 * n, make_plan, {i: i for i in range(n)})


def _in_place(fulls, n_sems, make_plan):
    return _Stage(fulls, [SDS(f.shape, f.dtype) for f in fulls], n_sems, make_plan, {i: i for i in range(len(fulls))})


def _stage_gather_neighbours(fulls, axes):
    n = len(fulls)

    def make_plan(ins, outs, send_sems, recv_sems):
        def plan(sx, sy, sc):
            cps = []
            for w in range(n):
                mine = _piece_ref(outs[w], axes[w], 2 * sx + sy, sc)
                for r, (px, py) in enumerate(((sx ^ 1, sy), (sx, sy ^ 1))):
                    got = _piece_ref(outs[w], axes[w], 2 * px + py, sc)
                    send = _remote(mine, mine, send_sems.at[2 * w + r], recv_sems.at[2 * w + r], (px, py, sc))
                    cps.append((send, send, _remote(got, got, send_sems.at[2 * w + r], recv_sems.at[2 * w + r],
                                                    (px, py, sc))))
            return cps

        return plan

    return _in_place(fulls, 2 * n, make_plan)


def _stage_gather_relay(fulls, axes):
    n = len(fulls)

    def make_plan(ins, outs, send_sems, recv_sems):
        def plan(sx, sy, sc):
            cps = []
            sib = (sx, sy, 1 - sc)
            jx, jy, jd = 2 * (sx ^ 1) + sy, 2 * sx + (sy ^ 1), 2 * (sx ^ 1) + (sy ^ 1)
            for w in range(n):
                piece = lambda j, h, w=w: _piece_ref(outs[w], axes[w], j, h)
                relayed, to = (jx, (sx, sy ^ 1, sc)) if sc == 0 else (jy, (sx ^ 1, sy, sc))
                k = 3 * w
                send = _remote(piece(relayed, sc), piece(relayed, sc), send_sems.at[k], recv_sems.at[k], to)
                cps.append((send, send, _remote(piece(jd, sc), piece(jd, sc), send_sems.at[k], recv_sems.at[k], to)))
                for r, j in enumerate((jx, jy)):
                    k = 3 * w + 1 + r
                    send = _remote(piece(j, sc), piece(j, sc), send_sems.at[k], recv_sems.at[k], sib)
                    cps.append((send, send, _remote(piece(j, 1 - sc), piece(j, 1 - sc), send_sems.at[k],
                                                    recv_sems.at[k], sib)))
            return cps

        return plan

    return _in_place(fulls, 3 * n, make_plan)


def _stage_gather_last(fulls, axes):
    n = len(fulls)

    def make_plan(ins, outs, send_sems, recv_sems):
        def plan(sx, sy, sc):
            cps = []
            sib = (sx, sy, 1 - sc)
            jd = 2 * (sx ^ 1) + (sy ^ 1)
            for w in range(n):
                have, want = _piece_ref(outs[w], axes[w], jd, sc), _piece_ref(outs[w], axes[w], jd, 1 - sc)
                send = _remote(have, have, send_sems.at[w], recv_sems.at[w], sib)
                cps.append((send, send, _remote(want, want, send_sems.at[w], recv_sems.at[w], sib)))
            return cps

        return plan

    return _in_place(fulls, n, make_plan)


def _stage_pair_exchange(grads, axes):
    n = len(grads)

    def make_plan(gs, land, send_sems, recv_sems):
        def plan(sx, sy, sc):
            return [_same(_remote(_piece_ref(gs[w], axes[w], jj, 1 - sc), land[w].at[jj], send_sems.at[4 * w + jj],
                                  recv_sems.at[4 * w + jj], (sx, sy, 1 - sc)))
                    for w in range(n) for jj in range(4)]

        return plan

    return _Stage(grads, [SDS((4,) + _piece_shape(g.shape, a), g.dtype) for g, a in zip(grads, axes)], 4 * n,
                  make_plan)


def _stage_chip_scatter(pieces):
    n = len(pieces)

    def make_plan(ps, land, send_sems, recv_sems):
        def plan(sx, sy, sc):
            return [_same(_remote(ps[w].at[2 * (sx ^ fx) + (sy ^ fy)], land[w].at[r], send_sems.at[3 * w + r],
                                  recv_sems.at[3 * w + r], (sx ^ fx, sy ^ fy, sc)))
                    for w in range(n) for r, (fx, fy) in enumerate(CHIP_RELS)]

        return plan

    return _Stage(pieces, [SDS((3,) + p.shape[1:], p.dtype) for p in pieces], 3 * n, make_plan)


def _stage_pair_share(shards):
    n = len(shards)

    def make_plan(ins, outs, send_sems, recv_sems):
        def plan(sx, sy, sc):
            cps = []
            sib = (sx, sy, 1 - sc)
            for w in range(n):
                hr = shards[w].shape[0] // 2
                mine = outs[w].at[pl.ds(sc * hr, hr), :]
                theirs = outs[w].at[pl.ds((1 - sc) * hr, hr), :]
                send = _remote(mine, mine, send_sems.at[w], recv_sems.at[w], sib)
                cps.append((send, send, _remote(theirs, theirs, send_sems.at[w], recv_sems.at[w], sib)))
            return cps

        return plan

    return _Stage(shards, [SDS(g.shape, g.dtype) for g in shards], n, make_plan, {i: i for i in range(n)})


def _stage_gather_small(stack):
    def make_plan(ins, outs, send_sems, recv_sems):
        def plan(sx, sy, sc):
            mine = outs[0].at[4 * sx + 2 * sy + sc]
            return [_same(_remote(mine, mine, send_sems.at[k], recv_sems.at[k], (sx ^ fx, sy ^ fy, sc ^ fc)))
                    for k, (fx, fy, fc) in enumerate(DEV_RELS)]

        return plan

    return _Stage([stack], [SDS(stack.shape, stack.dtype)], 7, make_plan, {0: 0})


def kernel(x, w_in, conv_a_w, w_out_a, conv_b_w, conv_b_bias, ln_b_gamma, ln_b_beta, w_out_b, w_o, ln1_gamma, ln1_beta, w_up, w_down, ln2_gamma, ln2_beta, loss_target, m_w_in, m_conv_a_w, m_w_out_a, m_conv_b_w, m_conv_b_bias, m_ln_b_gamma, m_ln_b_beta, m_w_out_b, m_w_o, m_ln1_gamma, m_ln1_beta, m_w_up, m_w_down, m_ln2_gamma, m_ln2_beta, v_w_in, v_conv_a_w, v_w_out_a, v_conv_b_w, v_conv_b_bias, v_ln_b_gamma, v_ln_b_beta, v_w_out_b, v_w_o, v_ln1_gamma, v_ln1_beta, v_w_up, v_w_down, v_ln2_gamma, v_ln2_beta):
    s, d = x.shape[1], x.shape[2]
    xs = x.reshape(s, d)
    tgt = loss_target.reshape(s, d)
    dq = d // 4
    chip = 2 * lax.axis_index("x") + lax.axis_index("y")
    core = lax.axis_index("c")
    pos = jnp.stack([chip, core]).astype(jnp.int32)
    names = ("w_in", "w_out_a", "w_out_b", "w_o", "w_up", "w_down")
    axes = (1, 0, 0, 0, 1, 0)

    conv_pack = jnp.concatenate([jnp.pad(conv_a_w, ((0, 8 - K_A), (0, 0))), jnp.pad(conv_b_w, ((0, 32 - K_B), (0, 0))),
                                 jnp.zeros((8, dq), F32)], axis=0)
    conv_full = lax.dynamic_update_slice(jnp.zeros((conv_pack.shape[0], d), F32), conv_pack, (0, chip * dq))
    fulls = [_place_cast(w, a, pos, "place_" + nm)
             for w, a, nm in zip((w_in, w_out_a, w_out_b, w_o, w_up, w_down), axes, names)]
    vec = lambda a: a.reshape(1, d)
    bias_b, lbg, lbb = vec(conv_b_bias), vec(ln_b_gamma), vec(ln_b_beta)
    l1g, l1b, l2g, l2b = vec(ln1_gamma), vec(ln1_beta), vec(ln2_gamma), vec(ln2_beta)

    (p, xb), landed = _in_proj_own(xs, w_in, pos, _stage_gather_neighbours([fulls[0], conv_full], (1, 1)))
    landed = _comm_call(_stage_gather_relay(landed, (1, 1)), "gather_relay_w_in")
    wi, convs = _comm_call(_stage_gather_last(landed, (1, 1)), "gather_last_w_in")
    cwa, cwb = convs[0:K_A], convs[8:8 + K_B]
    p, small3 = _in_proj_rest(xb, wi, p, pos, _stage_gather_send(fulls[1:4], axes[1:4]))
    (conva, yapre, nb, rstdb, u3), landed = _conv_fwd(
        p, cwa, cwb, bias_b, lbg, lbb, d,
        _both(_stage_gather_forward(small3, axes[1:4]), _stage_gather_send(fulls[4:6], axes[4:6])))
    woa, wob, wo = landed[:3]
    (ya, yb, merged, n1, rstd1), (wup, wdown) = _mixer_out(yapre, u3, p, xs, woa, wob, wo, d,
                                                           _stage_gather_forward(landed[3:], axes[4:6]))
    x1b, hb, dhpre, dr2b, dr1, dr1b, acc_mlp = _mlp(n1, rstd1, tgt, wup, wdown, l1g, l1b, l2g, l2b)
    g_up, gb_up, _ = _grad_w(x1b, dhpre, "grad_w_up")
    g_down, gb_down, _ = _grad_w(hb, dr2b, "grad_w_down")
    dya, dyb, dba, dga, dgb, dca, du1, acc_mix = _mixer_bwd_local(dr1b, p, ya, yb, conva, nb, rstdb, wo, woa, wob,
                                                                   lbg, lbb)
    g_oa, gb_oa, _ = _grad_w(yapre, dya, "grad_w_out_a")
    g_ob, gb_ob, _ = _grad_w(u3, dyb, "grad_w_out_b")
    g_o, gb_o, _ = _grad_w(merged, dr1b, "grad_w_o")

    early, e_axes, e_names = [g_oa, g_ob, g_o, g_up, g_down], axes[1:], names[1:]
    (dp, dcwa, dcwb), land1 = _conv_bwd(dca, du1, p, dba, dga, dgb, cwa, cwb, d,
                                        _stage_pair_exchange([gb_oa, gb_ob, gb_o, gb_up, gb_down], e_axes))
    pieces = [_pair_add(g, l, a, pos, "pair_add_" + nm) for g, l, a, nm in zip(early, land1, e_axes, e_names)]
    pack = jnp.concatenate([dcwa, dcwb, acc_mix, acc_mlp], axis=0)
    stack = lax.dynamic_update_slice(jnp.zeros((8,) + pack.shape, F32), pack[None], (2 * chip + core, 0, 0))
    g_wi, gb_wi, landed = _grad_w(xb, dp, "grad_w_in",
                                  _both(_stage_chip_scatter(pieces), _stage_gather_small(stack)))
    land2, stack = landed[:-1], landed[-1]
    halves = [_chip_sum(g, l1, l2, a, pos, "chip_sum_" + nm)
              for g, l1, l2, a, nm in zip(early, land1, land2, e_axes, e_names)]
    (land1_in,) = _comm_call(_stage_pair_exchange([gb_wi], (1,)), "pair_exchange_w_in")
    piece_in = _pair_add(g_wi, land1_in, 1, pos, "pair_add_w_in")
    n_tiles = s // min(s, GRAD_X_TILE)
    n_first = max(1, (5 * n_tiles) // 8)
    grad_x, landed = _grad_x(dr1, dp, wi, 0, n_first, "grad_x_a",
                             hosted=_both(_stage_chip_scatter([piece_in]), _stage_pair_share(halves)))
    land2_in, (g_oa, g_ob, g_o, g_up, g_down) = landed[0], landed[1:]
    if n_first < n_tiles:
        grad_x, _ = _grad_x(dr1, dp, wi, n_first, n_tiles - n_first, "grad_x_b", into=grad_x)
    half_in = _chip_sum(g_wi, land1_in, land2_in, 1, pos, "chip_sum_w_in")
    (g_in,) = _comm_call(_stage_pair_share([half_in]), "pair_share_w_in")
    small = _sum_parts(stack, "small_sum")
    g_ca = lax.dynamic_slice(small, (0, chip * dq), (K_A, dq))
    g_cb = lax.dynamic_slice(small, (8, chip * dq), (K_B, dq))
    g_vec = jnp.stack([small[r] for r in (42, 40, 41, 51, 52, 49, 50)])

    loss = lax.psum((0.5 / d) * jnp.sum(acc_mlp[0]), ("x", "y", "c"))

    big = {}
    for name, w, g, m, v in (("w_in", w_in, g_in, m_w_in, v_w_in), ("w_out_a", w_out_a, g_oa, m_w_out_a, v_w_out_a),
                             ("w_out_b", w_out_b, g_ob, m_w_out_b, v_w_out_b), ("w_o", w_o, g_o, m_w_o, v_w_o),
                             ("w_up", w_up, g_up, m_w_up, v_w_up), ("w_down", w_down, g_down, m_w_down, v_w_down),
                             ("conv_a_w", conv_a_w, g_ca, m_conv_a_w, v_conv_a_w),
                             ("conv_b_w", conv_b_w, g_cb, m_conv_b_w, v_conv_b_w)):
        big[name] = (g,) + tuple(_adamw(w, g, m, v, "adamw_" + name))
    vec_names = ("conv_b_bias", "ln_b_gamma", "ln_b_beta", "ln1_gamma", "ln1_beta", "ln2_gamma", "ln2_beta")
    w7 = jnp.stack([conv_b_bias, ln_b_gamma, ln_b_beta, ln1_gamma, ln1_beta, ln2_gamma, ln2_beta])
    m7 = jnp.stack([m_conv_b_bias, m_ln_b_gamma, m_ln_b_beta, m_ln1_gamma, m_ln1_beta, m_ln2_gamma, m_ln2_beta])
    v7 = jnp.stack([v_conv_b_bias, v_ln_b_gamma, v_ln_b_beta, v_ln1_gamma, v_ln1_beta, v_ln2_gamma, v_ln2_beta])
    d7, nm7, nv7 = _adamw(w7, g_vec, m7, v7, "adamw_vectors")
    for q, name in enumerate(vec_names):
        big[name] = (g_vec[q], d7[q], nm7[q], nv7[q])

    order = ("w_in", "conv_a_w", "w_out_a", "conv_b_w", "conv_b_bias", "ln_b_gamma", "ln_b_beta", "w_out_b", "w_o",
             "ln1_gamma", "ln1_beta", "w_up", "w_down", "ln2_gamma", "ln2_beta")
    outs = [loss, grad_x.reshape(x.shape)]
    for part in range(4):
        outs += [big[name][part] for name in order]
    return tuple(outs)
```

```python
import jax
import jax.numpy as jnp
from jax import lax
from jax.experimental import pallas as pl
from jax.experimental.pallas import tpu as pltpu

F32 = jnp.float32
BF = jnp.bfloat16
SDS = jax.ShapeDtypeStruct
MESH = pl.DeviceIdType.MESH

ALPHA = 2.0 ** 0.25
LN_EPS = 1e-5
K_A = 3
K_B = 31
HALO = 32
CONV_TILE = 512
CONV_ROWS = 64
CONV_LANES = 128
ADAM_LR = 0.001
ADAM_B1 = 0.9
ADAM_B2 = 0.999
ADAM_EPS = 1e-08
ADAM_WD = 0.01
ADAM_STEP = 10
P_DT = BF
CHIP_RELS = ((1, 0), (0, 1), (1, 1))
DEV_RELS = tuple((fx, fy, fc) for fx in (0, 1) for fy in (0, 1) for fc in (0, 1))[1:]


def _cp(sem=None, vmem_mb=56, side_effects=False):
    return pltpu.CompilerParams(dimension_semantics=sem, vmem_limit_bytes=vmem_mb << 20,
                                has_side_effects=side_effects)


def _const(shape):
    return pl.BlockSpec(shape, lambda *_: (0,) * len(shape), pipeline_mode=pl.Buffered(1))


def _sig(v):
    return jax.nn.sigmoid(v)


def _ln_fwd(r):
    mu = jnp.mean(r, axis=-1, keepdims=True)
    xc = r - mu
    var = jnp.mean(xc * xc, axis=-1, keepdims=True)
    rstd = lax.rsqrt(var + LN_EPS)
    return xc * rstd, rstd


def _ln_bwd(dn, n, rstd):
    m1 = jnp.mean(dn, axis=-1, keepdims=True)
    m2 = jnp.mean(dn * n, axis=-1, keepdims=True)
    return rstd * (dn - m1 - n * m2)


def _dot(a, b):
    return jnp.dot(a, b, preferred_element_type=F32)


def _dot_nt(a, b):
    return lax.dot_general(a, b, (((1,), (1,)), ((), ())), preferred_element_type=F32)


def _dot_tn(a, b):
    return lax.dot_general(a, b, (((0,), (0,)), ((), ())), preferred_element_type=F32)


def _tile(n, pref):
    if n <= pref:
        return n
    return max(t for t in range(128, pref + 1, 128) if n % t == 0)


def _rowsum8(v):
    acc = v[0:8]
    for g in range(1, v.shape[0] // 8):
        acc = acc + v[8 * g:8 * g + 8]
    return acc


def _taps(win, offsets, rows):
    r_all = win.shape[0]
    by_res = {}
    for k, o in enumerate(offsets):
        by_res.setdefault(o % 8, []).append((k, o // 8))
    for s, taps in sorted(by_res.items()):
        r = win if s == 0 else pltpu.roll(win, r_all - s, 0)
        for k, q in taps:
            yield k, r[8 * q:8 * q + rows]


CAUSAL_A = [HALO - (K_A - 1) + k for k in range(K_A)]
CAUSAL_B = [HALO - (K_B - 1) + k for k in range(K_B)]
ANTI_A = [K_A - 1 - k for k in range(K_A)]
ANTI_B = [K_B - 1 - k for k in range(K_B)]


def _host_call(body, *, name, grid, in_specs, out_specs, out_shape, scratch_shapes, args, hosted, prefetch=None,
               aliases=None):
    n_in, n_out, n_scr = len(in_specs), len(out_specs), len(scratch_shapes)
    n_pre = 0 if prefetch is None else 1
    pre = () if prefetch is None else (prefetch,)
    sem = ("arbitrary",) * len(grid)
    own_aliases = {n_pre + a: b for a, b in (aliases or {}).items()}

    def call(kernel_body, ins, outs, shapes, scratch, all_aliases, side_effects, operands):
        gs = pltpu.PrefetchScalarGridSpec(num_scalar_prefetch=n_pre, grid=grid, in_specs=ins, out_specs=outs,
                                          scratch_shapes=scratch)
        return pl.pallas_call(kernel_body, name=name, grid_spec=gs, out_shape=shapes,
                              input_output_aliases=all_aliases,
                              compiler_params=_cp(sem, side_effects=side_effects))(*pre, *operands)

    if hosted is None:
        res = call(body, list(in_specs), list(out_specs), list(out_shape), list(scratch_shapes), own_aliases, False,
                   args)
        return list(res), []
    h_in, h_out = len(hosted.ins), len(hosted.outs)

    def full_body(*refs):
        pre_refs, refs = refs[:n_pre], refs[n_pre:]
        ins, refs = refs[:n_in], refs[n_in:]
        hins, refs = refs[:h_in], refs[h_in:]
        outs, refs = refs[:n_out], refs[n_out:]
        houts, refs = refs[:h_out], refs[h_out:]
        scr, (send_sems, recv_sems) = refs[:n_scr], refs[n_scr:]
        plan = hosted.make_plan(hins, houts, send_sems, recv_sems)
        first = last = None
        for a, size in enumerate(grid):
            at0, at1 = pl.program_id(a) == 0, pl.program_id(a) == size - 1
            first = at0 if first is None else jnp.logical_and(first, at0)
            last = at1 if last is None else jnp.logical_and(last, at1)

        @pl.when(first)
        def _():
            _start(plan)

        body(*pre_refs, *ins, *outs, *scr)

        @pl.when(last)
        def _():
            _finish(plan)

    all_aliases = dict(own_aliases)
    all_aliases.update({n_pre + n_in + a: n_out + b for a, b in hosted.aliases.items()})
    res = call(full_body, list(in_specs) + [ANY] * h_in, list(out_specs) + [ANY] * h_out,
               list(out_shape) + list(hosted.outs),
               list(scratch_shapes) + [pltpu.SemaphoreType.DMA((hosted.n_sems,)),
                                       pltpu.SemaphoreType.DMA((hosted.n_sems,))],
               all_aliases, True, (*args, *hosted.ins))
    return list(res[:n_out]), list(res[n_out:])


def _in_proj_own(x, w_shard, pos, hosted):
    s, d = x.shape
    tn = w_shard.shape[1]
    tm = min(s, 1024)

    def body(pos_ref, x_ref, w_ref, p_ref, xb_ref, wb):
        @pl.when(pl.program_id(0) == 0)
        def _():
            wb[...] = w_ref[...].astype(BF)

        xb = x_ref[...].astype(BF)
        xb_ref[...] = xb
        p_ref[...] = _dot(xb, wb[...]).astype(p_ref.dtype)

    return _host_call(
        body, name="in_proj_own", grid=(s // tm,), prefetch=pos,
        in_specs=[pl.BlockSpec((tm, d), lambda i, pos: (i, 0)),
                  pl.BlockSpec((d, tn), lambda i, pos: (0, 0), pipeline_mode=pl.Buffered(1))],
        out_specs=[pl.BlockSpec((tm, tn), lambda i, pos: (i, pos[0])), pl.BlockSpec((tm, d), lambda i, pos: (i, 0))],
        out_shape=[SDS((s, 4 * tn), P_DT), SDS((s, d), BF)], scratch_shapes=[pltpu.VMEM((d, tn), BF)],
        args=(x, w_shard), hosted=hosted)


def _in_proj_rest(xb, wi, p, pos, hosted):
    s, d = xb.shape
    n = wi.shape[1]
    tm, tn = min(s, 1024), n // 4

    def body(pos_ref, xb_ref, w_ref, p_in, p_ref):
        p_ref[...] = _dot(xb_ref[...], w_ref[...]).astype(p_ref.dtype)

    col = lambda i, j, pos: (pos[0] + 1 + j) % 4
    (p,), extra = _host_call(
        body, name="in_proj_rest", grid=(s // tm, 3), prefetch=pos, aliases={2: 0},
        in_specs=[pl.BlockSpec((tm, d), lambda i, j, pos: (i, 0)),
                  pl.BlockSpec((d, tn), lambda i, j, pos: (0, col(i, j, pos))), ANY],
        out_specs=[pl.BlockSpec((tm, tn), lambda i, j, pos: (i, col(i, j, pos)))],
        out_shape=[SDS((s, n), P_DT)], scratch_shapes=[], args=(xb, wi, p), hosted=hosted)
    return p, extra


def _col_spec(tm, d, k):
    return pl.BlockSpec((tm, d), lambda i, k=k: (i, k))


def _prev_halo_spec(tm, d, k):
    r = tm // HALO
    return pl.BlockSpec((HALO, d), lambda i, k=k: (jnp.maximum(i * r - 1, 0), k))


def _next_halo_spec(tm, d, k, s):
    r = tm // HALO
    last = s // HALO - 1
    return pl.BlockSpec((HALO, d), lambda i, k=k: (jnp.minimum((i + 1) * r, last), k))


def _conv_fwd(p, cwa, cwb, bias_b, lbg, lbb, d, hosted=None):
    s = p.shape[0]
    tm = min(s, CONV_TILE)
    nt = s // tm

    def body(ba_ref, ca_ref, va_ref, vb_ref, gb_ref, hca_ref, hva_ref, hvb_ref, hgb_ref,
             cwa_ref, cwb_ref, bias_ref, lbg_ref, lbb_ref,
             conva_ref, yapre_ref, nb_ref, rstdb_ref, u3_ref,
             zbuf, ubuf, u1buf):
        i = pl.program_id(0)
        keep = (i > 0).astype(F32)
        zbuf[pl.ds(0, HALO), :] = hca_ref[...].astype(F32) * hva_ref[...].astype(F32) * keep
        ubuf[pl.ds(0, HALO), :] = hvb_ref[...].astype(F32) * _sig(hgb_ref[...].astype(F32)) * keep
        zbuf[pl.ds(HALO, tm), :] = ca_ref[...].astype(F32) * va_ref[...].astype(F32)
        ubuf[pl.ds(HALO, tm), :] = vb_ref[...].astype(F32) * _sig(gb_ref[...].astype(F32))

        def chunk(j, carry):
            r0 = pl.multiple_of(j * CONV_ROWS, CONV_ROWS)
            rows = pl.ds(r0, CONV_ROWS)
            for lc in range(d // CONV_LANES):
                ls = pl.ds(lc * CONV_LANES, CONV_LANES)
                acc = jnp.zeros((CONV_ROWS, CONV_LANES), F32)
                for k, sl in _taps(zbuf[pl.ds(r0, CONV_ROWS + HALO), ls], CAUSAL_A, CONV_ROWS):
                    acc = acc + cwa_ref[pl.ds(k, 1), ls] * sl
                conva_ref[rows, ls] = acc.astype(BF)
                yapre_ref[rows, ls] = (ba_ref[rows, ls].astype(F32) * acc).astype(BF)
                acc = jnp.zeros((CONV_ROWS, CONV_LANES), F32)
                for k, sl in _taps(ubuf[pl.ds(r0, CONV_ROWS + HALO), ls], CAUSAL_B, CONV_ROWS):
                    acc = acc + cwb_ref[pl.ds(k, 1), ls] * sl
                u1buf[rows, ls] = acc + bias_ref[:, ls]
            return carry

        lax.fori_loop(0, tm // CONV_ROWS, chunk, 0)
        nb, rstd = _ln_fwd(u1buf[...])
        nb_ref[...] = nb
        rstdb_ref[...] = rstd
        u2 = nb * lbg_ref[...] + lbb_ref[...]
        u3_ref[...] = (u2 * _sig(u2)).astype(BF)

    vec = _const((1, d))
    return _host_call(
        body, name="conv_fwd", grid=(nt,),
        in_specs=[_col_spec(tm, d, k) for k in range(5)] + [_prev_halo_spec(tm, d, k) for k in (1, 2, 3, 4)]
        + [_const((K_A, d)), _const((K_B, d)), vec, vec, vec],
        out_specs=[pl.BlockSpec((tm, d), lambda i: (i, 0)), pl.BlockSpec((tm, d), lambda i: (i, 0)),
                   pl.BlockSpec((tm, d), lambda i: (i, 0)), pl.BlockSpec((tm, 1), lambda i: (i, 0)),
                   pl.BlockSpec((tm, d), lambda i: (i, 0))],
        out_shape=[SDS((s, d), BF), SDS((s, d), BF), SDS((s, d), F32), SDS((s, 1), F32), SDS((s, d), BF)],
        scratch_shapes=[pltpu.VMEM((HALO + tm, d), F32), pltpu.VMEM((HALO + tm, d), F32), pltpu.VMEM((tm, d), F32)],
        args=(p, p, p, p, p, p, p, p, p, cwa, cwb, bias_b, lbg, lbb), hosted=hosted)


def _mixer_out(yapre, u3, p, x, woa, wob, wo, d, hosted=None):
    s = x.shape[0]
    tm = min(s, 512)

    def body(yapre_ref, u3_ref, ga_ref, gb_ref, x_ref, woa_ref, wob_ref, wo_ref,
             ya_ref, yb_ref, merged_ref, n1_ref, rstd1_ref):
        ya = _dot(yapre_ref[...], woa_ref[...])
        yb = _dot(u3_ref[...], wob_ref[...])
        ya_ref[...] = ya.astype(BF)
        yb_ref[...] = yb.astype(BF)
        merged = (_sig(ga_ref[...].astype(F32)) * ya + _sig(gb_ref[...].astype(F32)) * yb).astype(BF)
        merged_ref[...] = merged
        r1 = F32(ALPHA) * x_ref[...] + _dot(merged, wo_ref[...])
        n1, rstd1 = _ln_fwd(r1)
        n1_ref[...] = n1
        rstd1_ref[...] = rstd1

    row = pl.BlockSpec((tm, d), lambda i: (i, 0))
    return _host_call(
        body, name="mixer_out", grid=(s // tm,),
        in_specs=[row, row, _col_spec(tm, d, 5), _col_spec(tm, d, 6), row,
                  _const((d, d)), _const((d, d)), _const((d, d))],
        out_specs=[row, row, row, row, pl.BlockSpec((tm, 1), lambda i: (i, 0))],
        out_shape=[SDS((s, d), BF), SDS((s, d), BF), SDS((s, d), BF), SDS((s, d), F32), SDS((s, 1), F32)],
        scratch_shapes=[], args=(yapre, u3, p, p, x, woa, wob, wo), hosted=hosted)


def _mlp(n1, rstd1, tgt, wup, wdown, l1g, l1b, l2g, l2b):
    s, d = n1.shape
    dff = wup.shape[1]
    tm = min(s, 256)
    fc = min(dff, 1024)
    nq = dff // fc

    def body(n1_ref, rstd1_ref, tgt_ref, wup_ref, wdown_ref, l1g_ref, l1b_ref, l2g_ref, l2b_ref,
             x1b_ref, hb_ref, dhpre_ref, dr2b_ref, dr1_ref, dr1b_ref, acc_ref, rbuf):
        i = pl.program_id(0)
        n1v = n1_ref[...]
        x1 = n1v * l1g_ref[...] + l1b_ref[...]
        x1b = x1.astype(BF)
        x1b_ref[...] = x1b
        ff = jnp.zeros((tm, d), F32)
        for q in range(nq):
            cs = pl.ds(q * fc, fc)
            r = jnp.maximum(_dot(x1b, wup_ref[:, cs]), 0.0)
            rbuf[:, cs] = r
            hq = (r * r).astype(BF)
            hb_ref[:, cs] = hq
            ff = ff + _dot(hq, wdown_ref[cs, :])
        n2, rstd2 = _ln_fwd(F32(ALPHA) * x1 + ff)
        x2 = n2 * l2g_ref[...] + l2b_ref[...]
        err = x2 - tgt_ref[...]
        dx2 = err * F32(1.0 / d)
        dr2 = _ln_bwd(dx2 * l2g_ref[...], n2, rstd2)
        dr2b = dr2.astype(BF)
        dr2b_ref[...] = dr2b
        dx1 = F32(ALPHA) * dr2
        for q in range(nq):
            cs = pl.ds(q * fc, fc)
            dh = _dot_nt(dr2b, wdown_ref[cs, :])
            dhp = (dh * (2.0 * rbuf[:, cs])).astype(BF)
            dhpre_ref[:, cs] = dhp
            dx1 = dx1 + _dot_nt(dhp, wup_ref[:, cs])
        dr1 = _ln_bwd(dx1 * l1g_ref[...], n1v, rstd1_ref[...])
        dr1_ref[...] = dr1
        dr1b_ref[...] = dr1.astype(BF)

        @pl.when(i == 0)
        def _():
            acc_ref[...] = jnp.zeros_like(acc_ref)

        for q, val in enumerate((err * err, dx2 * n2, dx2, dx1 * n1v, dx1)):
            acc_ref[pl.ds(q, 1), :] += jnp.sum(val, axis=0, keepdims=True)

    row = pl.BlockSpec((tm, d), lambda i: (i, 0))
    wide = pl.BlockSpec((tm, dff), lambda i: (i, 0))
    vec = _const((1, d))
    return pl.pallas_call(
        body, name="mlp_fwd_bwd", grid=(s // tm,),
        in_specs=[row, pl.BlockSpec((tm, 1), lambda i: (i, 0)), row, _const((d, dff)), _const((dff, d)),
                  vec, vec, vec, vec],
        out_specs=[row, wide, wide, row, row, row, pl.BlockSpec((8, d), lambda i: (0, 0))],
        out_shape=[SDS((s, d), BF), SDS((s, dff), BF), SDS((s, dff), BF), SDS((s, d), BF), SDS((s, d), F32),
                   SDS((s, d), BF), SDS((8, d), F32)],
        scratch_shapes=[pltpu.VMEM((tm, dff), F32)],
        compiler_params=_cp(("arbitrary",)),
    )(n1, rstd1, tgt, wup, wdown, l1g, l1b, l2g, l2b)


def _mixer_bwd_local(dr1b, p, ya, yb, conva, nb, rstdb, wo, woa, wob, lbg, lbb):
    s, d = ya.shape
    tm = min(s, 256)

    def body(dr1b_ref, ba_ref, ga_ref, gb_ref, ya_ref, yb_ref, conva_ref, nb_ref, rstdb_ref,
             wo_ref, woa_ref, wob_ref, lbg_ref, lbb_ref,
             dya_ref, dyb_ref, dba_ref, dga_ref, dgb_ref, dca_ref, du1_ref, acc_ref):
        i = pl.program_id(0)
        dmerged = _dot_nt(dr1b_ref[...], wo_ref[...])
        sa = _sig(ga_ref[...].astype(F32))
        sb = _sig(gb_ref[...].astype(F32))
        dya = (dmerged * sa).astype(BF)
        dyb = (dmerged * sb).astype(BF)
        dya_ref[...] = dya
        dyb_ref[...] = dyb
        dga_ref[...] = (dmerged * ya_ref[...].astype(F32) * (sa * (1.0 - sa))).astype(BF)
        dgb_ref[...] = (dmerged * yb_ref[...].astype(F32) * (sb * (1.0 - sb))).astype(BF)
        dyapre = _dot_nt(dya, woa_ref[...])
        dba_ref[...] = (dyapre * conva_ref[...].astype(F32)).astype(BF)
        dca_ref[...] = dyapre * ba_ref[...].astype(F32)
        du3 = _dot_nt(dyb, wob_ref[...])
        nbv = nb_ref[...]
        u2 = nbv * lbg_ref[...] + lbb_ref[...]
        sg = _sig(u2)
        du2 = du3 * (sg * (1.0 + u2 * (1.0 - sg)))
        du1 = _ln_bwd(du2 * lbg_ref[...], nbv, rstdb_ref[...])
        du1_ref[...] = du1

        @pl.when(i == 0)
        def _():
            acc_ref[...] = jnp.zeros_like(acc_ref)

        for q, val in enumerate((du2 * nbv, du2, du1)):
            acc_ref[pl.ds(q, 1), :] += jnp.sum(val, axis=0, keepdims=True)

    row = pl.BlockSpec((tm, d), lambda i: (i, 0))
    vec = _const((1, d))
    return pl.pallas_call(
        body, name="mixer_bwd_local", grid=(s // tm,),
        in_specs=[row, _col_spec(tm, d, 0), _col_spec(tm, d, 5), _col_spec(tm, d, 6), row, row, row, row,
                  pl.BlockSpec((tm, 1), lambda i: (i, 0)), _const((d, d)), _const((d, d)), _const((d, d)), vec, vec],
        out_specs=[row, row, row, row, row, row, row, pl.BlockSpec((8, d), lambda i: (0, 0))],
        out_shape=[SDS((s, d), BF)] * 5 + [SDS((s, d), F32), SDS((s, d), F32), SDS((8, d), F32)],
        compiler_params=_cp(("arbitrary",)),
    )(dr1b, p, p, p, ya, yb, conva, nb, rstdb, wo, woa, wob, lbg, lbb)


def _conv_bwd(dca, du1, p, dba, dga, dgb, cwa, cwb, d, hosted=None):
    s = dca.shape[0]
    tm = min(s, CONV_TILE)
    nt = s // tm

    def body(dca_ref, du1_ref, ndca_ref, ndu1_ref, ca_ref, va_ref, vb_ref, gb_ref,
             dba_ref, dga_ref, dgb_ref, cwa_ref, cwb_ref,
             dp_ref, dcwa_ref, dcwb_ref,
             dcabuf, du1buf, sgbuf, acca, accb):
        i = pl.program_id(0)
        keep_next = (i < nt - 1).astype(F32)

        @pl.when(i == 0)
        def _():
            acca[...] = jnp.zeros_like(acca)
            accb[...] = jnp.zeros_like(accb)

        sgbuf[...] = _sig(gb_ref[...].astype(F32))
        dcabuf[pl.ds(0, tm), :] = dca_ref[...]
        dcabuf[pl.ds(tm, HALO), :] = ndca_ref[...] * keep_next
        du1buf[pl.ds(0, tm), :] = du1_ref[...]
        du1buf[pl.ds(tm, HALO), :] = ndu1_ref[...] * keep_next
        dp_ref[:, pl.ds(0, d)] = dba_ref[...]
        dp_ref[:, pl.ds(5 * d, d)] = dga_ref[...]
        dp_ref[:, pl.ds(6 * d, d)] = dgb_ref[...]

        def chunk(j, carry):
            r0 = pl.multiple_of(j * CONV_ROWS, CONV_ROWS)
            rows = pl.ds(r0, CONV_ROWS)
            for lc in range(d // CONV_LANES):
                lo = lc * CONV_LANES
                ls = pl.ds(lo, CONV_LANES)
                cac = ca_ref[rows, ls].astype(F32)
                vac = va_ref[rows, ls].astype(F32)
                zc = cac * vac
                acc = jnp.zeros((CONV_ROWS, CONV_LANES), F32)
                for k, sl in _taps(dcabuf[pl.ds(r0, CONV_ROWS + HALO), ls], ANTI_A, CONV_ROWS):
                    acc = acc + cwa_ref[pl.ds(k, 1), ls] * sl
                    acca[pl.ds(8 * k, 8), ls] += _rowsum8(sl * zc)
                dp_ref[rows, pl.ds(d + lo, CONV_LANES)] = (acc * vac).astype(BF)
                dp_ref[rows, pl.ds(2 * d + lo, CONV_LANES)] = (acc * cac).astype(BF)
                sgc = sgbuf[rows, ls]
                vbc = vb_ref[rows, ls].astype(F32)
                uc = vbc * sgc
                acc = jnp.zeros((CONV_ROWS, CONV_LANES), F32)
                for k, sl in _taps(du1buf[pl.ds(r0, CONV_ROWS + HALO), ls], ANTI_B, CONV_ROWS):
                    acc = acc + cwb_ref[pl.ds(k, 1), ls] * sl
                    accb[pl.ds(8 * k, 8), ls] += _rowsum8(sl * uc)
                dp_ref[rows, pl.ds(3 * d + lo, CONV_LANES)] = (acc * sgc).astype(BF)
                dp_ref[rows, pl.ds(4 * d + lo, CONV_LANES)] = (acc * vbc * (sgc * (1.0 - sgc))).astype(BF)
            return carry

        lax.fori_loop(0, tm // CONV_ROWS, chunk, 0)

        @pl.when(i == nt - 1)
        def _():
            dcwa_ref[...] = jnp.zeros_like(dcwa_ref)
            dcwb_ref[...] = jnp.zeros_like(dcwb_ref)
            for k in range(K_A):
                dcwa_ref[pl.ds(k, 1), :] = jnp.sum(acca[pl.ds(8 * k, 8), :], axis=0, keepdims=True)
            for k in range(K_B):
                dcwb_ref[pl.ds(k, 1), :] = jnp.sum(accb[pl.ds(8 * k, 8), :], axis=0, keepdims=True)

    row = pl.BlockSpec((tm, d), lambda i: (i, 0))
    nxt = _next_halo_spec(tm, d, 0, s)
    return _host_call(
        body, name="conv_bwd", grid=(nt,),
        in_specs=[row, row, nxt, nxt] + [_col_spec(tm, d, k) for k in (1, 2, 3, 4)]
        + [row, row, row, _const((K_A, d)), _const((K_B, d))],
        out_specs=[pl.BlockSpec((tm, 7 * d), lambda i: (i, 0)), pl.BlockSpec((8, d), lambda i: (0, 0)),
                   pl.BlockSpec((32, d), lambda i: (0, 0))],
        out_shape=[SDS((s, 7 * d), BF), SDS((8, d), F32), SDS((32, d), F32)],
        scratch_shapes=[pltpu.VMEM((tm + HALO, d), F32), pltpu.VMEM((tm + HALO, d), F32),
                        pltpu.VMEM((tm, d), F32), pltpu.VMEM((8 * K_A, d), F32), pltpu.VMEM((8 * K_B, d), F32)],
        args=(dca, du1, dca, du1, p, p, p, p, dba, dga, dgb, cwa, cwb), hosted=hosted)


def _grad_w(a, b, name, hosted=None):
    s, m = a.shape
    n = b.shape[1]
    tm, tn, tk = _tile(m, 1024), _tile(n, 1024), _tile(s, 2048)
    nk = s // tk

    def body(a_ref, b_ref, o_ref, ob_ref):
        k = pl.program_id(2)

        @pl.when(k == 0)
        def _():
            o_ref[...] = jnp.zeros_like(o_ref)

        o_ref[...] += _dot_tn(a_ref[...], b_ref[...])

        @pl.when(k == nk - 1)
        def _():
            ob_ref[...] = o_ref[...].astype(BF)

    blk = pl.BlockSpec((tm, tn), lambda i, j, k: (i, j))
    (g, gb), extra = _host_call(
        body, name=name, grid=(m // tm, n // tn, nk),
        in_specs=[pl.BlockSpec((tk, tm), lambda i, j, k: (k, i)), pl.BlockSpec((tk, tn), lambda i, j, k: (k, j))],
        out_specs=[blk, blk], out_shape=[SDS((m, n), F32), SDS((m, n), BF)], scratch_shapes=[], args=(a, b),
        hosted=hosted)
    return g, gb, extra


def _grad_x(dr1, dp, wi, hosted=None):
    s, d = dr1.shape
    n = wi.shape[1]
    tm, tk = min(s, 512), _tile(n, 3584)

    def body(dr1_ref, dp_ref, w_ref, o_ref):
        @pl.when(pl.program_id(1) == 0)
        def _():
            o_ref[...] = F32(ALPHA) * dr1_ref[...]

        o_ref[...] += _dot_nt(dp_ref[...], w_ref[...])

    (gx,), extra = _host_call(
        body, name="grad_x", grid=(s // tm, n // tk),
        in_specs=[pl.BlockSpec((tm, d), lambda i, k: (i, 0)), pl.BlockSpec((tm, tk), lambda i, k: (i, k)),
                  pl.BlockSpec((d, tk), lambda i, k: (0, k))],
        out_specs=[pl.BlockSpec((tm, d), lambda i, k: (i, 0))],
        out_shape=[SDS((s, d), F32)], scratch_shapes=[], args=(dr1, dp, wi), hosted=hosted)
    return gx, extra


def _adamw_math(w, g, m, v):
    m2 = ADAM_B1 * m + (1.0 - ADAM_B1) * g
    v2 = ADAM_B2 * v + (1.0 - ADAM_B2) * (g * g)
    m_hat = m2 / (1.0 - ADAM_B1 ** ADAM_STEP)
    v_hat = v2 / (1.0 - ADAM_B2 ** ADAM_STEP)
    delta = -ADAM_LR * (m_hat / (jnp.sqrt(v_hat) + ADAM_EPS) + ADAM_WD * w)
    return delta, m2, v2


def _adamw(w, g, m, v, name):
    r, c = w.shape
    tr = r if r <= 256 else 256

    def body(w_ref, g_ref, m_ref, v_ref, d_ref, m2_ref, v2_ref):
        delta, m2, v2 = _adamw_math(w_ref[...], g_ref[...], m_ref[...], v_ref[...])
        d_ref[...] = delta
        m2_ref[...] = m2
        v2_ref[...] = v2

    blk = pl.BlockSpec((tr, c), lambda i: (i, 0))
    return pl.pallas_call(
        body, name=name, grid=(r // tr,), in_specs=[blk] * 4, out_specs=[blk] * 3,
        out_shape=[SDS((r, c), F32)] * 3, compiler_params=_cp(("parallel",)),
    )(w, g, m, v)


def _sum_parts(parts, name):
    k, r, c = parts.shape

    def body(p_ref, o_ref):
        acc = p_ref[0]
        for q in range(1, k):
            acc = acc + p_ref[q]
        o_ref[...] = acc

    return pl.pallas_call(
        body, name=name, grid=(1,),
        in_specs=[pl.BlockSpec((k, r, c), lambda i: (0, 0, 0))],
        out_specs=pl.BlockSpec((r, c), lambda i: (0, 0)),
        out_shape=SDS((r, c), F32), compiler_params=_cp(("arbitrary",)),
    )(parts)


def _piece_shape(full_shape, axis):
    r, c = full_shape
    return (r // 2, c // 4) if axis == 1 else (r // 8, c)


def _piece_spec(full_shape, axis, tr, chip_of, half_of):
    hr, wc = _piece_shape(full_shape, axis)
    nb = hr // tr
    if axis == 1:
        return pl.BlockSpec((tr, wc), lambda *a: (half_of(*a) * nb + a[-2], chip_of(*a)))
    return pl.BlockSpec((tr, wc), lambda *a: ((2 * chip_of(*a) + half_of(*a)) * nb + a[-2], 0))


def _place_cast(w, axis, pos, name):
    r, c = w.shape
    tr = min(r, 256)
    nb = r // tr
    full = (r, 4 * c) if axis == 1 else (4 * r, c)
    out_map = (lambda i, pos: (i, pos[0])) if axis == 1 else (lambda i, pos: (pos[0] * nb + i, 0))

    def body(pos_ref, w_ref, o_ref):
        o_ref[...] = w_ref[...].astype(o_ref.dtype)

    gs = pltpu.PrefetchScalarGridSpec(
        num_scalar_prefetch=1, grid=(nb,),
        in_specs=[pl.BlockSpec((tr, c), lambda i, pos: (i, 0))], out_specs=pl.BlockSpec((tr, c), out_map))
    return pl.pallas_call(body, name=name, grid_spec=gs, out_shape=SDS(full, BF),
                          compiler_params=_cp(("arbitrary",)))(pos, w)


def _pair_add(g, land, axis, pos, name):
    hr, wc = _piece_shape(g.shape, axis)
    tr = min(hr, 256)

    def body(pos_ref, g_ref, l_ref, o_ref):
        o_ref[0] = (g_ref[...] + l_ref[0].astype(F32)).astype(BF)

    other = lambda q, i, pos: (pos[0] + 1 + q) % 4
    blk = pl.BlockSpec((1, tr, wc), lambda q, i, pos: (other(q, i, pos), i, 0))
    gs = pltpu.PrefetchScalarGridSpec(
        num_scalar_prefetch=1, grid=(3, hr // tr),
        in_specs=[_piece_spec(g.shape, axis, tr, other, lambda q, i, pos: pos[1]), blk], out_specs=blk)
    return pl.pallas_call(body, name=name, grid_spec=gs, out_shape=SDS((4, hr, wc), BF),
                          compiler_params=_cp(("arbitrary", "arbitrary")))(pos, g, land)


def _chip_sum(g, land1, land2, axis, pos, name):
    hr, wc = _piece_shape(g.shape, axis)
    tr = min(hr, 256)
    nb = hr // tr

    def body(pos_ref, g_ref, l1_ref, l2_ref, o_ref):
        acc = g_ref[...] + l1_ref[0].astype(F32)
        for q in range(3):
            acc = acc + l2_ref[q].astype(F32)
        o_ref[...] = acc

    gs = pltpu.PrefetchScalarGridSpec(
        num_scalar_prefetch=1, grid=(nb,),
        in_specs=[_piece_spec(g.shape, axis, tr, lambda i, pos: pos[0], lambda i, pos: pos[1]),
                  pl.BlockSpec((1, tr, wc), lambda i, pos: (pos[0], i, 0)),
                  pl.BlockSpec((3, tr, wc), lambda i, pos: (0, i, 0))],
        out_specs=pl.BlockSpec((tr, wc), lambda i, pos: (pos[1] * nb + i, 0)))
    return pl.pallas_call(body, name=name, grid_spec=gs, out_shape=SDS((2 * hr, wc), F32),
                          compiler_params=_cp(("arbitrary",)))(pos, g, land1, land2)


ANY = pl.BlockSpec(memory_space=pl.ANY)
COMM = pltpu.CompilerParams(has_side_effects=True)


def _on_each_device(fn):
    x, y, c = lax.axis_index("x"), lax.axis_index("y"), lax.axis_index("c")
    for sx in (0, 1):
        for sy in (0, 1):
            for sc in (0, 1):
                @pl.when(jnp.logical_and(jnp.logical_and(x == sx, y == sy), c == sc))
                def _(sx=sx, sy=sy, sc=sc):
                    fn(sx, sy, sc)


def _remote(src, dst, send_sem, recv_sem, to):
    return pltpu.make_async_remote_copy(src_ref=src, dst_ref=dst, send_sem=send_sem, recv_sem=recv_sem,
                                        device_id=to, device_id_type=MESH)


def _piece_ref(ref, axis, j, h):
    r, c = ref.shape
    hr, wc = _piece_shape((r, c), axis)
    if axis == 1:
        return ref.at[pl.ds(h * hr, hr), pl.ds(j * wc, wc)]
    return ref.at[pl.ds((2 * j + h) * hr, hr), :]


class _Stage:
    def __init__(self, ins, outs, n_sems, make_plan, aliases=None):
        self.ins, self.outs, self.n_sems, self.make_plan = list(ins), list(outs), n_sems, make_plan
        self.aliases = dict(aliases or {})


def _start(plan):
    def dev(sx, sy, sc):
        for cp, _, _ in plan(sx, sy, sc):
            cp.start()

    _on_each_device(dev)


def _finish(plan):
    def dev(sx, sy, sc):
        for _, sent, got in plan(sx, sy, sc):
            sent.wait_send()
            got.wait_recv()

    _on_each_device(dev)


def _comm_call(stage, name):
    n_in, n_out = len(stage.ins), len(stage.outs)

    def body(*refs):
        plan = stage.make_plan(refs[:n_in], refs[n_in:n_in + n_out], *refs[n_in + n_out:])
        _start(plan)
        _finish(plan)

    return pl.pallas_call(
        body, name=name, in_specs=[ANY] * n_in, out_specs=[ANY] * n_out, out_shape=stage.outs,
        input_output_aliases=stage.aliases,
        scratch_shapes=[pltpu.SemaphoreType.DMA((stage.n_sems,)), pltpu.SemaphoreType.DMA((stage.n_sems,))],
        compiler_params=COMM,
    )(*stage.ins)


class _SemsFrom:
    def __init__(self, sems, base):
        self.sems, self.base = sems, base

    @property
    def at(self):
        return self

    def __getitem__(self, k):
        return self.sems.at[self.base + k]


def _both(a, b):
    na, nb = len(a.ins), len(b.ins)
    ma = len(a.outs)

    def make_plan(ins, outs, send_sems, recv_sems):
        pa = a.make_plan(ins[:na], outs[:ma], send_sems, recv_sems)
        pb = b.make_plan(ins[na:], outs[ma:], _SemsFrom(send_sems, a.n_sems), _SemsFrom(recv_sems, a.n_sems))
        return lambda sx, sy, sc: pa(sx, sy, sc) + pb(sx, sy, sc)

    aliases = dict(a.aliases)
    aliases.update({na + i: ma + o for i, o in b.aliases.items()})
    return _Stage(a.ins + b.ins, a.outs + b.outs, a.n_sems + b.n_sems, make_plan, aliases)


def _same(cp):
    return (cp, cp, cp)


def _stage_gather_send(fulls, axes):
    n = len(fulls)

    def make_plan(ins, outs, send_sems, recv_sems):
        def plan(sx, sy, sc):
            cps = []
            for w in range(n):
                mine = _piece_ref(outs[w], axes[w], 2 * sx + sy, sc)
                for r, (fx, fy) in enumerate(CHIP_RELS):
                    k = 3 * w + r
                    to = (sx ^ fx, sy ^ fy, sc)
                    got = _piece_ref(outs[w], axes[w], 2 * (sx ^ fx) + (sy ^ fy), sc)
                    send = _remote(mine, mine, send_sems.at[k], recv_sems.at[k], to)
                    cps.append((send, send, _remote(got, got, send_sems.at[k], recv_sems.at[k], to)))
            return cps

        return plan

    return _Stage(fulls, [SDS(f.shape, f.dtype) for f in fulls], 3 * n, make_plan, {i: i for i in range(n)})


def _stage_gather_forward(fulls, axes):
    n = len(fulls)

    def make_plan(ins, outs, send_sems, recv_sems):
        def plan(sx, sy, sc):
            cps = []
            sib = (sx, sy, 1 - sc)
            for w in range(n):
                for r, (fx, fy) in enumerate(CHIP_RELS):
                    k = 3 * w + r
                    pj = 2 * (sx ^ fx) + (sy ^ fy)
                    have = _piece_ref(outs[w], axes[w], pj, sc)
                    want = _piece_ref(outs[w], axes[w], pj, 1 - sc)
                    send = _remote(have, have, send_sems.at[k], recv_sems.at[k], sib)
                    cps.append((send, send, _remote(want, want, send_sems.at[k], recv_sems.at[k], sib)))
            return cps

        return plan

    return _Stage(fulls, [SDS(f.shape, f.dtype) for f in fulls], 3 * n, make_plan, {i: i for i in range(n)})


def _in_place(fulls, n_sems, make_plan):
    return _Stage(fulls, [SDS(f.shape, f.dtype) for f in fulls], n_sems, make_plan, {i: i for i in range(len(fulls))})


def _stage_gather_neighbours(fulls, axes):
    n = len(fulls)

    def make_plan(ins, outs, send_sems, recv_sems):
        def plan(sx, sy, sc):
            cps = []
            for w in range(n):
                mine = _piece_ref(outs[w], axes[w], 2 * sx + sy, sc)
                for r, (px, py) in enumerate(((sx ^ 1, sy), (sx, sy ^ 1))):
                    got = _piece_ref(outs[w], axes[w], 2 * px + py, sc)
                    send = _remote(mine, mine, send_sems.at[2 * w + r], recv_sems.at[2 * w + r], (px, py, sc))
                    cps.append((send, send, _remote(got, got, send_sems.at[2 * w + r], recv_sems.at[2 * w + r],
                                                    (px, py, sc))))
            return cps

        return plan

    return _in_place(fulls, 2 * n, make_plan)


def _stage_gather_relay(fulls, axes):
    n = len(fulls)

    def make_plan(ins, outs, send_sems, recv_sems):
        def plan(sx, sy, sc):
            cps = []
            sib = (sx, sy, 1 - sc)
            jx, jy, jd = 2 * (sx ^ 1) + sy, 2 * sx + (sy ^ 1), 2 * (sx ^ 1) + (sy ^ 1)
            for w in range(n):
                piece = lambda j, h, w=w: _piece_ref(outs[w], axes[w], j, h)
                relayed, to = (jx, (sx, sy ^ 1, sc)) if sc == 0 else (jy, (sx ^ 1, sy, sc))
                k = 3 * w
                send = _remote(piece(relayed, sc), piece(relayed, sc), send_sems.at[k], recv_sems.at[k], to)
                cps.append((send, send, _remote(piece(jd, sc), piece(jd, sc), send_sems.at[k], recv_sems.at[k], to)))
                for r, j in enumerate((jx, jy)):
                    k = 3 * w + 1 + r
                    send = _remote(piece(j, sc), piece(j, sc), send_sems.at[k], recv_sems.at[k], sib)
                    cps.append((send, send, _remote(piece(j, 1 - sc), piece(j, 1 - sc), send_sems.at[k],
                                                    recv_sems.at[k], sib)))
            return cps

        return plan

    return _in_place(fulls, 3 * n, make_plan)


def _stage_gather_last(fulls, axes):
    n = len(fulls)

    def make_plan(ins, outs, send_sems, recv_sems):
        def plan(sx, sy, sc):
            cps = []
            sib = (sx, sy, 1 - sc)
            jd = 2 * (sx ^ 1) + (sy ^ 1)
            for w in range(n):
                have, want = _piece_ref(outs[w], axes[w], jd, sc), _piece_ref(outs[w], axes[w], jd, 1 - sc)
                send = _remote(have, have, send_sems.at[w], recv_sems.at[w], sib)
                cps.append((send, send, _remote(want, want, send_sems.at[w], recv_sems.at[w], sib)))
            return cps

        return plan

    return _in_place(fulls, n, make_plan)


def _stage_pair_exchange(grads, axes):
    n = len(grads)

    def make_plan(gs, land, send_sems, recv_sems):
        def plan(sx, sy, sc):
            return [_same(_remote(_piece_ref(gs[w], axes[w], jj, 1 - sc), land[w].at[jj], send_sems.at[4 * w + jj],
                                  recv_sems.at[4 * w + jj], (sx, sy, 1 - sc)))
                    for w in range(n) for jj in range(4)]

        return plan

    return _Stage(grads, [SDS((4,) + _piece_shape(g.shape, a), g.dtype) for g, a in zip(grads, axes)], 4 * n,
                  make_plan)


def _stage_chip_scatter(pieces):
    n = len(pieces)

    def make_plan(ps, land, send_sems, recv_sems):
        def plan(sx, sy, sc):
            return [_same(_remote(ps[w].at[2 * (sx ^ fx) + (sy ^ fy)], land[w].at[r], send_sems.at[3 * w + r],
                                  recv_sems.at[3 * w + r], (sx ^ fx, sy ^ fy, sc)))
                    for w in range(n) for r, (fx, fy) in enumerate(CHIP_RELS)]

        return plan

    return _Stage(pieces, [SDS((3,) + p.shape[1:], p.dtype) for p in pieces], 3 * n, make_plan)


def _stage_pair_share(shards):
    n = len(shards)

    def make_plan(ins, outs, send_sems, recv_sems):
        def plan(sx, sy, sc):
            cps = []
            sib = (sx, sy, 1 - sc)
            for w in range(n):
                hr = shards[w].shape[0] // 2
                mine = outs[w].at[pl.ds(sc * hr, hr), :]
                theirs = outs[w].at[pl.ds((1 - sc) * hr, hr), :]
                send = _remote(mine, mine, send_sems.at[w], recv_sems.at[w], sib)
                cps.append((send, send, _remote(theirs, theirs, send_sems.at[w], recv_sems.at[w], sib)))
            return cps

        return plan

    return _Stage(shards, [SDS(g.shape, g.dtype) for g in shards], n, make_plan, {i: i for i in range(n)})


def _stage_gather_small(stack):
    def make_plan(ins, outs, send_sems, recv_sems):
        def plan(sx, sy, sc):
            mine = outs[0].at[4 * sx + 2 * sy + sc]
            return [_same(_remote(mine, mine, send_sems.at[k], recv_sems.at[k], (sx ^ fx, sy ^ fy, sc ^ fc)))
                    for k, (fx, fy, fc) in enumerate(DEV_RELS)]

        return plan

    return _Stage([stack], [SDS(stack.shape, stack.dtype)], 7, make_plan, {0: 0})


def kernel(x, w_in, conv_a_w, w_out_a, conv_b_w, conv_b_bias, ln_b_gamma, ln_b_beta, w_out_b, w_o, ln1_gamma, ln1_beta, w_up, w_down, ln2_gamma, ln2_beta, loss_target, m_w_in, m_conv_a_w, m_w_out_a, m_conv_b_w, m_conv_b_bias, m_ln_b_gamma, m_ln_b_beta, m_w_out_b, m_w_o, m_ln1_gamma, m_ln1_beta, m_w_up, m_w_down, m_ln2_gamma, m_ln2_beta, v_w_in, v_conv_a_w, v_w_out_a, v_conv_b_w, v_conv_b_bias, v_ln_b_gamma, v_ln_b_beta, v_w_out_b, v_w_o, v_ln1_gamma, v_ln1_beta, v_w_up, v_w_down, v_ln2_gamma, v_ln2_beta):
    s, d = x.shape[1], x.shape[2]
    xs = x.reshape(s, d)
    tgt = loss_target.reshape(s, d)
    dq = d // 4
    chip = 2 * lax.axis_index("x") + lax.axis_index("y")
    core = lax.axis_index("c")
    pos = jnp.stack([chip, core]).astype(jnp.int32)
    names = ("w_in", "w_out_a", "w_out_b", "w_o", "w_up", "w_down")
    axes = (1, 0, 0, 0, 1, 0)

    conv_pack = jnp.concatenate([jnp.pad(conv_a_w, ((0, 8 - K_A), (0, 0))), jnp.pad(conv_b_w, ((0, 32 - K_B), (0, 0))),
                                 jnp.zeros((8, dq), F32)], axis=0)
    conv_full = lax.dynamic_update_slice(jnp.zeros((conv_pack.shape[0], d), F32), conv_pack, (0, chip * dq))
    fulls = [_place_cast(w, a, pos, "place_" + nm)
             for w, a, nm in zip((w_in, w_out_a, w_out_b, w_o, w_up, w_down), axes, names)]
    vec = lambda a: a.reshape(1, d)
    bias_b, lbg, lbb = vec(conv_b_bias), vec(ln_b_gamma), vec(ln_b_beta)
    l1g, l1b, l2g, l2b = vec(ln1_gamma), vec(ln1_beta), vec(ln2_gamma), vec(ln2_beta)

    (p, xb), landed = _in_proj_own(xs, w_in, pos, _stage_gather_neighbours([fulls[0], conv_full], (1, 1)))
    landed = _comm_call(_stage_gather_relay(landed, (1, 1)), "gather_relay_w_in")
    wi, convs = _comm_call(_stage_gather_last(landed, (1, 1)), "gather_last_w_in")
    cwa, cwb = convs[0:K_A], convs[8:8 + K_B]
    p, small3 = _in_proj_rest(xb, wi, p, pos, _stage_gather_send(fulls[1:4], axes[1:4]))
    (conva, yapre, nb, rstdb, u3), landed = _conv_fwd(
        p, cwa, cwb, bias_b, lbg, lbb, d,
        _both(_stage_gather_forward(small3, axes[1:4]), _stage_gather_send(fulls[4:6], axes[4:6])))
    woa, wob, wo = landed[:3]
    (ya, yb, merged, n1, rstd1), (wup, wdown) = _mixer_out(yapre, u3, p, xs, woa, wob, wo, d,
                                                           _stage_gather_forward(landed[3:], axes[4:6]))
    x1b, hb, dhpre, dr2b, dr1, dr1b, acc_mlp = _mlp(n1, rstd1, tgt, wup, wdown, l1g, l1b, l2g, l2b)
    g_up, gb_up, _ = _grad_w(x1b, dhpre, "grad_w_up")
    g_down, gb_down, _ = _grad_w(hb, dr2b, "grad_w_down")
    dya, dyb, dba, dga, dgb, dca, du1, acc_mix = _mixer_bwd_local(dr1b, p, ya, yb, conva, nb, rstdb, wo, woa, wob,
                                                                   lbg, lbb)
    g_oa, gb_oa, _ = _grad_w(yapre, dya, "grad_w_out_a")
    g_ob, gb_ob, _ = _grad_w(u3, dyb, "grad_w_out_b")
    g_o, gb_o, _ = _grad_w(merged, dr1b, "grad_w_o")

    early, e_axes, e_names = [g_oa, g_ob, g_o, g_up, g_down], axes[1:], names[1:]
    (dp, dcwa, dcwb), land1 = _conv_bwd(dca, du1, p, dba, dga, dgb, cwa, cwb, d,
                                        _stage_pair_exchange([gb_oa, gb_ob, gb_o, gb_up, gb_down], e_axes))
    pieces = [_pair_add(g, l, a, pos, "pair_add_" + nm) for g, l, a, nm in zip(early, land1, e_axes, e_names)]
    pack = jnp.concatenate([dcwa, dcwb, acc_mix, acc_mlp], axis=0)
    stack = lax.dynamic_update_slice(jnp.zeros((8,) + pack.shape, F32), pack[None], (2 * chip + core, 0, 0))
    g_wi, gb_wi, landed = _grad_w(xb, dp, "grad_w_in",
                                  _both(_stage_chip_scatter(pieces), _stage_gather_small(stack)))
    land2, stack = landed[:-1], landed[-1]
    halves = [_chip_sum(g, l1, l2, a, pos, "chip_sum_" + nm)
              for g, l1, l2, a, nm in zip(early, land1, land2, e_axes, e_names)]
    (land1_in,) = _comm_call(_stage_pair_exchange([gb_wi], (1,)), "pair_exchange_w_in")
    piece_in = _pair_add(g_wi, land1_in, 1, pos, "pair_add_w_in")
    grad_x, landed = _grad_x(dr1, dp, wi, _both(_stage_chip_scatter([piece_in]), _stage_pair_share(halves)))
    land2_in, (g_oa, g_ob, g_o, g_up, g_down) = landed[0], landed[1:]
    half_in = _chip_sum(g_wi, land1_in, land2_in, 1, pos, "chip_sum_w_in")
    (g_in,) = _comm_call(_stage_pair_share([half_in]), "pair_share_w_in")
    small = _sum_parts(stack, "small_sum")
    g_ca = lax.dynamic_slice(small, (0, chip * dq), (K_A, dq))
    g_cb = lax.dynamic_slice(small, (8, chip * dq), (K_B, dq))
    g_vec = jnp.stack([small[r] for r in (42, 40, 41, 51, 52, 49, 50)])

    loss = (0.5 / d) * jnp.sum(small[48])

    big = {}
    for name, w, g, m, v in (("w_in", w_in, g_in, m_w_in, v_w_in), ("w_out_a", w_out_a, g_oa, m_w_out_a, v_w_out_a),
                             ("w_out_b", w_out_b, g_ob, m_w_out_b, v_w_out_b), ("w_o", w_o, g_o, m_w_o, v_w_o),
                             ("w_up", w_up, g_up, m_w_up, v_w_up), ("w_down", w_down, g_down, m_w_down, v_w_down),
                             ("conv_a_w", conv_a_w, g_ca, m_conv_a_w, v_conv_a_w),
                             ("conv_b_w", conv_b_w, g_cb, m_conv_b_w, v_conv_b_w)):
        big[name] = (g,) + tuple(_adamw(w, g, m, v, "adamw_" + name))
    vec_names = ("conv_b_bias", "ln_b_gamma", "ln_b_beta", "ln1_gamma", "ln1_beta", "ln2_gamma", "ln2_beta")
    w7 = jnp.stack([conv_b_bias, ln_b_gamma, ln_b_beta, ln1_gamma, ln1_beta, ln2_gamma, ln2_beta])
    m7 = jnp.stack([m_conv_b_bias, m_ln_b_gamma, m_ln_b_beta, m_ln1_gamma, m_ln1_beta, m_ln2_gamma, m_ln2_beta])
    v7 = jnp.stack([v_conv_b_bias, v_ln_b_gamma, v_ln_b_beta, v_ln1_gamma, v_ln1_beta, v_ln2_gamma, v_ln2_beta])
    d7, nm7, nv7 = _adamw(w7, g_vec, m7, v7, "adamw_vectors")
    for q, name in enumerate(vec_names):
        big[name] = (g_vec[q], d7[q], nm7[q], nv7[q])

    order = ("w_in", "conv_a_w", "w_out_a", "conv_b_w", "conv_b_bias", "ln_b_gamma", "ln_b_beta", "w_out_b", "w_o",
             "ln1_gamma", "ln1_beta", "w_up", "w_down", "ln2_gamma", "ln2_beta")
    outs = [loss, grad_x.reshape(x.shape)]
    for part in range(4):
        outs += [big[name][part] for name in order]
    return tuple(outs)
```

```python
import jax
import jax.numpy as jnp
from jax import lax
from jax.experimental import pallas as pl
from jax.experimental.pallas import tpu as pltpu

F32 = jnp.float32
BF = jnp.bfloat16
SDS = jax.ShapeDtypeStruct
MESH = pl.DeviceIdType.MESH

ALPHA = 2.0 ** 0.25
LN_EPS = 1e-5
K_A = 3
K_B = 31
HALO = 32
CONV_TILE = 512
CONV_ROWS = 64
CONV_LANES = 128
ADAM_LR = 0.001
ADAM_B1 = 0.9
ADAM_B2 = 0.999
ADAM_EPS = 1e-08
ADAM_WD = 0.01
ADAM_STEP = 10
P_DT = BF
CHIP_RELS = ((1, 0), (0, 1), (1, 1))
DEV_RELS = tuple((fx, fy, fc) for fx in (0, 1) for fy in (0, 1) for fc in (0, 1))[1:]


def _cp(sem=None, vmem_mb=56, side_effects=False):
    return pltpu.CompilerParams(dimension_semantics=sem, vmem_limit_bytes=vmem_mb << 20,
                                has_side_effects=side_effects)


def _const(shape):
    return pl.BlockSpec(shape, lambda *_: (0,) * len(shape), pipeline_mode=pl.Buffered(1))


def _sig(v):
    return jax.nn.sigmoid(v)


def _ln_fwd(r):
    mu = jnp.mean(r, axis=-1, keepdims=True)
    xc = r - mu
    var = jnp.mean(xc * xc, axis=-1, keepdims=True)
    rstd = lax.rsqrt(var + LN_EPS)
    return xc * rstd, rstd


def _ln_bwd(dn, n, rstd):
    m1 = jnp.mean(dn, axis=-1, keepdims=True)
    m2 = jnp.mean(dn * n, axis=-1, keepdims=True)
    return rstd * (dn - m1 - n * m2)


def _dot(a, b):
    return jnp.dot(a, b, preferred_element_type=F32)


def _dot_nt(a, b):
    return lax.dot_general(a, b, (((1,), (1,)), ((), ())), preferred_element_type=F32)


def _dot_tn(a, b):
    return lax.dot_general(a, b, (((0,), (0,)), ((), ())), preferred_element_type=F32)


def _tile(n, pref):
    if n <= pref:
        return n
    return max(t for t in range(128, pref + 1, 128) if n % t == 0)


def _rowsum8(v):
    acc = v[0:8]
    for g in range(1, v.shape[0] // 8):
        acc = acc + v[8 * g:8 * g + 8]
    return acc


def _taps(win, offsets, rows):
    r_all = win.shape[0]
    by_res = {}
    for k, o in enumerate(offsets):
        by_res.setdefault(o % 8, []).append((k, o // 8))
    for s, taps in sorted(by_res.items()):
        r = win if s == 0 else pltpu.roll(win, r_all - s, 0)
        for k, q in taps:
            yield k, r[8 * q:8 * q + rows]


CAUSAL_A = [HALO - (K_A - 1) + k for k in range(K_A)]
CAUSAL_B = [HALO - (K_B - 1) + k for k in range(K_B)]
ANTI_A = [K_A - 1 - k for k in range(K_A)]
ANTI_B = [K_B - 1 - k for k in range(K_B)]


def _host_call(body, *, name, grid, in_specs, out_specs, out_shape, scratch_shapes, args, hosted, prefetch=None,
               aliases=None, body_gets_stage_refs=False):
    n_in, n_out, n_scr = len(in_specs), len(out_specs), len(scratch_shapes)
    n_steps = 1
    for size in grid:
        n_steps *= size
    if isinstance(hosted, _Stage):
        hosted = [(hosted, 0, n_steps - 1)]
    n_pre = 0 if prefetch is None else 1
    pre = () if prefetch is None else (prefetch,)
    sem = ("arbitrary",) * len(grid)
    own_aliases = {n_pre + a: b for a, b in (aliases or {}).items()}

    def call(kernel_body, ins, outs, shapes, scratch, all_aliases, side_effects, operands):
        gs = pltpu.PrefetchScalarGridSpec(num_scalar_prefetch=n_pre, grid=grid, in_specs=ins, out_specs=outs,
                                          scratch_shapes=scratch)
        return pl.pallas_call(kernel_body, name=name, grid_spec=gs, out_shape=shapes,
                              input_output_aliases=all_aliases,
                              compiler_params=_cp(sem, side_effects=side_effects))(*pre, *operands)

    if hosted is None:
        res = call(body, list(in_specs), list(out_specs), list(out_shape), list(scratch_shapes), own_aliases, False,
                   args)
        return list(res), []
    stages = [st for st, _, _ in hosted]
    h_ins = [a for st in stages for a in st.ins]
    h_outs = [o for st in stages for o in st.outs]
    borrowed = {hi: k for hi, a in enumerate(h_ins) for k, own in enumerate(args) if a is own}
    passed = [hi for hi in range(len(h_ins)) if hi not in borrowed]
    h_in, h_out, n_sems = len(passed), len(h_outs), sum(st.n_sems for st in stages)

    def full_body(*refs):
        pre_refs, refs = refs[:n_pre], refs[n_pre:]
        ins, refs = refs[:n_in], refs[n_in:]
        hins = [None] * len(h_ins)
        for hi, ref in zip(passed, refs[:h_in]):
            hins[hi] = ref
        refs = refs[h_in:]
        outs, refs = refs[:n_out], refs[n_out:]
        houts, refs = refs[:h_out], refs[h_out:]
        scr, (send_sems, recv_sems) = refs[:n_scr], refs[n_scr:]
        step = 0
        for a, size in enumerate(grid):
            step = step * size + pl.program_id(a)
        phases, i0, o0, k0 = [], 0, 0, 0
        for st, starts, finishes in hosted:
            plans = st.make_plan(hins[i0:i0 + len(st.ins)], houts[o0:o0 + len(st.outs)],
                                 _SemsFrom(send_sems, k0), _SemsFrom(recv_sems, k0))
            if not isinstance(plans, list):
                plans, starts, finishes = [plans], [starts], [finishes]
            phases += list(zip(plans, starts, finishes))
            i0, o0, k0 = i0 + len(st.ins), o0 + len(st.outs), k0 + st.n_sems
        for plan, at, _ in phases:
            @pl.when(step == at)
            def _(plan=plan):
                _start(plan)

        if body_gets_stage_refs:
            body(*pre_refs, *ins, *outs, *scr, houts)
        else:
            body(*pre_refs, *ins, *outs, *scr)

        for plan, _, at in phases:
            @pl.when(step == at)
            def _(plan=plan):
                _finish(plan)

    all_aliases = dict(own_aliases)
    i0 = o0 = 0
    for st in stages:
        for a, b in st.aliases.items():
            hi = i0 + a
            operand = borrowed[hi] if hi in borrowed else n_in + passed.index(hi)
            all_aliases[n_pre + operand] = n_out + o0 + b
        i0, o0 = i0 + len(st.ins), o0 + len(st.outs)
    res = call(full_body, list(in_specs) + [ANY] * h_in, list(out_specs) + [ANY] * h_out,
               list(out_shape) + h_outs,
               list(scratch_shapes) + [pltpu.SemaphoreType.DMA((n_sems,)), pltpu.SemaphoreType.DMA((n_sems,))],
               all_aliases, True, (*args, *[h_ins[hi] for hi in passed]))
    return list(res[:n_out]), list(res[n_out:])


def _in_proj_own(x, w_shard, pos, hosted):
    s, d = x.shape
    tn = w_shard.shape[1]
    tm = min(s, 1024)

    def body(pos_ref, x_ref, w_ref, p_ref, xb_ref, wb):
        @pl.when(pl.program_id(0) == 0)
        def _():
            wb[...] = w_ref[...].astype(BF)

        xb = x_ref[...].astype(BF)
        xb_ref[...] = xb
        p_ref[...] = _dot(xb, wb[...]).astype(p_ref.dtype)

    return _host_call(
        body, name="in_proj_own", grid=(s // tm,), prefetch=pos,
        in_specs=[pl.BlockSpec((tm, d), lambda i, pos: (i, 0)),
                  pl.BlockSpec((d, tn), lambda i, pos: (0, 0), pipeline_mode=pl.Buffered(1))],
        out_specs=[pl.BlockSpec((tm, tn), lambda i, pos: (i, pos[0])), pl.BlockSpec((tm, d), lambda i, pos: (i, 0))],
        out_shape=[SDS((s, 4 * tn), P_DT), SDS((s, d), BF)], scratch_shapes=[pltpu.VMEM((d, tn), BF)],
        args=(x, w_shard), hosted=hosted)


def _in_proj_rest(xb, wi, p, pos, diagonal, other):
    s, d = xb.shape
    n = wi.shape[1]
    tm, tn = min(s // 4, 1024), n // 4
    ni = s // tm

    def body(pos_ref, xb_ref, w_ref, p_in, p_ref, wdiag, dsem, stage_outs):
        j = pl.program_id(0)
        step = j * ni + pl.program_id(1)

        def diagonal_block(act):
            for chip in range(4):
                @pl.when(pos_ref[0] == chip)
                def _(chip=chip):
                    act(pltpu.make_async_copy(stage_outs[0].at[:, pl.ds((chip ^ 3) * tn, tn)], wdiag, dsem))

        @pl.when(step == 2 * ni - 1)
        def _():
            diagonal_block(lambda cp: cp.start())

        @pl.when(step == 2 * ni)
        def _():
            diagonal_block(lambda cp: cp.wait())

        @pl.when(j < 2)
        def _():
            p_ref[...] = _dot(xb_ref[...], w_ref[...]).astype(p_ref.dtype)

        @pl.when(j == 2)
        def _():
            p_ref[...] = _dot(xb_ref[...], wdiag[...]).astype(p_ref.dtype)

    def col(j, i, pos):
        return lax.bitwise_xor(pos[0], jnp.where(j == 0, 2, jnp.where(j == 1, 1, 3)))

    def piped_col(j, i, pos):
        return lax.bitwise_xor(pos[0], jnp.where(j == 0, 2, 1))

    (p,), extra = _host_call(
        body, name="in_proj_rest", grid=(3, ni), prefetch=pos, aliases={2: 0}, body_gets_stage_refs=True,
        in_specs=[pl.BlockSpec((tm, d), lambda j, i, pos: (i, 0)),
                  pl.BlockSpec((d, tn), lambda j, i, pos: (0, piped_col(j, i, pos))), ANY],
        out_specs=[pl.BlockSpec((tm, tn), lambda j, i, pos: (i, col(j, i, pos)))],
        out_shape=[SDS((s, n), P_DT)], scratch_shapes=[pltpu.VMEM((d, tn), BF), pltpu.SemaphoreType.DMA(())],
        args=(xb, wi, p),
        hosted=[(diagonal, [0, ni + 1], [ni, 2 * ni - 2]), (other, 0, 3 * ni - 1)])
    n_d = len(diagonal.outs)
    return p, extra[:n_d], extra[n_d:]


def _col_spec(tm, d, k):
    return pl.BlockSpec((tm, d), lambda i, k=k: (i, k))


def _prev_halo_spec(tm, d, k):
    r = tm // HALO
    return pl.BlockSpec((HALO, d), lambda i, k=k: (jnp.maximum(i * r - 1, 0), k))


def _next_halo_spec(tm, d, k, s):
    r = tm // HALO
    last = s // HALO - 1
    return pl.BlockSpec((HALO, d), lambda i, k=k: (jnp.minimum((i + 1) * r, last), k))


def _conv_fwd(p, cwa, cwb, bias_b, lbg, lbb, d, hosted=None):
    s = p.shape[0]
    tm = min(s, CONV_TILE)
    nt = s // tm

    def body(ba_ref, ca_ref, va_ref, vb_ref, gb_ref, hca_ref, hva_ref, hvb_ref, hgb_ref,
             cwa_ref, cwb_ref, bias_ref, lbg_ref, lbb_ref,
             conva_ref, yapre_ref, nb_ref, rstdb_ref, u3_ref,
             zbuf, ubuf, u1buf):
        i = pl.program_id(0)
        keep = (i > 0).astype(F32)
        zbuf[pl.ds(0, HALO), :] = hca_ref[...].astype(F32) * hva_ref[...].astype(F32) * keep
        ubuf[pl.ds(0, HALO), :] = hvb_ref[...].astype(F32) * _sig(hgb_ref[...].astype(F32)) * keep
        zbuf[pl.ds(HALO, tm), :] = ca_ref[...].astype(F32) * va_ref[...].astype(F32)
        ubuf[pl.ds(HALO, tm), :] = vb_ref[...].astype(F32) * _sig(gb_ref[...].astype(F32))

        def chunk(j, carry):
            r0 = pl.multiple_of(j * CONV_ROWS, CONV_ROWS)
            rows = pl.ds(r0, CONV_ROWS)
            for lc in range(d // CONV_LANES):
                ls = pl.ds(lc * CONV_LANES, CONV_LANES)
                acc = jnp.zeros((CONV_ROWS, CONV_LANES), F32)
                for k, sl in _taps(zbuf[pl.ds(r0, CONV_ROWS + HALO), ls], CAUSAL_A, CONV_ROWS):
                    acc = acc + cwa_ref[pl.ds(k, 1), ls] * sl
                conva_ref[rows, ls] = acc.astype(BF)
                yapre_ref[rows, ls] = (ba_ref[rows, ls].astype(F32) * acc).astype(BF)
                acc = jnp.zeros((CONV_ROWS, CONV_LANES), F32)
                for k, sl in _taps(ubuf[pl.ds(r0, CONV_ROWS + HALO), ls], CAUSAL_B, CONV_ROWS):
                    acc = acc + cwb_ref[pl.ds(k, 1), ls] * sl
                u1buf[rows, ls] = acc + bias_ref[:, ls]
            return carry

        lax.fori_loop(0, tm // CONV_ROWS, chunk, 0)
        nb, rstd = _ln_fwd(u1buf[...])
        nb_ref[...] = nb
        rstdb_ref[...] = rstd
        u2 = nb * lbg_ref[...] + lbb_ref[...]
        u3_ref[...] = (u2 * _sig(u2)).astype(BF)

    vec = _const((1, d))
    return _host_call(
        body, name="conv_fwd", grid=(nt,),
        in_specs=[_col_spec(tm, d, k) for k in range(5)] + [_prev_halo_spec(tm, d, k) for k in (1, 2, 3, 4)]
        + [_const((K_A, d)), _const((K_B, d)), vec, vec, vec],
        out_specs=[pl.BlockSpec((tm, d), lambda i: (i, 0)), pl.BlockSpec((tm, d), lambda i: (i, 0)),
                   pl.BlockSpec((tm, d), lambda i: (i, 0)), pl.BlockSpec((tm, 1), lambda i: (i, 0)),
                   pl.BlockSpec((tm, d), lambda i: (i, 0))],
        out_shape=[SDS((s, d), BF), SDS((s, d), BF), SDS((s, d), F32), SDS((s, 1), F32), SDS((s, d), BF)],
        scratch_shapes=[pltpu.VMEM((HALO + tm, d), F32), pltpu.VMEM((HALO + tm, d), F32), pltpu.VMEM((tm, d), F32)],
        args=(p, p, p, p, p, p, p, p, p, cwa, cwb, bias_b, lbg, lbb), hosted=hosted)


def _mixer_out(yapre, u3, p, x, woa, wob, wo, d, hosted=None):
    s = x.shape[0]
    tm = min(s, 512)

    def body(yapre_ref, u3_ref, ga_ref, gb_ref, x_ref, woa_ref, wob_ref, wo_ref,
             ya_ref, yb_ref, merged_ref, n1_ref, rstd1_ref):
        ya = _dot(yapre_ref[...], woa_ref[...])
        yb = _dot(u3_ref[...], wob_ref[...])
        ya_ref[...] = ya.astype(BF)
        yb_ref[...] = yb.astype(BF)
        merged = (_sig(ga_ref[...].astype(F32)) * ya + _sig(gb_ref[...].astype(F32)) * yb).astype(BF)
        merged_ref[...] = merged
        r1 = F32(ALPHA) * x_ref[...] + _dot(merged, wo_ref[...])
        n1, rstd1 = _ln_fwd(r1)
        n1_ref[...] = n1
        rstd1_ref[...] = rstd1

    row = pl.BlockSpec((tm, d), lambda i: (i, 0))
    return _host_call(
        body, name="mixer_out", grid=(s // tm,),
        in_specs=[row, row, _col_spec(tm, d, 5), _col_spec(tm, d, 6), row,
                  _const((d, d)), _const((d, d)), _const((d, d))],
        out_specs=[row, row, row, row, pl.BlockSpec((tm, 1), lambda i: (i, 0))],
        out_shape=[SDS((s, d), BF), SDS((s, d), BF), SDS((s, d), BF), SDS((s, d), F32), SDS((s, 1), F32)],
        scratch_shapes=[], args=(yapre, u3, p, p, x, woa, wob, wo), hosted=hosted)


def _mlp(n1, rstd1, tgt, wup, wdown, l1g, l1b, l2g, l2b):
    s, d = n1.shape
    dff = wup.shape[1]
    tm = min(s, 256)
    fc = min(dff, 1024)
    nq = dff // fc

    def body(n1_ref, rstd1_ref, tgt_ref, wup_ref, wdown_ref, l1g_ref, l1b_ref, l2g_ref, l2b_ref,
             x1b_ref, hb_ref, dhpre_ref, dr2b_ref, dr1_ref, dr1b_ref, acc_ref, rbuf):
        i = pl.program_id(0)
        n1v = n1_ref[...]
        x1 = n1v * l1g_ref[...] + l1b_ref[...]
        x1b = x1.astype(BF)
        x1b_ref[...] = x1b
        ff = jnp.zeros((tm, d), F32)
        for q in range(nq):
            cs = pl.ds(q * fc, fc)
            r = jnp.maximum(_dot(x1b, wup_ref[:, cs]), 0.0)
            rbuf[:, cs] = r
            hq = (r * r).astype(BF)
            hb_ref[:, cs] = hq
            ff = ff + _dot(hq, wdown_ref[cs, :])
        n2, rstd2 = _ln_fwd(F32(ALPHA) * x1 + ff)
        x2 = n2 * l2g_ref[...] + l2b_ref[...]
        err = x2 - tgt_ref[...]
        dx2 = err * F32(1.0 / d)
        dr2 = _ln_bwd(dx2 * l2g_ref[...], n2, rstd2)
        dr2b = dr2.astype(BF)
        dr2b_ref[...] = dr2b
        dx1 = F32(ALPHA) * dr2
        for q in range(nq):
            cs = pl.ds(q * fc, fc)
            dh = _dot_nt(dr2b, wdown_ref[cs, :])
            dhp = (dh * (2.0 * rbuf[:, cs])).astype(BF)
            dhpre_ref[:, cs] = dhp
            dx1 = dx1 + _dot_nt(dhp, wup_ref[:, cs])
        dr1 = _ln_bwd(dx1 * l1g_ref[...], n1v, rstd1_ref[...])
        dr1_ref[...] = dr1
        dr1b_ref[...] = dr1.astype(BF)

        @pl.when(i == 0)
        def _():
            acc_ref[...] = jnp.zeros_like(acc_ref)

        for q, val in enumerate((err * err, dx2 * n2, dx2, dx1 * n1v, dx1)):
            acc_ref[pl.ds(q, 1), :] += jnp.sum(val, axis=0, keepdims=True)

    row = pl.BlockSpec((tm, d), lambda i: (i, 0))
    wide = pl.BlockSpec((tm, dff), lambda i: (i, 0))
    vec = _const((1, d))
    return pl.pallas_call(
        body, name="mlp_fwd_bwd", grid=(s // tm,),
        in_specs=[row, pl.BlockSpec((tm, 1), lambda i: (i, 0)), row, _const((d, dff)), _const((dff, d)),
                  vec, vec, vec, vec],
        out_specs=[row, wide, wide, row, row, row, pl.BlockSpec((8, d), lambda i: (0, 0))],
        out_shape=[SDS((s, d), BF), SDS((s, dff), BF), SDS((s, dff), BF), SDS((s, d), BF), SDS((s, d), F32),
                   SDS((s, d), BF), SDS((8, d), F32)],
        scratch_shapes=[pltpu.VMEM((tm, dff), F32)],
        compiler_params=_cp(("arbitrary",)),
    )(n1, rstd1, tgt, wup, wdown, l1g, l1b, l2g, l2b)


def _mixer_bwd_local(dr1b, p, ya, yb, conva, nb, rstdb, wo, woa, wob, lbg, lbb):
    s, d = ya.shape
    tm = min(s, 256)

    def body(dr1b_ref, ba_ref, ga_ref, gb_ref, ya_ref, yb_ref, conva_ref, nb_ref, rstdb_ref,
             wo_ref, woa_ref, wob_ref, lbg_ref, lbb_ref,
             dya_ref, dyb_ref, dba_ref, dga_ref, dgb_ref, dca_ref, du1_ref, acc_ref):
        i = pl.program_id(0)
        dmerged = _dot_nt(dr1b_ref[...], wo_ref[...])
        sa = _sig(ga_ref[...].astype(F32))
        sb = _sig(gb_ref[...].astype(F32))
        dya = (dmerged * sa).astype(BF)
        dyb = (dmerged * sb).astype(BF)
        dya_ref[...] = dya
        dyb_ref[...] = dyb
        dga_ref[...] = (dmerged * ya_ref[...].astype(F32) * (sa * (1.0 - sa))).astype(BF)
        dgb_ref[...] = (dmerged * yb_ref[...].astype(F32) * (sb * (1.0 - sb))).astype(BF)
        dyapre = _dot_nt(dya, woa_ref[...])
        dba_ref[...] = (dyapre * conva_ref[...].astype(F32)).astype(BF)
        dca_ref[...] = dyapre * ba_ref[...].astype(F32)
        du3 = _dot_nt(dyb, wob_ref[...])
        nbv = nb_ref[...]
        u2 = nbv * lbg_ref[...] + lbb_ref[...]
        sg = _sig(u2)
        du2 = du3 * (sg * (1.0 + u2 * (1.0 - sg)))
        du1 = _ln_bwd(du2 * lbg_ref[...], nbv, rstdb_ref[...])
        du1_ref[...] = du1

        @pl.when(i == 0)
        def _():
            acc_ref[...] = jnp.zeros_like(acc_ref)

        for q, val in enumerate((du2 * nbv, du2, du1)):
            acc_ref[pl.ds(q, 1), :] += jnp.sum(val, axis=0, keepdims=True)

    row = pl.BlockSpec((tm, d), lambda i: (i, 0))
    vec = _const((1, d))
    return pl.pallas_call(
        body, name="mixer_bwd_local", grid=(s // tm,),
        in_specs=[row, _col_spec(tm, d, 0), _col_spec(tm, d, 5), _col_spec(tm, d, 6), row, row, row, row,
                  pl.BlockSpec((tm, 1), lambda i: (i, 0)), _const((d, d)), _const((d, d)), _const((d, d)), vec, vec],
        out_specs=[row, row, row, row, row, row, row, pl.BlockSpec((8, d), lambda i: (0, 0))],
        out_shape=[SDS((s, d), BF)] * 5 + [SDS((s, d), F32), SDS((s, d), F32), SDS((8, d), F32)],
        compiler_params=_cp(("arbitrary",)),
    )(dr1b, p, p, p, ya, yb, conva, nb, rstdb, wo, woa, wob, lbg, lbb)


def _conv_bwd(dca, du1, p, dba, dga, dgb, cwa, cwb, d, hosted=None):
    s = dca.shape[0]
    tm = min(s, CONV_TILE)
    nt = s // tm

    def body(dca_ref, du1_ref, ndca_ref, ndu1_ref, ca_ref, va_ref, vb_ref, gb_ref,
             dba_ref, dga_ref, dgb_ref, cwa_ref, cwb_ref,
             dp_ref, dcwa_ref, dcwb_ref,
             dcabuf, du1buf, sgbuf, acca, accb):
        i = pl.program_id(0)
        keep_next = (i < nt - 1).astype(F32)

        @pl.when(i == 0)
        def _():
            acca[...] = jnp.zeros_like(acca)
            accb[...] = jnp.zeros_like(accb)

        sgbuf[...] = _sig(gb_ref[...].astype(F32))
        dcabuf[pl.ds(0, tm), :] = dca_ref[...]
        dcabuf[pl.ds(tm, HALO), :] = ndca_ref[...] * keep_next
        du1buf[pl.ds(0, tm), :] = du1_ref[...]
        du1buf[pl.ds(tm, HALO), :] = ndu1_ref[...] * keep_next
        dp_ref[:, pl.ds(0, d)] = dba_ref[...]
        dp_ref[:, pl.ds(5 * d, d)] = dga_ref[...]
        dp_ref[:, pl.ds(6 * d, d)] = dgb_ref[...]

        def chunk(j, carry):
            r0 = pl.multiple_of(j * CONV_ROWS, CONV_ROWS)
            rows = pl.ds(r0, CONV_ROWS)
            for lc in range(d // CONV_LANES):
                lo = lc * CONV_LANES
                ls = pl.ds(lo, CONV_LANES)
                cac = ca_ref[rows, ls].astype(F32)
                vac = va_ref[rows, ls].astype(F32)
                zc = cac * vac
                acc = jnp.zeros((CONV_ROWS, CONV_LANES), F32)
                for k, sl in _taps(dcabuf[pl.ds(r0, CONV_ROWS + HALO), ls], ANTI_A, CONV_ROWS):
                    acc = acc + cwa_ref[pl.ds(k, 1), ls] * sl
                    acca[pl.ds(8 * k, 8), ls] += _rowsum8(sl * zc)
                dp_ref[rows, pl.ds(d + lo, CONV_LANES)] = (acc * vac).astype(BF)
                dp_ref[rows, pl.ds(2 * d + lo, CONV_LANES)] = (acc * cac).astype(BF)
                sgc = sgbuf[rows, ls]
                vbc = vb_ref[rows, ls].astype(F32)
                uc = vbc * sgc
                acc = jnp.zeros((CONV_ROWS, CONV_LANES), F32)
                for k, sl in _taps(du1buf[pl.ds(r0, CONV_ROWS + HALO), ls], ANTI_B, CONV_ROWS):
                    acc = acc + cwb_ref[pl.ds(k, 1), ls] * sl
                    accb[pl.ds(8 * k, 8), ls] += _rowsum8(sl * uc)
                dp_ref[rows, pl.ds(3 * d + lo, CONV_LANES)] = (acc * sgc).astype(BF)
                dp_ref[rows, pl.ds(4 * d + lo, CONV_LANES)] = (acc * vbc * (sgc * (1.0 - sgc))).astype(BF)
            return carry

        lax.fori_loop(0, tm // CONV_ROWS, chunk, 0)

        @pl.when(i == nt - 1)
        def _():
            dcwa_ref[...] = jnp.zeros_like(dcwa_ref)
            dcwb_ref[...] = jnp.zeros_like(dcwb_ref)
            for k in range(K_A):
                dcwa_ref[pl.ds(k, 1), :] = jnp.sum(acca[pl.ds(8 * k, 8), :], axis=0, keepdims=True)
            for k in range(K_B):
                dcwb_ref[pl.ds(k, 1), :] = jnp.sum(accb[pl.ds(8 * k, 8), :], axis=0, keepdims=True)

    row = pl.BlockSpec((tm, d), lambda i: (i, 0))
    nxt = _next_halo_spec(tm, d, 0, s)
    return _host_call(
        body, name="conv_bwd", grid=(nt,),
        in_specs=[row, row, nxt, nxt] + [_col_spec(tm, d, k) for k in (1, 2, 3, 4)]
        + [row, row, row, _const((K_A, d)), _const((K_B, d))],
        out_specs=[pl.BlockSpec((tm, 7 * d), lambda i: (i, 0)), pl.BlockSpec((8, d), lambda i: (0, 0)),
                   pl.BlockSpec((32, d), lambda i: (0, 0))],
        out_shape=[SDS((s, 7 * d), BF), SDS((8, d), F32), SDS((32, d), F32)],
        scratch_shapes=[pltpu.VMEM((tm + HALO, d), F32), pltpu.VMEM((tm + HALO, d), F32),
                        pltpu.VMEM((tm, d), F32), pltpu.VMEM((8 * K_A, d), F32), pltpu.VMEM((8 * K_B, d), F32)],
        args=(dca, du1, dca, du1, p, p, p, p, dba, dga, dgb, cwa, cwb), hosted=hosted)


def _grad_w(a, b, name, hosted=None):
    s, m = a.shape
    n = b.shape[1]
    tm, tn, tk = _tile(m, 1024), _tile(n, 1024), _tile(s, 2048)
    nk = s // tk

    def body(a_ref, b_ref, o_ref, ob_ref):
        k = pl.program_id(2)

        @pl.when(k == 0)
        def _():
            o_ref[...] = jnp.zeros_like(o_ref)

        o_ref[...] += _dot_tn(a_ref[...], b_ref[...])

        @pl.when(k == nk - 1)
        def _():
            ob_ref[...] = o_ref[...].astype(BF)

    blk = pl.BlockSpec((tm, tn), lambda i, j, k: (i, j))
    (g, gb), extra = _host_call(
        body, name=name, grid=(m // tm, n // tn, nk),
        in_specs=[pl.BlockSpec((tk, tm), lambda i, j, k: (k, i)), pl.BlockSpec((tk, tn), lambda i, j, k: (k, j))],
        out_specs=[blk, blk], out_shape=[SDS((m, n), F32), SDS((m, n), BF)], scratch_shapes=[], args=(a, b),
        hosted=hosted)
    return g, gb, extra


def _grad_x(dr1, dp, wi, hosted=None):
    s, d = dr1.shape
    n = wi.shape[1]
    tm, tk = min(s, 512), _tile(n, 3584)

    def body(dr1_ref, dp_ref, w_ref, o_ref):
        @pl.when(pl.program_id(1) == 0)
        def _():
            o_ref[...] = F32(ALPHA) * dr1_ref[...]

        o_ref[...] += _dot_nt(dp_ref[...], w_ref[...])

    (gx,), extra = _host_call(
        body, name="grad_x", grid=(s // tm, n // tk),
        in_specs=[pl.BlockSpec((tm, d), lambda i, k: (i, 0)), pl.BlockSpec((tm, tk), lambda i, k: (i, k)),
                  pl.BlockSpec((d, tk), lambda i, k: (0, k))],
        out_specs=[pl.BlockSpec((tm, d), lambda i, k: (i, 0))],
        out_shape=[SDS((s, d), F32)], scratch_shapes=[], args=(dr1, dp, wi), hosted=hosted)
    return gx, extra


def _adamw_math(w, g, m, v):
    m2 = ADAM_B1 * m + (1.0 - ADAM_B1) * g
    v2 = ADAM_B2 * v + (1.0 - ADAM_B2) * (g * g)
    m_hat = m2 / (1.0 - ADAM_B1 ** ADAM_STEP)
    v_hat = v2 / (1.0 - ADAM_B2 ** ADAM_STEP)
    delta = -ADAM_LR * (m_hat / (jnp.sqrt(v_hat) + ADAM_EPS) + ADAM_WD * w)
    return delta, m2, v2


def _adamw(w, g, m, v, name):
    r, c = w.shape
    tr = r if r <= 256 else 256

    def body(w_ref, g_ref, m_ref, v_ref, d_ref, m2_ref, v2_ref):
        delta, m2, v2 = _adamw_math(w_ref[...], g_ref[...], m_ref[...], v_ref[...])
        d_ref[...] = delta
        m2_ref[...] = m2
        v2_ref[...] = v2

    blk = pl.BlockSpec((tr, c), lambda i: (i, 0))
    return pl.pallas_call(
        body, name=name, grid=(r // tr,), in_specs=[blk] * 4, out_specs=[blk] * 3,
        out_shape=[SDS((r, c), F32)] * 3, compiler_params=_cp(("parallel",)),
    )(w, g, m, v)


def _sum_parts(parts, name):
    k, r, c = parts.shape

    def body(p_ref, o_ref):
        acc = p_ref[0]
        for q in range(1, k):
            acc = acc + p_ref[q]
        o_ref[...] = acc

    return pl.pallas_call(
        body, name=name, grid=(1,),
        in_specs=[pl.BlockSpec((k, r, c), lambda i: (0, 0, 0))],
        out_specs=pl.BlockSpec((r, c), lambda i: (0, 0)),
        out_shape=SDS((r, c), F32), compiler_params=_cp(("arbitrary",)),
    )(parts)


def _piece_shape(full_shape, axis):
    r, c = full_shape
    return (r // 2, c // 4) if axis == 1 else (r // 8, c)


def _piece_spec(full_shape, axis, tr, chip_of, half_of):
    hr, wc = _piece_shape(full_shape, axis)
    nb = hr // tr
    if axis == 1:
        return pl.BlockSpec((tr, wc), lambda *a: (half_of(*a) * nb + a[-2], chip_of(*a)))
    return pl.BlockSpec((tr, wc), lambda *a: ((2 * chip_of(*a) + half_of(*a)) * nb + a[-2], 0))


def _place_cast(w, axis, pos, name):
    r, c = w.shape
    tr = min(r, 256)
    nb = r // tr
    full = (r, 4 * c) if axis == 1 else (4 * r, c)
    out_map = (lambda i, pos: (i, pos[0])) if axis == 1 else (lambda i, pos: (pos[0] * nb + i, 0))

    def body(pos_ref, w_ref, o_ref):
        o_ref[...] = w_ref[...].astype(o_ref.dtype)

    gs = pltpu.PrefetchScalarGridSpec(
        num_scalar_prefetch=1, grid=(nb,),
        in_specs=[pl.BlockSpec((tr, c), lambda i, pos: (i, 0))], out_specs=pl.BlockSpec((tr, c), out_map))
    return pl.pallas_call(body, name=name, grid_spec=gs, out_shape=SDS(full, BF),
                          compiler_params=_cp(("arbitrary",)))(pos, w)


def _pair_add(g, land, axis, pos, name):
    hr, wc = _piece_shape(g.shape, axis)
    tr = min(hr, 256)

    def body(pos_ref, g_ref, l_ref, o_ref):
        o_ref[0] = (g_ref[...] + l_ref[0].astype(F32)).astype(BF)

    other = lambda q, i, pos: (pos[0] + 1 + q) % 4
    blk = pl.BlockSpec((1, tr, wc), lambda q, i, pos: (other(q, i, pos), i, 0))
    gs = pltpu.PrefetchScalarGridSpec(
        num_scalar_prefetch=1, grid=(3, hr // tr),
        in_specs=[_piece_spec(g.shape, axis, tr, other, lambda q, i, pos: pos[1]), blk], out_specs=blk)
    return pl.pallas_call(body, name=name, grid_spec=gs, out_shape=SDS((4, hr, wc), BF),
                          compiler_params=_cp(("arbitrary", "arbitrary")))(pos, g, land)


def _chip_sum(g, land1, land2, axis, pos, name):
    hr, wc = _piece_shape(g.shape, axis)
    tr = min(hr, 256)
    nb = hr // tr

    def body(pos_ref, g_ref, l1_ref, l2_ref, o_ref):
        acc = g_ref[...] + l1_ref[0].astype(F32)
        for q in range(3):
            acc = acc + l2_ref[q].astype(F32)
        o_ref[...] = acc

    gs = pltpu.PrefetchScalarGridSpec(
        num_scalar_prefetch=1, grid=(nb,),
        in_specs=[_piece_spec(g.shape, axis, tr, lambda i, pos: pos[0], lambda i, pos: pos[1]),
                  pl.BlockSpec((1, tr, wc), lambda i, pos: (pos[0], i, 0)),
                  pl.BlockSpec((3, tr, wc), lambda i, pos: (0, i, 0))],
        out_specs=pl.BlockSpec((tr, wc), lambda i, pos: (pos[1] * nb + i, 0)))
    return pl.pallas_call(body, name=name, grid_spec=gs, out_shape=SDS((2 * hr, wc), F32),
                          compiler_params=_cp(("arbitrary",)))(pos, g, land1, land2)


ANY = pl.BlockSpec(memory_space=pl.ANY)
COMM = pltpu.CompilerParams(has_side_effects=True)


def _on_each_device(fn):
    x, y, c = lax.axis_index("x"), lax.axis_index("y"), lax.axis_index("c")
    for sx in (0, 1):
        for sy in (0, 1):
            for sc in (0, 1):
                @pl.when(jnp.logical_and(jnp.logical_and(x == sx, y == sy), c == sc))
                def _(sx=sx, sy=sy, sc=sc):
                    fn(sx, sy, sc)


def _remote(src, dst, send_sem, recv_sem, to):
    return pltpu.make_async_remote_copy(src_ref=src, dst_ref=dst, send_sem=send_sem, recv_sem=recv_sem,
                                        device_id=to, device_id_type=MESH)


def _piece_ref(ref, axis, j, h):
    r, c = ref.shape
    hr, wc = _piece_shape((r, c), axis)
    if axis == 1:
        return ref.at[pl.ds(h * hr, hr), pl.ds(j * wc, wc)]
    return ref.at[pl.ds((2 * j + h) * hr, hr), :]


class _Stage:
    def __init__(self, ins, outs, n_sems, make_plan, aliases=None):
        self.ins, self.outs, self.n_sems, self.make_plan = list(ins), list(outs), n_sems, make_plan
        self.aliases = dict(aliases or {})


def _start(plan):
    def dev(sx, sy, sc):
        for cp, _, _ in plan(sx, sy, sc):
            cp.start()

    _on_each_device(dev)


def _finish(plan):
    def dev(sx, sy, sc):
        for _, sent, got in plan(sx, sy, sc):
            sent.wait_send()
            got.wait_recv()

    _on_each_device(dev)


def _comm_call(stage, name):
    n_in, n_out = len(stage.ins), len(stage.outs)

    def body(*refs):
        plan = stage.make_plan(refs[:n_in], refs[n_in:n_in + n_out], *refs[n_in + n_out:])
        _start(plan)
        _finish(plan)

    return pl.pallas_call(
        body, name=name, in_specs=[ANY] * n_in, out_specs=[ANY] * n_out, out_shape=stage.outs,
        input_output_aliases=stage.aliases,
        scratch_shapes=[pltpu.SemaphoreType.DMA((stage.n_sems,)), pltpu.SemaphoreType.DMA((stage.n_sems,))],
        compiler_params=COMM,
    )(*stage.ins)


class _SemsFrom:
    def __init__(self, sems, base):
        self.sems, self.base = sems, base

    @property
    def at(self):
        return self

    def __getitem__(self, k):
        return self.sems.at[self.base + k]


def _both(a, b):
    na, nb = len(a.ins), len(b.ins)
    ma = len(a.outs)

    def make_plan(ins, outs, send_sems, recv_sems):
        pa = a.make_plan(ins[:na], outs[:ma], send_sems, recv_sems)
        pb = b.make_plan(ins[na:], outs[ma:], _SemsFrom(send_sems, a.n_sems), _SemsFrom(recv_sems, a.n_sems))
        return lambda sx, sy, sc: pa(sx, sy, sc) + pb(sx, sy, sc)

    aliases = dict(a.aliases)
    aliases.update({na + i: ma + o for i, o in b.aliases.items()})
    return _Stage(a.ins + b.ins, a.outs + b.outs, a.n_sems + b.n_sems, make_plan, aliases)


def _same(cp):
    return (cp, cp, cp)


def _stage_gather_send(fulls, axes):
    n = len(fulls)

    def make_plan(ins, outs, send_sems, recv_sems):
        def plan(sx, sy, sc):
            cps = []
            for w in range(n):
                mine = _piece_ref(outs[w], axes[w], 2 * sx + sy, sc)
                for r, (fx, fy) in enumerate(CHIP_RELS):
                    k = 3 * w + r
                    to = (sx ^ fx, sy ^ fy, sc)
                    got = _piece_ref(outs[w], axes[w], 2 * (sx ^ fx) + (sy ^ fy), sc)
                    send = _remote(mine, mine, send_sems.at[k], recv_sems.at[k], to)
                    cps.append((send, send, _remote(got, got, send_sems.at[k], recv_sems.at[k], to)))
            return cps

        return plan

    return _Stage(fulls, [SDS(f.shape, f.dtype) for f in fulls], 3 * n, make_plan, {i: i for i in range(n)})


def _stage_gather_forward(fulls, axes):
    n = len(fulls)

    def make_plan(ins, outs, send_sems, recv_sems):
        def plan(sx, sy, sc):
            cps = []
            sib = (sx, sy, 1 - sc)
            for w in range(n):
                for r, (fx, fy) in enumerate(CHIP_RELS):
                    k = 3 * w + r
                    pj = 2 * (sx ^ fx) + (sy ^ fy)
                    have = _piece_ref(outs[w], axes[w], pj, sc)
                    want = _piece_ref(outs[w], axes[w], pj, 1 - sc)
                    send = _remote(have, have, send_sems.at[k], recv_sems.at[k], sib)
                    cps.append((send, send, _remote(want, want, send_sems.at[k], recv_sems.at[k], sib)))
            return cps

        return plan

    return _Stage(fulls, [SDS(f.shape, f.dtype) for f in fulls], 3 * n, make_plan, {i: i for i in range(n)})


def _in_place(fulls, n_sems, make_plan):
    return _Stage(fulls, [SDS(f.shape, f.dtype) for f in fulls], n_sems, make_plan, {i: i for i in range(len(fulls))})


def _stage_gather_neighbours(fulls, axes):
    n = len(fulls)

    def make_plan(ins, outs, send_sems, recv_sems):
        def plan(sx, sy, sc):
            cps = []
            for w in range(n):
                mine = _piece_ref(outs[w], axes[w], 2 * sx + sy, sc)
                for r, (px, py) in enumerate(((sx ^ 1, sy), (sx, sy ^ 1))):
                    got = _piece_ref(outs[w], axes[w], 2 * px + py, sc)
                    send = _remote(mine, mine, send_sems.at[2 * w + r], recv_sems.at[2 * w + r], (px, py, sc))
                    cps.append((send, send, _remote(got, got, send_sems.at[2 * w + r], recv_sems.at[2 * w + r],
                                                    (px, py, sc))))
            return cps

        return plan

    return _in_place(fulls, 2 * n, make_plan)


def _stage_gather_pair(fulls, axes):
    n = len(fulls)

    def make_plan(ins, outs, send_sems, recv_sems):
        def plan(sx, sy, sc):
            cps = []
            sib = (sx, sy, 1 - sc)
            for w in range(n):
                for r, j in enumerate((2 * (sx ^ 1) + sy, 2 * sx + (sy ^ 1))):
                    k = 2 * w + r
                    have, want = _piece_ref(outs[w], axes[w], j, sc), _piece_ref(outs[w], axes[w], j, 1 - sc)
                    send = _remote(have, have, send_sems.at[k], recv_sems.at[k], sib)
                    cps.append((send, send, _remote(want, want, send_sems.at[k], recv_sems.at[k], sib)))
            return cps

        return plan

    return _in_place(fulls, 2 * n, make_plan)


def _stage_gather_diagonal(fulls, axes):
    n = len(fulls)

    def make_plan(ins, outs, send_sems, recv_sems):
        def relay(sx, sy, sc):
            cps = []
            jx, jy, jd = 2 * (sx ^ 1) + sy, 2 * sx + (sy ^ 1), 2 * (sx ^ 1) + (sy ^ 1)
            passed, to = (jx, (sx, sy ^ 1, sc)) if sc == 0 else (jy, (sx ^ 1, sy, sc))
            for w in range(n):
                have, want = _piece_ref(outs[w], axes[w], passed, sc), _piece_ref(outs[w], axes[w], jd, sc)
                send = _remote(have, have, send_sems.at[w], recv_sems.at[w], to)
                cps.append((send, send, _remote(want, want, send_sems.at[w], recv_sems.at[w], to)))
            return cps

        def cross(sx, sy, sc):
            cps = []
            sib = (sx, sy, 1 - sc)
            jd = 2 * (sx ^ 1) + (sy ^ 1)
            for w in range(n):
                have, want = _piece_ref(outs[w], axes[w], jd, sc), _piece_ref(outs[w], axes[w], jd, 1 - sc)
                send = _remote(have, have, send_sems.at[n + w], recv_sems.at[n + w], sib)
                cps.append((send, send, _remote(want, want, send_sems.at[n + w], recv_sems.at[n + w], sib)))
            return cps

        return [relay, cross]

    return _in_place(fulls, 2 * n, make_plan)


def _stage_pair_exchange(grads, axes):
    n = len(grads)

    def make_plan(gs, land, send_sems, recv_sems):
        def plan(sx, sy, sc):
            return [_same(_remote(_piece_ref(gs[w], axes[w], jj, 1 - sc), land[w].at[jj], send_sems.at[4 * w + jj],
                                  recv_sems.at[4 * w + jj], (sx, sy, 1 - sc)))
                    for w in range(n) for jj in range(4)]

        return plan

    return _Stage(grads, [SDS((4,) + _piece_shape(g.shape, a), g.dtype) for g, a in zip(grads, axes)], 4 * n,
                  make_plan)


def _stage_chip_scatter(pieces):
    n = len(pieces)

    def make_plan(ps, land, send_sems, recv_sems):
        def plan(sx, sy, sc):
            return [_same(_remote(ps[w].at[2 * (sx ^ fx) + (sy ^ fy)], land[w].at[r], send_sems.at[3 * w + r],
                                  recv_sems.at[3 * w + r], (sx ^ fx, sy ^ fy, sc)))
                    for w in range(n) for r, (fx, fy) in enumerate(CHIP_RELS)]

        return plan

    return _Stage(pieces, [SDS((3,) + p.shape[1:], p.dtype) for p in pieces], 3 * n, make_plan)


def _stage_pair_share(shards):
    n = len(shards)

    def make_plan(ins, outs, send_sems, recv_sems):
        def plan(sx, sy, sc):
            cps = []
            sib = (sx, sy, 1 - sc)
            for w in range(n):
                hr = shards[w].shape[0] // 2
                mine = outs[w].at[pl.ds(sc * hr, hr), :]
                theirs = outs[w].at[pl.ds((1 - sc) * hr, hr), :]
                send = _remote(mine, mine, send_sems.at[w], recv_sems.at[w], sib)
                cps.append((send, send, _remote(theirs, theirs, send_sems.at[w], recv_sems.at[w], sib)))
            return cps

        return plan

    return _Stage(shards, [SDS(g.shape, g.dtype) for g in shards], n, make_plan, {i: i for i in range(n)})


def _stage_gather_small(stack):
    def make_plan(ins, outs, send_sems, recv_sems):
        def plan(sx, sy, sc):
            mine = outs[0].at[4 * sx + 2 * sy + sc]
            return [_same(_remote(mine, mine, send_sems.at[k], recv_sems.at[k], (sx ^ fx, sy ^ fy, sc ^ fc)))
                    for k, (fx, fy, fc) in enumerate(DEV_RELS)]

        return plan

    return _Stage([stack], [SDS(stack.shape, stack.dtype)], 7, make_plan, {0: 0})


def kernel(x, w_in, conv_a_w, w_out_a, conv_b_w, conv_b_bias, ln_b_gamma, ln_b_beta, w_out_b, w_o, ln1_gamma, ln1_beta, w_up, w_down, ln2_gamma, ln2_beta, loss_target, m_w_in, m_conv_a_w, m_w_out_a, m_conv_b_w, m_conv_b_bias, m_ln_b_gamma, m_ln_b_beta, m_w_out_b, m_w_o, m_ln1_gamma, m_ln1_beta, m_w_up, m_w_down, m_ln2_gamma, m_ln2_beta, v_w_in, v_conv_a_w, v_w_out_a, v_conv_b_w, v_conv_b_bias, v_ln_b_gamma, v_ln_b_beta, v_w_out_b, v_w_o, v_ln1_gamma, v_ln1_beta, v_w_up, v_w_down, v_ln2_gamma, v_ln2_beta):
    s, d = x.shape[1], x.shape[2]
    xs = x.reshape(s, d)
    tgt = loss_target.reshape(s, d)
    dq = d // 4
    chip = 2 * lax.axis_index("x") + lax.axis_index("y")
    core = lax.axis_index("c")
    pos = jnp.stack([chip, core]).astype(jnp.int32)
    names = ("w_in", "w_out_a", "w_out_b", "w_o", "w_up", "w_down")
    axes = (1, 0, 0, 0, 1, 0)

    conv_pack = jnp.concatenate([jnp.pad(conv_a_w, ((0, 8 - K_A), (0, 0))), jnp.pad(conv_b_w, ((0, 32 - K_B), (0, 0))),
                                 jnp.zeros((8, dq), F32)], axis=0)
    conv_full = lax.dynamic_update_slice(jnp.zeros((conv_pack.shape[0], d), F32), conv_pack, (0, chip * dq))
    fulls = [_place_cast(w, a, pos, "place_" + nm)
             for w, a, nm in zip((w_in, w_out_a, w_out_b, w_o, w_up, w_down), axes, names)]
    vec = lambda a: a.reshape(1, d)
    bias_b, lbg, lbb = vec(conv_b_bias), vec(ln_b_gamma), vec(ln_b_beta)
    l1g, l1b, l2g, l2b = vec(ln1_gamma), vec(ln1_beta), vec(ln2_gamma), vec(ln2_beta)

    (p, xb), landed = _in_proj_own(xs, w_in, pos, _stage_gather_neighbours([fulls[0], conv_full], (1, 1)))
    wi, convs = _comm_call(_stage_gather_pair(landed, (1, 1)), "gather_pair_w_in")
    p, (wi, convs), small3 = _in_proj_rest(xb, wi, p, pos, _stage_gather_diagonal([wi, convs], (1, 1)),
                                           _stage_gather_send(fulls[1:4], axes[1:4]))
    cwa, cwb = convs[0:K_A], convs[8:8 + K_B]
    (conva, yapre, nb, rstdb, u3), landed = _conv_fwd(
        p, cwa, cwb, bias_b, lbg, lbb, d,
        _both(_stage_gather_forward(small3, axes[1:4]), _stage_gather_send(fulls[4:6], axes[4:6])))
    woa, wob, wo = landed[:3]
    (ya, yb, merged, n1, rstd1), (wup, wdown) = _mixer_out(yapre, u3, p, xs, woa, wob, wo, d,
                                                           _stage_gather_forward(landed[3:], axes[4:6]))
    x1b, hb, dhpre, dr2b, dr1, dr1b, acc_mlp = _mlp(n1, rstd1, tgt, wup, wdown, l1g, l1b, l2g, l2b)
    g_up, gb_up, _ = _grad_w(x1b, dhpre, "grad_w_up")
    g_down, gb_down, _ = _grad_w(hb, dr2b, "grad_w_down")
    dya, dyb, dba, dga, dgb, dca, du1, acc_mix = _mixer_bwd_local(dr1b, p, ya, yb, conva, nb, rstdb, wo, woa, wob,
                                                                   lbg, lbb)
    g_oa, gb_oa, _ = _grad_w(yapre, dya, "grad_w_out_a")
    g_ob, gb_ob, _ = _grad_w(u3, dyb, "grad_w_out_b")
    g_o, gb_o, _ = _grad_w(merged, dr1b, "grad_w_o")

    early, e_axes, e_names = [g_oa, g_ob, g_o, g_up, g_down], axes[1:], names[1:]
    (dp, dcwa, dcwb), land1 = _conv_bwd(dca, du1, p, dba, dga, dgb, cwa, cwb, d,
                                        _stage_pair_exchange([gb_oa, gb_ob, gb_o, gb_up, gb_down], e_axes))
    pieces = [_pair_add(g, l, a, pos, "pair_add_" + nm) for g, l, a, nm in zip(early, land1, e_axes, e_names)]
    pack = jnp.concatenate([dcwa, dcwb, acc_mix, acc_mlp], axis=0)
    stack = lax.dynamic_update_slice(jnp.zeros((8,) + pack.shape, F32), pack[None], (2 * chip + core, 0, 0))
    g_wi, gb_wi, landed = _grad_w(xb, dp, "grad_w_in",
                                  _both(_stage_chip_scatter(pieces), _stage_gather_small(stack)))
    land2, stack = landed[:-1], landed[-1]
    halves = [_chip_sum(g, l1, l2, a, pos, "chip_sum_" + nm)
              for g, l1, l2, a, nm in zip(early, land1, land2, e_axes, e_names)]
    (land1_in,) = _comm_call(_stage_pair_exchange([gb_wi], (1,)), "pair_exchange_w_in")
    piece_in = _pair_add(g_wi, land1_in, 1, pos, "pair_add_w_in")
    grad_x, landed = _grad_x(dr1, dp, wi, _both(_stage_chip_scatter([piece_in]), _stage_pair_share(halves)))
    land2_in, (g_oa, g_ob, g_o, g_up, g_down) = landed[0], landed[1:]
    half_in = _chip_sum(g_wi, land1_in, land2_in, 1, pos, "chip_sum_w_in")
    (g_in,) = _comm_call(_stage_pair_share([half_in]), "pair_share_w_in")
    small = _sum_parts(stack, "small_sum")
    g_ca = lax.dynamic_slice(small, (0, chip * dq), (K_A, dq))
    g_cb = lax.dynamic_slice(small, (8, chip * dq), (K_B, dq))
    g_vec = jnp.stack([small[r] for r in (42, 40, 41, 51, 52, 49, 50)])

    loss = (0.5 / d) * jnp.sum(small[48])

    big = {}
    for name, w, g, m, v in (("w_in", w_in, g_in, m_w_in, v_w_in), ("w_out_a", w_out_a, g_oa, m_w_out_a, v_w_out_a),
                             ("w_out_b", w_out_b, g_ob, m_w_out_b, v_w_out_b), ("w_o", w_o, g_o, m_w_o, v_w_o),
                             ("w_up", w_up, g_up, m_w_up, v_w_up), ("w_down", w_down, g_down, m_w_down, v_w_down),
                             ("conv_a_w", conv_a_w, g_ca, m_conv_a_w, v_conv_a_w),
                             ("conv_b_w", conv_b_w, g_cb, m_conv_b_w, v_conv_b_w)):
        big[name] = (g,) + tuple(_adamw(w, g, m, v, "adamw_" + name))
    vec_names = ("conv_b_bias", "ln_b_gamma", "ln_b_beta", "ln1_gamma", "ln1_beta", "ln2_gamma", "ln2_beta")
    w7 = jnp.stack([conv_b_bias, ln_b_gamma, ln_b_beta, ln1_gamma, ln1_beta, ln2_gamma, ln2_beta])
    m7 = jnp.stack([m_conv_b_bias, m_ln_b_gamma, m_ln_b_beta, m_ln1_gamma, m_ln1_beta, m_ln2_gamma, m_ln2_beta])
    v7 = jnp.stack([v_conv_b_bias, v_ln_b_gamma, v_ln_b_beta, v_ln1_gamma, v_ln1_beta, v_ln2_gamma, v_ln2_beta])
    d7, nm7, nv7 = _adamw(w7, g_vec, m7, v7, "adamw_vectors")
    for q, name in enumerate(vec_names):
        big[name] = (g_vec[q], d7[q], nm7[q], nv7[q])

    order = ("w_in", "conv_a_w", "w_out_a", "conv_b_w", "conv_b_bias", "ln_b_gamma", "ln_b_beta", "w_out_b", "w_o",
             "ln1_gamma", "ln1_beta", "w_up", "w_down", "ln2_gamma", "ln2_beta")
    outs = [loss, grad_x.reshape(x.shape)]
    for part in range(4):
        outs += [big[name][part] for name in order]
    return tuple(outs)
```

```python
import jax
import jax.numpy as jnp
from jax import lax
from jax.experimental import pallas as pl
from jax.experimental.pallas import tpu as pltpu

F32 = jnp.float32
BF = jnp.bfloat16
SDS = jax.ShapeDtypeStruct
MESH = pl.DeviceIdType.MESH

ALPHA = 2.0 ** 0.25
LN_EPS = 1e-5
K_A = 3
K_B = 31
HALO = 32
CONV_TILE = 512
CONV_ROWS = 64
CONV_LANES = 128
ADAM_LR = 0.001
ADAM_B1 = 0.9
ADAM_B2 = 0.999
ADAM_EPS = 1e-08
ADAM_WD = 0.01
ADAM_STEP = 10
P_DT = BF
CHIP_RELS = ((1, 0), (0, 1), (1, 1))
DEV_RELS = tuple((fx, fy, fc) for fx in (0, 1) for fy in (0, 1) for fc in (0, 1))[1:]


def _cp(sem=None, vmem_mb=56, side_effects=False):
    return pltpu.CompilerParams(dimension_semantics=sem, vmem_limit_bytes=vmem_mb << 20,
                                has_side_effects=side_effects)


def _const(shape):
    return pl.BlockSpec(shape, lambda *_: (0,) * len(shape), pipeline_mode=pl.Buffered(1))


def _sig(v):
    return jax.nn.sigmoid(v)


def _ln_fwd(r):
    mu = jnp.mean(r, axis=-1, keepdims=True)
    xc = r - mu
    var = jnp.mean(xc * xc, axis=-1, keepdims=True)
    rstd = lax.rsqrt(var + LN_EPS)
    return xc * rstd, rstd


def _ln_bwd(dn, n, rstd):
    m1 = jnp.mean(dn, axis=-1, keepdims=True)
    m2 = jnp.mean(dn * n, axis=-1, keepdims=True)
    return rstd * (dn - m1 - n * m2)


def _dot(a, b):
    return jnp.dot(a, b, preferred_element_type=F32)


def _dot_nt(a, b):
    return lax.dot_general(a, b, (((1,), (1,)), ((), ())), preferred_element_type=F32)


def _dot_tn(a, b):
    return lax.dot_general(a, b, (((0,), (0,)), ((), ())), preferred_element_type=F32)


def _tile(n, pref):
    if n <= pref:
        return n
    return max(t for t in range(128, pref + 1, 128) if n % t == 0)


def _rowsum8(v):
    acc = v[0:8]
    for g in range(1, v.shape[0] // 8):
        acc = acc + v[8 * g:8 * g + 8]
    return acc


def _taps(win, offsets, rows):
    r_all = win.shape[0]
    by_res = {}
    for k, o in enumerate(offsets):
        by_res.setdefault(o % 8, []).append((k, o // 8))
    for s, taps in sorted(by_res.items()):
        r = win if s == 0 else pltpu.roll(win, r_all - s, 0)
        for k, q in taps:
            yield k, r[8 * q:8 * q + rows]


CAUSAL_A = [HALO - (K_A - 1) + k for k in range(K_A)]
CAUSAL_B = [HALO - (K_B - 1) + k for k in range(K_B)]
ANTI_A = [K_A - 1 - k for k in range(K_A)]
ANTI_B = [K_B - 1 - k for k in range(K_B)]


def _host_call(body, *, name, grid, in_specs, out_specs, out_shape, scratch_shapes, args, hosted, prefetch=None,
               aliases=None, body_gets_stage_refs=False):
    n_in, n_out, n_scr = len(in_specs), len(out_specs), len(scratch_shapes)
    n_steps = 1
    for size in grid:
        n_steps *= size
    if isinstance(hosted, _Stage):
        hosted = [(hosted, 0, n_steps - 1)]
    n_pre = 0 if prefetch is None else 1
    pre = () if prefetch is None else (prefetch,)
    sem = ("arbitrary",) * len(grid)
    own_aliases = {n_pre + a: b for a, b in (aliases or {}).items()}

    def call(kernel_body, ins, outs, shapes, scratch, all_aliases, side_effects, operands):
        gs = pltpu.PrefetchScalarGridSpec(num_scalar_prefetch=n_pre, grid=grid, in_specs=ins, out_specs=outs,
                                          scratch_shapes=scratch)
        return pl.pallas_call(kernel_body, name=name, grid_spec=gs, out_shape=shapes,
                              input_output_aliases=all_aliases,
                              compiler_params=_cp(sem, side_effects=side_effects))(*pre, *operands)

    if hosted is None:
        res = call(body, list(in_specs), list(out_specs), list(out_shape), list(scratch_shapes), own_aliases, False,
                   args)
        return list(res), []
    stages = [st for st, _, _ in hosted]
    h_ins = [a for st in stages for a in st.ins]
    h_outs = [o for st in stages for o in st.outs]
    borrowed = {hi: k for hi, a in enumerate(h_ins) for k, own in enumerate(args) if a is own}
    passed = [hi for hi in range(len(h_ins)) if hi not in borrowed]
    h_in, h_out, n_sems = len(passed), len(h_outs), sum(st.n_sems for st in stages)

    def full_body(*refs):
        pre_refs, refs = refs[:n_pre], refs[n_pre:]
        ins, refs = refs[:n_in], refs[n_in:]
        hins = [None] * len(h_ins)
        for hi, ref in zip(passed, refs[:h_in]):
            hins[hi] = ref
        refs = refs[h_in:]
        outs, refs = refs[:n_out], refs[n_out:]
        houts, refs = refs[:h_out], refs[h_out:]
        scr, (send_sems, recv_sems) = refs[:n_scr], refs[n_scr:]
        step = 0
        for a, size in enumerate(grid):
            step = step * size + pl.program_id(a)
        phases, i0, o0, k0 = [], 0, 0, 0
        for st, starts, finishes in hosted:
            plans = st.make_plan(hins[i0:i0 + len(st.ins)], houts[o0:o0 + len(st.outs)],
                                 _SemsFrom(send_sems, k0), _SemsFrom(recv_sems, k0))
            if not isinstance(plans, list):
                plans, starts, finishes = [plans], [starts], [finishes]
            phases += list(zip(plans, starts, finishes))
            i0, o0, k0 = i0 + len(st.ins), o0 + len(st.outs), k0 + st.n_sems
        for plan, at, _ in phases:
            @pl.when(step == at)
            def _(plan=plan):
                _start(plan)

        if body_gets_stage_refs:
            body(*pre_refs, *ins, *outs, *scr, houts)
        else:
            body(*pre_refs, *ins, *outs, *scr)

        for plan, _, at in phases:
            @pl.when(step == at)
            def _(plan=plan):
                _finish(plan)

    all_aliases = dict(own_aliases)
    i0 = o0 = 0
    for st in stages:
        for a, b in st.aliases.items():
            hi = i0 + a
            operand = borrowed[hi] if hi in borrowed else n_in + passed.index(hi)
            all_aliases[n_pre + operand] = n_out + o0 + b
        i0, o0 = i0 + len(st.ins), o0 + len(st.outs)
    res = call(full_body, list(in_specs) + [ANY] * h_in, list(out_specs) + [ANY] * h_out,
               list(out_shape) + h_outs,
               list(scratch_shapes) + [pltpu.SemaphoreType.DMA((n_sems,)), pltpu.SemaphoreType.DMA((n_sems,))],
               all_aliases, True, (*args, *[h_ins[hi] for hi in passed]))
    return list(res[:n_out]), list(res[n_out:])


def _in_proj_own(x, w_shard, pos, hosted):
    s, d = x.shape
    tn = w_shard.shape[1]
    tm = min(s, 1024)

    def body(pos_ref, x_ref, w_ref, p_ref, xb_ref, wb):
        @pl.when(pl.program_id(0) == 0)
        def _():
            wb[...] = w_ref[...].astype(BF)

        xb = x_ref[...].astype(BF)
        xb_ref[...] = xb
        p_ref[...] = _dot(xb, wb[...]).astype(p_ref.dtype)

    return _host_call(
        body, name="in_proj_own", grid=(s // tm,), prefetch=pos,
        in_specs=[pl.BlockSpec((tm, d), lambda i, pos: (i, 0)),
                  pl.BlockSpec((d, tn), lambda i, pos: (0, 0), pipeline_mode=pl.Buffered(1))],
        out_specs=[pl.BlockSpec((tm, tn), lambda i, pos: (i, pos[0])), pl.BlockSpec((tm, d), lambda i, pos: (i, 0))],
        out_shape=[SDS((s, 4 * tn), P_DT), SDS((s, d), BF)], scratch_shapes=[pltpu.VMEM((d, tn), BF)],
        args=(x, w_shard), hosted=hosted)


def _in_proj_rest(xb, wi, p, pos, diagonal, other):
    s, d = xb.shape
    n = wi.shape[1]
    tm, tn = min(s // 4, 1024), n // 4
    ni = s // tm

    def body(pos_ref, xb_ref, w_ref, p_in, p_ref, wdiag, dsem, stage_outs):
        j = pl.program_id(0)
        step = j * ni + pl.program_id(1)

        def diagonal_block(act):
            for chip in range(4):
                @pl.when(pos_ref[0] == chip)
                def _(chip=chip):
                    act(pltpu.make_async_copy(stage_outs[0].at[:, pl.ds((chip ^ 3) * tn, tn)], wdiag, dsem))

        @pl.when(step == 2 * ni - 1)
        def _():
            diagonal_block(lambda cp: cp.start())

        @pl.when(step == 2 * ni)
        def _():
            diagonal_block(lambda cp: cp.wait())

        @pl.when(j < 2)
        def _():
            p_ref[...] = _dot(xb_ref[...], w_ref[...]).astype(p_ref.dtype)

        @pl.when(j == 2)
        def _():
            p_ref[...] = _dot(xb_ref[...], wdiag[...]).astype(p_ref.dtype)

    def col(j, i, pos):
        return lax.bitwise_xor(pos[0], jnp.where(j == 0, 2, jnp.where(j == 1, 1, 3)))

    def piped_col(j, i, pos):
        return lax.bitwise_xor(pos[0], jnp.where(j == 0, 2, 1))

    (p,), extra = _host_call(
        body, name="in_proj_rest", grid=(3, ni), prefetch=pos, aliases={2: 0}, body_gets_stage_refs=True,
        in_specs=[pl.BlockSpec((tm, d), lambda j, i, pos: (i, 0)),
                  pl.BlockSpec((d, tn), lambda j, i, pos: (0, piped_col(j, i, pos))), ANY],
        out_specs=[pl.BlockSpec((tm, tn), lambda j, i, pos: (i, col(j, i, pos)))],
        out_shape=[SDS((s, n), P_DT)], scratch_shapes=[pltpu.VMEM((d, tn), BF), pltpu.SemaphoreType.DMA(())],
        args=(xb, wi, p),
        hosted=[(diagonal, [0, ni + 1], [ni, 2 * ni - 2]), (other, 0, 3 * ni - 1)])
    n_d = len(diagonal.outs)
    return p, extra[:n_d], extra[n_d:]


def _col_spec(tm, d, k):
    return pl.BlockSpec((tm, d), lambda i, k=k: (i, k))


def _prev_halo_spec(tm, d, k):
    r = tm // HALO
    return pl.BlockSpec((HALO, d), lambda i, k=k: (jnp.maximum(i * r - 1, 0), k))


def _next_halo_spec(tm, d, k, s):
    r = tm // HALO
    last = s // HALO - 1
    return pl.BlockSpec((HALO, d), lambda i, k=k: (jnp.minimum((i + 1) * r, last), k))


def _conv_fwd(p, cwa, cwb, bias_b, lbg, lbb, d, hosted=None):
    s = p.shape[0]
    tm = min(s, CONV_TILE)
    nt = s // tm

    def body(ba_ref, ca_ref, va_ref, vb_ref, gb_ref, hca_ref, hva_ref, hvb_ref, hgb_ref,
             cwa_ref, cwb_ref, bias_ref, lbg_ref, lbb_ref,
             conva_ref, yapre_ref, nb_ref, rstdb_ref, u3_ref,
             zbuf, ubuf, u1buf):
        i = pl.program_id(0)
        keep = (i > 0).astype(F32)
        zbuf[pl.ds(0, HALO), :] = hca_ref[...].astype(F32) * hva_ref[...].astype(F32) * keep
        ubuf[pl.ds(0, HALO), :] = hvb_ref[...].astype(F32) * _sig(hgb_ref[...].astype(F32)) * keep
        zbuf[pl.ds(HALO, tm), :] = ca_ref[...].astype(F32) * va_ref[...].astype(F32)
        ubuf[pl.ds(HALO, tm), :] = vb_ref[...].astype(F32) * _sig(gb_ref[...].astype(F32))

        def chunk(j, carry):
            r0 = pl.multiple_of(j * CONV_ROWS, CONV_ROWS)
            rows = pl.ds(r0, CONV_ROWS)
            for lc in range(d // CONV_LANES):
                ls = pl.ds(lc * CONV_LANES, CONV_LANES)
                acc = jnp.zeros((CONV_ROWS, CONV_LANES), F32)
                for k, sl in _taps(zbuf[pl.ds(r0, CONV_ROWS + HALO), ls], CAUSAL_A, CONV_ROWS):
                    acc = acc + cwa_ref[pl.ds(k, 1), ls] * sl
                conva_ref[rows, ls] = acc.astype(BF)
                yapre_ref[rows, ls] = (ba_ref[rows, ls].astype(F32) * acc).astype(BF)
                acc = jnp.zeros((CONV_ROWS, CONV_LANES), F32)
                for k, sl in _taps(ubuf[pl.ds(r0, CONV_ROWS + HALO), ls], CAUSAL_B, CONV_ROWS):
                    acc = acc + cwb_ref[pl.ds(k, 1), ls] * sl
                u1buf[rows, ls] = acc + bias_ref[:, ls]
            return carry

        lax.fori_loop(0, tm // CONV_ROWS, chunk, 0)
        nb, rstd = _ln_fwd(u1buf[...])
        nb_ref[...] = nb
        rstdb_ref[...] = rstd
        u2 = nb * lbg_ref[...] + lbb_ref[...]
        u3_ref[...] = (u2 * _sig(u2)).astype(BF)

    vec = _const((1, d))
    return _host_call(
        body, name="conv_fwd", grid=(nt,),
        in_specs=[_col_spec(tm, d, k) for k in range(5)] + [_prev_halo_spec(tm, d, k) for k in (1, 2, 3, 4)]
        + [_const((K_A, d)), _const((K_B, d)), vec, vec, vec],
        out_specs=[pl.BlockSpec((tm, d), lambda i: (i, 0)), pl.BlockSpec((tm, d), lambda i: (i, 0)),
                   pl.BlockSpec((tm, d), lambda i: (i, 0)), pl.BlockSpec((tm, 1), lambda i: (i, 0)),
                   pl.BlockSpec((tm, d), lambda i: (i, 0))],
        out_shape=[SDS((s, d), BF), SDS((s, d), BF), SDS((s, d), F32), SDS((s, 1), F32), SDS((s, d), BF)],
        scratch_shapes=[pltpu.VMEM((HALO + tm, d), F32), pltpu.VMEM((HALO + tm, d), F32), pltpu.VMEM((tm, d), F32)],
        args=(p, p, p, p, p, p, p, p, p, cwa, cwb, bias_b, lbg, lbb), hosted=hosted)


def _mixer_out(yapre, u3, p, x, woa, wob, wo, d, hosted=None):
    s = x.shape[0]
    tm = min(s, 512)

    def body(yapre_ref, u3_ref, ga_ref, gb_ref, x_ref, woa_ref, wob_ref, wo_ref,
             ya_ref, yb_ref, merged_ref, n1_ref, rstd1_ref):
        ya = _dot(yapre_ref[...], woa_ref[...])
        yb = _dot(u3_ref[...], wob_ref[...])
        ya_ref[...] = ya.astype(BF)
        yb_ref[...] = yb.astype(BF)
        merged = (_sig(ga_ref[...].astype(F32)) * ya + _sig(gb_ref[...].astype(F32)) * yb).astype(BF)
        merged_ref[...] = merged
        r1 = F32(ALPHA) * x_ref[...] + _dot(merged, wo_ref[...])
        n1, rstd1 = _ln_fwd(r1)
        n1_ref[...] = n1
        rstd1_ref[...] = rstd1

    row = pl.BlockSpec((tm, d), lambda i: (i, 0))
    return _host_call(
        body, name="mixer_out", grid=(s // tm,),
        in_specs=[row, row, _col_spec(tm, d, 5), _col_spec(tm, d, 6), row,
                  _const((d, d)), _const((d, d)), _const((d, d))],
        out_specs=[row, row, row, row, pl.BlockSpec((tm, 1), lambda i: (i, 0))],
        out_shape=[SDS((s, d), BF), SDS((s, d), BF), SDS((s, d), BF), SDS((s, d), F32), SDS((s, 1), F32)],
        scratch_shapes=[], args=(yapre, u3, p, p, x, woa, wob, wo), hosted=hosted)


def _mlp(n1, rstd1, tgt, wup, wdown, l1g, l1b, l2g, l2b):
    s, d = n1.shape
    dff = wup.shape[1]
    tm = min(s, 256)
    fc = min(dff, 1024)
    nq = dff // fc

    def body(n1_ref, rstd1_ref, tgt_ref, wup_ref, wdown_ref, l1g_ref, l1b_ref, l2g_ref, l2b_ref,
             x1b_ref, hb_ref, dhpre_ref, dr2b_ref, dr1_ref, dr1b_ref, acc_ref, rbuf):
        i = pl.program_id(0)
        n1v = n1_ref[...]
        x1 = n1v * l1g_ref[...] + l1b_ref[...]
        x1b = x1.astype(BF)
        x1b_ref[...] = x1b
        ff = jnp.zeros((tm, d), F32)
        for q in range(nq):
            cs = pl.ds(q * fc, fc)
            r = jnp.maximum(_dot(x1b, wup_ref[:, cs]), 0.0)
            rbuf[:, cs] = r
            hq = (r * r).astype(BF)
            hb_ref[:, cs] = hq
            ff = ff + _dot(hq, wdown_ref[cs, :])
        n2, rstd2 = _ln_fwd(F32(ALPHA) * x1 + ff)
        x2 = n2 * l2g_ref[...] + l2b_ref[...]
        err = x2 - tgt_ref[...]
        dx2 = err * F32(1.0 / d)
        dr2 = _ln_bwd(dx2 * l2g_ref[...], n2, rstd2)
        dr2b = dr2.astype(BF)
        dr2b_ref[...] = dr2b
        dx1 = F32(ALPHA) * dr2
        for q in range(nq):
            cs = pl.ds(q * fc, fc)
            dh = _dot_nt(dr2b, wdown_ref[cs, :])
            dhp = (dh * (2.0 * rbuf[:, cs])).astype(BF)
            dhpre_ref[:, cs] = dhp
            dx1 = dx1 + _dot_nt(dhp, wup_ref[:, cs])
        dr1 = _ln_bwd(dx1 * l1g_ref[...], n1v, rstd1_ref[...])
        dr1_ref[...] = dr1
        dr1b_ref[...] = dr1.astype(BF)

        @pl.when(i == 0)
        def _():
            acc_ref[...] = jnp.zeros_like(acc_ref)

        for q, val in enumerate((err * err, dx2 * n2, dx2, dx1 * n1v, dx1)):
            acc_ref[pl.ds(q, 1), :] += jnp.sum(val, axis=0, keepdims=True)

    row = pl.BlockSpec((tm, d), lambda i: (i, 0))
    wide = pl.BlockSpec((tm, dff), lambda i: (i, 0))
    vec = _const((1, d))
    return pl.pallas_call(
        body, name="mlp_fwd_bwd", grid=(s // tm,),
        in_specs=[row, pl.BlockSpec((tm, 1), lambda i: (i, 0)), row, _const((d, dff)), _const((dff, d)),
                  vec, vec, vec, vec],
        out_specs=[row, wide, wide, row, row, row, pl.BlockSpec((8, d), lambda i: (0, 0))],
        out_shape=[SDS((s, d), BF), SDS((s, dff), BF), SDS((s, dff), BF), SDS((s, d), BF), SDS((s, d), F32),
                   SDS((s, d), BF), SDS((8, d), F32)],
        scratch_shapes=[pltpu.VMEM((tm, dff), F32)],
        compiler_params=_cp(("arbitrary",)),
    )(n1, rstd1, tgt, wup, wdown, l1g, l1b, l2g, l2b)


def _mixer_bwd_local(dr1b, p, ya, yb, conva, nb, rstdb, wo, woa, wob, lbg, lbb):
    s, d = ya.shape
    tm = min(s, 256)

    def body(dr1b_ref, ba_ref, ga_ref, gb_ref, ya_ref, yb_ref, conva_ref, nb_ref, rstdb_ref,
             wo_ref, woa_ref, wob_ref, lbg_ref, lbb_ref,
             dya_ref, dyb_ref, dba_ref, dga_ref, dgb_ref, dca_ref, du1_ref, acc_ref):
        i = pl.program_id(0)
        dmerged = _dot_nt(dr1b_ref[...], wo_ref[...])
        sa = _sig(ga_ref[...].astype(F32))
        sb = _sig(gb_ref[...].astype(F32))
        dya = (dmerged * sa).astype(BF)
        dyb = (dmerged * sb).astype(BF)
        dya_ref[...] = dya
        dyb_ref[...] = dyb
        dga_ref[...] = (dmerged * ya_ref[...].astype(F32) * (sa * (1.0 - sa))).astype(BF)
        dgb_ref[...] = (dmerged * yb_ref[...].astype(F32) * (sb * (1.0 - sb))).astype(BF)
        dyapre = _dot_nt(dya, woa_ref[...])
        dba_ref[...] = (dyapre * conva_ref[...].astype(F32)).astype(BF)
        dca_ref[...] = (dyapre * ba_ref[...].astype(F32)).astype(BF)
        du3 = _dot_nt(dyb, wob_ref[...])
        nbv = nb_ref[...]
        u2 = nbv * lbg_ref[...] + lbb_ref[...]
        sg = _sig(u2)
        du2 = du3 * (sg * (1.0 + u2 * (1.0 - sg)))
        du1 = _ln_bwd(du2 * lbg_ref[...], nbv, rstdb_ref[...])
        du1_ref[...] = du1.astype(BF)

        @pl.when(i == 0)
        def _():
            acc_ref[...] = jnp.zeros_like(acc_ref)

        for q, val in enumerate((du2 * nbv, du2, du1)):
            acc_ref[pl.ds(q, 1), :] += jnp.sum(val, axis=0, keepdims=True)

    row = pl.BlockSpec((tm, d), lambda i: (i, 0))
    vec = _const((1, d))
    return pl.pallas_call(
        body, name="mixer_bwd_local", grid=(s // tm,),
        in_specs=[row, _col_spec(tm, d, 0), _col_spec(tm, d, 5), _col_spec(tm, d, 6), row, row, row, row,
                  pl.BlockSpec((tm, 1), lambda i: (i, 0)), _const((d, d)), _const((d, d)), _const((d, d)), vec, vec],
        out_specs=[row, row, row, row, row, row, row, pl.BlockSpec((8, d), lambda i: (0, 0))],
        out_shape=[SDS((s, d), BF)] * 7 + [SDS((8, d), F32)],
        compiler_params=_cp(("arbitrary",)),
    )(dr1b, p, p, p, ya, yb, conva, nb, rstdb, wo, woa, wob, lbg, lbb)


def _conv_bwd(dca, du1, p, dba, dga, dgb, cwa, cwb, d, hosted=None):
    s = dca.shape[0]
    tm = min(s, CONV_TILE)
    nt = s // tm

    def body(dca_ref, du1_ref, ndca_ref, ndu1_ref, ca_ref, va_ref, vb_ref, gb_ref,
             dba_ref, dga_ref, dgb_ref, cwa_ref, cwb_ref,
             dp_ref, dcwa_ref, dcwb_ref,
             dcabuf, du1buf, sgbuf, acca, accb):
        i = pl.program_id(0)
        keep_next = (i < nt - 1).astype(F32)

        @pl.when(i == 0)
        def _():
            acca[...] = jnp.zeros_like(acca)
            accb[...] = jnp.zeros_like(accb)

        sgbuf[...] = _sig(gb_ref[...].astype(F32))
        dcabuf[pl.ds(0, tm), :] = dca_ref[...].astype(F32)
        dcabuf[pl.ds(tm, HALO), :] = ndca_ref[...].astype(F32) * keep_next
        du1buf[pl.ds(0, tm), :] = du1_ref[...].astype(F32)
        du1buf[pl.ds(tm, HALO), :] = ndu1_ref[...].astype(F32) * keep_next
        dp_ref[:, pl.ds(0, d)] = dba_ref[...]
        dp_ref[:, pl.ds(5 * d, d)] = dga_ref[...]
        dp_ref[:, pl.ds(6 * d, d)] = dgb_ref[...]

        def chunk(j, carry):
            r0 = pl.multiple_of(j * CONV_ROWS, CONV_ROWS)
            rows = pl.ds(r0, CONV_ROWS)
            for lc in range(d // CONV_LANES):
                lo = lc * CONV_LANES
                ls = pl.ds(lo, CONV_LANES)
                cac = ca_ref[rows, ls].astype(F32)
                vac = va_ref[rows, ls].astype(F32)
                zc = cac * vac
                acc = jnp.zeros((CONV_ROWS, CONV_LANES), F32)
                for k, sl in _taps(dcabuf[pl.ds(r0, CONV_ROWS + HALO), ls], ANTI_A, CONV_ROWS):
                    acc = acc + cwa_ref[pl.ds(k, 1), ls] * sl
                    acca[pl.ds(8 * k, 8), ls] += _rowsum8(sl * zc)
                dp_ref[rows, pl.ds(d + lo, CONV_LANES)] = (acc * vac).astype(BF)
                dp_ref[rows, pl.ds(2 * d + lo, CONV_LANES)] = (acc * cac).astype(BF)
                sgc = sgbuf[rows, ls]
                vbc = vb_ref[rows, ls].astype(F32)
                uc = vbc * sgc
                acc = jnp.zeros((CONV_ROWS, CONV_LANES), F32)
                for k, sl in _taps(du1buf[pl.ds(r0, CONV_ROWS + HALO), ls], ANTI_B, CONV_ROWS):
                    acc = acc + cwb_ref[pl.ds(k, 1), ls] * sl
                    accb[pl.ds(8 * k, 8), ls] += _rowsum8(sl * uc)
                dp_ref[rows, pl.ds(3 * d + lo, CONV_LANES)] = (acc * sgc).astype(BF)
                dp_ref[rows, pl.ds(4 * d + lo, CONV_LANES)] = (acc * vbc * (sgc * (1.0 - sgc))).astype(BF)
            return carry

        lax.fori_loop(0, tm // CONV_ROWS, chunk, 0)

        @pl.when(i == nt - 1)
        def _():
            dcwa_ref[...] = jnp.zeros_like(dcwa_ref)
            dcwb_ref[...] = jnp.zeros_like(dcwb_ref)
            for k in range(K_A):
                dcwa_ref[pl.ds(k, 1), :] = jnp.sum(acca[pl.ds(8 * k, 8), :], axis=0, keepdims=True)
            for k in range(K_B):
                dcwb_ref[pl.ds(k, 1), :] = jnp.sum(accb[pl.ds(8 * k, 8), :], axis=0, keepdims=True)

    row = pl.BlockSpec((tm, d), lambda i: (i, 0))
    nxt = _next_halo_spec(tm, d, 0, s)
    return _host_call(
        body, name="conv_bwd", grid=(nt,),
        in_specs=[row, row, nxt, nxt] + [_col_spec(tm, d, k) for k in (1, 2, 3, 4)]
        + [row, row, row, _const((K_A, d)), _const((K_B, d))],
        out_specs=[pl.BlockSpec((tm, 7 * d), lambda i: (i, 0)), pl.BlockSpec((8, d), lambda i: (0, 0)),
                   pl.BlockSpec((32, d), lambda i: (0, 0))],
        out_shape=[SDS((s, 7 * d), BF), SDS((8, d), F32), SDS((32, d), F32)],
        scratch_shapes=[pltpu.VMEM((tm + HALO, d), F32), pltpu.VMEM((tm + HALO, d), F32),
                        pltpu.VMEM((tm, d), F32), pltpu.VMEM((8 * K_A, d), F32), pltpu.VMEM((8 * K_B, d), F32)],
        args=(dca, du1, dca, du1, p, p, p, p, dba, dga, dgb, cwa, cwb), hosted=hosted)


def _grad_w(a, b, name, hosted=None):
    s, m = a.shape
    n = b.shape[1]
    tm, tn, tk = _tile(m, 1024), _tile(n, 1024), _tile(s, 2048)
    nk = s // tk

    def body(a_ref, b_ref, o_ref, ob_ref):
        k = pl.program_id(2)

        @pl.when(k == 0)
        def _():
            o_ref[...] = jnp.zeros_like(o_ref)

        o_ref[...] += _dot_tn(a_ref[...], b_ref[...])

        @pl.when(k == nk - 1)
        def _():
            ob_ref[...] = o_ref[...].astype(BF)

    blk = pl.BlockSpec((tm, tn), lambda i, j, k: (i, j))
    (g, gb), extra = _host_call(
        body, name=name, grid=(m // tm, n // tn, nk),
        in_specs=[pl.BlockSpec((tk, tm), lambda i, j, k: (k, i)), pl.BlockSpec((tk, tn), lambda i, j, k: (k, j))],
        out_specs=[blk, blk], out_shape=[SDS((m, n), F32), SDS((m, n), BF)], scratch_shapes=[], args=(a, b),
        hosted=hosted)
    return g, gb, extra


def _grad_x(dr1, dp, wi, hosted=None):
    s, d = dr1.shape
    n = wi.shape[1]
    tm, tk = min(s, 512), _tile(n, 3584)

    def body(dr1_ref, dp_ref, w_ref, o_ref):
        @pl.when(pl.program_id(1) == 0)
        def _():
            o_ref[...] = F32(ALPHA) * dr1_ref[...]

        o_ref[...] += _dot_nt(dp_ref[...], w_ref[...])

    (gx,), extra = _host_call(
        body, name="grad_x", grid=(s // tm, n // tk),
        in_specs=[pl.BlockSpec((tm, d), lambda i, k: (i, 0)), pl.BlockSpec((tm, tk), lambda i, k: (i, k)),
                  pl.BlockSpec((d, tk), lambda i, k: (0, k))],
        out_specs=[pl.BlockSpec((tm, d), lambda i, k: (i, 0))],
        out_shape=[SDS((s, d), F32)], scratch_shapes=[], args=(dr1, dp, wi), hosted=hosted)
    return gx, extra


def _adamw_math(w, g, m, v):
    m2 = ADAM_B1 * m + (1.0 - ADAM_B1) * g
    v2 = ADAM_B2 * v + (1.0 - ADAM_B2) * (g * g)
    m_hat = m2 / (1.0 - ADAM_B1 ** ADAM_STEP)
    v_hat = v2 / (1.0 - ADAM_B2 ** ADAM_STEP)
    delta = -ADAM_LR * (m_hat / (jnp.sqrt(v_hat) + ADAM_EPS) + ADAM_WD * w)
    return delta, m2, v2


def _adamw(w, g, m, v, name):
    r, c = w.shape
    tr = r if r <= 256 else 256

    def body(w_ref, g_ref, m_ref, v_ref, d_ref, m2_ref, v2_ref):
        delta, m2, v2 = _adamw_math(w_ref[...], g_ref[...], m_ref[...], v_ref[...])
        d_ref[...] = delta
        m2_ref[...] = m2
        v2_ref[...] = v2

    blk = pl.BlockSpec((tr, c), lambda i: (i, 0))
    return pl.pallas_call(
        body, name=name, grid=(r // tr,), in_specs=[blk] * 4, out_specs=[blk] * 3,
        out_shape=[SDS((r, c), F32)] * 3, compiler_params=_cp(("parallel",)),
    )(w, g, m, v)


def _sum_parts(parts, name):
    k, r, c = parts.shape

    def body(p_ref, o_ref):
        acc = p_ref[0]
        for q in range(1, k):
            acc = acc + p_ref[q]
        o_ref[...] = acc

    return pl.pallas_call(
        body, name=name, grid=(1,),
        in_specs=[pl.BlockSpec((k, r, c), lambda i: (0, 0, 0))],
        out_specs=pl.BlockSpec((r, c), lambda i: (0, 0)),
        out_shape=SDS((r, c), F32), compiler_params=_cp(("arbitrary",)),
    )(parts)


def _piece_shape(full_shape, axis):
    r, c = full_shape
    return (r // 2, c // 4) if axis == 1 else (r // 8, c)


def _piece_spec(full_shape, axis, tr, chip_of, half_of):
    hr, wc = _piece_shape(full_shape, axis)
    nb = hr // tr
    if axis == 1:
        return pl.BlockSpec((tr, wc), lambda *a: (half_of(*a) * nb + a[-2], chip_of(*a)))
    return pl.BlockSpec((tr, wc), lambda *a: ((2 * chip_of(*a) + half_of(*a)) * nb + a[-2], 0))


def _place_cast(w, axis, pos, name):
    r, c = w.shape
    tr = min(r, 256)
    nb = r // tr
    full = (r, 4 * c) if axis == 1 else (4 * r, c)
    out_map = (lambda i, pos: (i, pos[0])) if axis == 1 else (lambda i, pos: (pos[0] * nb + i, 0))

    def body(pos_ref, w_ref, o_ref):
        o_ref[...] = w_ref[...].astype(o_ref.dtype)

    gs = pltpu.PrefetchScalarGridSpec(
        num_scalar_prefetch=1, grid=(nb,),
        in_specs=[pl.BlockSpec((tr, c), lambda i, pos: (i, 0))], out_specs=pl.BlockSpec((tr, c), out_map))
    return pl.pallas_call(body, name=name, grid_spec=gs, out_shape=SDS(full, BF),
                          compiler_params=_cp(("arbitrary",)))(pos, w)


def _pair_add(g, land, axis, pos, name):
    hr, wc = _piece_shape(g.shape, axis)
    tr = min(hr, 256)

    def body(pos_ref, g_ref, l_ref, o_ref):
        o_ref[0] = (g_ref[...] + l_ref[0].astype(F32)).astype(BF)

    other = lambda q, i, pos: (pos[0] + 1 + q) % 4
    blk = pl.BlockSpec((1, tr, wc), lambda q, i, pos: (other(q, i, pos), i, 0))
    gs = pltpu.PrefetchScalarGridSpec(
        num_scalar_prefetch=1, grid=(3, hr // tr),
        in_specs=[_piece_spec(g.shape, axis, tr, other, lambda q, i, pos: pos[1]), blk], out_specs=blk)
    return pl.pallas_call(body, name=name, grid_spec=gs, out_shape=SDS((4, hr, wc), BF),
                          compiler_params=_cp(("arbitrary", "arbitrary")))(pos, g, land)


def _chip_sum(g, land1, land2, axis, pos, name):
    hr, wc = _piece_shape(g.shape, axis)
    tr = min(hr, 256)
    nb = hr // tr

    def body(pos_ref, g_ref, l1_ref, l2_ref, o_ref):
        acc = g_ref[...] + l1_ref[0].astype(F32)
        for q in range(3):
            acc = acc + l2_ref[q].astype(F32)
        o_ref[...] = acc

    gs = pltpu.PrefetchScalarGridSpec(
        num_scalar_prefetch=1, grid=(nb,),
        in_specs=[_piece_spec(g.shape, axis, tr, lambda i, pos: pos[0], lambda i, pos: pos[1]),
                  pl.BlockSpec((1, tr, wc), lambda i, pos: (pos[0], i, 0)),
                  pl.BlockSpec((3, tr, wc), lambda i, pos: (0, i, 0))],
        out_specs=pl.BlockSpec((tr, wc), lambda i, pos: (pos[1] * nb + i, 0)))
    return pl.pallas_call(body, name=name, grid_spec=gs, out_shape=SDS((2 * hr, wc), F32),
                          compiler_params=_cp(("arbitrary",)))(pos, g, land1, land2)


ANY = pl.BlockSpec(memory_space=pl.ANY)
COMM = pltpu.CompilerParams(has_side_effects=True)


def _on_each_device(fn):
    x, y, c = lax.axis_index("x"), lax.axis_index("y"), lax.axis_index("c")
    for sx in (0, 1):
        for sy in (0, 1):
            for sc in (0, 1):
                @pl.when(jnp.logical_and(jnp.logical_and(x == sx, y == sy), c == sc))
                def _(sx=sx, sy=sy, sc=sc):
                    fn(sx, sy, sc)


def _remote(src, dst, send_sem, recv_sem, to):
    return pltpu.make_async_remote_copy(src_ref=src, dst_ref=dst, send_sem=send_sem, recv_sem=recv_sem,
                                        device_id=to, device_id_type=MESH)


def _piece_ref(ref, axis, j, h):
    r, c = ref.shape
    hr, wc = _piece_shape((r, c), axis)
    if axis == 1:
        return ref.at[pl.ds(h * hr, hr), pl.ds(j * wc, wc)]
    return ref.at[pl.ds((2 * j + h) * hr, hr), :]


class _Stage:
    def __init__(self, ins, outs, n_sems, make_plan, aliases=None):
        self.ins, self.outs, self.n_sems, self.make_plan = list(ins), list(outs), n_sems, make_plan
        self.aliases = dict(aliases or {})


def _start(plan):
    def dev(sx, sy, sc):
        for cp, _, _ in plan(sx, sy, sc):
            cp.start()

    _on_each_device(dev)


def _finish(plan):
    def dev(sx, sy, sc):
        for _, sent, got in plan(sx, sy, sc):
            sent.wait_send()
            got.wait_recv()

    _on_each_device(dev)


def _comm_call(stage, name):
    n_in, n_out = len(stage.ins), len(stage.outs)

    def body(*refs):
        plan = stage.make_plan(refs[:n_in], refs[n_in:n_in + n_out], *refs[n_in + n_out:])
        _start(plan)
        _finish(plan)

    return pl.pallas_call(
        body, name=name, in_specs=[ANY] * n_in, out_specs=[ANY] * n_out, out_shape=stage.outs,
        input_output_aliases=stage.aliases,
        scratch_shapes=[pltpu.SemaphoreType.DMA((stage.n_sems,)), pltpu.SemaphoreType.DMA((stage.n_sems,))],
        compiler_params=COMM,
    )(*stage.ins)


class _SemsFrom:
    def __init__(self, sems, base):
        self.sems, self.base = sems, base

    @property
    def at(self):
        return self

    def __getitem__(self, k):
        return self.sems.at[self.base + k]


def _both(a, b):
    na, nb = len(a.ins), len(b.ins)
    ma = len(a.outs)

    def make_plan(ins, outs, send_sems, recv_sems):
        pa = a.make_plan(ins[:na], outs[:ma], send_sems, recv_sems)
        pb = b.make_plan(ins[na:], outs[ma:], _SemsFrom(send_sems, a.n_sems), _SemsFrom(recv_sems, a.n_sems))
        return lambda sx, sy, sc: pa(sx, sy, sc) + pb(sx, sy, sc)

    aliases = dict(a.aliases)
    aliases.update({na + i: ma + o for i, o in b.aliases.items()})
    return _Stage(a.ins + b.ins, a.outs + b.outs, a.n_sems + b.n_sems, make_plan, aliases)


def _same(cp):
    return (cp, cp, cp)


def _stage_gather_send(fulls, axes):
    n = len(fulls)

    def make_plan(ins, outs, send_sems, recv_sems):
        def plan(sx, sy, sc):
            cps = []
            for w in range(n):
                mine = _piece_ref(outs[w], axes[w], 2 * sx + sy, sc)
                for r, (fx, fy) in enumerate(CHIP_RELS):
                    k = 3 * w + r
                    to = (sx ^ fx, sy ^ fy, sc)
                    got = _piece_ref(outs[w], axes[w], 2 * (sx ^ fx) + (sy ^ fy), sc)
                    send = _remote(mine, mine, send_sems.at[k], recv_sems.at[k], to)
                    cps.append((send, send, _remote(got, got, send_sems.at[k], recv_sems.at[k], to)))
            return cps

        return plan

    return _Stage(fulls, [SDS(f.shape, f.dtype) for f in fulls], 3 * n, make_plan, {i: i for i in range(n)})


def _stage_gather_forward(fulls, axes):
    n = len(fulls)

    def make_plan(ins, outs, send_sems, recv_sems):
        def plan(sx, sy, sc):
            cps = []
            sib = (sx, sy, 1 - sc)
            for w in range(n):
                for r, (fx, fy) in enumerate(CHIP_RELS):
                    k = 3 * w + r
                    pj = 2 * (sx ^ fx) + (sy ^ fy)
                    have = _piece_ref(outs[w], axes[w], pj, sc)
                    want = _piece_ref(outs[w], axes[w], pj, 1 - sc)
                    send = _remote(have, have, send_sems.at[k], recv_sems.at[k], sib)
                    cps.append((send, send, _remote(want, want, send_sems.at[k], recv_sems.at[k], sib)))
            return cps

        return plan

    return _Stage(fulls, [SDS(f.shape, f.dtype) for f in fulls], 3 * n, make_plan, {i: i for i in range(n)})


def _in_place(fulls, n_sems, make_plan):
    return _Stage(fulls, [SDS(f.shape, f.dtype) for f in fulls], n_sems, make_plan, {i: i for i in range(len(fulls))})


def _stage_gather_neighbours(fulls, axes):
    n = len(fulls)

    def make_plan(ins, outs, send_sems, recv_sems):
        def plan(sx, sy, sc):
            cps = []
            for w in range(n):
                mine = _piece_ref(outs[w], axes[w], 2 * sx + sy, sc)
                for r, (px, py) in enumerate(((sx ^ 1, sy), (sx, sy ^ 1))):
                    got = _piece_ref(outs[w], axes[w], 2 * px + py, sc)
                    send = _remote(mine, mine, send_sems.at[2 * w + r], recv_sems.at[2 * w + r], (px, py, sc))
                    cps.append((send, send, _remote(got, got, send_sems.at[2 * w + r], recv_sems.at[2 * w + r],
                                                    (px, py, sc))))
            return cps

        return plan

    return _in_place(fulls, 2 * n, make_plan)


def _stage_gather_pair(fulls, axes):
    n = len(fulls)

    def make_plan(ins, outs, send_sems, recv_sems):
        def plan(sx, sy, sc):
            cps = []
            sib = (sx, sy, 1 - sc)
            for w in range(n):
                for r, j in enumerate((2 * (sx ^ 1) + sy, 2 * sx + (sy ^ 1))):
                    k = 2 * w + r
                    have, want = _piece_ref(outs[w], axes[w], j, sc), _piece_ref(outs[w], axes[w], j, 1 - sc)
                    send = _remote(have, have, send_sems.at[k], recv_sems.at[k], sib)
                    cps.append((send, send, _remote(want, want, send_sems.at[k], recv_sems.at[k], sib)))
            return cps

        return plan

    return _in_place(fulls, 2 * n, make_plan)


def _stage_gather_diagonal(fulls, axes):
    n = len(fulls)

    def make_plan(ins, outs, send_sems, recv_sems):
        def relay(sx, sy, sc):
            cps = []
            jx, jy, jd = 2 * (sx ^ 1) + sy, 2 * sx + (sy ^ 1), 2 * (sx ^ 1) + (sy ^ 1)
            passed, to = (jx, (sx, sy ^ 1, sc)) if sc == 0 else (jy, (sx ^ 1, sy, sc))
            for w in range(n):
                have, want = _piece_ref(outs[w], axes[w], passed, sc), _piece_ref(outs[w], axes[w], jd, sc)
                send = _remote(have, have, send_sems.at[w], recv_sems.at[w], to)
                cps.append((send, send, _remote(want, want, send_sems.at[w], recv_sems.at[w], to)))
            return cps

        def cross(sx, sy, sc):
            cps = []
            sib = (sx, sy, 1 - sc)
            jd = 2 * (sx ^ 1) + (sy ^ 1)
            for w in range(n):
                have, want = _piece_ref(outs[w], axes[w], jd, sc), _piece_ref(outs[w], axes[w], jd, 1 - sc)
                send = _remote(have, have, send_sems.at[n + w], recv_sems.at[n + w], sib)
                cps.append((send, send, _remote(want, want, send_sems.at[n + w], recv_sems.at[n + w], sib)))
            return cps

        return [relay, cross]

    return _in_place(fulls, 2 * n, make_plan)


def _stage_pair_exchange(grads, axes):
    n = len(grads)

    def make_plan(gs, land, send_sems, recv_sems):
        def plan(sx, sy, sc):
            return [_same(_remote(_piece_ref(gs[w], axes[w], jj, 1 - sc), land[w].at[jj], send_sems.at[4 * w + jj],
                                  recv_sems.at[4 * w + jj], (sx, sy, 1 - sc)))
                    for w in range(n) for jj in range(4)]

        return plan

    return _Stage(grads, [SDS((4,) + _piece_shape(g.shape, a), g.dtype) for g, a in zip(grads, axes)], 4 * n,
                  make_plan)


def _stage_chip_scatter(pieces):
    n = len(pieces)

    def make_plan(ps, land, send_sems, recv_sems):
        def plan(sx, sy, sc):
            return [_same(_remote(ps[w].at[2 * (sx ^ fx) + (sy ^ fy)], land[w].at[r], send_sems.at[3 * w + r],
                                  recv_sems.at[3 * w + r], (sx ^ fx, sy ^ fy, sc)))
                    for w in range(n) for r, (fx, fy) in enumerate(CHIP_RELS)]

        return plan

    return _Stage(pieces, [SDS((3,) + p.shape[1:], p.dtype) for p in pieces], 3 * n, make_plan)


def _stage_pair_share(shards):
    n = len(shards)

    def make_plan(ins, outs, send_sems, recv_sems):
        def plan(sx, sy, sc):
            cps = []
            sib = (sx, sy, 1 - sc)
            for w in range(n):
                hr = shards[w].shape[0] // 2
                mine = outs[w].at[pl.ds(sc * hr, hr), :]
                theirs = outs[w].at[pl.ds((1 - sc) * hr, hr), :]
                send = _remote(mine, mine, send_sems.at[w], recv_sems.at[w], sib)
                cps.append((send, send, _remote(theirs, theirs, send_sems.at[w], recv_sems.at[w], sib)))
            return cps

        return plan

    return _Stage(shards, [SDS(g.shape, g.dtype) for g in shards], n, make_plan, {i: i for i in range(n)})


def _stage_gather_small(stack):
    def make_plan(ins, outs, send_sems, recv_sems):
        def plan(sx, sy, sc):
            mine = outs[0].at[4 * sx + 2 * sy + sc]
            return [_same(_remote(mine, mine, send_sems.at[k], recv_sems.at[k], (sx ^ fx, sy ^ fy, sc ^ fc)))
                    for k, (fx, fy, fc) in enumerate(DEV_RELS)]

        return plan

    return _Stage([stack], [SDS(stack.shape, stack.dtype)], 7, make_plan, {0: 0})


def kernel(x, w_in, conv_a_w, w_out_a, conv_b_w, conv_b_bias, ln_b_gamma, ln_b_beta, w_out_b, w_o, ln1_gamma, ln1_beta, w_up, w_down, ln2_gamma, ln2_beta, loss_target, m_w_in, m_conv_a_w, m_w_out_a, m_conv_b_w, m_conv_b_bias, m_ln_b_gamma, m_ln_b_beta, m_w_out_b, m_w_o, m_ln1_gamma, m_ln1_beta, m_w_up, m_w_down, m_ln2_gamma, m_ln2_beta, v_w_in, v_conv_a_w, v_w_out_a, v_conv_b_w, v_conv_b_bias, v_ln_b_gamma, v_ln_b_beta, v_w_out_b, v_w_o, v_ln1_gamma, v_ln1_beta, v_w_up, v_w_down, v_ln2_gamma, v_ln2_beta):
    s, d = x.shape[1], x.shape[2]
    xs = x.reshape(s, d)
    tgt = loss_target.reshape(s, d)
    dq = d // 4
    chip = 2 * lax.axis_index("x") + lax.axis_index("y")
    core = lax.axis_index("c")
    pos = jnp.stack([chip, core]).astype(jnp.int32)
    names = ("w_in", "w_out_a", "w_out_b", "w_o", "w_up", "w_down")
    axes = (1, 0, 0, 0, 1, 0)

    conv_pack = jnp.concatenate([jnp.pad(conv_a_w, ((0, 8 - K_A), (0, 0))), jnp.pad(conv_b_w, ((0, 32 - K_B), (0, 0))),
                                 jnp.zeros((8, dq), F32)], axis=0)
    conv_full = lax.dynamic_update_slice(jnp.zeros((conv_pack.shape[0], d), F32), conv_pack, (0, chip * dq))
    fulls = [_place_cast(w, a, pos, "place_" + nm)
             for w, a, nm in zip((w_in, w_out_a, w_out_b, w_o, w_up, w_down), axes, names)]
    vec = lambda a: a.reshape(1, d)
    bias_b, lbg, lbb = vec(conv_b_bias), vec(ln_b_gamma), vec(ln_b_beta)
    l1g, l1b, l2g, l2b = vec(ln1_gamma), vec(ln1_beta), vec(ln2_gamma), vec(ln2_beta)

    (p, xb), landed = _in_proj_own(xs, w_in, pos, _stage_gather_neighbours([fulls[0], conv_full], (1, 1)))
    wi, convs = _comm_call(_stage_gather_pair(landed, (1, 1)), "gather_pair_w_in")
    p, (wi, convs), small3 = _in_proj_rest(xb, wi, p, pos, _stage_gather_diagonal([wi, convs], (1, 1)),
                                           _stage_gather_send(fulls[1:4], axes[1:4]))
    cwa, cwb = convs[0:K_A], convs[8:8 + K_B]
    (conva, yapre, nb, rstdb, u3), landed = _conv_fwd(
        p, cwa, cwb, bias_b, lbg, lbb, d,
        _both(_stage_gather_forward(small3, axes[1:4]), _stage_gather_send(fulls[4:6], axes[4:6])))
    woa, wob, wo = landed[:3]
    (ya, yb, merged, n1, rstd1), (wup, wdown) = _mixer_out(yapre, u3, p, xs, woa, wob, wo, d,
                                                           _stage_gather_forward(landed[3:], axes[4:6]))
    x1b, hb, dhpre, dr2b, dr1, dr1b, acc_mlp = _mlp(n1, rstd1, tgt, wup, wdown, l1g, l1b, l2g, l2b)
    g_up, gb_up, _ = _grad_w(x1b, dhpre, "grad_w_up")
    g_down, gb_down, _ = _grad_w(hb, dr2b, "grad_w_down")
    dya, dyb, dba, dga, dgb, dca, du1, acc_mix = _mixer_bwd_local(dr1b, p, ya, yb, conva, nb, rstdb, wo, woa, wob,
                                                                   lbg, lbb)
    g_oa, gb_oa, _ = _grad_w(yapre, dya, "grad_w_out_a")
    g_ob, gb_ob, _ = _grad_w(u3, dyb, "grad_w_out_b")
    g_o, gb_o, _ = _grad_w(merged, dr1b, "grad_w_o")

    early, e_axes, e_names = [g_oa, g_ob, g_o, g_up, g_down], axes[1:], names[1:]
    (dp, dcwa, dcwb), land1 = _conv_bwd(dca, du1, p, dba, dga, dgb, cwa, cwb, d,
                                        _stage_pair_exchange([gb_oa, gb_ob, gb_o, gb_up, gb_down], e_axes))
    pieces = [_pair_add(g, l, a, pos, "pair_add_" + nm) for g, l, a, nm in zip(early, land1, e_axes, e_names)]
    pack = jnp.concatenate([dcwa, dcwb, acc_mix, acc_mlp], axis=0)
    stack = lax.dynamic_update_slice(jnp.zeros((8,) + pack.shape, F32), pack[None], (2 * chip + core, 0, 0))
    g_wi, gb_wi, landed = _grad_w(xb, dp, "grad_w_in",
                                  _both(_stage_chip_scatter(pieces), _stage_gather_small(stack)))
    land2, stack = landed[:-1], landed[-1]
    halves = [_chip_sum(g, l1, l2, a, pos, "chip_sum_" + nm)
              for g, l1, l2, a, nm in zip(early, land1, land2, e_axes, e_names)]
    (land1_in,) = _comm_call(_stage_pair_exchange([gb_wi], (1,)), "pair_exchange_w_in")
    piece_in = _pair_add(g_wi, land1_in, 1, pos, "pair_add_w_in")
    grad_x, landed = _grad_x(dr1, dp, wi, _both(_stage_chip_scatter([piece_in]), _stage_pair_share(halves)))
    land2_in, (g_oa, g_ob, g_o, g_up, g_down) = landed[0], landed[1:]
    half_in = _chip_sum(g_wi, land1_in, land2_in, 1, pos, "chip_sum_w_in")
    (g_in,) = _comm_call(_stage_pair_share([half_in]), "pair_share_w_in")
    small = _sum_parts(stack, "small_sum")
    g_ca = lax.dynamic_slice(small, (0, chip * dq), (K_A, dq))
    g_cb = lax.dynamic_slice(small, (8, chip * dq), (K_B, dq))
    g_vec = jnp.stack([small[r] for r in (42, 40, 41, 51, 52, 49, 50)])

    loss = (0.5 / d) * jnp.sum(small[48])

    big = {}
    for name, w, g, m, v in (("w_in", w_in, g_in, m_w_in, v_w_in), ("w_out_a", w_out_a, g_oa, m_w_out_a, v_w_out_a),
                             ("w_out_b", w_out_b, g_ob, m_w_out_b, v_w_out_b), ("w_o", w_o, g_o, m_w_o, v_w_o),
                             ("w_up", w_up, g_up, m_w_up, v_w_up), ("w_down", w_down, g_down, m_w_down, v_w_down),
                             ("conv_a_w", conv_a_w, g_ca, m_conv_a_w, v_conv_a_w),
                             ("conv_b_w", conv_b_w, g_cb, m_conv_b_w, v_conv_b_w)):
        big[name] = (g,) + tuple(_adamw(w, g, m, v, "adamw_" + name))
    vec_names = ("conv_b_bias", "ln_b_gamma", "ln_b_beta", "ln1_gamma", "ln1_beta", "ln2_gamma", "ln2_beta")
    w7 = jnp.stack([conv_b_bias, ln_b_gamma, ln_b_beta, ln1_gamma, ln1_beta, ln2_gamma, ln2_beta])
    m7 = jnp.stack([m_conv_b_bias, m_ln_b_gamma, m_ln_b_beta, m_ln1_gamma, m_ln1_beta, m_ln2_gamma, m_ln2_beta])
    v7 = jnp.stack([v_conv_b_bias, v_ln_b_gamma, v_ln_b_beta, v_ln1_gamma, v_ln1_beta, v_ln2_gamma, v_ln2_beta])
    d7, nm7, nv7 = _adamw(w7, g_vec, m7, v7, "adamw_vectors")
    for q, name in enumerate(vec_names):
        big[name] = (g_vec[q], d7[q], nm7[q], nv7[q])

    order = ("w_in", "conv_a_w", "w_out_a", "conv_b_w", "conv_b_bias", "ln_b_gamma", "ln_b_beta", "w_out_b", "w_o",
             "ln1_gamma", "ln1_beta", "w_up", "w_down", "ln2_gamma", "ln2_beta")
    outs = [loss, grad_x.reshape(x.shape)]
    for part in range(4):
        outs += [big[name][part] for name in order]
    return tuple(outs)
```

```python
import jax
import jax.numpy as jnp
from jax import lax
from jax.experimental import pallas as pl
from jax.experimental.pallas import tpu as pltpu

F32 = jnp.float32
BF = jnp.bfloat16
SDS = jax.ShapeDtypeStruct
MESH = pl.DeviceIdType.MESH

ALPHA = 2.0 ** 0.25
LN_EPS = 1e-5
K_A = 3
K_B = 31
HALO = 32
CONV_TILE = 512
CONV_ROWS = 64
CONV_LANES = 128
ADAM_LR = 0.001
ADAM_B1 = 0.9
ADAM_B2 = 0.999
ADAM_EPS = 1e-08
ADAM_WD = 0.01
ADAM_STEP = 10
P_DT = BF
CHIP_RELS = ((1, 0), (0, 1), (1, 1))
DEV_RELS = tuple((fx, fy, fc) for fx in (0, 1) for fy in (0, 1) for fc in (0, 1))[1:]


def _cp(sem=None, vmem_mb=56, side_effects=False):
    return pltpu.CompilerParams(dimension_semantics=sem, vmem_limit_bytes=vmem_mb << 20,
                                has_side_effects=side_effects)


def _const(shape):
    return pl.BlockSpec(shape, lambda *_: (0,) * len(shape), pipeline_mode=pl.Buffered(1))


def _sig(v):
    return jax.nn.sigmoid(v)


def _ln_fwd(r):
    mu = jnp.mean(r, axis=-1, keepdims=True)
    xc = r - mu
    var = jnp.mean(xc * xc, axis=-1, keepdims=True)
    rstd = lax.rsqrt(var + LN_EPS)
    return xc * rstd, rstd


def _ln_bwd(dn, n, rstd):
    m1 = jnp.mean(dn, axis=-1, keepdims=True)
    m2 = jnp.mean(dn * n, axis=-1, keepdims=True)
    return rstd * (dn - m1 - n * m2)


def _dot(a, b):
    return jnp.dot(a, b, preferred_element_type=F32)


def _dot_nt(a, b):
    return lax.dot_general(a, b, (((1,), (1,)), ((), ())), preferred_element_type=F32)


def _dot_tn(a, b):
    return lax.dot_general(a, b, (((0,), (0,)), ((), ())), preferred_element_type=F32)


def _tile(n, pref):
    if n <= pref:
        return n
    return max(t for t in range(128, pref + 1, 128) if n % t == 0)


def _rowsum8(v):
    acc = v[0:8]
    for g in range(1, v.shape[0] // 8):
        acc = acc + v[8 * g:8 * g + 8]
    return acc


def _taps(win, offsets, rows):
    r_all = win.shape[0]
    by_res = {}
    for k, o in enumerate(offsets):
        by_res.setdefault(o % 8, []).append((k, o // 8))
    for s, taps in sorted(by_res.items()):
        r = win if s == 0 else pltpu.roll(win, r_all - s, 0)
        for k, q in taps:
            yield k, r[8 * q:8 * q + rows]


CAUSAL_A = [HALO - (K_A - 1) + k for k in range(K_A)]
CAUSAL_B = [HALO - (K_B - 1) + k for k in range(K_B)]
ANTI_A = [K_A - 1 - k for k in range(K_A)]
ANTI_B = [K_B - 1 - k for k in range(K_B)]


def _host_call(body, *, name, grid, in_specs, out_specs, out_shape, scratch_shapes, args, hosted, prefetch=None,
               aliases=None, body_gets_stage_refs=False):
    n_in, n_out, n_scr = len(in_specs), len(out_specs), len(scratch_shapes)
    n_steps = 1
    for size in grid:
        n_steps *= size
    if isinstance(hosted, _Stage):
        hosted = [(hosted, 0, n_steps - 1)]
    n_pre = 0 if prefetch is None else 1
    pre = () if prefetch is None else (prefetch,)
    sem = ("arbitrary",) * len(grid)
    own_aliases = {n_pre + a: b for a, b in (aliases or {}).items()}

    def call(kernel_body, ins, outs, shapes, scratch, all_aliases, side_effects, operands):
        gs = pltpu.PrefetchScalarGridSpec(num_scalar_prefetch=n_pre, grid=grid, in_specs=ins, out_specs=outs,
                                          scratch_shapes=scratch)
        return pl.pallas_call(kernel_body, name=name, grid_spec=gs, out_shape=shapes,
                              input_output_aliases=all_aliases,
                              compiler_params=_cp(sem, side_effects=side_effects))(*pre, *operands)

    if hosted is None:
        res = call(body, list(in_specs), list(out_specs), list(out_shape), list(scratch_shapes), own_aliases, False,
                   args)
        return list(res), []
    stages = [st for st, _, _ in hosted]
    h_ins = [a for st in stages for a in st.ins]
    h_outs = [o for st in stages for o in st.outs]
    borrowed = {hi: k for hi, a in enumerate(h_ins) for k, own in enumerate(args) if a is own}
    passed = [hi for hi in range(len(h_ins)) if hi not in borrowed]
    h_in, h_out, n_sems = len(passed), len(h_outs), sum(st.n_sems for st in stages)

    def full_body(*refs):
        pre_refs, refs = refs[:n_pre], refs[n_pre:]
        ins, refs = refs[:n_in], refs[n_in:]
        hins = [None] * len(h_ins)
        for hi, ref in zip(passed, refs[:h_in]):
            hins[hi] = ref
        refs = refs[h_in:]
        outs, refs = refs[:n_out], refs[n_out:]
        houts, refs = refs[:h_out], refs[h_out:]
        scr, (send_sems, recv_sems) = refs[:n_scr], refs[n_scr:]
        step = 0
        for a, size in enumerate(grid):
            step = step * size + pl.program_id(a)
        phases, i0, o0, k0 = [], 0, 0, 0
        for st, starts, finishes in hosted:
            plans = st.make_plan(hins[i0:i0 + len(st.ins)], houts[o0:o0 + len(st.outs)],
                                 _SemsFrom(send_sems, k0), _SemsFrom(recv_sems, k0))
            if not isinstance(plans, list):
                plans, starts, finishes = [plans], [starts], [finishes]
            phases += list(zip(plans, starts, finishes))
            i0, o0, k0 = i0 + len(st.ins), o0 + len(st.outs), k0 + st.n_sems
        for plan, at, _ in phases:
            @pl.when(step == at)
            def _(plan=plan):
                _start(plan)

        if body_gets_stage_refs:
            body(*pre_refs, *ins, *outs, *scr, houts)
        else:
            body(*pre_refs, *ins, *outs, *scr)

        for plan, _, at in phases:
            @pl.when(step == at)
            def _(plan=plan):
                _finish(plan)

    all_aliases = dict(own_aliases)
    i0 = o0 = 0
    for st in stages:
        for a, b in st.aliases.items():
            hi = i0 + a
            operand = borrowed[hi] if hi in borrowed else n_in + passed.index(hi)
            all_aliases[n_pre + operand] = n_out + o0 + b
        i0, o0 = i0 + len(st.ins), o0 + len(st.outs)
    res = call(full_body, list(in_specs) + [ANY] * h_in, list(out_specs) + [ANY] * h_out,
               list(out_shape) + h_outs,
               list(scratch_shapes) + [pltpu.SemaphoreType.DMA((n_sems,)), pltpu.SemaphoreType.DMA((n_sems,))],
               all_aliases, True, (*args, *[h_ins[hi] for hi in passed]))
    return list(res[:n_out]), list(res[n_out:])


def _in_proj_own(x, w_shard, others, other_axes, pos, hosted):
    s, d = x.shape
    tn = w_shard.shape[1]
    tm = min(s, 1024)
    n_o = len(others)

    def body(pos_ref, x_ref, w_ref, *refs):
        o_in, (p_ref, xb_ref), o_out, wb = refs[:n_o], refs[n_o:n_o + 2], refs[n_o + 2:2 * n_o + 2], refs[-1]

        @pl.when(pl.program_id(0) == 0)
        def _():
            wb[...] = w_ref[...].astype(BF)
            for src, dst in zip(o_in, o_out):
                dst[...] = src[...].astype(BF)

        xb = x_ref[...].astype(BF)
        xb_ref[...] = xb
        p_ref[...] = _dot(xb, wb[...]).astype(p_ref.dtype)

    whole = lambda a: pl.BlockSpec(a.shape, lambda i, pos: (0, 0), pipeline_mode=pl.Buffered(1))
    placed = lambda a, ax: pl.BlockSpec(a.shape, (lambda i, pos: (0, pos[0])) if ax == 1 else (lambda i, pos: (pos[0], 0)))
    full = lambda a, ax: SDS((a.shape[0], 4 * a.shape[1]) if ax == 1 else (4 * a.shape[0], a.shape[1]), BF)
    outs, extra = _host_call(
        body, name="in_proj_own", grid=(s // tm,), prefetch=pos,
        in_specs=[pl.BlockSpec((tm, d), lambda i, pos: (i, 0)), whole(w_shard)] + [whole(a) for a in others],
        out_specs=[pl.BlockSpec((tm, tn), lambda i, pos: (i, pos[0])), pl.BlockSpec((tm, d), lambda i, pos: (i, 0))]
        + [placed(a, ax) for a, ax in zip(others, other_axes)],
        out_shape=[SDS((s, 4 * tn), P_DT), SDS((s, d), BF)] + [full(a, ax) for a, ax in zip(others, other_axes)],
        scratch_shapes=[pltpu.VMEM((d, tn), BF)], args=(x, w_shard, *others), hosted=hosted)
    return outs, extra


def _in_proj_rest(xb, wi, p, pos, diagonal, other):
    s, d = xb.shape
    n = wi.shape[1]
    tm, tn = min(s // 4, 1024), n // 4
    ni = s // tm

    def body(pos_ref, xb_ref, w_ref, p_in, p_ref, wdiag, dsem, stage_outs):
        j = pl.program_id(0)
        step = j * ni + pl.program_id(1)

        def diagonal_block(act):
            for chip in range(4):
                @pl.when(pos_ref[0] == chip)
                def _(chip=chip):
                    act(pltpu.make_async_copy(stage_outs[0].at[:, pl.ds((chip ^ 3) * tn, tn)], wdiag, dsem))

        @pl.when(step == 2 * ni - 1)
        def _():
            diagonal_block(lambda cp: cp.start())

        @pl.when(step == 2 * ni)
        def _():
            diagonal_block(lambda cp: cp.wait())

        @pl.when(j < 2)
        def _():
            p_ref[...] = _dot(xb_ref[...], w_ref[...]).astype(p_ref.dtype)

        @pl.when(j == 2)
        def _():
            p_ref[...] = _dot(xb_ref[...], wdiag[...]).astype(p_ref.dtype)

    def col(j, i, pos):
        return lax.bitwise_xor(pos[0], jnp.where(j == 0, 2, jnp.where(j == 1, 1, 3)))

    def piped_col(j, i, pos):
        return lax.bitwise_xor(pos[0], jnp.where(j == 0, 2, 1))

    (p,), extra = _host_call(
        body, name="in_proj_rest", grid=(3, ni), prefetch=pos, aliases={2: 0}, body_gets_stage_refs=True,
        in_specs=[pl.BlockSpec((tm, d), lambda j, i, pos: (i, 0)),
                  pl.BlockSpec((d, tn), lambda j, i, pos: (0, piped_col(j, i, pos))), ANY],
        out_specs=[pl.BlockSpec((tm, tn), lambda j, i, pos: (i, col(j, i, pos)))],
        out_shape=[SDS((s, n), P_DT)], scratch_shapes=[pltpu.VMEM((d, tn), BF), pltpu.SemaphoreType.DMA(())],
        args=(xb, wi, p),
        hosted=[(diagonal, [0, ni + 1], [ni, 2 * ni - 2]), (other, 0, 3 * ni - 1)])
    n_d = len(diagonal.outs)
    return p, extra[:n_d], extra[n_d:]


def _col_spec(tm, d, k):
    return pl.BlockSpec((tm, d), lambda i, k=k: (i, k))


def _prev_halo_spec(tm, d, k):
    r = tm // HALO
    return pl.BlockSpec((HALO, d), lambda i, k=k: (jnp.maximum(i * r - 1, 0), k))


def _next_halo_spec(tm, d, k, s):
    r = tm // HALO
    last = s // HALO - 1
    return pl.BlockSpec((HALO, d), lambda i, k=k: (jnp.minimum((i + 1) * r, last), k))


def _conv_fwd(p, cwa, cwb, bias_b, lbg, lbb, d, hosted=None):
    s = p.shape[0]
    tm = min(s, CONV_TILE)
    nt = s // tm

    def body(ba_ref, ca_ref, va_ref, vb_ref, gb_ref, hca_ref, hva_ref, hvb_ref, hgb_ref,
             cwa_ref, cwb_ref, bias_ref, lbg_ref, lbb_ref,
             conva_ref, yapre_ref, nb_ref, rstdb_ref, u3_ref,
             zbuf, ubuf, u1buf):
        i = pl.program_id(0)
        keep = (i > 0).astype(F32)
        zbuf[pl.ds(0, HALO), :] = hca_ref[...].astype(F32) * hva_ref[...].astype(F32) * keep
        ubuf[pl.ds(0, HALO), :] = hvb_ref[...].astype(F32) * _sig(hgb_ref[...].astype(F32)) * keep
        zbuf[pl.ds(HALO, tm), :] = ca_ref[...].astype(F32) * va_ref[...].astype(F32)
        ubuf[pl.ds(HALO, tm), :] = vb_ref[...].astype(F32) * _sig(gb_ref[...].astype(F32))

        def chunk(j, carry):
            r0 = pl.multiple_of(j * CONV_ROWS, CONV_ROWS)
            rows = pl.ds(r0, CONV_ROWS)
            for lc in range(d // CONV_LANES):
                ls = pl.ds(lc * CONV_LANES, CONV_LANES)
                acc = jnp.zeros((CONV_ROWS, CONV_LANES), F32)
                for k, sl in _taps(zbuf[pl.ds(r0, CONV_ROWS + HALO), ls], CAUSAL_A, CONV_ROWS):
                    acc = acc + cwa_ref[pl.ds(k, 1), ls] * sl
                conva_ref[rows, ls] = acc.astype(BF)
                yapre_ref[rows, ls] = (ba_ref[rows, ls].astype(F32) * acc).astype(BF)
                acc = jnp.zeros((CONV_ROWS, CONV_LANES), F32)
                for k, sl in _taps(ubuf[pl.ds(r0, CONV_ROWS + HALO), ls], CAUSAL_B, CONV_ROWS):
                    acc = acc + cwb_ref[pl.ds(k, 1), ls] * sl
                u1buf[rows, ls] = acc + bias_ref[:, ls]
            return carry

        lax.fori_loop(0, tm // CONV_ROWS, chunk, 0)
        nb, rstd = _ln_fwd(u1buf[...])
        nb_ref[...] = nb
        rstdb_ref[...] = rstd
        u2 = nb * lbg_ref[...] + lbb_ref[...]
        u3_ref[...] = (u2 * _sig(u2)).astype(BF)

    vec = _const((1, d))
    return _host_call(
        body, name="conv_fwd", grid=(nt,),
        in_specs=[_col_spec(tm, d, k) for k in range(5)] + [_prev_halo_spec(tm, d, k) for k in (1, 2, 3, 4)]
        + [_const((K_A, d)), _const((K_B, d)), vec, vec, vec],
        out_specs=[pl.BlockSpec((tm, d), lambda i: (i, 0)), pl.BlockSpec((tm, d), lambda i: (i, 0)),
                   pl.BlockSpec((tm, d), lambda i: (i, 0)), pl.BlockSpec((tm, 1), lambda i: (i, 0)),
                   pl.BlockSpec((tm, d), lambda i: (i, 0))],
        out_shape=[SDS((s, d), BF), SDS((s, d), BF), SDS((s, d), F32), SDS((s, 1), F32), SDS((s, d), BF)],
        scratch_shapes=[pltpu.VMEM((HALO + tm, d), F32), pltpu.VMEM((HALO + tm, d), F32), pltpu.VMEM((tm, d), F32)],
        args=(p, p, p, p, p, p, p, p, p, cwa, cwb, bias_b, lbg, lbb), hosted=hosted)


def _mixer_out(yapre, u3, p, x, woa, wob, wo, d, hosted=None):
    s = x.shape[0]
    tm = min(s, 512)

    def body(yapre_ref, u3_ref, ga_ref, gb_ref, x_ref, woa_ref, wob_ref, wo_ref,
             ya_ref, yb_ref, merged_ref, n1_ref, rstd1_ref):
        ya = _dot(yapre_ref[...], woa_ref[...])
        yb = _dot(u3_ref[...], wob_ref[...])
        ya_ref[...] = ya.astype(BF)
        yb_ref[...] = yb.astype(BF)
        merged = (_sig(ga_ref[...].astype(F32)) * ya + _sig(gb_ref[...].astype(F32)) * yb).astype(BF)
        merged_ref[...] = merged
        r1 = F32(ALPHA) * x_ref[...] + _dot(merged, wo_ref[...])
        n1, rstd1 = _ln_fwd(r1)
        n1_ref[...] = n1
        rstd1_ref[...] = rstd1

    row = pl.BlockSpec((tm, d), lambda i: (i, 0))
    return _host_call(
        body, name="mixer_out", grid=(s // tm,),
        in_specs=[row, row, _col_spec(tm, d, 5), _col_spec(tm, d, 6), row,
                  _const((d, d)), _const((d, d)), _const((d, d))],
        out_specs=[row, row, row, row, pl.BlockSpec((tm, 1), lambda i: (i, 0))],
        out_shape=[SDS((s, d), BF), SDS((s, d), BF), SDS((s, d), BF), SDS((s, d), F32), SDS((s, 1), F32)],
        scratch_shapes=[], args=(yapre, u3, p, p, x, woa, wob, wo), hosted=hosted)


def _mlp(n1, rstd1, tgt, wup, wdown, l1g, l1b, l2g, l2b):
    s, d = n1.shape
    dff = wup.shape[1]
    tm = min(s, 256)
    fc = min(dff, 1024)
    nq = dff // fc

    def body(n1_ref, rstd1_ref, tgt_ref, wup_ref, wdown_ref, l1g_ref, l1b_ref, l2g_ref, l2b_ref,
             x1b_ref, hb_ref, dhpre_ref, dr2b_ref, dr1_ref, dr1b_ref, acc_ref, rbuf):
        i = pl.program_id(0)
        n1v = n1_ref[...]
        x1 = n1v * l1g_ref[...] + l1b_ref[...]
        x1b = x1.astype(BF)
        x1b_ref[...] = x1b
        ff = jnp.zeros((tm, d), F32)
        for q in range(nq):
            cs = pl.ds(q * fc, fc)
            r = jnp.maximum(_dot(x1b, wup_ref[:, cs]), 0.0)
            rbuf[:, cs] = r
            hq = (r * r).astype(BF)
            hb_ref[:, cs] = hq
            ff = ff + _dot(hq, wdown_ref[cs, :])
        n2, rstd2 = _ln_fwd(F32(ALPHA) * x1 + ff)
        x2 = n2 * l2g_ref[...] + l2b_ref[...]
        err = x2 - tgt_ref[...]
        dx2 = err * F32(1.0 / d)
        dr2 = _ln_bwd(dx2 * l2g_ref[...], n2, rstd2)
        dr2b = dr2.astype(BF)
        dr2b_ref[...] = dr2b
        dx1 = F32(ALPHA) * dr2
        for q in range(nq):
            cs = pl.ds(q * fc, fc)
            dh = _dot_nt(dr2b, wdown_ref[cs, :])
            dhp = (dh * (2.0 * rbuf[:, cs])).astype(BF)
            dhpre_ref[:, cs] = dhp
            dx1 = dx1 + _dot_nt(dhp, wup_ref[:, cs])
        dr1 = _ln_bwd(dx1 * l1g_ref[...], n1v, rstd1_ref[...])
        dr1_ref[...] = dr1
        dr1b_ref[...] = dr1.astype(BF)

        @pl.when(i == 0)
        def _():
            acc_ref[...] = jnp.zeros_like(acc_ref)

        for q, val in enumerate((err * err, dx2 * n2, dx2, dx1 * n1v, dx1)):
            acc_ref[pl.ds(q, 1), :] += jnp.sum(val, axis=0, keepdims=True)

    row = pl.BlockSpec((tm, d), lambda i: (i, 0))
    wide = pl.BlockSpec((tm, dff), lambda i: (i, 0))
    vec = _const((1, d))
    return pl.pallas_call(
        body, name="mlp_fwd_bwd", grid=(s // tm,),
        in_specs=[row, pl.BlockSpec((tm, 1), lambda i: (i, 0)), row, _const((d, dff)), _const((dff, d)),
                  vec, vec, vec, vec],
        out_specs=[row, wide, wide, row, row, row, pl.BlockSpec((8, d), lambda i: (0, 0))],
        out_shape=[SDS((s, d), BF), SDS((s, dff), BF), SDS((s, dff), BF), SDS((s, d), BF), SDS((s, d), F32),
                   SDS((s, d), BF), SDS((8, d), F32)],
        scratch_shapes=[pltpu.VMEM((tm, dff), F32)],
        compiler_params=_cp(("arbitrary",)),
    )(n1, rstd1, tgt, wup, wdown, l1g, l1b, l2g, l2b)


def _mixer_bwd_local(dr1b, p, ya, yb, conva, nb, rstdb, wo, woa, wob, lbg, lbb):
    s, d = ya.shape
    tm = min(s, 256)

    def body(dr1b_ref, ba_ref, ga_ref, gb_ref, ya_ref, yb_ref, conva_ref, nb_ref, rstdb_ref,
             wo_ref, woa_ref, wob_ref, lbg_ref, lbb_ref,
             dya_ref, dyb_ref, dba_ref, dga_ref, dgb_ref, dca_ref, du1_ref, acc_ref):
        i = pl.program_id(0)
        dmerged = _dot_nt(dr1b_ref[...], wo_ref[...])
        sa = _sig(ga_ref[...].astype(F32))
        sb = _sig(gb_ref[...].astype(F32))
        dya = (dmerged * sa).astype(BF)
        dyb = (dmerged * sb).astype(BF)
        dya_ref[...] = dya
        dyb_ref[...] = dyb
        dga_ref[...] = (dmerged * ya_ref[...].astype(F32) * (sa * (1.0 - sa))).astype(BF)
        dgb_ref[...] = (dmerged * yb_ref[...].astype(F32) * (sb * (1.0 - sb))).astype(BF)
        dyapre = _dot_nt(dya, woa_ref[...])
        dba_ref[...] = (dyapre * conva_ref[...].astype(F32)).astype(BF)
        dca_ref[...] = (dyapre * ba_ref[...].astype(F32)).astype(BF)
        du3 = _dot_nt(dyb, wob_ref[...])
        nbv = nb_ref[...]
        u2 = nbv * lbg_ref[...] + lbb_ref[...]
        sg = _sig(u2)
        du2 = du3 * (sg * (1.0 + u2 * (1.0 - sg)))
        du1 = _ln_bwd(du2 * lbg_ref[...], nbv, rstdb_ref[...])
        du1_ref[...] = du1.astype(BF)

        @pl.when(i == 0)
        def _():
            acc_ref[...] = jnp.zeros_like(acc_ref)

        for q, val in enumerate((du2 * nbv, du2, du1)):
            acc_ref[pl.ds(q, 1), :] += jnp.sum(val, axis=0, keepdims=True)

    row = pl.BlockSpec((tm, d), lambda i: (i, 0))
    vec = _const((1, d))
    return pl.pallas_call(
        body, name="mixer_bwd_local", grid=(s // tm,),
        in_specs=[row, _col_spec(tm, d, 0), _col_spec(tm, d, 5), _col_spec(tm, d, 6), row, row, row, row,
                  pl.BlockSpec((tm, 1), lambda i: (i, 0)), _const((d, d)), _const((d, d)), _const((d, d)), vec, vec],
        out_specs=[row, row, row, row, row, row, row, pl.BlockSpec((8, d), lambda i: (0, 0))],
        out_shape=[SDS((s, d), BF)] * 7 + [SDS((8, d), F32)],
        compiler_params=_cp(("arbitrary",)),
    )(dr1b, p, p, p, ya, yb, conva, nb, rstdb, wo, woa, wob, lbg, lbb)


def _conv_bwd(dca, du1, p, dba, dga, dgb, cwa, cwb, d, hosted=None):
    s = dca.shape[0]
    tm = min(s, CONV_TILE)
    nt = s // tm

    def body(dca_ref, du1_ref, ndca_ref, ndu1_ref, ca_ref, va_ref, vb_ref, gb_ref,
             dba_ref, dga_ref, dgb_ref, cwa_ref, cwb_ref,
             dp_ref, dcwa_ref, dcwb_ref,
             dcabuf, du1buf, sgbuf, acca, accb):
        i = pl.program_id(0)
        keep_next = (i < nt - 1).astype(F32)

        @pl.when(i == 0)
        def _():
            acca[...] = jnp.zeros_like(acca)
            accb[...] = jnp.zeros_like(accb)

        sgbuf[...] = _sig(gb_ref[...].astype(F32))
        dcabuf[pl.ds(0, tm), :] = dca_ref[...].astype(F32)
        dcabuf[pl.ds(tm, HALO), :] = ndca_ref[...].astype(F32) * keep_next
        du1buf[pl.ds(0, tm), :] = du1_ref[...].astype(F32)
        du1buf[pl.ds(tm, HALO), :] = ndu1_ref[...].astype(F32) * keep_next
        dp_ref[:, pl.ds(0, d)] = dba_ref[...]
        dp_ref[:, pl.ds(5 * d, d)] = dga_ref[...]
        dp_ref[:, pl.ds(6 * d, d)] = dgb_ref[...]

        def chunk(j, carry):
            r0 = pl.multiple_of(j * CONV_ROWS, CONV_ROWS)
            rows = pl.ds(r0, CONV_ROWS)
            for lc in range(d // CONV_LANES):
                lo = lc * CONV_LANES
                ls = pl.ds(lo, CONV_LANES)
                cac = ca_ref[rows, ls].astype(F32)
                vac = va_ref[rows, ls].astype(F32)
                zc = cac * vac
                acc = jnp.zeros((CONV_ROWS, CONV_LANES), F32)
                for k, sl in _taps(dcabuf[pl.ds(r0, CONV_ROWS + HALO), ls], ANTI_A, CONV_ROWS):
                    acc = acc + cwa_ref[pl.ds(k, 1), ls] * sl
                    acca[pl.ds(8 * k, 8), ls] += _rowsum8(sl * zc)
                dp_ref[rows, pl.ds(d + lo, CONV_LANES)] = (acc * vac).astype(BF)
                dp_ref[rows, pl.ds(2 * d + lo, CONV_LANES)] = (acc * cac).astype(BF)
                sgc = sgbuf[rows, ls]
                vbc = vb_ref[rows, ls].astype(F32)
                uc = vbc * sgc
                acc = jnp.zeros((CONV_ROWS, CONV_LANES), F32)
                for k, sl in _taps(du1buf[pl.ds(r0, CONV_ROWS + HALO), ls], ANTI_B, CONV_ROWS):
                    acc = acc + cwb_ref[pl.ds(k, 1), ls] * sl
                    accb[pl.ds(8 * k, 8), ls] += _rowsum8(sl * uc)
                dp_ref[rows, pl.ds(3 * d + lo, CONV_LANES)] = (acc * sgc).astype(BF)
                dp_ref[rows, pl.ds(4 * d + lo, CONV_LANES)] = (acc * vbc * (sgc * (1.0 - sgc))).astype(BF)
            return carry

        lax.fori_loop(0, tm // CONV_ROWS, chunk, 0)

        @pl.when(i == nt - 1)
        def _():
            dcwa_ref[...] = jnp.zeros_like(dcwa_ref)
            dcwb_ref[...] = jnp.zeros_like(dcwb_ref)
            for k in range(K_A):
                dcwa_ref[pl.ds(k, 1), :] = jnp.sum(acca[pl.ds(8 * k, 8), :], axis=0, keepdims=True)
            for k in range(K_B):
                dcwb_ref[pl.ds(k, 1), :] = jnp.sum(accb[pl.ds(8 * k, 8), :], axis=0, keepdims=True)

    row = pl.BlockSpec((tm, d), lambda i: (i, 0))
    nxt = _next_halo_spec(tm, d, 0, s)
    return _host_call(
        body, name="conv_bwd", grid=(nt,),
        in_specs=[row, row, nxt, nxt] + [_col_spec(tm, d, k) for k in (1, 2, 3, 4)]
        + [row, row, row, _const((K_A, d)), _const((K_B, d))],
        out_specs=[pl.BlockSpec((tm, 7 * d), lambda i: (i, 0)), pl.BlockSpec((8, d), lambda i: (0, 0)),
                   pl.BlockSpec((32, d), lambda i: (0, 0))],
        out_shape=[SDS((s, 7 * d), BF), SDS((8, d), F32), SDS((32, d), F32)],
        scratch_shapes=[pltpu.VMEM((tm + HALO, d), F32), pltpu.VMEM((tm + HALO, d), F32),
                        pltpu.VMEM((tm, d), F32), pltpu.VMEM((8 * K_A, d), F32), pltpu.VMEM((8 * K_B, d), F32)],
        args=(dca, du1, dca, du1, p, p, p, p, dba, dga, dgb, cwa, cwb), hosted=hosted)


def _grad_w(a, b, name, hosted=None):
    s, m = a.shape
    n = b.shape[1]
    tm, tn, tk = _tile(m, 1024), _tile(n, 1024), _tile(s, 2048)
    nk = s // tk

    def body(a_ref, b_ref, o_ref, ob_ref):
        k = pl.program_id(2)

        @pl.when(k == 0)
        def _():
            o_ref[...] = jnp.zeros_like(o_ref)

        o_ref[...] += _dot_tn(a_ref[...], b_ref[...])

        @pl.when(k == nk - 1)
        def _():
            ob_ref[...] = o_ref[...].astype(BF)

    blk = pl.BlockSpec((tm, tn), lambda i, j, k: (i, j))
    (g, gb), extra = _host_call(
        body, name=name, grid=(m // tm, n // tn, nk),
        in_specs=[pl.BlockSpec((tk, tm), lambda i, j, k: (k, i)), pl.BlockSpec((tk, tn), lambda i, j, k: (k, j))],
        out_specs=[blk, blk], out_shape=[SDS((m, n), F32), SDS((m, n), BF)], scratch_shapes=[], args=(a, b),
        hosted=hosted)
    return g, gb, extra


def _grad_x(dr1, dp, wi, hosted=None):
    s, d = dr1.shape
    n = wi.shape[1]
    tm, tk = min(s, 512), _tile(n, 3584)

    def body(dr1_ref, dp_ref, w_ref, o_ref):
        @pl.when(pl.program_id(1) == 0)
        def _():
            o_ref[...] = F32(ALPHA) * dr1_ref[...]

        o_ref[...] += _dot_nt(dp_ref[...], w_ref[...])

    (gx,), extra = _host_call(
        body, name="grad_x", grid=(s // tm, n // tk),
        in_specs=[pl.BlockSpec((tm, d), lambda i, k: (i, 0)), pl.BlockSpec((tm, tk), lambda i, k: (i, k)),
                  pl.BlockSpec((d, tk), lambda i, k: (0, k))],
        out_specs=[pl.BlockSpec((tm, d), lambda i, k: (i, 0))],
        out_shape=[SDS((s, d), F32)], scratch_shapes=[], args=(dr1, dp, wi), hosted=hosted)
    return gx, extra


def _adamw_math(w, g, m, v):
    m2 = ADAM_B1 * m + (1.0 - ADAM_B1) * g
    v2 = ADAM_B2 * v + (1.0 - ADAM_B2) * (g * g)
    m_hat = m2 / (1.0 - ADAM_B1 ** ADAM_STEP)
    v_hat = v2 / (1.0 - ADAM_B2 ** ADAM_STEP)
    delta = -ADAM_LR * (m_hat / (jnp.sqrt(v_hat) + ADAM_EPS) + ADAM_WD * w)
    return delta, m2, v2


def _adamw(w, g, m, v, name):
    r, c = w.shape
    tr = r if r <= 256 else 256

    def body(w_ref, g_ref, m_ref, v_ref, d_ref, m2_ref, v2_ref):
        delta, m2, v2 = _adamw_math(w_ref[...], g_ref[...], m_ref[...], v_ref[...])
        d_ref[...] = delta
        m2_ref[...] = m2
        v2_ref[...] = v2

    blk = pl.BlockSpec((tr, c), lambda i: (i, 0))
    return pl.pallas_call(
        body, name=name, grid=(r // tr,), in_specs=[blk] * 4, out_specs=[blk] * 3,
        out_shape=[SDS((r, c), F32)] * 3, compiler_params=_cp(("parallel",)),
    )(w, g, m, v)


def _sum_parts(parts, name):
    k, r, c = parts.shape

    def body(p_ref, o_ref):
        acc = p_ref[0]
        for q in range(1, k):
            acc = acc + p_ref[q]
        o_ref[...] = acc

    return pl.pallas_call(
        body, name=name, grid=(1,),
        in_specs=[pl.BlockSpec((k, r, c), lambda i: (0, 0, 0))],
        out_specs=pl.BlockSpec((r, c), lambda i: (0, 0)),
        out_shape=SDS((r, c), F32), compiler_params=_cp(("arbitrary",)),
    )(parts)


def _piece_shape(full_shape, axis):
    r, c = full_shape
    return (r // 2, c // 4) if axis == 1 else (r // 8, c)


def _piece_spec(full_shape, axis, tr, chip_of, half_of):
    hr, wc = _piece_shape(full_shape, axis)
    nb = hr // tr
    if axis == 1:
        return pl.BlockSpec((tr, wc), lambda *a: (half_of(*a) * nb + a[-2], chip_of(*a)))
    return pl.BlockSpec((tr, wc), lambda *a: ((2 * chip_of(*a) + half_of(*a)) * nb + a[-2], 0))


def _place_cast(w, axis, pos, name):
    r, c = w.shape
    tr = min(r, 256)
    nb = r // tr
    full = (r, 4 * c) if axis == 1 else (4 * r, c)
    out_map = (lambda i, pos: (i, pos[0])) if axis == 1 else (lambda i, pos: (pos[0] * nb + i, 0))

    def body(pos_ref, w_ref, o_ref):
        o_ref[...] = w_ref[...].astype(o_ref.dtype)

    gs = pltpu.PrefetchScalarGridSpec(
        num_scalar_prefetch=1, grid=(nb,),
        in_specs=[pl.BlockSpec((tr, c), lambda i, pos: (i, 0))], out_specs=pl.BlockSpec((tr, c), out_map))
    return pl.pallas_call(body, name=name, grid_spec=gs, out_shape=SDS(full, BF),
                          compiler_params=_cp(("arbitrary",)))(pos, w)


def _pair_add(g, land, axis, pos, name):
    hr, wc = _piece_shape(g.shape, axis)
    tr = min(hr, 256)

    def body(pos_ref, g_ref, l_ref, o_ref):
        o_ref[0] = (g_ref[...] + l_ref[0].astype(F32)).astype(BF)

    other = lambda q, i, pos: (pos[0] + 1 + q) % 4
    blk = pl.BlockSpec((1, tr, wc), lambda q, i, pos: (other(q, i, pos), i, 0))
    gs = pltpu.PrefetchScalarGridSpec(
        num_scalar_prefetch=1, grid=(3, hr // tr),
        in_specs=[_piece_spec(g.shape, axis, tr, other, lambda q, i, pos: pos[1]), blk], out_specs=blk)
    return pl.pallas_call(body, name=name, grid_spec=gs, out_shape=SDS((4, hr, wc), BF),
                          compiler_params=_cp(("arbitrary", "arbitrary")))(pos, g, land)


def _chip_sum(g, land1, land2, axis, pos, name):
    hr, wc = _piece_shape(g.shape, axis)
    tr = min(hr, 256)
    nb = hr // tr

    def body(pos_ref, g_ref, l1_ref, l2_ref, o_ref):
        acc = g_ref[...] + l1_ref[0].astype(F32)
        for q in range(3):
            acc = acc + l2_ref[q].astype(F32)
        o_ref[...] = acc

    gs = pltpu.PrefetchScalarGridSpec(
        num_scalar_prefetch=1, grid=(nb,),
        in_specs=[_piece_spec(g.shape, axis, tr, lambda i, pos: pos[0], lambda i, pos: pos[1]),
                  pl.BlockSpec((1, tr, wc), lambda i, pos: (pos[0], i, 0)),
                  pl.BlockSpec((3, tr, wc), lambda i, pos: (0, i, 0))],
        out_specs=pl.BlockSpec((tr, wc), lambda i, pos: (pos[1] * nb + i, 0)))
    return pl.pallas_call(body, name=name, grid_spec=gs, out_shape=SDS((2 * hr, wc), F32),
                          compiler_params=_cp(("arbitrary",)))(pos, g, land1, land2)


ANY = pl.BlockSpec(memory_space=pl.ANY)
COMM = pltpu.CompilerParams(has_side_effects=True)


def _on_each_device(fn):
    x, y, c = lax.axis_index("x"), lax.axis_index("y"), lax.axis_index("c")
    for sx in (0, 1):
        for sy in (0, 1):
            for sc in (0, 1):
                @pl.when(jnp.logical_and(jnp.logical_and(x == sx, y == sy), c == sc))
                def _(sx=sx, sy=sy, sc=sc):
                    fn(sx, sy, sc)


def _remote(src, dst, send_sem, recv_sem, to):
    return pltpu.make_async_remote_copy(src_ref=src, dst_ref=dst, send_sem=send_sem, recv_sem=recv_sem,
                                        device_id=to, device_id_type=MESH)


def _piece_ref(ref, axis, j, h):
    r, c = ref.shape
    hr, wc = _piece_shape((r, c), axis)
    if axis == 1:
        return ref.at[pl.ds(h * hr, hr), pl.ds(j * wc, wc)]
    return ref.at[pl.ds((2 * j + h) * hr, hr), :]


class _Stage:
    def __init__(self, ins, outs, n_sems, make_plan, aliases=None):
        self.ins, self.outs, self.n_sems, self.make_plan = list(ins), list(outs), n_sems, make_plan
        self.aliases = dict(aliases or {})


def _start(plan):
    def dev(sx, sy, sc):
        for cp, _, _ in plan(sx, sy, sc):
            cp.start()

    _on_each_device(dev)


def _finish(plan):
    def dev(sx, sy, sc):
        for _, sent, got in plan(sx, sy, sc):
            sent.wait_send()
            got.wait_recv()

    _on_each_device(dev)


def _comm_call(stage, name):
    n_in, n_out = len(stage.ins), len(stage.outs)

    def body(*refs):
        plan = stage.make_plan(refs[:n_in], refs[n_in:n_in + n_out], *refs[n_in + n_out:])
        _start(plan)
        _finish(plan)

    return pl.pallas_call(
        body, name=name, in_specs=[ANY] * n_in, out_specs=[ANY] * n_out, out_shape=stage.outs,
        input_output_aliases=stage.aliases,
        scratch_shapes=[pltpu.SemaphoreType.DMA((stage.n_sems,)), pltpu.SemaphoreType.DMA((stage.n_sems,))],
        compiler_params=COMM,
    )(*stage.ins)


class _SemsFrom:
    def __init__(self, sems, base):
        self.sems, self.base = sems, base

    @property
    def at(self):
        return self

    def __getitem__(self, k):
        return self.sems.at[self.base + k]


def _both(a, b):
    na, nb = len(a.ins), len(b.ins)
    ma = len(a.outs)

    def make_plan(ins, outs, send_sems, recv_sems):
        pa = a.make_plan(ins[:na], outs[:ma], send_sems, recv_sems)
        pb = b.make_plan(ins[na:], outs[ma:], _SemsFrom(send_sems, a.n_sems), _SemsFrom(recv_sems, a.n_sems))
        return lambda sx, sy, sc: pa(sx, sy, sc) + pb(sx, sy, sc)

    aliases = dict(a.aliases)
    aliases.update({na + i: ma + o for i, o in b.aliases.items()})
    return _Stage(a.ins + b.ins, a.outs + b.outs, a.n_sems + b.n_sems, make_plan, aliases)


def _same(cp):
    return (cp, cp, cp)


def _stage_gather_send(fulls, axes):
    n = len(fulls)

    def make_plan(ins, outs, send_sems, recv_sems):
        def plan(sx, sy, sc):
            cps = []
            for w in range(n):
                mine = _piece_ref(outs[w], axes[w], 2 * sx + sy, sc)
                for r, (fx, fy) in enumerate(CHIP_RELS):
                    k = 3 * w + r
                    to = (sx ^ fx, sy ^ fy, sc)
                    got = _piece_ref(outs[w], axes[w], 2 * (sx ^ fx) + (sy ^ fy), sc)
                    send = _remote(mine, mine, send_sems.at[k], recv_sems.at[k], to)
                    cps.append((send, send, _remote(got, got, send_sems.at[k], recv_sems.at[k], to)))
            return cps

        return plan

    return _Stage(fulls, [SDS(f.shape, f.dtype) for f in fulls], 3 * n, make_plan, {i: i for i in range(n)})


def _stage_gather_forward(fulls, axes):
    n = len(fulls)

    def make_plan(ins, outs, send_sems, recv_sems):
        def plan(sx, sy, sc):
            cps = []
            sib = (sx, sy, 1 - sc)
            for w in range(n):
                for r, (fx, fy) in enumerate(CHIP_RELS):
                    k = 3 * w + r
                    pj = 2 * (sx ^ fx) + (sy ^ fy)
                    have = _piece_ref(outs[w], axes[w], pj, sc)
                    want = _piece_ref(outs[w], axes[w], pj, 1 - sc)
                    send = _remote(have, have, send_sems.at[k], recv_sems.at[k], sib)
                    cps.append((send, send, _remote(want, want, send_sems.at[k], recv_sems.at[k], sib)))
            return cps

        return plan

    return _Stage(fulls, [SDS(f.shape, f.dtype) for f in fulls], 3 * n, make_plan, {i: i for i in range(n)})


def _in_place(fulls, n_sems, make_plan):
    return _Stage(fulls, [SDS(f.shape, f.dtype) for f in fulls], n_sems, make_plan, {i: i for i in range(len(fulls))})


def _stage_gather_neighbours(fulls, axes):
    n = len(fulls)

    def make_plan(ins, outs, send_sems, recv_sems):
        def plan(sx, sy, sc):
            cps = []
            for w in range(n):
                mine = _piece_ref(outs[w], axes[w], 2 * sx + sy, sc)
                for r, (px, py) in enumerate(((sx ^ 1, sy), (sx, sy ^ 1))):
                    got = _piece_ref(outs[w], axes[w], 2 * px + py, sc)
                    send = _remote(mine, mine, send_sems.at[2 * w + r], recv_sems.at[2 * w + r], (px, py, sc))
                    cps.append((send, send, _remote(got, got, send_sems.at[2 * w + r], recv_sems.at[2 * w + r],
                                                    (px, py, sc))))
            return cps

        return plan

    return _in_place(fulls, 2 * n, make_plan)


def _stage_gather_pair(fulls, axes):
    n = len(fulls)

    def make_plan(ins, outs, send_sems, recv_sems):
        def plan(sx, sy, sc):
            cps = []
            sib = (sx, sy, 1 - sc)
            for w in range(n):
                for r, j in enumerate((2 * (sx ^ 1) + sy, 2 * sx + (sy ^ 1))):
                    k = 2 * w + r
                    have, want = _piece_ref(outs[w], axes[w], j, sc), _piece_ref(outs[w], axes[w], j, 1 - sc)
                    send = _remote(have, have, send_sems.at[k], recv_sems.at[k], sib)
                    cps.append((send, send, _remote(want, want, send_sems.at[k], recv_sems.at[k], sib)))
            return cps

        return plan

    return _in_place(fulls, 2 * n, make_plan)


def _stage_gather_diagonal(fulls, axes):
    n = len(fulls)

    def make_plan(ins, outs, send_sems, recv_sems):
        def relay(sx, sy, sc):
            cps = []
            jx, jy, jd = 2 * (sx ^ 1) + sy, 2 * sx + (sy ^ 1), 2 * (sx ^ 1) + (sy ^ 1)
            passed, to = (jx, (sx, sy ^ 1, sc)) if sc == 0 else (jy, (sx ^ 1, sy, sc))
            for w in range(n):
                have, want = _piece_ref(outs[w], axes[w], passed, sc), _piece_ref(outs[w], axes[w], jd, sc)
                send = _remote(have, have, send_sems.at[w], recv_sems.at[w], to)
                cps.append((send, send, _remote(want, want, send_sems.at[w], recv_sems.at[w], to)))
            return cps

        def cross(sx, sy, sc):
            cps = []
            sib = (sx, sy, 1 - sc)
            jd = 2 * (sx ^ 1) + (sy ^ 1)
            for w in range(n):
                have, want = _piece_ref(outs[w], axes[w], jd, sc), _piece_ref(outs[w], axes[w], jd, 1 - sc)
                send = _remote(have, have, send_sems.at[n + w], recv_sems.at[n + w], sib)
                cps.append((send, send, _remote(want, want, send_sems.at[n + w], recv_sems.at[n + w], sib)))
            return cps

        return [relay, cross]

    return _in_place(fulls, 2 * n, make_plan)


def _stage_pair_exchange(grads, axes):
    n = len(grads)

    def make_plan(gs, land, send_sems, recv_sems):
        def plan(sx, sy, sc):
            return [_same(_remote(_piece_ref(gs[w], axes[w], jj, 1 - sc), land[w].at[jj], send_sems.at[4 * w + jj],
                                  recv_sems.at[4 * w + jj], (sx, sy, 1 - sc)))
                    for w in range(n) for jj in range(4)]

        return plan

    return _Stage(grads, [SDS((4,) + _piece_shape(g.shape, a), g.dtype) for g, a in zip(grads, axes)], 4 * n,
                  make_plan)


def _stage_chip_scatter(pieces):
    n = len(pieces)

    def make_plan(ps, land, send_sems, recv_sems):
        def plan(sx, sy, sc):
            return [_same(_remote(ps[w].at[2 * (sx ^ fx) + (sy ^ fy)], land[w].at[r], send_sems.at[3 * w + r],
                                  recv_sems.at[3 * w + r], (sx ^ fx, sy ^ fy, sc)))
                    for w in range(n) for r, (fx, fy) in enumerate(CHIP_RELS)]

        return plan

    return _Stage(pieces, [SDS((3,) + p.shape[1:], p.dtype) for p in pieces], 3 * n, make_plan)


def _stage_pair_share(shards):
    n = len(shards)

    def make_plan(ins, outs, send_sems, recv_sems):
        def plan(sx, sy, sc):
            cps = []
            sib = (sx, sy, 1 - sc)
            for w in range(n):
                hr = shards[w].shape[0] // 2
                mine = outs[w].at[pl.ds(sc * hr, hr), :]
                theirs = outs[w].at[pl.ds((1 - sc) * hr, hr), :]
                send = _remote(mine, mine, send_sems.at[w], recv_sems.at[w], sib)
                cps.append((send, send, _remote(theirs, theirs, send_sems.at[w], recv_sems.at[w], sib)))
            return cps

        return plan

    return _Stage(shards, [SDS(g.shape, g.dtype) for g in shards], n, make_plan, {i: i for i in range(n)})


def _stage_gather_small(stack):
    def make_plan(ins, outs, send_sems, recv_sems):
        def plan(sx, sy, sc):
            mine = outs[0].at[4 * sx + 2 * sy + sc]
            return [_same(_remote(mine, mine, send_sems.at[k], recv_sems.at[k], (sx ^ fx, sy ^ fy, sc ^ fc)))
                    for k, (fx, fy, fc) in enumerate(DEV_RELS)]

        return plan

    return _Stage([stack], [SDS(stack.shape, stack.dtype)], 7, make_plan, {0: 0})


def kernel(x, w_in, conv_a_w, w_out_a, conv_b_w, conv_b_bias, ln_b_gamma, ln_b_beta, w_out_b, w_o, ln1_gamma, ln1_beta, w_up, w_down, ln2_gamma, ln2_beta, loss_target, m_w_in, m_conv_a_w, m_w_out_a, m_conv_b_w, m_conv_b_bias, m_ln_b_gamma, m_ln_b_beta, m_w_out_b, m_w_o, m_ln1_gamma, m_ln1_beta, m_w_up, m_w_down, m_ln2_gamma, m_ln2_beta, v_w_in, v_conv_a_w, v_w_out_a, v_conv_b_w, v_conv_b_bias, v_ln_b_gamma, v_ln_b_beta, v_w_out_b, v_w_o, v_ln1_gamma, v_ln1_beta, v_w_up, v_w_down, v_ln2_gamma, v_ln2_beta):
    s, d = x.shape[1], x.shape[2]
    xs = x.reshape(s, d)
    tgt = loss_target.reshape(s, d)
    dq = d // 4
    chip = 2 * lax.axis_index("x") + lax.axis_index("y")
    core = lax.axis_index("c")
    pos = jnp.stack([chip, core]).astype(jnp.int32)
    names = ("w_in", "w_out_a", "w_out_b", "w_o", "w_up", "w_down")
    axes = (1, 0, 0, 0, 1, 0)

    conv_pack = jnp.concatenate([jnp.pad(conv_a_w, ((0, 8 - K_A), (0, 0))), jnp.pad(conv_b_w, ((0, 32 - K_B), (0, 0))),
                                 jnp.zeros((8, dq), F32)], axis=0)
    conv_full = lax.dynamic_update_slice(jnp.zeros((conv_pack.shape[0], d), F32), conv_pack, (0, chip * dq))
    wi_own = _place_cast(w_in, 1, pos, "place_w_in")
    vec = lambda a: a.reshape(1, d)
    bias_b, lbg, lbb = vec(conv_b_bias), vec(ln_b_gamma), vec(ln_b_beta)
    l1g, l1b, l2g, l2b = vec(ln1_gamma), vec(ln1_beta), vec(ln2_gamma), vec(ln2_beta)

    (p, xb, *placed), landed = _in_proj_own(xs, w_in, (w_out_a, w_out_b, w_o, w_up, w_down), axes[1:], pos,
                                            _stage_gather_neighbours([wi_own, conv_full], (1, 1)))
    fulls = [wi_own] + placed
    wi, convs = _comm_call(_stage_gather_pair(landed, (1, 1)), "gather_pair_w_in")
    p, (wi, convs), small3 = _in_proj_rest(xb, wi, p, pos, _stage_gather_diagonal([wi, convs], (1, 1)),
                                           _stage_gather_send(fulls[1:4], axes[1:4]))
    cwa, cwb = convs[0:K_A], convs[8:8 + K_B]
    (conva, yapre, nb, rstdb, u3), landed = _conv_fwd(
        p, cwa, cwb, bias_b, lbg, lbb, d,
        _both(_stage_gather_forward(small3, axes[1:4]), _stage_gather_send(fulls[4:6], axes[4:6])))
    woa, wob, wo = landed[:3]
    (ya, yb, merged, n1, rstd1), (wup, wdown) = _mixer_out(yapre, u3, p, xs, woa, wob, wo, d,
                                                           _stage_gather_forward(landed[3:], axes[4:6]))
    x1b, hb, dhpre, dr2b, dr1, dr1b, acc_mlp = _mlp(n1, rstd1, tgt, wup, wdown, l1g, l1b, l2g, l2b)
    g_up, gb_up, _ = _grad_w(x1b, dhpre, "grad_w_up")
    g_down, gb_down, _ = _grad_w(hb, dr2b, "grad_w_down")
    dya, dyb, dba, dga, dgb, dca, du1, acc_mix = _mixer_bwd_local(dr1b, p, ya, yb, conva, nb, rstdb, wo, woa, wob,
                                                                   lbg, lbb)
    g_oa, gb_oa, _ = _grad_w(yapre, dya, "grad_w_out_a")
    g_ob, gb_ob, _ = _grad_w(u3, dyb, "grad_w_out_b")
    g_o, gb_o, _ = _grad_w(merged, dr1b, "grad_w_o")

    early, e_axes, e_names = [g_oa, g_ob, g_o, g_up, g_down], axes[1:], names[1:]
    (dp, dcwa, dcwb), land1 = _conv_bwd(dca, du1, p, dba, dga, dgb, cwa, cwb, d,
                                        _stage_pair_exchange([gb_oa, gb_ob, gb_o, gb_up, gb_down], e_axes))
    pieces = [_pair_add(g, l, a, pos, "pair_add_" + nm) for g, l, a, nm in zip(early, land1, e_axes, e_names)]
    pack = jnp.concatenate([dcwa, dcwb, acc_mix, acc_mlp], axis=0)
    stack = lax.dynamic_update_slice(jnp.zeros((8,) + pack.shape, F32), pack[None], (2 * chip + core, 0, 0))
    g_wi, gb_wi, landed = _grad_w(xb, dp, "grad_w_in",
                                  _both(_stage_chip_scatter(pieces), _stage_gather_small(stack)))
    land2, stack = landed[:-1], landed[-1]
    halves = [_chip_sum(g, l1, l2, a, pos, "chip_sum_" + nm)
              for g, l1, l2, a, nm in zip(early, land1, land2, e_axes, e_names)]
    (land1_in,) = _comm_call(_stage_pair_exchange([gb_wi], (1,)), "pair_exchange_w_in")
    piece_in = _pair_add(g_wi, land1_in, 1, pos, "pair_add_w_in")
    grad_x, landed = _grad_x(dr1, dp, wi, _both(_stage_chip_scatter([piece_in]), _stage_pair_share(halves)))
    land2_in, (g_oa, g_ob, g_o, g_up, g_down) = landed[0], landed[1:]
    half_in = _chip_sum(g_wi, land1_in, land2_in, 1, pos, "chip_sum_w_in")
    (g_in,) = _comm_call(_stage_pair_share([half_in]), "pair_share_w_in")
    small = _sum_parts(stack, "small_sum")
    g_ca = lax.dynamic_slice(small, (0, chip * dq), (K_A, dq))
    g_cb = lax.dynamic_slice(small, (8, chip * dq), (K_B, dq))
    g_vec = jnp.stack([small[r] for r in (42, 40, 41, 51, 52, 49, 50)])

    loss = (0.5 / d) * jnp.sum(small[48])

    big = {}
    for name, w, g, m, v in (("w_in", w_in, g_in, m_w_in, v_w_in), ("w_out_a", w_out_a, g_oa, m_w_out_a, v_w_out_a),
                             ("w_out_b", w_out_b, g_ob, m_w_out_b, v_w_out_b), ("w_o", w_o, g_o, m_w_o, v_w_o),
                             ("w_up", w_up, g_up, m_w_up, v_w_up), ("w_down", w_down, g_down, m_w_down, v_w_down),
                             ("conv_a_w", conv_a_w, g_ca, m_conv_a_w, v_conv_a_w),
                             ("conv_b_w", conv_b_w, g_cb, m_conv_b_w, v_conv_b_w)):
        big[name] = (g,) + tuple(_adamw(w, g, m, v, "adamw_" + name))
    vec_names = ("conv_b_bias", "ln_b_gamma", "ln_b_beta", "ln1_gamma", "ln1_beta", "ln2_gamma", "ln2_beta")
    w7 = jnp.stack([conv_b_bias, ln_b_gamma, ln_b_beta, ln1_gamma, ln1_beta, ln2_gamma, ln2_beta])
    m7 = jnp.stack([m_conv_b_bias, m_ln_b_gamma, m_ln_b_beta, m_ln1_gamma, m_ln1_beta, m_ln2_gamma, m_ln2_beta])
    v7 = jnp.stack([v_conv_b_bias, v_ln_b_gamma, v_ln_b_beta, v_ln1_gamma, v_ln1_beta, v_ln2_gamma, v_ln2_beta])
    d7, nm7, nv7 = _adamw(w7, g_vec, m7, v7, "adamw_vectors")
    for q, name in enumerate(vec_names):
        big[name] = (g_vec[q], d7[q], nm7[q], nv7[q])

    order = ("w_in", "conv_a_w", "w_out_a", "conv_b_w", "conv_b_bias", "ln_b_gamma", "ln_b_beta", "w_out_b", "w_o",
             "ln1_gamma", "ln1_beta", "w_up", "w_down", "ln2_gamma", "ln2_beta")
    outs = [loss, grad_x.reshape(x.shape)]
    for part in range(4):
        outs += [big[name][part] for name in order]
    return tuple(outs)
```

```python
import jax
import jax.numpy as jnp
from jax import lax
from jax.experimental import pallas as pl
from jax.experimental.pallas import tpu as pltpu

F32 = jnp.float32
BF = jnp.bfloat16
SDS = jax.ShapeDtypeStruct
MESH = pl.DeviceIdType.MESH

ALPHA = 2.0 ** 0.25
LN_EPS = 1e-5
K_A = 3
K_B = 31
HALO = 32
CONV_TILE = 512
CONV_ROWS = 64
CONV_LANES = 128
VMEM_LIMIT_MB = 56
PROJ_TILE = 1024
MIXER_TILE = 512
GRAD_X_TILE = 1024
LOCAL_TILE = 256
FF_CHUNK = 1024
GRAD_W_TILE = 1024
GRAD_W_TOKENS = 2048
GRAD_X_K = 3584
ROWS_TILE = 256
ADAM_LR = 0.001
ADAM_B1 = 0.9
ADAM_B2 = 0.999
ADAM_EPS = 1e-08
ADAM_WD = 0.01
ADAM_STEP = 10
P_DT = BF
CHIP_RELS = ((1, 0), (0, 1), (1, 1))
DEV_RELS = tuple((fx, fy, fc) for fx in (0, 1) for fy in (0, 1) for fc in (0, 1))[1:]


def _cp(sem=None, side_effects=False):
    return pltpu.CompilerParams(dimension_semantics=sem, vmem_limit_bytes=VMEM_LIMIT_MB << 20,
                                has_side_effects=side_effects)


def _const(shape):
    return pl.BlockSpec(shape, lambda *_: (0,) * len(shape), pipeline_mode=pl.Buffered(1))


def _sig(v):
    return jax.nn.sigmoid(v)


def _ln_fwd(r):
    mu = jnp.mean(r, axis=-1, keepdims=True)
    xc = r - mu
    var = jnp.mean(xc * xc, axis=-1, keepdims=True)
    rstd = lax.rsqrt(var + LN_EPS)
    return xc * rstd, rstd


def _ln_bwd(dn, n, rstd):
    m1 = jnp.mean(dn, axis=-1, keepdims=True)
    m2 = jnp.mean(dn * n, axis=-1, keepdims=True)
    return rstd * (dn - m1 - n * m2)


def _dot(a, b):
    return jnp.dot(a, b, preferred_element_type=F32)


def _dot_nt(a, b):
    return lax.dot_general(a, b, (((1,), (1,)), ((), ())), preferred_element_type=F32)


def _dot_tn(a, b):
    return lax.dot_general(a, b, (((0,), (0,)), ((), ())), preferred_element_type=F32)


def _tile(n, pref):
    if n <= pref:
        return n
    return max(t for t in range(128, pref + 1, 128) if n % t == 0)


def _rowsum8(v):
    acc = v[0:8]
    for g in range(1, v.shape[0] // 8):
        acc = acc + v[8 * g:8 * g + 8]
    return acc


def _taps(win, offsets, rows):
    r_all = win.shape[0]
    by_res = {}
    for k, o in enumerate(offsets):
        by_res.setdefault(o % 8, []).append((k, o // 8))
    for s, taps in sorted(by_res.items()):
        r = win if s == 0 else pltpu.roll(win, r_all - s, 0)
        for k, q in taps:
            yield k, r[8 * q:8 * q + rows]


CAUSAL_A = [HALO - (K_A - 1) + k for k in range(K_A)]
CAUSAL_B = [HALO - (K_B - 1) + k for k in range(K_B)]
ANTI_A = [K_A - 1 - k for k in range(K_A)]
ANTI_B = [K_B - 1 - k for k in range(K_B)]


def _host_call(body, *, name, grid, in_specs, out_specs, out_shape, scratch_shapes, args, hosted, prefetch=None,
               aliases=None, body_gets_stage_refs=False):
    n_in, n_out, n_scr = len(in_specs), len(out_specs), len(scratch_shapes)
    n_steps = 1
    for size in grid:
        n_steps *= size
    if isinstance(hosted, _Stage):
        hosted = [(hosted, 0, n_steps - 1)]
    n_pre = 0 if prefetch is None else 1
    pre = () if prefetch is None else (prefetch,)
    sem = ("arbitrary",) * len(grid)
    own_aliases = {n_pre + a: b for a, b in (aliases or {}).items()}

    def call(kernel_body, ins, outs, shapes, scratch, all_aliases, side_effects, operands):
        gs = pltpu.PrefetchScalarGridSpec(num_scalar_prefetch=n_pre, grid=grid, in_specs=ins, out_specs=outs,
                                          scratch_shapes=scratch)
        return pl.pallas_call(kernel_body, name=name, grid_spec=gs, out_shape=shapes,
                              input_output_aliases=all_aliases,
                              compiler_params=_cp(sem, side_effects=side_effects))(*pre, *operands)

    if hosted is None:
        res = call(body, list(in_specs), list(out_specs), list(out_shape), list(scratch_shapes), own_aliases, False,
                   args)
        return list(res), []
    stages = [st for st, _, _ in hosted]
    h_ins = [a for st in stages for a in st.ins]
    h_outs = [o for st in stages for o in st.outs]
    borrowed = {hi: k for hi, a in enumerate(h_ins) for k, own in enumerate(args) if a is own}
    passed = [hi for hi in range(len(h_ins)) if hi not in borrowed]
    h_in, h_out, n_sems = len(passed), len(h_outs), sum(st.n_sems for st in stages)

    def full_body(*refs):
        pre_refs, refs = refs[:n_pre], refs[n_pre:]
        ins, refs = refs[:n_in], refs[n_in:]
        hins = [None] * len(h_ins)
        for hi, ref in zip(passed, refs[:h_in]):
            hins[hi] = ref
        refs = refs[h_in:]
        outs, refs = refs[:n_out], refs[n_out:]
        houts, refs = refs[:h_out], refs[h_out:]
        scr, (send_sems, recv_sems) = refs[:n_scr], refs[n_scr:]
        step = 0
        for a, size in enumerate(grid):
            step = step * size + pl.program_id(a)
        phases, i0, o0, k0 = [], 0, 0, 0
        for st, starts, finishes in hosted:
            plans = st.make_plan(hins[i0:i0 + len(st.ins)], houts[o0:o0 + len(st.outs)],
                                 _SemsFrom(send_sems, k0), _SemsFrom(recv_sems, k0))
            if not isinstance(plans, list):
                plans, starts, finishes = [plans], [starts], [finishes]
            phases += list(zip(plans, starts, finishes))
            i0, o0, k0 = i0 + len(st.ins), o0 + len(st.outs), k0 + st.n_sems
        for plan, at, _ in phases:
            @pl.when(step == at)
            def _(plan=plan):
                _start(plan)

        if body_gets_stage_refs:
            body(*pre_refs, *ins, *outs, *scr, houts)
        else:
            body(*pre_refs, *ins, *outs, *scr)

        for plan, _, at in phases:
            @pl.when(step == at)
            def _(plan=plan):
                _finish(plan)

    all_aliases = dict(own_aliases)
    i0 = o0 = 0
    for st in stages:
        for a, b in st.aliases.items():
            hi = i0 + a
            operand = borrowed[hi] if hi in borrowed else n_in + passed.index(hi)
            all_aliases[n_pre + operand] = n_out + o0 + b
        i0, o0 = i0 + len(st.ins), o0 + len(st.outs)
    res = call(full_body, list(in_specs) + [ANY] * h_in, list(out_specs) + [ANY] * h_out,
               list(out_shape) + h_outs,
               list(scratch_shapes) + [pltpu.SemaphoreType.DMA((n_sems,)), pltpu.SemaphoreType.DMA((n_sems,))],
               all_aliases, True, (*args, *[h_ins[hi] for hi in passed]))
    return list(res[:n_out]), list(res[n_out:])


def _in_proj_own(x, w_shard, others, other_axes, pos, hosted):
    s, d = x.shape
    tn = w_shard.shape[1]
    tm = min(s, PROJ_TILE)
    n_o = len(others)

    def body(pos_ref, x_ref, w_ref, *refs):
        o_in, (p_ref, xb_ref), o_out, wb = refs[:n_o], refs[n_o:n_o + 2], refs[n_o + 2:2 * n_o + 2], refs[-1]

        @pl.when(pl.program_id(0) == 0)
        def _():
            wb[...] = w_ref[...].astype(BF)
            for src, dst in zip(o_in, o_out):
                dst[...] = src[...].astype(BF)

        xb = x_ref[...].astype(BF)
        xb_ref[...] = xb
        p_ref[...] = _dot(xb, wb[...]).astype(p_ref.dtype)

    whole = lambda a: pl.BlockSpec(a.shape, lambda i, pos: (0, 0), pipeline_mode=pl.Buffered(1))
    placed = lambda a, ax: pl.BlockSpec(a.shape, (lambda i, pos: (0, pos[0])) if ax == 1 else (lambda i, pos: (pos[0], 0)))
    full = lambda a, ax: SDS((a.shape[0], 4 * a.shape[1]) if ax == 1 else (4 * a.shape[0], a.shape[1]), BF)
    outs, extra = _host_call(
        body, name="in_proj_own", grid=(s // tm,), prefetch=pos,
        in_specs=[pl.BlockSpec((tm, d), lambda i, pos: (i, 0)), whole(w_shard)] + [whole(a) for a in others],
        out_specs=[pl.BlockSpec((tm, tn), lambda i, pos: (i, pos[0])), pl.BlockSpec((tm, d), lambda i, pos: (i, 0))]
        + [placed(a, ax) for a, ax in zip(others, other_axes)],
        out_shape=[SDS((s, 4 * tn), P_DT), SDS((s, d), BF)] + [full(a, ax) for a, ax in zip(others, other_axes)],
        scratch_shapes=[pltpu.VMEM((d, tn), BF)], args=(x, w_shard, *others), hosted=hosted)
    return outs, extra


def _in_proj_rest(xb, wi, p, pos, diagonal, other):
    s, d = xb.shape
    n = wi.shape[1]
    tm, tn = min(s // 4, PROJ_TILE), n // 4
    ni = s // tm

    def body(pos_ref, xb_ref, w_ref, p_in, p_ref, wdiag, dsem, stage_outs):
        j = pl.program_id(0)
        step = j * ni + pl.program_id(1)

        def diagonal_block(act):
            for chip in range(4):
                @pl.when(pos_ref[0] == chip)
                def _(chip=chip):
                    act(pltpu.make_async_copy(stage_outs[0].at[:, pl.ds((chip ^ 3) * tn, tn)], wdiag, dsem))

        @pl.when(step == 2 * ni - 1)
        def _():
            diagonal_block(lambda cp: cp.start())

        @pl.when(step == 2 * ni)
        def _():
            diagonal_block(lambda cp: cp.wait())

        @pl.when(j < 2)
        def _():
            p_ref[...] = _dot(xb_ref[...], w_ref[...]).astype(p_ref.dtype)

        @pl.when(j == 2)
        def _():
            p_ref[...] = _dot(xb_ref[...], wdiag[...]).astype(p_ref.dtype)

    def col(j, i, pos):
        return lax.bitwise_xor(pos[0], jnp.where(j == 0, 2, jnp.where(j == 1, 1, 3)))

    def piped_col(j, i, pos):
        return lax.bitwise_xor(pos[0], jnp.where(j == 0, 2, 1))

    (p,), extra = _host_call(
        body, name="in_proj_rest", grid=(3, ni), prefetch=pos, aliases={2: 0}, body_gets_stage_refs=True,
        in_specs=[pl.BlockSpec((tm, d), lambda j, i, pos: (i, 0)),
                  pl.BlockSpec((d, tn), lambda j, i, pos: (0, piped_col(j, i, pos))), ANY],
        out_specs=[pl.BlockSpec((tm, tn), lambda j, i, pos: (i, col(j, i, pos)))],
        out_shape=[SDS((s, n), P_DT)], scratch_shapes=[pltpu.VMEM((d, tn), BF), pltpu.SemaphoreType.DMA(())],
        args=(xb, wi, p),
        hosted=[(diagonal, [0, ni + 1], [ni, 2 * ni - 2]), (other, 0, 3 * ni - 1)])
    n_d = len(diagonal.outs)
    return p, extra[:n_d], extra[n_d:]


def _col_spec(tm, d, k):
    return pl.BlockSpec((tm, d), lambda i, k=k: (i, k))


def _prev_halo_spec(tm, d, k):
    r = tm // HALO
    return pl.BlockSpec((HALO, d), lambda i, k=k: (jnp.maximum(i * r - 1, 0), k))


def _next_halo_spec(tm, d, k, s):
    r = tm // HALO
    last = s // HALO - 1
    return pl.BlockSpec((HALO, d), lambda i, k=k: (jnp.minimum((i + 1) * r, last), k))


def _conv_fwd(p, cwa, cwb, bias_b, lbg, lbb, d, hosted=None):
    s = p.shape[0]
    tm = min(s, CONV_TILE)
    nt = s // tm

    def body(ba_ref, ca_ref, va_ref, vb_ref, gb_ref, hca_ref, hva_ref, hvb_ref, hgb_ref,
             cwa_ref, cwb_ref, bias_ref, lbg_ref, lbb_ref,
             conva_ref, yapre_ref, nb_ref, rstdb_ref, u3_ref,
             zbuf, ubuf, u1buf):
        i = pl.program_id(0)
        keep = (i > 0).astype(F32)
        zbuf[pl.ds(0, HALO), :] = hca_ref[...].astype(F32) * hva_ref[...].astype(F32) * keep
        ubuf[pl.ds(0, HALO), :] = hvb_ref[...].astype(F32) * _sig(hgb_ref[...].astype(F32)) * keep
        zbuf[pl.ds(HALO, tm), :] = ca_ref[...].astype(F32) * va_ref[...].astype(F32)
        ubuf[pl.ds(HALO, tm), :] = vb_ref[...].astype(F32) * _sig(gb_ref[...].astype(F32))

        def chunk(j, carry):
            r0 = pl.multiple_of(j * CONV_ROWS, CONV_ROWS)
            rows = pl.ds(r0, CONV_ROWS)
            for lc in range(d // CONV_LANES):
                ls = pl.ds(lc * CONV_LANES, CONV_LANES)
                acc = jnp.zeros((CONV_ROWS, CONV_LANES), F32)
                for k, sl in _taps(zbuf[pl.ds(r0, CONV_ROWS + HALO), ls], CAUSAL_A, CONV_ROWS):
                    acc = acc + cwa_ref[pl.ds(k, 1), ls] * sl
                conva_ref[rows, ls] = acc.astype(BF)
                yapre_ref[rows, ls] = (ba_ref[rows, ls].astype(F32) * acc).astype(BF)
                acc = jnp.zeros((CONV_ROWS, CONV_LANES), F32)
                for k, sl in _taps(ubuf[pl.ds(r0, CONV_ROWS + HALO), ls], CAUSAL_B, CONV_ROWS):
                    acc = acc + cwb_ref[pl.ds(k, 1), ls] * sl
                u1buf[rows, ls] = acc + bias_ref[:, ls]
            return carry

        lax.fori_loop(0, tm // CONV_ROWS, chunk, 0)
        nb, rstd = _ln_fwd(u1buf[...])
        nb_ref[...] = nb
        rstdb_ref[...] = rstd
        u2 = nb * lbg_ref[...] + lbb_ref[...]
        u3_ref[...] = (u2 * _sig(u2)).astype(BF)

    vec = _const((1, d))
    return _host_call(
        body, name="conv_fwd", grid=(nt,),
        in_specs=[_col_spec(tm, d, k) for k in range(5)] + [_prev_halo_spec(tm, d, k) for k in (1, 2, 3, 4)]
        + [_const((K_A, d)), _const((K_B, d)), vec, vec, vec],
        out_specs=[pl.BlockSpec((tm, d), lambda i: (i, 0)), pl.BlockSpec((tm, d), lambda i: (i, 0)),
                   pl.BlockSpec((tm, d), lambda i: (i, 0)), pl.BlockSpec((tm, 1), lambda i: (i, 0)),
                   pl.BlockSpec((tm, d), lambda i: (i, 0))],
        out_shape=[SDS((s, d), BF), SDS((s, d), BF), SDS((s, d), F32), SDS((s, 1), F32), SDS((s, d), BF)],
        scratch_shapes=[pltpu.VMEM((HALO + tm, d), F32), pltpu.VMEM((HALO + tm, d), F32), pltpu.VMEM((tm, d), F32)],
        args=(p, p, p, p, p, p, p, p, p, cwa, cwb, bias_b, lbg, lbb), hosted=hosted)


def _mixer_out(yapre, u3, p, x, woa, wob, wo, d, hosted=None):
    s = x.shape[0]
    tm = min(s, MIXER_TILE)

    def body(yapre_ref, u3_ref, ga_ref, gb_ref, x_ref, woa_ref, wob_ref, wo_ref,
             ya_ref, yb_ref, merged_ref, n1_ref, rstd1_ref):
        ya = _dot(yapre_ref[...], woa_ref[...])
        yb = _dot(u3_ref[...], wob_ref[...])
        ya_ref[...] = ya.astype(BF)
        yb_ref[...] = yb.astype(BF)
        merged = (_sig(ga_ref[...].astype(F32)) * ya + _sig(gb_ref[...].astype(F32)) * yb).astype(BF)
        merged_ref[...] = merged
        r1 = F32(ALPHA) * x_ref[...] + _dot(merged, wo_ref[...])
        n1, rstd1 = _ln_fwd(r1)
        n1_ref[...] = n1
        rstd1_ref[...] = rstd1

    row = pl.BlockSpec((tm, d), lambda i: (i, 0))
    return _host_call(
        body, name="mixer_out", grid=(s // tm,),
        in_specs=[row, row, _col_spec(tm, d, 5), _col_spec(tm, d, 6), row,
                  _const((d, d)), _const((d, d)), _const((d, d))],
        out_specs=[row, row, row, row, pl.BlockSpec((tm, 1), lambda i: (i, 0))],
        out_shape=[SDS((s, d), BF), SDS((s, d), BF), SDS((s, d), BF), SDS((s, d), F32), SDS((s, 1), F32)],
        scratch_shapes=[], args=(yapre, u3, p, p, x, woa, wob, wo), hosted=hosted)


def _mlp(n1, rstd1, tgt, wup, wdown, l1g, l1b, l2g, l2b):
    s, d = n1.shape
    dff = wup.shape[1]
    tm = min(s, LOCAL_TILE)
    fc = min(dff, FF_CHUNK)
    nq = dff // fc

    def body(n1_ref, rstd1_ref, tgt_ref, wup_ref, wdown_ref, l1g_ref, l1b_ref, l2g_ref, l2b_ref,
             x1b_ref, hb_ref, dhpre_ref, dr2b_ref, dr1_ref, dr1b_ref, acc_ref, rbuf):
        i = pl.program_id(0)
        n1v = n1_ref[...]
        x1 = n1v * l1g_ref[...] + l1b_ref[...]
        x1b = x1.astype(BF)
        x1b_ref[...] = x1b
        ff = jnp.zeros((tm, d), F32)
        for q in range(nq):
            cs = pl.ds(q * fc, fc)
            r = jnp.maximum(_dot(x1b, wup_ref[:, cs]), 0.0)
            rbuf[:, cs] = r
            hq = (r * r).astype(BF)
            hb_ref[:, cs] = hq
            ff = ff + _dot(hq, wdown_ref[cs, :])
        n2, rstd2 = _ln_fwd(F32(ALPHA) * x1 + ff)
        x2 = n2 * l2g_ref[...] + l2b_ref[...]
        err = x2 - tgt_ref[...]
        dx2 = err * F32(1.0 / d)
        dr2 = _ln_bwd(dx2 * l2g_ref[...], n2, rstd2)
        dr2b = dr2.astype(BF)
        dr2b_ref[...] = dr2b
        dx1 = F32(ALPHA) * dr2
        for q in range(nq):
            cs = pl.ds(q * fc, fc)
            dh = _dot_nt(dr2b, wdown_ref[cs, :])
            dhp = (dh * (2.0 * rbuf[:, cs])).astype(BF)
            dhpre_ref[:, cs] = dhp
            dx1 = dx1 + _dot_nt(dhp, wup_ref[:, cs])
        dr1 = _ln_bwd(dx1 * l1g_ref[...], n1v, rstd1_ref[...])
        dr1_ref[...] = dr1
        dr1b_ref[...] = dr1.astype(BF)

        @pl.when(i == 0)
        def _():
            acc_ref[...] = jnp.zeros_like(acc_ref)

        for q, val in enumerate((err * err, dx2 * n2, dx2, dx1 * n1v, dx1)):
            acc_ref[pl.ds(q, 1), :] += jnp.sum(val, axis=0, keepdims=True)

    row = pl.BlockSpec((tm, d), lambda i: (i, 0))
    wide = pl.BlockSpec((tm, dff), lambda i: (i, 0))
    vec = _const((1, d))
    return pl.pallas_call(
        body, name="mlp_fwd_bwd", grid=(s // tm,),
        in_specs=[row, pl.BlockSpec((tm, 1), lambda i: (i, 0)), row, _const((d, dff)), _const((dff, d)),
                  vec, vec, vec, vec],
        out_specs=[row, wide, wide, row, row, row, pl.BlockSpec((8, d), lambda i: (0, 0))],
        out_shape=[SDS((s, d), BF), SDS((s, dff), BF), SDS((s, dff), BF), SDS((s, d), BF), SDS((s, d), F32),
                   SDS((s, d), BF), SDS((8, d), F32)],
        scratch_shapes=[pltpu.VMEM((tm, dff), F32)],
        compiler_params=_cp(("arbitrary",)),
    )(n1, rstd1, tgt, wup, wdown, l1g, l1b, l2g, l2b)


def _mixer_bwd_local(dr1b, p, ya, yb, conva, nb, rstdb, wo, woa, wob, lbg, lbb):
    s, d = ya.shape
    tm = min(s, LOCAL_TILE)

    def body(dr1b_ref, ba_ref, ga_ref, gb_ref, ya_ref, yb_ref, conva_ref, nb_ref, rstdb_ref,
             wo_ref, woa_ref, wob_ref, lbg_ref, lbb_ref,
             dya_ref, dyb_ref, dba_ref, dga_ref, dgb_ref, dca_ref, du1_ref, acc_ref):
        i = pl.program_id(0)
        dmerged = _dot_nt(dr1b_ref[...], wo_ref[...])
        sa = _sig(ga_ref[...].astype(F32))
        sb = _sig(gb_ref[...].astype(F32))
        dya = (dmerged * sa).astype(BF)
        dyb = (dmerged * sb).astype(BF)
        dya_ref[...] = dya
        dyb_ref[...] = dyb
        dga_ref[...] = (dmerged * ya_ref[...].astype(F32) * (sa * (1.0 - sa))).astype(BF)
        dgb_ref[...] = (dmerged * yb_ref[...].astype(F32) * (sb * (1.0 - sb))).astype(BF)
        dyapre = _dot_nt(dya, woa_ref[...])
        dba_ref[...] = (dyapre * conva_ref[...].astype(F32)).astype(BF)
        dca_ref[...] = (dyapre * ba_ref[...].astype(F32)).astype(BF)
        du3 = _dot_nt(dyb, wob_ref[...])
        nbv = nb_ref[...]
        u2 = nbv * lbg_ref[...] + lbb_ref[...]
        sg = _sig(u2)
        du2 = du3 * (sg * (1.0 + u2 * (1.0 - sg)))
        du1 = _ln_bwd(du2 * lbg_ref[...], nbv, rstdb_ref[...])
        du1_ref[...] = du1.astype(BF)

        @pl.when(i == 0)
        def _():
            acc_ref[...] = jnp.zeros_like(acc_ref)

        for q, val in enumerate((du2 * nbv, du2, du1)):
            acc_ref[pl.ds(q, 1), :] += jnp.sum(val, axis=0, keepdims=True)

    row = pl.BlockSpec((tm, d), lambda i: (i, 0))
    vec = _const((1, d))
    return pl.pallas_call(
        body, name="mixer_bwd_local", grid=(s // tm,),
        in_specs=[row, _col_spec(tm, d, 0), _col_spec(tm, d, 5), _col_spec(tm, d, 6), row, row, row, row,
                  pl.BlockSpec((tm, 1), lambda i: (i, 0)), _const((d, d)), _const((d, d)), _const((d, d)), vec, vec],
        out_specs=[row, row, row, row, row, row, row, pl.BlockSpec((8, d), lambda i: (0, 0))],
        out_shape=[SDS((s, d), BF)] * 7 + [SDS((8, d), F32)],
        compiler_params=_cp(("arbitrary",)),
    )(dr1b, p, p, p, ya, yb, conva, nb, rstdb, wo, woa, wob, lbg, lbb)


def _conv_bwd(dca, du1, p, dba, dga, dgb, cwa, cwb, d, hosted=None):
    s = dca.shape[0]
    tm = min(s, CONV_TILE)
    nt = s // tm

    def body(dca_ref, du1_ref, ndca_ref, ndu1_ref, ca_ref, va_ref, vb_ref, gb_ref,
             dba_ref, dga_ref, dgb_ref, cwa_ref, cwb_ref,
             dp_ref, dcwa_ref, dcwb_ref,
             dcabuf, du1buf, sgbuf, acca, accb):
        i = pl.program_id(0)
        keep_next = (i < nt - 1).astype(F32)

        @pl.when(i == 0)
        def _():
            acca[...] = jnp.zeros_like(acca)
            accb[...] = jnp.zeros_like(accb)

        sgbuf[...] = _sig(gb_ref[...].astype(F32))
        dcabuf[pl.ds(0, tm), :] = dca_ref[...].astype(F32)
        dcabuf[pl.ds(tm, HALO), :] = ndca_ref[...].astype(F32) * keep_next
        du1buf[pl.ds(0, tm), :] = du1_ref[...].astype(F32)
        du1buf[pl.ds(tm, HALO), :] = ndu1_ref[...].astype(F32) * keep_next
        dp_ref[:, pl.ds(0, d)] = dba_ref[...]
        dp_ref[:, pl.ds(5 * d, d)] = dga_ref[...]
        dp_ref[:, pl.ds(6 * d, d)] = dgb_ref[...]

        def chunk(j, carry):
            r0 = pl.multiple_of(j * CONV_ROWS, CONV_ROWS)
            rows = pl.ds(r0, CONV_ROWS)
            for lc in range(d // CONV_LANES):
                lo = lc * CONV_LANES
                ls = pl.ds(lo, CONV_LANES)
                cac = ca_ref[rows, ls].astype(F32)
                vac = va_ref[rows, ls].astype(F32)
                zc = cac * vac
                acc = jnp.zeros((CONV_ROWS, CONV_LANES), F32)
                for k, sl in _taps(dcabuf[pl.ds(r0, CONV_ROWS + HALO), ls], ANTI_A, CONV_ROWS):
                    acc = acc + cwa_ref[pl.ds(k, 1), ls] * sl
                    acca[pl.ds(8 * k, 8), ls] += _rowsum8(sl * zc)
                dp_ref[rows, pl.ds(d + lo, CONV_LANES)] = (acc * vac).astype(BF)
                dp_ref[rows, pl.ds(2 * d + lo, CONV_LANES)] = (acc * cac).astype(BF)
                sgc = sgbuf[rows, ls]
                vbc = vb_ref[rows, ls].astype(F32)
                uc = vbc * sgc
                acc = jnp.zeros((CONV_ROWS, CONV_LANES), F32)
                for k, sl in _taps(du1buf[pl.ds(r0, CONV_ROWS + HALO), ls], ANTI_B, CONV_ROWS):
                    acc = acc + cwb_ref[pl.ds(k, 1), ls] * sl
                    accb[pl.ds(8 * k, 8), ls] += _rowsum8(sl * uc)
                dp_ref[rows, pl.ds(3 * d + lo, CONV_LANES)] = (acc * sgc).astype(BF)
                dp_ref[rows, pl.ds(4 * d + lo, CONV_LANES)] = (acc * vbc * (sgc * (1.0 - sgc))).astype(BF)
            return carry

        lax.fori_loop(0, tm // CONV_ROWS, chunk, 0)

        @pl.when(i == nt - 1)
        def _():
            dcwa_ref[...] = jnp.zeros_like(dcwa_ref)
            dcwb_ref[...] = jnp.zeros_like(dcwb_ref)
            for k in range(K_A):
                dcwa_ref[pl.ds(k, 1), :] = jnp.sum(acca[pl.ds(8 * k, 8), :], axis=0, keepdims=True)
            for k in range(K_B):
                dcwb_ref[pl.ds(k, 1), :] = jnp.sum(accb[pl.ds(8 * k, 8), :], axis=0, keepdims=True)

    row = pl.BlockSpec((tm, d), lambda i: (i, 0))
    nxt = _next_halo_spec(tm, d, 0, s)
    return _host_call(
        body, name="conv_bwd", grid=(nt,),
        in_specs=[row, row, nxt, nxt] + [_col_spec(tm, d, k) for k in (1, 2, 3, 4)]
        + [row, row, row, _const((K_A, d)), _const((K_B, d))],
        out_specs=[pl.BlockSpec((tm, 7 * d), lambda i: (i, 0)), pl.BlockSpec((8, d), lambda i: (0, 0)),
                   pl.BlockSpec((32, d), lambda i: (0, 0))],
        out_shape=[SDS((s, 7 * d), BF), SDS((8, d), F32), SDS((32, d), F32)],
        scratch_shapes=[pltpu.VMEM((tm + HALO, d), F32), pltpu.VMEM((tm + HALO, d), F32),
                        pltpu.VMEM((tm, d), F32), pltpu.VMEM((8 * K_A, d), F32), pltpu.VMEM((8 * K_B, d), F32)],
        args=(dca, du1, dca, du1, p, p, p, p, dba, dga, dgb, cwa, cwb), hosted=hosted)


def _grad_w(a, b, name, hosted=None):
    s, m = a.shape
    n = b.shape[1]
    tm, tn, tk = _tile(m, GRAD_W_TILE), _tile(n, GRAD_W_TILE), _tile(s, GRAD_W_TOKENS)
    nk = s // tk

    def body(a_ref, b_ref, o_ref, ob_ref):
        k = pl.program_id(2)

        @pl.when(k == 0)
        def _():
            o_ref[...] = jnp.zeros_like(o_ref)

        o_ref[...] += _dot_tn(a_ref[...], b_ref[...])

        @pl.when(k == nk - 1)
        def _():
            ob_ref[...] = o_ref[...].astype(BF)

    blk = pl.BlockSpec((tm, tn), lambda i, j, k: (i, j))
    (g, gb), extra = _host_call(
        body, name=name, grid=(m // tm, n // tn, nk),
        in_specs=[pl.BlockSpec((tk, tm), lambda i, j, k: (k, i)), pl.BlockSpec((tk, tn), lambda i, j, k: (k, j))],
        out_specs=[blk, blk], out_shape=[SDS((m, n), F32), SDS((m, n), BF)], scratch_shapes=[], args=(a, b),
        hosted=hosted)
    return g, gb, extra


def _grad_x(dr1, dp, wi, hosted=None):
    s, d = dr1.shape
    n = wi.shape[1]
    tm, tk = min(s, GRAD_X_TILE), _tile(n, GRAD_X_K)

    def body(dr1_ref, dp_ref, w_ref, o_ref):
        @pl.when(pl.program_id(1) == 0)
        def _():
            o_ref[...] = F32(ALPHA) * dr1_ref[...]

        o_ref[...] += _dot_nt(dp_ref[...], w_ref[...])

    (gx,), extra = _host_call(
        body, name="grad_x", grid=(s // tm, n // tk),
        in_specs=[pl.BlockSpec((tm, d), lambda i, k: (i, 0)), pl.BlockSpec((tm, tk), lambda i, k: (i, k)),
                  pl.BlockSpec((d, tk), lambda i, k: (0, k))],
        out_specs=[pl.BlockSpec((tm, d), lambda i, k: (i, 0))],
        out_shape=[SDS((s, d), F32)], scratch_shapes=[], args=(dr1, dp, wi), hosted=hosted)
    return gx, extra


def _adamw_math(w, g, m, v):
    m2 = ADAM_B1 * m + (1.0 - ADAM_B1) * g
    v2 = ADAM_B2 * v + (1.0 - ADAM_B2) * (g * g)
    m_hat = m2 / (1.0 - ADAM_B1 ** ADAM_STEP)
    v_hat = v2 / (1.0 - ADAM_B2 ** ADAM_STEP)
    delta = -ADAM_LR * (m_hat / (jnp.sqrt(v_hat) + ADAM_EPS) + ADAM_WD * w)
    return delta, m2, v2


def _adamw(w, g, m, v, name):
    r, c = w.shape
    tr = min(r, ROWS_TILE)

    def body(w_ref, g_ref, m_ref, v_ref, g_out, d_ref, m2_ref, v2_ref):
        gv = g_ref[...]
        delta, m2, v2 = _adamw_math(w_ref[...], gv, m_ref[...], v_ref[...])
        g_out[...] = gv
        d_ref[...] = delta
        m2_ref[...] = m2
        v2_ref[...] = v2

    blk = pl.BlockSpec((tr, c), lambda i: (i, 0))
    return pl.pallas_call(
        body, name=name, grid=(r // tr,), in_specs=[blk] * 4, out_specs=[blk] * 4,
        out_shape=[SDS((r, c), F32)] * 4, compiler_params=_cp(("parallel",)),
    )(w, g, m, v)


def _sum_parts(parts, name):
    k, r, c = parts.shape

    def body(p_ref, o_ref):
        acc = p_ref[0]
        for q in range(1, k):
            acc = acc + p_ref[q]
        o_ref[...] = acc

    return pl.pallas_call(
        body, name=name, grid=(1,),
        in_specs=[pl.BlockSpec((k, r, c), lambda i: (0, 0, 0))],
        out_specs=pl.BlockSpec((r, c), lambda i: (0, 0)),
        out_shape=SDS((r, c), F32), compiler_params=_cp(("arbitrary",)),
    )(parts)


def _piece_shape(full_shape, axis):
    r, c = full_shape
    return (r // 2, c // 4) if axis == 1 else (r // 8, c)


def _piece_spec(full_shape, axis, tr, chip_of, half_of):
    hr, wc = _piece_shape(full_shape, axis)
    nb = hr // tr
    if axis == 1:
        return pl.BlockSpec((tr, wc), lambda *a: (half_of(*a) * nb + a[-2], chip_of(*a)))
    return pl.BlockSpec((tr, wc), lambda *a: ((2 * chip_of(*a) + half_of(*a)) * nb + a[-2], 0))


def _place_cast(w, axis, pos, name):
    r, c = w.shape
    tr = min(r, ROWS_TILE)
    nb = r // tr
    full = (r, 4 * c) if axis == 1 else (4 * r, c)
    out_map = (lambda i, pos: (i, pos[0])) if axis == 1 else (lambda i, pos: (pos[0] * nb + i, 0))

    def body(pos_ref, w_ref, o_ref):
        o_ref[...] = w_ref[...].astype(o_ref.dtype)

    gs = pltpu.PrefetchScalarGridSpec(
        num_scalar_prefetch=1, grid=(nb,),
        in_specs=[pl.BlockSpec((tr, c), lambda i, pos: (i, 0))], out_specs=pl.BlockSpec((tr, c), out_map))
    return pl.pallas_call(body, name=name, grid_spec=gs, out_shape=SDS(full, BF),
                          compiler_params=_cp(("arbitrary",)))(pos, w)


def _pair_add(g, land, axis, pos, name):
    hr, wc = _piece_shape(g.shape, axis)
    tr = min(hr, ROWS_TILE)

    def body(pos_ref, g_ref, l_ref, o_ref):
        o_ref[0] = (g_ref[...] + l_ref[0].astype(F32)).astype(BF)

    other = lambda q, i, pos: (pos[0] + 1 + q) % 4
    blk = pl.BlockSpec((1, tr, wc), lambda q, i, pos: (other(q, i, pos), i, 0))
    gs = pltpu.PrefetchScalarGridSpec(
        num_scalar_prefetch=1, grid=(3, hr // tr),
        in_specs=[_piece_spec(g.shape, axis, tr, other, lambda q, i, pos: pos[1]), blk], out_specs=blk)
    return pl.pallas_call(body, name=name, grid_spec=gs, out_shape=SDS((4, hr, wc), BF),
                          compiler_params=_cp(("arbitrary", "arbitrary")))(pos, g, land)


def _chip_sum(g, land1, land2, axis, pos, name):
    hr, wc = _piece_shape(g.shape, axis)
    tr = min(hr, ROWS_TILE)
    nb = hr // tr

    def body(pos_ref, g_ref, l1_ref, l2_ref, o_ref):
        acc = g_ref[...] + l1_ref[0].astype(F32)
        for q in range(3):
            acc = acc + l2_ref[q].astype(F32)
        o_ref[...] = acc

    gs = pltpu.PrefetchScalarGridSpec(
        num_scalar_prefetch=1, grid=(nb,),
        in_specs=[_piece_spec(g.shape, axis, tr, lambda i, pos: pos[0], lambda i, pos: pos[1]),
                  pl.BlockSpec((1, tr, wc), lambda i, pos: (pos[0], i, 0)),
                  pl.BlockSpec((3, tr, wc), lambda i, pos: (0, i, 0))],
        out_specs=pl.BlockSpec((tr, wc), lambda i, pos: (pos[1] * nb + i, 0)))
    return pl.pallas_call(body, name=name, grid_spec=gs, out_shape=SDS((2 * hr, wc), F32),
                          compiler_params=_cp(("arbitrary",)))(pos, g, land1, land2)


ANY = pl.BlockSpec(memory_space=pl.ANY)
COMM = pltpu.CompilerParams(has_side_effects=True)


def _on_each_device(fn):
    x, y, c = lax.axis_index("x"), lax.axis_index("y"), lax.axis_index("c")
    for sx in (0, 1):
        for sy in (0, 1):
            for sc in (0, 1):
                @pl.when(jnp.logical_and(jnp.logical_and(x == sx, y == sy), c == sc))
                def _(sx=sx, sy=sy, sc=sc):
                    fn(sx, sy, sc)


def _remote(src, dst, send_sem, recv_sem, to):
    return pltpu.make_async_remote_copy(src_ref=src, dst_ref=dst, send_sem=send_sem, recv_sem=recv_sem,
                                        device_id=to, device_id_type=MESH)


def _piece_ref(ref, axis, j, h):
    r, c = ref.shape
    hr, wc = _piece_shape((r, c), axis)
    if axis == 1:
        return ref.at[pl.ds(h * hr, hr), pl.ds(j * wc, wc)]
    return ref.at[pl.ds((2 * j + h) * hr, hr), :]


class _Stage:
    def __init__(self, ins, outs, n_sems, make_plan, aliases=None):
        self.ins, self.outs, self.n_sems, self.make_plan = list(ins), list(outs), n_sems, make_plan
        self.aliases = dict(aliases or {})


def _start(plan):
    def dev(sx, sy, sc):
        for cp, _, _ in plan(sx, sy, sc):
            cp.start()

    _on_each_device(dev)


def _finish(plan):
    def dev(sx, sy, sc):
        for _, sent, got in plan(sx, sy, sc):
            sent.wait_send()
            got.wait_recv()

    _on_each_device(dev)


def _comm_call(stage, name):
    n_in, n_out = len(stage.ins), len(stage.outs)

    def body(*refs):
        plan = stage.make_plan(refs[:n_in], refs[n_in:n_in + n_out], *refs[n_in + n_out:])
        _start(plan)
        _finish(plan)

    return pl.pallas_call(
        body, name=name, in_specs=[ANY] * n_in, out_specs=[ANY] * n_out, out_shape=stage.outs,
        input_output_aliases=stage.aliases,
        scratch_shapes=[pltpu.SemaphoreType.DMA((stage.n_sems,)), pltpu.SemaphoreType.DMA((stage.n_sems,))],
        compiler_params=COMM,
    )(*stage.ins)


class _SemsFrom:
    def __init__(self, sems, base):
        self.sems, self.base = sems, base

    @property
    def at(self):
        return self

    def __getitem__(self, k):
        return self.sems.at[self.base + k]


def _both(a, b):
    na, nb = len(a.ins), len(b.ins)
    ma = len(a.outs)

    def make_plan(ins, outs, send_sems, recv_sems):
        pa = a.make_plan(ins[:na], outs[:ma], send_sems, recv_sems)
        pb = b.make_plan(ins[na:], outs[ma:], _SemsFrom(send_sems, a.n_sems), _SemsFrom(recv_sems, a.n_sems))
        return lambda sx, sy, sc: pa(sx, sy, sc) + pb(sx, sy, sc)

    aliases = dict(a.aliases)
    aliases.update({na + i: ma + o for i, o in b.aliases.items()})
    return _Stage(a.ins + b.ins, a.outs + b.outs, a.n_sems + b.n_sems, make_plan, aliases)


def _same(cp):
    return (cp, cp, cp)


def _stage_gather_send(fulls, axes):
    n = len(fulls)

    def make_plan(ins, outs, send_sems, recv_sems):
        def plan(sx, sy, sc):
            cps = []
            for w in range(n):
                mine = _piece_ref(outs[w], axes[w], 2 * sx + sy, sc)
                for r, (fx, fy) in enumerate(CHIP_RELS):
                    k = 3 * w + r
                    to = (sx ^ fx, sy ^ fy, sc)
                    got = _piece_ref(outs[w], axes[w], 2 * (sx ^ fx) + (sy ^ fy), sc)
                    send = _remote(mine, mine, send_sems.at[k], recv_sems.at[k], to)
                    cps.append((send, send, _remote(got, got, send_sems.at[k], recv_sems.at[k], to)))
            return cps

        return plan

    return _Stage(fulls, [SDS(f.shape, f.dtype) for f in fulls], 3 * n, make_plan, {i: i for i in range(n)})


def _stage_gather_forward(fulls, axes):
    n = len(fulls)

    def make_plan(ins, outs, send_sems, recv_sems):
        def plan(sx, sy, sc):
            cps = []
            sib = (sx, sy, 1 - sc)
            for w in range(n):
                for r, (fx, fy) in enumerate(CHIP_RELS):
                    k = 3 * w + r
                    pj = 2 * (sx ^ fx) + (sy ^ fy)
                    have = _piece_ref(outs[w], axes[w], pj, sc)
                    want = _piece_ref(outs[w], axes[w], pj, 1 - sc)
                    send = _remote(have, have, send_sems.at[k], recv_sems.at[k], sib)
                    cps.append((send, send, _remote(want, want, send_sems.at[k], recv_sems.at[k], sib)))
            return cps

        return plan

    return _Stage(fulls, [SDS(f.shape, f.dtype) for f in fulls], 3 * n, make_plan, {i: i for i in range(n)})


def _in_place(fulls, n_sems, make_plan):
    return _Stage(fulls, [SDS(f.shape, f.dtype) for f in fulls], n_sems, make_plan, {i: i for i in range(len(fulls))})


def _stage_gather_neighbours(fulls, axes):
    n = len(fulls)

    def make_plan(ins, outs, send_sems, recv_sems):
        def plan(sx, sy, sc):
            cps = []
            for w in range(n):
                mine = _piece_ref(outs[w], axes[w], 2 * sx + sy, sc)
                for r, (px, py) in enumerate(((sx ^ 1, sy), (sx, sy ^ 1))):
                    got = _piece_ref(outs[w], axes[w], 2 * px + py, sc)
                    send = _remote(mine, mine, send_sems.at[2 * w + r], recv_sems.at[2 * w + r], (px, py, sc))
                    cps.append((send, send, _remote(got, got, send_sems.at[2 * w + r], recv_sems.at[2 * w + r],
                                                    (px, py, sc))))
            return cps

        return plan

    return _in_place(fulls, 2 * n, make_plan)


def _stage_gather_pair(fulls, axes):
    n = len(fulls)

    def make_plan(ins, outs, send_sems, recv_sems):
        def plan(sx, sy, sc):
            cps = []
            sib = (sx, sy, 1 - sc)
            for w in range(n):
                for r, j in enumerate((2 * (sx ^ 1) + sy, 2 * sx + (sy ^ 1))):
                    k = 2 * w + r
                    have, want = _piece_ref(outs[w], axes[w], j, sc), _piece_ref(outs[w], axes[w], j, 1 - sc)
                    send = _remote(have, have, send_sems.at[k], recv_sems.at[k], sib)
                    cps.append((send, send, _remote(want, want, send_sems.at[k], recv_sems.at[k], sib)))
            return cps

        return plan

    return _in_place(fulls, 2 * n, make_plan)


def _stage_gather_diagonal(fulls, axes):
    n = len(fulls)

    def make_plan(ins, outs, send_sems, recv_sems):
        def relay(sx, sy, sc):
            cps = []
            jx, jy, jd = 2 * (sx ^ 1) + sy, 2 * sx + (sy ^ 1), 2 * (sx ^ 1) + (sy ^ 1)
            passed, to = (jx, (sx, sy ^ 1, sc)) if sc == 0 else (jy, (sx ^ 1, sy, sc))
            for w in range(n):
                have, want = _piece_ref(outs[w], axes[w], passed, sc), _piece_ref(outs[w], axes[w], jd, sc)
                send = _remote(have, have, send_sems.at[w], recv_sems.at[w], to)
                cps.append((send, send, _remote(want, want, send_sems.at[w], recv_sems.at[w], to)))
            return cps

        def cross(sx, sy, sc):
            cps = []
            sib = (sx, sy, 1 - sc)
            jd = 2 * (sx ^ 1) + (sy ^ 1)
            for w in range(n):
                have, want = _piece_ref(outs[w], axes[w], jd, sc), _piece_ref(outs[w], axes[w], jd, 1 - sc)
                send = _remote(have, have, send_sems.at[n + w], recv_sems.at[n + w], sib)
                cps.append((send, send, _remote(want, want, send_sems.at[n + w], recv_sems.at[n + w], sib)))
            return cps

        return [relay, cross]

    return _in_place(fulls, 2 * n, make_plan)


def _stage_pair_exchange(grads, axes):
    n = len(grads)

    def make_plan(gs, land, send_sems, recv_sems):
        def plan(sx, sy, sc):
            return [_same(_remote(_piece_ref(gs[w], axes[w], jj, 1 - sc), land[w].at[jj], send_sems.at[4 * w + jj],
                                  recv_sems.at[4 * w + jj], (sx, sy, 1 - sc)))
                    for w in range(n) for jj in range(4)]

        return plan

    return _Stage(grads, [SDS((4,) + _piece_shape(g.shape, a), g.dtype) for g, a in zip(grads, axes)], 4 * n,
                  make_plan)


def _stage_chip_scatter(pieces):
    n = len(pieces)

    def make_plan(ps, land, send_sems, recv_sems):
        def plan(sx, sy, sc):
            return [_same(_remote(ps[w].at[2 * (sx ^ fx) + (sy ^ fy)], land[w].at[r], send_sems.at[3 * w + r],
                                  recv_sems.at[3 * w + r], (sx ^ fx, sy ^ fy, sc)))
                    for w in range(n) for r, (fx, fy) in enumerate(CHIP_RELS)]

        return plan

    return _Stage(pieces, [SDS((3,) + p.shape[1:], p.dtype) for p in pieces], 3 * n, make_plan)


def _stage_pair_share(shards):
    n = len(shards)

    def make_plan(ins, outs, send_sems, recv_sems):
        def plan(sx, sy, sc):
            cps = []
            sib = (sx, sy, 1 - sc)
            for w in range(n):
                hr = shards[w].shape[0] // 2
                mine = outs[w].at[pl.ds(sc * hr, hr), :]
                theirs = outs[w].at[pl.ds((1 - sc) * hr, hr), :]
                send = _remote(mine, mine, send_sems.at[w], recv_sems.at[w], sib)
                cps.append((send, send, _remote(theirs, theirs, send_sems.at[w], recv_sems.at[w], sib)))
            return cps

        return plan

    return _Stage(shards, [SDS(g.shape, g.dtype) for g in shards], n, make_plan, {i: i for i in range(n)})


def _stage_gather_small(stack):
    def make_plan(ins, outs, send_sems, recv_sems):
        def plan(sx, sy, sc):
            mine = outs[0].at[4 * sx + 2 * sy + sc]
            return [_same(_remote(mine, mine, send_sems.at[k], recv_sems.at[k], (sx ^ fx, sy ^ fy, sc ^ fc)))
                    for k, (fx, fy, fc) in enumerate(DEV_RELS)]

        return plan

    return _Stage([stack], [SDS(stack.shape, stack.dtype)], 7, make_plan, {0: 0})


def kernel(x, w_in, conv_a_w, w_out_a, conv_b_w, conv_b_bias, ln_b_gamma, ln_b_beta, w_out_b, w_o, ln1_gamma, ln1_beta, w_up, w_down, ln2_gamma, ln2_beta, loss_target, m_w_in, m_conv_a_w, m_w_out_a, m_conv_b_w, m_conv_b_bias, m_ln_b_gamma, m_ln_b_beta, m_w_out_b, m_w_o, m_ln1_gamma, m_ln1_beta, m_w_up, m_w_down, m_ln2_gamma, m_ln2_beta, v_w_in, v_conv_a_w, v_w_out_a, v_conv_b_w, v_conv_b_bias, v_ln_b_gamma, v_ln_b_beta, v_w_out_b, v_w_o, v_ln1_gamma, v_ln1_beta, v_w_up, v_w_down, v_ln2_gamma, v_ln2_beta):
    s, d = x.shape[1], x.shape[2]
    xs = x.reshape(s, d)
    tgt = loss_target.reshape(s, d)
    dq = d // 4
    chip = 2 * lax.axis_index("x") + lax.axis_index("y")
    core = lax.axis_index("c")
    pos = jnp.stack([chip, core]).astype(jnp.int32)
    names = ("w_in", "w_out_a", "w_out_b", "w_o", "w_up", "w_down")
    axes = (1, 0, 0, 0, 1, 0)

    conv_pack = jnp.concatenate([jnp.pad(conv_a_w, ((0, 8 - K_A), (0, 0))), jnp.pad(conv_b_w, ((0, 32 - K_B), (0, 0))),
                                 jnp.zeros((8, dq), F32)], axis=0)
    conv_full = lax.dynamic_update_slice(jnp.zeros((conv_pack.shape[0], d), F32), conv_pack, (0, chip * dq))
    wi_own = _place_cast(w_in, 1, pos, "place_w_in")
    vec = lambda a: a.reshape(1, d)
    bias_b, lbg, lbb = vec(conv_b_bias), vec(ln_b_gamma), vec(ln_b_beta)
    l1g, l1b, l2g, l2b = vec(ln1_gamma), vec(ln1_beta), vec(ln2_gamma), vec(ln2_beta)

    (p, xb, *placed), landed = _in_proj_own(xs, w_in, (w_out_a, w_out_b, w_o, w_up, w_down), axes[1:], pos,
                                            _stage_gather_neighbours([wi_own, conv_full], (1, 1)))
    fulls = [wi_own] + placed
    wi, convs = _comm_call(_stage_gather_pair(landed, (1, 1)), "gather_pair_w_in")
    p, (wi, convs), small3 = _in_proj_rest(xb, wi, p, pos, _stage_gather_diagonal([wi, convs], (1, 1)),
                                           _stage_gather_send(fulls[1:4], axes[1:4]))
    cwa, cwb = convs[0:K_A], convs[8:8 + K_B]
    (conva, yapre, nb, rstdb, u3), landed = _conv_fwd(
        p, cwa, cwb, bias_b, lbg, lbb, d,
        _both(_stage_gather_forward(small3, axes[1:4]), _stage_gather_send(fulls[4:6], axes[4:6])))
    woa, wob, wo = landed[:3]
    (ya, yb, merged, n1, rstd1), (wup, wdown) = _mixer_out(yapre, u3, p, xs, woa, wob, wo, d,
                                                           _stage_gather_forward(landed[3:], axes[4:6]))
    x1b, hb, dhpre, dr2b, dr1, dr1b, acc_mlp = _mlp(n1, rstd1, tgt, wup, wdown, l1g, l1b, l2g, l2b)
    g_up, gb_up, _ = _grad_w(x1b, dhpre, "grad_w_up")
    g_down, gb_down, _ = _grad_w(hb, dr2b, "grad_w_down")
    dya, dyb, dba, dga, dgb, dca, du1, acc_mix = _mixer_bwd_local(dr1b, p, ya, yb, conva, nb, rstdb, wo, woa, wob,
                                                                   lbg, lbb)
    g_oa, gb_oa, _ = _grad_w(yapre, dya, "grad_w_out_a")
    g_ob, gb_ob, _ = _grad_w(u3, dyb, "grad_w_out_b")
    g_o, gb_o, _ = _grad_w(merged, dr1b, "grad_w_o")

    early, e_axes, e_names = [g_oa, g_ob, g_o, g_up, g_down], axes[1:], names[1:]
    (dp, dcwa, dcwb), land1 = _conv_bwd(dca, du1, p, dba, dga, dgb, cwa, cwb, d,
                                        _stage_pair_exchange([gb_oa, gb_ob, gb_o, gb_up, gb_down], e_axes))
    pieces = [_pair_add(g, l, a, pos, "pair_add_" + nm) for g, l, a, nm in zip(early, land1, e_axes, e_names)]
    pack = jnp.concatenate([dcwa, dcwb, acc_mix, acc_mlp], axis=0)
    stack = lax.dynamic_update_slice(jnp.zeros((8,) + pack.shape, F32), pack[None], (2 * chip + core, 0, 0))
    g_wi, gb_wi, landed = _grad_w(xb, dp, "grad_w_in",
                                  _both(_stage_chip_scatter(pieces), _stage_gather_small(stack)))
    land2, stack = landed[:-1], landed[-1]
    halves = [_chip_sum(g, l1, l2, a, pos, "chip_sum_" + nm)
              for g, l1, l2, a, nm in zip(early, land1, land2, e_axes, e_names)]
    (land1_in,) = _comm_call(_stage_pair_exchange([gb_wi], (1,)), "pair_exchange_w_in")
    piece_in = _pair_add(g_wi, land1_in, 1, pos, "pair_add_w_in")
    grad_x, landed = _grad_x(dr1, dp, wi, _both(_stage_chip_scatter([piece_in]), _stage_pair_share(halves)))
    land2_in, (g_oa, g_ob, g_o, g_up, g_down) = landed[0], landed[1:]
    half_in = _chip_sum(g_wi, land1_in, land2_in, 1, pos, "chip_sum_w_in")
    (g_in,) = _comm_call(_stage_pair_share([half_in]), "pair_share_w_in")
    small = _sum_parts(stack, "small_sum")
    g_ca = lax.dynamic_slice(small, (0, chip * dq), (K_A, dq))
    g_cb = lax.dynamic_slice(small, (8, chip * dq), (K_B, dq))
    g_vec = jnp.stack([small[r] for r in (42, 40, 41, 51, 52, 49, 50)])

    loss = (0.5 / d) * jnp.sum(small[48])

    big = {}
    for name, w, g, m, v in (("w_in", w_in, g_in, m_w_in, v_w_in), ("w_out_a", w_out_a, g_oa, m_w_out_a, v_w_out_a),
                             ("w_out_b", w_out_b, g_ob, m_w_out_b, v_w_out_b), ("w_o", w_o, g_o, m_w_o, v_w_o),
                             ("w_up", w_up, g_up, m_w_up, v_w_up), ("w_down", w_down, g_down, m_w_down, v_w_down),
                             ("conv_a_w", conv_a_w, g_ca, m_conv_a_w, v_conv_a_w),
                             ("conv_b_w", conv_b_w, g_cb, m_conv_b_w, v_conv_b_w)):
        big[name] = tuple(_adamw(w, g, m, v, "adamw_" + name))
    vec_names = ("conv_b_bias", "ln_b_gamma", "ln_b_beta", "ln1_gamma", "ln1_beta", "ln2_gamma", "ln2_beta")
    w7 = jnp.stack([conv_b_bias, ln_b_gamma, ln_b_beta, ln1_gamma, ln1_beta, ln2_gamma, ln2_beta])
    m7 = jnp.stack([m_conv_b_bias, m_ln_b_gamma, m_ln_b_beta, m_ln1_gamma, m_ln1_beta, m_ln2_gamma, m_ln2_beta])
    v7 = jnp.stack([v_conv_b_bias, v_ln_b_gamma, v_ln_b_beta, v_ln1_gamma, v_ln1_beta, v_ln2_gamma, v_ln2_beta])
    g7, d7, nm7, nv7 = _adamw(w7, g_vec, m7, v7, "adamw_vectors")
    for q, name in enumerate(vec_names):
        big[name] = (g7[q], d7[q], nm7[q], nv7[q])

    order = ("w_in", "conv_a_w", "w_out_a", "conv_b_w", "conv_b_bias", "ln_b_gamma", "ln_b_beta", "w_out_b", "w_o",
             "ln1_gamma", "ln1_beta", "w_up", "w_down", "ln2_gamma", "ln2_beta")
    outs = [loss, grad_x.reshape(x.shape)]
    for part in range(4):
        outs += [big[name][part] for name in order]
    return tuple(outs)
```

```python
import jax
import jax.numpy as jnp
from jax import lax
from jax.experimental import pallas as pl
from jax.experimental.pallas import tpu as pltpu

F32 = jnp.float32
BF = jnp.bfloat16
SDS = jax.ShapeDtypeStruct
MESH = pl.DeviceIdType.MESH

ALPHA = 2.0 ** 0.25
LN_EPS = 1e-5
K_A = 3
K_B = 31
HALO = 32
CONV_TILE = 512
CONV_ROWS = 64
CONV_LANES = 128
VMEM_LIMIT_MB = 56
PROJ_TILE = 1024
PROJ_REST_TILE = 2048
MIXER_TILE = 512
GRAD_X_TILE = 512
LOCAL_TILE = 256
MIXER_BWD_TILE = 512
FF_CHUNK = 1024
GRAD_W_TILE = 1024
GRAD_W_TOKENS = 4096
ROWS_TILE = 256
ADAM_LR = 0.001
ADAM_B1 = 0.9
ADAM_B2 = 0.999
ADAM_EPS = 1e-08
ADAM_WD = 0.01
ADAM_STEP = 10
P_DT = BF
CHIP_RELS = ((1, 0), (0, 1), (1, 1))
DEV_RELS = tuple((fx, fy, fc) for fx in (0, 1) for fy in (0, 1) for fc in (0, 1))[1:]


def _cp(sem=None, side_effects=False):
    return pltpu.CompilerParams(dimension_semantics=sem, vmem_limit_bytes=VMEM_LIMIT_MB << 20,
                                has_side_effects=side_effects)


def _const(shape):
    return pl.BlockSpec(shape, lambda *_: (0,) * len(shape), pipeline_mode=pl.Buffered(1))


def _sig(v):
    return jax.nn.sigmoid(v)


def _ln_fwd(r):
    mu = jnp.mean(r, axis=-1, keepdims=True)
    xc = r - mu
    var = jnp.mean(xc * xc, axis=-1, keepdims=True)
    rstd = lax.rsqrt(var + LN_EPS)
    return xc * rstd, rstd


def _ln_bwd(dn, n, rstd):
    m1 = jnp.mean(dn, axis=-1, keepdims=True)
    m2 = jnp.mean(dn * n, axis=-1, keepdims=True)
    return rstd * (dn - m1 - n * m2)


def _dot(a, b):
    return jnp.dot(a, b, preferred_element_type=F32)


def _dot_nt(a, b):
    return lax.dot_general(a, b, (((1,), (1,)), ((), ())), preferred_element_type=F32)


def _dot_tn(a, b):
    return lax.dot_general(a, b, (((0,), (0,)), ((), ())), preferred_element_type=F32)


def _tile(n, pref):
    if n <= pref:
        return n
    return max(t for t in range(128, pref + 1, 128) if n % t == 0)


def _rowsum8(v):
    acc = v[0:8]
    for g in range(1, v.shape[0] // 8):
        acc = acc + v[8 * g:8 * g + 8]
    return acc


def _taps(win, offsets, rows):
    r_all = win.shape[0]
    by_res = {}
    for k, o in enumerate(offsets):
        by_res.setdefault(o % 8, []).append((k, o // 8))
    for s, taps in sorted(by_res.items()):
        r = win if s == 0 else pltpu.roll(win, r_all - s, 0)
        for k, q in taps:
            yield k, r[8 * q:8 * q + rows]


CAUSAL_A = [HALO - (K_A - 1) + k for k in range(K_A)]
CAUSAL_B = [HALO - (K_B - 1) + k for k in range(K_B)]
ANTI_A = [K_A - 1 - k for k in range(K_A)]
ANTI_B = [K_B - 1 - k for k in range(K_B)]


def _host_call(body, *, name, grid, in_specs, out_specs, out_shape, scratch_shapes, args, hosted, prefetch=None,
               aliases=None, body_gets_stage_refs=False):
    n_in, n_out, n_scr = len(in_specs), len(out_specs), len(scratch_shapes)
    n_steps = 1
    for size in grid:
        n_steps *= size
    if isinstance(hosted, _Stage):
        hosted = [(hosted, 0, n_steps - 1)]
    n_pre = 0 if prefetch is None else 1
    pre = () if prefetch is None else (prefetch,)
    sem = ("arbitrary",) * len(grid)
    own_aliases = {n_pre + a: b for a, b in (aliases or {}).items()}

    def call(kernel_body, ins, outs, shapes, scratch, all_aliases, side_effects, operands):
        gs = pltpu.PrefetchScalarGridSpec(num_scalar_prefetch=n_pre, grid=grid, in_specs=ins, out_specs=outs,
                                          scratch_shapes=scratch)
        return pl.pallas_call(kernel_body, name=name, grid_spec=gs, out_shape=shapes,
                              input_output_aliases=all_aliases,
                              compiler_params=_cp(sem, side_effects=side_effects))(*pre, *operands)

    if hosted is None:
        res = call(body, list(in_specs), list(out_specs), list(out_shape), list(scratch_shapes), own_aliases, False,
                   args)
        return list(res), []
    stages = [st for st, _, _ in hosted]
    h_ins = [a for st in stages for a in st.ins]
    h_outs = [o for st in stages for o in st.outs]
    borrowed = {hi: k for hi, a in enumerate(h_ins) for k, own in enumerate(args) if a is own}
    passed = [hi for hi in range(len(h_ins)) if hi not in borrowed]
    h_in, h_out, n_sems = len(passed), len(h_outs), sum(st.n_sems for st in stages)

    def full_body(*refs):
        pre_refs, refs = refs[:n_pre], refs[n_pre:]
        ins, refs = refs[:n_in], refs[n_in:]
        hins = [None] * len(h_ins)
        for hi, ref in zip(passed, refs[:h_in]):
            hins[hi] = ref
        refs = refs[h_in:]
        outs, refs = refs[:n_out], refs[n_out:]
        houts, refs = refs[:h_out], refs[h_out:]
        scr, (send_sems, recv_sems) = refs[:n_scr], refs[n_scr:]
        step = 0
        for a, size in enumerate(grid):
            step = step * size + pl.program_id(a)
        phases, i0, o0, k0 = [], 0, 0, 0
        for st, starts, finishes in hosted:
            plans = st.make_plan(hins[i0:i0 + len(st.ins)], houts[o0:o0 + len(st.outs)],
                                 _SemsFrom(send_sems, k0), _SemsFrom(recv_sems, k0))
            if not isinstance(plans, list):
                plans, starts, finishes = [plans], [starts], [finishes]
            phases += list(zip(plans, starts, finishes))
            i0, o0, k0 = i0 + len(st.ins), o0 + len(st.outs), k0 + st.n_sems
        for plan, at, _ in phases:
            @pl.when(step == at)
            def _(plan=plan):
                _start(plan)

        if body_gets_stage_refs:
            body(*pre_refs, *ins, *outs, *scr, houts)
        else:
            body(*pre_refs, *ins, *outs, *scr)

        for plan, _, at in phases:
            @pl.when(step == at)
            def _(plan=plan):
                _finish(plan)

    all_aliases = dict(own_aliases)
    i0 = o0 = 0
    for st in stages:
        for a, b in st.aliases.items():
            hi = i0 + a
            operand = borrowed[hi] if hi in borrowed else n_in + passed.index(hi)
            all_aliases[n_pre + operand] = n_out + o0 + b
        i0, o0 = i0 + len(st.ins), o0 + len(st.outs)
    res = call(full_body, list(in_specs) + [ANY] * h_in, list(out_specs) + [ANY] * h_out,
               list(out_shape) + h_outs,
               list(scratch_shapes) + [pltpu.SemaphoreType.DMA((n_sems,)), pltpu.SemaphoreType.DMA((n_sems,))],
               all_aliases, True, (*args, *[h_ins[hi] for hi in passed]))
    return list(res[:n_out]), list(res[n_out:])


def _in_proj_own(x, w_shard, others, other_axes, pos, hosted):
    s, d = x.shape
    tn = w_shard.shape[1]
    tm = min(s, PROJ_TILE)
    n_o = len(others)

    def body(pos_ref, x_ref, w_ref, *refs):
        o_in, (p_ref, xb_ref), o_out, wb = refs[:n_o], refs[n_o:n_o + 2], refs[n_o + 2:2 * n_o + 2], refs[-1]

        @pl.when(pl.program_id(0) == 0)
        def _():
            wb[...] = w_ref[...].astype(BF)
            for src, dst in zip(o_in, o_out):
                dst[...] = src[...].astype(BF)

        xb = x_ref[...].astype(BF)
        xb_ref[...] = xb
        p_ref[...] = _dot(xb, wb[...]).astype(p_ref.dtype)

    whole = lambda a: pl.BlockSpec(a.shape, lambda i, pos: (0, 0), pipeline_mode=pl.Buffered(1))
    placed = lambda a, ax: pl.BlockSpec(a.shape, (lambda i, pos: (0, pos[0])) if ax == 1 else (lambda i, pos: (pos[0], 0)))
    full = lambda a, ax: SDS((a.shape[0], 4 * a.shape[1]) if ax == 1 else (4 * a.shape[0], a.shape[1]), BF)
    outs, extra = _host_call(
        body, name="in_proj_own", grid=(s // tm,), prefetch=pos,
        in_specs=[pl.BlockSpec((tm, d), lambda i, pos: (i, 0)), whole(w_shard)] + [whole(a) for a in others],
        out_specs=[pl.BlockSpec((tm, tn), lambda i, pos: (i, pos[0])), pl.BlockSpec((tm, d), lambda i, pos: (i, 0))]
        + [placed(a, ax) for a, ax in zip(others, other_axes)],
        out_shape=[SDS((s, 4 * tn), P_DT), SDS((s, d), BF)] + [full(a, ax) for a, ax in zip(others, other_axes)],
        scratch_shapes=[pltpu.VMEM((d, tn), BF)], args=(x, w_shard, *others), hosted=hosted)
    return outs, extra


def _in_proj_rest(xb, wi, p, pos, diagonal, other):
    s, d = xb.shape
    n = wi.shape[1]
    tm, tn = min(s // 4, PROJ_REST_TILE), n // 4
    ni = s // tm

    def body(pos_ref, xb_ref, w_ref, p_in, p_ref, wdiag, dsem, stage_outs):
        j = pl.program_id(0)
        step = j * ni + pl.program_id(1)

        def diagonal_block(act):
            for chip in range(4):
                @pl.when(pos_ref[0] == chip)
                def _(chip=chip):
                    act(pltpu.make_async_copy(stage_outs[0].at[:, pl.ds((chip ^ 3) * tn, tn)], wdiag, dsem))

        @pl.when(step == 2 * ni - 1)
        def _():
            diagonal_block(lambda cp: cp.start())

        @pl.when(step == 2 * ni)
        def _():
            diagonal_block(lambda cp: cp.wait())

        @pl.when(j < 2)
        def _():
            p_ref[...] = _dot(xb_ref[...], w_ref[...]).astype(p_ref.dtype)

        @pl.when(j == 2)
        def _():
            p_ref[...] = _dot(xb_ref[...], wdiag[...]).astype(p_ref.dtype)

    def col(j, i, pos):
        return lax.bitwise_xor(pos[0], jnp.where(j == 0, 2, jnp.where(j == 1, 1, 3)))

    def piped_col(j, i, pos):
        return lax.bitwise_xor(pos[0], jnp.where(j == 0, 2, 1))

    (p,), extra = _host_call(
        body, name="in_proj_rest", grid=(3, ni), prefetch=pos, aliases={2: 0}, body_gets_stage_refs=True,
        in_specs=[pl.BlockSpec((tm, d), lambda j, i, pos: (i, 0)),
                  pl.BlockSpec((d, tn), lambda j, i, pos: (0, piped_col(j, i, pos))), ANY],
        out_specs=[pl.BlockSpec((tm, tn), lambda j, i, pos: (i, col(j, i, pos)))],
        out_shape=[SDS((s, n), P_DT)], scratch_shapes=[pltpu.VMEM((d, tn), BF), pltpu.SemaphoreType.DMA(())],
        args=(xb, wi, p),
        hosted=[(diagonal, [0, ni + 1], [ni, 2 * ni - 2]), (other, 0, 3 * ni - 1)])
    n_d = len(diagonal.outs)
    return p, extra[:n_d], extra[n_d:]


def _col_spec(tm, d, k):
    return pl.BlockSpec((tm, d), lambda i, k=k: (i, k))


def _prev_halo_spec(tm, d, k):
    r = tm // HALO
    return pl.BlockSpec((HALO, d), lambda i, k=k: (jnp.maximum(i * r - 1, 0), k))


def _next_halo_spec(tm, d, k, s):
    r = tm // HALO
    last = s // HALO - 1
    return pl.BlockSpec((HALO, d), lambda i, k=k: (jnp.minimum((i + 1) * r, last), k))


def _conv_fwd(p, cwa, cwb, bias_b, lbg, lbb, d, hosted=None):
    s = p.shape[0]
    tm = min(s, CONV_TILE)
    nt = s // tm

    def body(ba_ref, ca_ref, va_ref, vb_ref, gb_ref, hca_ref, hva_ref, hvb_ref, hgb_ref,
             cwa_ref, cwb_ref, bias_ref, lbg_ref, lbb_ref,
             conva_ref, yapre_ref, nb_ref, rstdb_ref, u3_ref,
             zbuf, ubuf, u1buf):
        i = pl.program_id(0)
        keep = (i > 0).astype(F32)
        zbuf[pl.ds(0, HALO), :] = hca_ref[...].astype(F32) * hva_ref[...].astype(F32) * keep
        ubuf[pl.ds(0, HALO), :] = hvb_ref[...].astype(F32) * _sig(hgb_ref[...].astype(F32)) * keep
        zbuf[pl.ds(HALO, tm), :] = ca_ref[...].astype(F32) * va_ref[...].astype(F32)
        ubuf[pl.ds(HALO, tm), :] = vb_ref[...].astype(F32) * _sig(gb_ref[...].astype(F32))

        def chunk(j, carry):
            r0 = pl.multiple_of(j * CONV_ROWS, CONV_ROWS)
            rows = pl.ds(r0, CONV_ROWS)
            for lc in range(d // CONV_LANES):
                ls = pl.ds(lc * CONV_LANES, CONV_LANES)
                acc = jnp.zeros((CONV_ROWS, CONV_LANES), F32)
                for k, sl in _taps(zbuf[pl.ds(r0, CONV_ROWS + HALO), ls], CAUSAL_A, CONV_ROWS):
                    acc = acc + cwa_ref[pl.ds(k, 1), ls] * sl
                conva_ref[rows, ls] = acc.astype(BF)
                yapre_ref[rows, ls] = (ba_ref[rows, ls].astype(F32) * acc).astype(BF)
                acc = jnp.zeros((CONV_ROWS, CONV_LANES), F32)
                for k, sl in _taps(ubuf[pl.ds(r0, CONV_ROWS + HALO), ls], CAUSAL_B, CONV_ROWS):
                    acc = acc + cwb_ref[pl.ds(k, 1), ls] * sl
                u1buf[rows, ls] = acc + bias_ref[:, ls]
            return carry

        lax.fori_loop(0, tm // CONV_ROWS, chunk, 0)
        nb, rstd = _ln_fwd(u1buf[...])
        nb_ref[...] = nb
        rstdb_ref[...] = rstd
        u2 = nb * lbg_ref[...] + lbb_ref[...]
        u3_ref[...] = (u2 * _sig(u2)).astype(BF)

    vec = _const((1, d))
    return _host_call(
        body, name="conv_fwd", grid=(nt,),
        in_specs=[_col_spec(tm, d, k) for k in range(5)] + [_prev_halo_spec(tm, d, k) for k in (1, 2, 3, 4)]
        + [_const((K_A, d)), _const((K_B, d)), vec, vec, vec],
        out_specs=[pl.BlockSpec((tm, d), lambda i: (i, 0)), pl.BlockSpec((tm, d), lambda i: (i, 0)),
                   pl.BlockSpec((tm, d), lambda i: (i, 0)), pl.BlockSpec((tm, 1), lambda i: (i, 0)),
                   pl.BlockSpec((tm, d), lambda i: (i, 0))],
        out_shape=[SDS((s, d), BF), SDS((s, d), BF), SDS((s, d), F32), SDS((s, 1), F32), SDS((s, d), BF)],
        scratch_shapes=[pltpu.VMEM((HALO + tm, d), F32), pltpu.VMEM((HALO + tm, d), F32), pltpu.VMEM((tm, d), F32)],
        args=(p, p, p, p, p, p, p, p, p, cwa, cwb, bias_b, lbg, lbb), hosted=hosted)


def _mixer_out(yapre, u3, p, x, woa, wob, wo, d, hosted=None):
    s = x.shape[0]
    tm = min(s, MIXER_TILE)

    def body(yapre_ref, u3_ref, ga_ref, gb_ref, x_ref, woa_ref, wob_ref, wo_ref,
             ya_ref, yb_ref, merged_ref, n1_ref, rstd1_ref):
        ya = _dot(yapre_ref[...], woa_ref[...])
        yb = _dot(u3_ref[...], wob_ref[...])
        ya_ref[...] = ya.astype(BF)
        yb_ref[...] = yb.astype(BF)
        merged = (_sig(ga_ref[...].astype(F32)) * ya + _sig(gb_ref[...].astype(F32)) * yb).astype(BF)
        merged_ref[...] = merged
        r1 = F32(ALPHA) * x_ref[...] + _dot(merged, wo_ref[...])
        n1, rstd1 = _ln_fwd(r1)
        n1_ref[...] = n1
        rstd1_ref[...] = rstd1

    row = pl.BlockSpec((tm, d), lambda i: (i, 0))
    return _host_call(
        body, name="mixer_out", grid=(s // tm,),
        in_specs=[row, row, _col_spec(tm, d, 5), _col_spec(tm, d, 6), row,
                  _const((d, d)), _const((d, d)), _const((d, d))],
        out_specs=[row, row, row, row, pl.BlockSpec((tm, 1), lambda i: (i, 0))],
        out_shape=[SDS((s, d), BF), SDS((s, d), BF), SDS((s, d), BF), SDS((s, d), F32), SDS((s, 1), F32)],
        scratch_shapes=[], args=(yapre, u3, p, p, x, woa, wob, wo), hosted=hosted)


def _mlp(n1, rstd1, tgt, wup, wdown, l1g, l1b, l2g, l2b):
    s, d = n1.shape
    dff = wup.shape[1]
    tm = min(s, LOCAL_TILE)
    fc = min(dff, FF_CHUNK)
    nq = dff // fc

    def body(n1_ref, rstd1_ref, tgt_ref, wup_ref, wdown_ref, l1g_ref, l1b_ref, l2g_ref, l2b_ref,
             x1b_ref, hb_ref, dhpre_ref, dr2b_ref, dr1_ref, dr1b_ref, acc_ref, rbuf):
        i = pl.program_id(0)
        n1v = n1_ref[...]
        x1 = n1v * l1g_ref[...] + l1b_ref[...]
        x1b = x1.astype(BF)
        x1b_ref[...] = x1b
        ff = jnp.zeros((tm, d), F32)
        for q in range(nq):
            cs = pl.ds(q * fc, fc)
            r = jnp.maximum(_dot(x1b, wup_ref[:, cs]), 0.0)
            rbuf[:, cs] = r
            hq = (r * r).astype(BF)
            hb_ref[:, cs] = hq
            ff = ff + _dot(hq, wdown_ref[cs, :])
        n2, rstd2 = _ln_fwd(F32(ALPHA) * x1 + ff)
        x2 = n2 * l2g_ref[...] + l2b_ref[...]
        err = x2 - tgt_ref[...]
        dx2 = err * F32(1.0 / d)
        dr2 = _ln_bwd(dx2 * l2g_ref[...], n2, rstd2)
        dr2b = dr2.astype(BF)
        dr2b_ref[...] = dr2b
        dx1 = F32(ALPHA) * dr2
        for q in range(nq):
            cs = pl.ds(q * fc, fc)
            dh = _dot_nt(dr2b, wdown_ref[cs, :])
            dhp = (dh * (2.0 * rbuf[:, cs])).astype(BF)
            dhpre_ref[:, cs] = dhp
            dx1 = dx1 + _dot_nt(dhp, wup_ref[:, cs])
        dr1 = _ln_bwd(dx1 * l1g_ref[...], n1v, rstd1_ref[...])
        dr1_ref[...] = dr1
        dr1b_ref[...] = dr1.astype(BF)

        @pl.when(i == 0)
        def _():
            acc_ref[...] = jnp.zeros_like(acc_ref)

        for q, val in enumerate((err * err, dx2 * n2, dx2, dx1 * n1v, dx1)):
            acc_ref[pl.ds(q, 1), :] += jnp.sum(val, axis=0, keepdims=True)

    row = pl.BlockSpec((tm, d), lambda i: (i, 0))
    wide = pl.BlockSpec((tm, dff), lambda i: (i, 0))
    vec = _const((1, d))
    return pl.pallas_call(
        body, name="mlp_fwd_bwd", grid=(s // tm,),
        in_specs=[row, pl.BlockSpec((tm, 1), lambda i: (i, 0)), row, _const((d, dff)), _const((dff, d)),
                  vec, vec, vec, vec],
        out_specs=[row, wide, wide, row, row, row, pl.BlockSpec((8, d), lambda i: (0, 0))],
        out_shape=[SDS((s, d), BF), SDS((s, dff), BF), SDS((s, dff), BF), SDS((s, d), BF), SDS((s, d), F32),
                   SDS((s, d), BF), SDS((8, d), F32)],
        scratch_shapes=[pltpu.VMEM((tm, dff), F32)],
        compiler_params=_cp(("arbitrary",)),
    )(n1, rstd1, tgt, wup, wdown, l1g, l1b, l2g, l2b)


def _mixer_bwd_local(dr1b, p, ya, yb, conva, nb, rstdb, wo, woa, wob, lbg, lbb):
    s, d = ya.shape
    tm = min(s, MIXER_BWD_TILE)

    def body(dr1b_ref, ba_ref, ga_ref, gb_ref, ya_ref, yb_ref, conva_ref, nb_ref, rstdb_ref,
             wo_ref, woa_ref, wob_ref, lbg_ref, lbb_ref,
             dya_ref, dyb_ref, dba_ref, dga_ref, dgb_ref, dca_ref, du1_ref, acc_ref):
        i = pl.program_id(0)
        dmerged = _dot_nt(dr1b_ref[...], wo_ref[...])
        sa = _sig(ga_ref[...].astype(F32))
        sb = _sig(gb_ref[...].astype(F32))
        dya = (dmerged * sa).astype(BF)
        dyb = (dmerged * sb).astype(BF)
        dya_ref[...] = dya
        dyb_ref[...] = dyb
        dga_ref[...] = (dmerged * ya_ref[...].astype(F32) * (sa * (1.0 - sa))).astype(BF)
        dgb_ref[...] = (dmerged * yb_ref[...].astype(F32) * (sb * (1.0 - sb))).astype(BF)
        dyapre = _dot_nt(dya, woa_ref[...])
        dba_ref[...] = (dyapre * conva_ref[...].astype(F32)).astype(BF)
        dca_ref[...] = (dyapre * ba_ref[...].astype(F32)).astype(BF)
        du3 = _dot_nt(dyb, wob_ref[...])
        nbv = nb_ref[...]
        u2 = nbv * lbg_ref[...] + lbb_ref[...]
        sg = _sig(u2)
        du2 = du3 * (sg * (1.0 + u2 * (1.0 - sg)))
        du1 = _ln_bwd(du2 * lbg_ref[...], nbv, rstdb_ref[...])
        du1_ref[...] = du1.astype(BF)

        @pl.when(i == 0)
        def _():
            acc_ref[...] = jnp.zeros_like(acc_ref)

        for q, val in enumerate((du2 * nbv, du2, du1)):
            acc_ref[pl.ds(q, 1), :] += jnp.sum(val, axis=0, keepdims=True)

    row = pl.BlockSpec((tm, d), lambda i: (i, 0))
    vec = _const((1, d))
    return pl.pallas_call(
        body, name="mixer_bwd_local", grid=(s // tm,),
        in_specs=[row, _col_spec(tm, d, 0), _col_spec(tm, d, 5), _col_spec(tm, d, 6), row, row, row, row,
                  pl.BlockSpec((tm, 1), lambda i: (i, 0)), _const((d, d)), _const((d, d)), _const((d, d)), vec, vec],
        out_specs=[row, row, row, row, row, row, row, pl.BlockSpec((8, d), lambda i: (0, 0))],
        out_shape=[SDS((s, d), BF)] * 7 + [SDS((8, d), F32)],
        compiler_params=_cp(("arbitrary",)),
    )(dr1b, p, p, p, ya, yb, conva, nb, rstdb, wo, woa, wob, lbg, lbb)


def _conv_bwd(dca, du1, p, dba, dga, dgb, cwa, cwb, d, hosted=None):
    s = dca.shape[0]
    tm = min(s, CONV_TILE)
    nt = s // tm

    def body(dca_ref, du1_ref, ndca_ref, ndu1_ref, ca_ref, va_ref, vb_ref, gb_ref,
             dba_ref, dga_ref, dgb_ref, cwa_ref, cwb_ref,
             dp_ref, dcwa_ref, dcwb_ref,
             dcabuf, du1buf, sgbuf, acca, accb):
        i = pl.program_id(0)
        keep_next = (i < nt - 1).astype(F32)

        @pl.when(i == 0)
        def _():
            acca[...] = jnp.zeros_like(acca)
            accb[...] = jnp.zeros_like(accb)

        sgbuf[...] = _sig(gb_ref[...].astype(F32))
        dcabuf[pl.ds(0, tm), :] = dca_ref[...].astype(F32)
        dcabuf[pl.ds(tm, HALO), :] = ndca_ref[...].astype(F32) * keep_next
        du1buf[pl.ds(0, tm), :] = du1_ref[...].astype(F32)
        du1buf[pl.ds(tm, HALO), :] = ndu1_ref[...].astype(F32) * keep_next
        dp_ref[:, pl.ds(0, d)] = dba_ref[...]
        dp_ref[:, pl.ds(5 * d, d)] = dga_ref[...]
        dp_ref[:, pl.ds(6 * d, d)] = dgb_ref[...]

        def chunk(j, carry):
            r0 = pl.multiple_of(j * CONV_ROWS, CONV_ROWS)
            rows = pl.ds(r0, CONV_ROWS)
            for lc in range(d // CONV_LANES):
                lo = lc * CONV_LANES
                ls = pl.ds(lo, CONV_LANES)
                cac = ca_ref[rows, ls].astype(F32)
                vac = va_ref[rows, ls].astype(F32)
                zc = cac * vac
                acc = jnp.zeros((CONV_ROWS, CONV_LANES), F32)
                for k, sl in _taps(dcabuf[pl.ds(r0, CONV_ROWS + HALO), ls], ANTI_A, CONV_ROWS):
                    acc = acc + cwa_ref[pl.ds(k, 1), ls] * sl
                    acca[pl.ds(8 * k, 8), ls] += _rowsum8(sl * zc)
                dp_ref[rows, pl.ds(d + lo, CONV_LANES)] = (acc * vac).astype(BF)
                dp_ref[rows, pl.ds(2 * d + lo, CONV_LANES)] = (acc * cac).astype(BF)
                sgc = sgbuf[rows, ls]
                vbc = vb_ref[rows, ls].astype(F32)
                uc = vbc * sgc
                acc = jnp.zeros((CONV_ROWS, CONV_LANES), F32)
                for k, sl in _taps(du1buf[pl.ds(r0, CONV_ROWS + HALO), ls], ANTI_B, CONV_ROWS):
                    acc = acc + cwb_ref[pl.ds(k, 1), ls] * sl
                    accb[pl.ds(8 * k, 8), ls] += _rowsum8(sl * uc)
                dp_ref[rows, pl.ds(3 * d + lo, CONV_LANES)] = (acc * sgc).astype(BF)
                dp_ref[rows, pl.ds(4 * d + lo, CONV_LANES)] = (acc * vbc * (sgc * (1.0 - sgc))).astype(BF)
            return carry

        lax.fori_loop(0, tm // CONV_ROWS, chunk, 0)

        @pl.when(i == nt - 1)
        def _():
            dcwa_ref[...] = jnp.zeros_like(dcwa_ref)
            dcwb_ref[...] = jnp.zeros_like(dcwb_ref)
            for k in range(K_A):
                dcwa_ref[pl.ds(k, 1), :] = jnp.sum(acca[pl.ds(8 * k, 8), :], axis=0, keepdims=True)
            for k in range(K_B):
                dcwb_ref[pl.ds(k, 1), :] = jnp.sum(accb[pl.ds(8 * k, 8), :], axis=0, keepdims=True)

    row = pl.BlockSpec((tm, d), lambda i: (i, 0))
    nxt = _next_halo_spec(tm, d, 0, s)
    return _host_call(
        body, name="conv_bwd", grid=(nt,),
        in_specs=[row, row, nxt, nxt] + [_col_spec(tm, d, k) for k in (1, 2, 3, 4)]
        + [row, row, row, _const((K_A, d)), _const((K_B, d))],
        out_specs=[pl.BlockSpec((tm, 7 * d), lambda i: (i, 0)), pl.BlockSpec((8, d), lambda i: (0, 0)),
                   pl.BlockSpec((32, d), lambda i: (0, 0))],
        out_shape=[SDS((s, 7 * d), BF), SDS((8, d), F32), SDS((32, d), F32)],
        scratch_shapes=[pltpu.VMEM((tm + HALO, d), F32), pltpu.VMEM((tm + HALO, d), F32),
                        pltpu.VMEM((tm, d), F32), pltpu.VMEM((8 * K_A, d), F32), pltpu.VMEM((8 * K_B, d), F32)],
        args=(dca, du1, dca, du1, p, p, p, p, dba, dga, dgb, cwa, cwb), hosted=hosted)


def _grad_w(a, b, name, hosted=None):
    s, m = a.shape
    n = b.shape[1]
    tm, tn, tk = _tile(m, GRAD_W_TILE), _tile(n, GRAD_W_TILE), _tile(s, GRAD_W_TOKENS)
    nk = s // tk

    def body(a_ref, b_ref, o_ref, ob_ref):
        k = pl.program_id(2)

        @pl.when(k == 0)
        def _():
            o_ref[...] = jnp.zeros_like(o_ref)

        o_ref[...] += _dot_tn(a_ref[...], b_ref[...])

        @pl.when(k == nk - 1)
        def _():
            ob_ref[...] = o_ref[...].astype(BF)

    blk = pl.BlockSpec((tm, tn), lambda i, j, k: (i, j))
    (g, gb), extra = _host_call(
        body, name=name, grid=(m // tm, n // tn, nk),
        in_specs=[pl.BlockSpec((tk, tm), lambda i, j, k: (k, i)), pl.BlockSpec((tk, tn), lambda i, j, k: (k, j))],
        out_specs=[blk, blk], out_shape=[SDS((m, n), F32), SDS((m, n), BF)], scratch_shapes=[], args=(a, b),
        hosted=hosted)
    return g, gb, extra


def _grad_x(dr1, dp, wi, hosted=None):
    s, d = dr1.shape
    n = wi.shape[1]
    tm = min(s, GRAD_X_TILE)

    def body(dr1_ref, dp_ref, w_ref, o_ref):
        o_ref[...] = F32(ALPHA) * dr1_ref[...] + _dot_nt(dp_ref[...], w_ref[...])

    (gx,), extra = _host_call(
        body, name="grad_x", grid=(s // tm,),
        in_specs=[pl.BlockSpec((tm, d), lambda i: (i, 0)), pl.BlockSpec((tm, n), lambda i: (i, 0)), _const((d, n))],
        out_specs=[pl.BlockSpec((tm, d), lambda i: (i, 0))],
        out_shape=[SDS((s, d), F32)], scratch_shapes=[], args=(dr1, dp, wi), hosted=hosted)
    return gx, extra


def _adamw_math(w, g, m, v):
    m2 = ADAM_B1 * m + (1.0 - ADAM_B1) * g
    v2 = ADAM_B2 * v + (1.0 - ADAM_B2) * (g * g)
    m_hat = m2 / (1.0 - ADAM_B1 ** ADAM_STEP)
    v_hat = v2 / (1.0 - ADAM_B2 ** ADAM_STEP)
    delta = -ADAM_LR * (m_hat / (jnp.sqrt(v_hat) + ADAM_EPS) + ADAM_WD * w)
    return delta, m2, v2


def _adamw(w, g, m, v, name):
    r, c = w.shape
    tr = min(r, ROWS_TILE)

    def body(w_ref, g_ref, m_ref, v_ref, g_out, d_ref, m2_ref, v2_ref):
        gv = g_ref[...]
        delta, m2, v2 = _adamw_math(w_ref[...], gv, m_ref[...], v_ref[...])
        g_out[...] = gv
        d_ref[...] = delta
        m2_ref[...] = m2
        v2_ref[...] = v2

    blk = pl.BlockSpec((tr, c), lambda i: (i, 0))
    return pl.pallas_call(
        body, name=name, grid=(r // tr,), in_specs=[blk] * 4, out_specs=[blk] * 4,
        out_shape=[SDS((r, c), F32)] * 4, compiler_params=_cp(("parallel",)),
    )(w, g, m, v)


def _sum_parts(parts, name):
    k, r, c = parts.shape

    def body(p_ref, o_ref):
        acc = p_ref[0]
        for q in range(1, k):
            acc = acc + p_ref[q]
        o_ref[...] = acc

    return pl.pallas_call(
        body, name=name, grid=(1,),
        in_specs=[pl.BlockSpec((k, r, c), lambda i: (0, 0, 0))],
        out_specs=pl.BlockSpec((r, c), lambda i: (0, 0)),
        out_shape=SDS((r, c), F32), compiler_params=_cp(("arbitrary",)),
    )(parts)


def _piece_shape(full_shape, axis):
    r, c = full_shape
    return (r // 2, c // 4) if axis == 1 else (r // 8, c)


def _piece_spec(full_shape, axis, tr, chip_of, half_of):
    hr, wc = _piece_shape(full_shape, axis)
    nb = hr // tr
    if axis == 1:
        return pl.BlockSpec((tr, wc), lambda *a: (half_of(*a) * nb + a[-2], chip_of(*a)))
    return pl.BlockSpec((tr, wc), lambda *a: ((2 * chip_of(*a) + half_of(*a)) * nb + a[-2], 0))


def _place_cast(w, axis, pos, name):
    r, c = w.shape
    tr = min(r, ROWS_TILE)
    nb = r // tr
    full = (r, 4 * c) if axis == 1 else (4 * r, c)
    out_map = (lambda i, pos: (i, pos[0])) if axis == 1 else (lambda i, pos: (pos[0] * nb + i, 0))

    def body(pos_ref, w_ref, o_ref):
        o_ref[...] = w_ref[...].astype(o_ref.dtype)

    gs = pltpu.PrefetchScalarGridSpec(
        num_scalar_prefetch=1, grid=(nb,),
        in_specs=[pl.BlockSpec((tr, c), lambda i, pos: (i, 0))], out_specs=pl.BlockSpec((tr, c), out_map))
    return pl.pallas_call(body, name=name, grid_spec=gs, out_shape=SDS(full, BF),
                          compiler_params=_cp(("arbitrary",)))(pos, w)


def _pair_add(g, land, axis, pos, name):
    hr, wc = _piece_shape(g.shape, axis)
    tr = min(hr, ROWS_TILE)

    def body(pos_ref, g_ref, l_ref, o_ref):
        o_ref[0] = (g_ref[...] + l_ref[0].astype(F32)).astype(BF)

    other = lambda q, i, pos: (pos[0] + 1 + q) % 4
    blk = pl.BlockSpec((1, tr, wc), lambda q, i, pos: (other(q, i, pos), i, 0))
    gs = pltpu.PrefetchScalarGridSpec(
        num_scalar_prefetch=1, grid=(3, hr // tr),
        in_specs=[_piece_spec(g.shape, axis, tr, other, lambda q, i, pos: pos[1]), blk], out_specs=blk)
    return pl.pallas_call(body, name=name, grid_spec=gs, out_shape=SDS((4, hr, wc), BF),
                          compiler_params=_cp(("arbitrary", "arbitrary")))(pos, g, land)


def _chip_sum(g, land1, land2, axis, pos, name):
    hr, wc = _piece_shape(g.shape, axis)
    tr = min(hr, ROWS_TILE)
    nb = hr // tr

    def body(pos_ref, g_ref, l1_ref, l2_ref, o_ref):
        acc = g_ref[...] + l1_ref[0].astype(F32)
        for q in range(3):
            acc = acc + l2_ref[q].astype(F32)
        o_ref[...] = acc

    gs = pltpu.PrefetchScalarGridSpec(
        num_scalar_prefetch=1, grid=(nb,),
        in_specs=[_piece_spec(g.shape, axis, tr, lambda i, pos: pos[0], lambda i, pos: pos[1]),
                  pl.BlockSpec((1, tr, wc), lambda i, pos: (pos[0], i, 0)),
                  pl.BlockSpec((3, tr, wc), lambda i, pos: (0, i, 0))],
        out_specs=pl.BlockSpec((tr, wc), lambda i, pos: (pos[1] * nb + i, 0)))
    return pl.pallas_call(body, name=name, grid_spec=gs, out_shape=SDS((2 * hr, wc), F32),
                          compiler_params=_cp(("arbitrary",)))(pos, g, land1, land2)


ANY = pl.BlockSpec(memory_space=pl.ANY)
COMM = pltpu.CompilerParams(has_side_effects=True)


def _on_each_device(fn):
    x, y, c = lax.axis_index("x"), lax.axis_index("y"), lax.axis_index("c")
    for sx in (0, 1):
        for sy in (0, 1):
            for sc in (0, 1):
                @pl.when(jnp.logical_and(jnp.logical_and(x == sx, y == sy), c == sc))
                def _(sx=sx, sy=sy, sc=sc):
                    fn(sx, sy, sc)


def _remote(src, dst, send_sem, recv_sem, to):
    return pltpu.make_async_remote_copy(src_ref=src, dst_ref=dst, send_sem=send_sem, recv_sem=recv_sem,
                                        device_id=to, device_id_type=MESH)


def _piece_ref(ref, axis, j, h):
    r, c = ref.shape
    hr, wc = _piece_shape((r, c), axis)
    if axis == 1:
        return ref.at[pl.ds(h * hr, hr), pl.ds(j * wc, wc)]
    return ref.at[pl.ds((2 * j + h) * hr, hr), :]


class _Stage:
    def __init__(self, ins, outs, n_sems, make_plan, aliases=None):
        self.ins, self.outs, self.n_sems, self.make_plan = list(ins), list(outs), n_sems, make_plan
        self.aliases = dict(aliases or {})


def _start(plan):
    def dev(sx, sy, sc):
        for cp, _, _ in plan(sx, sy, sc):
            cp.start()

    _on_each_device(dev)


def _finish(plan):
    def dev(sx, sy, sc):
        for _, sent, got in plan(sx, sy, sc):
            sent.wait_send()
            got.wait_recv()

    _on_each_device(dev)


def _comm_call(stage, name):
    n_in, n_out = len(stage.ins), len(stage.outs)

    def body(*refs):
        plan = stage.make_plan(refs[:n_in], refs[n_in:n_in + n_out], *refs[n_in + n_out:])
        _start(plan)
        _finish(plan)

    return pl.pallas_call(
        body, name=name, in_specs=[ANY] * n_in, out_specs=[ANY] * n_out, out_shape=stage.outs,
        input_output_aliases=stage.aliases,
        scratch_shapes=[pltpu.SemaphoreType.DMA((stage.n_sems,)), pltpu.SemaphoreType.DMA((stage.n_sems,))],
        compiler_params=COMM,
    )(*stage.ins)


class _SemsFrom:
    def __init__(self, sems, base):
        self.sems, self.base = sems, base

    @property
    def at(self):
        return self

    def __getitem__(self, k):
        return self.sems.at[self.base + k]


def _both(a, b):
    na, nb = len(a.ins), len(b.ins)
    ma = len(a.outs)

    def make_plan(ins, outs, send_sems, recv_sems):
        pa = a.make_plan(ins[:na], outs[:ma], send_sems, recv_sems)
        pb = b.make_plan(ins[na:], outs[ma:], _SemsFrom(send_sems, a.n_sems), _SemsFrom(recv_sems, a.n_sems))
        return lambda sx, sy, sc: pa(sx, sy, sc) + pb(sx, sy, sc)

    aliases = dict(a.aliases)
    aliases.update({na + i: ma + o for i, o in b.aliases.items()})
    return _Stage(a.ins + b.ins, a.outs + b.outs, a.n_sems + b.n_sems, make_plan, aliases)


def _same(cp):
    return (cp, cp, cp)


def _stage_gather_send(fulls, axes):
    n = len(fulls)

    def make_plan(ins, outs, send_sems, recv_sems):
        def plan(sx, sy, sc):
            cps = []
            for w in range(n):
                mine = _piece_ref(outs[w], axes[w], 2 * sx + sy, sc)
                for r, (fx, fy) in enumerate(CHIP_RELS):
                    k = 3 * w + r
                    to = (sx ^ fx, sy ^ fy, sc)
                    got = _piece_ref(outs[w], axes[w], 2 * (sx ^ fx) + (sy ^ fy), sc)
                    send = _remote(mine, mine, send_sems.at[k], recv_sems.at[k], to)
                    cps.append((send, send, _remote(got, got, send_sems.at[k], recv_sems.at[k], to)))
            return cps

        return plan

    return _Stage(fulls, [SDS(f.shape, f.dtype) for f in fulls], 3 * n, make_plan, {i: i for i in range(n)})


def _stage_gather_forward(fulls, axes):
    n = len(fulls)

    def make_plan(ins, outs, send_sems, recv_sems):
        def plan(sx, sy, sc):
            cps = []
            sib = (sx, sy, 1 - sc)
            for w in range(n):
                for r, (fx, fy) in enumerate(CHIP_RELS):
                    k = 3 * w + r
                    pj = 2 * (sx ^ fx) + (sy ^ fy)
                    have = _piece_ref(outs[w], axes[w], pj, sc)
                    want = _piece_ref(outs[w], axes[w], pj, 1 - sc)
                    send = _remote(have, have, send_sems.at[k], recv_sems.at[k], sib)
                    cps.append((send, send, _remote(want, want, send_sems.at[k], recv_sems.at[k], sib)))
            return cps

        return plan

    return _Stage(fulls, [SDS(f.shape, f.dtype) for f in fulls], 3 * n, make_plan, {i: i for i in range(n)})


def _in_place(fulls, n_sems, make_plan):
    return _Stage(fulls, [SDS(f.shape, f.dtype) for f in fulls], n_sems, make_plan, {i: i for i in range(len(fulls))})


def _stage_gather_neighbours(fulls, axes):
    n = len(fulls)

    def make_plan(ins, outs, send_sems, recv_sems):
        def plan(sx, sy, sc):
            cps = []
            for w in range(n):
                mine = _piece_ref(outs[w], axes[w], 2 * sx + sy, sc)
                for r, (px, py) in enumerate(((sx ^ 1, sy), (sx, sy ^ 1))):
                    got = _piece_ref(outs[w], axes[w], 2 * px + py, sc)
                    send = _remote(mine, mine, send_sems.at[2 * w + r], recv_sems.at[2 * w + r], (px, py, sc))
                    cps.append((send, send, _remote(got, got, send_sems.at[2 * w + r], recv_sems.at[2 * w + r],
                                                    (px, py, sc))))
            return cps

        return plan

    return _in_place(fulls, 2 * n, make_plan)


def _stage_gather_pair(fulls, axes):
    n = len(fulls)

    def make_plan(ins, outs, send_sems, recv_sems):
        def plan(sx, sy, sc):
            cps = []
            sib = (sx, sy, 1 - sc)
            for w in range(n):
                for r, j in enumerate((2 * (sx ^ 1) + sy, 2 * sx + (sy ^ 1))):
                    k = 2 * w + r
                    have, want = _piece_ref(outs[w], axes[w], j, sc), _piece_ref(outs[w], axes[w], j, 1 - sc)
                    send = _remote(have, have, send_sems.at[k], recv_sems.at[k], sib)
                    cps.append((send, send, _remote(want, want, send_sems.at[k], recv_sems.at[k], sib)))
            return cps

        return plan

    return _in_place(fulls, 2 * n, make_plan)


def _stage_gather_diagonal(fulls, axes):
    n = len(fulls)

    def make_plan(ins, outs, send_sems, recv_sems):
        def relay(sx, sy, sc):
            cps = []
            jx, jy, jd = 2 * (sx ^ 1) + sy, 2 * sx + (sy ^ 1), 2 * (sx ^ 1) + (sy ^ 1)
            passed, to = (jx, (sx, sy ^ 1, sc)) if sc == 0 else (jy, (sx ^ 1, sy, sc))
            for w in range(n):
                have, want = _piece_ref(outs[w], axes[w], passed, sc), _piece_ref(outs[w], axes[w], jd, sc)
                send = _remote(have, have, send_sems.at[w], recv_sems.at[w], to)
                cps.append((send, send, _remote(want, want, send_sems.at[w], recv_sems.at[w], to)))
            return cps

        def cross(sx, sy, sc):
            cps = []
            sib = (sx, sy, 1 - sc)
            jd = 2 * (sx ^ 1) + (sy ^ 1)
            for w in range(n):
                have, want = _piece_ref(outs[w], axes[w], jd, sc), _piece_ref(outs[w], axes[w], jd, 1 - sc)
                send = _remote(have, have, send_sems.at[n + w], recv_sems.at[n + w], sib)
                cps.append((send, send, _remote(want, want, send_sems.at[n + w], recv_sems.at[n + w], sib)))
            return cps

        return [relay, cross]

    return _in_place(fulls, 2 * n, make_plan)


def _stage_pair_exchange(grads, axes):
    n = len(grads)

    def make_plan(gs, land, send_sems, recv_sems):
        def plan(sx, sy, sc):
            return [_same(_remote(_piece_ref(gs[w], axes[w], jj, 1 - sc), land[w].at[jj], send_sems.at[4 * w + jj],
                                  recv_sems.at[4 * w + jj], (sx, sy, 1 - sc)))
                    for w in range(n) for jj in range(4)]

        return plan

    return _Stage(grads, [SDS((4,) + _piece_shape(g.shape, a), g.dtype) for g, a in zip(grads, axes)], 4 * n,
                  make_plan)


def _stage_chip_scatter(pieces):
    n = len(pieces)

    def make_plan(ps, land, send_sems, recv_sems):
        def plan(sx, sy, sc):
            return [_same(_remote(ps[w].at[2 * (sx ^ fx) + (sy ^ fy)], land[w].at[r], send_sems.at[3 * w + r],
                                  recv_sems.at[3 * w + r], (sx ^ fx, sy ^ fy, sc)))
                    for w in range(n) for r, (fx, fy) in enumerate(CHIP_RELS)]

        return plan

    return _Stage(pieces, [SDS((3,) + p.shape[1:], p.dtype) for p in pieces], 3 * n, make_plan)


def _stage_pair_share(shards):
    n = len(shards)

    def make_plan(ins, outs, send_sems, recv_sems):
        def plan(sx, sy, sc):
            cps = []
            sib = (sx, sy, 1 - sc)
            for w in range(n):
                hr = shards[w].shape[0] // 2
                mine = outs[w].at[pl.ds(sc * hr, hr), :]
                theirs = outs[w].at[pl.ds((1 - sc) * hr, hr), :]
                send = _remote(mine, mine, send_sems.at[w], recv_sems.at[w], sib)
                cps.append((send, send, _remote(theirs, theirs, send_sems.at[w], recv_sems.at[w], sib)))
            return cps

        return plan

    return _Stage(shards, [SDS(g.shape, g.dtype) for g in shards], n, make_plan, {i: i for i in range(n)})


def _stage_gather_small(stack):
    def make_plan(ins, outs, send_sems, recv_sems):
        def plan(sx, sy, sc):
            mine = outs[0].at[4 * sx + 2 * sy + sc]
            return [_same(_remote(mine, mine, send_sems.at[k], recv_sems.at[k], (sx ^ fx, sy ^ fy, sc ^ fc)))
                    for k, (fx, fy, fc) in enumerate(DEV_RELS)]

        return plan

    return _Stage([stack], [SDS(stack.shape, stack.dtype)], 7, make_plan, {0: 0})


def kernel(x, w_in, conv_a_w, w_out_a, conv_b_w, conv_b_bias, ln_b_gamma, ln_b_beta, w_out_b, w_o, ln1_gamma, ln1_beta, w_up, w_down, ln2_gamma, ln2_beta, loss_target, m_w_in, m_conv_a_w, m_w_out_a, m_conv_b_w, m_conv_b_bias, m_ln_b_gamma, m_ln_b_beta, m_w_out_b, m_w_o, m_ln1_gamma, m_ln1_beta, m_w_up, m_w_down, m_ln2_gamma, m_ln2_beta, v_w_in, v_conv_a_w, v_w_out_a, v_conv_b_w, v_conv_b_bias, v_ln_b_gamma, v_ln_b_beta, v_w_out_b, v_w_o, v_ln1_gamma, v_ln1_beta, v_w_up, v_w_down, v_ln2_gamma, v_ln2_beta):
    s, d = x.shape[1], x.shape[2]
    xs = x.reshape(s, d)
    tgt = loss_target.reshape(s, d)
    dq = d // 4
    chip = 2 * lax.axis_index("x") + lax.axis_index("y")
    core = lax.axis_index("c")
    pos = jnp.stack([chip, core]).astype(jnp.int32)
    names = ("w_in", "w_out_a", "w_out_b", "w_o", "w_up", "w_down")
    axes = (1, 0, 0, 0, 1, 0)

    conv_pack = jnp.concatenate([jnp.pad(conv_a_w, ((0, 8 - K_A), (0, 0))), jnp.pad(conv_b_w, ((0, 32 - K_B), (0, 0))),
                                 jnp.zeros((8, dq), F32)], axis=0)
    conv_full = lax.dynamic_update_slice(jnp.zeros((conv_pack.shape[0], d), F32), conv_pack, (0, chip * dq))
    wi_own = _place_cast(w_in, 1, pos, "place_w_in")
    vec = lambda a: a.reshape(1, d)
    bias_b, lbg, lbb = vec(conv_b_bias), vec(ln_b_gamma), vec(ln_b_beta)
    l1g, l1b, l2g, l2b = vec(ln1_gamma), vec(ln1_beta), vec(ln2_gamma), vec(ln2_beta)

    (p, xb, *placed), landed = _in_proj_own(xs, w_in, (w_out_a, w_out_b, w_o, w_up, w_down), axes[1:], pos,
                                            _stage_gather_neighbours([wi_own, conv_full], (1, 1)))
    fulls = [wi_own] + placed
    wi, convs = _comm_call(_stage_gather_pair(landed, (1, 1)), "gather_pair_w_in")
    p, (wi, convs), small3 = _in_proj_rest(xb, wi, p, pos, _stage_gather_diagonal([wi, convs], (1, 1)),
                                           _stage_gather_send(fulls[1:4], axes[1:4]))
    cwa, cwb = convs[0:K_A], convs[8:8 + K_B]
    (conva, yapre, nb, rstdb, u3), landed = _conv_fwd(
        p, cwa, cwb, bias_b, lbg, lbb, d,
        _both(_stage_gather_forward(small3, axes[1:4]), _stage_gather_send(fulls[4:6], axes[4:6])))
    woa, wob, wo = landed[:3]
    (ya, yb, merged, n1, rstd1), (wup, wdown) = _mixer_out(yapre, u3, p, xs, woa, wob, wo, d,
                                                           _stage_gather_forward(landed[3:], axes[4:6]))
    x1b, hb, dhpre, dr2b, dr1, dr1b, acc_mlp = _mlp(n1, rstd1, tgt, wup, wdown, l1g, l1b, l2g, l2b)
    g_up, gb_up, _ = _grad_w(x1b, dhpre, "grad_w_up")
    g_down, gb_down, _ = _grad_w(hb, dr2b, "grad_w_down")
    dya, dyb, dba, dga, dgb, dca, du1, acc_mix = _mixer_bwd_local(dr1b, p, ya, yb, conva, nb, rstdb, wo, woa, wob,
                                                                   lbg, lbb)
    g_oa, gb_oa, _ = _grad_w(yapre, dya, "grad_w_out_a")
    g_ob, gb_ob, _ = _grad_w(u3, dyb, "grad_w_out_b")
    g_o, gb_o, _ = _grad_w(merged, dr1b, "grad_w_o")

    early, e_axes, e_names = [g_oa, g_ob, g_o, g_up, g_down], axes[1:], names[1:]
    (dp, dcwa, dcwb), land1 = _conv_bwd(dca, du1, p, dba, dga, dgb, cwa, cwb, d,
                                        _stage_pair_exchange([gb_oa, gb_ob, gb_o, gb_up, gb_down], e_axes))
    pieces = [_pair_add(g, l, a, pos, "pair_add_" + nm) for g, l, a, nm in zip(early, land1, e_axes, e_names)]
    pack = jnp.concatenate([dcwa, dcwb, acc_mix, acc_mlp], axis=0)
    stack = lax.dynamic_update_slice(jnp.zeros((8,) + pack.shape, F32), pack[None], (2 * chip + core, 0, 0))
    g_wi, gb_wi, landed = _grad_w(xb, dp, "grad_w_in",
                                  _both(_stage_chip_scatter(pieces), _stage_gather_small(stack)))
    land2, stack = landed[:-1], landed[-1]
    halves = [_chip_sum(g, l1, l2, a, pos, "chip_sum_" + nm)
              for g, l1, l2, a, nm in zip(early, land1, land2, e_axes, e_names)]
    (land1_in,) = _comm_call(_stage_pair_exchange([gb_wi], (1,)), "pair_exchange_w_in")
    piece_in = _pair_add(g_wi, land1_in, 1, pos, "pair_add_w_in")
    grad_x, landed = _grad_x(dr1, dp, wi, _both(_stage_chip_scatter([piece_in]), _stage_pair_share(halves)))
    land2_in, (g_oa, g_ob, g_o, g_up, g_down) = landed[0], landed[1:]
    half_in = _chip_sum(g_wi, land1_in, land2_in, 1, pos, "chip_sum_w_in")
    (g_in,) = _comm_call(_stage_pair_share([half_in]), "pair_share_w_in")
    small = _sum_parts(stack, "small_sum")
    g_ca = lax.dynamic_slice(small, (0, chip * dq), (K_A, dq))
    g_cb = lax.dynamic_slice(small, (8, chip * dq), (K_B, dq))
    g_vec = jnp.stack([small[r] for r in (42, 40, 41, 51, 52, 49, 50)])

    loss = (0.5 / d) * jnp.sum(small[48])

    big = {}
    for name, w, g, m, v in (("w_in", w_in, g_in, m_w_in, v_w_in), ("w_out_a", w_out_a, g_oa, m_w_out_a, v_w_out_a),
                             ("w_out_b", w_out_b, g_ob, m_w_out_b, v_w_out_b), ("w_o", w_o, g_o, m_w_o, v_w_o),
                             ("w_up", w_up, g_up, m_w_up, v_w_up), ("w_down", w_down, g_down, m_w_down, v_w_down),
                             ("conv_a_w", conv_a_w, g_ca, m_conv_a_w, v_conv_a_w),
                             ("conv_b_w", conv_b_w, g_cb, m_conv_b_w, v_conv_b_w)):
        big[name] = tuple(_adamw(w, g, m, v, "adamw_" + name))
    vec_names = ("conv_b_bias", "ln_b_gamma", "ln_b_beta", "ln1_gamma", "ln1_beta", "ln2_gamma", "ln2_beta")
    w7 = jnp.stack([conv_b_bias, ln_b_gamma, ln_b_beta, ln1_gamma, ln1_beta, ln2_gamma, ln2_beta])
    m7 = jnp.stack([m_conv_b_bias, m_ln_b_gamma, m_ln_b_beta, m_ln1_gamma, m_ln1_beta, m_ln2_gamma, m_ln2_beta])
    v7 = jnp.stack([v_conv_b_bias, v_ln_b_gamma, v_ln_b_beta, v_ln1_gamma, v_ln1_beta, v_ln2_gamma, v_ln2_beta])
    g7, d7, nm7, nv7 = _adamw(w7, g_vec, m7, v7, "adamw_vectors")
    for q, name in enumerate(vec_names):
        big[name] = (g7[q], d7[q], nm7[q], nv7[q])

    order = ("w_in", "conv_a_w", "w_out_a", "conv_b_w", "conv_b_bias", "ln_b_gamma", "ln_b_beta", "w_out_b", "w_o",
             "ln1_gamma", "ln1_beta", "w_up", "w_down", "ln2_gamma", "ln2_beta")
    outs = [loss, grad_x.reshape(x.shape)]
    for part in range(4):
        outs += [big[name][part] for name in order]
    return tuple(outs)
```

```python
import jax
import jax.numpy as jnp
from jax import lax
from jax.experimental import pallas as pl
from jax.experimental.pallas import tpu as pltpu

F32 = jnp.float32
BF = jnp.bfloat16
SDS = jax.ShapeDtypeStruct
MESH = pl.DeviceIdType.MESH

ALPHA = 2.0 ** 0.25
LN_EPS = 1e-5
K_A = 3
K_B = 31
HALO = 32
CONV_TILE = 512
CONV_ROWS = 64
CONV_LANES = 128
VMEM_LIMIT_MB = 56
PROJ_TILE = 1024
PROJ_REST_TILE = 2048
MIXER_TILE = 512
GRAD_X_TILE = 512
LOCAL_TILE = 256
MIXER_BWD_TILE = 512
FF_CHUNK = 1024
GRAD_W_TILE = 1024
GRAD_W_TOKENS = 2048
GRAD_W_TOKENS_MLP = 4096
SUM_ROWS = 512
ROWS_TILE = 256
ADAM_LR = 0.001
ADAM_B1 = 0.9
ADAM_B2 = 0.999
ADAM_EPS = 1e-08
ADAM_WD = 0.01
ADAM_STEP = 10
P_DT = BF
CHIP_RELS = ((1, 0), (0, 1), (1, 1))
DEV_RELS = tuple((fx, fy, fc) for fx in (0, 1) for fy in (0, 1) for fc in (0, 1))[1:]


def _cp(sem=None, side_effects=False):
    return pltpu.CompilerParams(dimension_semantics=sem, vmem_limit_bytes=VMEM_LIMIT_MB << 20,
                                has_side_effects=side_effects)


def _const(shape):
    return pl.BlockSpec(shape, lambda *_: (0,) * len(shape), pipeline_mode=pl.Buffered(1))


def _sig(v):
    return jax.nn.sigmoid(v)


def _ln_fwd(r):
    mu = jnp.mean(r, axis=-1, keepdims=True)
    xc = r - mu
    var = jnp.mean(xc * xc, axis=-1, keepdims=True)
    rstd = lax.rsqrt(var + LN_EPS)
    return xc * rstd, rstd


def _ln_bwd(dn, n, rstd):
    m1 = jnp.mean(dn, axis=-1, keepdims=True)
    m2 = jnp.mean(dn * n, axis=-1, keepdims=True)
    return rstd * (dn - m1 - n * m2)


def _dot(a, b):
    return jnp.dot(a, b, preferred_element_type=F32)


def _dot_nt(a, b):
    return lax.dot_general(a, b, (((1,), (1,)), ((), ())), preferred_element_type=F32)


def _dot_tn(a, b):
    return lax.dot_general(a, b, (((0,), (0,)), ((), ())), preferred_element_type=F32)


def _tile(n, pref):
    if n <= pref:
        return n
    return max(t for t in range(128, pref + 1, 128) if n % t == 0)


def _rowsum8(v):
    acc = v[0:8]
    for g in range(1, v.shape[0] // 8):
        acc = acc + v[8 * g:8 * g + 8]
    return acc


def _taps(win, offsets, rows):
    r_all = win.shape[0]
    by_res = {}
    for k, o in enumerate(offsets):
        by_res.setdefault(o % 8, []).append((k, o // 8))
    for s, taps in sorted(by_res.items()):
        r = win if s == 0 else pltpu.roll(win, r_all - s, 0)
        for k, q in taps:
            yield k, r[8 * q:8 * q + rows]


CAUSAL_A = [HALO - (K_A - 1) + k for k in range(K_A)]
CAUSAL_B = [HALO - (K_B - 1) + k for k in range(K_B)]
ANTI_A = [K_A - 1 - k for k in range(K_A)]
ANTI_B = [K_B - 1 - k for k in range(K_B)]


def _host_call(body, *, name, grid, in_specs, out_specs, out_shape, scratch_shapes, args, hosted, prefetch=None,
               aliases=None, body_gets_stage_refs=False):
    n_in, n_out, n_scr = len(in_specs), len(out_specs), len(scratch_shapes)
    n_steps = 1
    for size in grid:
        n_steps *= size
    if isinstance(hosted, _Stage):
        hosted = [(hosted, 0, n_steps - 1)]
    n_pre = 0 if prefetch is None else 1
    pre = () if prefetch is None else (prefetch,)
    sem = ("arbitrary",) * len(grid)
    own_aliases = {n_pre + a: b for a, b in (aliases or {}).items()}

    def call(kernel_body, ins, outs, shapes, scratch, all_aliases, side_effects, operands):
        gs = pltpu.PrefetchScalarGridSpec(num_scalar_prefetch=n_pre, grid=grid, in_specs=ins, out_specs=outs,
                                          scratch_shapes=scratch)
        return pl.pallas_call(kernel_body, name=name, grid_spec=gs, out_shape=shapes,
                              input_output_aliases=all_aliases,
                              compiler_params=_cp(sem, side_effects=side_effects))(*pre, *operands)

    if hosted is None:
        res = call(body, list(in_specs), list(out_specs), list(out_shape), list(scratch_shapes), own_aliases, False,
                   args)
        return list(res), []
    stages = [st for st, _, _ in hosted]
    h_ins = [a for st in stages for a in st.ins]
    h_outs = [o for st in stages for o in st.outs]
    borrowed = {hi: k for hi, a in enumerate(h_ins) for k, own in enumerate(args) if a is own}
    passed = [hi for hi in range(len(h_ins)) if hi not in borrowed]
    h_in, h_out, n_sems = len(passed), len(h_outs), sum(st.n_sems for st in stages)

    def full_body(*refs):
        pre_refs, refs = refs[:n_pre], refs[n_pre:]
        ins, refs = refs[:n_in], refs[n_in:]
        hins = [None] * len(h_ins)
        for hi, ref in zip(passed, refs[:h_in]):
            hins[hi] = ref
        refs = refs[h_in:]
        outs, refs = refs[:n_out], refs[n_out:]
        houts, refs = refs[:h_out], refs[h_out:]
        scr, (send_sems, recv_sems) = refs[:n_scr], refs[n_scr:]
        step = 0
        for a, size in enumerate(grid):
            step = step * size + pl.program_id(a)
        phases, i0, o0, k0 = [], 0, 0, 0
        for st, starts, finishes in hosted:
            plans = st.make_plan(hins[i0:i0 + len(st.ins)], houts[o0:o0 + len(st.outs)],
                                 _SemsFrom(send_sems, k0), _SemsFrom(recv_sems, k0))
            if not isinstance(plans, list):
                plans, starts, finishes = [plans], [starts], [finishes]
            phases += list(zip(plans, starts, finishes))
            i0, o0, k0 = i0 + len(st.ins), o0 + len(st.outs), k0 + st.n_sems
        for plan, at, _ in phases:
            @pl.when(step == at)
            def _(plan=plan):
                _start(plan)

        if body_gets_stage_refs:
            body(*pre_refs, *ins, *outs, *scr, houts)
        else:
            body(*pre_refs, *ins, *outs, *scr)

        for plan, _, at in phases:
            @pl.when(step == at)
            def _(plan=plan):
                _finish(plan)

    all_aliases = dict(own_aliases)
    i0 = o0 = 0
    for st in stages:
        for a, b in st.aliases.items():
            hi = i0 + a
            operand = borrowed[hi] if hi in borrowed else n_in + passed.index(hi)
            all_aliases[n_pre + operand] = n_out + o0 + b
        i0, o0 = i0 + len(st.ins), o0 + len(st.outs)
    res = call(full_body, list(in_specs) + [ANY] * h_in, list(out_specs) + [ANY] * h_out,
               list(out_shape) + h_outs,
               list(scratch_shapes) + [pltpu.SemaphoreType.DMA((n_sems,)), pltpu.SemaphoreType.DMA((n_sems,))],
               all_aliases, True, (*args, *[h_ins[hi] for hi in passed]))
    return list(res[:n_out]), list(res[n_out:])


def _in_proj_own(x, w_shard, others, other_axes, pos, hosted):
    s, d = x.shape
    tn = w_shard.shape[1]
    tm = min(s, PROJ_TILE)
    n_o = len(others)

    def body(pos_ref, x_ref, w_ref, *refs):
        o_in, (p_ref, xb_ref), o_out, wb = refs[:n_o], refs[n_o:n_o + 2], refs[n_o + 2:2 * n_o + 2], refs[-1]

        @pl.when(pl.program_id(0) == 0)
        def _():
            wb[...] = w_ref[...].astype(BF)
            for src, dst in zip(o_in, o_out):
                dst[...] = src[...].astype(BF)

        xb = x_ref[...].astype(BF)
        xb_ref[...] = xb
        p_ref[...] = _dot(xb, wb[...]).astype(p_ref.dtype)

    whole = lambda a: pl.BlockSpec(a.shape, lambda i, pos: (0, 0), pipeline_mode=pl.Buffered(1))
    placed = lambda a, ax: pl.BlockSpec(a.shape, (lambda i, pos: (0, pos[0])) if ax == 1 else (lambda i, pos: (pos[0], 0)))
    full = lambda a, ax: SDS((a.shape[0], 4 * a.shape[1]) if ax == 1 else (4 * a.shape[0], a.shape[1]), BF)
    outs, extra = _host_call(
        body, name="in_proj_own", grid=(s // tm,), prefetch=pos,
        in_specs=[pl.BlockSpec((tm, d), lambda i, pos: (i, 0)), whole(w_shard)] + [whole(a) for a in others],
        out_specs=[pl.BlockSpec((tm, tn), lambda i, pos: (i, pos[0])), pl.BlockSpec((tm, d), lambda i, pos: (i, 0))]
        + [placed(a, ax) for a, ax in zip(others, other_axes)],
        out_shape=[SDS((s, 4 * tn), P_DT), SDS((s, d), BF)] + [full(a, ax) for a, ax in zip(others, other_axes)],
        scratch_shapes=[pltpu.VMEM((d, tn), BF)], args=(x, w_shard, *others), hosted=hosted)
    return outs, extra


def _in_proj_rest(xb, wi, p, pos, diagonal, other):
    s, d = xb.shape
    n = wi.shape[1]
    tm, tn = min(s // 4, PROJ_REST_TILE), n // 4
    ni = s // tm

    def body(pos_ref, xb_ref, w_ref, p_in, p_ref, wdiag, dsem, stage_outs):
        j = pl.program_id(0)
        step = j * ni + pl.program_id(1)

        def diagonal_block(act):
            for chip in range(4):
                @pl.when(pos_ref[0] == chip)
                def _(chip=chip):
                    act(pltpu.make_async_copy(stage_outs[0].at[:, pl.ds((chip ^ 3) * tn, tn)], wdiag, dsem))

        @pl.when(step == 2 * ni - 1)
        def _():
            diagonal_block(lambda cp: cp.start())

        @pl.when(step == 2 * ni)
        def _():
            diagonal_block(lambda cp: cp.wait())

        @pl.when(j < 2)
        def _():
            p_ref[...] = _dot(xb_ref[...], w_ref[...]).astype(p_ref.dtype)

        @pl.when(j == 2)
        def _():
            p_ref[...] = _dot(xb_ref[...], wdiag[...]).astype(p_ref.dtype)

    def col(j, i, pos):
        return lax.bitwise_xor(pos[0], jnp.where(j == 0, 2, jnp.where(j == 1, 1, 3)))

    def piped_col(j, i, pos):
        return lax.bitwise_xor(pos[0], jnp.where(j == 0, 2, 1))

    (p,), extra = _host_call(
        body, name="in_proj_rest", grid=(3, ni), prefetch=pos, aliases={2: 0}, body_gets_stage_refs=True,
        in_specs=[pl.BlockSpec((tm, d), lambda j, i, pos: (i, 0)),
                  pl.BlockSpec((d, tn), lambda j, i, pos: (0, piped_col(j, i, pos))), ANY],
        out_specs=[pl.BlockSpec((tm, tn), lambda j, i, pos: (i, col(j, i, pos)))],
        out_shape=[SDS((s, n), P_DT)], scratch_shapes=[pltpu.VMEM((d, tn), BF), pltpu.SemaphoreType.DMA(())],
        args=(xb, wi, p),
        hosted=[(diagonal, [0, ni + 1], [ni, 2 * ni - 2]), (other, 0, 3 * ni - 1)])
    n_d = len(diagonal.outs)
    return p, extra[:n_d], extra[n_d:]


def _col_spec(tm, d, k):
    return pl.BlockSpec((tm, d), lambda i, k=k: (i, k))


def _prev_halo_spec(tm, d, k):
    r = tm // HALO
    return pl.BlockSpec((HALO, d), lambda i, k=k: (jnp.maximum(i * r - 1, 0), k))


def _next_halo_spec(tm, d, k, s):
    r = tm // HALO
    last = s // HALO - 1
    return pl.BlockSpec((HALO, d), lambda i, k=k: (jnp.minimum((i + 1) * r, last), k))


def _conv_fwd(p, cwa, cwb, bias_b, lbg, lbb, d, hosted=None):
    s = p.shape[0]
    tm = min(s, CONV_TILE)
    nt = s // tm

    def body(ba_ref, ca_ref, va_ref, vb_ref, gb_ref, hca_ref, hva_ref, hvb_ref, hgb_ref,
             cwa_ref, cwb_ref, bias_ref, lbg_ref, lbb_ref,
             conva_ref, yapre_ref, nb_ref, rstdb_ref, u3_ref,
             zbuf, ubuf, u1buf):
        i = pl.program_id(0)
        keep = (i > 0).astype(F32)
        zbuf[pl.ds(0, HALO), :] = hca_ref[...].astype(F32) * hva_ref[...].astype(F32) * keep
        ubuf[pl.ds(0, HALO), :] = hvb_ref[...].astype(F32) * _sig(hgb_ref[...].astype(F32)) * keep
        zbuf[pl.ds(HALO, tm), :] = ca_ref[...].astype(F32) * va_ref[...].astype(F32)
        ubuf[pl.ds(HALO, tm), :] = vb_ref[...].astype(F32) * _sig(gb_ref[...].astype(F32))

        def chunk(j, carry):
            r0 = pl.multiple_of(j * CONV_ROWS, CONV_ROWS)
            rows = pl.ds(r0, CONV_ROWS)
            for lc in range(d // CONV_LANES):
                ls = pl.ds(lc * CONV_LANES, CONV_LANES)
                acc = jnp.zeros((CONV_ROWS, CONV_LANES), F32)
                for k, sl in _taps(zbuf[pl.ds(r0, CONV_ROWS + HALO), ls], CAUSAL_A, CONV_ROWS):
                    acc = acc + cwa_ref[pl.ds(k, 1), ls] * sl
                conva_ref[rows, ls] = acc.astype(BF)
                yapre_ref[rows, ls] = (ba_ref[rows, ls].astype(F32) * acc).astype(BF)
                acc = jnp.zeros((CONV_ROWS, CONV_LANES), F32)
                for k, sl in _taps(ubuf[pl.ds(r0, CONV_ROWS + HALO), ls], CAUSAL_B, CONV_ROWS):
                    acc = acc + cwb_ref[pl.ds(k, 1), ls] * sl
                u1buf[rows, ls] = acc + bias_ref[:, ls]
            return carry

        lax.fori_loop(0, tm // CONV_ROWS, chunk, 0)
        nb, rstd = _ln_fwd(u1buf[...])
        nb_ref[...] = nb
        rstdb_ref[...] = rstd
        u2 = nb * lbg_ref[...] + lbb_ref[...]
        u3_ref[...] = (u2 * _sig(u2)).astype(BF)

    vec = _const((1, d))
    return _host_call(
        body, name="conv_fwd", grid=(nt,),
        in_specs=[_col_spec(tm, d, k) for k in range(5)] + [_prev_halo_spec(tm, d, k) for k in (1, 2, 3, 4)]
        + [_const((K_A, d)), _const((K_B, d)), vec, vec, vec],
        out_specs=[pl.BlockSpec((tm, d), lambda i: (i, 0)), pl.BlockSpec((tm, d), lambda i: (i, 0)),
                   pl.BlockSpec((tm, d), lambda i: (i, 0)), pl.BlockSpec((tm, 1), lambda i: (i, 0)),
                   pl.BlockSpec((tm, d), lambda i: (i, 0))],
        out_shape=[SDS((s, d), BF), SDS((s, d), BF), SDS((s, d), F32), SDS((s, 1), F32), SDS((s, d), BF)],
        scratch_shapes=[pltpu.VMEM((HALO + tm, d), F32), pltpu.VMEM((HALO + tm, d), F32), pltpu.VMEM((tm, d), F32)],
        args=(p, p, p, p, p, p, p, p, p, cwa, cwb, bias_b, lbg, lbb), hosted=hosted)


def _mixer_out(yapre, u3, p, x, woa, wob, wo, d, hosted=None):
    s = x.shape[0]
    tm = min(s, MIXER_TILE)

    def body(yapre_ref, u3_ref, ga_ref, gb_ref, x_ref, woa_ref, wob_ref, wo_ref,
             ya_ref, yb_ref, merged_ref, n1_ref, rstd1_ref):
        ya = _dot(yapre_ref[...], woa_ref[...])
        yb = _dot(u3_ref[...], wob_ref[...])
        ya_ref[...] = ya.astype(BF)
        yb_ref[...] = yb.astype(BF)
        merged = (_sig(ga_ref[...].astype(F32)) * ya + _sig(gb_ref[...].astype(F32)) * yb).astype(BF)
        merged_ref[...] = merged
        r1 = F32(ALPHA) * x_ref[...] + _dot(merged, wo_ref[...])
        n1, rstd1 = _ln_fwd(r1)
        n1_ref[...] = n1
        rstd1_ref[...] = rstd1

    row = pl.BlockSpec((tm, d), lambda i: (i, 0))
    return _host_call(
        body, name="mixer_out", grid=(s // tm,),
        in_specs=[row, row, _col_spec(tm, d, 5), _col_spec(tm, d, 6), row,
                  _const((d, d)), _const((d, d)), _const((d, d))],
        out_specs=[row, row, row, row, pl.BlockSpec((tm, 1), lambda i: (i, 0))],
        out_shape=[SDS((s, d), BF), SDS((s, d), BF), SDS((s, d), BF), SDS((s, d), F32), SDS((s, 1), F32)],
        scratch_shapes=[], args=(yapre, u3, p, p, x, woa, wob, wo), hosted=hosted)


def _mlp(n1, rstd1, tgt, wup, wdown, l1g, l1b, l2g, l2b):
    s, d = n1.shape
    dff = wup.shape[1]
    tm = min(s, LOCAL_TILE)
    fc = min(dff, FF_CHUNK)
    nq = dff // fc

    def body(n1_ref, rstd1_ref, tgt_ref, wup_ref, wdown_ref, l1g_ref, l1b_ref, l2g_ref, l2b_ref,
             x1b_ref, hb_ref, dhpre_ref, dr2b_ref, dr1_ref, dr1b_ref, acc_ref, rbuf):
        i = pl.program_id(0)
        n1v = n1_ref[...]
        x1 = n1v * l1g_ref[...] + l1b_ref[...]
        x1b = x1.astype(BF)
        x1b_ref[...] = x1b
        ff = jnp.zeros((tm, d), F32)
        for q in range(nq):
            cs = pl.ds(q * fc, fc)
            r = jnp.maximum(_dot(x1b, wup_ref[:, cs]), 0.0)
            rbuf[:, cs] = r
            hq = (r * r).astype(BF)
            hb_ref[:, cs] = hq
            ff = ff + _dot(hq, wdown_ref[cs, :])
        n2, rstd2 = _ln_fwd(F32(ALPHA) * x1 + ff)
        x2 = n2 * l2g_ref[...] + l2b_ref[...]
        err = x2 - tgt_ref[...]
        dx2 = err * F32(1.0 / d)
        dr2 = _ln_bwd(dx2 * l2g_ref[...], n2, rstd2)
        dr2b = dr2.astype(BF)
        dr2b_ref[...] = dr2b
        dx1 = F32(ALPHA) * dr2
        for q in range(nq):
            cs = pl.ds(q * fc, fc)
            dh = _dot_nt(dr2b, wdown_ref[cs, :])
            dhp = (dh * (2.0 * rbuf[:, cs])).astype(BF)
            dhpre_ref[:, cs] = dhp
            dx1 = dx1 + _dot_nt(dhp, wup_ref[:, cs])
        dr1 = _ln_bwd(dx1 * l1g_ref[...], n1v, rstd1_ref[...])
        dr1_ref[...] = dr1
        dr1b_ref[...] = dr1.astype(BF)

        @pl.when(i == 0)
        def _():
            acc_ref[...] = jnp.zeros_like(acc_ref)

        for q, val in enumerate((err * err, dx2 * n2, dx2, dx1 * n1v, dx1)):
            acc_ref[pl.ds(q, 1), :] += jnp.sum(val, axis=0, keepdims=True)

    row = pl.BlockSpec((tm, d), lambda i: (i, 0))
    wide = pl.BlockSpec((tm, dff), lambda i: (i, 0))
    vec = _const((1, d))
    return pl.pallas_call(
        body, name="mlp_fwd_bwd", grid=(s // tm,),
        in_specs=[row, pl.BlockSpec((tm, 1), lambda i: (i, 0)), row, _const((d, dff)), _const((dff, d)),
                  vec, vec, vec, vec],
        out_specs=[row, wide, wide, row, row, row, pl.BlockSpec((8, d), lambda i: (0, 0))],
        out_shape=[SDS((s, d), BF), SDS((s, dff), BF), SDS((s, dff), BF), SDS((s, d), BF), SDS((s, d), F32),
                   SDS((s, d), BF), SDS((8, d), F32)],
        scratch_shapes=[pltpu.VMEM((tm, dff), F32)],
        compiler_params=_cp(("arbitrary",)),
    )(n1, rstd1, tgt, wup, wdown, l1g, l1b, l2g, l2b)


def _mixer_bwd_local(dr1b, p, ya, yb, conva, nb, rstdb, wo, woa, wob, lbg, lbb):
    s, d = ya.shape
    tm = min(s, MIXER_BWD_TILE)

    def body(dr1b_ref, ba_ref, ga_ref, gb_ref, ya_ref, yb_ref, conva_ref, nb_ref, rstdb_ref,
             wo_ref, woa_ref, wob_ref, lbg_ref, lbb_ref,
             dya_ref, dyb_ref, dba_ref, dga_ref, dgb_ref, dca_ref, du1_ref, acc_ref):
        i = pl.program_id(0)
        dmerged = _dot_nt(dr1b_ref[...], wo_ref[...])
        sa = _sig(ga_ref[...].astype(F32))
        sb = _sig(gb_ref[...].astype(F32))
        dya = (dmerged * sa).astype(BF)
        dyb = (dmerged * sb).astype(BF)
        dya_ref[...] = dya
        dyb_ref[...] = dyb
        dga_ref[...] = (dmerged * ya_ref[...].astype(F32) * (sa * (1.0 - sa))).astype(BF)
        dgb_ref[...] = (dmerged * yb_ref[...].astype(F32) * (sb * (1.0 - sb))).astype(BF)
        dyapre = _dot_nt(dya, woa_ref[...])
        dba_ref[...] = (dyapre * conva_ref[...].astype(F32)).astype(BF)
        dca_ref[...] = (dyapre * ba_ref[...].astype(F32)).astype(BF)
        du3 = _dot_nt(dyb, wob_ref[...])
        nbv = nb_ref[...]
        u2 = nbv * lbg_ref[...] + lbb_ref[...]
        sg = _sig(u2)
        du2 = du3 * (sg * (1.0 + u2 * (1.0 - sg)))
        du1 = _ln_bwd(du2 * lbg_ref[...], nbv, rstdb_ref[...])
        du1_ref[...] = du1.astype(BF)

        @pl.when(i == 0)
        def _():
            acc_ref[...] = jnp.zeros_like(acc_ref)

        for q, val in enumerate((du2 * nbv, du2, du1)):
            acc_ref[pl.ds(q, 1), :] += jnp.sum(val, axis=0, keepdims=True)

    row = pl.BlockSpec((tm, d), lambda i: (i, 0))
    vec = _const((1, d))
    return pl.pallas_call(
        body, name="mixer_bwd_local", grid=(s // tm,),
        in_specs=[row, _col_spec(tm, d, 0), _col_spec(tm, d, 5), _col_spec(tm, d, 6), row, row, row, row,
                  pl.BlockSpec((tm, 1), lambda i: (i, 0)), _const((d, d)), _const((d, d)), _const((d, d)), vec, vec],
        out_specs=[row, row, row, row, row, row, row, pl.BlockSpec((8, d), lambda i: (0, 0))],
        out_shape=[SDS((s, d), BF)] * 7 + [SDS((8, d), F32)],
        compiler_params=_cp(("arbitrary",)),
    )(dr1b, p, p, p, ya, yb, conva, nb, rstdb, wo, woa, wob, lbg, lbb)


def _conv_bwd(dca, du1, p, dba, dga, dgb, cwa, cwb, d, hosted=None):
    s = dca.shape[0]
    tm = min(s, CONV_TILE)
    nt = s // tm

    def body(dca_ref, du1_ref, ndca_ref, ndu1_ref, ca_ref, va_ref, vb_ref, gb_ref,
             dba_ref, dga_ref, dgb_ref, cwa_ref, cwb_ref,
             dp_ref, dcwa_ref, dcwb_ref,
             dcabuf, du1buf, sgbuf, acca, accb):
        i = pl.program_id(0)
        keep_next = (i < nt - 1).astype(F32)

        @pl.when(i == 0)
        def _():
            acca[...] = jnp.zeros_like(acca)
            accb[...] = jnp.zeros_like(accb)

        sgbuf[...] = _sig(gb_ref[...].astype(F32))
        dcabuf[pl.ds(0, tm), :] = dca_ref[...].astype(F32)
        dcabuf[pl.ds(tm, HALO), :] = ndca_ref[...].astype(F32) * keep_next
        du1buf[pl.ds(0, tm), :] = du1_ref[...].astype(F32)
        du1buf[pl.ds(tm, HALO), :] = ndu1_ref[...].astype(F32) * keep_next
        dp_ref[:, pl.ds(0, d)] = dba_ref[...]
        dp_ref[:, pl.ds(5 * d, d)] = dga_ref[...]
        dp_ref[:, pl.ds(6 * d, d)] = dgb_ref[...]

        def chunk(j, carry):
            r0 = pl.multiple_of(j * CONV_ROWS, CONV_ROWS)
            rows = pl.ds(r0, CONV_ROWS)
            for lc in range(d // CONV_LANES):
                lo = lc * CONV_LANES
                ls = pl.ds(lo, CONV_LANES)
                cac = ca_ref[rows, ls].astype(F32)
                vac = va_ref[rows, ls].astype(F32)
                zc = cac * vac
                acc = jnp.zeros((CONV_ROWS, CONV_LANES), F32)
                for k, sl in _taps(dcabuf[pl.ds(r0, CONV_ROWS + HALO), ls], ANTI_A, CONV_ROWS):
                    acc = acc + cwa_ref[pl.ds(k, 1), ls] * sl
                    acca[pl.ds(8 * k, 8), ls] += _rowsum8(sl * zc)
                dp_ref[rows, pl.ds(d + lo, CONV_LANES)] = (acc * vac).astype(BF)
                dp_ref[rows, pl.ds(2 * d + lo, CONV_LANES)] = (acc * cac).astype(BF)
                sgc = sgbuf[rows, ls]
                vbc = vb_ref[rows, ls].astype(F32)
                uc = vbc * sgc
                acc = jnp.zeros((CONV_ROWS, CONV_LANES), F32)
                for k, sl in _taps(du1buf[pl.ds(r0, CONV_ROWS + HALO), ls], ANTI_B, CONV_ROWS):
                    acc = acc + cwb_ref[pl.ds(k, 1), ls] * sl
                    accb[pl.ds(8 * k, 8), ls] += _rowsum8(sl * uc)
                dp_ref[rows, pl.ds(3 * d + lo, CONV_LANES)] = (acc * sgc).astype(BF)
                dp_ref[rows, pl.ds(4 * d + lo, CONV_LANES)] = (acc * vbc * (sgc * (1.0 - sgc))).astype(BF)
            return carry

        lax.fori_loop(0, tm // CONV_ROWS, chunk, 0)

        @pl.when(i == nt - 1)
        def _():
            dcwa_ref[...] = jnp.zeros_like(dcwa_ref)
            dcwb_ref[...] = jnp.zeros_like(dcwb_ref)
            for k in range(K_A):
                dcwa_ref[pl.ds(k, 1), :] = jnp.sum(acca[pl.ds(8 * k, 8), :], axis=0, keepdims=True)
            for k in range(K_B):
                dcwb_ref[pl.ds(k, 1), :] = jnp.sum(accb[pl.ds(8 * k, 8), :], axis=0, keepdims=True)

    row = pl.BlockSpec((tm, d), lambda i: (i, 0))
    nxt = _next_halo_spec(tm, d, 0, s)
    return _host_call(
        body, name="conv_bwd", grid=(nt,),
        in_specs=[row, row, nxt, nxt] + [_col_spec(tm, d, k) for k in (1, 2, 3, 4)]
        + [row, row, row, _const((K_A, d)), _const((K_B, d))],
        out_specs=[pl.BlockSpec((tm, 7 * d), lambda i: (i, 0)), pl.BlockSpec((8, d), lambda i: (0, 0)),
                   pl.BlockSpec((32, d), lambda i: (0, 0))],
        out_shape=[SDS((s, 7 * d), BF), SDS((8, d), F32), SDS((32, d), F32)],
        scratch_shapes=[pltpu.VMEM((tm + HALO, d), F32), pltpu.VMEM((tm + HALO, d), F32),
                        pltpu.VMEM((tm, d), F32), pltpu.VMEM((8 * K_A, d), F32), pltpu.VMEM((8 * K_B, d), F32)],
        args=(dca, du1, dca, du1, p, p, p, p, dba, dga, dgb, cwa, cwb), hosted=hosted)


def _grad_w(a, b, name, hosted=None, tokens=None):
    s, m = a.shape
    n = b.shape[1]
    tm, tn, tk = _tile(m, GRAD_W_TILE), _tile(n, GRAD_W_TILE), _tile(s, tokens or GRAD_W_TOKENS)
    nk = s // tk

    def body(a_ref, b_ref, o_ref, ob_ref):
        k = pl.program_id(2)

        @pl.when(k == 0)
        def _():
            o_ref[...] = jnp.zeros_like(o_ref)

        o_ref[...] += _dot_tn(a_ref[...], b_ref[...])

        @pl.when(k == nk - 1)
        def _():
            ob_ref[...] = o_ref[...].astype(BF)

    blk = pl.BlockSpec((tm, tn), lambda i, j, k: (i, j))
    (g, gb), extra = _host_call(
        body, name=name, grid=(m // tm, n // tn, nk),
        in_specs=[pl.BlockSpec((tk, tm), lambda i, j, k: (k, i)), pl.BlockSpec((tk, tn), lambda i, j, k: (k, j))],
        out_specs=[blk, blk], out_shape=[SDS((m, n), F32), SDS((m, n), BF)], scratch_shapes=[], args=(a, b),
        hosted=hosted)
    return g, gb, extra


def _grad_x(dr1, dp, wi, hosted=None):
    s, d = dr1.shape
    n = wi.shape[1]
    tm = min(s, GRAD_X_TILE)

    def body(dr1_ref, dp_ref, w_ref, o_ref):
        o_ref[...] = F32(ALPHA) * dr1_ref[...] + _dot_nt(dp_ref[...], w_ref[...])

    (gx,), extra = _host_call(
        body, name="grad_x", grid=(s // tm,),
        in_specs=[pl.BlockSpec((tm, d), lambda i: (i, 0)), pl.BlockSpec((tm, n), lambda i: (i, 0)), _const((d, n))],
        out_specs=[pl.BlockSpec((tm, d), lambda i: (i, 0))],
        out_shape=[SDS((s, d), F32)], scratch_shapes=[], args=(dr1, dp, wi), hosted=hosted)
    return gx, extra


def _adamw_math(w, g, m, v):
    m2 = ADAM_B1 * m + (1.0 - ADAM_B1) * g
    v2 = ADAM_B2 * v + (1.0 - ADAM_B2) * (g * g)
    m_hat = m2 / (1.0 - ADAM_B1 ** ADAM_STEP)
    v_hat = v2 / (1.0 - ADAM_B2 ** ADAM_STEP)
    delta = -ADAM_LR * (m_hat / (jnp.sqrt(v_hat) + ADAM_EPS) + ADAM_WD * w)
    return delta, m2, v2


def _adamw(w, g, m, v, name):
    r, c = w.shape
    tr = min(r, ROWS_TILE)

    def body(w_ref, g_ref, m_ref, v_ref, g_out, d_ref, m2_ref, v2_ref):
        gv = g_ref[...]
        delta, m2, v2 = _adamw_math(w_ref[...], gv, m_ref[...], v_ref[...])
        g_out[...] = gv
        d_ref[...] = delta
        m2_ref[...] = m2
        v2_ref[...] = v2

    blk = pl.BlockSpec((tr, c), lambda i: (i, 0))
    return pl.pallas_call(
        body, name=name, grid=(r // tr,), in_specs=[blk] * 4, out_specs=[blk] * 4,
        out_shape=[SDS((r, c), F32)] * 4, compiler_params=_cp(("parallel",)),
    )(w, g, m, v)


def _sum_parts(parts, name):
    k, r, c = parts.shape

    def body(p_ref, o_ref):
        acc = p_ref[0]
        for q in range(1, k):
            acc = acc + p_ref[q]
        o_ref[...] = acc

    return pl.pallas_call(
        body, name=name, grid=(1,),
        in_specs=[pl.BlockSpec((k, r, c), lambda i: (0, 0, 0))],
        out_specs=pl.BlockSpec((r, c), lambda i: (0, 0)),
        out_shape=SDS((r, c), F32), compiler_params=_cp(("arbitrary",)),
    )(parts)


def _piece_shape(full_shape, axis):
    r, c = full_shape
    return (r // 2, c // 4) if axis == 1 else (r // 8, c)


def _piece_spec(full_shape, axis, tr, chip_of, half_of):
    hr, wc = _piece_shape(full_shape, axis)
    nb = hr // tr
    if axis == 1:
        return pl.BlockSpec((tr, wc), lambda *a: (half_of(*a) * nb + a[-2], chip_of(*a)))
    return pl.BlockSpec((tr, wc), lambda *a: ((2 * chip_of(*a) + half_of(*a)) * nb + a[-2], 0))


def _place_cast(w, axis, pos, name):
    r, c = w.shape
    tr = min(r, ROWS_TILE)
    nb = r // tr
    full = (r, 4 * c) if axis == 1 else (4 * r, c)
    out_map = (lambda i, pos: (i, pos[0])) if axis == 1 else (lambda i, pos: (pos[0] * nb + i, 0))

    def body(pos_ref, w_ref, o_ref):
        o_ref[...] = w_ref[...].astype(o_ref.dtype)

    gs = pltpu.PrefetchScalarGridSpec(
        num_scalar_prefetch=1, grid=(nb,),
        in_specs=[pl.BlockSpec((tr, c), lambda i, pos: (i, 0))], out_specs=pl.BlockSpec((tr, c), out_map))
    return pl.pallas_call(body, name=name, grid_spec=gs, out_shape=SDS(full, BF),
                          compiler_params=_cp(("arbitrary",)))(pos, w)


def _pair_add(g, land, axis, pos, name):
    hr, wc = _piece_shape(g.shape, axis)
    tr = min(hr, SUM_ROWS)

    def body(pos_ref, g_ref, l_ref, o_ref):
        o_ref[0] = (g_ref[...] + l_ref[0].astype(F32)).astype(BF)

    other = lambda q, i, pos: (pos[0] + 1 + q) % 4
    blk = pl.BlockSpec((1, tr, wc), lambda q, i, pos: (other(q, i, pos), i, 0))
    gs = pltpu.PrefetchScalarGridSpec(
        num_scalar_prefetch=1, grid=(3, hr // tr),
        in_specs=[_piece_spec(g.shape, axis, tr, other, lambda q, i, pos: pos[1]), blk], out_specs=blk)
    return pl.pallas_call(body, name=name, grid_spec=gs, out_shape=SDS((4, hr, wc), BF),
                          compiler_params=_cp(("arbitrary", "arbitrary")))(pos, g, land)


def _chip_sum(g, land1, land2, axis, pos, name):
    hr, wc = _piece_shape(g.shape, axis)
    tr = min(hr, SUM_ROWS)
    nb = hr // tr

    def body(pos_ref, g_ref, l1_ref, l2_ref, o_ref):
        acc = g_ref[...] + l1_ref[0].astype(F32)
        for q in range(3):
            acc = acc + l2_ref[q].astype(F32)
        o_ref[...] = acc

    gs = pltpu.PrefetchScalarGridSpec(
        num_scalar_prefetch=1, grid=(nb,),
        in_specs=[_piece_spec(g.shape, axis, tr, lambda i, pos: pos[0], lambda i, pos: pos[1]),
                  pl.BlockSpec((1, tr, wc), lambda i, pos: (pos[0], i, 0)),
                  pl.BlockSpec((3, tr, wc), lambda i, pos: (0, i, 0))],
        out_specs=pl.BlockSpec((tr, wc), lambda i, pos: (pos[1] * nb + i, 0)))
    return pl.pallas_call(body, name=name, grid_spec=gs, out_shape=SDS((2 * hr, wc), F32),
                          compiler_params=_cp(("arbitrary",)))(pos, g, land1, land2)


ANY = pl.BlockSpec(memory_space=pl.ANY)
COMM = pltpu.CompilerParams(has_side_effects=True)


def _on_each_device(fn):
    x, y, c = lax.axis_index("x"), lax.axis_index("y"), lax.axis_index("c")
    for sx in (0, 1):
        for sy in (0, 1):
            for sc in (0, 1):
                @pl.when(jnp.logical_and(jnp.logical_and(x == sx, y == sy), c == sc))
                def _(sx=sx, sy=sy, sc=sc):
                    fn(sx, sy, sc)


def _remote(src, dst, send_sem, recv_sem, to):
    return pltpu.make_async_remote_copy(src_ref=src, dst_ref=dst, send_sem=send_sem, recv_sem=recv_sem,
                                        device_id=to, device_id_type=MESH)


def _piece_ref(ref, axis, j, h):
    r, c = ref.shape
    hr, wc = _piece_shape((r, c), axis)
    if axis == 1:
        return ref.at[pl.ds(h * hr, hr), pl.ds(j * wc, wc)]
    return ref.at[pl.ds((2 * j + h) * hr, hr), :]


class _Stage:
    def __init__(self, ins, outs, n_sems, make_plan, aliases=None):
        self.ins, self.outs, self.n_sems, self.make_plan = list(ins), list(outs), n_sems, make_plan
        self.aliases = dict(aliases or {})


def _start(plan):
    def dev(sx, sy, sc):
        for cp, _, _ in plan(sx, sy, sc):
            cp.start()

    _on_each_device(dev)


def _finish(plan):
    def dev(sx, sy, sc):
        for _, sent, got in plan(sx, sy, sc):
            sent.wait_send()
            got.wait_recv()

    _on_each_device(dev)


def _comm_call(stage, name):
    n_in, n_out = len(stage.ins), len(stage.outs)

    def body(*refs):
        plan = stage.make_plan(refs[:n_in], refs[n_in:n_in + n_out], *refs[n_in + n_out:])
        _start(plan)
        _finish(plan)

    return pl.pallas_call(
        body, name=name, in_specs=[ANY] * n_in, out_specs=[ANY] * n_out, out_shape=stage.outs,
        input_output_aliases=stage.aliases,
        scratch_shapes=[pltpu.SemaphoreType.DMA((stage.n_sems,)), pltpu.SemaphoreType.DMA((stage.n_sems,))],
        compiler_params=COMM,
    )(*stage.ins)


class _SemsFrom:
    def __init__(self, sems, base):
        self.sems, self.base = sems, base

    @property
    def at(self):
        return self

    def __getitem__(self, k):
        return self.sems.at[self.base + k]


def _both(a, b):
    na, nb = len(a.ins), len(b.ins)
    ma = len(a.outs)

    def make_plan(ins, outs, send_sems, recv_sems):
        pa = a.make_plan(ins[:na], outs[:ma], send_sems, recv_sems)
        pb = b.make_plan(ins[na:], outs[ma:], _SemsFrom(send_sems, a.n_sems), _SemsFrom(recv_sems, a.n_sems))
        return lambda sx, sy, sc: pa(sx, sy, sc) + pb(sx, sy, sc)

    aliases = dict(a.aliases)
    aliases.update({na + i: ma + o for i, o in b.aliases.items()})
    return _Stage(a.ins + b.ins, a.outs + b.outs, a.n_sems + b.n_sems, make_plan, aliases)


def _same(cp):
    return (cp, cp, cp)


def _stage_gather_send(fulls, axes):
    n = len(fulls)

    def make_plan(ins, outs, send_sems, recv_sems):
        def plan(sx, sy, sc):
            cps = []
            for w in range(n):
                mine = _piece_ref(outs[w], axes[w], 2 * sx + sy, sc)
                for r, (fx, fy) in enumerate(CHIP_RELS):
                    k = 3 * w + r
                    to = (sx ^ fx, sy ^ fy, sc)
                    got = _piece_ref(outs[w], axes[w], 2 * (sx ^ fx) + (sy ^ fy), sc)
                    send = _remote(mine, mine, send_sems.at[k], recv_sems.at[k], to)
                    cps.append((send, send, _remote(got, got, send_sems.at[k], recv_sems.at[k], to)))
            return cps

        return plan

    return _Stage(fulls, [SDS(f.shape, f.dtype) for f in fulls], 3 * n, make_plan, {i: i for i in range(n)})


def _stage_gather_forward(fulls, axes):
    n = len(fulls)

    def make_plan(ins, outs, send_sems, recv_sems):
        def plan(sx, sy, sc):
            cps = []
            sib = (sx, sy, 1 - sc)
            for w in range(n):
                for r, (fx, fy) in enumerate(CHIP_RELS):
                    k = 3 * w + r
                    pj = 2 * (sx ^ fx) + (sy ^ fy)
                    have = _piece_ref(outs[w], axes[w], pj, sc)
                    want = _piece_ref(outs[w], axes[w], pj, 1 - sc)
                    send = _remote(have, have, send_sems.at[k], recv_sems.at[k], sib)
                    cps.append((send, send, _remote(want, want, send_sems.at[k], recv_sems.at[k], sib)))
            return cps

        return plan

    return _Stage(fulls, [SDS(f.shape, f.dtype) for f in fulls], 3 * n, make_plan, {i: i for i in range(n)})


def _in_place(fulls, n_sems, make_plan):
    return _Stage(fulls, [SDS(f.shape, f.dtype) for f in fulls], n_sems, make_plan, {i: i for i in range(len(fulls))})


def _stage_gather_neighbours(fulls, axes):
    n = len(fulls)

    def make_plan(ins, outs, send_sems, recv_sems):
        def plan(sx, sy, sc):
            cps = []
            for w in range(n):
                mine = _piece_ref(outs[w], axes[w], 2 * sx + sy, sc)
                for r, (px, py) in enumerate(((sx ^ 1, sy), (sx, sy ^ 1))):
                    got = _piece_ref(outs[w], axes[w], 2 * px + py, sc)
                    send = _remote(mine, mine, send_sems.at[2 * w + r], recv_sems.at[2 * w + r], (px, py, sc))
                    cps.append((send, send, _remote(got, got, send_sems.at[2 * w + r], recv_sems.at[2 * w + r],
                                                    (px, py, sc))))
            return cps

        return plan

    return _in_place(fulls, 2 * n, make_plan)


def _stage_gather_pair(fulls, axes):
    n = len(fulls)

    def make_plan(ins, outs, send_sems, recv_sems):
        def plan(sx, sy, sc):
            cps = []
            sib = (sx, sy, 1 - sc)
            for w in range(n):
                for r, j in enumerate((2 * (sx ^ 1) + sy, 2 * sx + (sy ^ 1))):
                    k = 2 * w + r
                    have, want = _piece_ref(outs[w], axes[w], j, sc), _piece_ref(outs[w], axes[w], j, 1 - sc)
                    send = _remote(have, have, send_sems.at[k], recv_sems.at[k], sib)
                    cps.append((send, send, _remote(want, want, send_sems.at[k], recv_sems.at[k], sib)))
            return cps

        return plan

    return _in_place(fulls, 2 * n, make_plan)


def _stage_gather_diagonal(fulls, axes):
    n = len(fulls)

    def make_plan(ins, outs, send_sems, recv_sems):
        def relay(sx, sy, sc):
            cps = []
            jx, jy, jd = 2 * (sx ^ 1) + sy, 2 * sx + (sy ^ 1), 2 * (sx ^ 1) + (sy ^ 1)
            passed, to = (jx, (sx, sy ^ 1, sc)) if sc == 0 else (jy, (sx ^ 1, sy, sc))
            for w in range(n):
                have, want = _piece_ref(outs[w], axes[w], passed, sc), _piece_ref(outs[w], axes[w], jd, sc)
                send = _remote(have, have, send_sems.at[w], recv_sems.at[w], to)
                cps.append((send, send, _remote(want, want, send_sems.at[w], recv_sems.at[w], to)))
            return cps

        def cross(sx, sy, sc):
            cps = []
            sib = (sx, sy, 1 - sc)
            jd = 2 * (sx ^ 1) + (sy ^ 1)
            for w in range(n):
                have, want = _piece_ref(outs[w], axes[w], jd, sc), _piece_ref(outs[w], axes[w], jd, 1 - sc)
                send = _remote(have, have, send_sems.at[n + w], recv_sems.at[n + w], sib)
                cps.append((send, send, _remote(want, want, send_sems.at[n + w], recv_sems.at[n + w], sib)))
            return cps

        return [relay, cross]

    return _in_place(fulls, 2 * n, make_plan)


def _stage_pair_exchange(grads, axes):
    n = len(grads)

    def make_plan(gs, land, send_sems, recv_sems):
        def plan(sx, sy, sc):
            return [_same(_remote(_piece_ref(gs[w], axes[w], jj, 1 - sc), land[w].at[jj], send_sems.at[4 * w + jj],
                                  recv_sems.at[4 * w + jj], (sx, sy, 1 - sc)))
                    for w in range(n) for jj in range(4)]

        return plan

    return _Stage(grads, [SDS((4,) + _piece_shape(g.shape, a), g.dtype) for g, a in zip(grads, axes)], 4 * n,
                  make_plan)


def _stage_chip_scatter(pieces):
    n = len(pieces)

    def make_plan(ps, land, send_sems, recv_sems):
        def plan(sx, sy, sc):
            return [_same(_remote(ps[w].at[2 * (sx ^ fx) + (sy ^ fy)], land[w].at[r], send_sems.at[3 * w + r],
                                  recv_sems.at[3 * w + r], (sx ^ fx, sy ^ fy, sc)))
                    for w in range(n) for r, (fx, fy) in enumerate(CHIP_RELS)]

        return plan

    return _Stage(pieces, [SDS((3,) + p.shape[1:], p.dtype) for p in pieces], 3 * n, make_plan)


def _stage_pair_share(shards):
    n = len(shards)

    def make_plan(ins, outs, send_sems, recv_sems):
        def plan(sx, sy, sc):
            cps = []
            sib = (sx, sy, 1 - sc)
            for w in range(n):
                hr = shards[w].shape[0] // 2
                mine = outs[w].at[pl.ds(sc * hr, hr), :]
                theirs = outs[w].at[pl.ds((1 - sc) * hr, hr), :]
                send = _remote(mine, mine, send_sems.at[w], recv_sems.at[w], sib)
                cps.append((send, send, _remote(theirs, theirs, send_sems.at[w], recv_sems.at[w], sib)))
            return cps

        return plan

    return _Stage(shards, [SDS(g.shape, g.dtype) for g in shards], n, make_plan, {i: i for i in range(n)})


def _stage_gather_small(stack):
    def make_plan(ins, outs, send_sems, recv_sems):
        def plan(sx, sy, sc):
            mine = outs[0].at[4 * sx + 2 * sy + sc]
            return [_same(_remote(mine, mine, send_sems.at[k], recv_sems.at[k], (sx ^ fx, sy ^ fy, sc ^ fc)))
                    for k, (fx, fy, fc) in enumerate(DEV_RELS)]

        return plan

    return _Stage([stack], [SDS(stack.shape, stack.dtype)], 7, make_plan, {0: 0})


def kernel(x, w_in, conv_a_w, w_out_a, conv_b_w, conv_b_bias, ln_b_gamma, ln_b_beta, w_out_b, w_o, ln1_gamma, ln1_beta, w_up, w_down, ln2_gamma, ln2_beta, loss_target, m_w_in, m_conv_a_w, m_w_out_a, m_conv_b_w, m_conv_b_bias, m_ln_b_gamma, m_ln_b_beta, m_w_out_b, m_w_o, m_ln1_gamma, m_ln1_beta, m_w_up, m_w_down, m_ln2_gamma, m_ln2_beta, v_w_in, v_conv_a_w, v_w_out_a, v_conv_b_w, v_conv_b_bias, v_ln_b_gamma, v_ln_b_beta, v_w_out_b, v_w_o, v_ln1_gamma, v_ln1_beta, v_w_up, v_w_down, v_ln2_gamma, v_ln2_beta):
    s, d = x.shape[1], x.shape[2]
    xs = x.reshape(s, d)
    tgt = loss_target.reshape(s, d)
    dq = d // 4
    chip = 2 * lax.axis_index("x") + lax.axis_index("y")
    core = lax.axis_index("c")
    pos = jnp.stack([chip, core]).astype(jnp.int32)
    names = ("w_in", "w_out_a", "w_out_b", "w_o", "w_up", "w_down")
    axes = (1, 0, 0, 0, 1, 0)

    conv_pack = jnp.concatenate([jnp.pad(conv_a_w, ((0, 8 - K_A), (0, 0))), jnp.pad(conv_b_w, ((0, 32 - K_B), (0, 0))),
                                 jnp.zeros((8, dq), F32)], axis=0)
    conv_full = lax.dynamic_update_slice(jnp.zeros((conv_pack.shape[0], d), F32), conv_pack, (0, chip * dq))
    wi_own = _place_cast(w_in, 1, pos, "place_w_in")
    vec = lambda a: a.reshape(1, d)
    bias_b, lbg, lbb = vec(conv_b_bias), vec(ln_b_gamma), vec(ln_b_beta)
    l1g, l1b, l2g, l2b = vec(ln1_gamma), vec(ln1_beta), vec(ln2_gamma), vec(ln2_beta)

    (p, xb, *placed), landed = _in_proj_own(xs, w_in, (w_out_a, w_out_b, w_o, w_up, w_down), axes[1:], pos,
                                            _stage_gather_neighbours([wi_own, conv_full], (1, 1)))
    fulls = [wi_own] + placed
    wi, convs = _comm_call(_stage_gather_pair(landed, (1, 1)), "gather_pair_w_in")
    p, (wi, convs), small3 = _in_proj_rest(xb, wi, p, pos, _stage_gather_diagonal([wi, convs], (1, 1)),
                                           _stage_gather_send(fulls[1:4], axes[1:4]))
    cwa, cwb = convs[0:K_A], convs[8:8 + K_B]
    (conva, yapre, nb, rstdb, u3), landed = _conv_fwd(
        p, cwa, cwb, bias_b, lbg, lbb, d,
        _both(_stage_gather_forward(small3, axes[1:4]), _stage_gather_send(fulls[4:6], axes[4:6])))
    woa, wob, wo = landed[:3]
    (ya, yb, merged, n1, rstd1), (wup, wdown) = _mixer_out(yapre, u3, p, xs, woa, wob, wo, d,
                                                           _stage_gather_forward(landed[3:], axes[4:6]))
    x1b, hb, dhpre, dr2b, dr1, dr1b, acc_mlp = _mlp(n1, rstd1, tgt, wup, wdown, l1g, l1b, l2g, l2b)
    g_up, gb_up, _ = _grad_w(x1b, dhpre, "grad_w_up", tokens=GRAD_W_TOKENS_MLP)
    g_down, gb_down, _ = _grad_w(hb, dr2b, "grad_w_down", tokens=GRAD_W_TOKENS_MLP)
    dya, dyb, dba, dga, dgb, dca, du1, acc_mix = _mixer_bwd_local(dr1b, p, ya, yb, conva, nb, rstdb, wo, woa, wob,
                                                                   lbg, lbb)
    g_oa, gb_oa, _ = _grad_w(yapre, dya, "grad_w_out_a")
    g_ob, gb_ob, _ = _grad_w(u3, dyb, "grad_w_out_b")
    g_o, gb_o, _ = _grad_w(merged, dr1b, "grad_w_o")

    early, e_axes, e_names = [g_oa, g_ob, g_o, g_up, g_down], axes[1:], names[1:]
    (dp, dcwa, dcwb), land1 = _conv_bwd(dca, du1, p, dba, dga, dgb, cwa, cwb, d,
                                        _stage_pair_exchange([gb_oa, gb_ob, gb_o, gb_up, gb_down], e_axes))
    pieces = [_pair_add(g, l, a, pos, "pair_add_" + nm) for g, l, a, nm in zip(early, land1, e_axes, e_names)]
    pack = jnp.concatenate([dcwa, dcwb, acc_mix, acc_mlp], axis=0)
    stack = lax.dynamic_update_slice(jnp.zeros((8,) + pack.shape, F32), pack[None], (2 * chip + core, 0, 0))
    g_wi, gb_wi, landed = _grad_w(xb, dp, "grad_w_in",
                                  _both(_stage_chip_scatter(pieces), _stage_gather_small(stack)))
    land2, stack = landed[:-1], landed[-1]
    halves = [_chip_sum(g, l1, l2, a, pos, "chip_sum_" + nm)
              for g, l1, l2, a, nm in zip(early, land1, land2, e_axes, e_names)]
    (land1_in,) = _comm_call(_stage_pair_exchange([gb_wi], (1,)), "pair_exchange_w_in")
    piece_in = _pair_add(g_wi, land1_in, 1, pos, "pair_add_w_in")
    grad_x, landed = _grad_x(dr1, dp, wi, _both(_stage_chip_scatter([piece_in]), _stage_pair_share(halves)))
    land2_in, (g_oa, g_ob, g_o, g_up, g_down) = landed[0], landed[1:]
    half_in = _chip_sum(g_wi, land1_in, land2_in, 1, pos, "chip_sum_w_in")
    (g_in,) = _comm_call(_stage_pair_share([half_in]), "pair_share_w_in")
    small = _sum_parts(stack, "small_sum")
    g_ca = lax.dynamic_slice(small, (0, chip * dq), (K_A, dq))
    g_cb = lax.dynamic_slice(small, (8, chip * dq), (K_B, dq))
    g_vec = jnp.stack([small[r] for r in (42, 40, 41, 51, 52, 49, 50)])

    loss = (0.5 / d) * jnp.sum(small[48])

    big = {}
    for name, w, g, m, v in (("w_in", w_in, g_in, m_w_in, v_w_in), ("w_out_a", w_out_a, g_oa, m_w_out_a, v_w_out_a),
                             ("w_out_b", w_out_b, g_ob, m_w_out_b, v_w_out_b), ("w_o", w_o, g_o, m_w_o, v_w_o),
                             ("w_up", w_up, g_up, m_w_up, v_w_up), ("w_down", w_down, g_down, m_w_down, v_w_down),
                             ("conv_a_w", conv_a_w, g_ca, m_conv_a_w, v_conv_a_w),
                             ("conv_b_w", conv_b_w, g_cb, m_conv_b_w, v_conv_b_w)):
        big[name] = tuple(_adamw(w, g, m, v, "adamw_" + name))
    vec_names = ("conv_b_bias", "ln_b_gamma", "ln_b_beta", "ln1_gamma", "ln1_beta", "ln2_gamma", "ln2_beta")
    w7 = jnp.stack([conv_b_bias, ln_b_gamma, ln_b_beta, ln1_gamma, ln1_beta, ln2_gamma, ln2_beta])
    m7 = jnp.stack([m_conv_b_bias, m_ln_b_gamma, m_ln_b_beta, m_ln1_gamma, m_ln1_beta, m_ln2_gamma, m_ln2_beta])
    v7 = jnp.stack([v_conv_b_bias, v_ln_b_gamma, v_ln_b_beta, v_ln1_gamma, v_ln1_beta, v_ln2_gamma, v_ln2_beta])
    g7, d7, nm7, nv7 = _adamw(w7, g_vec, m7, v7, "adamw_vectors")
    for q, name in enumerate(vec_names):
        big[name] = (g7[q], d7[q], nm7[q], nv7[q])

    order = ("w_in", "conv_a_w", "w_out_a", "conv_b_w", "conv_b_bias", "ln_b_gamma", "ln_b_beta", "w_out_b", "w_o",
             "ln1_gamma", "ln1_beta", "w_up", "w_down", "ln2_gamma", "ln2_beta")
    outs = [loss, grad_x.reshape(x.shape)]
    for part in range(4):
        outs += [big[name][part] for name in order]
    return tuple(outs)
```

```python
import jax
import jax.numpy as jnp
from jax import lax
from jax.experimental import pallas as pl
from jax.experimental.pallas import tpu as pltpu

F32 = jnp.float32
BF = jnp.bfloat16
SDS = jax.ShapeDtypeStruct
MESH = pl.DeviceIdType.MESH

ALPHA = 2.0 ** 0.25
LN_EPS = 1e-5
K_A = 3
K_B = 31
HALO = 32
CONV_TILE = 512
CONV_ROWS = 64
CONV_LANES = 128
VMEM_LIMIT_MB = 56
PROJ_TILE = 1024
PROJ_REST_TILE = 2048
MIXER_TILE = 512
GRAD_X_TILE = 512
LOCAL_TILE = 256
MIXER_BWD_TILE = 512
FF_CHUNK = 1024
GRAD_W_TILE = 1024
GRAD_W_TOKENS = 2048
GRAD_W_TOKENS_MLP = 4096
SUM_ROWS = 512
ROWS_TILE = 256
ADAM_LR = 0.001
ADAM_B1 = 0.9
ADAM_B2 = 0.999
ADAM_EPS = 1e-08
ADAM_WD = 0.01
ADAM_STEP = 10
P_DT = BF
CHIP_RELS = ((1, 0), (0, 1), (1, 1))
DEV_RELS = tuple((fx, fy, fc) for fx in (0, 1) for fy in (0, 1) for fc in (0, 1))[1:]


def _cp(sem=None, side_effects=False):
    return pltpu.CompilerParams(dimension_semantics=sem, vmem_limit_bytes=VMEM_LIMIT_MB << 20,
                                has_side_effects=side_effects)


def _const(shape):
    return pl.BlockSpec(shape, lambda *_: (0,) * len(shape), pipeline_mode=pl.Buffered(1))


def _sig(v):
    return jax.nn.sigmoid(v)


def _ln_fwd(r):
    mu = jnp.mean(r, axis=-1, keepdims=True)
    xc = r - mu
    var = jnp.mean(xc * xc, axis=-1, keepdims=True)
    rstd = lax.rsqrt(var + LN_EPS)
    return xc * rstd, rstd


def _ln_bwd(dn, n, rstd):
    m1 = jnp.mean(dn, axis=-1, keepdims=True)
    m2 = jnp.mean(dn * n, axis=-1, keepdims=True)
    return rstd * (dn - m1 - n * m2)


def _dot(a, b):
    return jnp.dot(a, b, preferred_element_type=F32)


def _dot_nt(a, b):
    return lax.dot_general(a, b, (((1,), (1,)), ((), ())), preferred_element_type=F32)


def _dot_tn(a, b):
    return lax.dot_general(a, b, (((0,), (0,)), ((), ())), preferred_element_type=F32)


def _tile(n, pref):
    if n <= pref:
        return n
    return max(t for t in range(128, pref + 1, 128) if n % t == 0)


def _rowsum8(v):
    acc = v[0:8]
    for g in range(1, v.shape[0] // 8):
        acc = acc + v[8 * g:8 * g + 8]
    return acc


def _taps(win, offsets, rows):
    r_all = win.shape[0]
    by_res = {}
    for k, o in enumerate(offsets):
        by_res.setdefault(o % 8, []).append((k, o // 8))
    for s, taps in sorted(by_res.items()):
        r = win if s == 0 else pltpu.roll(win, r_all - s, 0)
        for k, q in taps:
            yield k, r[8 * q:8 * q + rows]


CAUSAL_A = [HALO - (K_A - 1) + k for k in range(K_A)]
CAUSAL_B = [HALO - (K_B - 1) + k for k in range(K_B)]
ANTI_A = [K_A - 1 - k for k in range(K_A)]
ANTI_B = [K_B - 1 - k for k in range(K_B)]


def _host_call(body, *, name, grid, in_specs, out_specs, out_shape, scratch_shapes, args, hosted, prefetch=None,
               aliases=None, body_gets_stage_refs=False):
    n_in, n_out, n_scr = len(in_specs), len(out_specs), len(scratch_shapes)
    n_steps = 1
    for size in grid:
        n_steps *= size
    if isinstance(hosted, _Stage):
        hosted = [(hosted, 0, n_steps - 1)]
    n_pre = 0 if prefetch is None else 1
    pre = () if prefetch is None else (prefetch,)
    sem = ("arbitrary",) * len(grid)
    own_aliases = {n_pre + a: b for a, b in (aliases or {}).items()}

    def call(kernel_body, ins, outs, shapes, scratch, all_aliases, side_effects, operands):
        gs = pltpu.PrefetchScalarGridSpec(num_scalar_prefetch=n_pre, grid=grid, in_specs=ins, out_specs=outs,
                                          scratch_shapes=scratch)
        return pl.pallas_call(kernel_body, name=name, grid_spec=gs, out_shape=shapes,
                              input_output_aliases=all_aliases,
                              compiler_params=_cp(sem, side_effects=side_effects))(*pre, *operands)

    if hosted is None:
        res = call(body, list(in_specs), list(out_specs), list(out_shape), list(scratch_shapes), own_aliases, False,
                   args)
        return list(res), []
    stages = [st for st, _, _ in hosted]
    h_ins = [a for st in stages for a in st.ins]
    h_outs = [o for st in stages for o in st.outs]
    borrowed = {hi: k for hi, a in enumerate(h_ins) for k, own in enumerate(args) if a is own}
    passed = [hi for hi in range(len(h_ins)) if hi not in borrowed]
    h_in, h_out, n_sems = len(passed), len(h_outs), sum(st.n_sems for st in stages)

    def full_body(*refs):
        pre_refs, refs = refs[:n_pre], refs[n_pre:]
        ins, refs = refs[:n_in], refs[n_in:]
        hins = [None] * len(h_ins)
        for hi, ref in zip(passed, refs[:h_in]):
            hins[hi] = ref
        refs = refs[h_in:]
        outs, refs = refs[:n_out], refs[n_out:]
        houts, refs = refs[:h_out], refs[h_out:]
        scr, (send_sems, recv_sems) = refs[:n_scr], refs[n_scr:]
        step = 0
        for a, size in enumerate(grid):
            step = step * size + pl.program_id(a)
        phases, i0, o0, k0 = [], 0, 0, 0
        for st, starts, finishes in hosted:
            plans = st.make_plan(hins[i0:i0 + len(st.ins)], houts[o0:o0 + len(st.outs)],
                                 _SemsFrom(send_sems, k0), _SemsFrom(recv_sems, k0))
            if not isinstance(plans, list):
                plans, starts, finishes = [plans], [starts], [finishes]
            phases += list(zip(plans, starts, finishes))
            i0, o0, k0 = i0 + len(st.ins), o0 + len(st.outs), k0 + st.n_sems
        for plan, at, _ in phases:
            @pl.when(step == at)
            def _(plan=plan):
                _start(plan)

        if body_gets_stage_refs:
            body(*pre_refs, *ins, *outs, *scr, houts)
        else:
            body(*pre_refs, *ins, *outs, *scr)

        for plan, _, at in phases:
            @pl.when(step == at)
            def _(plan=plan):
                _finish(plan)

    all_aliases = dict(own_aliases)
    i0 = o0 = 0
    for st in stages:
        for a, b in st.aliases.items():
            hi = i0 + a
            operand = borrowed[hi] if hi in borrowed else n_in + passed.index(hi)
            all_aliases[n_pre + operand] = n_out + o0 + b
        i0, o0 = i0 + len(st.ins), o0 + len(st.outs)
    res = call(full_body, list(in_specs) + [ANY] * h_in, list(out_specs) + [ANY] * h_out,
               list(out_shape) + h_outs,
               list(scratch_shapes) + [pltpu.SemaphoreType.DMA((n_sems,)), pltpu.SemaphoreType.DMA((n_sems,))],
               all_aliases, True, (*args, *[h_ins[hi] for hi in passed]))
    return list(res[:n_out]), list(res[n_out:])


def _in_proj_own(x, w_shard, others, other_axes, pos, near):
    s, d = x.shape
    tn = w_shard.shape[1]
    tm = min(s // 4, PROJ_TILE)
    n_o = len(others)
    last = s // tm - 1
    hosted = [(near, [0, last], [last - 1, last])]

    def body(pos_ref, x_ref, w_ref, *refs):
        o_in, (p_ref, xb_ref), o_out, wb = refs[:n_o], refs[n_o:n_o + 2], refs[n_o + 2:2 * n_o + 2], refs[-1]

        @pl.when(pl.program_id(0) == 0)
        def _():
            wb[...] = w_ref[...].astype(BF)
            for src, dst in zip(o_in, o_out):
                dst[...] = src[...].astype(BF)

        xb = x_ref[...].astype(BF)
        xb_ref[...] = xb
        p_ref[...] = _dot(xb, wb[...]).astype(p_ref.dtype)

    whole = lambda a: pl.BlockSpec(a.shape, lambda i, pos: (0, 0), pipeline_mode=pl.Buffered(1))
    placed = lambda a, ax: pl.BlockSpec(a.shape, (lambda i, pos: (0, pos[0])) if ax == 1 else (lambda i, pos: (pos[0], 0)))
    full = lambda a, ax: SDS((a.shape[0], 4 * a.shape[1]) if ax == 1 else (4 * a.shape[0], a.shape[1]), BF)
    outs, extra = _host_call(
        body, name="in_proj_own", grid=(s // tm,), prefetch=pos,
        in_specs=[pl.BlockSpec((tm, d), lambda i, pos: (i, 0)), whole(w_shard)] + [whole(a) for a in others],
        out_specs=[pl.BlockSpec((tm, tn), lambda i, pos: (i, pos[0])), pl.BlockSpec((tm, d), lambda i, pos: (i, 0))]
        + [placed(a, ax) for a, ax in zip(others, other_axes)],
        out_shape=[SDS((s, 4 * tn), P_DT), SDS((s, d), BF)] + [full(a, ax) for a, ax in zip(others, other_axes)],
        scratch_shapes=[pltpu.VMEM((d, tn), BF)], args=(x, w_shard, *others), hosted=hosted)
    return outs, extra


def _in_proj_rest(xb, wi, p, pos, diagonal, other):
    s, d = xb.shape
    n = wi.shape[1]
    tm, tn = min(s // 4, PROJ_REST_TILE), n // 4
    ni = s // tm

    def body(pos_ref, xb_ref, w_ref, p_in, p_ref, wdiag, dsem, stage_outs):
        j = pl.program_id(0)
        step = j * ni + pl.program_id(1)

        def diagonal_block(act):
            for chip in range(4):
                @pl.when(pos_ref[0] == chip)
                def _(chip=chip):
                    act(pltpu.make_async_copy(stage_outs[0].at[:, pl.ds((chip ^ 3) * tn, tn)], wdiag, dsem))

        @pl.when(step == 2 * ni - 1)
        def _():
            diagonal_block(lambda cp: cp.start())

        @pl.when(step == 2 * ni)
        def _():
            diagonal_block(lambda cp: cp.wait())

        @pl.when(j < 2)
        def _():
            p_ref[...] = _dot(xb_ref[...], w_ref[...]).astype(p_ref.dtype)

        @pl.when(j == 2)
        def _():
            p_ref[...] = _dot(xb_ref[...], wdiag[...]).astype(p_ref.dtype)

    def col(j, i, pos):
        return lax.bitwise_xor(pos[0], jnp.where(j == 0, 2, jnp.where(j == 1, 1, 3)))

    def piped_col(j, i, pos):
        return lax.bitwise_xor(pos[0], jnp.where(j == 0, 2, 1))

    (p,), extra = _host_call(
        body, name="in_proj_rest", grid=(3, ni), prefetch=pos, aliases={2: 0}, body_gets_stage_refs=True,
        in_specs=[pl.BlockSpec((tm, d), lambda j, i, pos: (i, 0)),
                  pl.BlockSpec((d, tn), lambda j, i, pos: (0, piped_col(j, i, pos))), ANY],
        out_specs=[pl.BlockSpec((tm, tn), lambda j, i, pos: (i, col(j, i, pos)))],
        out_shape=[SDS((s, n), P_DT)], scratch_shapes=[pltpu.VMEM((d, tn), BF), pltpu.SemaphoreType.DMA(())],
        args=(xb, wi, p),
        hosted=[(diagonal, [0, ni + 1], [ni, 2 * ni - 2]), (other, 0, 3 * ni - 1)])
    n_d = len(diagonal.outs)
    return p, extra[:n_d], extra[n_d:]


def _col_spec(tm, d, k):
    return pl.BlockSpec((tm, d), lambda i, k=k: (i, k))


def _prev_halo_spec(tm, d, k):
    r = tm // HALO
    return pl.BlockSpec((HALO, d), lambda i, k=k: (jnp.maximum(i * r - 1, 0), k))


def _next_halo_spec(tm, d, k, s):
    r = tm // HALO
    last = s // HALO - 1
    return pl.BlockSpec((HALO, d), lambda i, k=k: (jnp.minimum((i + 1) * r, last), k))


def _conv_fwd(p, cwa, cwb, bias_b, lbg, lbb, d, hosted=None):
    s = p.shape[0]
    tm = min(s, CONV_TILE)
    nt = s // tm

    def body(ba_ref, ca_ref, va_ref, vb_ref, gb_ref, hca_ref, hva_ref, hvb_ref, hgb_ref,
             cwa_ref, cwb_ref, bias_ref, lbg_ref, lbb_ref,
             conva_ref, yapre_ref, nb_ref, rstdb_ref, u3_ref,
             zbuf, ubuf, u1buf):
        i = pl.program_id(0)
        keep = (i > 0).astype(F32)
        zbuf[pl.ds(0, HALO), :] = hca_ref[...].astype(F32) * hva_ref[...].astype(F32) * keep
        ubuf[pl.ds(0, HALO), :] = hvb_ref[...].astype(F32) * _sig(hgb_ref[...].astype(F32)) * keep
        zbuf[pl.ds(HALO, tm), :] = ca_ref[...].astype(F32) * va_ref[...].astype(F32)
        ubuf[pl.ds(HALO, tm), :] = vb_ref[...].astype(F32) * _sig(gb_ref[...].astype(F32))

        def chunk(j, carry):
            r0 = pl.multiple_of(j * CONV_ROWS, CONV_ROWS)
            rows = pl.ds(r0, CONV_ROWS)
            for lc in range(d // CONV_LANES):
                ls = pl.ds(lc * CONV_LANES, CONV_LANES)
                acc = jnp.zeros((CONV_ROWS, CONV_LANES), F32)
                for k, sl in _taps(zbuf[pl.ds(r0, CONV_ROWS + HALO), ls], CAUSAL_A, CONV_ROWS):
                    acc = acc + cwa_ref[pl.ds(k, 1), ls] * sl
                conva_ref[rows, ls] = acc.astype(BF)
                yapre_ref[rows, ls] = (ba_ref[rows, ls].astype(F32) * acc).astype(BF)
                acc = jnp.zeros((CONV_ROWS, CONV_LANES), F32)
                for k, sl in _taps(ubuf[pl.ds(r0, CONV_ROWS + HALO), ls], CAUSAL_B, CONV_ROWS):
                    acc = acc + cwb_ref[pl.ds(k, 1), ls] * sl
                u1buf[rows, ls] = acc + bias_ref[:, ls]
            return carry

        lax.fori_loop(0, tm // CONV_ROWS, chunk, 0)
        nb, rstd = _ln_fwd(u1buf[...])
        nb_ref[...] = nb
        rstdb_ref[...] = rstd
        u2 = nb * lbg_ref[...] + lbb_ref[...]
        u3_ref[...] = (u2 * _sig(u2)).astype(BF)

    vec = _const((1, d))
    return _host_call(
        body, name="conv_fwd", grid=(nt,),
        in_specs=[_col_spec(tm, d, k) for k in range(5)] + [_prev_halo_spec(tm, d, k) for k in (1, 2, 3, 4)]
        + [_const((K_A, d)), _const((K_B, d)), vec, vec, vec],
        out_specs=[pl.BlockSpec((tm, d), lambda i: (i, 0)), pl.BlockSpec((tm, d), lambda i: (i, 0)),
                   pl.BlockSpec((tm, d), lambda i: (i, 0)), pl.BlockSpec((tm, 1), lambda i: (i, 0)),
                   pl.BlockSpec((tm, d), lambda i: (i, 0))],
        out_shape=[SDS((s, d), BF), SDS((s, d), BF), SDS((s, d), F32), SDS((s, 1), F32), SDS((s, d), BF)],
        scratch_shapes=[pltpu.VMEM((HALO + tm, d), F32), pltpu.VMEM((HALO + tm, d), F32), pltpu.VMEM((tm, d), F32)],
        args=(p, p, p, p, p, p, p, p, p, cwa, cwb, bias_b, lbg, lbb), hosted=hosted)


def _mixer_out(yapre, u3, p, x, woa, wob, wo, d, hosted=None):
    s = x.shape[0]
    tm = min(s, MIXER_TILE)

    def body(yapre_ref, u3_ref, ga_ref, gb_ref, x_ref, woa_ref, wob_ref, wo_ref,
             ya_ref, yb_ref, merged_ref, n1_ref, rstd1_ref):
        ya = _dot(yapre_ref[...], woa_ref[...])
        yb = _dot(u3_ref[...], wob_ref[...])
        ya_ref[...] = ya.astype(BF)
        yb_ref[...] = yb.astype(BF)
        merged = (_sig(ga_ref[...].astype(F32)) * ya + _sig(gb_ref[...].astype(F32)) * yb).astype(BF)
        merged_ref[...] = merged
        r1 = F32(ALPHA) * x_ref[...] + _dot(merged, wo_ref[...])
        n1, rstd1 = _ln_fwd(r1)
        n1_ref[...] = n1
        rstd1_ref[...] = rstd1

    row = pl.BlockSpec((tm, d), lambda i: (i, 0))
    return _host_call(
        body, name="mixer_out", grid=(s // tm,),
        in_specs=[row, row, _col_spec(tm, d, 5), _col_spec(tm, d, 6), row,
                  _const((d, d)), _const((d, d)), _const((d, d))],
        out_specs=[row, row, row, row, pl.BlockSpec((tm, 1), lambda i: (i, 0))],
        out_shape=[SDS((s, d), BF), SDS((s, d), BF), SDS((s, d), BF), SDS((s, d), F32), SDS((s, 1), F32)],
        scratch_shapes=[], args=(yapre, u3, p, p, x, woa, wob, wo), hosted=hosted)


def _mlp(n1, rstd1, tgt, wup, wdown, l1g, l1b, l2g, l2b):
    s, d = n1.shape
    dff = wup.shape[1]
    tm = min(s, LOCAL_TILE)
    fc = min(dff, FF_CHUNK)
    nq = dff // fc

    def body(n1_ref, rstd1_ref, tgt_ref, wup_ref, wdown_ref, l1g_ref, l1b_ref, l2g_ref, l2b_ref,
             x1b_ref, hb_ref, dhpre_ref, dr2b_ref, dr1_ref, dr1b_ref, acc_ref, rbuf):
        i = pl.program_id(0)
        n1v = n1_ref[...]
        x1 = n1v * l1g_ref[...] + l1b_ref[...]
        x1b = x1.astype(BF)
        x1b_ref[...] = x1b
        ff = jnp.zeros((tm, d), F32)
        for q in range(nq):
            cs = pl.ds(q * fc, fc)
            r = jnp.maximum(_dot(x1b, wup_ref[:, cs]), 0.0)
            rbuf[:, cs] = r
            hq = (r * r).astype(BF)
            hb_ref[:, cs] = hq
            ff = ff + _dot(hq, wdown_ref[cs, :])
        n2, rstd2 = _ln_fwd(F32(ALPHA) * x1 + ff)
        x2 = n2 * l2g_ref[...] + l2b_ref[...]
        err = x2 - tgt_ref[...]
        dx2 = err * F32(1.0 / d)
        dr2 = _ln_bwd(dx2 * l2g_ref[...], n2, rstd2)
        dr2b = dr2.astype(BF)
        dr2b_ref[...] = dr2b
        dx1 = F32(ALPHA) * dr2
        for q in range(nq):
            cs = pl.ds(q * fc, fc)
            dh = _dot_nt(dr2b, wdown_ref[cs, :])
            dhp = (dh * (2.0 * rbuf[:, cs])).astype(BF)
            dhpre_ref[:, cs] = dhp
            dx1 = dx1 + _dot_nt(dhp, wup_ref[:, cs])
        dr1 = _ln_bwd(dx1 * l1g_ref[...], n1v, rstd1_ref[...])
        dr1_ref[...] = dr1
        dr1b_ref[...] = dr1.astype(BF)

        @pl.when(i == 0)
        def _():
            acc_ref[...] = jnp.zeros_like(acc_ref)

        for q, val in enumerate((err * err, dx2 * n2, dx2, dx1 * n1v, dx1)):
            acc_ref[pl.ds(q, 1), :] += jnp.sum(val, axis=0, keepdims=True)

    row = pl.BlockSpec((tm, d), lambda i: (i, 0))
    wide = pl.BlockSpec((tm, dff), lambda i: (i, 0))
    vec = _const((1, d))
    return pl.pallas_call(
        body, name="mlp_fwd_bwd", grid=(s // tm,),
        in_specs=[row, pl.BlockSpec((tm, 1), lambda i: (i, 0)), row, _const((d, dff)), _const((dff, d)),
                  vec, vec, vec, vec],
        out_specs=[row, wide, wide, row, row, row, pl.BlockSpec((8, d), lambda i: (0, 0))],
        out_shape=[SDS((s, d), BF), SDS((s, dff), BF), SDS((s, dff), BF), SDS((s, d), BF), SDS((s, d), F32),
                   SDS((s, d), BF), SDS((8, d), F32)],
        scratch_shapes=[pltpu.VMEM((tm, dff), F32)],
        compiler_params=_cp(("arbitrary",)),
    )(n1, rstd1, tgt, wup, wdown, l1g, l1b, l2g, l2b)


def _mixer_bwd_local(dr1b, p, ya, yb, conva, nb, rstdb, wo, woa, wob, lbg, lbb):
    s, d = ya.shape
    tm = min(s, MIXER_BWD_TILE)

    def body(dr1b_ref, ba_ref, ga_ref, gb_ref, ya_ref, yb_ref, conva_ref, nb_ref, rstdb_ref,
             wo_ref, woa_ref, wob_ref, lbg_ref, lbb_ref,
             dya_ref, dyb_ref, dba_ref, dga_ref, dgb_ref, dca_ref, du1_ref, acc_ref):
        i = pl.program_id(0)
        dmerged = _dot_nt(dr1b_ref[...], wo_ref[...])
        sa = _sig(ga_ref[...].astype(F32))
        sb = _sig(gb_ref[...].astype(F32))
        dya = (dmerged * sa).astype(BF)
        dyb = (dmerged * sb).astype(BF)
        dya_ref[...] = dya
        dyb_ref[...] = dyb
        dga_ref[...] = (dmerged * ya_ref[...].astype(F32) * (sa * (1.0 - sa))).astype(BF)
        dgb_ref[...] = (dmerged * yb_ref[...].astype(F32) * (sb * (1.0 - sb))).astype(BF)
        dyapre = _dot_nt(dya, woa_ref[...])
        dba_ref[...] = (dyapre * conva_ref[...].astype(F32)).astype(BF)
        dca_ref[...] = (dyapre * ba_ref[...].astype(F32)).astype(BF)
        du3 = _dot_nt(dyb, wob_ref[...])
        nbv = nb_ref[...]
        u2 = nbv * lbg_ref[...] + lbb_ref[...]
        sg = _sig(u2)
        du2 = du3 * (sg * (1.0 + u2 * (1.0 - sg)))
        du1 = _ln_bwd(du2 * lbg_ref[...], nbv, rstdb_ref[...])
        du1_ref[...] = du1.astype(BF)

        @pl.when(i == 0)
        def _():
            acc_ref[...] = jnp.zeros_like(acc_ref)

        for q, val in enumerate((du2 * nbv, du2, du1)):
            acc_ref[pl.ds(q, 1), :] += jnp.sum(val, axis=0, keepdims=True)

    row = pl.BlockSpec((tm, d), lambda i: (i, 0))
    vec = _const((1, d))
    return pl.pallas_call(
        body, name="mixer_bwd_local", grid=(s // tm,),
        in_specs=[row, _col_spec(tm, d, 0), _col_spec(tm, d, 5), _col_spec(tm, d, 6), row, row, row, row,
                  pl.BlockSpec((tm, 1), lambda i: (i, 0)), _const((d, d)), _const((d, d)), _const((d, d)), vec, vec],
        out_specs=[row, row, row, row, row, row, row, pl.BlockSpec((8, d), lambda i: (0, 0))],
        out_shape=[SDS((s, d), BF)] * 7 + [SDS((8, d), F32)],
        compiler_params=_cp(("arbitrary",)),
    )(dr1b, p, p, p, ya, yb, conva, nb, rstdb, wo, woa, wob, lbg, lbb)


def _conv_bwd(dca, du1, p, dba, dga, dgb, cwa, cwb, d, hosted=None):
    s = dca.shape[0]
    tm = min(s, CONV_TILE)
    nt = s // tm

    def body(dca_ref, du1_ref, ndca_ref, ndu1_ref, ca_ref, va_ref, vb_ref, gb_ref,
             dba_ref, dga_ref, dgb_ref, cwa_ref, cwb_ref,
             dp_ref, dcwa_ref, dcwb_ref,
             dcabuf, du1buf, sgbuf, acca, accb):
        i = pl.program_id(0)
        keep_next = (i < nt - 1).astype(F32)

        @pl.when(i == 0)
        def _():
            acca[...] = jnp.zeros_like(acca)
            accb[...] = jnp.zeros_like(accb)

        sgbuf[...] = _sig(gb_ref[...].astype(F32))
        dcabuf[pl.ds(0, tm), :] = dca_ref[...].astype(F32)
        dcabuf[pl.ds(tm, HALO), :] = ndca_ref[...].astype(F32) * keep_next
        du1buf[pl.ds(0, tm), :] = du1_ref[...].astype(F32)
        du1buf[pl.ds(tm, HALO), :] = ndu1_ref[...].astype(F32) * keep_next
        dp_ref[:, pl.ds(0, d)] = dba_ref[...]
        dp_ref[:, pl.ds(5 * d, d)] = dga_ref[...]
        dp_ref[:, pl.ds(6 * d, d)] = dgb_ref[...]

        def chunk(j, carry):
            r0 = pl.multiple_of(j * CONV_ROWS, CONV_ROWS)
            rows = pl.ds(r0, CONV_ROWS)
            for lc in range(d // CONV_LANES):
                lo = lc * CONV_LANES
                ls = pl.ds(lo, CONV_LANES)
                cac = ca_ref[rows, ls].astype(F32)
                vac = va_ref[rows, ls].astype(F32)
                zc = cac * vac
                acc = jnp.zeros((CONV_ROWS, CONV_LANES), F32)
                for k, sl in _taps(dcabuf[pl.ds(r0, CONV_ROWS + HALO), ls], ANTI_A, CONV_ROWS):
                    acc = acc + cwa_ref[pl.ds(k, 1), ls] * sl
                    acca[pl.ds(8 * k, 8), ls] += _rowsum8(sl * zc)
                dp_ref[rows, pl.ds(d + lo, CONV_LANES)] = (acc * vac).astype(BF)
                dp_ref[rows, pl.ds(2 * d + lo, CONV_LANES)] = (acc * cac).astype(BF)
                sgc = sgbuf[rows, ls]
                vbc = vb_ref[rows, ls].astype(F32)
                uc = vbc * sgc
                acc = jnp.zeros((CONV_ROWS, CONV_LANES), F32)
                for k, sl in _taps(du1buf[pl.ds(r0, CONV_ROWS + HALO), ls], ANTI_B, CONV_ROWS):
                    acc = acc + cwb_ref[pl.ds(k, 1), ls] * sl
                    accb[pl.ds(8 * k, 8), ls] += _rowsum8(sl * uc)
                dp_ref[rows, pl.ds(3 * d + lo, CONV_LANES)] = (acc * sgc).astype(BF)
                dp_ref[rows, pl.ds(4 * d + lo, CONV_LANES)] = (acc * vbc * (sgc * (1.0 - sgc))).astype(BF)
            return carry

        lax.fori_loop(0, tm // CONV_ROWS, chunk, 0)

        @pl.when(i == nt - 1)
        def _():
            dcwa_ref[...] = jnp.zeros_like(dcwa_ref)
            dcwb_ref[...] = jnp.zeros_like(dcwb_ref)
            for k in range(K_A):
                dcwa_ref[pl.ds(k, 1), :] = jnp.sum(acca[pl.ds(8 * k, 8), :], axis=0, keepdims=True)
            for k in range(K_B):
                dcwb_ref[pl.ds(k, 1), :] = jnp.sum(accb[pl.ds(8 * k, 8), :], axis=0, keepdims=True)

    row = pl.BlockSpec((tm, d), lambda i: (i, 0))
    nxt = _next_halo_spec(tm, d, 0, s)
    return _host_call(
        body, name="conv_bwd", grid=(nt,),
        in_specs=[row, row, nxt, nxt] + [_col_spec(tm, d, k) for k in (1, 2, 3, 4)]
        + [row, row, row, _const((K_A, d)), _const((K_B, d))],
        out_specs=[pl.BlockSpec((tm, 7 * d), lambda i: (i, 0)), pl.BlockSpec((8, d), lambda i: (0, 0)),
                   pl.BlockSpec((32, d), lambda i: (0, 0))],
        out_shape=[SDS((s, 7 * d), BF), SDS((8, d), F32), SDS((32, d), F32)],
        scratch_shapes=[pltpu.VMEM((tm + HALO, d), F32), pltpu.VMEM((tm + HALO, d), F32),
                        pltpu.VMEM((tm, d), F32), pltpu.VMEM((8 * K_A, d), F32), pltpu.VMEM((8 * K_B, d), F32)],
        args=(dca, du1, dca, du1, p, p, p, p, dba, dga, dgb, cwa, cwb), hosted=hosted)


def _grad_w(a, b, name, hosted=None, tokens=None):
    s, m = a.shape
    n = b.shape[1]
    tm, tn, tk = _tile(m, GRAD_W_TILE), _tile(n, GRAD_W_TILE), _tile(s, tokens or GRAD_W_TOKENS)
    nk = s // tk

    def body(a_ref, b_ref, o_ref, ob_ref):
        k = pl.program_id(2)

        @pl.when(k == 0)
        def _():
            o_ref[...] = jnp.zeros_like(o_ref)

        o_ref[...] += _dot_tn(a_ref[...], b_ref[...])

        @pl.when(k == nk - 1)
        def _():
            ob_ref[...] = o_ref[...].astype(BF)

    blk = pl.BlockSpec((tm, tn), lambda i, j, k: (i, j))
    (g, gb), extra = _host_call(
        body, name=name, grid=(m // tm, n // tn, nk),
        in_specs=[pl.BlockSpec((tk, tm), lambda i, j, k: (k, i)), pl.BlockSpec((tk, tn), lambda i, j, k: (k, j))],
        out_specs=[blk, blk], out_shape=[SDS((m, n), F32), SDS((m, n), BF)], scratch_shapes=[], args=(a, b),
        hosted=hosted)
    return g, gb, extra


def _grad_x(dr1, dp, wi, hosted=None):
    s, d = dr1.shape
    n = wi.shape[1]
    tm = min(s, GRAD_X_TILE)

    def body(dr1_ref, dp_ref, w_ref, o_ref):
        o_ref[...] = F32(ALPHA) * dr1_ref[...] + _dot_nt(dp_ref[...], w_ref[...])

    (gx,), extra = _host_call(
        body, name="grad_x", grid=(s // tm,),
        in_specs=[pl.BlockSpec((tm, d), lambda i: (i, 0)), pl.BlockSpec((tm, n), lambda i: (i, 0)), _const((d, n))],
        out_specs=[pl.BlockSpec((tm, d), lambda i: (i, 0))],
        out_shape=[SDS((s, d), F32)], scratch_shapes=[], args=(dr1, dp, wi), hosted=hosted)
    return gx, extra


def _adamw_math(w, g, m, v):
    m2 = ADAM_B1 * m + (1.0 - ADAM_B1) * g
    v2 = ADAM_B2 * v + (1.0 - ADAM_B2) * (g * g)
    m_hat = m2 / (1.0 - ADAM_B1 ** ADAM_STEP)
    v_hat = v2 / (1.0 - ADAM_B2 ** ADAM_STEP)
    delta = -ADAM_LR * (m_hat / (jnp.sqrt(v_hat) + ADAM_EPS) + ADAM_WD * w)
    return delta, m2, v2


def _adamw(w, g, m, v, name):
    r, c = w.shape
    tr = min(r, ROWS_TILE)

    def body(w_ref, g_ref, m_ref, v_ref, g_out, d_ref, m2_ref, v2_ref):
        gv = g_ref[...]
        delta, m2, v2 = _adamw_math(w_ref[...], gv, m_ref[...], v_ref[...])
        g_out[...] = gv
        d_ref[...] = delta
        m2_ref[...] = m2
        v2_ref[...] = v2

    blk = pl.BlockSpec((tr, c), lambda i: (i, 0))
    return pl.pallas_call(
        body, name=name, grid=(r // tr,), in_specs=[blk] * 4, out_specs=[blk] * 4,
        out_shape=[SDS((r, c), F32)] * 4, compiler_params=_cp(("parallel",)),
    )(w, g, m, v)


def _sum_parts(parts, name):
    k, r, c = parts.shape

    def body(p_ref, o_ref):
        acc = p_ref[0]
        for q in range(1, k):
            acc = acc + p_ref[q]
        o_ref[...] = acc

    return pl.pallas_call(
        body, name=name, grid=(1,),
        in_specs=[pl.BlockSpec((k, r, c), lambda i: (0, 0, 0))],
        out_specs=pl.BlockSpec((r, c), lambda i: (0, 0)),
        out_shape=SDS((r, c), F32), compiler_params=_cp(("arbitrary",)),
    )(parts)


def _piece_shape(full_shape, axis):
    r, c = full_shape
    return (r // 2, c // 4) if axis == 1 else (r // 8, c)


def _piece_spec(full_shape, axis, tr, chip_of, half_of):
    hr, wc = _piece_shape(full_shape, axis)
    nb = hr // tr
    if axis == 1:
        return pl.BlockSpec((tr, wc), lambda *a: (half_of(*a) * nb + a[-2], chip_of(*a)))
    return pl.BlockSpec((tr, wc), lambda *a: ((2 * chip_of(*a) + half_of(*a)) * nb + a[-2], 0))


def _place_cast(w, axis, pos, name):
    r, c = w.shape
    tr = min(r, ROWS_TILE)
    nb = r // tr
    full = (r, 4 * c) if axis == 1 else (4 * r, c)
    out_map = (lambda i, pos: (i, pos[0])) if axis == 1 else (lambda i, pos: (pos[0] * nb + i, 0))

    def body(pos_ref, w_ref, o_ref):
        o_ref[...] = w_ref[...].astype(o_ref.dtype)

    gs = pltpu.PrefetchScalarGridSpec(
        num_scalar_prefetch=1, grid=(nb,),
        in_specs=[pl.BlockSpec((tr, c), lambda i, pos: (i, 0))], out_specs=pl.BlockSpec((tr, c), out_map))
    return pl.pallas_call(body, name=name, grid_spec=gs, out_shape=SDS(full, BF),
                          compiler_params=_cp(("arbitrary",)))(pos, w)


def _pair_add(g, land, axis, pos, name):
    hr, wc = _piece_shape(g.shape, axis)
    tr = min(hr, SUM_ROWS)

    def body(pos_ref, g_ref, l_ref, o_ref):
        o_ref[0] = (g_ref[...] + l_ref[0].astype(F32)).astype(BF)

    other = lambda q, i, pos: (pos[0] + 1 + q) % 4
    blk = pl.BlockSpec((1, tr, wc), lambda q, i, pos: (other(q, i, pos), i, 0))
    gs = pltpu.PrefetchScalarGridSpec(
        num_scalar_prefetch=1, grid=(3, hr // tr),
        in_specs=[_piece_spec(g.shape, axis, tr, other, lambda q, i, pos: pos[1]), blk], out_specs=blk)
    return pl.pallas_call(body, name=name, grid_spec=gs, out_shape=SDS((4, hr, wc), BF),
                          compiler_params=_cp(("arbitrary", "arbitrary")))(pos, g, land)


def _chip_sum(g, land1, land2, axis, pos, name):
    hr, wc = _piece_shape(g.shape, axis)
    tr = min(hr, SUM_ROWS)
    nb = hr // tr

    def body(pos_ref, g_ref, l1_ref, l2_ref, o_ref):
        acc = g_ref[...] + l1_ref[0].astype(F32)
        for q in range(3):
            acc = acc + l2_ref[q].astype(F32)
        o_ref[...] = acc

    gs = pltpu.PrefetchScalarGridSpec(
        num_scalar_prefetch=1, grid=(nb,),
        in_specs=[_piece_spec(g.shape, axis, tr, lambda i, pos: pos[0], lambda i, pos: pos[1]),
                  pl.BlockSpec((1, tr, wc), lambda i, pos: (pos[0], i, 0)),
                  pl.BlockSpec((3, tr, wc), lambda i, pos: (0, i, 0))],
        out_specs=pl.BlockSpec((tr, wc), lambda i, pos: (pos[1] * nb + i, 0)))
    return pl.pallas_call(body, name=name, grid_spec=gs, out_shape=SDS((2 * hr, wc), F32),
                          compiler_params=_cp(("arbitrary",)))(pos, g, land1, land2)


ANY = pl.BlockSpec(memory_space=pl.ANY)
COMM = pltpu.CompilerParams(has_side_effects=True)


def _on_each_device(fn):
    x, y, c = lax.axis_index("x"), lax.axis_index("y"), lax.axis_index("c")
    for sx in (0, 1):
        for sy in (0, 1):
            for sc in (0, 1):
                @pl.when(jnp.logical_and(jnp.logical_and(x == sx, y == sy), c == sc))
                def _(sx=sx, sy=sy, sc=sc):
                    fn(sx, sy, sc)


def _remote(src, dst, send_sem, recv_sem, to):
    return pltpu.make_async_remote_copy(src_ref=src, dst_ref=dst, send_sem=send_sem, recv_sem=recv_sem,
                                        device_id=to, device_id_type=MESH)


def _piece_ref(ref, axis, j, h):
    r, c = ref.shape
    hr, wc = _piece_shape((r, c), axis)
    if axis == 1:
        return ref.at[pl.ds(h * hr, hr), pl.ds(j * wc, wc)]
    return ref.at[pl.ds((2 * j + h) * hr, hr), :]


class _Stage:
    def __init__(self, ins, outs, n_sems, make_plan, aliases=None):
        self.ins, self.outs, self.n_sems, self.make_plan = list(ins), list(outs), n_sems, make_plan
        self.aliases = dict(aliases or {})


def _start(plan):
    def dev(sx, sy, sc):
        for cp, _, _ in plan(sx, sy, sc):
            cp.start()

    _on_each_device(dev)


def _finish(plan):
    def dev(sx, sy, sc):
        for _, sent, got in plan(sx, sy, sc):
            sent.wait_send()
            got.wait_recv()

    _on_each_device(dev)


def _comm_call(stage, name):
    n_in, n_out = len(stage.ins), len(stage.outs)

    def body(*refs):
        plan = stage.make_plan(refs[:n_in], refs[n_in:n_in + n_out], *refs[n_in + n_out:])
        _start(plan)
        _finish(plan)

    return pl.pallas_call(
        body, name=name, in_specs=[ANY] * n_in, out_specs=[ANY] * n_out, out_shape=stage.outs,
        input_output_aliases=stage.aliases,
        scratch_shapes=[pltpu.SemaphoreType.DMA((stage.n_sems,)), pltpu.SemaphoreType.DMA((stage.n_sems,))],
        compiler_params=COMM,
    )(*stage.ins)


class _SemsFrom:
    def __init__(self, sems, base):
        self.sems, self.base = sems, base

    @property
    def at(self):
        return self

    def __getitem__(self, k):
        return self.sems.at[self.base + k]


def _both(a, b):
    na, nb = len(a.ins), len(b.ins)
    ma = len(a.outs)

    def make_plan(ins, outs, send_sems, recv_sems):
        pa = a.make_plan(ins[:na], outs[:ma], send_sems, recv_sems)
        pb = b.make_plan(ins[na:], outs[ma:], _SemsFrom(send_sems, a.n_sems), _SemsFrom(recv_sems, a.n_sems))
        return lambda sx, sy, sc: pa(sx, sy, sc) + pb(sx, sy, sc)

    aliases = dict(a.aliases)
    aliases.update({na + i: ma + o for i, o in b.aliases.items()})
    return _Stage(a.ins + b.ins, a.outs + b.outs, a.n_sems + b.n_sems, make_plan, aliases)


def _same(cp):
    return (cp, cp, cp)


def _stage_gather_send(fulls, axes):
    n = len(fulls)

    def make_plan(ins, outs, send_sems, recv_sems):
        def plan(sx, sy, sc):
            cps = []
            for w in range(n):
                mine = _piece_ref(outs[w], axes[w], 2 * sx + sy, sc)
                for r, (fx, fy) in enumerate(CHIP_RELS):
                    k = 3 * w + r
                    to = (sx ^ fx, sy ^ fy, sc)
                    got = _piece_ref(outs[w], axes[w], 2 * (sx ^ fx) + (sy ^ fy), sc)
                    send = _remote(mine, mine, send_sems.at[k], recv_sems.at[k], to)
                    cps.append((send, send, _remote(got, got, send_sems.at[k], recv_sems.at[k], to)))
            return cps

        return plan

    return _Stage(fulls, [SDS(f.shape, f.dtype) for f in fulls], 3 * n, make_plan, {i: i for i in range(n)})


def _stage_gather_forward(fulls, axes):
    n = len(fulls)

    def make_plan(ins, outs, send_sems, recv_sems):
        def plan(sx, sy, sc):
            cps = []
            sib = (sx, sy, 1 - sc)
            for w in range(n):
                for r, (fx, fy) in enumerate(CHIP_RELS):
                    k = 3 * w + r
                    pj = 2 * (sx ^ fx) + (sy ^ fy)
                    have = _piece_ref(outs[w], axes[w], pj, sc)
                    want = _piece_ref(outs[w], axes[w], pj, 1 - sc)
                    send = _remote(have, have, send_sems.at[k], recv_sems.at[k], sib)
                    cps.append((send, send, _remote(want, want, send_sems.at[k], recv_sems.at[k], sib)))
            return cps

        return plan

    return _Stage(fulls, [SDS(f.shape, f.dtype) for f in fulls], 3 * n, make_plan, {i: i for i in range(n)})


def _in_place(fulls, n_sems, make_plan):
    return _Stage(fulls, [SDS(f.shape, f.dtype) for f in fulls], n_sems, make_plan, {i: i for i in range(len(fulls))})


def _stage_gather_neighbours(fulls, axes):
    n = len(fulls)

    def make_plan(ins, outs, send_sems, recv_sems):
        def plan(sx, sy, sc):
            cps = []
            for w in range(n):
                mine = _piece_ref(outs[w], axes[w], 2 * sx + sy, sc)
                for r, (px, py) in enumerate(((sx ^ 1, sy), (sx, sy ^ 1))):
                    got = _piece_ref(outs[w], axes[w], 2 * px + py, sc)
                    send = _remote(mine, mine, send_sems.at[2 * w + r], recv_sems.at[2 * w + r], (px, py, sc))
                    cps.append((send, send, _remote(got, got, send_sems.at[2 * w + r], recv_sems.at[2 * w + r],
                                                    (px, py, sc))))
            return cps

        return plan

    return _in_place(fulls, 2 * n, make_plan)


def _stage_gather_near(fulls, axes):
    first, second = _stage_gather_neighbours(fulls, axes), _stage_gather_pair(fulls, axes)

    def make_plan(ins, outs, send_sems, recv_sems):
        return [first.make_plan(ins, outs, send_sems, recv_sems),
                second.make_plan(ins, outs, _SemsFrom(send_sems, first.n_sems), _SemsFrom(recv_sems, first.n_sems))]

    return _in_place(fulls, first.n_sems + second.n_sems, make_plan)


def _stage_gather_pair(fulls, axes):
    n = len(fulls)

    def make_plan(ins, outs, send_sems, recv_sems):
        def plan(sx, sy, sc):
            cps = []
            sib = (sx, sy, 1 - sc)
            for w in range(n):
                for r, j in enumerate((2 * (sx ^ 1) + sy, 2 * sx + (sy ^ 1))):
                    k = 2 * w + r
                    have, want = _piece_ref(outs[w], axes[w], j, sc), _piece_ref(outs[w], axes[w], j, 1 - sc)
                    send = _remote(have, have, send_sems.at[k], recv_sems.at[k], sib)
                    cps.append((send, send, _remote(want, want, send_sems.at[k], recv_sems.at[k], sib)))
            return cps

        return plan

    return _in_place(fulls, 2 * n, make_plan)


def _stage_gather_diagonal(fulls, axes):
    n = len(fulls)

    def make_plan(ins, outs, send_sems, recv_sems):
        def relay(sx, sy, sc):
            cps = []
            jx, jy, jd = 2 * (sx ^ 1) + sy, 2 * sx + (sy ^ 1), 2 * (sx ^ 1) + (sy ^ 1)
            passed, to = (jx, (sx, sy ^ 1, sc)) if sc == 0 else (jy, (sx ^ 1, sy, sc))
            for w in range(n):
                have, want = _piece_ref(outs[w], axes[w], passed, sc), _piece_ref(outs[w], axes[w], jd, sc)
                send = _remote(have, have, send_sems.at[w], recv_sems.at[w], to)
                cps.append((send, send, _remote(want, want, send_sems.at[w], recv_sems.at[w], to)))
            return cps

        def cross(sx, sy, sc):
            cps = []
            sib = (sx, sy, 1 - sc)
            jd = 2 * (sx ^ 1) + (sy ^ 1)
            for w in range(n):
                have, want = _piece_ref(outs[w], axes[w], jd, sc), _piece_ref(outs[w], axes[w], jd, 1 - sc)
                send = _remote(have, have, send_sems.at[n + w], recv_sems.at[n + w], sib)
                cps.append((send, send, _remote(want, want, send_sems.at[n + w], recv_sems.at[n + w], sib)))
            return cps

        return [relay, cross]

    return _in_place(fulls, 2 * n, make_plan)


def _stage_pair_exchange(grads, axes):
    n = len(grads)

    def make_plan(gs, land, send_sems, recv_sems):
        def plan(sx, sy, sc):
            return [_same(_remote(_piece_ref(gs[w], axes[w], jj, 1 - sc), land[w].at[jj], send_sems.at[4 * w + jj],
                                  recv_sems.at[4 * w + jj], (sx, sy, 1 - sc)))
                    for w in range(n) for jj in range(4)]

        return plan

    return _Stage(grads, [SDS((4,) + _piece_shape(g.shape, a), g.dtype) for g, a in zip(grads, axes)], 4 * n,
                  make_plan)


def _stage_chip_scatter(pieces):
    n = len(pieces)

    def make_plan(ps, land, send_sems, recv_sems):
        def plan(sx, sy, sc):
            return [_same(_remote(ps[w].at[2 * (sx ^ fx) + (sy ^ fy)], land[w].at[r], send_sems.at[3 * w + r],
                                  recv_sems.at[3 * w + r], (sx ^ fx, sy ^ fy, sc)))
                    for w in range(n) for r, (fx, fy) in enumerate(CHIP_RELS)]

        return plan

    return _Stage(pieces, [SDS((3,) + p.shape[1:], p.dtype) for p in pieces], 3 * n, make_plan)


def _stage_pair_share(shards):
    n = len(shards)

    def make_plan(ins, outs, send_sems, recv_sems):
        def plan(sx, sy, sc):
            cps = []
            sib = (sx, sy, 1 - sc)
            for w in range(n):
                hr = shards[w].shape[0] // 2
                mine = outs[w].at[pl.ds(sc * hr, hr), :]
                theirs = outs[w].at[pl.ds((1 - sc) * hr, hr), :]
                send = _remote(mine, mine, send_sems.at[w], recv_sems.at[w], sib)
                cps.append((send, send, _remote(theirs, theirs, send_sems.at[w], recv_sems.at[w], sib)))
            return cps

        return plan

    return _Stage(shards, [SDS(g.shape, g.dtype) for g in shards], n, make_plan, {i: i for i in range(n)})


def _stage_gather_small(stack):
    def make_plan(ins, outs, send_sems, recv_sems):
        def plan(sx, sy, sc):
            mine = outs[0].at[4 * sx + 2 * sy + sc]
            return [_same(_remote(mine, mine, send_sems.at[k], recv_sems.at[k], (sx ^ fx, sy ^ fy, sc ^ fc)))
                    for k, (fx, fy, fc) in enumerate(DEV_RELS)]

        return plan

    return _Stage([stack], [SDS(stack.shape, stack.dtype)], 7, make_plan, {0: 0})


def kernel(x, w_in, conv_a_w, w_out_a, conv_b_w, conv_b_bias, ln_b_gamma, ln_b_beta, w_out_b, w_o, ln1_gamma, ln1_beta, w_up, w_down, ln2_gamma, ln2_beta, loss_target, m_w_in, m_conv_a_w, m_w_out_a, m_conv_b_w, m_conv_b_bias, m_ln_b_gamma, m_ln_b_beta, m_w_out_b, m_w_o, m_ln1_gamma, m_ln1_beta, m_w_up, m_w_down, m_ln2_gamma, m_ln2_beta, v_w_in, v_conv_a_w, v_w_out_a, v_conv_b_w, v_conv_b_bias, v_ln_b_gamma, v_ln_b_beta, v_w_out_b, v_w_o, v_ln1_gamma, v_ln1_beta, v_w_up, v_w_down, v_ln2_gamma, v_ln2_beta):
    s, d = x.shape[1], x.shape[2]
    xs = x.reshape(s, d)
    tgt = loss_target.reshape(s, d)
    dq = d // 4
    chip = 2 * lax.axis_index("x") + lax.axis_index("y")
    core = lax.axis_index("c")
    pos = jnp.stack([chip, core]).astype(jnp.int32)
    names = ("w_in", "w_out_a", "w_out_b", "w_o", "w_up", "w_down")
    axes = (1, 0, 0, 0, 1, 0)

    conv_pack = jnp.concatenate([jnp.pad(conv_a_w, ((0, 8 - K_A), (0, 0))), jnp.pad(conv_b_w, ((0, 32 - K_B), (0, 0))),
                                 jnp.zeros((8, dq), F32)], axis=0)
    conv_full = lax.dynamic_update_slice(jnp.zeros((conv_pack.shape[0], d), F32), conv_pack, (0, chip * dq))
    wi_own = _place_cast(w_in, 1, pos, "place_w_in")
    vec = lambda a: a.reshape(1, d)
    bias_b, lbg, lbb = vec(conv_b_bias), vec(ln_b_gamma), vec(ln_b_beta)
    l1g, l1b, l2g, l2b = vec(ln1_gamma), vec(ln1_beta), vec(ln2_gamma), vec(ln2_beta)

    (p, xb, *placed), (wi, convs) = _in_proj_own(xs, w_in, (w_out_a, w_out_b, w_o, w_up, w_down), axes[1:], pos,
                                                 _stage_gather_near([wi_own, conv_full], (1, 1)))
    fulls = [wi_own] + placed
    p, (wi, convs), small3 = _in_proj_rest(xb, wi, p, pos, _stage_gather_diagonal([wi, convs], (1, 1)),
                                           _stage_gather_send(fulls[1:4], axes[1:4]))
    cwa, cwb = convs[0:K_A], convs[8:8 + K_B]
    (conva, yapre, nb, rstdb, u3), landed = _conv_fwd(
        p, cwa, cwb, bias_b, lbg, lbb, d,
        _both(_stage_gather_forward(small3, axes[1:4]), _stage_gather_send(fulls[4:6], axes[4:6])))
    woa, wob, wo = landed[:3]
    (ya, yb, merged, n1, rstd1), (wup, wdown) = _mixer_out(yapre, u3, p, xs, woa, wob, wo, d,
                                                           _stage_gather_forward(landed[3:], axes[4:6]))
    x1b, hb, dhpre, dr2b, dr1, dr1b, acc_mlp = _mlp(n1, rstd1, tgt, wup, wdown, l1g, l1b, l2g, l2b)
    g_up, gb_up, _ = _grad_w(x1b, dhpre, "grad_w_up", tokens=GRAD_W_TOKENS_MLP)
    g_down, gb_down, _ = _grad_w(hb, dr2b, "grad_w_down", tokens=GRAD_W_TOKENS_MLP)
    dya, dyb, dba, dga, dgb, dca, du1, acc_mix = _mixer_bwd_local(dr1b, p, ya, yb, conva, nb, rstdb, wo, woa, wob,
                                                                   lbg, lbb)
    g_oa, gb_oa, _ = _grad_w(yapre, dya, "grad_w_out_a")
    g_ob, gb_ob, _ = _grad_w(u3, dyb, "grad_w_out_b")
    g_o, gb_o, _ = _grad_w(merged, dr1b, "grad_w_o")

    early, e_axes, e_names = [g_oa, g_ob, g_o, g_up, g_down], axes[1:], names[1:]
    (dp, dcwa, dcwb), land1 = _conv_bwd(dca, du1, p, dba, dga, dgb, cwa, cwb, d,
                                        _stage_pair_exchange([gb_oa, gb_ob, gb_o, gb_up, gb_down], e_axes))
    pieces = [_pair_add(g, l, a, pos, "pair_add_" + nm) for g, l, a, nm in zip(early, land1, e_axes, e_names)]
    pack = jnp.concatenate([dcwa, dcwb, acc_mix, acc_mlp], axis=0)
    stack = lax.dynamic_update_slice(jnp.zeros((8,) + pack.shape, F32), pack[None], (2 * chip + core, 0, 0))
    g_wi, gb_wi, landed = _grad_w(xb, dp, "grad_w_in",
                                  _both(_stage_chip_scatter(pieces), _stage_gather_small(stack)))
    land2, stack = landed[:-1], landed[-1]
    halves = [_chip_sum(g, l1, l2, a, pos, "chip_sum_" + nm)
              for g, l1, l2, a, nm in zip(early, land1, land2, e_axes, e_names)]
    (land1_in,) = _comm_call(_stage_pair_exchange([gb_wi], (1,)), "pair_exchange_w_in")
    piece_in = _pair_add(g_wi, land1_in, 1, pos, "pair_add_w_in")
    grad_x, landed = _grad_x(dr1, dp, wi, _both(_stage_chip_scatter([piece_in]), _stage_pair_share(halves)))
    land2_in, (g_oa, g_ob, g_o, g_up, g_down) = landed[0], landed[1:]
    half_in = _chip_sum(g_wi, land1_in, land2_in, 1, pos, "chip_sum_w_in")
    (g_in,) = _comm_call(_stage_pair_share([half_in]), "pair_share_w_in")
    small = _sum_parts(stack, "small_sum")
    g_ca = lax.dynamic_slice(small, (0, chip * dq), (K_A, dq))
    g_cb = lax.dynamic_slice(small, (8, chip * dq), (K_B, dq))
    g_vec = jnp.stack([small[r] for r in (42, 40, 41, 51, 52, 49, 50)])

    loss = (0.5 / d) * jnp.sum(small[48])

    big = {}
    for name, w, g, m, v in (("w_in", w_in, g_in, m_w_in, v_w_in), ("w_out_a", w_out_a, g_oa, m_w_out_a, v_w_out_a),
                             ("w_out_b", w_out_b, g_ob, m_w_out_b, v_w_out_b), ("w_o", w_o, g_o, m_w_o, v_w_o),
                             ("w_up", w_up, g_up, m_w_up, v_w_up), ("w_down", w_down, g_down, m_w_down, v_w_down),
                             ("conv_a_w", conv_a_w, g_ca, m_conv_a_w, v_conv_a_w),
                             ("conv_b_w", conv_b_w, g_cb, m_conv_b_w, v_conv_b_w)):
        big[name] = tuple(_adamw(w, g, m, v, "adamw_" + name))
    vec_names = ("conv_b_bias", "ln_b_gamma", "ln_b_beta", "ln1_gamma", "ln1_beta", "ln2_gamma", "ln2_beta")
    w7 = jnp.stack([conv_b_bias, ln_b_gamma, ln_b_beta, ln1_gamma, ln1_beta, ln2_gamma, ln2_beta])
    m7 = jnp.stack([m_conv_b_bias, m_ln_b_gamma, m_ln_b_beta, m_ln1_gamma, m_ln1_beta, m_ln2_gamma, m_ln2_beta])
    v7 = jnp.stack([v_conv_b_bias, v_ln_b_gamma, v_ln_b_beta, v_ln1_gamma, v_ln1_beta, v_ln2_gamma, v_ln2_beta])
    g7, d7, nm7, nv7 = _adamw(w7, g_vec, m7, v7, "adamw_vectors")
    for q, name in enumerate(vec_names):
        big[name] = (g7[q], d7[q], nm7[q], nv7[q])

    order = ("w_in", "conv_a_w", "w_out_a", "conv_b_w", "conv_b_bias", "ln_b_gamma", "ln_b_beta", "w_out_b", "w_o",
             "ln1_gamma", "ln1_beta", "w_up", "w_down", "ln2_gamma", "ln2_beta")
    outs = [loss, grad_x.reshape(x.shape)]
    for part in range(4):
        outs += [big[name][part] for name in order]
    return tuple(outs)
```

```python
import jax
import jax.numpy as jnp
from jax import lax
from jax.experimental import pallas as pl
from jax.experimental.pallas import tpu as pltpu

F32 = jnp.float32
BF = jnp.bfloat16
SDS = jax.ShapeDtypeStruct
MESH = pl.DeviceIdType.MESH

ALPHA = 2.0 ** 0.25
LN_EPS = 1e-5
K_A = 3
K_B = 31
HALO = 32
CONV_TILE = 512
CONV_ROWS = 64
CONV_LANES = 128
VMEM_LIMIT_MB = 56
PROJ_TILE = 1024
PROJ_REST_TILE = 2048
MIXER_TILE = 512
GRAD_X_TILE = 512
LOCAL_TILE = 256
MIXER_BWD_TILE = 512
FF_CHUNK = 1024
GRAD_W_TILE = 1024
GRAD_W_TOKENS = 2048
GRAD_W_TOKENS_MLP = 4096
SUM_ROWS = 512
ROWS_TILE = 256
ADAM_LR = 0.001
ADAM_B1 = 0.9
ADAM_B2 = 0.999
ADAM_EPS = 1e-08
ADAM_WD = 0.01
ADAM_STEP = 10
P_DT = BF
CHIP_RELS = ((1, 0), (0, 1), (1, 1))
DEV_RELS = tuple((fx, fy, fc) for fx in (0, 1) for fy in (0, 1) for fc in (0, 1))[1:]


def _cp(sem=None, side_effects=False):
    return pltpu.CompilerParams(dimension_semantics=sem, vmem_limit_bytes=VMEM_LIMIT_MB << 20,
                                has_side_effects=side_effects)


def _const(shape):
    return pl.BlockSpec(shape, lambda *_: (0,) * len(shape), pipeline_mode=pl.Buffered(1))


def _sig(v):
    return jax.nn.sigmoid(v)


def _ln_fwd(r):
    mu = jnp.mean(r, axis=-1, keepdims=True)
    xc = r - mu
    var = jnp.mean(xc * xc, axis=-1, keepdims=True)
    rstd = lax.rsqrt(var + LN_EPS)
    return xc * rstd, rstd


def _ln_bwd(dn, n, rstd):
    m1 = jnp.mean(dn, axis=-1, keepdims=True)
    m2 = jnp.mean(dn * n, axis=-1, keepdims=True)
    return rstd * (dn - m1 - n * m2)


def _dot(a, b):
    return jnp.dot(a, b, preferred_element_type=F32)


def _dot_nt(a, b):
    return lax.dot_general(a, b, (((1,), (1,)), ((), ())), preferred_element_type=F32)


def _dot_tn(a, b):
    return lax.dot_general(a, b, (((0,), (0,)), ((), ())), preferred_element_type=F32)


def _tile(n, pref):
    if n <= pref:
        return n
    return max(t for t in range(128, pref + 1, 128) if n % t == 0)


def _rowsum8(v):
    acc = v[0:8]
    for g in range(1, v.shape[0] // 8):
        acc = acc + v[8 * g:8 * g + 8]
    return acc


def _taps(win, offsets, rows):
    r_all = win.shape[0]
    by_res = {}
    for k, o in enumerate(offsets):
        by_res.setdefault(o % 8, []).append((k, o // 8))
    for s, taps in sorted(by_res.items()):
        r = win if s == 0 else pltpu.roll(win, r_all - s, 0)
        for k, q in taps:
            yield k, r[8 * q:8 * q + rows]


CAUSAL_A = [HALO - (K_A - 1) + k for k in range(K_A)]
CAUSAL_B = [HALO - (K_B - 1) + k for k in range(K_B)]
ANTI_A = [K_A - 1 - k for k in range(K_A)]
ANTI_B = [K_B - 1 - k for k in range(K_B)]


def _host_call(body, *, name, grid, in_specs, out_specs, out_shape, scratch_shapes, args, hosted, prefetch=None,
               aliases=None, body_gets_stage_refs=False):
    n_in, n_out, n_scr = len(in_specs), len(out_specs), len(scratch_shapes)
    n_steps = 1
    for size in grid:
        n_steps *= size
    if isinstance(hosted, _Stage):
        hosted = [(hosted, 0, n_steps - 1)]
    n_pre = 0 if prefetch is None else 1
    pre = () if prefetch is None else (prefetch,)
    sem = ("arbitrary",) * len(grid)
    own_aliases = {n_pre + a: b for a, b in (aliases or {}).items()}

    def call(kernel_body, ins, outs, shapes, scratch, all_aliases, side_effects, operands):
        gs = pltpu.PrefetchScalarGridSpec(num_scalar_prefetch=n_pre, grid=grid, in_specs=ins, out_specs=outs,
                                          scratch_shapes=scratch)
        return pl.pallas_call(kernel_body, name=name, grid_spec=gs, out_shape=shapes,
                              input_output_aliases=all_aliases,
                              compiler_params=_cp(sem, side_effects=side_effects))(*pre, *operands)

    if hosted is None:
        res = call(body, list(in_specs), list(out_specs), list(out_shape), list(scratch_shapes), own_aliases, False,
                   args)
        return list(res), []
    stages = [st for st, _, _ in hosted]
    h_ins = [a for st in stages for a in st.ins]
    h_outs = [o for st in stages for o in st.outs]
    borrowed = {hi: k for hi, a in enumerate(h_ins) for k, own in enumerate(args) if a is own}
    passed = [hi for hi in range(len(h_ins)) if hi not in borrowed]
    h_in, h_out, n_sems = len(passed), len(h_outs), sum(st.n_sems for st in stages)

    def full_body(*refs):
        pre_refs, refs = refs[:n_pre], refs[n_pre:]
        ins, refs = refs[:n_in], refs[n_in:]
        hins = [None] * len(h_ins)
        for hi, ref in zip(passed, refs[:h_in]):
            hins[hi] = ref
        refs = refs[h_in:]
        outs, refs = refs[:n_out], refs[n_out:]
        houts, refs = refs[:h_out], refs[h_out:]
        scr, (send_sems, recv_sems) = refs[:n_scr], refs[n_scr:]
        step = 0
        for a, size in enumerate(grid):
            step = step * size + pl.program_id(a)
        phases, i0, o0, k0 = [], 0, 0, 0
        for st, starts, finishes in hosted:
            plans = st.make_plan(hins[i0:i0 + len(st.ins)], houts[o0:o0 + len(st.outs)],
                                 _SemsFrom(send_sems, k0), _SemsFrom(recv_sems, k0))
            if not isinstance(plans, list):
                plans, starts, finishes = [plans], [starts], [finishes]
            phases += list(zip(plans, starts, finishes))
            i0, o0, k0 = i0 + len(st.ins), o0 + len(st.outs), k0 + st.n_sems
        for plan, at, _ in phases:
            @pl.when(step == at)
            def _(plan=plan):
                _start(plan)

        if body_gets_stage_refs:
            body(*pre_refs, *ins, *outs, *scr, houts)
        else:
            body(*pre_refs, *ins, *outs, *scr)

        for plan, _, at in phases:
            @pl.when(step == at)
            def _(plan=plan):
                _finish(plan)

    all_aliases = dict(own_aliases)
    i0 = o0 = 0
    for st in stages:
        for a, b in st.aliases.items():
            hi = i0 + a
            operand = borrowed[hi] if hi in borrowed else n_in + passed.index(hi)
            all_aliases[n_pre + operand] = n_out + o0 + b
        i0, o0 = i0 + len(st.ins), o0 + len(st.outs)
    res = call(full_body, list(in_specs) + [ANY] * h_in, list(out_specs) + [ANY] * h_out,
               list(out_shape) + h_outs,
               list(scratch_shapes) + [pltpu.SemaphoreType.DMA((n_sems,)), pltpu.SemaphoreType.DMA((n_sems,))],
               all_aliases, True, (*args, *[h_ins[hi] for hi in passed]))
    return list(res[:n_out]), list(res[n_out:])


def _in_proj_own(x, w_shard, others, other_axes, pos, near):
    s, d = x.shape
    tn = w_shard.shape[1]
    tm = min(s // 4, PROJ_TILE)
    n_o = len(others)
    last = s // tm - 1
    hosted = [(near, [0, last], [last - 1, last])]

    def body(pos_ref, x_ref, w_ref, *refs):
        o_in, (p_ref, xb_ref), o_out, wb = refs[:n_o], refs[n_o:n_o + 2], refs[n_o + 2:2 * n_o + 2], refs[-1]

        @pl.when(pl.program_id(0) == 0)
        def _():
            wb[...] = w_ref[...].astype(BF)
            for src, dst in zip(o_in, o_out):
                dst[...] = src[...].astype(BF)

        xb = x_ref[...].astype(BF)
        xb_ref[...] = xb
        p_ref[...] = _dot(xb, wb[...]).astype(p_ref.dtype)

    whole = lambda a: pl.BlockSpec(a.shape, lambda i, pos: (0, 0), pipeline_mode=pl.Buffered(1))
    placed = lambda a, ax: pl.BlockSpec(a.shape, (lambda i, pos: (0, pos[0])) if ax == 1 else (lambda i, pos: (pos[0], 0)))
    full = lambda a, ax: SDS((a.shape[0], 4 * a.shape[1]) if ax == 1 else (4 * a.shape[0], a.shape[1]), BF)
    outs, extra = _host_call(
        body, name="in_proj_own", grid=(s // tm,), prefetch=pos,
        in_specs=[pl.BlockSpec((tm, d), lambda i, pos: (i, 0)), whole(w_shard)] + [whole(a) for a in others],
        out_specs=[pl.BlockSpec((tm, tn), lambda i, pos: (i, pos[0])), pl.BlockSpec((tm, d), lambda i, pos: (i, 0))]
        + [placed(a, ax) for a, ax in zip(others, other_axes)],
        out_shape=[SDS((s, 4 * tn), P_DT), SDS((s, d), BF)] + [full(a, ax) for a, ax in zip(others, other_axes)],
        scratch_shapes=[pltpu.VMEM((d, tn), BF)], args=(x, w_shard, *others), hosted=hosted)
    return outs, extra


def _in_proj_rest(xb, wi, p, pos, diagonal, other):
    s, d = xb.shape
    n = wi.shape[1]
    tm, tn = min(s // 4, PROJ_REST_TILE), n // 4
    ni = s // tm

    def body(pos_ref, xb_ref, w_ref, p_in, p_ref, wdiag, dsem, stage_outs):
        j = pl.program_id(0)
        step = j * ni + pl.program_id(1)

        def diagonal_block(act):
            for chip in range(4):
                @pl.when(pos_ref[0] == chip)
                def _(chip=chip):
                    act(pltpu.make_async_copy(stage_outs[0].at[:, pl.ds((chip ^ 3) * tn, tn)], wdiag, dsem))

        @pl.when(step == 2 * ni - 1)
        def _():
            diagonal_block(lambda cp: cp.start())

        @pl.when(step == 2 * ni)
        def _():
            diagonal_block(lambda cp: cp.wait())

        @pl.when(j < 2)
        def _():
            p_ref[...] = _dot(xb_ref[...], w_ref[...]).astype(p_ref.dtype)

        @pl.when(j == 2)
        def _():
            p_ref[...] = _dot(xb_ref[...], wdiag[...]).astype(p_ref.dtype)

    def col(j, i, pos):
        return lax.bitwise_xor(pos[0], jnp.where(j == 0, 2, jnp.where(j == 1, 1, 3)))

    def piped_col(j, i, pos):
        return lax.bitwise_xor(pos[0], jnp.where(j == 0, 2, 1))

    (p,), extra = _host_call(
        body, name="in_proj_rest", grid=(3, ni), prefetch=pos, aliases={2: 0}, body_gets_stage_refs=True,
        in_specs=[pl.BlockSpec((tm, d), lambda j, i, pos: (i, 0)),
                  pl.BlockSpec((d, tn), lambda j, i, pos: (0, piped_col(j, i, pos))), ANY],
        out_specs=[pl.BlockSpec((tm, tn), lambda j, i, pos: (i, col(j, i, pos)))],
        out_shape=[SDS((s, n), P_DT)], scratch_shapes=[pltpu.VMEM((d, tn), BF), pltpu.SemaphoreType.DMA(())],
        args=(xb, wi, p),
        hosted=[(diagonal, [0, ni + 1], [ni, 2 * ni - 2]), (other, 0, 3 * ni - 1)])
    n_d = len(diagonal.outs)
    return p, extra[:n_d], extra[n_d:]


def _col_spec(tm, d, k):
    return pl.BlockSpec((tm, d), lambda i, k=k: (i, k))


def _prev_halo_spec(tm, d, k):
    r = tm // HALO
    return pl.BlockSpec((HALO, d), lambda i, k=k: (jnp.maximum(i * r - 1, 0), k))


def _next_halo_spec(tm, d, k, s):
    r = tm // HALO
    last = s // HALO - 1
    return pl.BlockSpec((HALO, d), lambda i, k=k: (jnp.minimum((i + 1) * r, last), k))


def _conv_fwd(p, cwa, cwb, bias_b, lbg, lbb, d, hosted=None):
    s = p.shape[0]
    tm = min(s, CONV_TILE)
    nt = s // tm

    def body(ba_ref, ca_ref, va_ref, vb_ref, gb_ref, hca_ref, hva_ref, hvb_ref, hgb_ref,
             cwa_ref, cwb_ref, bias_ref, lbg_ref, lbb_ref,
             conva_ref, yapre_ref, nb_ref, rstdb_ref, u3_ref,
             zbuf, ubuf, u1buf):
        i = pl.program_id(0)
        keep = (i > 0).astype(F32)
        zbuf[pl.ds(0, HALO), :] = hca_ref[...].astype(F32) * hva_ref[...].astype(F32) * keep
        ubuf[pl.ds(0, HALO), :] = hvb_ref[...].astype(F32) * _sig(hgb_ref[...].astype(F32)) * keep
        zbuf[pl.ds(HALO, tm), :] = ca_ref[...].astype(F32) * va_ref[...].astype(F32)
        ubuf[pl.ds(HALO, tm), :] = vb_ref[...].astype(F32) * _sig(gb_ref[...].astype(F32))

        def chunk(j, carry):
            r0 = pl.multiple_of(j * CONV_ROWS, CONV_ROWS)
            rows = pl.ds(r0, CONV_ROWS)
            for lc in range(d // CONV_LANES):
                ls = pl.ds(lc * CONV_LANES, CONV_LANES)
                acc = jnp.zeros((CONV_ROWS, CONV_LANES), F32)
                for k, sl in _taps(zbuf[pl.ds(r0, CONV_ROWS + HALO), ls], CAUSAL_A, CONV_ROWS):
                    acc = acc + cwa_ref[pl.ds(k, 1), ls] * sl
                conva_ref[rows, ls] = acc.astype(BF)
                yapre_ref[rows, ls] = (ba_ref[rows, ls].astype(F32) * acc).astype(BF)
                acc = jnp.zeros((CONV_ROWS, CONV_LANES), F32)
                for k, sl in _taps(ubuf[pl.ds(r0, CONV_ROWS + HALO), ls], CAUSAL_B, CONV_ROWS):
                    acc = acc + cwb_ref[pl.ds(k, 1), ls] * sl
                u1buf[rows, ls] = acc + bias_ref[:, ls]
            return carry

        lax.fori_loop(0, tm // CONV_ROWS, chunk, 0)
        nb, rstd = _ln_fwd(u1buf[...])
        nb_ref[...] = nb
        rstdb_ref[...] = rstd
        u2 = nb * lbg_ref[...] + lbb_ref[...]
        u3_ref[...] = (u2 * _sig(u2)).astype(BF)

    vec = _const((1, d))
    return _host_call(
        body, name="conv_fwd", grid=(nt,),
        in_specs=[_col_spec(tm, d, k) for k in range(5)] + [_prev_halo_spec(tm, d, k) for k in (1, 2, 3, 4)]
        + [_const((K_A, d)), _const((K_B, d)), vec, vec, vec],
        out_specs=[pl.BlockSpec((tm, d), lambda i: (i, 0)), pl.BlockSpec((tm, d), lambda i: (i, 0)),
                   pl.BlockSpec((tm, d), lambda i: (i, 0)), pl.BlockSpec((tm, 1), lambda i: (i, 0)),
                   pl.BlockSpec((tm, d), lambda i: (i, 0))],
        out_shape=[SDS((s, d), BF), SDS((s, d), BF), SDS((s, d), F32), SDS((s, 1), F32), SDS((s, d), BF)],
        scratch_shapes=[pltpu.VMEM((HALO + tm, d), F32), pltpu.VMEM((HALO + tm, d), F32), pltpu.VMEM((tm, d), F32)],
        args=(p, p, p, p, p, p, p, p, p, cwa, cwb, bias_b, lbg, lbb), hosted=hosted)


def _mixer_out(yapre, u3, p, x, woa, wob, wo, d, hosted=None):
    s = x.shape[0]
    tm = min(s, MIXER_TILE)

    def body(yapre_ref, u3_ref, ga_ref, gb_ref, x_ref, woa_ref, wob_ref, wo_ref,
             ya_ref, yb_ref, merged_ref, n1_ref, rstd1_ref):
        ya = _dot(yapre_ref[...], woa_ref[...])
        yb = _dot(u3_ref[...], wob_ref[...])
        ya_ref[...] = ya.astype(BF)
        yb_ref[...] = yb.astype(BF)
        merged = (_sig(ga_ref[...].astype(F32)) * ya + _sig(gb_ref[...].astype(F32)) * yb).astype(BF)
        merged_ref[...] = merged
        r1 = F32(ALPHA) * x_ref[...] + _dot(merged, wo_ref[...])
        n1, rstd1 = _ln_fwd(r1)
        n1_ref[...] = n1
        rstd1_ref[...] = rstd1

    row = pl.BlockSpec((tm, d), lambda i: (i, 0))
    return _host_call(
        body, name="mixer_out", grid=(s // tm,),
        in_specs=[row, row, _col_spec(tm, d, 5), _col_spec(tm, d, 6), row,
                  _const((d, d)), _const((d, d)), _const((d, d))],
        out_specs=[row, row, row, row, pl.BlockSpec((tm, 1), lambda i: (i, 0))],
        out_shape=[SDS((s, d), BF), SDS((s, d), BF), SDS((s, d), BF), SDS((s, d), F32), SDS((s, 1), F32)],
        scratch_shapes=[], args=(yapre, u3, p, p, x, woa, wob, wo), hosted=hosted)


def _mlp(n1, rstd1, tgt, wup, wdown, l1g, l1b, l2g, l2b):
    s, d = n1.shape
    dff = wup.shape[1]
    tm = min(s, LOCAL_TILE)
    fc = min(dff, FF_CHUNK)
    nq = dff // fc

    def body(n1_ref, rstd1_ref, tgt_ref, wup_ref, wdown_ref, l1g_ref, l1b_ref, l2g_ref, l2b_ref,
             x1b_ref, hb_ref, dhpre_ref, dr2b_ref, dr1_ref, dr1b_ref, acc_ref, rbuf):
        i = pl.program_id(0)
        n1v = n1_ref[...]
        x1 = n1v * l1g_ref[...] + l1b_ref[...]
        x1b = x1.astype(BF)
        x1b_ref[...] = x1b
        ff = jnp.zeros((tm, d), F32)
        for q in range(nq):
            cs = pl.ds(q * fc, fc)
            r = jnp.maximum(_dot(x1b, wup_ref[:, cs]), 0.0)
            rbuf[:, cs] = r
            hq = (r * r).astype(BF)
            hb_ref[:, cs] = hq
            ff = ff + _dot(hq, wdown_ref[cs, :])
        n2, rstd2 = _ln_fwd(F32(ALPHA) * x1 + ff)
        x2 = n2 * l2g_ref[...] + l2b_ref[...]
        err = x2 - tgt_ref[...]
        dx2 = err * F32(1.0 / d)
        dr2 = _ln_bwd(dx2 * l2g_ref[...], n2, rstd2)
        dr2b = dr2.astype(BF)
        dr2b_ref[...] = dr2b
        dx1 = F32(ALPHA) * dr2
        for q in range(nq):
            cs = pl.ds(q * fc, fc)
            dh = _dot_nt(dr2b, wdown_ref[cs, :])
            dhp = (dh * (2.0 * rbuf[:, cs])).astype(BF)
            dhpre_ref[:, cs] = dhp
            dx1 = dx1 + _dot_nt(dhp, wup_ref[:, cs])
        dr1 = _ln_bwd(dx1 * l1g_ref[...], n1v, rstd1_ref[...])
        dr1_ref[...] = dr1
        dr1b_ref[...] = dr1.astype(BF)

        @pl.when(i == 0)
        def _():
            acc_ref[...] = jnp.zeros_like(acc_ref)

        for q, val in enumerate((err * err, dx2 * n2, dx2, dx1 * n1v, dx1)):
            acc_ref[pl.ds(q, 1), :] += jnp.sum(val, axis=0, keepdims=True)

    row = pl.BlockSpec((tm, d), lambda i: (i, 0))
    wide = pl.BlockSpec((tm, dff), lambda i: (i, 0))
    vec = _const((1, d))
    return pl.pallas_call(
        body, name="mlp_fwd_bwd", grid=(s // tm,),
        in_specs=[row, pl.BlockSpec((tm, 1), lambda i: (i, 0)), row, _const((d, dff)), _const((dff, d)),
                  vec, vec, vec, vec],
        out_specs=[row, wide, wide, row, row, row, pl.BlockSpec((8, d), lambda i: (0, 0))],
        out_shape=[SDS((s, d), BF), SDS((s, dff), BF), SDS((s, dff), BF), SDS((s, d), BF), SDS((s, d), F32),
                   SDS((s, d), BF), SDS((8, d), F32)],
        scratch_shapes=[pltpu.VMEM((tm, dff), F32)],
        compiler_params=_cp(("arbitrary",)),
    )(n1, rstd1, tgt, wup, wdown, l1g, l1b, l2g, l2b)


def _mixer_bwd_local(dr1b, p, ya, yb, conva, nb, rstdb, wo, woa, wob, lbg, lbb):
    s, d = ya.shape
    tm = min(s, MIXER_BWD_TILE)

    def body(dr1b_ref, ba_ref, ga_ref, gb_ref, ya_ref, yb_ref, conva_ref, nb_ref, rstdb_ref,
             wo_ref, woa_ref, wob_ref, lbg_ref, lbb_ref,
             dya_ref, dyb_ref, dba_ref, dga_ref, dgb_ref, dca_ref, du1_ref, acc_ref):
        i = pl.program_id(0)
        dmerged = _dot_nt(dr1b_ref[...], wo_ref[...])
        sa = _sig(ga_ref[...].astype(F32))
        sb = _sig(gb_ref[...].astype(F32))
        dya_f = dmerged * sa
        dyb_f = dmerged * sb
        dya = dya_f.astype(BF)
        dyb = dyb_f.astype(BF)
        dya_ref[...] = dya
        dyb_ref[...] = dyb
        dga_ref[...] = (dya_f * ya_ref[...].astype(F32) * (1.0 - sa)).astype(BF)
        dgb_ref[...] = (dyb_f * yb_ref[...].astype(F32) * (1.0 - sb)).astype(BF)
        dyapre = _dot_nt(dya, woa_ref[...])
        dba_ref[...] = (dyapre * conva_ref[...].astype(F32)).astype(BF)
        dca_ref[...] = (dyapre * ba_ref[...].astype(F32)).astype(BF)
        du3 = _dot_nt(dyb, wob_ref[...])
        nbv = nb_ref[...]
        u2 = nbv * lbg_ref[...] + lbb_ref[...]
        sg = _sig(u2)
        du2 = du3 * (sg * (1.0 + u2 * (1.0 - sg)))
        du1 = _ln_bwd(du2 * lbg_ref[...], nbv, rstdb_ref[...])
        du1_ref[...] = du1.astype(BF)

        @pl.when(i == 0)
        def _():
            acc_ref[...] = jnp.zeros_like(acc_ref)

        for q, val in enumerate((du2 * nbv, du2, du1)):
            acc_ref[pl.ds(q, 1), :] += jnp.sum(val, axis=0, keepdims=True)

    row = pl.BlockSpec((tm, d), lambda i: (i, 0))
    vec = _const((1, d))
    return pl.pallas_call(
        body, name="mixer_bwd_local", grid=(s // tm,),
        in_specs=[row, _col_spec(tm, d, 0), _col_spec(tm, d, 5), _col_spec(tm, d, 6), row, row, row, row,
                  pl.BlockSpec((tm, 1), lambda i: (i, 0)), _const((d, d)), _const((d, d)), _const((d, d)), vec, vec],
        out_specs=[row, row, row, row, row, row, row, pl.BlockSpec((8, d), lambda i: (0, 0))],
        out_shape=[SDS((s, d), BF)] * 7 + [SDS((8, d), F32)],
        compiler_params=_cp(("arbitrary",)),
    )(dr1b, p, p, p, ya, yb, conva, nb, rstdb, wo, woa, wob, lbg, lbb)


def _conv_bwd(dca, du1, p, dba, dga, dgb, cwa, cwb, d, hosted=None):
    s = dca.shape[0]
    tm = min(s, CONV_TILE)
    nt = s // tm

    def body(dca_ref, du1_ref, ndca_ref, ndu1_ref, ca_ref, va_ref, vb_ref, gb_ref,
             dba_ref, dga_ref, dgb_ref, cwa_ref, cwb_ref,
             dp_ref, dcwa_ref, dcwb_ref,
             dcabuf, du1buf, sgbuf, acca, accb):
        i = pl.program_id(0)
        keep_next = (i < nt - 1).astype(F32)

        @pl.when(i == 0)
        def _():
            acca[...] = jnp.zeros_like(acca)
            accb[...] = jnp.zeros_like(accb)

        sgbuf[...] = _sig(gb_ref[...].astype(F32))
        dcabuf[pl.ds(0, tm), :] = dca_ref[...].astype(F32)
        dcabuf[pl.ds(tm, HALO), :] = ndca_ref[...].astype(F32) * keep_next
        du1buf[pl.ds(0, tm), :] = du1_ref[...].astype(F32)
        du1buf[pl.ds(tm, HALO), :] = ndu1_ref[...].astype(F32) * keep_next
        dp_ref[:, pl.ds(0, d)] = dba_ref[...]
        dp_ref[:, pl.ds(5 * d, d)] = dga_ref[...]
        dp_ref[:, pl.ds(6 * d, d)] = dgb_ref[...]

        def chunk(j, carry):
            r0 = pl.multiple_of(j * CONV_ROWS, CONV_ROWS)
            rows = pl.ds(r0, CONV_ROWS)
            for lc in range(d // CONV_LANES):
                lo = lc * CONV_LANES
                ls = pl.ds(lo, CONV_LANES)
                cac = ca_ref[rows, ls].astype(F32)
                vac = va_ref[rows, ls].astype(F32)
                zc = cac * vac
                acc = jnp.zeros((CONV_ROWS, CONV_LANES), F32)
                for k, sl in _taps(dcabuf[pl.ds(r0, CONV_ROWS + HALO), ls], ANTI_A, CONV_ROWS):
                    acc = acc + cwa_ref[pl.ds(k, 1), ls] * sl
                    acca[pl.ds(8 * k, 8), ls] += _rowsum8(sl * zc)
                dp_ref[rows, pl.ds(d + lo, CONV_LANES)] = (acc * vac).astype(BF)
                dp_ref[rows, pl.ds(2 * d + lo, CONV_LANES)] = (acc * cac).astype(BF)
                sgc = sgbuf[rows, ls]
                vbc = vb_ref[rows, ls].astype(F32)
                uc = vbc * sgc
                acc = jnp.zeros((CONV_ROWS, CONV_LANES), F32)
                for k, sl in _taps(du1buf[pl.ds(r0, CONV_ROWS + HALO), ls], ANTI_B, CONV_ROWS):
                    acc = acc + cwb_ref[pl.ds(k, 1), ls] * sl
                    accb[pl.ds(8 * k, 8), ls] += _rowsum8(sl * uc)
                dp_ref[rows, pl.ds(3 * d + lo, CONV_LANES)] = (acc * sgc).astype(BF)
                dp_ref[rows, pl.ds(4 * d + lo, CONV_LANES)] = (acc * vbc * (sgc * (1.0 - sgc))).astype(BF)
            return carry

        lax.fori_loop(0, tm // CONV_ROWS, chunk, 0)

        @pl.when(i == nt - 1)
        def _():
            dcwa_ref[...] = jnp.zeros_like(dcwa_ref)
            dcwb_ref[...] = jnp.zeros_like(dcwb_ref)
            for k in range(K_A):
                dcwa_ref[pl.ds(k, 1), :] = jnp.sum(acca[pl.ds(8 * k, 8), :], axis=0, keepdims=True)
            for k in range(K_B):
                dcwb_ref[pl.ds(k, 1), :] = jnp.sum(accb[pl.ds(8 * k, 8), :], axis=0, keepdims=True)

    row = pl.BlockSpec((tm, d), lambda i: (i, 0))
    nxt = _next_halo_spec(tm, d, 0, s)
    return _host_call(
        body, name="conv_bwd", grid=(nt,),
        in_specs=[row, row, nxt, nxt] + [_col_spec(tm, d, k) for k in (1, 2, 3, 4)]
        + [row, row, row, _const((K_A, d)), _const((K_B, d))],
        out_specs=[pl.BlockSpec((tm, 7 * d), lambda i: (i, 0)), pl.BlockSpec((8, d), lambda i: (0, 0)),
                   pl.BlockSpec((32, d), lambda i: (0, 0))],
        out_shape=[SDS((s, 7 * d), BF), SDS((8, d), F32), SDS((32, d), F32)],
        scratch_shapes=[pltpu.VMEM((tm + HALO, d), F32), pltpu.VMEM((tm + HALO, d), F32),
                        pltpu.VMEM((tm, d), F32), pltpu.VMEM((8 * K_A, d), F32), pltpu.VMEM((8 * K_B, d), F32)],
        args=(dca, du1, dca, du1, p, p, p, p, dba, dga, dgb, cwa, cwb), hosted=hosted)


def _grad_w(a, b, name, hosted=None, tokens=None):
    s, m = a.shape
    n = b.shape[1]
    tm, tn, tk = _tile(m, GRAD_W_TILE), _tile(n, GRAD_W_TILE), _tile(s, tokens or GRAD_W_TOKENS)
    nk = s // tk

    def body(a_ref, b_ref, o_ref, ob_ref):
        k = pl.program_id(2)

        @pl.when(k == 0)
        def _():
            o_ref[...] = jnp.zeros_like(o_ref)

        o_ref[...] += _dot_tn(a_ref[...], b_ref[...])

        @pl.when(k == nk - 1)
        def _():
            ob_ref[...] = o_ref[...].astype(BF)

    blk = pl.BlockSpec((tm, tn), lambda i, j, k: (i, j))
    (g, gb), extra = _host_call(
        body, name=name, grid=(m // tm, n // tn, nk),
        in_specs=[pl.BlockSpec((tk, tm), lambda i, j, k: (k, i)), pl.BlockSpec((tk, tn), lambda i, j, k: (k, j))],
        out_specs=[blk, blk], out_shape=[SDS((m, n), F32), SDS((m, n), BF)], scratch_shapes=[], args=(a, b),
        hosted=hosted)
    return g, gb, extra


def _grad_x(dr1, dp, wi, hosted=None):
    s, d = dr1.shape
    n = wi.shape[1]
    tm = min(s, GRAD_X_TILE)

    def body(dr1_ref, dp_ref, w_ref, o_ref):
        o_ref[...] = F32(ALPHA) * dr1_ref[...] + _dot_nt(dp_ref[...], w_ref[...])

    (gx,), extra = _host_call(
        body, name="grad_x", grid=(s // tm,),
        in_specs=[pl.BlockSpec((tm, d), lambda i: (i, 0)), pl.BlockSpec((tm, n), lambda i: (i, 0)), _const((d, n))],
        out_specs=[pl.BlockSpec((tm, d), lambda i: (i, 0))],
        out_shape=[SDS((s, d), F32)], scratch_shapes=[], args=(dr1, dp, wi), hosted=hosted)
    return gx, extra


def _adamw_math(w, g, m, v):
    m2 = ADAM_B1 * m + (1.0 - ADAM_B1) * g
    v2 = ADAM_B2 * v + (1.0 - ADAM_B2) * (g * g)
    m_hat = m2 / (1.0 - ADAM_B1 ** ADAM_STEP)
    v_hat = v2 / (1.0 - ADAM_B2 ** ADAM_STEP)
    delta = -ADAM_LR * (m_hat / (jnp.sqrt(v_hat) + ADAM_EPS) + ADAM_WD * w)
    return delta, m2, v2


def _adamw(w, g, m, v, name):
    r, c = w.shape
    tr = min(r, ROWS_TILE)

    def body(w_ref, g_ref, m_ref, v_ref, g_out, d_ref, m2_ref, v2_ref):
        gv = g_ref[...]
        delta, m2, v2 = _adamw_math(w_ref[...], gv, m_ref[...], v_ref[...])
        g_out[...] = gv
        d_ref[...] = delta
        m2_ref[...] = m2
        v2_ref[...] = v2

    blk = pl.BlockSpec((tr, c), lambda i: (i, 0))
    return pl.pallas_call(
        body, name=name, grid=(r // tr,), in_specs=[blk] * 4, out_specs=[blk] * 4,
        out_shape=[SDS((r, c), F32)] * 4, compiler_params=_cp(("parallel",)),
    )(w, g, m, v)


def _sum_parts(parts, name):
    k, r, c = parts.shape

    def body(p_ref, o_ref):
        acc = p_ref[0]
        for q in range(1, k):
            acc = acc + p_ref[q]
        o_ref[...] = acc

    return pl.pallas_call(
        body, name=name, grid=(1,),
        in_specs=[pl.BlockSpec((k, r, c), lambda i: (0, 0, 0))],
        out_specs=pl.BlockSpec((r, c), lambda i: (0, 0)),
        out_shape=SDS((r, c), F32), compiler_params=_cp(("arbitrary",)),
    )(parts)


def _piece_shape(full_shape, axis):
    r, c = full_shape
    return (r // 2, c // 4) if axis == 1 else (r // 8, c)


def _piece_spec(full_shape, axis, tr, chip_of, half_of):
    hr, wc = _piece_shape(full_shape, axis)
    nb = hr // tr
    if axis == 1:
        return pl.BlockSpec((tr, wc), lambda *a: (half_of(*a) * nb + a[-2], chip_of(*a)))
    return pl.BlockSpec((tr, wc), lambda *a: ((2 * chip_of(*a) + half_of(*a)) * nb + a[-2], 0))


def _place_cast(w, axis, pos, name):
    r, c = w.shape
    tr = min(r, ROWS_TILE)
    nb = r // tr
    full = (r, 4 * c) if axis == 1 else (4 * r, c)
    out_map = (lambda i, pos: (i, pos[0])) if axis == 1 else (lambda i, pos: (pos[0] * nb + i, 0))

    def body(pos_ref, w_ref, o_ref):
        o_ref[...] = w_ref[...].astype(o_ref.dtype)

    gs = pltpu.PrefetchScalarGridSpec(
        num_scalar_prefetch=1, grid=(nb,),
        in_specs=[pl.BlockSpec((tr, c), lambda i, pos: (i, 0))], out_specs=pl.BlockSpec((tr, c), out_map))
    return pl.pallas_call(body, name=name, grid_spec=gs, out_shape=SDS(full, BF),
                          compiler_params=_cp(("arbitrary",)))(pos, w)


def _pair_add(g, land, axis, pos, name):
    hr, wc = _piece_shape(g.shape, axis)
    tr = min(hr, SUM_ROWS)

    def body(pos_ref, g_ref, l_ref, o_ref):
        o_ref[0] = (g_ref[...] + l_ref[0].astype(F32)).astype(BF)

    other = lambda q, i, pos: (pos[0] + 1 + q) % 4
    blk = pl.BlockSpec((1, tr, wc), lambda q, i, pos: (other(q, i, pos), i, 0))
    gs = pltpu.PrefetchScalarGridSpec(
        num_scalar_prefetch=1, grid=(3, hr // tr),
        in_specs=[_piece_spec(g.shape, axis, tr, other, lambda q, i, pos: pos[1]), blk], out_specs=blk)
    return pl.pallas_call(body, name=name, grid_spec=gs, out_shape=SDS((4, hr, wc), BF),
                          compiler_params=_cp(("arbitrary", "arbitrary")))(pos, g, land)


def _chip_sum(g, land1, land2, axis, pos, name):
    hr, wc = _piece_shape(g.shape, axis)
    tr = min(hr, SUM_ROWS)
    nb = hr // tr

    def body(pos_ref, g_ref, l1_ref, l2_ref, o_ref):
        acc = g_ref[...] + l1_ref[0].astype(F32)
        for q in range(3):
            acc = acc + l2_ref[q].astype(F32)
        o_ref[...] = acc

    gs = pltpu.PrefetchScalarGridSpec(
        num_scalar_prefetch=1, grid=(nb,),
        in_specs=[_piece_spec(g.shape, axis, tr, lambda i, pos: pos[0], lambda i, pos: pos[1]),
                  pl.BlockSpec((1, tr, wc), lambda i, pos: (pos[0], i, 0)),
                  pl.BlockSpec((3, tr, wc), lambda i, pos: (0, i, 0))],
        out_specs=pl.BlockSpec((tr, wc), lambda i, pos: (pos[1] * nb + i, 0)))
    return pl.pallas_call(body, name=name, grid_spec=gs, out_shape=SDS((2 * hr, wc), F32),
                          compiler_params=_cp(("arbitrary",)))(pos, g, land1, land2)


ANY = pl.BlockSpec(memory_space=pl.ANY)
COMM = pltpu.CompilerParams(has_side_effects=True)


def _on_each_device(fn):
    x, y, c = lax.axis_index("x"), lax.axis_index("y"), lax.axis_index("c")
    for sx in (0, 1):
        for sy in (0, 1):
            for sc in (0, 1):
                @pl.when(jnp.logical_and(jnp.logical_and(x == sx, y == sy), c == sc))
                def _(sx=sx, sy=sy, sc=sc):
                    fn(sx, sy, sc)


def _remote(src, dst, send_sem, recv_sem, to):
    return pltpu.make_async_remote_copy(src_ref=src, dst_ref=dst, send_sem=send_sem, recv_sem=recv_sem,
                                        device_id=to, device_id_type=MESH)


def _piece_ref(ref, axis, j, h):
    r, c = ref.shape
    hr, wc = _piece_shape((r, c), axis)
    if axis == 1:
        return ref.at[pl.ds(h * hr, hr), pl.ds(j * wc, wc)]
    return ref.at[pl.ds((2 * j + h) * hr, hr), :]


class _Stage:
    def __init__(self, ins, outs, n_sems, make_plan, aliases=None):
        self.ins, self.outs, self.n_sems, self.make_plan = list(ins), list(outs), n_sems, make_plan
        self.aliases = dict(aliases or {})


def _start(plan):
    def dev(sx, sy, sc):
        for cp, _, _ in plan(sx, sy, sc):
            cp.start()

    _on_each_device(dev)


def _finish(plan):
    def dev(sx, sy, sc):
        for _, sent, got in plan(sx, sy, sc):
            sent.wait_send()
            got.wait_recv()

    _on_each_device(dev)


def _comm_call(stage, name):
    n_in, n_out = len(stage.ins), len(stage.outs)

    def body(*refs):
        plan = stage.make_plan(refs[:n_in], refs[n_in:n_in + n_out], *refs[n_in + n_out:])
        _start(plan)
        _finish(plan)

    return pl.pallas_call(
        body, name=name, in_specs=[ANY] * n_in, out_specs=[ANY] * n_out, out_shape=stage.outs,
        input_output_aliases=stage.aliases,
        scratch_shapes=[pltpu.SemaphoreType.DMA((stage.n_sems,)), pltpu.SemaphoreType.DMA((stage.n_sems,))],
        compiler_params=COMM,
    )(*stage.ins)


class _SemsFrom:
    def __init__(self, sems, base):
        self.sems, self.base = sems, base

    @property
    def at(self):
        return self

    def __getitem__(self, k):
        return self.sems.at[self.base + k]


def _both(a, b):
    na, nb = len(a.ins), len(b.ins)
    ma = len(a.outs)

    def make_plan(ins, outs, send_sems, recv_sems):
        pa = a.make_plan(ins[:na], outs[:ma], send_sems, recv_sems)
        pb = b.make_plan(ins[na:], outs[ma:], _SemsFrom(send_sems, a.n_sems), _SemsFrom(recv_sems, a.n_sems))
        return lambda sx, sy, sc: pa(sx, sy, sc) + pb(sx, sy, sc)

    aliases = dict(a.aliases)
    aliases.update({na + i: ma + o for i, o in b.aliases.items()})
    return _Stage(a.ins + b.ins, a.outs + b.outs, a.n_sems + b.n_sems, make_plan, aliases)


def _same(cp):
    return (cp, cp, cp)


def _stage_gather_send(fulls, axes):
    n = len(fulls)

    def make_plan(ins, outs, send_sems, recv_sems):
        def plan(sx, sy, sc):
            cps = []
            for w in range(n):
                mine = _piece_ref(outs[w], axes[w], 2 * sx + sy, sc)
                for r, (fx, fy) in enumerate(CHIP_RELS):
                    k = 3 * w + r
                    to = (sx ^ fx, sy ^ fy, sc)
                    got = _piece_ref(outs[w], axes[w], 2 * (sx ^ fx) + (sy ^ fy), sc)
                    send = _remote(mine, mine, send_sems.at[k], recv_sems.at[k], to)
                    cps.append((send, send, _remote(got, got, send_sems.at[k], recv_sems.at[k], to)))
            return cps

        return plan

    return _Stage(fulls, [SDS(f.shape, f.dtype) for f in fulls], 3 * n, make_plan, {i: i for i in range(n)})


def _stage_gather_forward(fulls, axes):
    n = len(fulls)

    def make_plan(ins, outs, send_sems, recv_sems):
        def plan(sx, sy, sc):
            cps = []
            sib = (sx, sy, 1 - sc)
            for w in range(n):
                for r, (fx, fy) in enumerate(CHIP_RELS):
                    k = 3 * w + r
                    pj = 2 * (sx ^ fx) + (sy ^ fy)
                    have = _piece_ref(outs[w], axes[w], pj, sc)
                    want = _piece_ref(outs[w], axes[w], pj, 1 - sc)
                    send = _remote(have, have, send_sems.at[k], recv_sems.at[k], sib)
                    cps.append((send, send, _remote(want, want, send_sems.at[k], recv_sems.at[k], sib)))
            return cps

        return plan

    return _Stage(fulls, [SDS(f.shape, f.dtype) for f in fulls], 3 * n, make_plan, {i: i for i in range(n)})


def _in_place(fulls, n_sems, make_plan):
    return _Stage(fulls, [SDS(f.shape, f.dtype) for f in fulls], n_sems, make_plan, {i: i for i in range(len(fulls))})


def _stage_gather_neighbours(fulls, axes):
    n = len(fulls)

    def make_plan(ins, outs, send_sems, recv_sems):
        def plan(sx, sy, sc):
            cps = []
            for w in range(n):
                mine = _piece_ref(outs[w], axes[w], 2 * sx + sy, sc)
                for r, (px, py) in enumerate(((sx ^ 1, sy), (sx, sy ^ 1))):
                    got = _piece_ref(outs[w], axes[w], 2 * px + py, sc)
                    send = _remote(mine, mine, send_sems.at[2 * w + r], recv_sems.at[2 * w + r], (px, py, sc))
                    cps.append((send, send, _remote(got, got, send_sems.at[2 * w + r], recv_sems.at[2 * w + r],
                                                    (px, py, sc))))
            return cps

        return plan

    return _in_place(fulls, 2 * n, make_plan)


def _stage_gather_near(fulls, axes):
    first, second = _stage_gather_neighbours(fulls, axes), _stage_gather_pair(fulls, axes)

    def make_plan(ins, outs, send_sems, recv_sems):
        return [first.make_plan(ins, outs, send_sems, recv_sems),
                second.make_plan(ins, outs, _SemsFrom(send_sems, first.n_sems), _SemsFrom(recv_sems, first.n_sems))]

    return _in_place(fulls, first.n_sems + second.n_sems, make_plan)


def _stage_gather_pair(fulls, axes):
    n = len(fulls)

    def make_plan(ins, outs, send_sems, recv_sems):
        def plan(sx, sy, sc):
            cps = []
            sib = (sx, sy, 1 - sc)
            for w in range(n):
                for r, j in enumerate((2 * (sx ^ 1) + sy, 2 * sx + (sy ^ 1))):
                    k = 2 * w + r
                    have, want = _piece_ref(outs[w], axes[w], j, sc), _piece_ref(outs[w], axes[w], j, 1 - sc)
                    send = _remote(have, have, send_sems.at[k], recv_sems.at[k], sib)
                    cps.append((send, send, _remote(want, want, send_sems.at[k], recv_sems.at[k], sib)))
            return cps

        return plan

    return _in_place(fulls, 2 * n, make_plan)


def _stage_gather_diagonal(fulls, axes):
    n = len(fulls)

    def make_plan(ins, outs, send_sems, recv_sems):
        def relay(sx, sy, sc):
            cps = []
            jx, jy, jd = 2 * (sx ^ 1) + sy, 2 * sx + (sy ^ 1), 2 * (sx ^ 1) + (sy ^ 1)
            passed, to = (jx, (sx, sy ^ 1, sc)) if sc == 0 else (jy, (sx ^ 1, sy, sc))
            for w in range(n):
                have, want = _piece_ref(outs[w], axes[w], passed, sc), _piece_ref(outs[w], axes[w], jd, sc)
                send = _remote(have, have, send_sems.at[w], recv_sems.at[w], to)
                cps.append((send, send, _remote(want, want, send_sems.at[w], recv_sems.at[w], to)))
            return cps

        def cross(sx, sy, sc):
            cps = []
            sib = (sx, sy, 1 - sc)
            jd = 2 * (sx ^ 1) + (sy ^ 1)
            for w in range(n):
                have, want = _piece_ref(outs[w], axes[w], jd, sc), _piece_ref(outs[w], axes[w], jd, 1 - sc)
                send = _remote(have, have, send_sems.at[n + w], recv_sems.at[n + w], sib)
                cps.append((send, send, _remote(want, want, send_sems.at[n + w], recv_sems.at[n + w], sib)))
            return cps

        return [relay, cross]

    return _in_place(fulls, 2 * n, make_plan)


def _stage_pair_exchange(grads, axes):
    n = len(grads)

    def make_plan(gs, land, send_sems, recv_sems):
        def plan(sx, sy, sc):
            return [_same(_remote(_piece_ref(gs[w], axes[w], jj, 1 - sc), land[w].at[jj], send_sems.at[4 * w + jj],
                                  recv_sems.at[4 * w + jj], (sx, sy, 1 - sc)))
                    for w in range(n) for jj in range(4)]

        return plan

    return _Stage(grads, [SDS((4,) + _piece_shape(g.shape, a), g.dtype) for g, a in zip(grads, axes)], 4 * n,
                  make_plan)


def _stage_chip_scatter(pieces):
    n = len(pieces)

    def make_plan(ps, land, send_sems, recv_sems):
        def plan(sx, sy, sc):
            return [_same(_remote(ps[w].at[2 * (sx ^ fx) + (sy ^ fy)], land[w].at[r], send_sems.at[3 * w + r],
                                  recv_sems.at[3 * w + r], (sx ^ fx, sy ^ fy, sc)))
                    for w in range(n) for r, (fx, fy) in enumerate(CHIP_RELS)]

        return plan

    return _Stage(pieces, [SDS((3,) + p.shape[1:], p.dtype) for p in pieces], 3 * n, make_plan)


def _stage_pair_share(shards):
    n = len(shards)

    def make_plan(ins, outs, send_sems, recv_sems):
        def plan(sx, sy, sc):
            cps = []
            sib = (sx, sy, 1 - sc)
            for w in range(n):
                hr = shards[w].shape[0] // 2
                mine = outs[w].at[pl.ds(sc * hr, hr), :]
                theirs = outs[w].at[pl.ds((1 - sc) * hr, hr), :]
                send = _remote(mine, mine, send_sems.at[w], recv_sems.at[w], sib)
                cps.append((send, send, _remote(theirs, theirs, send_sems.at[w], recv_sems.at[w], sib)))
            return cps

        return plan

    return _Stage(shards, [SDS(g.shape, g.dtype) for g in shards], n, make_plan, {i: i for i in range(n)})


def _stage_gather_small(stack):
    def make_plan(ins, outs, send_sems, recv_sems):
        def plan(sx, sy, sc):
            mine = outs[0].at[4 * sx + 2 * sy + sc]
            return [_same(_remote(mine, mine, send_sems.at[k], recv_sems.at[k], (sx ^ fx, sy ^ fy, sc ^ fc)))
                    for k, (fx, fy, fc) in enumerate(DEV_RELS)]

        return plan

    return _Stage([stack], [SDS(stack.shape, stack.dtype)], 7, make_plan, {0: 0})


def kernel(x, w_in, conv_a_w, w_out_a, conv_b_w, conv_b_bias, ln_b_gamma, ln_b_beta, w_out_b, w_o, ln1_gamma, ln1_beta, w_up, w_down, ln2_gamma, ln2_beta, loss_target, m_w_in, m_conv_a_w, m_w_out_a, m_conv_b_w, m_conv_b_bias, m_ln_b_gamma, m_ln_b_beta, m_w_out_b, m_w_o, m_ln1_gamma, m_ln1_beta, m_w_up, m_w_down, m_ln2_gamma, m_ln2_beta, v_w_in, v_conv_a_w, v_w_out_a, v_conv_b_w, v_conv_b_bias, v_ln_b_gamma, v_ln_b_beta, v_w_out_b, v_w_o, v_ln1_gamma, v_ln1_beta, v_w_up, v_w_down, v_ln2_gamma, v_ln2_beta):
    s, d = x.shape[1], x.shape[2]
    xs = x.reshape(s, d)
    tgt = loss_target.reshape(s, d)
    dq = d // 4
    chip = 2 * lax.axis_index("x") + lax.axis_index("y")
    core = lax.axis_index("c")
    pos = jnp.stack([chip, core]).astype(jnp.int32)
    names = ("w_in", "w_out_a", "w_out_b", "w_o", "w_up", "w_down")
    axes = (1, 0, 0, 0, 1, 0)

    conv_pack = jnp.concatenate([jnp.pad(conv_a_w, ((0, 8 - K_A), (0, 0))), jnp.pad(conv_b_w, ((0, 32 - K_B), (0, 0))),
                                 jnp.zeros((8, dq), F32)], axis=0)
    conv_full = lax.dynamic_update_slice(jnp.zeros((conv_pack.shape[0], d), F32), conv_pack, (0, chip * dq))
    wi_own = _place_cast(w_in, 1, pos, "place_w_in")
    vec = lambda a: a.reshape(1, d)
    bias_b, lbg, lbb = vec(conv_b_bias), vec(ln_b_gamma), vec(ln_b_beta)
    l1g, l1b, l2g, l2b = vec(ln1_gamma), vec(ln1_beta), vec(ln2_gamma), vec(ln2_beta)

    (p, xb, *placed), (wi, convs) = _in_proj_own(xs, w_in, (w_out_a, w_out_b, w_o, w_up, w_down), axes[1:], pos,
                                                 _stage_gather_near([wi_own, conv_full], (1, 1)))
    fulls = [wi_own] + placed
    p, (wi, convs), small3 = _in_proj_rest(xb, wi, p, pos, _stage_gather_diagonal([wi, convs], (1, 1)),
                                           _stage_gather_send(fulls[1:4], axes[1:4]))
    cwa, cwb = convs[0:K_A], convs[8:8 + K_B]
    (conva, yapre, nb, rstdb, u3), landed = _conv_fwd(
        p, cwa, cwb, bias_b, lbg, lbb, d,
        _both(_stage_gather_forward(small3, axes[1:4]), _stage_gather_send(fulls[4:6], axes[4:6])))
    woa, wob, wo = landed[:3]
    (ya, yb, merged, n1, rstd1), (wup, wdown) = _mixer_out(yapre, u3, p, xs, woa, wob, wo, d,
                                                           _stage_gather_forward(landed[3:], axes[4:6]))
    x1b, hb, dhpre, dr2b, dr1, dr1b, acc_mlp = _mlp(n1, rstd1, tgt, wup, wdown, l1g, l1b, l2g, l2b)
    g_up, gb_up, _ = _grad_w(x1b, dhpre, "grad_w_up", tokens=GRAD_W_TOKENS_MLP)
    g_down, gb_down, _ = _grad_w(hb, dr2b, "grad_w_down", tokens=GRAD_W_TOKENS_MLP)
    dya, dyb, dba, dga, dgb, dca, du1, acc_mix = _mixer_bwd_local(dr1b, p, ya, yb, conva, nb, rstdb, wo, woa, wob,
                                                                   lbg, lbb)
    g_oa, gb_oa, _ = _grad_w(yapre, dya, "grad_w_out_a")
    g_ob, gb_ob, _ = _grad_w(u3, dyb, "grad_w_out_b")
    g_o, gb_o, _ = _grad_w(merged, dr1b, "grad_w_o")

    early, e_axes, e_names = [g_oa, g_ob, g_o, g_up, g_down], axes[1:], names[1:]
    (dp, dcwa, dcwb), land1 = _conv_bwd(dca, du1, p, dba, dga, dgb, cwa, cwb, d,
                                        _stage_pair_exchange([gb_oa, gb_ob, gb_o, gb_up, gb_down], e_axes))
    pieces = [_pair_add(g, l, a, pos, "pair_add_" + nm) for g, l, a, nm in zip(early, land1, e_axes, e_names)]
    pack = jnp.concatenate([dcwa, dcwb, acc_mix, acc_mlp], axis=0)
    stack = lax.dynamic_update_slice(jnp.zeros((8,) + pack.shape, F32), pack[None], (2 * chip + core, 0, 0))
    g_wi, gb_wi, landed = _grad_w(xb, dp, "grad_w_in",
                                  _both(_stage_chip_scatter(pieces), _stage_gather_small(stack)))
    land2, stack = landed[:-1], landed[-1]
    halves = [_chip_sum(g, l1, l2, a, pos, "chip_sum_" + nm)
              for g, l1, l2, a, nm in zip(early, land1, land2, e_axes, e_names)]
    (land1_in,) = _comm_call(_stage_pair_exchange([gb_wi], (1,)), "pair_exchange_w_in")
    piece_in = _pair_add(g_wi, land1_in, 1, pos, "pair_add_w_in")
    grad_x, landed = _grad_x(dr1, dp, wi, _both(_stage_chip_scatter([piece_in]), _stage_pair_share(halves)))
    land2_in, (g_oa, g_ob, g_o, g_up, g_down) = landed[0], landed[1:]
    half_in = _chip_sum(g_wi, land1_in, land2_in, 1, pos, "chip_sum_w_in")
    (g_in,) = _comm_call(_stage_pair_share([half_in]), "pair_share_w_in")
    small = _sum_parts(stack, "small_sum")
    g_ca = lax.dynamic_slice(small, (0, chip * dq), (K_A, dq))
    g_cb = lax.dynamic_slice(small, (8, chip * dq), (K_B, dq))
    g_vec = jnp.stack([small[r] for r in (42, 40, 41, 51, 52, 49, 50)])

    loss = (0.5 / d) * jnp.sum(small[48])

    big = {}
    for name, w, g, m, v in (("w_in", w_in, g_in, m_w_in, v_w_in), ("w_out_a", w_out_a, g_oa, m_w_out_a, v_w_out_a),
                             ("w_out_b", w_out_b, g_ob, m_w_out_b, v_w_out_b), ("w_o", w_o, g_o, m_w_o, v_w_o),
                             ("w_up", w_up, g_up, m_w_up, v_w_up), ("w_down", w_down, g_down, m_w_down, v_w_down),
                             ("conv_a_w", conv_a_w, g_ca, m_conv_a_w, v_conv_a_w),
                             ("conv_b_w", conv_b_w, g_cb, m_conv_b_w, v_conv_b_w)):
        big[name] = tuple(_adamw(w, g, m, v, "adamw_" + name))
    vec_names = ("conv_b_bias", "ln_b_gamma", "ln_b_beta", "ln1_gamma", "ln1_beta", "ln2_gamma", "ln2_beta")
    w7 = jnp.stack([conv_b_bias, ln_b_gamma, ln_b_beta, ln1_gamma, ln1_beta, ln2_gamma, ln2_beta])
    m7 = jnp.stack([m_conv_b_bias, m_ln_b_gamma, m_ln_b_beta, m_ln1_gamma, m_ln1_beta, m_ln2_gamma, m_ln2_beta])
    v7 = jnp.stack([v_conv_b_bias, v_ln_b_gamma, v_ln_b_beta, v_ln1_gamma, v_ln1_beta, v_ln2_gamma, v_ln2_beta])
    g7, d7, nm7, nv7 = _adamw(w7, g_vec, m7, v7, "adamw_vectors")
    for q, name in enumerate(vec_names):
        big[name] = (g7[q], d7[q], nm7[q], nv7[q])

    order = ("w_in", "conv_a_w", "w_out_a", "conv_b_w", "conv_b_bias", "ln_b_gamma", "ln_b_beta", "w_out_b", "w_o",
             "ln1_gamma", "ln1_beta", "w_up", "w_down", "ln2_gamma", "ln2_beta")
    outs = [loss, grad_x.reshape(x.shape)]
    for part in range(4):
        outs += [big[name][part] for name in order]
    return tuple(outs)
```

```python
import jax
import jax.numpy as jnp
from jax import lax
from jax.experimental import pallas as pl
from jax.experimental.pallas import tpu as pltpu

F32 = jnp.float32
BF = jnp.bfloat16
SDS = jax.ShapeDtypeStruct
MESH = pl.DeviceIdType.MESH

ALPHA = 2.0 ** 0.25
LN_EPS = 1e-5
K_A = 3
K_B = 31
HALO = 32
CONV_TILE = 512
CONV_ROWS = 64
CONV_LANES = 128
VMEM_LIMIT_MB = 56
PROJ_TILE = 1024
PROJ_REST_TILE = 2048
MIXER_TILE = 512
GRAD_X_TILE = 512
LOCAL_TILE = 256
MIXER_BWD_TILE = 512
FF_CHUNK = 1024
GRAD_W_TILE = 1024
GRAD_W_TOKENS = 2048
GRAD_W_TOKENS_MLP = 4096
SUM_ROWS = 512
ROWS_TILE = 256
ADAM_LR = 0.001
ADAM_B1 = 0.9
ADAM_B2 = 0.999
ADAM_EPS = 1e-08
ADAM_WD = 0.01
ADAM_STEP = 10
P_DT = BF
CHIP_RELS = ((1, 0), (0, 1), (1, 1))
DEV_RELS = tuple((fx, fy, fc) for fx in (0, 1) for fy in (0, 1) for fc in (0, 1))[1:]


def _cp(sem=None, side_effects=False):
    return pltpu.CompilerParams(dimension_semantics=sem, vmem_limit_bytes=VMEM_LIMIT_MB << 20,
                                has_side_effects=side_effects)


def _const(shape):
    return pl.BlockSpec(shape, lambda *_: (0,) * len(shape), pipeline_mode=pl.Buffered(1))


def _sig(v):
    return jax.nn.sigmoid(v)


def _ln_fwd(r):
    mu = jnp.mean(r, axis=-1, keepdims=True)
    xc = r - mu
    var = jnp.mean(xc * xc, axis=-1, keepdims=True)
    rstd = lax.rsqrt(var + LN_EPS)
    return xc * rstd, rstd


def _ln_bwd(dn, n, rstd):
    m1 = jnp.mean(dn, axis=-1, keepdims=True)
    m2 = jnp.mean(dn * n, axis=-1, keepdims=True)
    return rstd * (dn - m1 - n * m2)


def _dot(a, b):
    return jnp.dot(a, b, preferred_element_type=F32)


def _dot_nt(a, b):
    return lax.dot_general(a, b, (((1,), (1,)), ((), ())), preferred_element_type=F32)


def _dot_tn(a, b):
    return lax.dot_general(a, b, (((0,), (0,)), ((), ())), preferred_element_type=F32)


def _tile(n, pref):
    if n <= pref:
        return n
    return max(t for t in range(128, pref + 1, 128) if n % t == 0)


def _rowsum8(v):
    acc = v[0:8]
    for g in range(1, v.shape[0] // 8):
        acc = acc + v[8 * g:8 * g + 8]
    return acc


def _taps(win, offsets, rows):
    r_all = win.shape[0]
    by_res = {}
    for k, o in enumerate(offsets):
        by_res.setdefault(o % 8, []).append((k, o // 8))
    for s, taps in sorted(by_res.items()):
        r = win if s == 0 else pltpu.roll(win, r_all - s, 0)
        for k, q in taps:
            yield k, r[8 * q:8 * q + rows]


CAUSAL_A = [HALO - (K_A - 1) + k for k in range(K_A)]
CAUSAL_B = [HALO - (K_B - 1) + k for k in range(K_B)]
ANTI_A = [K_A - 1 - k for k in range(K_A)]
ANTI_B = [K_B - 1 - k for k in range(K_B)]


def _host_call(body, *, name, grid, in_specs, out_specs, out_shape, scratch_shapes, args, hosted, prefetch=None,
               aliases=None, body_gets_stage_refs=False):
    n_in, n_out, n_scr = len(in_specs), len(out_specs), len(scratch_shapes)
    n_steps = 1
    for size in grid:
        n_steps *= size
    if isinstance(hosted, _Stage):
        hosted = [(hosted, 0, n_steps - 1)]
    n_pre = 0 if prefetch is None else 1
    pre = () if prefetch is None else (prefetch,)
    sem = ("arbitrary",) * len(grid)
    own_aliases = {n_pre + a: b for a, b in (aliases or {}).items()}

    def call(kernel_body, ins, outs, shapes, scratch, all_aliases, side_effects, operands):
        gs = pltpu.PrefetchScalarGridSpec(num_scalar_prefetch=n_pre, grid=grid, in_specs=ins, out_specs=outs,
                                          scratch_shapes=scratch)
        return pl.pallas_call(kernel_body, name=name, grid_spec=gs, out_shape=shapes,
                              input_output_aliases=all_aliases,
                              compiler_params=_cp(sem, side_effects=side_effects))(*pre, *operands)

    if hosted is None:
        res = call(body, list(in_specs), list(out_specs), list(out_shape), list(scratch_shapes), own_aliases, False,
                   args)
        return list(res), []
    stages = [st for st, _, _ in hosted]
    h_ins = [a for st in stages for a in st.ins]
    h_outs = [o for st in stages for o in st.outs]
    borrowed = {hi: k for hi, a in enumerate(h_ins) for k, own in enumerate(args) if a is own}
    passed = [hi for hi in range(len(h_ins)) if hi not in borrowed]
    h_in, h_out, n_sems = len(passed), len(h_outs), sum(st.n_sems for st in stages)

    def full_body(*refs):
        pre_refs, refs = refs[:n_pre], refs[n_pre:]
        ins, refs = refs[:n_in], refs[n_in:]
        hins = [None] * len(h_ins)
        for hi, ref in zip(passed, refs[:h_in]):
            hins[hi] = ref
        refs = refs[h_in:]
        outs, refs = refs[:n_out], refs[n_out:]
        houts, refs = refs[:h_out], refs[h_out:]
        scr, (send_sems, recv_sems) = refs[:n_scr], refs[n_scr:]
        step = 0
        for a, size in enumerate(grid):
            step = step * size + pl.program_id(a)
        phases, i0, o0, k0 = [], 0, 0, 0
        for st, starts, finishes in hosted:
            plans = st.make_plan(hins[i0:i0 + len(st.ins)], houts[o0:o0 + len(st.outs)],
                                 _SemsFrom(send_sems, k0), _SemsFrom(recv_sems, k0))
            if not isinstance(plans, list):
                plans, starts, finishes = [plans], [starts], [finishes]
            phases += list(zip(plans, starts, finishes))
            i0, o0, k0 = i0 + len(st.ins), o0 + len(st.outs), k0 + st.n_sems
        for plan, at, _ in phases:
            @pl.when(step == at)
            def _(plan=plan):
                _start(plan)

        if body_gets_stage_refs:
            body(*pre_refs, *ins, *outs, *scr, houts)
        else:
            body(*pre_refs, *ins, *outs, *scr)

        for plan, _, at in phases:
            @pl.when(step == at)
            def _(plan=plan):
                _finish(plan)

    all_aliases = dict(own_aliases)
    i0 = o0 = 0
    for st in stages:
        for a, b in st.aliases.items():
            hi = i0 + a
            operand = borrowed[hi] if hi in borrowed else n_in + passed.index(hi)
            all_aliases[n_pre + operand] = n_out + o0 + b
        i0, o0 = i0 + len(st.ins), o0 + len(st.outs)
    res = call(full_body, list(in_specs) + [ANY] * h_in, list(out_specs) + [ANY] * h_out,
               list(out_shape) + h_outs,
               list(scratch_shapes) + [pltpu.SemaphoreType.DMA((n_sems,)), pltpu.SemaphoreType.DMA((n_sems,))],
               all_aliases, True, (*args, *[h_ins[hi] for hi in passed]))
    return list(res[:n_out]), list(res[n_out:])


def _in_proj_own(x, w_shard, others, other_axes, pos, near):
    s, d = x.shape
    tn = w_shard.shape[1]
    tm = min(s // 4, PROJ_TILE)
    n_o = len(others)
    last = s // tm - 1
    hosted = [(near, [0, last], [last - 1, last])]

    def body(pos_ref, x_ref, w_ref, *refs):
        o_in, (p_ref, xb_ref), o_out, wb = refs[:n_o], refs[n_o:n_o + 2], refs[n_o + 2:2 * n_o + 2], refs[-1]

        @pl.when(pl.program_id(0) == 0)
        def _():
            wb[...] = w_ref[...].astype(BF)
            for src, dst in zip(o_in, o_out):
                dst[...] = src[...].astype(BF)

        xb = x_ref[...].astype(BF)
        xb_ref[...] = xb
        p_ref[...] = _dot(xb, wb[...]).astype(p_ref.dtype)

    whole = lambda a: pl.BlockSpec(a.shape, lambda i, pos: (0, 0), pipeline_mode=pl.Buffered(1))
    placed = lambda a, ax: pl.BlockSpec(a.shape, (lambda i, pos: (0, pos[0])) if ax == 1 else (lambda i, pos: (pos[0], 0)))
    full = lambda a, ax: SDS((a.shape[0], 4 * a.shape[1]) if ax == 1 else (4 * a.shape[0], a.shape[1]), BF)
    outs, extra = _host_call(
        body, name="in_proj_own", grid=(s // tm,), prefetch=pos,
        in_specs=[pl.BlockSpec((tm, d), lambda i, pos: (i, 0)), whole(w_shard)] + [whole(a) for a in others],
        out_specs=[pl.BlockSpec((tm, tn), lambda i, pos: (i, pos[0])), pl.BlockSpec((tm, d), lambda i, pos: (i, 0))]
        + [placed(a, ax) for a, ax in zip(others, other_axes)],
        out_shape=[SDS((s, 4 * tn), P_DT), SDS((s, d), BF)] + [full(a, ax) for a, ax in zip(others, other_axes)],
        scratch_shapes=[pltpu.VMEM((d, tn), BF)], args=(x, w_shard, *others), hosted=hosted)
    return outs, extra


def _in_proj_rest(xb, wi, p, pos, diagonal, other):
    s, d = xb.shape
    n = wi.shape[1]
    tm, tn = min(s // 4, PROJ_REST_TILE), n // 4
    ni = s // tm

    def body(pos_ref, xb_ref, w_ref, p_in, p_ref, wdiag, dsem, stage_outs):
        j = pl.program_id(0)
        step = j * ni + pl.program_id(1)

        def diagonal_block(act):
            for chip in range(4):
                @pl.when(pos_ref[0] == chip)
                def _(chip=chip):
                    act(pltpu.make_async_copy(stage_outs[0].at[:, pl.ds((chip ^ 3) * tn, tn)], wdiag, dsem))

        @pl.when(step == 2 * ni - 1)
        def _():
            diagonal_block(lambda cp: cp.start())

        @pl.when(step == 2 * ni)
        def _():
            diagonal_block(lambda cp: cp.wait())

        @pl.when(j < 2)
        def _():
            p_ref[...] = _dot(xb_ref[...], w_ref[...]).astype(p_ref.dtype)

        @pl.when(j == 2)
        def _():
            p_ref[...] = _dot(xb_ref[...], wdiag[...]).astype(p_ref.dtype)

    def col(j, i, pos):
        return lax.bitwise_xor(pos[0], jnp.where(j == 0, 2, jnp.where(j == 1, 1, 3)))

    def piped_col(j, i, pos):
        return lax.bitwise_xor(pos[0], jnp.where(j == 0, 2, 1))

    (p,), extra = _host_call(
        body, name="in_proj_rest", grid=(3, ni), prefetch=pos, aliases={2: 0}, body_gets_stage_refs=True,
        in_specs=[pl.BlockSpec((tm, d), lambda j, i, pos: (i, 0)),
                  pl.BlockSpec((d, tn), lambda j, i, pos: (0, piped_col(j, i, pos))), ANY],
        out_specs=[pl.BlockSpec((tm, tn), lambda j, i, pos: (i, col(j, i, pos)))],
        out_shape=[SDS((s, n), P_DT)], scratch_shapes=[pltpu.VMEM((d, tn), BF), pltpu.SemaphoreType.DMA(())],
        args=(xb, wi, p),
        hosted=[(diagonal, [0, ni + 1], [ni, 2 * ni - 2]), (other, 0, 3 * ni - 1)])
    n_d = len(diagonal.outs)
    return p, extra[:n_d], extra[n_d:]


def _col_spec(tm, d, k):
    return pl.BlockSpec((tm, d), lambda i, k=k: (i, k))


def _prev_halo_spec(tm, d, k):
    r = tm // HALO
    return pl.BlockSpec((HALO, d), lambda i, k=k: (jnp.maximum(i * r - 1, 0), k))


def _next_halo_spec(tm, d, k, s):
    r = tm // HALO
    last = s // HALO - 1
    return pl.BlockSpec((HALO, d), lambda i, k=k: (jnp.minimum((i + 1) * r, last), k))


def _conv_fwd(p, cwa, cwb, bias_b, lbg, lbb, d, hosted=None):
    s = p.shape[0]
    tm = min(s, CONV_TILE)
    nt = s // tm

    def body(ba_ref, ca_ref, va_ref, vb_ref, gb_ref, hca_ref, hva_ref, hvb_ref, hgb_ref,
             cwa_ref, cwb_ref, bias_ref, lbg_ref, lbb_ref,
             conva_ref, yapre_ref, nb_ref, rstdb_ref, u3_ref,
             zbuf, ubuf, u1buf):
        i = pl.program_id(0)
        keep = (i > 0).astype(F32)
        zbuf[pl.ds(0, HALO), :] = hca_ref[...].astype(F32) * hva_ref[...].astype(F32) * keep
        ubuf[pl.ds(0, HALO), :] = hvb_ref[...].astype(F32) * _sig(hgb_ref[...].astype(F32)) * keep
        zbuf[pl.ds(HALO, tm), :] = ca_ref[...].astype(F32) * va_ref[...].astype(F32)
        ubuf[pl.ds(HALO, tm), :] = vb_ref[...].astype(F32) * _sig(gb_ref[...].astype(F32))

        def chunk(j, carry):
            r0 = pl.multiple_of(j * CONV_ROWS, CONV_ROWS)
            rows = pl.ds(r0, CONV_ROWS)
            for lc in range(d // CONV_LANES):
                ls = pl.ds(lc * CONV_LANES, CONV_LANES)
                acc = jnp.zeros((CONV_ROWS, CONV_LANES), F32)
                for k, sl in _taps(zbuf[pl.ds(r0, CONV_ROWS + HALO), ls], CAUSAL_A, CONV_ROWS):
                    acc = acc + cwa_ref[pl.ds(k, 1), ls] * sl
                conva_ref[rows, ls] = acc.astype(BF)
                yapre_ref[rows, ls] = (ba_ref[rows, ls].astype(F32) * acc).astype(BF)
                acc = jnp.zeros((CONV_ROWS, CONV_LANES), F32)
                for k, sl in _taps(ubuf[pl.ds(r0, CONV_ROWS + HALO), ls], CAUSAL_B, CONV_ROWS):
                    acc = acc + cwb_ref[pl.ds(k, 1), ls] * sl
                u1buf[rows, ls] = acc + bias_ref[:, ls]
            return carry

        lax.fori_loop(0, tm // CONV_ROWS, chunk, 0)
        nb, rstd = _ln_fwd(u1buf[...])
        nb_ref[...] = nb
        rstdb_ref[...] = rstd
        u2 = nb * lbg_ref[...] + lbb_ref[...]
        u3_ref[...] = (u2 * _sig(u2)).astype(BF)

    vec = _const((1, d))
    return _host_call(
        body, name="conv_fwd", grid=(nt,),
        in_specs=[_col_spec(tm, d, k) for k in range(5)] + [_prev_halo_spec(tm, d, k) for k in (1, 2, 3, 4)]
        + [_const((K_A, d)), _const((K_B, d)), vec, vec, vec],
        out_specs=[pl.BlockSpec((tm, d), lambda i: (i, 0)), pl.BlockSpec((tm, d), lambda i: (i, 0)),
                   pl.BlockSpec((tm, d), lambda i: (i, 0)), pl.BlockSpec((tm, 1), lambda i: (i, 0)),
                   pl.BlockSpec((tm, d), lambda i: (i, 0))],
        out_shape=[SDS((s, d), BF), SDS((s, d), BF), SDS((s, d), F32), SDS((s, 1), F32), SDS((s, d), BF)],
        scratch_shapes=[pltpu.VMEM((HALO + tm, d), F32), pltpu.VMEM((HALO + tm, d), F32), pltpu.VMEM((tm, d), F32)],
        args=(p, p, p, p, p, p, p, p, p, cwa, cwb, bias_b, lbg, lbb), hosted=hosted)


def _mixer_out(yapre, u3, p, x, woa, wob, wo, d, hosted=None):
    s = x.shape[0]
    tm = min(s, MIXER_TILE)

    def body(yapre_ref, u3_ref, ga_ref, gb_ref, x_ref, woa_ref, wob_ref, wo_ref,
             ya_ref, yb_ref, merged_ref, n1_ref, rstd1_ref):
        ya = _dot(yapre_ref[...], woa_ref[...])
        yb = _dot(u3_ref[...], wob_ref[...])
        ya_ref[...] = ya.astype(BF)
        yb_ref[...] = yb.astype(BF)
        merged = (_sig(ga_ref[...].astype(F32)) * ya + _sig(gb_ref[...].astype(F32)) * yb).astype(BF)
        merged_ref[...] = merged
        r1 = F32(ALPHA) * x_ref[...] + _dot(merged, wo_ref[...])
        n1, rstd1 = _ln_fwd(r1)
        n1_ref[...] = n1
        rstd1_ref[...] = rstd1

    row = pl.BlockSpec((tm, d), lambda i: (i, 0))
    return _host_call(
        body, name="mixer_out", grid=(s // tm,),
        in_specs=[row, row, _col_spec(tm, d, 5), _col_spec(tm, d, 6), row,
                  _const((d, d)), _const((d, d)), _const((d, d))],
        out_specs=[row, row, row, row, pl.BlockSpec((tm, 1), lambda i: (i, 0))],
        out_shape=[SDS((s, d), BF), SDS((s, d), BF), SDS((s, d), BF), SDS((s, d), F32), SDS((s, 1), F32)],
        scratch_shapes=[], args=(yapre, u3, p, p, x, woa, wob, wo), hosted=hosted)


def _mlp(n1, rstd1, tgt, wup, wdown, l1g, l1b, l2g, l2b):
    s, d = n1.shape
    dff = wup.shape[1]
    tm = min(s, LOCAL_TILE)
    fc = min(dff, FF_CHUNK)
    nq = dff // fc

    def body(n1_ref, rstd1_ref, tgt_ref, wup_ref, wdown_ref, l1g_ref, l1b_ref, l2g_ref, l2b_ref,
             x1b_ref, hb_ref, dhpre_ref, dr2b_ref, dr1_ref, dr1b_ref, acc_ref, rbuf):
        i = pl.program_id(0)
        n1v = n1_ref[...]
        x1 = n1v * l1g_ref[...] + l1b_ref[...]
        x1b = x1.astype(BF)
        x1b_ref[...] = x1b
        ff = jnp.zeros((tm, d), F32)
        for q in range(nq):
            cs = pl.ds(q * fc, fc)
            r = jnp.maximum(_dot(x1b, wup_ref[:, cs]), 0.0)
            rbuf[:, cs] = r
            hq = (r * r).astype(BF)
            hb_ref[:, cs] = hq
            ff = ff + _dot(hq, wdown_ref[cs, :])
        n2, rstd2 = _ln_fwd(F32(ALPHA) * x1 + ff)
        x2 = n2 * l2g_ref[...] + l2b_ref[...]
        err = x2 - tgt_ref[...]
        dx2 = err * F32(1.0 / d)
        dr2 = _ln_bwd(dx2 * l2g_ref[...], n2, rstd2)
        dr2b = dr2.astype(BF)
        dr2b_ref[...] = dr2b
        dx1 = F32(ALPHA) * dr2
        for q in range(nq):
            cs = pl.ds(q * fc, fc)
            dh = _dot_nt(dr2b, wdown_ref[cs, :])
            dhp = (dh * (2.0 * rbuf[:, cs])).astype(BF)
            dhpre_ref[:, cs] = dhp
            dx1 = dx1 + _dot_nt(dhp, wup_ref[:, cs])
        dr1 = _ln_bwd(dx1 * l1g_ref[...], n1v, rstd1_ref[...])
        dr1_ref[...] = dr1
        dr1b_ref[...] = dr1.astype(BF)

        @pl.when(i == 0)
        def _():
            acc_ref[...] = jnp.zeros_like(acc_ref)

        for q, val in enumerate((err * err, dx2 * n2, dx2, dx1 * n1v, dx1)):
            acc_ref[pl.ds(q, 1), :] += jnp.sum(val, axis=0, keepdims=True)

    row = pl.BlockSpec((tm, d), lambda i: (i, 0))
    wide = pl.BlockSpec((tm, dff), lambda i: (i, 0))
    vec = _const((1, d))
    return pl.pallas_call(
        body, name="mlp_fwd_bwd", grid=(s // tm,),
        in_specs=[row, pl.BlockSpec((tm, 1), lambda i: (i, 0)), row, _const((d, dff)), _const((dff, d)),
                  vec, vec, vec, vec],
        out_specs=[row, wide, wide, row, row, row, pl.BlockSpec((8, d), lambda i: (0, 0))],
        out_shape=[SDS((s, d), BF), SDS((s, dff), BF), SDS((s, dff), BF), SDS((s, d), BF), SDS((s, d), F32),
                   SDS((s, d), BF), SDS((8, d), F32)],
        scratch_shapes=[pltpu.VMEM((tm, dff), F32)],
        compiler_params=_cp(("arbitrary",)),
    )(n1, rstd1, tgt, wup, wdown, l1g, l1b, l2g, l2b)


def _mixer_bwd_local(dr1b, p, ya, yb, conva, nb, rstdb, wo, woa, wob, lbg, lbb):
    s, d = ya.shape
    tm = min(s, MIXER_BWD_TILE)

    def body(dr1b_ref, ba_ref, ga_ref, gb_ref, ya_ref, yb_ref, conva_ref, nb_ref, rstdb_ref,
             wo_ref, woa_ref, wob_ref, lbg_ref, lbb_ref,
             dya_ref, dyb_ref, dba_ref, dga_ref, dgb_ref, dca_ref, du1_ref, acc_ref):
        i = pl.program_id(0)
        dmerged = _dot_nt(dr1b_ref[...], wo_ref[...])
        sa = _sig(ga_ref[...].astype(F32))
        sb = _sig(gb_ref[...].astype(F32))
        dya_f = dmerged * sa
        dyb_f = dmerged * sb
        dya = dya_f.astype(BF)
        dyb = dyb_f.astype(BF)
        dya_ref[...] = dya
        dyb_ref[...] = dyb
        dga_ref[...] = (dya_f * ya_ref[...].astype(F32) * (1.0 - sa)).astype(BF)
        dgb_ref[...] = (dyb_f * yb_ref[...].astype(F32) * (1.0 - sb)).astype(BF)
        dyapre = _dot_nt(dya, woa_ref[...])
        dba_ref[...] = (dyapre * conva_ref[...].astype(F32)).astype(BF)
        dca_ref[...] = (dyapre * ba_ref[...].astype(F32)).astype(BF)
        du3 = _dot_nt(dyb, wob_ref[...])
        nbv = nb_ref[...]
        u2 = nbv * lbg_ref[...] + lbb_ref[...]
        sg = _sig(u2)
        du2 = du3 * (sg * (1.0 + u2 * (1.0 - sg)))
        du1 = _ln_bwd(du2 * lbg_ref[...], nbv, rstdb_ref[...])
        du1_ref[...] = du1.astype(BF)

        @pl.when(i == 0)
        def _():
            acc_ref[...] = jnp.zeros_like(acc_ref)

        for q, val in enumerate((du2 * nbv, du2, du1)):
            acc_ref[pl.ds(q, 1), :] += jnp.sum(val, axis=0, keepdims=True)

    row = pl.BlockSpec((tm, d), lambda i: (i, 0))
    vec = _const((1, d))
    return pl.pallas_call(
        body, name="mixer_bwd_local", grid=(s // tm,),
        in_specs=[row, _col_spec(tm, d, 0), _col_spec(tm, d, 5), _col_spec(tm, d, 6), row, row, row, row,
                  pl.BlockSpec((tm, 1), lambda i: (i, 0)), _const((d, d)), _const((d, d)), _const((d, d)), vec, vec],
        out_specs=[row, row, row, row, row, row, row, pl.BlockSpec((8, d), lambda i: (0, 0))],
        out_shape=[SDS((s, d), BF)] * 7 + [SDS((8, d), F32)],
        compiler_params=_cp(("arbitrary",)),
    )(dr1b, p, p, p, ya, yb, conva, nb, rstdb, wo, woa, wob, lbg, lbb)


def _conv_bwd(dca, du1, p, dba, dga, dgb, cwa, cwb, d, hosted=None):
    s = dca.shape[0]
    tm = min(s, CONV_TILE)
    nt = s // tm

    def body(dca_ref, du1_ref, ndca_ref, ndu1_ref, ca_ref, va_ref, vb_ref, gb_ref,
             dba_ref, dga_ref, dgb_ref, cwa_ref, cwb_ref,
             dp_ref, dcwa_ref, dcwb_ref,
             dcabuf, du1buf, sgbuf, acca, accb):
        i = pl.program_id(0)
        keep_next = (i < nt - 1).astype(F32)

        @pl.when(i == 0)
        def _():
            acca[...] = jnp.zeros_like(acca)
            accb[...] = jnp.zeros_like(accb)

        sgbuf[...] = _sig(gb_ref[...].astype(F32))
        dcabuf[pl.ds(0, tm), :] = dca_ref[...].astype(F32)
        dcabuf[pl.ds(tm, HALO), :] = ndca_ref[...].astype(F32) * keep_next
        du1buf[pl.ds(0, tm), :] = du1_ref[...].astype(F32)
        du1buf[pl.ds(tm, HALO), :] = ndu1_ref[...].astype(F32) * keep_next
        dp_ref[:, pl.ds(0, d)] = dba_ref[...]
        dp_ref[:, pl.ds(5 * d, d)] = dga_ref[...]
        dp_ref[:, pl.ds(6 * d, d)] = dgb_ref[...]

        def chunk(j, carry):
            r0 = pl.multiple_of(j * CONV_ROWS, CONV_ROWS)
            rows = pl.ds(r0, CONV_ROWS)
            for lc in range(d // CONV_LANES):
                lo = lc * CONV_LANES
                ls = pl.ds(lo, CONV_LANES)
                cac = ca_ref[rows, ls].astype(F32)
                vac = va_ref[rows, ls].astype(F32)
                zc = cac * vac
                acc = jnp.zeros((CONV_ROWS, CONV_LANES), F32)
                for k, sl in _taps(dcabuf[pl.ds(r0, CONV_ROWS + HALO), ls], ANTI_A, CONV_ROWS):
                    acc = acc + cwa_ref[pl.ds(k, 1), ls] * sl
                    acca[pl.ds(8 * k, 8), ls] += _rowsum8(sl * zc)
                dp_ref[rows, pl.ds(d + lo, CONV_LANES)] = (acc * vac).astype(BF)
                dp_ref[rows, pl.ds(2 * d + lo, CONV_LANES)] = (acc * cac).astype(BF)
                sgc = sgbuf[rows, ls]
                vbc = vb_ref[rows, ls].astype(F32)
                uc = vbc * sgc
                acc = jnp.zeros((CONV_ROWS, CONV_LANES), F32)
                for k, sl in _taps(du1buf[pl.ds(r0, CONV_ROWS + HALO), ls], ANTI_B, CONV_ROWS):
                    acc = acc + cwb_ref[pl.ds(k, 1), ls] * sl
                    accb[pl.ds(8 * k, 8), ls] += _rowsum8(sl * uc)
                dval = acc * sgc
                dp_ref[rows, pl.ds(3 * d + lo, CONV_LANES)] = dval.astype(BF)
                dp_ref[rows, pl.ds(4 * d + lo, CONV_LANES)] = (dval * vbc * (1.0 - sgc)).astype(BF)
            return carry

        lax.fori_loop(0, tm // CONV_ROWS, chunk, 0)

        @pl.when(i == nt - 1)
        def _():
            dcwa_ref[...] = jnp.zeros_like(dcwa_ref)
            dcwb_ref[...] = jnp.zeros_like(dcwb_ref)
            for k in range(K_A):
                dcwa_ref[pl.ds(k, 1), :] = jnp.sum(acca[pl.ds(8 * k, 8), :], axis=0, keepdims=True)
            for k in range(K_B):
                dcwb_ref[pl.ds(k, 1), :] = jnp.sum(accb[pl.ds(8 * k, 8), :], axis=0, keepdims=True)

    row = pl.BlockSpec((tm, d), lambda i: (i, 0))
    nxt = _next_halo_spec(tm, d, 0, s)
    return _host_call(
        body, name="conv_bwd", grid=(nt,),
        in_specs=[row, row, nxt, nxt] + [_col_spec(tm, d, k) for k in (1, 2, 3, 4)]
        + [row, row, row, _const((K_A, d)), _const((K_B, d))],
        out_specs=[pl.BlockSpec((tm, 7 * d), lambda i: (i, 0)), pl.BlockSpec((8, d), lambda i: (0, 0)),
                   pl.BlockSpec((32, d), lambda i: (0, 0))],
        out_shape=[SDS((s, 7 * d), BF), SDS((8, d), F32), SDS((32, d), F32)],
        scratch_shapes=[pltpu.VMEM((tm + HALO, d), F32), pltpu.VMEM((tm + HALO, d), F32),
                        pltpu.VMEM((tm, d), F32), pltpu.VMEM((8 * K_A, d), F32), pltpu.VMEM((8 * K_B, d), F32)],
        args=(dca, du1, dca, du1, p, p, p, p, dba, dga, dgb, cwa, cwb), hosted=hosted)


def _grad_w(a, b, name, hosted=None, tokens=None):
    s, m = a.shape
    n = b.shape[1]
    tm, tn, tk = _tile(m, GRAD_W_TILE), _tile(n, GRAD_W_TILE), _tile(s, tokens or GRAD_W_TOKENS)
    nk = s // tk

    def body(a_ref, b_ref, o_ref, ob_ref):
        k = pl.program_id(2)

        @pl.when(k == 0)
        def _():
            o_ref[...] = jnp.zeros_like(o_ref)

        o_ref[...] += _dot_tn(a_ref[...], b_ref[...])

        @pl.when(k == nk - 1)
        def _():
            ob_ref[...] = o_ref[...].astype(BF)

    blk = pl.BlockSpec((tm, tn), lambda i, j, k: (i, j))
    (g, gb), extra = _host_call(
        body, name=name, grid=(m // tm, n // tn, nk),
        in_specs=[pl.BlockSpec((tk, tm), lambda i, j, k: (k, i)), pl.BlockSpec((tk, tn), lambda i, j, k: (k, j))],
        out_specs=[blk, blk], out_shape=[SDS((m, n), F32), SDS((m, n), BF)], scratch_shapes=[], args=(a, b),
        hosted=hosted)
    return g, gb, extra


def _grad_x(dr1, dp, wi, hosted=None):
    s, d = dr1.shape
    n = wi.shape[1]
    tm = min(s, GRAD_X_TILE)

    def body(dr1_ref, dp_ref, w_ref, o_ref):
        o_ref[...] = F32(ALPHA) * dr1_ref[...] + _dot_nt(dp_ref[...], w_ref[...])

    (gx,), extra = _host_call(
        body, name="grad_x", grid=(s // tm,),
        in_specs=[pl.BlockSpec((tm, d), lambda i: (i, 0)), pl.BlockSpec((tm, n), lambda i: (i, 0)), _const((d, n))],
        out_specs=[pl.BlockSpec((tm, d), lambda i: (i, 0))],
        out_shape=[SDS((s, d), F32)], scratch_shapes=[], args=(dr1, dp, wi), hosted=hosted)
    return gx, extra


def _adamw_math(w, g, m, v):
    m2 = ADAM_B1 * m + (1.0 - ADAM_B1) * g
    v2 = ADAM_B2 * v + (1.0 - ADAM_B2) * (g * g)
    m_hat = m2 / (1.0 - ADAM_B1 ** ADAM_STEP)
    v_hat = v2 / (1.0 - ADAM_B2 ** ADAM_STEP)
    delta = -ADAM_LR * (m_hat / (jnp.sqrt(v_hat) + ADAM_EPS) + ADAM_WD * w)
    return delta, m2, v2


def _adamw(w, g, m, v, name):
    r, c = w.shape
    tr = min(r, ROWS_TILE)

    def body(w_ref, g_ref, m_ref, v_ref, g_out, d_ref, m2_ref, v2_ref):
        gv = g_ref[...]
        delta, m2, v2 = _adamw_math(w_ref[...], gv, m_ref[...], v_ref[...])
        g_out[...] = gv
        d_ref[...] = delta
        m2_ref[...] = m2
        v2_ref[...] = v2

    blk = pl.BlockSpec((tr, c), lambda i: (i, 0))
    return pl.pallas_call(
        body, name=name, grid=(r // tr,), in_specs=[blk] * 4, out_specs=[blk] * 4,
        out_shape=[SDS((r, c), F32)] * 4, compiler_params=_cp(("parallel",)),
    )(w, g, m, v)


def _sum_parts(parts, name):
    k, r, c = parts.shape

    def body(p_ref, o_ref):
        acc = p_ref[0]
        for q in range(1, k):
            acc = acc + p_ref[q]
        o_ref[...] = acc

    return pl.pallas_call(
        body, name=name, grid=(1,),
        in_specs=[pl.BlockSpec((k, r, c), lambda i: (0, 0, 0))],
        out_specs=pl.BlockSpec((r, c), lambda i: (0, 0)),
        out_shape=SDS((r, c), F32), compiler_params=_cp(("arbitrary",)),
    )(parts)


def _piece_shape(full_shape, axis):
    r, c = full_shape
    return (r // 2, c // 4) if axis == 1 else (r // 8, c)


def _piece_spec(full_shape, axis, tr, chip_of, half_of):
    hr, wc = _piece_shape(full_shape, axis)
    nb = hr // tr
    if axis == 1:
        return pl.BlockSpec((tr, wc), lambda *a: (half_of(*a) * nb + a[-2], chip_of(*a)))
    return pl.BlockSpec((tr, wc), lambda *a: ((2 * chip_of(*a) + half_of(*a)) * nb + a[-2], 0))


def _place_cast(w, axis, pos, name):
    r, c = w.shape
    tr = min(r, ROWS_TILE)
    nb = r // tr
    full = (r, 4 * c) if axis == 1 else (4 * r, c)
    out_map = (lambda i, pos: (i, pos[0])) if axis == 1 else (lambda i, pos: (pos[0] * nb + i, 0))

    def body(pos_ref, w_ref, o_ref):
        o_ref[...] = w_ref[...].astype(o_ref.dtype)

    gs = pltpu.PrefetchScalarGridSpec(
        num_scalar_prefetch=1, grid=(nb,),
        in_specs=[pl.BlockSpec((tr, c), lambda i, pos: (i, 0))], out_specs=pl.BlockSpec((tr, c), out_map))
    return pl.pallas_call(body, name=name, grid_spec=gs, out_shape=SDS(full, BF),
                          compiler_params=_cp(("arbitrary",)))(pos, w)


def _pair_add(g, land, axis, pos, name):
    hr, wc = _piece_shape(g.shape, axis)
    tr = min(hr, SUM_ROWS)

    def body(pos_ref, g_ref, l_ref, o_ref):
        o_ref[0] = (g_ref[...] + l_ref[0].astype(F32)).astype(BF)

    other = lambda q, i, pos: (pos[0] + 1 + q) % 4
    blk = pl.BlockSpec((1, tr, wc), lambda q, i, pos: (other(q, i, pos), i, 0))
    gs = pltpu.PrefetchScalarGridSpec(
        num_scalar_prefetch=1, grid=(3, hr // tr),
        in_specs=[_piece_spec(g.shape, axis, tr, other, lambda q, i, pos: pos[1]), blk], out_specs=blk)
    return pl.pallas_call(body, name=name, grid_spec=gs, out_shape=SDS((4, hr, wc), BF),
                          compiler_params=_cp(("arbitrary", "arbitrary")))(pos, g, land)


def _chip_sum(g, land1, land2, axis, pos, name):
    hr, wc = _piece_shape(g.shape, axis)
    tr = min(hr, SUM_ROWS)
    nb = hr // tr

    def body(pos_ref, g_ref, l1_ref, l2_ref, o_ref):
        acc = g_ref[...] + l1_ref[0].astype(F32)
        for q in range(3):
            acc = acc + l2_ref[q].astype(F32)
        o_ref[...] = acc

    gs = pltpu.PrefetchScalarGridSpec(
        num_scalar_prefetch=1, grid=(nb,),
        in_specs=[_piece_spec(g.shape, axis, tr, lambda i, pos: pos[0], lambda i, pos: pos[1]),
                  pl.BlockSpec((1, tr, wc), lambda i, pos: (pos[0], i, 0)),
                  pl.BlockSpec((3, tr, wc), lambda i, pos: (0, i, 0))],
        out_specs=pl.BlockSpec((tr, wc), lambda i, pos: (pos[1] * nb + i, 0)))
    return pl.pallas_call(body, name=name, grid_spec=gs, out_shape=SDS((2 * hr, wc), F32),
                          compiler_params=_cp(("arbitrary",)))(pos, g, land1, land2)


ANY = pl.BlockSpec(memory_space=pl.ANY)
COMM = pltpu.CompilerParams(has_side_effects=True)


def _on_each_device(fn):
    x, y, c = lax.axis_index("x"), lax.axis_index("y"), lax.axis_index("c")
    for sx in (0, 1):
        for sy in (0, 1):
            for sc in (0, 1):
                @pl.when(jnp.logical_and(jnp.logical_and(x == sx, y == sy), c == sc))
                def _(sx=sx, sy=sy, sc=sc):
                    fn(sx, sy, sc)


def _remote(src, dst, send_sem, recv_sem, to):
    return pltpu.make_async_remote_copy(src_ref=src, dst_ref=dst, send_sem=send_sem, recv_sem=recv_sem,
                                        device_id=to, device_id_type=MESH)


def _piece_ref(ref, axis, j, h):
    r, c = ref.shape
    hr, wc = _piece_shape((r, c), axis)
    if axis == 1:
        return ref.at[pl.ds(h * hr, hr), pl.ds(j * wc, wc)]
    return ref.at[pl.ds((2 * j + h) * hr, hr), :]


class _Stage:
    def __init__(self, ins, outs, n_sems, make_plan, aliases=None):
        self.ins, self.outs, self.n_sems, self.make_plan = list(ins), list(outs), n_sems, make_plan
        self.aliases = dict(aliases or {})


def _start(plan):
    def dev(sx, sy, sc):
        for cp, _, _ in plan(sx, sy, sc):
            cp.start()

    _on_each_device(dev)


def _finish(plan):
    def dev(sx, sy, sc):
        for _, sent, got in plan(sx, sy, sc):
            sent.wait_send()
            got.wait_recv()

    _on_each_device(dev)


def _comm_call(stage, name):
    n_in, n_out = len(stage.ins), len(stage.outs)

    def body(*refs):
        plan = stage.make_plan(refs[:n_in], refs[n_in:n_in + n_out], *refs[n_in + n_out:])
        _start(plan)
        _finish(plan)

    return pl.pallas_call(
        body, name=name, in_specs=[ANY] * n_in, out_specs=[ANY] * n_out, out_shape=stage.outs,
        input_output_aliases=stage.aliases,
        scratch_shapes=[pltpu.SemaphoreType.DMA((stage.n_sems,)), pltpu.SemaphoreType.DMA((stage.n_sems,))],
        compiler_params=COMM,
    )(*stage.ins)


class _SemsFrom:
    def __init__(self, sems, base):
        self.sems, self.base = sems, base

    @property
    def at(self):
        return self

    def __getitem__(self, k):
        return self.sems.at[self.base + k]


def _both(a, b):
    na, nb = len(a.ins), len(b.ins)
    ma = len(a.outs)

    def make_plan(ins, outs, send_sems, recv_sems):
        pa = a.make_plan(ins[:na], outs[:ma], send_sems, recv_sems)
        pb = b.make_plan(ins[na:], outs[ma:], _SemsFrom(send_sems, a.n_sems), _SemsFrom(recv_sems, a.n_sems))
        return lambda sx, sy, sc: pa(sx, sy, sc) + pb(sx, sy, sc)

    aliases = dict(a.aliases)
    aliases.update({na + i: ma + o for i, o in b.aliases.items()})
    return _Stage(a.ins + b.ins, a.outs + b.outs, a.n_sems + b.n_sems, make_plan, aliases)


def _same(cp):
    return (cp, cp, cp)


def _stage_gather_send(fulls, axes):
    n = len(fulls)

    def make_plan(ins, outs, send_sems, recv_sems):
        def plan(sx, sy, sc):
            cps = []
            for w in range(n):
                mine = _piece_ref(outs[w], axes[w], 2 * sx + sy, sc)
                for r, (fx, fy) in enumerate(CHIP_RELS):
                    k = 3 * w + r
                    to = (sx ^ fx, sy ^ fy, sc)
                    got = _piece_ref(outs[w], axes[w], 2 * (sx ^ fx) + (sy ^ fy), sc)
                    send = _remote(mine, mine, send_sems.at[k], recv_sems.at[k], to)
                    cps.append((send, send, _remote(got, got, send_sems.at[k], recv_sems.at[k], to)))
            return cps

        return plan

    return _Stage(fulls, [SDS(f.shape, f.dtype) for f in fulls], 3 * n, make_plan, {i: i for i in range(n)})


def _stage_gather_forward(fulls, axes):
    n = len(fulls)

    def make_plan(ins, outs, send_sems, recv_sems):
        def plan(sx, sy, sc):
            cps = []
            sib = (sx, sy, 1 - sc)
            for w in range(n):
                for r, (fx, fy) in enumerate(CHIP_RELS):
                    k = 3 * w + r
                    pj = 2 * (sx ^ fx) + (sy ^ fy)
                    have = _piece_ref(outs[w], axes[w], pj, sc)
                    want = _piece_ref(outs[w], axes[w], pj, 1 - sc)
                    send = _remote(have, have, send_sems.at[k], recv_sems.at[k], sib)
                    cps.append((send, send, _remote(want, want, send_sems.at[k], recv_sems.at[k], sib)))
            return cps

        return plan

    return _Stage(fulls, [SDS(f.shape, f.dtype) for f in fulls], 3 * n, make_plan, {i: i for i in range(n)})


def _in_place(fulls, n_sems, make_plan):
    return _Stage(fulls, [SDS(f.shape, f.dtype) for f in fulls], n_sems, make_plan, {i: i for i in range(len(fulls))})


def _stage_gather_neighbours(fulls, axes):
    n = len(fulls)

    def make_plan(ins, outs, send_sems, recv_sems):
        def plan(sx, sy, sc):
            cps = []
            for w in range(n):
                mine = _piece_ref(outs[w], axes[w], 2 * sx + sy, sc)
                for r, (px, py) in enumerate(((sx ^ 1, sy), (sx, sy ^ 1))):
                    got = _piece_ref(outs[w], axes[w], 2 * px + py, sc)
                    send = _remote(mine, mine, send_sems.at[2 * w + r], recv_sems.at[2 * w + r], (px, py, sc))
                    cps.append((send, send, _remote(got, got, send_sems.at[2 * w + r], recv_sems.at[2 * w + r],
                                                    (px, py, sc))))
            return cps

        return plan

    return _in_place(fulls, 2 * n, make_plan)


def _stage_gather_near(fulls, axes):
    first, second = _stage_gather_neighbours(fulls, axes), _stage_gather_pair(fulls, axes)

    def make_plan(ins, outs, send_sems, recv_sems):
        return [first.make_plan(ins, outs, send_sems, recv_sems),
                second.make_plan(ins, outs, _SemsFrom(send_sems, first.n_sems), _SemsFrom(recv_sems, first.n_sems))]

    return _in_place(fulls, first.n_sems + second.n_sems, make_plan)


def _stage_gather_pair(fulls, axes):
    n = len(fulls)

    def make_plan(ins, outs, send_sems, recv_sems):
        def plan(sx, sy, sc):
            cps = []
            sib = (sx, sy, 1 - sc)
            for w in range(n):
                for r, j in enumerate((2 * (sx ^ 1) + sy, 2 * sx + (sy ^ 1))):
                    k = 2 * w + r
                    have, want = _piece_ref(outs[w], axes[w], j, sc), _piece_ref(outs[w], axes[w], j, 1 - sc)
                    send = _remote(have, have, send_sems.at[k], recv_sems.at[k], sib)
                    cps.append((send, send, _remote(want, want, send_sems.at[k], recv_sems.at[k], sib)))
            return cps

        return plan

    return _in_place(fulls, 2 * n, make_plan)


def _stage_gather_diagonal(fulls, axes):
    n = len(fulls)

    def make_plan(ins, outs, send_sems, recv_sems):
        def relay(sx, sy, sc):
            cps = []
            jx, jy, jd = 2 * (sx ^ 1) + sy, 2 * sx + (sy ^ 1), 2 * (sx ^ 1) + (sy ^ 1)
            passed, to = (jx, (sx, sy ^ 1, sc)) if sc == 0 else (jy, (sx ^ 1, sy, sc))
            for w in range(n):
                have, want = _piece_ref(outs[w], axes[w], passed, sc), _piece_ref(outs[w], axes[w], jd, sc)
                send = _remote(have, have, send_sems.at[w], recv_sems.at[w], to)
                cps.append((send, send, _remote(want, want, send_sems.at[w], recv_sems.at[w], to)))
            return cps

        def cross(sx, sy, sc):
            cps = []
            sib = (sx, sy, 1 - sc)
            jd = 2 * (sx ^ 1) + (sy ^ 1)
            for w in range(n):
                have, want = _piece_ref(outs[w], axes[w], jd, sc), _piece_ref(outs[w], axes[w], jd, 1 - sc)
                send = _remote(have, have, send_sems.at[n + w], recv_sems.at[n + w], sib)
                cps.append((send, send, _remote(want, want, send_sems.at[n + w], recv_sems.at[n + w], sib)))
            return cps

        return [relay, cross]

    return _in_place(fulls, 2 * n, make_plan)


def _stage_pair_exchange(grads, axes):
    n = len(grads)

    def make_plan(gs, land, send_sems, recv_sems):
        def plan(sx, sy, sc):
            return [_same(_remote(_piece_ref(gs[w], axes[w], jj, 1 - sc), land[w].at[jj], send_sems.at[4 * w + jj],
                                  recv_sems.at[4 * w + jj], (sx, sy, 1 - sc)))
                    for w in range(n) for jj in range(4)]

        return plan

    return _Stage(grads, [SDS((4,) + _piece_shape(g.shape, a), g.dtype) for g, a in zip(grads, axes)], 4 * n,
                  make_plan)


def _stage_chip_scatter(pieces):
    n = len(pieces)

    def make_plan(ps, land, send_sems, recv_sems):
        def plan(sx, sy, sc):
            return [_same(_remote(ps[w].at[2 * (sx ^ fx) + (sy ^ fy)], land[w].at[r], send_sems.at[3 * w + r],
                                  recv_sems.at[3 * w + r], (sx ^ fx, sy ^ fy, sc)))
                    for w in range(n) for r, (fx, fy) in enumerate(CHIP_RELS)]

        return plan

    return _Stage(pieces, [SDS((3,) + p.shape[1:], p.dtype) for p in pieces], 3 * n, make_plan)


def _stage_pair_share(shards):
    n = len(shards)

    def make_plan(ins, outs, send_sems, recv_sems):
        def plan(sx, sy, sc):
            cps = []
            sib = (sx, sy, 1 - sc)
            for w in range(n):
                hr = shards[w].shape[0] // 2
                mine = outs[w].at[pl.ds(sc * hr, hr), :]
                theirs = outs[w].at[pl.ds((1 - sc) * hr, hr), :]
                send = _remote(mine, mine, send_sems.at[w], recv_sems.at[w], sib)
                cps.append((send, send, _remote(theirs, theirs, send_sems.at[w], recv_sems.at[w], sib)))
            return cps

        return plan

    return _Stage(shards, [SDS(g.shape, g.dtype) for g in shards], n, make_plan, {i: i for i in range(n)})


def _stage_gather_small(stack):
    def make_plan(ins, outs, send_sems, recv_sems):
        def plan(sx, sy, sc):
            mine = outs[0].at[4 * sx + 2 * sy + sc]
            return [_same(_remote(mine, mine, send_sems.at[k], recv_sems.at[k], (sx ^ fx, sy ^ fy, sc ^ fc)))
                    for k, (fx, fy, fc) in enumerate(DEV_RELS)]

        return plan

    return _Stage([stack], [SDS(stack.shape, stack.dtype)], 7, make_plan, {0: 0})


def kernel(x, w_in, conv_a_w, w_out_a, conv_b_w, conv_b_bias, ln_b_gamma, ln_b_beta, w_out_b, w_o, ln1_gamma, ln1_beta, w_up, w_down, ln2_gamma, ln2_beta, loss_target, m_w_in, m_conv_a_w, m_w_out_a, m_conv_b_w, m_conv_b_bias, m_ln_b_gamma, m_ln_b_beta, m_w_out_b, m_w_o, m_ln1_gamma, m_ln1_beta, m_w_up, m_w_down, m_ln2_gamma, m_ln2_beta, v_w_in, v_conv_a_w, v_w_out_a, v_conv_b_w, v_conv_b_bias, v_ln_b_gamma, v_ln_b_beta, v_w_out_b, v_w_o, v_ln1_gamma, v_ln1_beta, v_w_up, v_w_down, v_ln2_gamma, v_ln2_beta):
    s, d = x.shape[1], x.shape[2]
    xs = x.reshape(s, d)
    tgt = loss_target.reshape(s, d)
    dq = d // 4
    chip = 2 * lax.axis_index("x") + lax.axis_index("y")
    core = lax.axis_index("c")
    pos = jnp.stack([chip, core]).astype(jnp.int32)
    names = ("w_in", "w_out_a", "w_out_b", "w_o", "w_up", "w_down")
    axes = (1, 0, 0, 0, 1, 0)

    conv_pack = jnp.concatenate([jnp.pad(conv_a_w, ((0, 8 - K_A), (0, 0))), jnp.pad(conv_b_w, ((0, 32 - K_B), (0, 0))),
                                 jnp.zeros((8, dq), F32)], axis=0)
    conv_full = lax.dynamic_update_slice(jnp.zeros((conv_pack.shape[0], d), F32), conv_pack, (0, chip * dq))
    wi_own = _place_cast(w_in, 1, pos, "place_w_in")
    vec = lambda a: a.reshape(1, d)
    bias_b, lbg, lbb = vec(conv_b_bias), vec(ln_b_gamma), vec(ln_b_beta)
    l1g, l1b, l2g, l2b = vec(ln1_gamma), vec(ln1_beta), vec(ln2_gamma), vec(ln2_beta)

    (p, xb, *placed), (wi, convs) = _in_proj_own(xs, w_in, (w_out_a, w_out_b, w_o, w_up, w_down), axes[1:], pos,
                                                 _stage_gather_near([wi_own, conv_full], (1, 1)))
    fulls = [wi_own] + placed
    p, (wi, convs), small3 = _in_proj_rest(xb, wi, p, pos, _stage_gather_diagonal([wi, convs], (1, 1)),
                                           _stage_gather_send(fulls[1:4], axes[1:4]))
    cwa, cwb = convs[0:K_A], convs[8:8 + K_B]
    (conva, yapre, nb, rstdb, u3), landed = _conv_fwd(
        p, cwa, cwb, bias_b, lbg, lbb, d,
        _both(_stage_gather_forward(small3, axes[1:4]), _stage_gather_send(fulls[4:6], axes[4:6])))
    woa, wob, wo = landed[:3]
    (ya, yb, merged, n1, rstd1), (wup, wdown) = _mixer_out(yapre, u3, p, xs, woa, wob, wo, d,
                                                           _stage_gather_forward(landed[3:], axes[4:6]))
    x1b, hb, dhpre, dr2b, dr1, dr1b, acc_mlp = _mlp(n1, rstd1, tgt, wup, wdown, l1g, l1b, l2g, l2b)
    g_up, gb_up, _ = _grad_w(x1b, dhpre, "grad_w_up", tokens=GRAD_W_TOKENS_MLP)
    g_down, gb_down, _ = _grad_w(hb, dr2b, "grad_w_down", tokens=GRAD_W_TOKENS_MLP)
    dya, dyb, dba, dga, dgb, dca, du1, acc_mix = _mixer_bwd_local(dr1b, p, ya, yb, conva, nb, rstdb, wo, woa, wob,
                                                                   lbg, lbb)
    g_oa, gb_oa, _ = _grad_w(yapre, dya, "grad_w_out_a")
    g_ob, gb_ob, _ = _grad_w(u3, dyb, "grad_w_out_b")
    g_o, gb_o, _ = _grad_w(merged, dr1b, "grad_w_o")

    early, e_axes, e_names = [g_oa, g_ob, g_o, g_up, g_down], axes[1:], names[1:]
    (dp, dcwa, dcwb), land1 = _conv_bwd(dca, du1, p, dba, dga, dgb, cwa, cwb, d,
                                        _stage_pair_exchange([gb_oa, gb_ob, gb_o, gb_up, gb_down], e_axes))
    pieces = [_pair_add(g, l, a, pos, "pair_add_" + nm) for g, l, a, nm in zip(early, land1, e_axes, e_names)]
    pack = jnp.concatenate([dcwa, dcwb, acc_mix, acc_mlp], axis=0)
    stack = lax.dynamic_update_slice(jnp.zeros((8,) + pack.shape, F32), pack[None], (2 * chip + core, 0, 0))
    g_wi, gb_wi, landed = _grad_w(xb, dp, "grad_w_in",
                                  _both(_stage_chip_scatter(pieces), _stage_gather_small(stack)))
    land2, stack = landed[:-1], landed[-1]
    halves = [_chip_sum(g, l1, l2, a, pos, "chip_sum_" + nm)
              for g, l1, l2, a, nm in zip(early, land1, land2, e_axes, e_names)]
    (land1_in,) = _comm_call(_stage_pair_exchange([gb_wi], (1,)), "pair_exchange_w_in")
    piece_in = _pair_add(g_wi, land1_in, 1, pos, "pair_add_w_in")
    grad_x, landed = _grad_x(dr1, dp, wi, _both(_stage_chip_scatter([piece_in]), _stage_pair_share(halves)))
    land2_in, (g_oa, g_ob, g_o, g_up, g_down) = landed[0], landed[1:]
    half_in = _chip_sum(g_wi, land1_in, land2_in, 1, pos, "chip_sum_w_in")
    (g_in,) = _comm_call(_stage_pair_share([half_in]), "pair_share_w_in")
    small = _sum_parts(stack, "small_sum")
    g_ca = lax.dynamic_slice(small, (0, chip * dq), (K_A, dq))
    g_cb = lax.dynamic_slice(small, (8, chip * dq), (K_B, dq))
    g_vec = jnp.stack([small[r] for r in (42, 40, 41, 51, 52, 49, 50)])

    loss = (0.5 / d) * jnp.sum(small[48])

    big = {}
    for name, w, g, m, v in (("w_in", w_in, g_in, m_w_in, v_w_in), ("w_out_a", w_out_a, g_oa, m_w_out_a, v_w_out_a),
                             ("w_out_b", w_out_b, g_ob, m_w_out_b, v_w_out_b), ("w_o", w_o, g_o, m_w_o, v_w_o),
                             ("w_up", w_up, g_up, m_w_up, v_w_up), ("w_down", w_down, g_down, m_w_down, v_w_down),
                             ("conv_a_w", conv_a_w, g_ca, m_conv_a_w, v_conv_a_w),
                             ("conv_b_w", conv_b_w, g_cb, m_conv_b_w, v_conv_b_w)):
        big[name] = tuple(_adamw(w, g, m, v, "adamw_" + name))
    vec_names = ("conv_b_bias", "ln_b_gamma", "ln_b_beta", "ln1_gamma", "ln1_beta", "ln2_gamma", "ln2_beta")
    w7 = jnp.stack([conv_b_bias, ln_b_gamma, ln_b_beta, ln1_gamma, ln1_beta, ln2_gamma, ln2_beta])
    m7 = jnp.stack([m_conv_b_bias, m_ln_b_gamma, m_ln_b_beta, m_ln1_gamma, m_ln1_beta, m_ln2_gamma, m_ln2_beta])
    v7 = jnp.stack([v_conv_b_bias, v_ln_b_gamma, v_ln_b_beta, v_ln1_gamma, v_ln1_beta, v_ln2_gamma, v_ln2_beta])
    g7, d7, nm7, nv7 = _adamw(w7, g_vec, m7, v7, "adamw_vectors")
    for q, name in enumerate(vec_names):
        big[name] = (g7[q], d7[q], nm7[q], nv7[q])

    order = ("w_in", "conv_a_w", "w_out_a", "conv_b_w", "conv_b_bias", "ln_b_gamma", "ln_b_beta", "w_out_b", "w_o",
             "ln1_gamma", "ln1_beta", "w_up", "w_down", "ln2_gamma", "ln2_beta")
    outs = [loss, grad_x.reshape(x.shape)]
    for part in range(4):
        outs += [big[name][part] for name in order]
    return tuple(outs)
```
